```python
import math
import jax, jax.numpy as jnp
from jax import lax
import numpy as np

D_MODEL = 1024
BATCH = 8
SEQ = 8192
DEPTH = 2

N_MIXERS = 2
N_META = 16
GDN_HEADS = 8
GDN_HEAD_DIM = 128
GDN_WIDTH = GDN_HEADS * GDN_HEAD_DIM
GDN_CONV = 4
CHUNK = 64
META_PAD = (-N_META) % CHUNK
SC_WIDTH = D_MODEL
SC_CONV = 3
D_FF = 2816
FFN_CONV = 3
N_LAYERS_A = (DEPTH + 1) // 2
N_LAYERS_B = DEPTH // 2
ALPHA = (2.0 * DEPTH) ** 0.25
BETA_INIT = (8.0 * DEPTH) ** -0.25
LN_EPS = 1e-5
RMS_EPS = 1e-6
L2_EPS = 1e-6

kernel_name = "hybrid_gdn_shortconv_convffn_deepnorm"


def causal_dwconv(x, w):
    width, ch = w.shape
    return lax.conv_general_dilated(
        x, w[:, None, :].astype(x.dtype), window_strides=(1,), padding=[(width - 1, 0)],
        dimension_numbers=("NWC", "WIO", "NWC"), feature_group_count=ch)


def layer_norm(x, g, b):
    xf = x.astype(jnp.float32)
    mu = jnp.mean(xf, axis=-1, keepdims=True)
    var = jnp.mean(jnp.square(xf - mu), axis=-1, keepdims=True)
    y = (xf - mu) * lax.rsqrt(var + LN_EPS) * g.astype(jnp.float32) + b.astype(jnp.float32)
    return y.astype(x.dtype)


def l2norm(x):
    return x * lax.rsqrt(jnp.sum(x * x, axis=-1, keepdims=True) + L2_EPS)


def chunk_gated_delta_rule(q, k, v, g, beta):
    bsz, t_len, h, dk = q.shape
    n = t_len // CHUNK

    def to_chunks(t):
        return jnp.transpose(t.reshape(bsz, n, CHUNK, h, -1), (0, 3, 1, 2, 4))

    q, k, v = to_chunks(q), to_chunks(k), to_chunks(v)
    g = jnp.transpose(g.reshape(bsz, n, CHUNK, h), (0, 3, 1, 2))
    beta = jnp.transpose(beta.reshape(bsz, n, CHUNK, h), (0, 3, 1, 2))
    g = jnp.cumsum(g, axis=-1)

    causal = jnp.tril(jnp.ones((CHUNK, CHUNK), dtype=bool))
    strict = jnp.tril(jnp.ones((CHUNK, CHUNK), dtype=bool), -1)
    decay = jnp.exp(jnp.where(causal, g[..., :, None] - g[..., None, :], -jnp.inf))

    k_beta = k * beta[..., None]
    v_beta = v * beta[..., None]
    m = jnp.where(strict, jnp.einsum("bhnid,bhnjd->bhnij", k_beta, k) * decay, 0.0)
    a_mat = jnp.eye(CHUNK, dtype=q.dtype) + m
    u = lax.linalg.triangular_solve(a_mat, v_beta, left_side=True, lower=True)
    w = lax.linalg.triangular_solve(a_mat, k_beta * jnp.exp(g)[..., None], left_side=True, lower=True)

    qk = jnp.einsum("bhnid,bhnjd->bhnij", q, k) * decay
    q_g = q * jnp.exp(g)[..., None]
    g_last = g[..., -1]
    k_dec = k * jnp.exp(g_last[..., None] - g)[..., None]

    def step(state, inp):
        qg_i, kd_i, u_i, w_i, qk_i, gl_i = inp
        v_new = u_i - jnp.einsum("bhcd,bhde->bhce", w_i, state)
        o_i = jnp.einsum("bhcd,bhde->bhce", qg_i, state) + jnp.einsum("bhij,bhje->bhie", qk_i, v_new)
        state = state * jnp.exp(gl_i)[..., None, None] + jnp.einsum("bhcd,bhce->bhde", kd_i, v_new)
        return state, o_i

    xs = tuple(jnp.moveaxis(t, 2, 0) for t in (q_g, k_dec, u, w, qk, g_last))
    state0 = jnp.zeros((bsz, h, dk, v.shape[-1]), dtype=q.dtype)
    _, o = lax.scan(step, state0, xs)
    return jnp.transpose(o, (1, 0, 3, 2, 4)).reshape(bsz, t_len, h, -1)


def gated_deltanet(h, w_in, conv_w, a_log, dt_bias, norm_w, w_out):
    bsz, seq_len, _ = h.shape
    proj = h @ w_in
    qkv, z, b_raw, a_raw = jnp.split(
        proj, [3 * GDN_WIDTH, 4 * GDN_WIDTH, 4 * GDN_WIDTH + GDN_HEADS], axis=-1)
    qkv = jax.nn.silu(causal_dwconv(qkv, conv_w))
    q, k, v = [t.reshape(bsz, seq_len, GDN_HEADS, GDN_HEAD_DIM).astype(jnp.float32)
               for t in jnp.split(qkv, 3, axis=-1)]
    q = l2norm(q) * (GDN_HEAD_DIM ** -0.5)
    k = l2norm(k)
    beta = jax.nn.sigmoid(b_raw.astype(jnp.float32))
    g = -jnp.exp(a_log.astype(jnp.float32)) * jax.nn.softplus(
        a_raw.astype(jnp.float32) + dt_bias.astype(jnp.float32))
    pad4 = ((0, 0), (META_PAD, 0), (0, 0), (0, 0))
    pad3 = ((0, 0), (META_PAD, 0), (0, 0))
    o = chunk_gated_delta_rule(jnp.pad(q, pad4), jnp.pad(k, pad4), jnp.pad(v, pad4),
                               jnp.pad(g, pad3), jnp.pad(beta, pad3))[:, META_PAD:]
    o = o * lax.rsqrt(jnp.mean(o * o, axis=-1, keepdims=True) + RMS_EPS) * norm_w.astype(jnp.float32)
    o = o * jax.nn.silu(z.reshape(bsz, seq_len, GDN_HEADS, GDN_HEAD_DIM).astype(jnp.float32))
    return o.reshape(bsz, seq_len, GDN_WIDTH).astype(h.dtype) @ w_out


def short_conv_mixer(h, w_in, conv_w, w_out):
    b_gate, c_gate, xv = jnp.split(h @ w_in, 3, axis=-1)
    u = causal_dwconv(c_gate * xv, conv_w)
    return (b_gate * u) @ w_out


def conv_ffn(h, w_up, conv_w, w_down):
    u, gate = jnp.split(h @ w_up, 2, axis=-1)
    u = causal_dwconv(u, conv_w)
    return (jax.nn.silu(u) * gate) @ w_down


def _fwd_setup_inputs(seed: int = 0) -> dict:
    key = jax.random.key(seed)
    ks = iter(jax.random.split(key, 32))
    f32 = jnp.float32

    def nrm(shape, scale):
        return jax.random.normal(next(ks), shape, f32) * scale

    a_in_cols = 4 * GDN_WIDTH + 2 * GDN_HEADS
    dt = jnp.exp(jax.random.uniform(next(ks), (N_LAYERS_A, GDN_HEADS), f32,
                                    math.log(1e-3), math.log(1e-1)))
    return {
        "x": nrm((BATCH, SEQ, D_MODEL), 1.0),
        "meta": nrm((N_META, D_MODEL), 1.0),
        "a_w_in": nrm((N_LAYERS_A, D_MODEL, a_in_cols), D_MODEL ** -0.5),
        "a_conv": nrm((N_LAYERS_A, GDN_CONV, 3 * GDN_WIDTH), GDN_CONV ** -0.5),
        "a_log": jnp.log(jax.random.uniform(next(ks), (N_LAYERS_A, GDN_HEADS), f32, 1.0, 16.0)),
        "a_dt_bias": dt + jnp.log(-jnp.expm1(-dt)),
        "a_norm": 1.0 + nrm((N_LAYERS_A, GDN_HEAD_DIM), 0.02),
        "a_w_out": nrm((N_LAYERS_A, GDN_WIDTH, D_MODEL), BETA_INIT * GDN_WIDTH ** -0.5),
        "b_w_in": nrm((N_LAYERS_B, D_MODEL, 3 * SC_WIDTH), D_MODEL ** -0.5),
        "b_conv": nrm((N_LAYERS_B, SC_CONV, SC_WIDTH), SC_CONV ** -0.5),
        "b_w_out": nrm((N_LAYERS_B, SC_WIDTH, D_MODEL), BETA_INIT * SC_WIDTH ** -0.5),
        "ln_mix_g": 1.0 + nrm((DEPTH, D_MODEL), 0.02),
        "ln_mix_b": nrm((DEPTH, D_MODEL), 0.02),
        "ffn_w_up": nrm((DEPTH, D_MODEL, 2 * D_FF), D_MODEL ** -0.5),
        "ffn_conv": nrm((DEPTH, FFN_CONV, D_FF), FFN_CONV ** -0.5),
        "ffn_w_down": nrm((DEPTH, D_FF, D_MODEL), BETA_INIT * D_FF ** -0.5),
        "ln_ffn_g": 1.0 + nrm((DEPTH, D_MODEL), 0.02),
        "ln_ffn_b": nrm((DEPTH, D_MODEL), 0.02),
    }


def _fwd_reference(x, meta, a_w_in, a_conv, a_log, a_dt_bias, a_norm, a_w_out,
              b_w_in, b_conv, b_w_out, ln_mix_g, ln_mix_b,
              ffn_w_up, ffn_conv, ffn_w_down, ln_ffn_g, ln_ffn_b):
    bsz = x.shape[0]
    h = jnp.concatenate(
        [jnp.broadcast_to(meta.astype(x.dtype)[None], (bsz, N_META, D_MODEL)), x], axis=1)
    for i in range(DEPTH):
        j = i // N_MIXERS
        if i % N_MIXERS == 0:
            mix = gated_deltanet(h, a_w_in[j], a_conv[j], a_log[j], a_dt_bias[j], a_norm[j], a_w_out[j])
        else:
            mix = short_conv_mixer(h, b_w_in[j], b_conv[j], b_w_out[j])
        h = layer_norm(ALPHA * h + mix, ln_mix_g[i], ln_mix_b[i])
        h = layer_norm(ALPHA * h + conv_ffn(h, ffn_w_up[i], ffn_conv[i], ffn_w_down[i]),
                       ln_ffn_g[i], ln_ffn_b[i])
    return h[:, N_META:]


import jax as _jax
import jax.numpy as _jnp

TWIN_FORMAT = 'train_step'
FWD_PARAMS = ['x', 'meta', 'a_w_in', 'a_conv', 'a_log', 'a_dt_bias', 'a_norm', 'a_w_out', 'b_w_in', 'b_conv', 'b_w_out', 'ln_mix_g', 'ln_mix_b', 'ffn_w_up', 'ffn_conv', 'ffn_w_down', 'ln_ffn_g', 'ln_ffn_b']
TWIN_WEIGHTS = ['meta', 'a_w_in', 'a_conv', 'a_log', 'a_dt_bias', 'a_norm', 'a_w_out', 'b_w_in', 'b_conv', 'b_w_out', 'ln_mix_g', 'ln_mix_b', 'ffn_w_up', 'ffn_conv', 'ffn_w_down', 'ln_ffn_g', 'ln_ffn_b']
TWIN_DIFF_INPUT = 'x'
TWIN_INPUTS = ['x', 'meta', 'a_w_in', 'a_conv', 'a_log', 'a_dt_bias', 'a_norm', 'a_w_out', 'b_w_in', 'b_conv', 'b_w_out', 'ln_mix_g', 'ln_mix_b', 'ffn_w_up', 'ffn_conv', 'ffn_w_down', 'ln_ffn_g', 'ln_ffn_b', 'loss_target', 'm_meta', 'm_a_w_in', 'm_a_conv', 'm_a_log', 'm_a_dt_bias', 'm_a_norm', 'm_a_w_out', 'm_b_w_in', 'm_b_conv', 'm_b_w_out', 'm_ln_mix_g', 'm_ln_mix_b', 'm_ffn_w_up', 'm_ffn_conv', 'm_ffn_w_down', 'm_ln_ffn_g', 'm_ln_ffn_b', 'v_meta', 'v_a_w_in', 'v_a_conv', 'v_a_log', 'v_a_dt_bias', 'v_a_norm', 'v_a_w_out', 'v_b_w_in', 'v_b_conv', 'v_b_w_out', 'v_ln_mix_g', 'v_ln_mix_b', 'v_ffn_w_up', 'v_ffn_conv', 'v_ffn_w_down', 'v_ln_ffn_g', 'v_ln_ffn_b']
TWIN_OUTPUTS = ['loss', 'grad_x', 'grad_meta', 'grad_a_w_in', 'grad_a_conv', 'grad_a_log', 'grad_a_dt_bias', 'grad_a_norm', 'grad_a_w_out', 'grad_b_w_in', 'grad_b_conv', 'grad_b_w_out', 'grad_ln_mix_g', 'grad_ln_mix_b', 'grad_ffn_w_up', 'grad_ffn_conv', 'grad_ffn_w_down', 'grad_ln_ffn_g', 'grad_ln_ffn_b', 'delta_meta', 'delta_a_w_in', 'delta_a_conv', 'delta_a_log', 'delta_a_dt_bias', 'delta_a_norm', 'delta_a_w_out', 'delta_b_w_in', 'delta_b_conv', 'delta_b_w_out', 'delta_ln_mix_g', 'delta_ln_mix_b', 'delta_ffn_w_up', 'delta_ffn_conv', 'delta_ffn_w_down', 'delta_ln_ffn_g', 'delta_ln_ffn_b', 'new_m_meta', 'new_m_a_w_in', 'new_m_a_conv', 'new_m_a_log', 'new_m_a_dt_bias', 'new_m_a_norm', 'new_m_a_w_out', 'new_m_b_w_in', 'new_m_b_conv', 'new_m_b_w_out', 'new_m_ln_mix_g', 'new_m_ln_mix_b', 'new_m_ffn_w_up', 'new_m_ffn_conv', 'new_m_ffn_w_down', 'new_m_ln_ffn_g', 'new_m_ln_ffn_b', 'new_v_meta', 'new_v_a_w_in', 'new_v_a_conv', 'new_v_a_log', 'new_v_a_dt_bias', 'new_v_a_norm', 'new_v_a_w_out', 'new_v_b_w_in', 'new_v_b_conv', 'new_v_b_w_out', 'new_v_ln_mix_g', 'new_v_ln_mix_b', 'new_v_ffn_w_up', 'new_v_ffn_conv', 'new_v_ffn_w_down', 'new_v_ln_ffn_g', 'new_v_ln_ffn_b']
TWIN_LEAF_KINDS = {'loss': 'loss', 'grad_x': 'grad_x', 'grad_meta': 'grad_w', 'grad_a_w_in': 'grad_w', 'grad_a_conv': 'grad_w', 'grad_a_log': 'grad_w', 'grad_a_dt_bias': 'grad_w', 'grad_a_norm': 'grad_w', 'grad_a_w_out': 'grad_w', 'grad_b_w_in': 'grad_w', 'grad_b_conv': 'grad_w', 'grad_b_w_out': 'grad_w', 'grad_ln_mix_g': 'grad_w', 'grad_ln_mix_b': 'grad_w', 'grad_ffn_w_up': 'grad_w', 'grad_ffn_conv': 'grad_w', 'grad_ffn_w_down': 'grad_w', 'grad_ln_ffn_g': 'grad_w', 'grad_ln_ffn_b': 'grad_w', 'delta_meta': 'delta_w', 'delta_a_w_in': 'delta_w', 'delta_a_conv': 'delta_w', 'delta_a_log': 'delta_w', 'delta_a_dt_bias': 'delta_w', 'delta_a_norm': 'delta_w', 'delta_a_w_out': 'delta_w', 'delta_b_w_in': 'delta_w', 'delta_b_conv': 'delta_w', 'delta_b_w_out': 'delta_w', 'delta_ln_mix_g': 'delta_w', 'delta_ln_mix_b': 'delta_w', 'delta_ffn_w_up': 'delta_w', 'delta_ffn_conv': 'delta_w', 'delta_ffn_w_down': 'delta_w', 'delta_ln_ffn_g': 'delta_w', 'delta_ln_ffn_b': 'delta_w', 'new_m_meta': 'new_m', 'new_m_a_w_in': 'new_m', 'new_m_a_conv': 'new_m', 'new_m_a_log': 'new_m', 'new_m_a_dt_bias': 'new_m', 'new_m_a_norm': 'new_m', 'new_m_a_w_out': 'new_m', 'new_m_b_w_in': 'new_m', 'new_m_b_conv': 'new_m', 'new_m_b_w_out': 'new_m', 'new_m_ln_mix_g': 'new_m', 'new_m_ln_mix_b': 'new_m', 'new_m_ffn_w_up': 'new_m', 'new_m_ffn_conv': 'new_m', 'new_m_ffn_w_down': 'new_m', 'new_m_ln_ffn_g': 'new_m', 'new_m_ln_ffn_b': 'new_m', 'new_v_meta': 'new_v', 'new_v_a_w_in': 'new_v', 'new_v_a_conv': 'new_v', 'new_v_a_log': 'new_v', 'new_v_a_dt_bias': 'new_v', 'new_v_a_norm': 'new_v', 'new_v_a_w_out': 'new_v', 'new_v_b_w_in': 'new_v', 'new_v_b_conv': 'new_v', 'new_v_b_w_out': 'new_v', 'new_v_ln_mix_g': 'new_v', 'new_v_ln_mix_b': 'new_v', 'new_v_ffn_w_up': 'new_v', 'new_v_ffn_conv': 'new_v', 'new_v_ffn_w_down': 'new_v', 'new_v_ln_ffn_g': 'new_v', 'new_v_ln_ffn_b': 'new_v'}


def _forward(args):
    return _fwd_reference(*[args[k] for k in FWD_PARAMS])


def _output_shape():
    def fwd():
        inp = _fwd_setup_inputs(0)
        return _fwd_reference(*[inp[k] for k in FWD_PARAMS])
    out = _jax.eval_shape(fwd)
    return out.shape, out.dtype

N_MICROBATCH = 1
ADAM_LR = 0.001
ADAM_B1 = 0.9
ADAM_B2 = 0.999
ADAM_EPS = 1e-08
ADAM_WD = 0.01
ADAM_STEP = 10
PER_EXAMPLE_BATCH_AXIS = {'x': 0, 'loss_target': 0}
SHARED_INPUTS = []
_WEIGHT_DTYPES = {'meta': _jnp.float32, 'a_w_in': _jnp.float32, 'a_conv': _jnp.float32, 'a_log': _jnp.float32, 'a_dt_bias': _jnp.float32, 'a_norm': _jnp.float32, 'a_w_out': _jnp.float32, 'b_w_in': _jnp.float32, 'b_conv': _jnp.float32, 'b_w_out': _jnp.float32, 'ln_mix_g': _jnp.float32, 'ln_mix_b': _jnp.float32, 'ffn_w_up': _jnp.float32, 'ffn_conv': _jnp.float32, 'ffn_w_down': _jnp.float32, 'ln_ffn_g': _jnp.float32, 'ln_ffn_b': _jnp.float32}
MOMENT_SCALE = {'meta': 4.085288e-03, 'a_w_in': 4.978877e-02, 'a_conv': 4.536354e-02, 'a_log': 1.606919e-01, 'a_dt_bias': 1.581458e-01, 'a_norm': 1.871097e-01, 'a_w_out': 1.297309e-01, 'b_w_in': 8.627457e-02, 'b_conv': 8.714873e-02, 'b_w_out': 1.717547e-01, 'ln_mix_g': 1.958784e+00, 'ln_mix_b': 9.433148e-01, 'ffn_w_up': 3.468808e-02, 'ffn_conv': 3.652358e-02, 'ffn_w_down': 1.135428e-01, 'ln_ffn_g': 4.536549e+01, 'ln_ffn_b': 2.362494e+00}


def _to_microbatches(a, axis):
    t = _jnp.moveaxis(a, axis, 0)
    t = t.reshape((N_MICROBATCH, t.shape[0] // N_MICROBATCH) + t.shape[1:])
    return _jnp.moveaxis(t, 1, axis + 1)


def setup_inputs(seed: int = 0) -> dict:
    inp = _fwd_setup_inputs(seed)
    key = _jax.random.fold_in(_jax.random.key(seed), 7919)
    shape, _ = _output_shape()
    out = dict(inp)
    out["loss_target"] = _jax.random.normal(_jax.random.fold_in(key, 0), shape, _jnp.float32)
    for i, name in enumerate(TWIN_WEIGHTS):
        w = inp[name].astype(_jnp.float32)
        if MOMENT_SCALE is None:
            s = _jnp.sqrt(_jnp.mean(_jnp.square(w)) + 1e-30)
        else:
            s = MOMENT_SCALE[name]
        km, kv = _jax.random.split(_jax.random.fold_in(key, i + 1))
        out[name] = w
        out["m_" + name] = s * _jax.random.normal(km, w.shape, _jnp.float32)
        out["v_" + name] = (s * s) * _jax.random.uniform(kv, w.shape, _jnp.float32, 0.5, 1.5)
    if N_MICROBATCH > 1:
        for name, axis in PER_EXAMPLE_BATCH_AXIS.items():
            out[name] = _to_microbatches(out[name], axis)
    return {'x': out['x'], 'meta': out['meta'], 'a_w_in': out['a_w_in'], 'a_conv': out['a_conv'], 'a_log': out['a_log'], 'a_dt_bias': out['a_dt_bias'], 'a_norm': out['a_norm'], 'a_w_out': out['a_w_out'], 'b_w_in': out['b_w_in'], 'b_conv': out['b_conv'], 'b_w_out': out['b_w_out'], 'ln_mix_g': out['ln_mix_g'], 'ln_mix_b': out['ln_mix_b'], 'ffn_w_up': out['ffn_w_up'], 'ffn_conv': out['ffn_conv'], 'ffn_w_down': out['ffn_w_down'], 'ln_ffn_g': out['ln_ffn_g'], 'ln_ffn_b': out['ln_ffn_b'], 'loss_target': out['loss_target'], 'm_meta': out['m_meta'], 'm_a_w_in': out['m_a_w_in'], 'm_a_conv': out['m_a_conv'], 'm_a_log': out['m_a_log'], 'm_a_dt_bias': out['m_a_dt_bias'], 'm_a_norm': out['m_a_norm'], 'm_a_w_out': out['m_a_w_out'], 'm_b_w_in': out['m_b_w_in'], 'm_b_conv': out['m_b_conv'], 'm_b_w_out': out['m_b_w_out'], 'm_ln_mix_g': out['m_ln_mix_g'], 'm_ln_mix_b': out['m_ln_mix_b'], 'm_ffn_w_up': out['m_ffn_w_up'], 'm_ffn_conv': out['m_ffn_conv'], 'm_ffn_w_down': out['m_ffn_w_down'], 'm_ln_ffn_g': out['m_ln_ffn_g'], 'm_ln_ffn_b': out['m_ln_ffn_b'], 'v_meta': out['v_meta'], 'v_a_w_in': out['v_a_w_in'], 'v_a_conv': out['v_a_conv'], 'v_a_log': out['v_a_log'], 'v_a_dt_bias': out['v_a_dt_bias'], 'v_a_norm': out['v_a_norm'], 'v_a_w_out': out['v_a_w_out'], 'v_b_w_in': out['v_b_w_in'], 'v_b_conv': out['v_b_conv'], 'v_b_w_out': out['v_b_w_out'], 'v_ln_mix_g': out['v_ln_mix_g'], 'v_ln_mix_b': out['v_ln_mix_b'], 'v_ffn_w_up': out['v_ffn_w_up'], 'v_ffn_conv': out['v_ffn_conv'], 'v_ffn_w_down': out['v_ffn_w_down'], 'v_ln_ffn_g': out['v_ln_ffn_g'], 'v_ln_ffn_b': out['v_ln_ffn_b']}


def _loss(weights, diff, rest, loss_target):
    with _jax.named_scope("forward"):
        args = {**rest, TWIN_DIFF_INPUT: diff, **{k: w.astype(_WEIGHT_DTYPES[k]) for k, w in weights.items()}}
        y = _forward(args)
    with _jax.named_scope("loss_head"):
        err = _jnp.square(y.astype(_jnp.float32) - loss_target)
        return 0.5 * _jnp.sum(_jnp.mean(err, axis=-1)) if err.ndim else 0.5 * err


def _adamw(w, g, m, v):
    m = ADAM_B1 * m + (1.0 - ADAM_B1) * g
    v = ADAM_B2 * v + (1.0 - ADAM_B2) * _jnp.square(g)
    m_hat = m / (1.0 - ADAM_B1 ** ADAM_STEP)
    v_hat = v / (1.0 - ADAM_B2 ** ADAM_STEP)
    delta = -ADAM_LR * (m_hat / (_jnp.sqrt(v_hat) + ADAM_EPS) + ADAM_WD * w)
    return delta, m, v


def reference(x, meta, a_w_in, a_conv, a_log, a_dt_bias, a_norm, a_w_out, b_w_in, b_conv, b_w_out, ln_mix_g, ln_mix_b, ffn_w_up, ffn_conv, ffn_w_down, ln_ffn_g, ln_ffn_b, loss_target, m_meta, m_a_w_in, m_a_conv, m_a_log, m_a_dt_bias, m_a_norm, m_a_w_out, m_b_w_in, m_b_conv, m_b_w_out, m_ln_mix_g, m_ln_mix_b, m_ffn_w_up, m_ffn_conv, m_ffn_w_down, m_ln_ffn_g, m_ln_ffn_b, v_meta, v_a_w_in, v_a_conv, v_a_log, v_a_dt_bias, v_a_norm, v_a_w_out, v_b_w_in, v_b_conv, v_b_w_out, v_ln_mix_g, v_ln_mix_b, v_ffn_w_up, v_ffn_conv, v_ffn_w_down, v_ln_ffn_g, v_ln_ffn_b):
    given = dict(x=x, meta=meta, a_w_in=a_w_in, a_conv=a_conv, a_log=a_log, a_dt_bias=a_dt_bias, a_norm=a_norm, a_w_out=a_w_out, b_w_in=b_w_in, b_conv=b_conv, b_w_out=b_w_out, ln_mix_g=ln_mix_g, ln_mix_b=ln_mix_b, ffn_w_up=ffn_w_up, ffn_conv=ffn_conv, ffn_w_down=ffn_w_down, ln_ffn_g=ln_ffn_g, ln_ffn_b=ln_ffn_b, loss_target=loss_target, m_meta=m_meta, m_a_w_in=m_a_w_in, m_a_conv=m_a_conv, m_a_log=m_a_log, m_a_dt_bias=m_a_dt_bias, m_a_norm=m_a_norm, m_a_w_out=m_a_w_out, m_b_w_in=m_b_w_in, m_b_conv=m_b_conv, m_b_w_out=m_b_w_out, m_ln_mix_g=m_ln_mix_g, m_ln_mix_b=m_ln_mix_b, m_ffn_w_up=m_ffn_w_up, m_ffn_conv=m_ffn_conv, m_ffn_w_down=m_ffn_w_down, m_ln_ffn_g=m_ln_ffn_g, m_ln_ffn_b=m_ln_ffn_b, v_meta=v_meta, v_a_w_in=v_a_w_in, v_a_conv=v_a_conv, v_a_log=v_a_log, v_a_dt_bias=v_a_dt_bias, v_a_norm=v_a_norm, v_a_w_out=v_a_w_out, v_b_w_in=v_b_w_in, v_b_conv=v_b_conv, v_b_w_out=v_b_w_out, v_ln_mix_g=v_ln_mix_g, v_ln_mix_b=v_ln_mix_b, v_ffn_w_up=v_ffn_w_up, v_ffn_conv=v_ffn_conv, v_ffn_w_down=v_ffn_w_down, v_ln_ffn_g=v_ln_ffn_g, v_ln_ffn_b=v_ln_ffn_b)
    weights = {n: given[n] for n in TWIN_WEIGHTS}
    shared = {n: given[n] for n in SHARED_INPUTS}
    per_example = {n: given[n] for n in ['x']}
    grad_fn = _jax.value_and_grad(_loss, argnums=(0, 1))

    def one_microbatch(ex, loss_target):
        ex = dict(ex)
        diff = ex.pop(TWIN_DIFF_INPUT)
        return grad_fn(weights, diff, {**shared, **ex}, loss_target)

    if N_MICROBATCH == 1:
        loss, (grad_w, grad_x) = one_microbatch(per_example, given["loss_target"])
    else:
        def body(carry, xs):
            loss_sum, grad_sum = carry
            l_k, (gw_k, gx_k) = one_microbatch(xs[0], xs[1])
            with _jax.named_scope("update"):
                return (loss_sum + l_k, _jax.tree.map(_jnp.add, grad_sum, gw_k)), gx_k

        init = (_jnp.zeros((), _jnp.float32), _jax.tree.map(_jnp.zeros_like, weights))
        (loss, grad_w), grad_x = _jax.lax.scan(body, init, (per_example, given["loss_target"]))
    with _jax.named_scope("update"):
        delta_w, new_m, new_v = {}, {}, {}
        for n in TWIN_WEIGHTS:
            delta_w[n], new_m[n], new_v[n] = _adamw(weights[n], grad_w[n], given["m_" + n], given["v_" + n])
    return (loss, grad_x, *[grad_w[n] for n in TWIN_WEIGHTS], *[delta_w[n] for n in TWIN_WEIGHTS],
            *[new_m[n] for n in TWIN_WEIGHTS], *[new_v[n] for n in TWIN_WEIGHTS])
```

```python
import functools
import math

import jax
import jax.numpy as jnp
from jax import lax
from jax.experimental import pallas as pl
from jax.experimental.pallas import tpu as pltpu

f32, bf16 = jnp.float32, jnp.bfloat16
S = jax.ShapeDtypeStruct
HI = lax.Precision.HIGHEST
MESH = pl.DeviceIdType.MESH

V7X_VMEM_LIMIT = 56 * 1024 * 1024
LANE = 128
DH = 128
CH = 64
PADF = 256
TM = 256
TMM = 768
N_DEV = 8

DEPTH = 2
ALPHA = (2.0 * DEPTH) ** 0.25
LN_EPS = 1e-5
RMS_EPS = 1e-6
L2_EPS = 1e-6
ADAM_LR, ADAM_B1, ADAM_B2, ADAM_EPS, ADAM_WD, ADAM_STEP = 0.001, 0.9, 0.999, 1e-08, 0.01, 10


def _cp(**kw):
    return pltpu.CompilerParams(vmem_limit_bytes=V7X_VMEM_LIMIT, **kw)


def _bf(x):
    return x.astype(bf16)


def _dot(a, b, precision=None):
    return jnp.dot(a, b, preferred_element_type=f32, precision=precision)


def _dot_nt(a, b):
    return lax.dot_general(a, b, (((1,), (1,)), ((), ())), preferred_element_type=f32)


def _dot_tn(a, b):
    return lax.dot_general(a, b, (((0,), (0,)), ((), ())), preferred_element_type=f32)


def _sigmoid(x):
    return 1.0 / (1.0 + jnp.exp(-x))


def _load_once(pairs, sem):
    @pl.when(pl.program_id(0) == 0)
    def _():
        cps = [pltpu.make_async_copy(src, dst, sem.at[n]) for n, (src, dst) in enumerate(pairs)]
        for c in cps:
            c.start()
        for c in cps:
            c.wait()


def _row_ids(i, tm, width):
    return i * tm + lax.broadcasted_iota(jnp.int32, (tm, width), 0)


def _ln_fwd(pre, g, b, rows, first_row):
    mu = jnp.mean(pre, axis=-1, keepdims=True)
    xc = pre - mu
    var = jnp.mean(xc * xc, axis=-1, keepdims=True)
    y = xc * lax.rsqrt(var + LN_EPS) * g + b
    return jnp.where(rows >= first_row, y, 0.0)


ANY = pl.BlockSpec(memory_space=pl.ANY)


def _linear_dw(x, dy, *, name):
    L, K = x.shape
    N = dy.shape[1]
    tm = TMM if L % TMM == 0 else TM
    tn = LANE
    for d in range(N // LANE, 0, -1):
        if (N // LANE) % d == 0 and K * d * LANE * 4 <= 9 * 1024 * 1024:
            tn = d * LANE
            break

    def body(x_ref, dy_ref, o_ref):
        @pl.when(pl.program_id(1) == 0)
        def _():
            o_ref[...] = jnp.zeros_like(o_ref)
        o_ref[...] += _dot_tn(_bf(x_ref[...]), _bf(dy_ref[...]))

    return pl.pallas_call(
        body, name=name, grid=(N // tn, L // tm), out_shape=S((K, N), f32),
        in_specs=[pl.BlockSpec((tm, K), lambda j, i: (i, 0)), pl.BlockSpec((tm, tn), lambda j, i: (i, j))],
        out_specs=pl.BlockSpec((K, tn), lambda j, i: (0, j)),
        compiler_params=_cp(dimension_semantics=("arbitrary", "arbitrary")))(x, dy)


def _linear_dx(dy, w, res, *, alpha, out_dtype, name):
    L, N = dy.shape
    K = w.shape[0]
    tm = TM
    has_res = res is not None

    def body(*refs):
        if has_res:
            dy_ref, w_hbm, res_ref, o_ref, w_vmem, sem = refs
        else:
            dy_ref, w_hbm, o_ref, w_vmem, sem = refs
        _load_once([(w_hbm, w_vmem)], sem)
        acc = _dot_nt(_bf(dy_ref[...]), w_vmem[...])
        if has_res:
            acc = acc + alpha * res_ref[...]
        o_ref[...] = acc.astype(out_dtype)

    in_specs = [pl.BlockSpec((tm, N), lambda i: (i, 0)), ANY]
    args = [dy, w]
    if has_res:
        in_specs.append(pl.BlockSpec((tm, K), lambda i: (i, 0)))
        args.append(res)
    return pl.pallas_call(
        body, name=name, grid=(L // tm,), out_shape=S((L, K), out_dtype),
        in_specs=in_specs, out_specs=pl.BlockSpec((tm, K), lambda i: (i, 0)),
        scratch_shapes=[pltpu.VMEM((K, N), w.dtype), pltpu.SemaphoreType.DMA((1,))],
        compiler_params=_cp(dimension_semantics=("arbitrary",)))(*args)


def _out_res_ln(y, w, h, g, b, *, first_row, name):
    L, K = y.shape
    D = w.shape[1]
    tm = TM
    alpha = ALPHA

    def body(y_ref, w_hbm, h_ref, g_ref, b_ref, pre_ref, out_ref, w_vmem, sem):
        _load_once([(w_hbm, w_vmem)], sem)
        pre = alpha * h_ref[...] + _dot(_bf(y_ref[...]), w_vmem[...])
        pre_ref[...] = pre
        out_ref[...] = _ln_fwd(pre, g_ref[...], b_ref[...], _row_ids(pl.program_id(0), tm, D), first_row)

    row = lambda i: (i, 0)
    fix = lambda i: (0, 0)
    return pl.pallas_call(
        body, name=name, grid=(L // tm,), out_shape=(S((L, D), f32), S((L, D), f32)),
        in_specs=[pl.BlockSpec((tm, K), row), ANY, pl.BlockSpec((tm, D), row),
                  pl.BlockSpec((1, D), fix), pl.BlockSpec((1, D), fix)],
        out_specs=(pl.BlockSpec((tm, D), row), pl.BlockSpec((tm, D), row)),
        scratch_shapes=[pltpu.VMEM((K, D), w.dtype), pltpu.SemaphoreType.DMA((1,))],
        compiler_params=_cp(dimension_semantics=("arbitrary",)))(y, w, h, g, b)


def _gdn_in_fwd(h, w_full, conv_w, alog, dtb, *, first_row, H):
    L, D = h.shape
    W = H * DH
    NW = w_full.shape[1]
    KW = conv_w.shape[0]
    tm = TM

    def body(h_ref, w_hbm, cw_ref, alog_ref, dtb_ref,
             pre_ref, z_ref, raw_ref, q_ref, k_ref, v_ref, beta_ref, g_ref,
             w_vmem, scr, carry, sem):
        i = pl.program_id(0)
        _load_once([(w_hbm, w_vmem)], sem)

        @pl.when(i == 0)
        def _():
            carry[...] = jnp.zeros_like(carry)

        hb = _bf(h_ref[...])
        outs = (q_ref, k_ref, v_ref)
        for s in range(3):
            pre = _dot(hb, w_vmem[:, s * W:(s + 1) * W])
            pre_ref[:, s * W:(s + 1) * W] = pre
            scr[0:8, :] = carry[s]
            scr[8:8 + tm, :] = pre
            carry[s] = pre[tm - 8:tm, :]
            c = jnp.zeros((tm, W), f32)
            for j in range(KW):
                c = c + cw_ref[j:j + 1, s * W:(s + 1) * W] * scr[pl.ds(8 - (KW - 1) + j, tm), :]
            sl = c * _sigmoid(c)
            if s < 2:
                scale = DH ** -0.5 if s == 0 else 1.0
                for hh in range(H):
                    seg = sl[:, hh * DH:(hh + 1) * DH]
                    r = lax.rsqrt(jnp.sum(seg * seg, axis=-1, keepdims=True) + L2_EPS)
                    outs[s][:, hh * DH:(hh + 1) * DH] = seg * (r * scale)
            else:
                v_ref[...] = sl
        z_ref[...] = _dot(hb, w_vmem[:, 3 * W:4 * W])
        raw = _dot(hb, w_vmem[:, 4 * W:4 * W + 2 * LANE])
        raw_ref[...] = raw
        ok = (_row_ids(i, tm, LANE) >= first_row) & (lax.broadcasted_iota(jnp.int32, (tm, LANE), 1) < H)
        beta_ref[...] = jnp.where(ok, _sigmoid(raw[:, :LANE]), 0.0)
        a = raw[:, LANE:] + dtb_ref[...]
        sp = jnp.maximum(a, 0.0) + jnp.log(1.0 + jnp.exp(-jnp.abs(a)))
        g_ref[...] = jnp.where(ok, -jnp.exp(alog_ref[...]) * sp, 0.0)

    row = lambda i: (i, 0)
    fix = lambda i: (0, 0)
    out_shape = (S((L, 3 * W), f32), S((L, W), f32), S((L, 2 * LANE), f32),
                 S((L, W), f32), S((L, W), f32), S((L, W), f32), S((L, LANE), f32), S((L, LANE), f32))
    out_specs = (pl.BlockSpec((tm, 3 * W), row), pl.BlockSpec((tm, W), row), pl.BlockSpec((tm, 2 * LANE), row),
                 pl.BlockSpec((tm, W), row), pl.BlockSpec((tm, W), row), pl.BlockSpec((tm, W), row),
                 pl.BlockSpec((tm, LANE), row), pl.BlockSpec((tm, LANE), row))
    return pl.pallas_call(
        body, name="gdn_in_fwd", grid=(L // tm,), out_shape=out_shape,
        in_specs=[pl.BlockSpec((tm, D), row), ANY, pl.BlockSpec((KW, 3 * W), fix),
                  pl.BlockSpec((1, LANE), fix), pl.BlockSpec((1, LANE), fix)],
        out_specs=out_specs,
        scratch_shapes=[pltpu.VMEM((D, NW), w_full.dtype), pltpu.VMEM((tm + 8, W), f32), pltpu.VMEM((3, 8, W), f32),
                        pltpu.SemaphoreType.DMA((1,))],
        compiler_params=_cp(dimension_semantics=("arbitrary",)))(h, w_full, conv_w, alog, dtb)


def _gdn_in_bwd(dq, dk, dv, dz, dg, dbeta, pre, raw, conv_w, alog, dtb, *, first_row, H):
    L = dq.shape[0]
    W = H * DH
    KW = conv_w.shape[0]
    tm = TM
    nb = L // tm
    NW = 4 * W + 2 * LANE

    def body(dq_ref, dk_ref, dv_ref, dz_ref, dg_ref, dbeta_ref, pre_ref, hq_ref, hk_ref, hv_ref, raw_ref,
             cw_ref, alog_ref, dtb_ref, dproj_ref, dcw_ref, dal_ref, ddt_ref, scrx, scrd, carry, tmp):
        i = pl.program_id(0)
        blk = nb - 1 - i

        @pl.when(i == 0)
        def _():
            carry[...] = jnp.zeros_like(carry)
            dcw_ref[...] = jnp.zeros_like(dcw_ref)
            dal_ref[...] = jnp.zeros_like(dal_ref)
            ddt_ref[...] = jnp.zeros_like(ddt_ref)

        halos = (hq_ref, hk_ref, hv_ref)
        douts = (dq_ref, dk_ref, dv_ref)
        for s in range(3):
            sec = slice(s * W, (s + 1) * W)
            scrx[0:8, :] = jnp.where(blk > 0, halos[s][...], 0.0)
            scrx[8:8 + tm, :] = pre_ref[:, sec]
            c = jnp.zeros((tm, W), f32)
            for j in range(KW):
                c = c + cw_ref[j:j + 1, sec] * scrx[pl.ds(8 - (KW - 1) + j, tm), :]
            sig = _sigmoid(c)
            sl = c * sig
            if s < 2:
                scale = DH ** -0.5 if s == 0 else 1.0
                for hh in range(H):
                    hs = slice(hh * DH, (hh + 1) * DH)
                    seg = sl[:, hs]
                    r = lax.rsqrt(jnp.sum(seg * seg, axis=-1, keepdims=True) + L2_EPS)
                    n = seg * r
                    dqs = douts[s][:, hs]
                    tmp[:, hs] = (scale * r) * (dqs - n * jnp.sum(n * dqs, axis=-1, keepdims=True))
                dsl = tmp[...]
            else:
                dsl = dv_ref[...]
            dc = dsl * (sig * (1.0 + c * (1.0 - sig)))
            scrd[0:tm, :] = dc
            scrd[tm:tm + 8, :] = carry[s]
            carry[s] = dc[0:8, :]
            dx = jnp.zeros((tm, W), f32)
            for j in range(KW):
                dx = dx + cw_ref[j:j + 1, sec] * scrd[pl.ds(KW - 1 - j, tm), :]
                dcw_ref[j:j + 1, sec] += jnp.sum(dc * scrx[pl.ds(8 - (KW - 1) + j, tm), :], axis=0, keepdims=True)
            dproj_ref[:, sec] = _bf(dx)
        dproj_ref[:, 3 * W:4 * W] = _bf(dz_ref[...])
        raw_v = raw_ref[...]
        ok = (_row_ids(blk, tm, LANE) >= first_row) & (lax.broadcasted_iota(jnp.int32, (tm, LANE), 1) < H)
        beta = _sigmoid(raw_v[:, :LANE])
        dbraw = jnp.where(ok, dbeta_ref[...] * beta * (1.0 - beta), 0.0)
        a = raw_v[:, LANE:] + dtb_ref[...]
        sp = jnp.maximum(a, 0.0) + jnp.log(1.0 + jnp.exp(-jnp.abs(a)))
        nea = -jnp.exp(alog_ref[...])
        dgm = jnp.where(ok, dg_ref[...], 0.0)
        daraw = dgm * nea * _sigmoid(a)
        dal_ref[0:1, :] += jnp.sum(dgm * nea * sp, axis=0, keepdims=True)
        ddt_ref[0:1, :] += jnp.sum(daraw, axis=0, keepdims=True)
        dproj_ref[:, 4 * W:4 * W + LANE] = _bf(dbraw)
        dproj_ref[:, 4 * W + LANE:4 * W + 2 * LANE] = _bf(daraw)

    rev = lambda i: (nb - 1 - i, 0)
    fix = lambda i: (0, 0)

    def halo(col):
        return pl.BlockSpec((8, W), lambda i: (jnp.maximum((nb - 1 - i) * (tm // 8) - 1, 0), col))

    return pl.pallas_call(
        body, name="gdn_in_bwd", grid=(nb,),
        out_shape=(S((L, NW), bf16), S((8, 3 * W), f32), S((8, LANE), f32), S((8, LANE), f32)),
        in_specs=[pl.BlockSpec((tm, W), rev)] * 4 + [pl.BlockSpec((tm, LANE), rev)] * 2
        + [pl.BlockSpec((tm, 3 * W), rev), halo(0), halo(1), halo(2), pl.BlockSpec((tm, 2 * LANE), rev),
           pl.BlockSpec((KW, 3 * W), fix), pl.BlockSpec((1, LANE), fix), pl.BlockSpec((1, LANE), fix)],
        out_specs=(pl.BlockSpec((tm, NW), rev), pl.BlockSpec((8, 3 * W), fix),
                   pl.BlockSpec((8, LANE), fix), pl.BlockSpec((8, LANE), fix)),
        scratch_shapes=[pltpu.VMEM((tm + 8, W), f32), pltpu.VMEM((tm + 8, W), f32), pltpu.VMEM((3, 8, W), f32),
                        pltpu.VMEM((tm, W), f32)],
        compiler_params=_cp(dimension_semantics=("arbitrary",)))(
            dq, dk, dv, dz, dg, dbeta, pre, pre, pre, pre, raw, conv_w, alog, dtb)


def _tri_inv(m, eye):
    t = eye - m
    p = m
    for _ in range(int(math.log2(CH)) - 1):
        p = _dot(p, p, HI)
        t = t + _dot(t, p, HI)
    return t


def _chunk_local(q, k, v, gcol, grow, glast, bcol, ii, jj):
    dec = jnp.where(ii >= jj, jnp.exp(jnp.minimum(gcol - grow, 0.0)), 0.0)
    eg = jnp.exp(gcol)
    kb = k * bcol
    kbg = kb * eg
    vb = v * bcol
    qt = q * eg
    kt = k * jnp.exp(glast - gcol)
    kk = _dot_nt(_bf(kb), _bf(k))
    qk = _dot_nt(_bf(q), _bf(k))
    return dec, eg, kb, kbg, vb, qt, kt, kk, qk


def _delta_fwd(q, k, v, g, beta, z, nw, *, H):
    L = q.shape[0]
    W = H * DH
    rb = TM
    nc = rb // CH
    nblk = L // rb

    def body(q_ref, k_ref, v_ref, g_ref, b_ref, z_ref, nw_ref, o_ref, y_ref, s_out, t_out, s_scr):
        @pl.when(pl.program_id(0) == 0)
        def _():
            s_scr[...] = jnp.zeros_like(s_scr)

        ii = lax.broadcasted_iota(jnp.int32, (CH, CH), 0)
        jj = lax.broadcasted_iota(jnp.int32, (CH, CH), 1)
        ltri = (ii >= jj).astype(f32)
        eye = (ii == jj).astype(f32)
        nwv = nw_ref[...]

        def chunk(c, carry):
            r0 = pl.multiple_of(c * CH, CH)
            rows = pl.ds(r0, CH)
            gam = _dot(ltri, g_ref[rows, :], HI)
            gam_t = gam.T
            bb = b_ref[rows, :]
            for hh in range(H):
                hs = slice(hh * DH, (hh + 1) * DH)
                gcol, grow, glast = gam[:, hh:hh + 1], gam_t[hh:hh + 1, :], gam[CH - 1:CH, hh:hh + 1]
                qh, kh, vh = q_ref[rows, hs], k_ref[rows, hs], v_ref[rows, hs]
                dec, eg, kb, kbg, vb, qt, kt, kk, qk = _chunk_local(qh, kh, vh, gcol, grow, glast, bb[:, hh:hh + 1], ii, jj)
                t = _tri_inv(jnp.where(ii > jj, kk * dec, 0.0), eye)
                u = _dot(t, vb, HI)
                w = _dot(t, kbg, HI)
                st = s_scr[hh]
                s_out[c, hh] = st
                t_out[c, hh] = t
                stb = _bf(st)
                vn = u - _dot(_bf(w), stb)
                o = _dot(_bf(qt), stb) + _dot(_bf(qk * dec), _bf(vn))
                s_scr[hh] = st * jnp.exp(glast) + _dot_tn(_bf(kt), _bf(vn))
                o_ref[rows, hs] = o
                on = o * lax.rsqrt(jnp.mean(o * o, axis=-1, keepdims=True) + RMS_EPS) * nwv
                zh = z_ref[rows, hs]
                y_ref[rows, hs] = _bf(on * (zh * _sigmoid(zh)))
            return carry

        lax.fori_loop(0, nc, chunk, 0)

    row = lambda i: (i, 0)
    fix = lambda i: (0, 0)
    return pl.pallas_call(
        body, name="delta_fwd", grid=(nblk,),
        out_shape=(S((L, W), f32), S((L, W), bf16), S((L // CH, H, DH, DH), f32), S((L // CH, H, CH, CH), f32)),
        in_specs=[pl.BlockSpec((rb, W), row)] * 3 + [pl.BlockSpec((rb, LANE), row)] * 2
        + [pl.BlockSpec((rb, W), row), pl.BlockSpec((1, DH), fix)],
        out_specs=(pl.BlockSpec((rb, W), row), pl.BlockSpec((rb, W), row),
                   pl.BlockSpec((nc, H, DH, DH), lambda i: (i, 0, 0, 0)),
                   pl.BlockSpec((nc, H, CH, CH), lambda i: (i, 0, 0, 0))),
        scratch_shapes=[pltpu.VMEM((H, DH, DH), f32)],
        compiler_params=_cp(dimension_semantics=("arbitrary",)))(q, k, v, g, beta, z, nw)


def _delta_bwd(dy, o, z, nw, q, k, v, g, beta, s_all, t_all, *, H):
    L = q.shape[0]
    W = H * DH
    rb = TM
    nc = rb // CH
    nblk = L // rb

    def body(dy_ref, o_ref, z_ref, nw_ref, q_ref, k_ref, v_ref, g_ref, b_ref, s_ref, t_ref,
             dq_ref, dk_ref, dv_ref, dz_ref, dg_ref, db_ref, dnw_ref, ds_scr):
        @pl.when(pl.program_id(0) == 0)
        def _():
            ds_scr[...] = jnp.zeros_like(ds_scr)
            dnw_ref[...] = jnp.zeros_like(dnw_ref)

        ii = lax.broadcasted_iota(jnp.int32, (CH, CH), 0)
        jj = lax.broadcasted_iota(jnp.int32, (CH, CH), 1)
        ltri = (ii >= jj).astype(f32)
        utri = (jj >= ii).astype(f32)
        lane = lax.broadcasted_iota(jnp.int32, (CH, LANE), 1)
        last_row = lax.broadcasted_iota(jnp.int32, (CH, 1), 0) == CH - 1
        nwv = nw_ref[...]

        def chunk(cc, carry):
            c = nc - 1 - cc
            r0 = pl.multiple_of(c * CH, CH)
            rows = pl.ds(r0, CH)
            gam = _dot(ltri, g_ref[rows, :], HI)
            gam_t = gam.T
            bb = b_ref[rows, :]
            dgam_all = jnp.zeros((CH, LANE), f32)
            dbeta_all = jnp.zeros((CH, LANE), f32)
            dnw_acc = jnp.zeros((1, DH), f32)
            for hh in range(H):
                hs = slice(hh * DH, (hh + 1) * DH)
                gcol, grow, glast = gam[:, hh:hh + 1], gam_t[hh:hh + 1, :], gam[CH - 1:CH, hh:hh + 1]
                bcol = bb[:, hh:hh + 1]
                qh, kh, vh = q_ref[rows, hs], k_ref[rows, hs], v_ref[rows, hs]
                oh, zh, dyh = o_ref[rows, hs], z_ref[rows, hs], dy_ref[rows, hs]
                rms = lax.rsqrt(jnp.mean(oh * oh, axis=-1, keepdims=True) + RMS_EPS)
                on = oh * rms
                sig = _sigmoid(zh)
                sz = zh * sig
                dz_ref[rows, hs] = dyh * on * nwv * (sig * (1.0 + zh * (1.0 - sig)))
                dnw_acc = dnw_acc + jnp.sum(dyh * on * sz, axis=0, keepdims=True)
                don = dyh * nwv * sz
                do = rms * (don - on * jnp.mean(don * on, axis=-1, keepdims=True))
                dec, eg, kb, kbg, vb, qt, kt, kk, qk = _chunk_local(qh, kh, vh, gcol, grow, glast, bcol, ii, jj)
                t = t_ref[c, hh]
                st = s_ref[c, hh]
                stb = _bf(st)
                mm = jnp.where(ii > jj, kk * dec, 0.0)
                attn = qk * dec
                u = _dot(t, vb, HI)
                w = _dot(t, kbg, HI)
                vn = u - _dot(_bf(w), stb)
                egl = jnp.exp(glast)
                ekt = jnp.exp(glast - gcol)
                dsn = ds_scr[hh]
                dsnb = _bf(dsn)
                dob = _bf(do)
                vnb = _bf(vn)
                dvn = _dot_tn(_bf(attn), dob) + _dot(_bf(kt), dsnb)
                dvnb = _bf(dvn)
                dqt = _dot_nt(dob, stb)
                dattn = jnp.where(ii >= jj, _dot_nt(dob, vnb), 0.0)
                dkt = _dot_nt(vnb, dsnb)
                ds_scr[hh] = _dot_tn(_bf(qt), dob) + egl * dsn - _dot_tn(_bf(w), dvnb)
                dglast = egl * jnp.sum(jnp.sum(dsn * st, axis=1, keepdims=True), axis=0, keepdims=True)
                dw = -_dot_nt(dvnb, stb)
                tb = _bf(t)
                dwb = _bf(dw)
                dvb = _dot_tn(tb, dvnb)
                dkbg = _dot_tn(tb, dwb)
                dt = _dot_nt(dvnb, _bf(vb)) + _dot_nt(dwb, _bf(kbg))
                dm = -_dot_tn(tb, _bf(_dot_nt(_bf(dt), tb)))
                dm = jnp.where(ii > jj, dm, 0.0)
                dkk = dm * dec
                dqk = dattn * dec
                e = dm * mm + dattn * attn
                dgam = jnp.sum(e, axis=1, keepdims=True) - jnp.sum(e.T, axis=1, keepdims=True)
                dkkb = _bf(dkk)
                dqkb = _bf(dqk)
                kbf = _bf(kh)
                dkb = _dot(dkkb, kbf) + dkbg * eg
                dk = _dot_tn(dkkb, _bf(kb)) + _dot_tn(dqkb, _bf(qh)) + dkt * ekt + dkb * bcol
                dq = _dot(dqkb, kbf) + dqt * eg
                dktkt = jnp.sum(dkt * kt, axis=1, keepdims=True)
                dgam = dgam + jnp.sum(dqt * qt, axis=1, keepdims=True) - dktkt + jnp.sum(dkbg * kbg, axis=1, keepdims=True)
                dglast = dglast + jnp.sum(dktkt, axis=0, keepdims=True)
                dgam = dgam + jnp.where(last_row, dglast, 0.0)
                dbeta = jnp.sum(dkb * kh, axis=1, keepdims=True) + jnp.sum(dvb * vh, axis=1, keepdims=True)
                dq_ref[rows, hs] = dq
                dk_ref[rows, hs] = dk
                dv_ref[rows, hs] = dvb * bcol
                dgam_all = dgam_all + jnp.where(lane == hh, dgam, 0.0)
                dbeta_all = dbeta_all + jnp.where(lane == hh, dbeta, 0.0)
            dg_ref[rows, :] = _dot(utri, dgam_all, HI)
            db_ref[rows, :] = dbeta_all
            dnw_ref[0:1, :] += dnw_acc
            return carry

        lax.fori_loop(0, nc, chunk, 0)

    rev = lambda i: (nblk - 1 - i, 0)
    rev4 = lambda i: (nblk - 1 - i, 0, 0, 0)
    fix = lambda i: (0, 0)
    wide = pl.BlockSpec((rb, W), rev)
    thin = pl.BlockSpec((rb, LANE), rev)
    return pl.pallas_call(
        body, name="delta_bwd", grid=(nblk,),
        out_shape=(S((L, W), f32), S((L, W), f32), S((L, W), f32), S((L, W), f32),
                   S((L, LANE), f32), S((L, LANE), f32), S((8, DH), f32)),
        in_specs=[wide, wide, wide, pl.BlockSpec((1, DH), fix), wide, wide, wide, thin, thin,
                  pl.BlockSpec((nc, H, DH, DH), rev4), pl.BlockSpec((nc, H, CH, CH), rev4)],
        out_specs=(wide, wide, wide, wide, thin, thin, pl.BlockSpec((8, DH), fix)),
        scratch_shapes=[pltpu.VMEM((H, DH, DH), f32)],
        compiler_params=_cp(dimension_semantics=("arbitrary",)))(dy, o, z, nw, q, k, v, g, beta, s_all, t_all)


def _sc_fwd(h, w_in, conv_w, w_out, g, b, *, first_row):
    L, D = h.shape
    W = w_out.shape[0]
    KW = conv_w.shape[0]
    tm = TM
    alpha = ALPHA

    def body(h_ref, win_hbm, cw_ref, wout_hbm, g_ref, b_ref, proj_ref, bu_ref, pre_ref, out_ref,
             win, wout, scr, carry, sem):
        i = pl.program_id(0)
        _load_once([(win_hbm, win), (wout_hbm, wout)], sem)

        @pl.when(i == 0)
        def _():
            carry[...] = jnp.zeros_like(carry)

        hv = h_ref[...]
        hb = _bf(hv)
        bg = _dot(hb, win[:, 0:W])
        cg = _dot(hb, win[:, W:2 * W])
        xv = _dot(hb, win[:, 2 * W:3 * W])
        proj_ref[:, 0:W] = bg
        proj_ref[:, W:2 * W] = cg
        proj_ref[:, 2 * W:3 * W] = xv
        p = cg * xv
        scr[0:8, :] = carry[...]
        scr[8:8 + tm, :] = p
        carry[...] = p[tm - 8:tm, :]
        u = jnp.zeros((tm, W), f32)
        for j in range(KW):
            u = u + cw_ref[j:j + 1, :] * scr[pl.ds(8 - (KW - 1) + j, tm), :]
        bu = _bf(bg * u)
        bu_ref[...] = bu
        pre = alpha * hv + _dot(bu, wout[...])
        pre_ref[...] = pre
        out_ref[...] = _ln_fwd(pre, g_ref[...], b_ref[...], _row_ids(i, tm, D), first_row)

    row = lambda i: (i, 0)
    fix = lambda i: (0, 0)
    return pl.pallas_call(
        body, name="sc_fwd", grid=(L // tm,),
        out_shape=(S((L, 3 * W), f32), S((L, W), bf16), S((L, D), f32), S((L, D), f32)),
        in_specs=[pl.BlockSpec((tm, D), row), ANY, pl.BlockSpec((KW, W), fix), ANY,
                  pl.BlockSpec((1, D), fix), pl.BlockSpec((1, D), fix)],
        out_specs=(pl.BlockSpec((tm, 3 * W), row), pl.BlockSpec((tm, W), row),
                   pl.BlockSpec((tm, D), row), pl.BlockSpec((tm, D), row)),
        scratch_shapes=[pltpu.VMEM((D, 3 * W), w_in.dtype), pltpu.VMEM((W, D), w_out.dtype), pltpu.VMEM((tm + 8, W), f32),
                        pltpu.VMEM((8, W), f32), pltpu.SemaphoreType.DMA((2,))],
        compiler_params=_cp(dimension_semantics=("arbitrary",)))(h, w_in, conv_w, w_out, g, b)


def _sc_bwd(dbu, proj, conv_w):
    L, W = dbu.shape
    KW = conv_w.shape[0]
    tm = TM
    nb = L // tm

    def body(dbu_ref, proj_ref, hc_ref, hx_ref, cw_ref, dproj_ref, dcw_ref, scrx, scrd, carry):
        i = pl.program_id(0)
        blk = nb - 1 - i

        @pl.when(i == 0)
        def _():
            carry[...] = jnp.zeros_like(carry)
            dcw_ref[...] = jnp.zeros_like(dcw_ref)

        bg, cg, xv = proj_ref[:, 0:W], proj_ref[:, W:2 * W], proj_ref[:, 2 * W:3 * W]
        scrx[0:8, :] = jnp.where(blk > 0, hc_ref[...] * hx_ref[...], 0.0)
        scrx[8:8 + tm, :] = cg * xv
        u = jnp.zeros((tm, W), f32)
        for j in range(KW):
            u = u + cw_ref[j:j + 1, :] * scrx[pl.ds(8 - (KW - 1) + j, tm), :]
        d = dbu_ref[...]
        dproj_ref[:, 0:W] = _bf(d * u)
        du = d * bg
        scrd[0:tm, :] = du
        scrd[tm:tm + 8, :] = carry[...]
        carry[...] = du[0:8, :]
        dp = jnp.zeros((tm, W), f32)
        for j in range(KW):
            dp = dp + cw_ref[j:j + 1, :] * scrd[pl.ds(KW - 1 - j, tm), :]
            dcw_ref[j:j + 1, :] += jnp.sum(du * scrx[pl.ds(8 - (KW - 1) + j, tm), :], axis=0, keepdims=True)
        dproj_ref[:, W:2 * W] = _bf(dp * xv)
        dproj_ref[:, 2 * W:3 * W] = _bf(dp * cg)

    rev = lambda i: (nb - 1 - i, 0)
    fix = lambda i: (0, 0)

    def halo(col):
        return pl.BlockSpec((8, W), lambda i: (jnp.maximum((nb - 1 - i) * (tm // 8) - 1, 0), col))

    return pl.pallas_call(
        body, name="sc_bwd", grid=(nb,), out_shape=(S((L, 3 * W), bf16), S((8, W), f32)),
        in_specs=[pl.BlockSpec((tm, W), rev), pl.BlockSpec((tm, 3 * W), rev), halo(1), halo(2),
                  pl.BlockSpec((KW, W), fix)],
        out_specs=(pl.BlockSpec((tm, 3 * W), rev), pl.BlockSpec((8, W), fix)),
        scratch_shapes=[pltpu.VMEM((tm + 8, W), f32), pltpu.VMEM((tm + 8, W), f32), pltpu.VMEM((8, W), f32)],
        compiler_params=_cp(dimension_semantics=("arbitrary",)))(dbu, proj, proj, proj, conv_w)


def _ffn_cols(F):
    fc = F
    for cand in (1408, 1024, 512, 256, 128):
        if F % cand == 0:
            fc = cand
            break
    return fc


def _ffn_fwd(h, w_up, conv_w, w_down, g, b, *, first_row, name):
    L, D = h.shape
    F = w_down.shape[0]
    KW = conv_w.shape[0]
    tm = TM
    fc = _ffn_cols(F)
    alpha = ALPHA

    def body(h_ref, wup_hbm, cw_ref, wdn_hbm, g_ref, b_ref, up_ref, a_ref, pre_ref, out_ref,
             wup, wdn, scr, carry, sem):
        i = pl.program_id(0)
        _load_once([(wup_hbm, wup), (wdn_hbm, wdn)], sem)

        @pl.when(i == 0)
        def _():
            carry[...] = jnp.zeros_like(carry)

        hv = h_ref[...]
        hb = _bf(hv)
        pre = alpha * hv
        for c0 in range(0, F, fc):
            cs = slice(c0, c0 + fc)
            u = _dot(hb, wup[:, cs])
            gate = _dot(hb, wup[:, F + c0:F + c0 + fc])
            up_ref[:, cs] = u
            up_ref[:, F + c0:F + c0 + fc] = gate
            scr[0:8, :] = carry[:, cs]
            scr[8:8 + tm, :] = u
            carry[:, cs] = u[tm - 8:tm, :]
            uc = jnp.zeros((tm, fc), f32)
            for j in range(KW):
                uc = uc + cw_ref[j:j + 1, cs] * scr[pl.ds(8 - (KW - 1) + j, tm), :]
            ab = _bf(uc * _sigmoid(uc) * gate)
            a_ref[:, cs] = ab
            pre = pre + _dot(ab, wdn[cs, :])
        pre_ref[...] = pre
        out_ref[...] = _ln_fwd(pre, g_ref[...], b_ref[...], _row_ids(i, tm, D), first_row)

    row = lambda i: (i, 0)
    fix = lambda i: (0, 0)
    return pl.pallas_call(
        body, name=name, grid=(L // tm,),
        out_shape=(S((L, 2 * F), f32), S((L, F), bf16), S((L, D), f32), S((L, D), f32)),
        in_specs=[pl.BlockSpec((tm, D), row), ANY, pl.BlockSpec((KW, F), fix), ANY,
                  pl.BlockSpec((1, D), fix), pl.BlockSpec((1, D), fix)],
        out_specs=(pl.BlockSpec((tm, 2 * F), row), pl.BlockSpec((tm, F), row),
                   pl.BlockSpec((tm, D), row), pl.BlockSpec((tm, D), row)),
        scratch_shapes=[pltpu.VMEM((D, 2 * F), w_up.dtype), pltpu.VMEM((F, D), w_down.dtype), pltpu.VMEM((tm + 8, fc), f32),
                        pltpu.VMEM((8, F), f32), pltpu.SemaphoreType.DMA((2,))],
        compiler_params=_cp(dimension_semantics=("arbitrary",)))(h, w_up, conv_w, w_down, g, b)


def _ffn_bwd(dpre, up, w_down, conv_w, *, name):
    L, D = dpre.shape
    F = w_down.shape[0]
    KW = conv_w.shape[0]
    tm = TM
    nb = L // tm
    fc = _ffn_cols(F)

    def body(dpre_ref, up_ref, halo_ref, wdn_hbm, cw_ref, dup_ref, dcw_ref, wdn, scrx, scrd, carry, sem):
        i = pl.program_id(0)
        blk = nb - 1 - i
        _load_once([(wdn_hbm, wdn)], sem)

        @pl.when(i == 0)
        def _():
            carry[...] = jnp.zeros_like(carry)
            dcw_ref[...] = jnp.zeros_like(dcw_ref)

        db = _bf(dpre_ref[...])
        for c0 in range(0, F, fc):
            cs = slice(c0, c0 + fc)
            da = _dot_nt(db, wdn[cs, :])
            gate = up_ref[:, F + c0:F + c0 + fc]
            scrx[0:8, :] = jnp.where(blk > 0, halo_ref[:, cs], 0.0)
            scrx[8:8 + tm, :] = up_ref[:, cs]
            uc = jnp.zeros((tm, fc), f32)
            for j in range(KW):
                uc = uc + cw_ref[j:j + 1, cs] * scrx[pl.ds(8 - (KW - 1) + j, tm), :]
            sig = _sigmoid(uc)
            dup_ref[:, F + c0:F + c0 + fc] = _bf(da * (uc * sig))
            duc = da * gate * (sig * (1.0 + uc * (1.0 - sig)))
            scrd[0:tm, :] = duc
            scrd[tm:tm + 8, :] = carry[:, cs]
            carry[:, cs] = duc[0:8, :]
            du = jnp.zeros((tm, fc), f32)
            for j in range(KW):
                du = du + cw_ref[j:j + 1, cs] * scrd[pl.ds(KW - 1 - j, tm), :]
                dcw_ref[j:j + 1, cs] += jnp.sum(duc * scrx[pl.ds(8 - (KW - 1) + j, tm), :], axis=0, keepdims=True)
            dup_ref[:, cs] = _bf(du)

    rev = lambda i: (nb - 1 - i, 0)
    fix = lambda i: (0, 0)
    return pl.pallas_call(
        body, name=name, grid=(nb,), out_shape=(S((L, 2 * F), bf16), S((8, F), f32)),
        in_specs=[pl.BlockSpec((tm, D), rev), pl.BlockSpec((tm, 2 * F), rev),
                  pl.BlockSpec((8, F), lambda i: (jnp.maximum((nb - 1 - i) * (tm // 8) - 1, 0), 0)),
                  ANY, pl.BlockSpec((KW, F), fix)],
        out_specs=(pl.BlockSpec((tm, 2 * F), rev), pl.BlockSpec((8, F), fix)),
        scratch_shapes=[pltpu.VMEM((F, D), w_down.dtype), pltpu.VMEM((tm + 8, fc), f32), pltpu.VMEM((tm + 8, fc), f32),
                        pltpu.VMEM((8, F), f32), pltpu.SemaphoreType.DMA((1,))],
        compiler_params=_cp(dimension_semantics=("arbitrary",)))(dpre, up, up, w_down, conv_w)


def _loss_head(h, target):
    L, D = h.shape
    tm = TM
    pb = PADF // tm

    def body(h_ref, t_ref, dh_ref, loss_ref):
        i = pl.program_id(0)

        @pl.when(i == 0)
        def _():
            loss_ref[...] = jnp.zeros_like(loss_ref)

        valid = i >= pb
        err = h_ref[...] - t_ref[...]
        dh_ref[...] = jnp.where(valid, err * (1.0 / D), 0.0)
        part = 0.5 * jnp.sum(jnp.sum(err * err, axis=-1, keepdims=True) * (1.0 / D), axis=0, keepdims=True)
        loss_ref[...] += jnp.where(valid, part, 0.0)

    return pl.pallas_call(
        body, name="loss_head", grid=(L // tm,), out_shape=(S((L, D), f32), S((8, LANE), f32)),
        in_specs=[pl.BlockSpec((tm, D), lambda i: (i, 0)),
                  pl.BlockSpec((tm, D), lambda i: (jnp.maximum(i - pb, 0), 0))],
        out_specs=(pl.BlockSpec((tm, D), lambda i: (i, 0)), pl.BlockSpec((8, LANE), lambda i: (0, 0))),
        compiler_params=_cp(dimension_semantics=("arbitrary",)))(h, target)


def _ln_bwd(dout, pre, g, *, first_row, name):
    L, D = pre.shape
    tm = TM

    def body(do_ref, pre_ref, g_ref, dpre_ref, dg_ref, db_ref):
        i = pl.program_id(0)

        @pl.when(i == 0)
        def _():
            dg_ref[...] = jnp.zeros_like(dg_ref)
            db_ref[...] = jnp.zeros_like(db_ref)

        pre_v = pre_ref[...]
        mu = jnp.mean(pre_v, axis=-1, keepdims=True)
        xc = pre_v - mu
        rstd = lax.rsqrt(jnp.mean(xc * xc, axis=-1, keepdims=True) + LN_EPS)
        xh = xc * rstd
        dy = jnp.where(_row_ids(i, tm, D) >= first_row, do_ref[...], 0.0)
        dg_ref[0:1, :] += jnp.sum(dy * xh, axis=0, keepdims=True)
        db_ref[0:1, :] += jnp.sum(dy, axis=0, keepdims=True)
        dxh = dy * g_ref[...]
        dpre_ref[...] = rstd * (dxh - jnp.mean(dxh, axis=-1, keepdims=True)
                                - xh * jnp.mean(dxh * xh, axis=-1, keepdims=True))

    row = lambda i: (i, 0)
    fix = lambda i: (0, 0)
    return pl.pallas_call(
        body, name=name, grid=(L // tm,), out_shape=(S((L, D), f32), S((8, D), f32), S((8, D), f32)),
        in_specs=[pl.BlockSpec((tm, D), row), pl.BlockSpec((tm, D), row), pl.BlockSpec((1, D), fix)],
        out_specs=(pl.BlockSpec((tm, D), row), pl.BlockSpec((8, D), fix), pl.BlockSpec((8, D), fix)),
        compiler_params=_cp(dimension_semantics=("arbitrary",)))(dout, pre, g)


def _adamw(g_terms, w, m, v, *, name):
    R, C = w.shape
    tr = _row_tile(R)
    n = len(g_terms)
    c1 = 1.0 - ADAM_B1 ** ADAM_STEP
    c2 = 1.0 - ADAM_B2 ** ADAM_STEP

    def body(*refs):
        g = refs[0][...].astype(f32)
        for r in refs[1:n]:
            g = g + r[...].astype(f32)
        w_ref, m_ref, v_ref, g_out, d_out, m_out, v_out = refs[n:]
        mn = ADAM_B1 * m_ref[...] + (1.0 - ADAM_B1) * g
        vn = ADAM_B2 * v_ref[...] + (1.0 - ADAM_B2) * (g * g)
        g_out[...] = g
        m_out[...] = mn
        v_out[...] = vn
        d_out[...] = -ADAM_LR * ((mn / c1) / (jnp.sqrt(vn / c2) + ADAM_EPS) + ADAM_WD * w_ref[...])

    spec = pl.BlockSpec((tr, C), lambda i: (i, 0))
    return pl.pallas_call(
        body, name=name, grid=(R // tr,), out_shape=(S((R, C), f32),) * 4,
        in_specs=[spec] * (n + 3), out_specs=(spec,) * 4,
        compiler_params=_cp(dimension_semantics=("arbitrary",)))(*g_terms, w, m, v)


def _sum_devices(x):
    n, R, C = x.shape

    def body(x_ref, o_ref):
        acc = x_ref[0]
        for d in range(1, n):
            acc = acc + x_ref[d]
        o_ref[...] = acc

    return pl.pallas_call(body, name="sum_devices", out_shape=S((R, C), f32), compiler_params=_cp())(x)


def _row_tile(R):
    for step in (16, 8):
        for t in range(256, 0, -step):
            if R % t == 0:
                return t
    return R


def _pair_add(keep, recv, slots, *, out_dtype, name):
    _, R, C = keep.shape
    n = slots.shape[0]
    tr = _row_tile(R)

    def body(slots_ref, k_ref, r_ref, o_ref):
        o_ref[0] = (k_ref[0] + r_ref[0].astype(f32)).astype(out_dtype)

    grid_spec = pltpu.PrefetchScalarGridSpec(
        num_scalar_prefetch=1, grid=(n, R // tr),
        in_specs=[pl.BlockSpec((1, tr, C), lambda j, i, sl: (sl[j], i, 0)),
                  pl.BlockSpec((1, tr, C), lambda j, i, sl: (sl[j], i, 0))],
        out_specs=pl.BlockSpec((1, tr, C), lambda j, i, sl: (j, i, 0)))
    return pl.pallas_call(
        body, name=name, grid_spec=grid_spec, out_shape=S((n, R, C), out_dtype),
        compiler_params=_cp(dimension_semantics=("arbitrary", "arbitrary")))(slots, keep, recv)


def _peer(rel):
    x, y, c = lax.axis_index("x"), lax.axis_index("y"), lax.axis_index("c")
    return {"c": (x, y, 1 - c), "x": (1 - x, y, c), "y": (x, 1 - y, c), "xy": (1 - x, 1 - y, c)}[rel]


def _all_gather(x, *, name):
    R, C = x.shape

    def body(x_ref, out_ref, send_sems, recv_sems, local_sem):
        mx, my, mc = lax.axis_index("x"), lax.axis_index("y"), lax.axis_index("c")
        me, sibling = (mx, my, mc), (mx, my, 1 - mc)
        chips = [(1 - mx, my), (mx, 1 - my), (1 - mx, 1 - my)]

        def rows(px, py, pc):
            return out_ref.at[4 * px + 2 * py + pc]

        def copy(kk, block, to, src=None):
            return pltpu.make_async_remote_copy(
                src_ref=rows(*block) if src is None else src, dst_ref=rows(*block),
                send_sem=send_sems.at[kk], recv_sem=recv_sems.at[kk], device_id=to, device_id_type=MESH)

        mine = pltpu.make_async_copy(x_ref, rows(*me), local_sem)
        mine.start()
        first = [copy(0, me, sibling, src=x_ref)]
        first += [copy(1 + j, me, (*chip, mc), src=x_ref) for j, chip in enumerate(chips)]
        for cp in first:
            cp.start()
        passed = [copy(4 + j, (*chip, mc), sibling) for j, chip in enumerate(chips)]
        for j, chip in enumerate(chips):
            copy(1 + j, (*chip, mc), me).wait_recv()
            passed[j].start()
        copy(0, sibling, me).wait_recv()
        for j, chip in enumerate(chips):
            copy(4 + j, (*chip, 1 - mc), me).wait_recv()
        for cp in first + passed:
            cp.wait_send()
        mine.wait()

    return pl.pallas_call(
        body, name=name, out_shape=S((N_DEV, R, C), x.dtype), in_specs=[ANY], out_specs=ANY,
        scratch_shapes=[pltpu.SemaphoreType.DMA((7,)), pltpu.SemaphoreType.DMA((7,)), pltpu.SemaphoreType.DMA],
        compiler_params=pltpu.CompilerParams(has_side_effects=True))(x)


def _exchange(send, rels, *, name):
    n = send.shape[0]

    def body(send_ref, recv_ref, send_sems, recv_sems):
        cps = [pltpu.make_async_remote_copy(
            src_ref=send_ref.at[j], dst_ref=recv_ref.at[j], send_sem=send_sems.at[j], recv_sem=recv_sems.at[j],
            device_id=_peer(rels[j]), device_id_type=MESH) for j in range(n)]
        for cp in cps:
            cp.start()
        for cp in cps:
            cp.wait()

    return pl.pallas_call(
        body, name=name, out_shape=S(send.shape, send.dtype), in_specs=[ANY], out_specs=ANY,
        scratch_shapes=[pltpu.SemaphoreType.DMA((n,)), pltpu.SemaphoreType.DMA((n,))],
        compiler_params=pltpu.CompilerParams(has_side_effects=True))(send)


def _rows16(n):
    return -(-n // 16) * 16


class _Big:
    def __init__(self, name, shard_shape, col_sharded, width):
        self.name, self.shard_shape, self.col_sharded, self.width = name, tuple(shard_shape), col_sharded, width
        self.size = math.prod(shard_shape)
        assert self.size % width == 0
        self.rows = self.size // width
        self.prows = _rows16(self.rows)

    def to_flat(self, a, dtype):
        a = a.astype(dtype).reshape(self.rows, self.width)
        return jnp.pad(a, ((0, self.prows - self.rows), (0, 0))) if self.prows != self.rows else a

    def from_flat(self, a):
        return a[:self.rows].reshape(self.shard_shape)

    def full_from_shards(self, a):
        sh = a[:, :self.rows].reshape((N_DEV,) + self.shard_shape)
        nd = len(self.shard_shape)
        if self.col_sharded:
            perm = tuple(range(1, nd)) + (0, nd)
            full = jnp.transpose(sh, perm)
            return full.reshape(self.shard_shape[:-1] + (N_DEV * self.shard_shape[-1],))
        perm = tuple(range(1, nd - 1)) + (0, nd - 1, nd)
        full = jnp.transpose(sh, perm)
        return full.reshape(self.shard_shape[:-2] + (N_DEV * self.shard_shape[-2], self.shard_shape[-1]))

    def shards_from_full(self, full):
        nd = len(self.shard_shape)
        if self.col_sharded:
            sh = full.reshape(self.shard_shape[:-1] + (N_DEV, self.shard_shape[-1]))
            sh = jnp.transpose(sh, (nd - 1,) + tuple(range(nd - 1)) + (nd,))
        else:
            sh = full.reshape(self.shard_shape[:-2] + (N_DEV,) + self.shard_shape[-2:])
            sh = jnp.transpose(sh, (nd - 2,) + tuple(range(nd - 2)) + (nd - 1, nd))
        sh = sh.reshape(N_DEV, self.rows, self.width)
        return jnp.pad(sh, ((0, 0), (0, self.prows - self.rows), (0, 0))) if self.prows != self.rows else sh


def _pack_small(parts, width):
    rows, offs, r = [], [], 0
    for a in parts:
        n = a.size
        nr = -(-n // width)
        flat = a.reshape(-1).astype(f32)
        if nr * width != n:
            flat = jnp.pad(flat, (0, nr * width - n))
        rows.append(flat.reshape(nr, width))
        offs.append((r, nr))
        r += nr
    buf = jnp.concatenate(rows, axis=0)
    pad = (-r) % 8
    if pad:
        buf = jnp.pad(buf, ((0, pad), (0, 0)))
    return buf, offs


def _unpack_small(buf, off, shape):
    r, nr = off
    return buf[r:r + nr].reshape(-1)[:math.prod(shape)].reshape(shape)


def _local_step(x, target, meta, wts, small):
    SEQ, D = x.shape
    n_meta = meta.shape[0]
    first_row = PADF - n_meta
    H = small["a_log"].shape[-1]
    W = H * DH
    F = wts["ffn_w_down"].shape[1]

    h0 = jnp.concatenate([jnp.zeros((first_row, D), f32), meta, x], axis=0)

    def lanes(a):
        return jnp.pad(a.reshape(1, -1), ((0, 0), (0, LANE - a.size)))

    alog, dtb = lanes(small["a_log"][0]), lanes(small["a_dt_bias"][0])
    a_conv, b_conv = small["a_conv"][0], small["b_conv"][0]
    nw = small["a_norm"][0].reshape(1, DH)
    lmg, lmb, lfg, lfb = small["ln_mix_g"], small["ln_mix_b"], small["ln_ffn_g"], small["ln_ffn_b"]

    pre_a, z, raw, q, k, v, beta, g = _gdn_in_fwd(h0, wts["a_w_in"], a_conv, alog, dtb, first_row=first_row, H=H)
    o, y, s_all, t_all = _delta_fwd(q, k, v, g, beta, z, nw, H=H)
    pre1, h1 = _out_res_ln(y, wts["a_w_out"], h0, lmg[0:1], lmb[0:1], first_row=first_row, name="gdn_out_ln")
    up0, act0, pre2, h2 = _ffn_fwd(h1, wts["ffn_w_up"][0], small["ffn_conv"][0], wts["ffn_w_down"][0],
                                   lfg[0:1], lfb[0:1], first_row=first_row, name="ffn_fwd0")
    proj_b, bu, pre3, h3 = _sc_fwd(h2, wts["b_w_in"], b_conv, wts["b_w_out"], lmg[1:2], lmb[1:2], first_row=first_row)
    up1, act1, pre4, h4 = _ffn_fwd(h3, wts["ffn_w_up"][1], small["ffn_conv"][1], wts["ffn_w_down"][1],
                                   lfg[1:2], lfb[1:2], first_row=first_row, name="ffn_fwd1")
    dh4, loss_tile = _loss_head(h4, target)

    gw, gs = {}, {}
    alpha = ALPHA

    def ffn_backward(dh_out, pre, up, act, h_in, layer, tag):
        dpre, dg, db = _ln_bwd(dh_out, pre, lfg[layer:layer + 1], first_row=first_row, name="ln_bwd_ffn" + tag)
        dup, dcw = _ffn_bwd(dpre, up, wts["ffn_w_down"][layer], small["ffn_conv"][layer], name="ffn_bwd" + tag)
        dwd = _linear_dw(act, dpre, name="dw_down" + tag)
        dwu = _linear_dw(h_in, dup, name="dw_up" + tag)
        dh_in = _linear_dx(dup, wts["ffn_w_up"][layer], dpre, alpha=alpha, out_dtype=f32, name="dx_up" + tag)
        return dh_in, dwu, dwd, dcw[0:3], dg[0], db[0]

    dh3, dwu1, dwd1, dcf1, dlfg1, dlfb1 = ffn_backward(dh4, pre4, up1, act1, h3, 1, "1")

    dpre3, dlmg1, dlmb1 = _ln_bwd(dh3, pre3, lmg[1:2], first_row=first_row, name="ln_bwd_mix1")
    dbu = _linear_dx(dpre3, wts["b_w_out"], None, alpha=0.0, out_dtype=f32, name="dx_b_out")
    gw["b_w_out"] = _linear_dw(bu, dpre3, name="dw_b_out")[None]
    dproj_b, dcb = _sc_bwd(dbu, proj_b, b_conv)
    gw["b_w_in"] = _linear_dw(h2, dproj_b, name="dw_b_in")[None]
    dh2 = _linear_dx(dproj_b, wts["b_w_in"], dpre3, alpha=alpha, out_dtype=f32, name="dx_b_in")

    dh1, dwu0, dwd0, dcf0, dlfg0, dlfb0 = ffn_backward(dh2, pre2, up0, act0, h1, 0, "0")

    dpre1, dlmg0, dlmb0 = _ln_bwd(dh1, pre1, lmg[0:1], first_row=first_row, name="ln_bwd_mix0")
    dy = _linear_dx(dpre1, wts["a_w_out"], None, alpha=0.0, out_dtype=f32, name="dx_a_out")
    gw["a_w_out"] = _linear_dw(y, dpre1, name="dw_a_out")[None]
    dq, dk, dv, dz, dg_, dbeta, dnw = _delta_bwd(dy, o, z, nw, q, k, v, g, beta, s_all, t_all, H=H)
    dproj_a, dca, dal, ddt = _gdn_in_bwd(dq, dk, dv, dz, dg_, dbeta, pre_a, raw, a_conv, alog, dtb,
                                         first_row=first_row, H=H)
    dwa = _linear_dw(h0, dproj_a, name="dw_a_in")
    gw["a_w_in"] = jnp.concatenate([dwa[:, :4 * W], dwa[:, 4 * W:4 * W + H], dwa[:, 4 * W + LANE:4 * W + LANE + H]],
                                   axis=1)[None]
    dh0 = _linear_dx(dproj_a, wts["a_w_in"], dpre1, alpha=alpha, out_dtype=f32, name="dx_a_in")

    gw["ffn_w_up"] = jnp.stack([dwu0, dwu1])
    gw["ffn_w_down"] = jnp.stack([dwd0, dwd1])
    gs["meta"] = dh0[first_row:PADF]
    gs["a_conv"] = dca[0:a_conv.shape[0]][None]
    gs["a_log"] = dal[0:1, 0:H]
    gs["a_dt_bias"] = ddt[0:1, 0:H]
    gs["a_norm"] = dnw[0:1]
    gs["b_conv"] = dcb[0:b_conv.shape[0]][None]
    gs["ln_mix_g"] = jnp.stack([dlmg0[0], dlmg1[0]])
    gs["ln_mix_b"] = jnp.stack([dlmb0[0], dlmb1[0]])
    gs["ffn_conv"] = jnp.stack([dcf0, dcf1])
    gs["ln_ffn_g"] = jnp.stack([dlfg0, dlfg1])
    gs["ln_ffn_b"] = jnp.stack([dlfb0, dlfb1])
    return loss_tile, dh0[PADF:], gw, gs


_BIG = ("a_w_in", "a_w_out", "b_w_in", "b_w_out", "ffn_w_up", "ffn_w_down")
_BIG_COL = {"a_w_in": True, "a_w_out": False, "b_w_in": True, "b_w_out": False, "ffn_w_up": True, "ffn_w_down": False}
_SMALL = ("meta", "a_conv", "a_log", "a_dt_bias", "a_norm", "b_conv", "ln_mix_g", "ln_mix_b",
          "ffn_conv", "ln_ffn_g", "ln_ffn_b")
_SMALL_SHARDED = ("meta", "a_conv", "b_conv", "ffn_conv")
_ORDER = ("meta", "a_w_in", "a_conv", "a_log", "a_dt_bias", "a_norm", "a_w_out", "b_w_in", "b_conv", "b_w_out",
          "ln_mix_g", "ln_mix_b", "ffn_w_up", "ffn_conv", "ffn_w_down", "ln_ffn_g", "ln_ffn_b")


def _expand_a_w_in(w, H):
    W = H * DH
    z = jnp.zeros((w.shape[0], LANE - H), w.dtype)
    return jnp.concatenate([w[:, :4 * W], w[:, 4 * W:4 * W + H], z, w[:, 4 * W + H:], z], axis=1)


def kernel(x, meta, a_w_in, a_conv, a_log, a_dt_bias, a_norm, a_w_out, b_w_in, b_conv, b_w_out, ln_mix_g, ln_mix_b, ffn_w_up, ffn_conv, ffn_w_down, ln_ffn_g, ln_ffn_b, loss_target, m_meta, m_a_w_in, m_a_conv, m_a_log, m_a_dt_bias, m_a_norm, m_a_w_out, m_b_w_in, m_b_conv, m_b_w_out, m_ln_mix_g, m_ln_mix_b, m_ffn_w_up, m_ffn_conv, m_ffn_w_down, m_ln_ffn_g, m_ln_ffn_b, v_meta, v_a_w_in, v_a_conv, v_a_log, v_a_dt_bias, v_a_norm, v_a_w_out, v_b_w_in, v_b_conv, v_b_w_out, v_ln_mix_g, v_ln_mix_b, v_ffn_w_up, v_ffn_conv, v_ffn_w_down, v_ln_ffn_g, v_ln_ffn_b):
    wloc = dict(meta=meta, a_w_in=a_w_in, a_conv=a_conv, a_log=a_log, a_dt_bias=a_dt_bias, a_norm=a_norm,
                a_w_out=a_w_out, b_w_in=b_w_in, b_conv=b_conv, b_w_out=b_w_out, ln_mix_g=ln_mix_g, ln_mix_b=ln_mix_b,
                ffn_w_up=ffn_w_up, ffn_conv=ffn_conv, ffn_w_down=ffn_w_down, ln_ffn_g=ln_ffn_g, ln_ffn_b=ln_ffn_b)
    mloc = dict(meta=m_meta, a_w_in=m_a_w_in, a_conv=m_a_conv, a_log=m_a_log, a_dt_bias=m_a_dt_bias, a_norm=m_a_norm,
                a_w_out=m_a_w_out, b_w_in=m_b_w_in, b_conv=m_b_conv, b_w_out=m_b_w_out, ln_mix_g=m_ln_mix_g,
                ln_mix_b=m_ln_mix_b, ffn_w_up=m_ffn_w_up, ffn_conv=m_ffn_conv, ffn_w_down=m_ffn_w_down,
                ln_ffn_g=m_ln_ffn_g, ln_ffn_b=m_ln_ffn_b)
    vloc = dict(meta=v_meta, a_w_in=v_a_w_in, a_conv=v_a_conv, a_log=v_a_log, a_dt_bias=v_a_dt_bias, a_norm=v_a_norm,
                a_w_out=v_a_w_out, b_w_in=v_b_w_in, b_conv=v_b_conv, b_w_out=v_b_w_out, ln_mix_g=v_ln_mix_g,
                ln_mix_b=v_ln_mix_b, ffn_w_up=v_ffn_w_up, ffn_conv=v_ffn_conv, ffn_w_down=v_ffn_w_down,
                ln_ffn_g=v_ln_ffn_g, ln_ffn_b=v_ln_ffn_b)
    H = a_log.shape[-1]
    mx, my, mc = lax.axis_index("x"), lax.axis_index("y"), lax.axis_index("c")
    me = 4 * mx + 2 * my + mc

    D = x.shape[-1]
    big = [_Big(n, wloc[n].shape, _BIG_COL[n], D) for n in _BIG]
    offs, r = {}, 0
    for bgw in big:
        offs[bgw.name] = r
        r += bgw.prows
    R = -(-r // 256) * 256

    def flat_cat(src, dtype):
        parts = [bgw.to_flat(src[bgw.name], dtype) for bgw in big]
        if R != r:
            parts.append(jnp.zeros((R - r, D), dtype))
        return jnp.concatenate(parts, axis=0)

    gathered = _all_gather(flat_cat(wloc, bf16), name="gather_weights")
    wts = {bgw.name: bgw.full_from_shards(gathered[:, offs[bgw.name]:offs[bgw.name] + bgw.prows]) for bgw in big}
    wts["a_w_in"] = _expand_a_w_in(wts["a_w_in"][0], H)
    wts["a_w_out"], wts["b_w_in"], wts["b_w_out"] = wts["a_w_out"][0], wts["b_w_in"][0], wts["b_w_out"][0]

    sm_sh = [wloc[n] for n in _SMALL_SHARDED]
    sbuf, soffs = _pack_small(sm_sh, 128)
    sg = _all_gather(sbuf, name="gather_small")
    small = {n: wloc[n] for n in _SMALL}
    for n, off in zip(_SMALL_SHARDED, soffs):
        sh = wloc[n].shape
        parts = jnp.stack([_unpack_small(sg[d], off, sh) for d in range(N_DEV)])
        nd = len(sh)
        small[n] = jnp.transpose(parts, tuple(range(1, nd)) + (0, nd)).reshape(sh[:-1] + (N_DEV * sh[-1],))

    loss_tile, grad_x, gw, gs = _local_step(x[0], loss_target[0], small["meta"], wts, small)

    gparts = [bgw.shards_from_full(gw[bgw.name]) for bgw in big]
    if R != r:
        gparts.append(jnp.zeros((N_DEV, R - r, D), f32))
    gchip = jnp.concatenate(gparts, axis=1).reshape(4, 2, R, D)
    keep = lax.dynamic_index_in_dim(gchip, mc, axis=1, keepdims=False)
    send = _bf(lax.dynamic_index_in_dim(gchip, 1 - mc, axis=1, keepdims=False))
    recv = _exchange(send, ["c"] * 4, name="grad_to_sibling")
    slot_mine = jnp.stack([2 * mx + my]).astype(jnp.int32)
    slot_others = jnp.stack([2 * (1 - mx) + my, 2 * mx + (1 - my), 2 * (1 - mx) + (1 - my)]).astype(jnp.int32)
    mine = _pair_add(keep, recv, slot_mine, out_dtype=f32, name="pair_add_mine")[0]
    to_send = _pair_add(keep, recv, slot_others, out_dtype=bf16, name="pair_add_send")
    recv2 = _exchange(to_send, ["x", "y", "xy"], name="grad_to_chips")
    g_b, d_b, m_b, v_b = _adamw([mine, recv2[0], recv2[1], recv2[2]], flat_cat(wloc, f32), flat_cat(mloc, f32),
                                flat_cat(vloc, f32), name="adamw_big")

    names = list(_SMALL)
    pbuf, poffs = _pack_small([gs[n] for n in names] + [loss_tile[0:1, 0:1]], 1024)
    psum = _sum_devices(_all_gather(pbuf, name="gather_small_grads"))
    loss = psum[poffs[-1][0], 0]
    g_small = {}
    for n, off in zip(names, poffs[:-1]):
        full_shape = gs[n].shape
        gfull = _unpack_small(psum, off, full_shape)
        if n in _SMALL_SHARDED:
            ns = wloc[n].shape[-1]
            gfull = lax.dynamic_slice_in_dim(gfull, me * ns, ns, axis=gfull.ndim - 1)
        g_small[n] = gfull.reshape(wloc[n].shape)
    gbuf, aoffs = _pack_small([g_small[n] for n in names], 128)
    wbuf, _ = _pack_small([wloc[n] for n in names], 128)
    mbuf, _ = _pack_small([mloc[n] for n in names], 128)
    vbuf, _ = _pack_small([vloc[n] for n in names], 128)
    _, d_s, m_s, v_s = _adamw([gbuf], wbuf, mbuf, vbuf, name="adamw_small")

    grads, deltas, new_m, new_v = {}, {}, {}, {}
    for bgw in big:
        o0 = offs[bgw.name]
        sl = slice(o0, o0 + bgw.prows)
        grads[bgw.name], deltas[bgw.name] = bgw.from_flat(g_b[sl]), bgw.from_flat(d_b[sl])
        new_m[bgw.name], new_v[bgw.name] = bgw.from_flat(m_b[sl]), bgw.from_flat(v_b[sl])
    for n, off in zip(names, aoffs):
        sh = wloc[n].shape
        grads[n] = g_small[n]
        deltas[n], new_m[n], new_v[n] = (_unpack_small(b_, off, sh) for b_ in (d_s, m_s, v_s))
    return (loss, grad_x[None], *[grads[n] for n in _ORDER], *[deltas[n] for n in _ORDER],
            *[new_m[n] for n in _ORDER], *[new_v[n] for n in _ORDER])
```

```python
import math

import jax
import jax.numpy as jnp
from jax import lax
from jax.experimental import pallas as pl
from jax.experimental.pallas import tpu as pltpu

f32, bf16 = jnp.float32, jnp.bfloat16
S = jax.ShapeDtypeStruct
HI = lax.Precision.HIGHEST
HI3 = lax.Precision.HIGH
MESH = pl.DeviceIdType.MESH

V7X_VMEM_LIMIT = 56 * 1024 * 1024
LANE = 128
DH = 128
CH = 64
PADF = 256
TM = 256
TMM = 768
N_DEV = 8
BWD_HEAD_GROUP = 8

DEPTH = 2
ALPHA = (2.0 * DEPTH) ** 0.25
LN_EPS = 1e-5
RMS_EPS = 1e-6
L2_EPS = 1e-6
ADAM_LR, ADAM_B1, ADAM_B2, ADAM_EPS, ADAM_WD, ADAM_STEP = 0.001, 0.9, 0.999, 1e-08, 0.01, 10


def _cp(**kw):
    return pltpu.CompilerParams(vmem_limit_bytes=V7X_VMEM_LIMIT, **kw)


def _bf(x):
    return x.astype(bf16)


def _dot(a, b, precision=None):
    return jnp.dot(a, b, preferred_element_type=f32, precision=precision)


def _dot_nt(a, b):
    return lax.dot_general(a, b, (((1,), (1,)), ((), ())), preferred_element_type=f32)


def _dot_tn(a, b):
    return lax.dot_general(a, b, (((0,), (0,)), ((), ())), preferred_element_type=f32)


def _sigmoid(x):
    return 1.0 / (1.0 + jnp.exp(-x))


def _load_once(pairs, sem):
    @pl.when(pl.program_id(0) == 0)
    def _():
        cps = [pltpu.make_async_copy(src, dst, sem.at[n]) for n, (src, dst) in enumerate(pairs)]
        for c in cps:
            c.start()
        for c in cps:
            c.wait()


def _row_ids(i, tm, width):
    return i * tm + lax.broadcasted_iota(jnp.int32, (tm, width), 0)


def _ln_fwd(pre, g, b, rows, first_row):
    mu = jnp.mean(pre, axis=-1, keepdims=True)
    xc = pre - mu
    var = jnp.mean(xc * xc, axis=-1, keepdims=True)
    y = xc * lax.rsqrt(var + LN_EPS) * g + b
    return jnp.where(rows >= first_row, y, 0.0)


ANY = pl.BlockSpec(memory_space=pl.ANY)


def _linear_dw(x, dy, *, name):
    L, K = x.shape
    N = dy.shape[1]
    tm = TMM if L % TMM == 0 else TM
    tn = LANE
    for d in range(N // LANE, 0, -1):
        if (N // LANE) % d == 0 and K * d * LANE * 4 <= 9 * 1024 * 1024:
            tn = d * LANE
            break

    def body(x_ref, dy_ref, o_ref):
        @pl.when(pl.program_id(1) == 0)
        def _():
            o_ref[...] = jnp.zeros_like(o_ref)
        o_ref[...] += _dot_tn(_bf(x_ref[...]), _bf(dy_ref[...]))

    return pl.pallas_call(
        body, name=name, grid=(N // tn, L // tm), out_shape=S((K, N), f32),
        in_specs=[pl.BlockSpec((tm, K), lambda j, i: (i, 0)), pl.BlockSpec((tm, tn), lambda j, i: (i, j))],
        out_specs=pl.BlockSpec((K, tn), lambda j, i: (0, j)),
        compiler_params=_cp(dimension_semantics=("arbitrary", "arbitrary")))(x, dy)


def _linear_dx(dy, w, res, *, alpha, out_dtype, name):
    L, N = dy.shape
    K = w.shape[0]
    tm = TM
    has_res = res is not None

    def body(*refs):
        if has_res:
            dy_ref, w_hbm, res_ref, o_ref, w_vmem, sem = refs
        else:
            dy_ref, w_hbm, o_ref, w_vmem, sem = refs
        _load_once([(w_hbm, w_vmem)], sem)
        acc = _dot_nt(_bf(dy_ref[...]), w_vmem[...])
        if has_res:
            acc = acc + alpha * res_ref[...]
        o_ref[...] = acc.astype(out_dtype)

    in_specs = [pl.BlockSpec((tm, N), lambda i: (i, 0)), ANY]
    args = [dy, w]
    if has_res:
        in_specs.append(pl.BlockSpec((tm, K), lambda i: (i, 0)))
        args.append(res)
    return pl.pallas_call(
        body, name=name, grid=(L // tm,), out_shape=S((L, K), out_dtype),
        in_specs=in_specs, out_specs=pl.BlockSpec((tm, K), lambda i: (i, 0)),
        scratch_shapes=[pltpu.VMEM((K, N), w.dtype), pltpu.SemaphoreType.DMA((1,))],
        compiler_params=_cp(dimension_semantics=("arbitrary",)))(*args)


def _out_res_ln(y, w, h, g, b, *, first_row, name):
    L, K = y.shape
    D = w.shape[1]
    tm = TM
    alpha = ALPHA

    def body(y_ref, w_hbm, h_ref, g_ref, b_ref, pre_ref, out_ref, w_vmem, sem):
        _load_once([(w_hbm, w_vmem)], sem)
        pre = alpha * h_ref[...] + _dot(_bf(y_ref[...]), w_vmem[...])
        pre_ref[...] = pre
        out_ref[...] = _ln_fwd(pre, g_ref[...], b_ref[...], _row_ids(pl.program_id(0), tm, D), first_row)

    row = lambda i: (i, 0)
    fix = lambda i: (0, 0)
    return pl.pallas_call(
        body, name=name, grid=(L // tm,), out_shape=(S((L, D), f32), S((L, D), f32)),
        in_specs=[pl.BlockSpec((tm, K), row), ANY, pl.BlockSpec((tm, D), row),
                  pl.BlockSpec((1, D), fix), pl.BlockSpec((1, D), fix)],
        out_specs=(pl.BlockSpec((tm, D), row), pl.BlockSpec((tm, D), row)),
        scratch_shapes=[pltpu.VMEM((K, D), w.dtype), pltpu.SemaphoreType.DMA((1,))],
        compiler_params=_cp(dimension_semantics=("arbitrary",)))(y, w, h, g, b)


def _gdn_in_fwd(h, w_full, conv_w, alog, dtb, *, first_row, H):
    L, D = h.shape
    W = H * DH
    NW = w_full.shape[1]
    KW = conv_w.shape[0]
    tm = TM

    def body(h_ref, w_hbm, cw_ref, alog_ref, dtb_ref,
             pre_ref, z_ref, raw_ref, q_ref, k_ref, v_ref, beta_ref, g_ref,
             w_vmem, scr, carry, sem):
        i = pl.program_id(0)
        _load_once([(w_hbm, w_vmem)], sem)

        @pl.when(i == 0)
        def _():
            carry[...] = jnp.zeros_like(carry)

        hb = _bf(h_ref[...])
        outs = (q_ref, k_ref, v_ref)
        for s in range(3):
            pre = _dot(hb, w_vmem[:, s * W:(s + 1) * W])
            pre_ref[:, s * W:(s + 1) * W] = pre
            scr[0:8, :] = carry[s]
            scr[8:8 + tm, :] = pre
            carry[s] = pre[tm - 8:tm, :]
            c = jnp.zeros((tm, W), f32)
            for j in range(KW):
                c = c + cw_ref[j:j + 1, s * W:(s + 1) * W] * scr[pl.ds(8 - (KW - 1) + j, tm), :]
            sl = c * _sigmoid(c)
            if s < 2:
                scale = DH ** -0.5 if s == 0 else 1.0
                for hh in range(H):
                    seg = sl[:, hh * DH:(hh + 1) * DH]
                    r = lax.rsqrt(jnp.sum(seg * seg, axis=-1, keepdims=True) + L2_EPS)
                    outs[s][:, hh * DH:(hh + 1) * DH] = seg * (r * scale)
            else:
                v_ref[...] = sl
        z_ref[...] = _dot(hb, w_vmem[:, 3 * W:4 * W])
        raw = _dot(hb, w_vmem[:, 4 * W:4 * W + 2 * LANE])
        raw_ref[...] = raw
        ok = (_row_ids(i, tm, LANE) >= first_row) & (lax.broadcasted_iota(jnp.int32, (tm, LANE), 1) < H)
        beta_ref[...] = jnp.where(ok, _sigmoid(raw[:, :LANE]), 0.0)
        a = raw[:, LANE:] + dtb_ref[...]
        sp = jnp.maximum(a, 0.0) + jnp.log(1.0 + jnp.exp(-jnp.abs(a)))
        gv = jnp.where(ok, -jnp.exp(alog_ref[...]) * sp, 0.0)
        g_ref[...] = _dot(_chunk_tri(tm, lower=True), gv, HI)

    row = lambda i: (i, 0)
    fix = lambda i: (0, 0)
    out_shape = (S((L, 3 * W), f32), S((L, W), f32), S((L, 2 * LANE), f32),
                 S((L, W), f32), S((L, W), f32), S((L, W), f32), S((L, LANE), f32), S((L, LANE), f32))
    out_specs = (pl.BlockSpec((tm, 3 * W), row), pl.BlockSpec((tm, W), row), pl.BlockSpec((tm, 2 * LANE), row),
                 pl.BlockSpec((tm, W), row), pl.BlockSpec((tm, W), row), pl.BlockSpec((tm, W), row),
                 pl.BlockSpec((tm, LANE), row), pl.BlockSpec((tm, LANE), row))
    return pl.pallas_call(
        body, name="gdn_in_fwd", grid=(L // tm,), out_shape=out_shape,
        in_specs=[pl.BlockSpec((tm, D), row), ANY, pl.BlockSpec((KW, 3 * W), fix),
                  pl.BlockSpec((1, LANE), fix), pl.BlockSpec((1, LANE), fix)],
        out_specs=out_specs,
        scratch_shapes=[pltpu.VMEM((D, NW), w_full.dtype), pltpu.VMEM((tm + 8, W), f32), pltpu.VMEM((3, 8, W), f32),
                        pltpu.SemaphoreType.DMA((1,))],
        compiler_params=_cp(dimension_semantics=("arbitrary",)))(h, w_full, conv_w, alog, dtb)


def _gdn_in_bwd(dq, dk, dv, dz, dg, dbeta, pre, raw, conv_w, alog, dtb, *, first_row, H):
    L = dq.shape[0]
    W = H * DH
    KW = conv_w.shape[0]
    tm = TM
    nb = L // tm
    NW = 4 * W + 2 * LANE

    def body(dq_ref, dk_ref, dv_ref, dz_ref, dg_ref, dbeta_ref, pre_ref, hq_ref, hk_ref, hv_ref, raw_ref,
             cw_ref, alog_ref, dtb_ref, dproj_ref, dcw_ref, dal_ref, ddt_ref, scrx, scrd, carry, tmp):
        i = pl.program_id(0)
        blk = nb - 1 - i

        @pl.when(i == 0)
        def _():
            carry[...] = jnp.zeros_like(carry)
            dcw_ref[...] = jnp.zeros_like(dcw_ref)
            dal_ref[...] = jnp.zeros_like(dal_ref)
            ddt_ref[...] = jnp.zeros_like(ddt_ref)

        halos = (hq_ref, hk_ref, hv_ref)
        douts = (dq_ref, dk_ref, dv_ref)
        for s in range(3):
            sec = slice(s * W, (s + 1) * W)
            scrx[0:8, :] = jnp.where(blk > 0, halos[s][...], 0.0)
            scrx[8:8 + tm, :] = pre_ref[:, sec]
            c = jnp.zeros((tm, W), f32)
            for j in range(KW):
                c = c + cw_ref[j:j + 1, sec] * scrx[pl.ds(8 - (KW - 1) + j, tm), :]
            sig = _sigmoid(c)
            sl = c * sig
            if s < 2:
                scale = DH ** -0.5 if s == 0 else 1.0
                for hh in range(H):
                    hs = slice(hh * DH, (hh + 1) * DH)
                    seg = sl[:, hs]
                    r = lax.rsqrt(jnp.sum(seg * seg, axis=-1, keepdims=True) + L2_EPS)
                    n = seg * r
                    dqs = douts[s][:, hs]
                    tmp[:, hs] = (scale * r) * (dqs - n * jnp.sum(n * dqs, axis=-1, keepdims=True))
                dsl = tmp[...]
            else:
                dsl = dv_ref[...]
            dc = dsl * (sig * (1.0 + c * (1.0 - sig)))
            scrd[0:tm, :] = dc
            scrd[tm:tm + 8, :] = carry[s]
            carry[s] = dc[0:8, :]
            dx = jnp.zeros((tm, W), f32)
            for j in range(KW):
                dx = dx + cw_ref[j:j + 1, sec] * scrd[pl.ds(KW - 1 - j, tm), :]
                dcw_ref[j:j + 1, sec] += jnp.sum(dc * scrx[pl.ds(8 - (KW - 1) + j, tm), :], axis=0, keepdims=True)
            dproj_ref[:, sec] = _bf(dx)
        dproj_ref[:, 3 * W:4 * W] = _bf(dz_ref[...])
        raw_v = raw_ref[...]
        ok = (_row_ids(blk, tm, LANE) >= first_row) & (lax.broadcasted_iota(jnp.int32, (tm, LANE), 1) < H)
        beta = _sigmoid(raw_v[:, :LANE])
        dbraw = jnp.where(ok, dbeta_ref[...] * beta * (1.0 - beta), 0.0)
        a = raw_v[:, LANE:] + dtb_ref[...]
        sp = jnp.maximum(a, 0.0) + jnp.log(1.0 + jnp.exp(-jnp.abs(a)))
        nea = -jnp.exp(alog_ref[...])
        dgm = jnp.where(ok, _dot(_chunk_tri(tm, lower=False), dg_ref[...], HI), 0.0)
        daraw = dgm * nea * _sigmoid(a)
        dal_ref[0:1, :] += jnp.sum(dgm * nea * sp, axis=0, keepdims=True)
        ddt_ref[0:1, :] += jnp.sum(daraw, axis=0, keepdims=True)
        dproj_ref[:, 4 * W:4 * W + LANE] = _bf(dbraw)
        dproj_ref[:, 4 * W + LANE:4 * W + 2 * LANE] = _bf(daraw)

    rev = lambda i: (nb - 1 - i, 0)
    fix = lambda i: (0, 0)

    def halo(col):
        return pl.BlockSpec((8, W), lambda i: (jnp.maximum((nb - 1 - i) * (tm // 8) - 1, 0), col))

    return pl.pallas_call(
        body, name="gdn_in_bwd", grid=(nb,),
        out_shape=(S((L, NW), bf16), S((8, 3 * W), f32), S((8, LANE), f32), S((8, LANE), f32)),
        in_specs=[pl.BlockSpec((tm, W), rev)] * 4 + [pl.BlockSpec((tm, LANE), rev)] * 2
        + [pl.BlockSpec((tm, 3 * W), rev), halo(0), halo(1), halo(2), pl.BlockSpec((tm, 2 * LANE), rev),
           pl.BlockSpec((KW, 3 * W), fix), pl.BlockSpec((1, LANE), fix), pl.BlockSpec((1, LANE), fix)],
        out_specs=(pl.BlockSpec((tm, NW), rev), pl.BlockSpec((8, 3 * W), fix),
                   pl.BlockSpec((8, LANE), fix), pl.BlockSpec((8, LANE), fix)),
        scratch_shapes=[pltpu.VMEM((tm + 8, W), f32), pltpu.VMEM((tm + 8, W), f32), pltpu.VMEM((3, 8, W), f32),
                        pltpu.VMEM((tm, W), f32)],
        compiler_params=_cp(dimension_semantics=("arbitrary",)))(
            dq, dk, dv, dz, dg, dbeta, pre, pre, pre, pre, raw, conv_w, alog, dtb)


def _chunk_tri(n, lower):
    i = lax.broadcasted_iota(jnp.int32, (n, n), 0)
    j = lax.broadcasted_iota(jnp.int32, (n, n), 1)
    sh = int(math.log2(CH))
    same = lax.shift_right_logical(i, sh) == lax.shift_right_logical(j, sh)
    return (same & ((i >= j) if lower else (j >= i))).astype(f32)


def _tri_inv_many(ms, eye):
    ts = [eye - m for m in ms]
    ps = list(ms)
    for _ in range(int(math.log2(CH)) - 1):
        ps = [_dot(p, p, HI3) for p in ps]
        ts = [t + _dot(t, p, HI3) for t, p in zip(ts, ps)]
    return ts


def _chunk_local(q, k, v, gcol, grow, glast, bcol, ii, jj):
    dec = jnp.where(ii >= jj, jnp.exp(jnp.minimum(gcol - grow, 0.0)), 0.0)
    eg = jnp.exp(gcol)
    kb = k * bcol
    kbg = kb * eg
    vb = v * bcol
    qt = q * eg
    kt = k * jnp.exp(glast - gcol)
    kk = _dot_nt(_bf(kb), _bf(k))
    qk = _dot_nt(_bf(q), _bf(k))
    return dec, eg, kb, kbg, vb, qt, kt, kk, qk


def _delta_fwd(q, k, v, g, beta, z, nw, *, H):
    L = q.shape[0]
    W = H * DH
    rb = TM
    nc = rb // CH
    nblk = L // rb

    def body(q_ref, k_ref, v_ref, g_ref, b_ref, z_ref, nw_ref, o_ref, y_ref, s_out, t_out, s_scr):
        @pl.when(pl.program_id(0) == 0)
        def _():
            s_scr[...] = jnp.zeros_like(s_scr)

        ii = lax.broadcasted_iota(jnp.int32, (CH, CH), 0)
        jj = lax.broadcasted_iota(jnp.int32, (CH, CH), 1)
        eye = (ii == jj).astype(f32)
        nwv = nw_ref[...]

        heads = range(H)
        hsl = [slice(hh * DH, (hh + 1) * DH) for hh in heads]

        def chunk(c, carry):
            r0 = pl.multiple_of(c * CH, CH)
            rows = pl.ds(r0, CH)
            gam = g_ref[rows, :]
            gam_t = gam.T
            bb = b_ref[rows, :]
            glast = [gam[CH - 1:CH, hh:hh + 1] for hh in heads]
            loc = [_chunk_local(q_ref[rows, hsl[hh]], k_ref[rows, hsl[hh]], v_ref[rows, hsl[hh]],
                                gam[:, hh:hh + 1], gam_t[hh:hh + 1, :], glast[hh], bb[:, hh:hh + 1], ii, jj)
                   for hh in heads]
            st = [s_scr[hh] for hh in heads]
            zs = [z_ref[rows, hsl[hh]] for hh in heads]
            ts = _tri_inv_many([jnp.where(ii > jj, l[7] * l[0], 0.0) for l in loc], eye)
            us = [_dot(t, l[4], HI3) for t, l in zip(ts, loc)]
            ws = [_dot(t, l[3], HI3) for t, l in zip(ts, loc)]
            stb = [_bf(s) for s in st]
            vn = [u - _dot(_bf(w), sb) for u, w, sb in zip(us, ws, stb)]
            vnb = [_bf(x) for x in vn]
            snew = [s * jnp.exp(gl) + _dot_tn(_bf(l[6]), xb) for s, gl, l, xb in zip(st, glast, loc, vnb)]
            os_ = [_dot(_bf(l[5]), sb) + _dot(_bf(l[8] * l[0]), xb) for l, sb, xb in zip(loc, stb, vnb)]
            for hh in heads:
                o = os_[hh]
                s_out[c, hh] = st[hh]
                t_out[c, hh] = ts[hh]
                s_scr[hh] = snew[hh]
                o_ref[rows, hsl[hh]] = o
                on = o * lax.rsqrt(jnp.mean(o * o, axis=-1, keepdims=True) + RMS_EPS) * nwv
                y_ref[rows, hsl[hh]] = _bf(on * (zs[hh] * _sigmoid(zs[hh])))
            return carry

        lax.fori_loop(0, nc, chunk, 0)

    row = lambda i: (i, 0)
    fix = lambda i: (0, 0)
    return pl.pallas_call(
        body, name="delta_fwd", grid=(nblk,),
        out_shape=(S((L, W), f32), S((L, W), bf16), S((L // CH, H, DH, DH), f32), S((L // CH, H, CH, CH), f32)),
        in_specs=[pl.BlockSpec((rb, W), row)] * 3 + [pl.BlockSpec((rb, LANE), row)] * 2
        + [pl.BlockSpec((rb, W), row), pl.BlockSpec((1, DH), fix)],
        out_specs=(pl.BlockSpec((rb, W), row), pl.BlockSpec((rb, W), row),
                   pl.BlockSpec((nc, H, DH, DH), lambda i: (i, 0, 0, 0)),
                   pl.BlockSpec((nc, H, CH, CH), lambda i: (i, 0, 0, 0))),
        scratch_shapes=[pltpu.VMEM((H, DH, DH), f32)],
        compiler_params=_cp(dimension_semantics=("arbitrary",)))(q, k, v, g, beta, z, nw)


def _delta_bwd(dy, o, z, nw, q, k, v, g, beta, s_all, t_all, *, H):
    L = q.shape[0]
    W = H * DH
    rb = TM
    nc = rb // CH
    nblk = L // rb

    def body(dy_ref, o_ref, z_ref, nw_ref, q_ref, k_ref, v_ref, g_ref, b_ref, s_ref, t_ref,
             dq_ref, dk_ref, dv_ref, dz_ref, dg_ref, db_ref, dnw_ref, ds_scr):
        @pl.when(pl.program_id(0) == 0)
        def _():
            ds_scr[...] = jnp.zeros_like(ds_scr)
            dnw_ref[...] = jnp.zeros_like(dnw_ref)

        ii = lax.broadcasted_iota(jnp.int32, (CH, CH), 0)
        jj = lax.broadcasted_iota(jnp.int32, (CH, CH), 1)
        lane = lax.broadcasted_iota(jnp.int32, (CH, LANE), 1)
        last_row = lax.broadcasted_iota(jnp.int32, (CH, 1), 0) == CH - 1
        nwv = nw_ref[...]

        def chunk(cc, carry):
            c = nc - 1 - cc
            r0 = pl.multiple_of(c * CH, CH)
            rows = pl.ds(r0, CH)
            gam = g_ref[rows, :]
            gam_t = gam.T
            bb = b_ref[rows, :]
            def head(hh):
                hs = slice(hh * DH, (hh + 1) * DH)
                gcol, grow, glast = gam[:, hh:hh + 1], gam_t[hh:hh + 1, :], gam[CH - 1:CH, hh:hh + 1]
                bcol = bb[:, hh:hh + 1]
                qh, kh, vh = q_ref[rows, hs], k_ref[rows, hs], v_ref[rows, hs]
                oh, zh, dyh = o_ref[rows, hs], z_ref[rows, hs], dy_ref[rows, hs]
                t = t_ref[c, hh]
                st = s_ref[c, hh]
                dsn = ds_scr[hh]
                rms = lax.rsqrt(jnp.mean(oh * oh, axis=-1, keepdims=True) + RMS_EPS)
                on = oh * rms
                sig = _sigmoid(zh)
                sz = zh * sig
                dz = dyh * on * nwv * (sig * (1.0 + zh * (1.0 - sig)))
                dnw = jnp.sum(dyh * on * sz, axis=0, keepdims=True)
                don = dyh * nwv * sz
                do = rms * (don - on * jnp.mean(don * on, axis=-1, keepdims=True))
                dec, eg, kb, kbg, vb, qt, kt, kk, qk = _chunk_local(qh, kh, vh, gcol, grow, glast, bcol, ii, jj)
                stb, dsnb, dob, tb = _bf(st), _bf(dsn), _bf(do), _bf(t)
                u = _dot(t, vb, HI3)
                w = _dot(t, kbg, HI3)
                dqt = _dot_nt(dob, stb)
                ds_new = _dot_tn(_bf(qt), dob)
                yield
                mm = jnp.where(ii > jj, kk * dec, 0.0)
                attn = qk * dec
                wb = _bf(w)
                vn = u - _dot(wb, stb)
                dvn = _dot_tn(_bf(attn), dob) + _dot(_bf(kt), dsnb)
                egl = jnp.exp(glast)
                ekt = jnp.exp(glast - gcol)
                yield
                vnb, dvnb = _bf(vn), _bf(dvn)
                dattn = jnp.where(ii >= jj, _dot_nt(dob, vnb), 0.0)
                dkt = _dot_nt(vnb, dsnb)
                ds_new = ds_new + egl * dsn - _dot_tn(wb, dvnb)
                dw = -_dot_nt(dvnb, stb)
                dvb = _dot_tn(tb, dvnb)
                dt_u = _dot_nt(dvnb, _bf(vb))
                dglast = egl * jnp.sum(jnp.sum(dsn * st, axis=0, keepdims=True), axis=1, keepdims=True)
                yield
                dwb = _bf(dw)
                dkbg = _dot_tn(tb, dwb)
                dt = dt_u + _dot_nt(dwb, _bf(kbg))
                yield
                x = _dot_nt(_bf(dt), tb)
                yield
                dm = jnp.where(ii > jj, -_dot_tn(tb, _bf(x)), 0.0)
                dkk = dm * dec
                dqk = dattn * dec
                e = dm * mm + dattn * attn
                dgam = jnp.sum(e, axis=1, keepdims=True) - jnp.sum(e.T, axis=1, keepdims=True)
                dkkb, dqkb, kbf = _bf(dkk), _bf(dqk), _bf(kh)
                dkb = _dot(dkkb, kbf) + dkbg * eg
                dk = _dot_tn(dkkb, _bf(kb)) + _dot_tn(dqkb, _bf(qh)) + dkt * ekt + dkb * bcol
                dq = _dot(dqkb, kbf) + dqt * eg
                yield
                dktkt = dkt * kt
                dgam = dgam + jnp.sum(dqt * qt - dktkt + dkbg * kbg, axis=1, keepdims=True)
                dglast = dglast + jnp.sum(jnp.sum(dktkt, axis=0, keepdims=True), axis=1, keepdims=True)
                dgam = dgam + jnp.where(last_row, dglast, 0.0)
                dbeta = jnp.sum(dkb * kh + dvb * vh, axis=1, keepdims=True)
                return dq, dk, dvb * bcol, dz, dgam, dbeta, dnw, ds_new

            res = [None] * H
            for h0 in range(0, H, BWD_HEAD_GROUP):
                group = range(h0, min(h0 + BWD_HEAD_GROUP, H))
                gens = {hh: head(hh) for hh in group}
                while any(res[hh] is None for hh in group):
                    for hh in group:
                        try:
                            next(gens[hh])
                        except StopIteration as stop:
                            res[hh] = stop.value
            dgam_all = jnp.zeros((CH, LANE), f32)
            dbeta_all = jnp.zeros((CH, LANE), f32)
            dnw_acc = jnp.zeros((1, DH), f32)
            for hh in range(H):
                hs = slice(hh * DH, (hh + 1) * DH)
                dq, dk, dv_, dz, dgam, dbeta, dnw, ds_new = res[hh]
                dq_ref[rows, hs] = dq
                dk_ref[rows, hs] = dk
                dv_ref[rows, hs] = dv_
                dz_ref[rows, hs] = dz
                ds_scr[hh] = ds_new
                dgam_all = dgam_all + jnp.where(lane == hh, dgam, 0.0)
                dbeta_all = dbeta_all + jnp.where(lane == hh, dbeta, 0.0)
                dnw_acc = dnw_acc + dnw
            dg_ref[rows, :] = dgam_all
            db_ref[rows, :] = dbeta_all
            dnw_ref[0:1, :] += dnw_acc
            return carry

        lax.fori_loop(0, nc, chunk, 0)

    rev = lambda i: (nblk - 1 - i, 0)
    rev4 = lambda i: (nblk - 1 - i, 0, 0, 0)
    fix = lambda i: (0, 0)
    wide = pl.BlockSpec((rb, W), rev)
    thin = pl.BlockSpec((rb, LANE), rev)
    return pl.pallas_call(
        body, name="delta_bwd", grid=(nblk,),
        out_shape=(S((L, W), f32), S((L, W), f32), S((L, W), f32), S((L, W), f32),
                   S((L, LANE), f32), S((L, LANE), f32), S((8, DH), f32)),
        in_specs=[wide, wide, wide, pl.BlockSpec((1, DH), fix), wide, wide, wide, thin, thin,
                  pl.BlockSpec((nc, H, DH, DH), rev4), pl.BlockSpec((nc, H, CH, CH), rev4)],
        out_specs=(wide, wide, wide, wide, thin, thin, pl.BlockSpec((8, DH), fix)),
        scratch_shapes=[pltpu.VMEM((H, DH, DH), f32)],
        compiler_params=_cp(dimension_semantics=("arbitrary",)))(dy, o, z, nw, q, k, v, g, beta, s_all, t_all)


def _sc_fwd(h, w_in, conv_w, w_out, g, b, *, first_row):
    L, D = h.shape
    W = w_out.shape[0]
    KW = conv_w.shape[0]
    tm = TM
    alpha = ALPHA

    def body(h_ref, win_hbm, cw_ref, wout_hbm, g_ref, b_ref, proj_ref, bu_ref, pre_ref, out_ref,
             win, wout, scr, carry, sem):
        i = pl.program_id(0)
        _load_once([(win_hbm, win), (wout_hbm, wout)], sem)

        @pl.when(i == 0)
        def _():
            carry[...] = jnp.zeros_like(carry)

        hv = h_ref[...]
        hb = _bf(hv)
        bg = _dot(hb, win[:, 0:W])
        cg = _dot(hb, win[:, W:2 * W])
        xv = _dot(hb, win[:, 2 * W:3 * W])
        proj_ref[:, 0:W] = bg
        proj_ref[:, W:2 * W] = cg
        proj_ref[:, 2 * W:3 * W] = xv
        p = cg * xv
        scr[0:8, :] = carry[...]
        scr[8:8 + tm, :] = p
        carry[...] = p[tm - 8:tm, :]
        u = jnp.zeros((tm, W), f32)
        for j in range(KW):
            u = u + cw_ref[j:j + 1, :] * scr[pl.ds(8 - (KW - 1) + j, tm), :]
        bu = _bf(bg * u)
        bu_ref[...] = bu
        pre = alpha * hv + _dot(bu, wout[...])
        pre_ref[...] = pre
        out_ref[...] = _ln_fwd(pre, g_ref[...], b_ref[...], _row_ids(i, tm, D), first_row)

    row = lambda i: (i, 0)
    fix = lambda i: (0, 0)
    return pl.pallas_call(
        body, name="sc_fwd", grid=(L // tm,),
        out_shape=(S((L, 3 * W), f32), S((L, W), bf16), S((L, D), f32), S((L, D), f32)),
        in_specs=[pl.BlockSpec((tm, D), row), ANY, pl.BlockSpec((KW, W), fix), ANY,
                  pl.BlockSpec((1, D), fix), pl.BlockSpec((1, D), fix)],
        out_specs=(pl.BlockSpec((tm, 3 * W), row), pl.BlockSpec((tm, W), row),
                   pl.BlockSpec((tm, D), row), pl.BlockSpec((tm, D), row)),
        scratch_shapes=[pltpu.VMEM((D, 3 * W), w_in.dtype), pltpu.VMEM((W, D), w_out.dtype), pltpu.VMEM((tm + 8, W), f32),
                        pltpu.VMEM((8, W), f32), pltpu.SemaphoreType.DMA((2,))],
        compiler_params=_cp(dimension_semantics=("arbitrary",)))(h, w_in, conv_w, w_out, g, b)


def _sc_bwd(dbu, proj, conv_w):
    L, W = dbu.shape
    KW = conv_w.shape[0]
    tm = TM
    nb = L // tm

    def body(dbu_ref, proj_ref, hc_ref, hx_ref, cw_ref, dproj_ref, dcw_ref, scrx, scrd, carry):
        i = pl.program_id(0)
        blk = nb - 1 - i

        @pl.when(i == 0)
        def _():
            carry[...] = jnp.zeros_like(carry)
            dcw_ref[...] = jnp.zeros_like(dcw_ref)

        bg, cg, xv = proj_ref[:, 0:W], proj_ref[:, W:2 * W], proj_ref[:, 2 * W:3 * W]
        scrx[0:8, :] = jnp.where(blk > 0, hc_ref[...] * hx_ref[...], 0.0)
        scrx[8:8 + tm, :] = cg * xv
        u = jnp.zeros((tm, W), f32)
        for j in range(KW):
            u = u + cw_ref[j:j + 1, :] * scrx[pl.ds(8 - (KW - 1) + j, tm), :]
        d = dbu_ref[...]
        dproj_ref[:, 0:W] = _bf(d * u)
        du = d * bg
        scrd[0:tm, :] = du
        scrd[tm:tm + 8, :] = carry[...]
        carry[...] = du[0:8, :]
        dp = jnp.zeros((tm, W), f32)
        for j in range(KW):
            dp = dp + cw_ref[j:j + 1, :] * scrd[pl.ds(KW - 1 - j, tm), :]
            dcw_ref[j:j + 1, :] += jnp.sum(du * scrx[pl.ds(8 - (KW - 1) + j, tm), :], axis=0, keepdims=True)
        dproj_ref[:, W:2 * W] = _bf(dp * xv)
        dproj_ref[:, 2 * W:3 * W] = _bf(dp * cg)

    rev = lambda i: (nb - 1 - i, 0)
    fix = lambda i: (0, 0)

    def halo(col):
        return pl.BlockSpec((8, W), lambda i: (jnp.maximum((nb - 1 - i) * (tm // 8) - 1, 0), col))

    return pl.pallas_call(
        body, name="sc_bwd", grid=(nb,), out_shape=(S((L, 3 * W), bf16), S((8, W), f32)),
        in_specs=[pl.BlockSpec((tm, W), rev), pl.BlockSpec((tm, 3 * W), rev), halo(1), halo(2),
                  pl.BlockSpec((KW, W), fix)],
        out_specs=(pl.BlockSpec((tm, 3 * W), rev), pl.BlockSpec((8, W), fix)),
        scratch_shapes=[pltpu.VMEM((tm + 8, W), f32), pltpu.VMEM((tm + 8, W), f32), pltpu.VMEM((8, W), f32)],
        compiler_params=_cp(dimension_semantics=("arbitrary",)))(dbu, proj, proj, proj, conv_w)


def _ffn_cols(F):
    fc = F
    for cand in (1408, 1024, 512, 256, 128):
        if F % cand == 0:
            fc = cand
            break
    return fc


def _ffn_fwd(h, w_up, conv_w, w_down, g, b, *, first_row, name):
    L, D = h.shape
    F = w_down.shape[0]
    KW = conv_w.shape[0]
    tm = TM
    fc = _ffn_cols(F)
    alpha = ALPHA

    def body(h_ref, wup_hbm, cw_ref, wdn_hbm, g_ref, b_ref, up_ref, a_ref, pre_ref, out_ref,
             wup, wdn, scr, carry, sem):
        i = pl.program_id(0)
        _load_once([(wup_hbm, wup), (wdn_hbm, wdn)], sem)

        @pl.when(i == 0)
        def _():
            carry[...] = jnp.zeros_like(carry)

        hv = h_ref[...]
        hb = _bf(hv)
        pre = alpha * hv
        for c0 in range(0, F, fc):
            cs = slice(c0, c0 + fc)
            u = _dot(hb, wup[:, cs])
            gate = _dot(hb, wup[:, F + c0:F + c0 + fc])
            up_ref[:, cs] = u
            up_ref[:, F + c0:F + c0 + fc] = gate
            scr[0:8, :] = carry[:, cs]
            scr[8:8 + tm, :] = u
            carry[:, cs] = u[tm - 8:tm, :]
            uc = jnp.zeros((tm, fc), f32)
            for j in range(KW):
                uc = uc + cw_ref[j:j + 1, cs] * scr[pl.ds(8 - (KW - 1) + j, tm), :]
            ab = _bf(uc * _sigmoid(uc) * gate)
            a_ref[:, cs] = ab
            pre = pre + _dot(ab, wdn[cs, :])
        pre_ref[...] = pre
        out_ref[...] = _ln_fwd(pre, g_ref[...], b_ref[...], _row_ids(i, tm, D), first_row)

    row = lambda i: (i, 0)
    fix = lambda i: (0, 0)
    return pl.pallas_call(
        body, name=name, grid=(L // tm,),
        out_shape=(S((L, 2 * F), f32), S((L, F), bf16), S((L, D), f32), S((L, D), f32)),
        in_specs=[pl.BlockSpec((tm, D), row), ANY, pl.BlockSpec((KW, F), fix), ANY,
                  pl.BlockSpec((1, D), fix), pl.BlockSpec((1, D), fix)],
        out_specs=(pl.BlockSpec((tm, 2 * F), row), pl.BlockSpec((tm, F), row),
                   pl.BlockSpec((tm, D), row), pl.BlockSpec((tm, D), row)),
        scratch_shapes=[pltpu.VMEM((D, 2 * F), w_up.dtype), pltpu.VMEM((F, D), w_down.dtype), pltpu.VMEM((tm + 8, fc), f32),
                        pltpu.VMEM((8, F), f32), pltpu.SemaphoreType.DMA((2,))],
        compiler_params=_cp(dimension_semantics=("arbitrary",)))(h, w_up, conv_w, w_down, g, b)


def _ffn_bwd(dpre, up, w_down, conv_w, *, name):
    L, D = dpre.shape
    F = w_down.shape[0]
    KW = conv_w.shape[0]
    tm = TM
    nb = L // tm
    fc = _ffn_cols(F)

    def body(dpre_ref, up_ref, halo_ref, wdn_hbm, cw_ref, dup_ref, dcw_ref, wdn, scrx, scrd, carry, sem):
        i = pl.program_id(0)
        blk = nb - 1 - i
        _load_once([(wdn_hbm, wdn)], sem)

        @pl.when(i == 0)
        def _():
            carry[...] = jnp.zeros_like(carry)
            dcw_ref[...] = jnp.zeros_like(dcw_ref)

        db = _bf(dpre_ref[...])
        for c0 in range(0, F, fc):
            cs = slice(c0, c0 + fc)
            da = _dot_nt(db, wdn[cs, :])
            gate = up_ref[:, F + c0:F + c0 + fc]
            scrx[0:8, :] = jnp.where(blk > 0, halo_ref[:, cs], 0.0)
            scrx[8:8 + tm, :] = up_ref[:, cs]
            uc = jnp.zeros((tm, fc), f32)
            for j in range(KW):
                uc = uc + cw_ref[j:j + 1, cs] * scrx[pl.ds(8 - (KW - 1) + j, tm), :]
            sig = _sigmoid(uc)
            dup_ref[:, F + c0:F + c0 + fc] = _bf(da * (uc * sig))
            duc = da * gate * (sig * (1.0 + uc * (1.0 - sig)))
            scrd[0:tm, :] = duc
            scrd[tm:tm + 8, :] = carry[:, cs]
            carry[:, cs] = duc[0:8, :]
            du = jnp.zeros((tm, fc), f32)
            for j in range(KW):
                du = du + cw_ref[j:j + 1, cs] * scrd[pl.ds(KW - 1 - j, tm), :]
                dcw_ref[j:j + 1, cs] += jnp.sum(duc * scrx[pl.ds(8 - (KW - 1) + j, tm), :], axis=0, keepdims=True)
            dup_ref[:, cs] = _bf(du)

    rev = lambda i: (nb - 1 - i, 0)
    fix = lambda i: (0, 0)
    return pl.pallas_call(
        body, name=name, grid=(nb,), out_shape=(S((L, 2 * F), bf16), S((8, F), f32)),
        in_specs=[pl.BlockSpec((tm, D), rev), pl.BlockSpec((tm, 2 * F), rev),
                  pl.BlockSpec((8, F), lambda i: (jnp.maximum((nb - 1 - i) * (tm // 8) - 1, 0), 0)),
                  ANY, pl.BlockSpec((KW, F), fix)],
        out_specs=(pl.BlockSpec((tm, 2 * F), rev), pl.BlockSpec((8, F), fix)),
        scratch_shapes=[pltpu.VMEM((F, D), w_down.dtype), pltpu.VMEM((tm + 8, fc), f32), pltpu.VMEM((tm + 8, fc), f32),
                        pltpu.VMEM((8, F), f32), pltpu.SemaphoreType.DMA((1,))],
        compiler_params=_cp(dimension_semantics=("arbitrary",)))(dpre, up, up, w_down, conv_w)


def _loss_head(h, target):
    L, D = h.shape
    tm = TM
    pb = PADF // tm

    def body(h_ref, t_ref, dh_ref, loss_ref):
        i = pl.program_id(0)

        @pl.when(i == 0)
        def _():
            loss_ref[...] = jnp.zeros_like(loss_ref)

        valid = i >= pb
        err = h_ref[...] - t_ref[...]
        dh_ref[...] = jnp.where(valid, err * (1.0 / D), 0.0)
        part = 0.5 * jnp.sum(jnp.sum(err * err, axis=-1, keepdims=True) * (1.0 / D), axis=0, keepdims=True)
        loss_ref[...] += jnp.where(valid, part, 0.0)

    return pl.pallas_call(
        body, name="loss_head", grid=(L // tm,), out_shape=(S((L, D), f32), S((8, LANE), f32)),
        in_specs=[pl.BlockSpec((tm, D), lambda i: (i, 0)),
                  pl.BlockSpec((tm, D), lambda i: (jnp.maximum(i - pb, 0), 0))],
        out_specs=(pl.BlockSpec((tm, D), lambda i: (i, 0)), pl.BlockSpec((8, LANE), lambda i: (0, 0))),
        compiler_params=_cp(dimension_semantics=("arbitrary",)))(h, target)


def _ln_bwd(dout, pre, g, *, first_row, name):
    L, D = pre.shape
    tm = TM

    def body(do_ref, pre_ref, g_ref, dpre_ref, dg_ref, db_ref):
        i = pl.program_id(0)

        @pl.when(i == 0)
        def _():
            dg_ref[...] = jnp.zeros_like(dg_ref)
            db_ref[...] = jnp.zeros_like(db_ref)

        pre_v = pre_ref[...]
        mu = jnp.mean(pre_v, axis=-1, keepdims=True)
        xc = pre_v - mu
        rstd = lax.rsqrt(jnp.mean(xc * xc, axis=-1, keepdims=True) + LN_EPS)
        xh = xc * rstd
        dy = jnp.where(_row_ids(i, tm, D) >= first_row, do_ref[...], 0.0)
        dg_ref[0:1, :] += jnp.sum(dy * xh, axis=0, keepdims=True)
        db_ref[0:1, :] += jnp.sum(dy, axis=0, keepdims=True)
        dxh = dy * g_ref[...]
        dpre_ref[...] = rstd * (dxh - jnp.mean(dxh, axis=-1, keepdims=True)
                                - xh * jnp.mean(dxh * xh, axis=-1, keepdims=True))

    row = lambda i: (i, 0)
    fix = lambda i: (0, 0)
    return pl.pallas_call(
        body, name=name, grid=(L // tm,), out_shape=(S((L, D), f32), S((8, D), f32), S((8, D), f32)),
        in_specs=[pl.BlockSpec((tm, D), row), pl.BlockSpec((tm, D), row), pl.BlockSpec((1, D), fix)],
        out_specs=(pl.BlockSpec((tm, D), row), pl.BlockSpec((8, D), fix), pl.BlockSpec((8, D), fix)),
        compiler_params=_cp(dimension_semantics=("arbitrary",)))(dout, pre, g)


def _adamw(g_terms, w, m, v, *, name):
    R, C = w.shape
    tr = _row_tile(R)
    n = len(g_terms)
    c1 = 1.0 - ADAM_B1 ** ADAM_STEP
    c2 = 1.0 - ADAM_B2 ** ADAM_STEP

    def body(*refs):
        g = refs[0][...].astype(f32)
        for r in refs[1:n]:
            g = g + r[...].astype(f32)
        w_ref, m_ref, v_ref, g_out, d_out, m_out, v_out = refs[n:]
        mn = ADAM_B1 * m_ref[...] + (1.0 - ADAM_B1) * g
        vn = ADAM_B2 * v_ref[...] + (1.0 - ADAM_B2) * (g * g)
        g_out[...] = g
        m_out[...] = mn
        v_out[...] = vn
        d_out[...] = -ADAM_LR * ((mn / c1) / (jnp.sqrt(vn / c2) + ADAM_EPS) + ADAM_WD * w_ref[...])

    spec = pl.BlockSpec((tr, C), lambda i: (i, 0))
    return pl.pallas_call(
        body, name=name, grid=(R // tr,), out_shape=(S((R, C), f32),) * 4,
        in_specs=[spec] * (n + 3), out_specs=(spec,) * 4,
        compiler_params=_cp(dimension_semantics=("arbitrary",)))(*g_terms, w, m, v)


def _sum_devices(x):
    n, R, C = x.shape

    def body(x_ref, o_ref):
        acc = x_ref[0]
        for d in range(1, n):
            acc = acc + x_ref[d]
        o_ref[...] = acc

    return pl.pallas_call(body, name="sum_devices", out_shape=S((R, C), f32), compiler_params=_cp())(x)


def _row_tile(R):
    for step in (16, 8):
        for t in range(256, 0, -step):
            if R % t == 0:
                return t
    return R


def _pair_add(keep, recv, slots, *, out_dtype, name):
    _, R, C = keep.shape
    n = slots.shape[0]
    tr = _row_tile(R)

    def body(slots_ref, k_ref, r_ref, o_ref):
        o_ref[0] = (k_ref[0] + r_ref[0].astype(f32)).astype(out_dtype)

    grid_spec = pltpu.PrefetchScalarGridSpec(
        num_scalar_prefetch=1, grid=(n, R // tr),
        in_specs=[pl.BlockSpec((1, tr, C), lambda j, i, sl: (sl[j], i, 0)),
                  pl.BlockSpec((1, tr, C), lambda j, i, sl: (sl[j], i, 0))],
        out_specs=pl.BlockSpec((1, tr, C), lambda j, i, sl: (j, i, 0)))
    return pl.pallas_call(
        body, name=name, grid_spec=grid_spec, out_shape=S((n, R, C), out_dtype),
        compiler_params=_cp(dimension_semantics=("arbitrary", "arbitrary")))(slots, keep, recv)


def _peer(rel):
    x, y, c = lax.axis_index("x"), lax.axis_index("y"), lax.axis_index("c")
    return {"c": (x, y, 1 - c), "x": (1 - x, y, c), "y": (x, 1 - y, c), "xy": (1 - x, 1 - y, c)}[rel]


def _all_gather(x, *, name):
    R, C = x.shape

    def body(x_ref, out_ref, send_sems, recv_sems, local_sem):
        mx, my, mc = lax.axis_index("x"), lax.axis_index("y"), lax.axis_index("c")
        me, sibling = (mx, my, mc), (mx, my, 1 - mc)
        chips = [(1 - mx, my), (mx, 1 - my), (1 - mx, 1 - my)]

        def rows(px, py, pc):
            return out_ref.at[4 * px + 2 * py + pc]

        def copy(kk, block, to, src=None):
            return pltpu.make_async_remote_copy(
                src_ref=rows(*block) if src is None else src, dst_ref=rows(*block),
                send_sem=send_sems.at[kk], recv_sem=recv_sems.at[kk], device_id=to, device_id_type=MESH)

        mine = pltpu.make_async_copy(x_ref, rows(*me), local_sem)
        mine.start()
        first = [copy(0, me, sibling, src=x_ref)]
        first += [copy(1 + j, me, (*chip, mc), src=x_ref) for j, chip in enumerate(chips)]
        for cp in first:
            cp.start()
        passed = [copy(4 + j, (*chip, mc), sibling) for j, chip in enumerate(chips)]
        for j, chip in enumerate(chips):
            copy(1 + j, (*chip, mc), me).wait_recv()
            passed[j].start()
        copy(0, sibling, me).wait_recv()
        for j, chip in enumerate(chips):
            copy(4 + j, (*chip, 1 - mc), me).wait_recv()
        for cp in first + passed:
            cp.wait_send()
        mine.wait()

    return pl.pallas_call(
        body, name=name, out_shape=S((N_DEV, R, C), x.dtype), in_specs=[ANY], out_specs=ANY,
        scratch_shapes=[pltpu.SemaphoreType.DMA((7,)), pltpu.SemaphoreType.DMA((7,)), pltpu.SemaphoreType.DMA],
        compiler_params=pltpu.CompilerParams(has_side_effects=True))(x)


def _exchange(send, rels, *, name):
    n = send.shape[0]

    def body(send_ref, recv_ref, send_sems, recv_sems):
        cps = [pltpu.make_async_remote_copy(
            src_ref=send_ref.at[j], dst_ref=recv_ref.at[j], send_sem=send_sems.at[j], recv_sem=recv_sems.at[j],
            device_id=_peer(rels[j]), device_id_type=MESH) for j in range(n)]
        for cp in cps:
            cp.start()
        for cp in cps:
            cp.wait()

    return pl.pallas_call(
        body, name=name, out_shape=S(send.shape, send.dtype), in_specs=[ANY], out_specs=ANY,
        scratch_shapes=[pltpu.SemaphoreType.DMA((n,)), pltpu.SemaphoreType.DMA((n,))],
        compiler_params=pltpu.CompilerParams(has_side_effects=True))(send)


def _rows16(n):
    return -(-n // 16) * 16


class _Big:
    def __init__(self, name, shard_shape, col_sharded, width):
        self.name, self.shard_shape, self.col_sharded, self.width = name, tuple(shard_shape), col_sharded, width
        self.size = math.prod(shard_shape)
        assert self.size % width == 0
        self.rows = self.size // width
        self.prows = _rows16(self.rows)

    def to_flat(self, a, dtype):
        a = a.astype(dtype).reshape(self.rows, self.width)
        return jnp.pad(a, ((0, self.prows - self.rows), (0, 0))) if self.prows != self.rows else a

    def from_flat(self, a):
        return a[:self.rows].reshape(self.shard_shape)

    def full_from_shards(self, a):
        sh = a[:, :self.rows].reshape((N_DEV,) + self.shard_shape)
        nd = len(self.shard_shape)
        if self.col_sharded:
            perm = tuple(range(1, nd)) + (0, nd)
            full = jnp.transpose(sh, perm)
            return full.reshape(self.shard_shape[:-1] + (N_DEV * self.shard_shape[-1],))
        perm = tuple(range(1, nd - 1)) + (0, nd - 1, nd)
        full = jnp.transpose(sh, perm)
        return full.reshape(self.shard_shape[:-2] + (N_DEV * self.shard_shape[-2], self.shard_shape[-1]))

    def shards_from_full(self, full):
        nd = len(self.shard_shape)
        if self.col_sharded:
            sh = full.reshape(self.shard_shape[:-1] + (N_DEV, self.shard_shape[-1]))
            sh = jnp.transpose(sh, (nd - 1,) + tuple(range(nd - 1)) + (nd,))
        else:
            sh = full.reshape(self.shard_shape[:-2] + (N_DEV,) + self.shard_shape[-2:])
            sh = jnp.transpose(sh, (nd - 2,) + tuple(range(nd - 2)) + (nd - 1, nd))
        sh = sh.reshape(N_DEV, self.rows, self.width)
        return jnp.pad(sh, ((0, 0), (0, self.prows - self.rows), (0, 0))) if self.prows != self.rows else sh


def _pack_small(parts, width):
    rows, offs, r = [], [], 0
    for a in parts:
        n = a.size
        nr = -(-n // width)
        flat = a.reshape(-1).astype(f32)
        if nr * width != n:
            flat = jnp.pad(flat, (0, nr * width - n))
        rows.append(flat.reshape(nr, width))
        offs.append((r, nr))
        r += nr
    buf = jnp.concatenate(rows, axis=0)
    pad = (-r) % 8
    if pad:
        buf = jnp.pad(buf, ((0, pad), (0, 0)))
    return buf, offs


def _unpack_small(buf, off, shape):
    r, nr = off
    return buf[r:r + nr].reshape(-1)[:math.prod(shape)].reshape(shape)


def _local_step(x, target, meta, wts, small):
    SEQ, D = x.shape
    n_meta = meta.shape[0]
    first_row = PADF - n_meta
    H = small["a_log"].shape[-1]
    W = H * DH
    F = wts["ffn_w_down"].shape[1]

    h0 = jnp.concatenate([jnp.zeros((first_row, D), f32), meta, x], axis=0)

    def lanes(a):
        return jnp.pad(a.reshape(1, -1), ((0, 0), (0, LANE - a.size)))

    alog, dtb = lanes(small["a_log"][0]), lanes(small["a_dt_bias"][0])
    a_conv, b_conv = small["a_conv"][0], small["b_conv"][0]
    nw = small["a_norm"][0].reshape(1, DH)
    lmg, lmb, lfg, lfb = small["ln_mix_g"], small["ln_mix_b"], small["ln_ffn_g"], small["ln_ffn_b"]

    pre_a, z, raw, q, k, v, beta, g = _gdn_in_fwd(h0, wts["a_w_in"], a_conv, alog, dtb, first_row=first_row, H=H)
    o, y, s_all, t_all = _delta_fwd(q, k, v, g, beta, z, nw, H=H)
    pre1, h1 = _out_res_ln(y, wts["a_w_out"], h0, lmg[0:1], lmb[0:1], first_row=first_row, name="gdn_out_ln")
    up0, act0, pre2, h2 = _ffn_fwd(h1, wts["ffn_w_up"][0], small["ffn_conv"][0], wts["ffn_w_down"][0],
                                   lfg[0:1], lfb[0:1], first_row=first_row, name="ffn_fwd0")
    proj_b, bu, pre3, h3 = _sc_fwd(h2, wts["b_w_in"], b_conv, wts["b_w_out"], lmg[1:2], lmb[1:2], first_row=first_row)
    up1, act1, pre4, h4 = _ffn_fwd(h3, wts["ffn_w_up"][1], small["ffn_conv"][1], wts["ffn_w_down"][1],
                                   lfg[1:2], lfb[1:2], first_row=first_row, name="ffn_fwd1")
    dh4, loss_tile = _loss_head(h4, target)

    gw, gs = {}, {}
    alpha = ALPHA

    def ffn_backward(dh_out, pre, up, act, h_in, layer, tag):
        dpre, dg, db = _ln_bwd(dh_out, pre, lfg[layer:layer + 1], first_row=first_row, name="ln_bwd_ffn" + tag)
        dup, dcw = _ffn_bwd(dpre, up, wts["ffn_w_down"][layer], small["ffn_conv"][layer], name="ffn_bwd" + tag)
        dwd = _linear_dw(act, dpre, name="dw_down" + tag)
        dwu = _linear_dw(h_in, dup, name="dw_up" + tag)
        dh_in = _linear_dx(dup, wts["ffn_w_up"][layer], dpre, alpha=alpha, out_dtype=f32, name="dx_up" + tag)
        return dh_in, dwu, dwd, dcw[0:3], dg[0], db[0]

    dh3, dwu1, dwd1, dcf1, dlfg1, dlfb1 = ffn_backward(dh4, pre4, up1, act1, h3, 1, "1")

    dpre3, dlmg1, dlmb1 = _ln_bwd(dh3, pre3, lmg[1:2], first_row=first_row, name="ln_bwd_mix1")
    dbu = _linear_dx(dpre3, wts["b_w_out"], None, alpha=0.0, out_dtype=f32, name="dx_b_out")
    gw["b_w_out"] = _linear_dw(bu, dpre3, name="dw_b_out")[None]
    dproj_b, dcb = _sc_bwd(dbu, proj_b, b_conv)
    gw["b_w_in"] = _linear_dw(h2, dproj_b, name="dw_b_in")[None]
    dh2 = _linear_dx(dproj_b, wts["b_w_in"], dpre3, alpha=alpha, out_dtype=f32, name="dx_b_in")

    dh1, dwu0, dwd0, dcf0, dlfg0, dlfb0 = ffn_backward(dh2, pre2, up0, act0, h1, 0, "0")

    dpre1, dlmg0, dlmb0 = _ln_bwd(dh1, pre1, lmg[0:1], first_row=first_row, name="ln_bwd_mix0")
    dy = _linear_dx(dpre1, wts["a_w_out"], None, alpha=0.0, out_dtype=f32, name="dx_a_out")
    gw["a_w_out"] = _linear_dw(y, dpre1, name="dw_a_out")[None]
    dq, dk, dv, dz, dg_, dbeta, dnw = _delta_bwd(dy, o, z, nw, q, k, v, g, beta, s_all, t_all, H=H)
    dproj_a, dca, dal, ddt = _gdn_in_bwd(dq, dk, dv, dz, dg_, dbeta, pre_a, raw, a_conv, alog, dtb,
                                         first_row=first_row, H=H)
    dwa = _linear_dw(h0, dproj_a, name="dw_a_in")
    gw["a_w_in"] = jnp.concatenate([dwa[:, :4 * W], dwa[:, 4 * W:4 * W + H], dwa[:, 4 * W + LANE:4 * W + LANE + H]],
                                   axis=1)[None]
    dh0 = _linear_dx(dproj_a, wts["a_w_in"], dpre1, alpha=alpha, out_dtype=f32, name="dx_a_in")

    gw["ffn_w_up"] = jnp.stack([dwu0, dwu1])
    gw["ffn_w_down"] = jnp.stack([dwd0, dwd1])
    gs["meta"] = dh0[first_row:PADF]
    gs["a_conv"] = dca[0:a_conv.shape[0]][None]
    gs["a_log"] = dal[0:1, 0:H]
    gs["a_dt_bias"] = ddt[0:1, 0:H]
    gs["a_norm"] = dnw[0:1]
    gs["b_conv"] = dcb[0:b_conv.shape[0]][None]
    gs["ln_mix_g"] = jnp.stack([dlmg0[0], dlmg1[0]])
    gs["ln_mix_b"] = jnp.stack([dlmb0[0], dlmb1[0]])
    gs["ffn_conv"] = jnp.stack([dcf0, dcf1])
    gs["ln_ffn_g"] = jnp.stack([dlfg0, dlfg1])
    gs["ln_ffn_b"] = jnp.stack([dlfb0, dlfb1])
    return loss_tile, dh0[PADF:], gw, gs


_BIG = ("a_w_in", "a_w_out", "b_w_in", "b_w_out", "ffn_w_up", "ffn_w_down")
_BIG_COL = {"a_w_in": True, "a_w_out": False, "b_w_in": True, "b_w_out": False, "ffn_w_up": True, "ffn_w_down": False}
_SMALL = ("meta", "a_conv", "a_log", "a_dt_bias", "a_norm", "b_conv", "ln_mix_g", "ln_mix_b",
          "ffn_conv", "ln_ffn_g", "ln_ffn_b")
_SMALL_SHARDED = ("meta", "a_conv", "b_conv", "ffn_conv")
_ORDER = ("meta", "a_w_in", "a_conv", "a_log", "a_dt_bias", "a_norm", "a_w_out", "b_w_in", "b_conv", "b_w_out",
          "ln_mix_g", "ln_mix_b", "ffn_w_up", "ffn_conv", "ffn_w_down", "ln_ffn_g", "ln_ffn_b")


def _expand_a_w_in(w, H):
    W = H * DH
    z = jnp.zeros((w.shape[0], LANE - H), w.dtype)
    return jnp.concatenate([w[:, :4 * W], w[:, 4 * W:4 * W + H], z, w[:, 4 * W + H:], z], axis=1)


def kernel(x, meta, a_w_in, a_conv, a_log, a_dt_bias, a_norm, a_w_out, b_w_in, b_conv, b_w_out, ln_mix_g, ln_mix_b, ffn_w_up, ffn_conv, ffn_w_down, ln_ffn_g, ln_ffn_b, loss_target, m_meta, m_a_w_in, m_a_conv, m_a_log, m_a_dt_bias, m_a_norm, m_a_w_out, m_b_w_in, m_b_conv, m_b_w_out, m_ln_mix_g, m_ln_mix_b, m_ffn_w_up, m_ffn_conv, m_ffn_w_down, m_ln_ffn_g, m_ln_ffn_b, v_meta, v_a_w_in, v_a_conv, v_a_log, v_a_dt_bias, v_a_norm, v_a_w_out, v_b_w_in, v_b_conv, v_b_w_out, v_ln_mix_g, v_ln_mix_b, v_ffn_w_up, v_ffn_conv, v_ffn_w_down, v_ln_ffn_g, v_ln_ffn_b):
    wloc = dict(meta=meta, a_w_in=a_w_in, a_conv=a_conv, a_log=a_log, a_dt_bias=a_dt_bias, a_norm=a_norm,
                a_w_out=a_w_out, b_w_in=b_w_in, b_conv=b_conv, b_w_out=b_w_out, ln_mix_g=ln_mix_g, ln_mix_b=ln_mix_b,
                ffn_w_up=ffn_w_up, ffn_conv=ffn_conv, ffn_w_down=ffn_w_down, ln_ffn_g=ln_ffn_g, ln_ffn_b=ln_ffn_b)
    mloc = dict(meta=m_meta, a_w_in=m_a_w_in, a_conv=m_a_conv, a_log=m_a_log, a_dt_bias=m_a_dt_bias, a_norm=m_a_norm,
                a_w_out=m_a_w_out, b_w_in=m_b_w_in, b_conv=m_b_conv, b_w_out=m_b_w_out, ln_mix_g=m_ln_mix_g,
                ln_mix_b=m_ln_mix_b, ffn_w_up=m_ffn_w_up, ffn_conv=m_ffn_conv, ffn_w_down=m_ffn_w_down,
                ln_ffn_g=m_ln_ffn_g, ln_ffn_b=m_ln_ffn_b)
    vloc = dict(meta=v_meta, a_w_in=v_a_w_in, a_conv=v_a_conv, a_log=v_a_log, a_dt_bias=v_a_dt_bias, a_norm=v_a_norm,
                a_w_out=v_a_w_out, b_w_in=v_b_w_in, b_conv=v_b_conv, b_w_out=v_b_w_out, ln_mix_g=v_ln_mix_g,
                ln_mix_b=v_ln_mix_b, ffn_w_up=v_ffn_w_up, ffn_conv=v_ffn_conv, ffn_w_down=v_ffn_w_down,
                ln_ffn_g=v_ln_ffn_g, ln_ffn_b=v_ln_ffn_b)
    H = a_log.shape[-1]
    mx, my, mc = lax.axis_index("x"), lax.axis_index("y"), lax.axis_index("c")
    me = 4 * mx + 2 * my + mc

    D = x.shape[-1]
    big = [_Big(n, wloc[n].shape, _BIG_COL[n], D) for n in _BIG]
    offs, r = {}, 0
    for bgw in big:
        offs[bgw.name] = r
        r += bgw.prows
    R = -(-r // 256) * 256

    def flat_cat(src, dtype):
        parts = [bgw.to_flat(src[bgw.name], dtype) for bgw in big]
        if R != r:
            parts.append(jnp.zeros((R - r, D), dtype))
        return jnp.concatenate(parts, axis=0)

    gathered = _all_gather(flat_cat(wloc, bf16), name="gather_weights")
    wts = {bgw.name: bgw.full_from_shards(gathered[:, offs[bgw.name]:offs[bgw.name] + bgw.prows]) for bgw in big}
    wts["a_w_in"] = _expand_a_w_in(wts["a_w_in"][0], H)
    wts["a_w_out"], wts["b_w_in"], wts["b_w_out"] = wts["a_w_out"][0], wts["b_w_in"][0], wts["b_w_out"][0]

    sm_sh = [wloc[n] for n in _SMALL_SHARDED]
    sbuf, soffs = _pack_small(sm_sh, 128)
    sg = _all_gather(sbuf, name="gather_small")
    small = {n: wloc[n] for n in _SMALL}
    for n, off in zip(_SMALL_SHARDED, soffs):
        sh = wloc[n].shape
        parts = jnp.stack([_unpack_small(sg[d], off, sh) for d in range(N_DEV)])
        nd = len(sh)
        small[n] = jnp.transpose(parts, tuple(range(1, nd)) + (0, nd)).reshape(sh[:-1] + (N_DEV * sh[-1],))

    loss_tile, grad_x, gw, gs = _local_step(x[0], loss_target[0], small["meta"], wts, small)

    gparts = [bgw.shards_from_full(gw[bgw.name]) for bgw in big]
    if R != r:
        gparts.append(jnp.zeros((N_DEV, R - r, D), f32))
    gchip = jnp.concatenate(gparts, axis=1).reshape(4, 2, R, D)
    keep = lax.dynamic_index_in_dim(gchip, mc, axis=1, keepdims=False)
    send = _bf(lax.dynamic_index_in_dim(gchip, 1 - mc, axis=1, keepdims=False))
    recv = _exchange(send, ["c"] * 4, name="grad_to_sibling")
    slot_mine = jnp.stack([2 * mx + my]).astype(jnp.int32)
    slot_others = jnp.stack([2 * (1 - mx) + my, 2 * mx + (1 - my), 2 * (1 - mx) + (1 - my)]).astype(jnp.int32)
    mine = _pair_add(keep, recv, slot_mine, out_dtype=f32, name="pair_add_mine")[0]
    to_send = _pair_add(keep, recv, slot_others, out_dtype=bf16, name="pair_add_send")
    recv2 = _exchange(to_send, ["x", "y", "xy"], name="grad_to_chips")
    g_b, d_b, m_b, v_b = _adamw([mine, recv2[0], recv2[1], recv2[2]], flat_cat(wloc, f32), flat_cat(mloc, f32),
                                flat_cat(vloc, f32), name="adamw_big")

    names = list(_SMALL)
    pbuf, poffs = _pack_small([gs[n] for n in names] + [loss_tile[0:1, 0:1]], 1024)
    psum = _sum_devices(_all_gather(pbuf, name="gather_small_grads"))
    loss = psum[poffs[-1][0], 0]
    g_small = {}
    for n, off in zip(names, poffs[:-1]):
        full_shape = gs[n].shape
        gfull = _unpack_small(psum, off, full_shape)
        if n in _SMALL_SHARDED:
            ns = wloc[n].shape[-1]
            gfull = lax.dynamic_slice_in_dim(gfull, me * ns, ns, axis=gfull.ndim - 1)
        g_small[n] = gfull.reshape(wloc[n].shape)
    gbuf, aoffs = _pack_small([g_small[n] for n in names], 128)
    wbuf, _ = _pack_small([wloc[n] for n in names], 128)
    mbuf, _ = _pack_small([mloc[n] for n in names], 128)
    vbuf, _ = _pack_small([vloc[n] for n in names], 128)
    _, d_s, m_s, v_s = _adamw([gbuf], wbuf, mbuf, vbuf, name="adamw_small")

    grads, deltas, new_m, new_v = {}, {}, {}, {}
    for bgw in big:
        o0 = offs[bgw.name]
        sl = slice(o0, o0 + bgw.prows)
        grads[bgw.name], deltas[bgw.name] = bgw.from_flat(g_b[sl]), bgw.from_flat(d_b[sl])
        new_m[bgw.name], new_v[bgw.name] = bgw.from_flat(m_b[sl]), bgw.from_flat(v_b[sl])
    for n, off in zip(names, aoffs):
        sh = wloc[n].shape
        grads[n] = g_small[n]
        deltas[n], new_m[n], new_v[n] = (_unpack_small(b_, off, sh) for b_ in (d_s, m_s, v_s))
    return (loss, grad_x[None], *[grads[n] for n in _ORDER], *[deltas[n] for n in _ORDER],
            *[new_m[n] for n in _ORDER], *[new_v[n] for n in _ORDER])
```

```python
import math

import jax
import jax.numpy as jnp
from jax import lax
from jax.experimental import pallas as pl
from jax.experimental.pallas import tpu as pltpu

f32, bf16 = jnp.float32, jnp.bfloat16
S = jax.ShapeDtypeStruct
HI = lax.Precision.HIGHEST
HI3 = lax.Precision.HIGH
MESH = pl.DeviceIdType.MESH

V7X_VMEM_LIMIT = 56 * 1024 * 1024
LANE = 128
DH = 128
CH = 64
PADF = 256
TM = 256
TMM = 768
N_DEV = 8
BWD_HEAD_GROUP = 8

DEPTH = 2
ALPHA = (2.0 * DEPTH) ** 0.25
LN_EPS = 1e-5
RMS_EPS = 1e-6
L2_EPS = 1e-6
ADAM_LR, ADAM_B1, ADAM_B2, ADAM_EPS, ADAM_WD, ADAM_STEP = 0.001, 0.9, 0.999, 1e-08, 0.01, 10


def _cp(**kw):
    return pltpu.CompilerParams(vmem_limit_bytes=V7X_VMEM_LIMIT, **kw)


def _bf(x):
    return x.astype(bf16)


def _dot(a, b, precision=None):
    return jnp.dot(a, b, preferred_element_type=f32, precision=precision)


def _dot_nt(a, b):
    return lax.dot_general(a, b, (((1,), (1,)), ((), ())), preferred_element_type=f32)


def _dot_tn(a, b):
    return lax.dot_general(a, b, (((0,), (0,)), ((), ())), preferred_element_type=f32)


def _sigmoid(x):
    return 1.0 / (1.0 + jnp.exp(-x))


def _load_once(pairs, sem):
    @pl.when(pl.program_id(0) == 0)
    def _():
        cps = [pltpu.make_async_copy(src, dst, sem.at[n]) for n, (src, dst) in enumerate(pairs)]
        for c in cps:
            c.start()
        for c in cps:
            c.wait()


def _row_ids(i, tm, width):
    return i * tm + lax.broadcasted_iota(jnp.int32, (tm, width), 0)


def _ln_fwd(pre, g, b, rows, first_row):
    mu = jnp.mean(pre, axis=-1, keepdims=True)
    xc = pre - mu
    var = jnp.mean(xc * xc, axis=-1, keepdims=True)
    y = xc * lax.rsqrt(var + LN_EPS) * g + b
    return jnp.where(rows >= first_row, y, 0.0)


ANY = pl.BlockSpec(memory_space=pl.ANY)


def _linear_dw(x, dy, *, name):
    L, K = x.shape
    N = dy.shape[1]
    tm = TMM if L % TMM == 0 else TM
    tn = LANE
    for d in range(N // LANE, 0, -1):
        if (N // LANE) % d == 0 and K * d * LANE * 4 <= 9 * 1024 * 1024:
            tn = d * LANE
            break

    def body(x_ref, dy_ref, o_ref):
        @pl.when(pl.program_id(1) == 0)
        def _():
            o_ref[...] = jnp.zeros_like(o_ref)
        o_ref[...] += _dot_tn(_bf(x_ref[...]), _bf(dy_ref[...]))

    return pl.pallas_call(
        body, name=name, grid=(N // tn, L // tm), out_shape=S((K, N), f32),
        in_specs=[pl.BlockSpec((tm, K), lambda j, i: (i, 0)), pl.BlockSpec((tm, tn), lambda j, i: (i, j))],
        out_specs=pl.BlockSpec((K, tn), lambda j, i: (0, j)),
        compiler_params=_cp(dimension_semantics=("arbitrary", "arbitrary")))(x, dy)


def _linear_dx(dy, w, res, *, alpha, out_dtype, name):
    L, N = dy.shape
    K = w.shape[0]
    tm = TM
    has_res = res is not None

    def body(*refs):
        if has_res:
            dy_ref, w_hbm, res_ref, o_ref, w_vmem, sem = refs
        else:
            dy_ref, w_hbm, o_ref, w_vmem, sem = refs
        _load_once([(w_hbm, w_vmem)], sem)
        acc = _dot_nt(_bf(dy_ref[...]), w_vmem[...])
        if has_res:
            acc = acc + alpha * res_ref[...]
        o_ref[...] = acc.astype(out_dtype)

    in_specs = [pl.BlockSpec((tm, N), lambda i: (i, 0)), ANY]
    args = [dy, w]
    if has_res:
        in_specs.append(pl.BlockSpec((tm, K), lambda i: (i, 0)))
        args.append(res)
    return pl.pallas_call(
        body, name=name, grid=(L // tm,), out_shape=S((L, K), out_dtype),
        in_specs=in_specs, out_specs=pl.BlockSpec((tm, K), lambda i: (i, 0)),
        scratch_shapes=[pltpu.VMEM((K, N), w.dtype), pltpu.SemaphoreType.DMA((1,))],
        compiler_params=_cp(dimension_semantics=("arbitrary",)))(*args)


def _out_res_ln(y, w, h, g, b, *, first_row, name):
    L, K = y.shape
    D = w.shape[1]
    tm = TM
    alpha = ALPHA

    def body(y_ref, w_hbm, h_ref, g_ref, b_ref, pre_ref, out_ref, w_vmem, sem):
        _load_once([(w_hbm, w_vmem)], sem)
        pre = alpha * h_ref[...] + _dot(_bf(y_ref[...]), w_vmem[...])
        pre_ref[...] = pre
        out_ref[...] = _ln_fwd(pre, g_ref[...], b_ref[...], _row_ids(pl.program_id(0), tm, D), first_row)

    row = lambda i: (i, 0)
    fix = lambda i: (0, 0)
    return pl.pallas_call(
        body, name=name, grid=(L // tm,), out_shape=(S((L, D), f32), S((L, D), f32)),
        in_specs=[pl.BlockSpec((tm, K), row), ANY, pl.BlockSpec((tm, D), row),
                  pl.BlockSpec((1, D), fix), pl.BlockSpec((1, D), fix)],
        out_specs=(pl.BlockSpec((tm, D), row), pl.BlockSpec((tm, D), row)),
        scratch_shapes=[pltpu.VMEM((K, D), w.dtype), pltpu.SemaphoreType.DMA((1,))],
        compiler_params=_cp(dimension_semantics=("arbitrary",)))(y, w, h, g, b)


def _gdn_in_fwd(h, w_full, conv_w, alog, dtb, *, first_row, H):
    L, D = h.shape
    W = H * DH
    NW = w_full.shape[1]
    KW = conv_w.shape[0]
    tm = TM

    def body(h_ref, w_hbm, cw_ref, alog_ref, dtb_ref,
             pre_ref, z_ref, raw_ref, q_ref, k_ref, v_ref, beta_ref, g_ref,
             w_vmem, scr, carry, sem):
        i = pl.program_id(0)
        _load_once([(w_hbm, w_vmem)], sem)

        @pl.when(i == 0)
        def _():
            carry[...] = jnp.zeros_like(carry)

        hb = _bf(h_ref[...])
        outs = (q_ref, k_ref, v_ref)
        for s in range(3):
            pre = _dot(hb, w_vmem[:, s * W:(s + 1) * W])
            pre_ref[:, s * W:(s + 1) * W] = pre
            scr[0:8, :] = carry[s]
            scr[8:8 + tm, :] = pre
            carry[s] = pre[tm - 8:tm, :]
            c = jnp.zeros((tm, W), f32)
            for j in range(KW):
                c = c + cw_ref[j:j + 1, s * W:(s + 1) * W] * scr[pl.ds(8 - (KW - 1) + j, tm), :]
            sl = c * _sigmoid(c)
            if s < 2:
                scale = DH ** -0.5 if s == 0 else 1.0
                for hh in range(H):
                    seg = sl[:, hh * DH:(hh + 1) * DH]
                    r = lax.rsqrt(jnp.sum(seg * seg, axis=-1, keepdims=True) + L2_EPS)
                    outs[s][:, hh * DH:(hh + 1) * DH] = seg * (r * scale)
            else:
                v_ref[...] = sl
        z_ref[...] = _dot(hb, w_vmem[:, 3 * W:4 * W])
        raw = _dot(hb, w_vmem[:, 4 * W:4 * W + 2 * LANE])
        raw_ref[...] = raw
        ok = (_row_ids(i, tm, LANE) >= first_row) & (lax.broadcasted_iota(jnp.int32, (tm, LANE), 1) < H)
        beta_ref[...] = jnp.where(ok, _sigmoid(raw[:, :LANE]), 0.0)
        a = raw[:, LANE:] + dtb_ref[...]
        sp = jnp.maximum(a, 0.0) + jnp.log(1.0 + jnp.exp(-jnp.abs(a)))
        gv = jnp.where(ok, -jnp.exp(alog_ref[...]) * sp, 0.0)
        g_ref[...] = _dot(_chunk_tri(tm, lower=True), gv, HI)

    row = lambda i: (i, 0)
    fix = lambda i: (0, 0)
    out_shape = (S((L, 3 * W), f32), S((L, W), f32), S((L, 2 * LANE), f32),
                 S((L, W), f32), S((L, W), f32), S((L, W), f32), S((L, LANE), f32), S((L, LANE), f32))
    out_specs = (pl.BlockSpec((tm, 3 * W), row), pl.BlockSpec((tm, W), row), pl.BlockSpec((tm, 2 * LANE), row),
                 pl.BlockSpec((tm, W), row), pl.BlockSpec((tm, W), row), pl.BlockSpec((tm, W), row),
                 pl.BlockSpec((tm, LANE), row), pl.BlockSpec((tm, LANE), row))
    return pl.pallas_call(
        body, name="gdn_in_fwd", grid=(L // tm,), out_shape=out_shape,
        in_specs=[pl.BlockSpec((tm, D), row), ANY, pl.BlockSpec((KW, 3 * W), fix),
                  pl.BlockSpec((1, LANE), fix), pl.BlockSpec((1, LANE), fix)],
        out_specs=out_specs,
        scratch_shapes=[pltpu.VMEM((D, NW), w_full.dtype), pltpu.VMEM((tm + 8, W), f32), pltpu.VMEM((3, 8, W), f32),
                        pltpu.SemaphoreType.DMA((1,))],
        compiler_params=_cp(dimension_semantics=("arbitrary",)))(h, w_full, conv_w, alog, dtb)


def _gdn_in_bwd(dq, dk, dv, dz, dg, dbeta, pre, raw, conv_w, alog, dtb, *, first_row, H):
    L = dq.shape[0]
    W = H * DH
    KW = conv_w.shape[0]
    tm = TM
    nb = L // tm
    NW = 4 * W + 2 * LANE

    def body(dq_ref, dk_ref, dv_ref, dz_ref, dg_ref, dbeta_ref, pre_ref, hq_ref, hk_ref, hv_ref, raw_ref,
             cw_ref, alog_ref, dtb_ref, dproj_ref, dcw_ref, dal_ref, ddt_ref, scrx, scrd, carry, tmp):
        i = pl.program_id(0)
        blk = nb - 1 - i

        @pl.when(i == 0)
        def _():
            carry[...] = jnp.zeros_like(carry)
            dcw_ref[...] = jnp.zeros_like(dcw_ref)
            dal_ref[...] = jnp.zeros_like(dal_ref)
            ddt_ref[...] = jnp.zeros_like(ddt_ref)

        halos = (hq_ref, hk_ref, hv_ref)
        douts = (dq_ref, dk_ref, dv_ref)
        for s in range(3):
            sec = slice(s * W, (s + 1) * W)
            scrx[0:8, :] = jnp.where(blk > 0, halos[s][...], 0.0)
            scrx[8:8 + tm, :] = pre_ref[:, sec]
            c = jnp.zeros((tm, W), f32)
            for j in range(KW):
                c = c + cw_ref[j:j + 1, sec] * scrx[pl.ds(8 - (KW - 1) + j, tm), :]
            sig = _sigmoid(c)
            sl = c * sig
            if s < 2:
                scale = DH ** -0.5 if s == 0 else 1.0
                for hh in range(H):
                    hs = slice(hh * DH, (hh + 1) * DH)
                    seg = sl[:, hs]
                    r = lax.rsqrt(jnp.sum(seg * seg, axis=-1, keepdims=True) + L2_EPS)
                    n = seg * r
                    dqs = douts[s][:, hs]
                    tmp[:, hs] = (scale * r) * (dqs - n * jnp.sum(n * dqs, axis=-1, keepdims=True))
                dsl = tmp[...]
            else:
                dsl = dv_ref[...]
            dc = dsl * (sig * (1.0 + c * (1.0 - sig)))
            scrd[0:tm, :] = dc
            scrd[tm:tm + 8, :] = carry[s]
            carry[s] = dc[0:8, :]
            dx = jnp.zeros((tm, W), f32)
            for j in range(KW):
                dx = dx + cw_ref[j:j + 1, sec] * scrd[pl.ds(KW - 1 - j, tm), :]
                dcw_ref[j:j + 1, sec] += jnp.sum(dc * scrx[pl.ds(8 - (KW - 1) + j, tm), :], axis=0, keepdims=True)
            dproj_ref[:, sec] = _bf(dx)
        dproj_ref[:, 3 * W:4 * W] = _bf(dz_ref[...])
        raw_v = raw_ref[...]
        ok = (_row_ids(blk, tm, LANE) >= first_row) & (lax.broadcasted_iota(jnp.int32, (tm, LANE), 1) < H)
        beta = _sigmoid(raw_v[:, :LANE])
        dbraw = jnp.where(ok, dbeta_ref[...] * beta * (1.0 - beta), 0.0)
        a = raw_v[:, LANE:] + dtb_ref[...]
        sp = jnp.maximum(a, 0.0) + jnp.log(1.0 + jnp.exp(-jnp.abs(a)))
        nea = -jnp.exp(alog_ref[...])
        dgm = jnp.where(ok, _dot(_chunk_tri(tm, lower=False), dg_ref[...], HI), 0.0)
        daraw = dgm * nea * _sigmoid(a)
        dal_ref[0:1, :] += jnp.sum(dgm * nea * sp, axis=0, keepdims=True)
        ddt_ref[0:1, :] += jnp.sum(daraw, axis=0, keepdims=True)
        dproj_ref[:, 4 * W:4 * W + LANE] = _bf(dbraw)
        dproj_ref[:, 4 * W + LANE:4 * W + 2 * LANE] = _bf(daraw)

    rev = lambda i: (nb - 1 - i, 0)
    fix = lambda i: (0, 0)

    def halo(col):
        return pl.BlockSpec((8, W), lambda i: (jnp.maximum((nb - 1 - i) * (tm // 8) - 1, 0), col))

    return pl.pallas_call(
        body, name="gdn_in_bwd", grid=(nb,),
        out_shape=(S((L, NW), bf16), S((8, 3 * W), f32), S((8, LANE), f32), S((8, LANE), f32)),
        in_specs=[pl.BlockSpec((tm, W), rev)] * 4 + [pl.BlockSpec((tm, LANE), rev)] * 2
        + [pl.BlockSpec((tm, 3 * W), rev), halo(0), halo(1), halo(2), pl.BlockSpec((tm, 2 * LANE), rev),
           pl.BlockSpec((KW, 3 * W), fix), pl.BlockSpec((1, LANE), fix), pl.BlockSpec((1, LANE), fix)],
        out_specs=(pl.BlockSpec((tm, NW), rev), pl.BlockSpec((8, 3 * W), fix),
                   pl.BlockSpec((8, LANE), fix), pl.BlockSpec((8, LANE), fix)),
        scratch_shapes=[pltpu.VMEM((tm + 8, W), f32), pltpu.VMEM((tm + 8, W), f32), pltpu.VMEM((3, 8, W), f32),
                        pltpu.VMEM((tm, W), f32)],
        compiler_params=_cp(dimension_semantics=("arbitrary",)))(
            dq, dk, dv, dz, dg, dbeta, pre, pre, pre, pre, raw, conv_w, alog, dtb)


def _chunk_tri(n, lower):
    i = lax.broadcasted_iota(jnp.int32, (n, n), 0)
    j = lax.broadcasted_iota(jnp.int32, (n, n), 1)
    sh = int(math.log2(CH))
    same = lax.shift_right_logical(i, sh) == lax.shift_right_logical(j, sh)
    return (same & ((i >= j) if lower else (j >= i))).astype(f32)


def _tri_inv_many(ms, eye):
    ts = [eye - m for m in ms]
    ps = list(ms)
    for _ in range(int(math.log2(CH)) - 1):
        ps = [_dot(p, p, HI3) for p in ps]
        ts = [t + _dot(t, p, HI3) for t, p in zip(ts, ps)]
    return ts


def _chunk_local(q, k, v, gcol, grow, glast, bcol, ii, jj):
    dec = jnp.where(ii >= jj, jnp.exp(jnp.minimum(gcol - grow, 0.0)), 0.0)
    eg = jnp.exp(gcol)
    kb = k * bcol
    kbg = kb * eg
    vb = v * bcol
    qt = q * eg
    kt = k * jnp.exp(glast - gcol)
    kk = _dot_nt(_bf(kb), _bf(k))
    qk = _dot_nt(_bf(q), _bf(k))
    return dec, eg, kb, kbg, vb, qt, kt, kk, qk


def _delta_fwd(q, k, v, g, beta, z, nw, *, H):
    L = q.shape[0]
    W = H * DH
    rb = TM
    nc = rb // CH
    nblk = L // rb

    def body(q_ref, k_ref, v_ref, g_ref, b_ref, z_ref, nw_ref, o_ref, y_ref, s_out, t_out, s_scr):
        @pl.when(pl.program_id(0) == 0)
        def _():
            s_scr[...] = jnp.zeros_like(s_scr)

        ii = lax.broadcasted_iota(jnp.int32, (CH, CH), 0)
        jj = lax.broadcasted_iota(jnp.int32, (CH, CH), 1)
        eye = (ii == jj).astype(f32)
        nwv = nw_ref[...]

        heads = range(H)
        hsl = [slice(hh * DH, (hh + 1) * DH) for hh in heads]

        def chunk(c, carry):
            r0 = pl.multiple_of(c * CH, CH)
            rows = pl.ds(r0, CH)
            gam = g_ref[rows, :]
            gam_t = gam.T
            bb = b_ref[rows, :]
            glast = [gam[CH - 1:CH, hh:hh + 1] for hh in heads]
            loc = [_chunk_local(q_ref[rows, hsl[hh]], k_ref[rows, hsl[hh]], v_ref[rows, hsl[hh]],
                                gam[:, hh:hh + 1], gam_t[hh:hh + 1, :], glast[hh], bb[:, hh:hh + 1], ii, jj)
                   for hh in heads]
            st = [s_scr[hh] for hh in heads]
            zs = [z_ref[rows, hsl[hh]] for hh in heads]
            ts = _tri_inv_many([jnp.where(ii > jj, l[7] * l[0], 0.0) for l in loc], eye)
            us = [_dot(t, l[4], HI3) for t, l in zip(ts, loc)]
            ws = [_dot(t, l[3], HI3) for t, l in zip(ts, loc)]
            stb = [_bf(s) for s in st]
            vn = [u - _dot(_bf(w), sb) for u, w, sb in zip(us, ws, stb)]
            vnb = [_bf(x) for x in vn]
            snew = [s * jnp.exp(gl) + _dot_tn(_bf(l[6]), xb) for s, gl, l, xb in zip(st, glast, loc, vnb)]
            os_ = [_dot(_bf(l[5]), sb) + _dot(_bf(l[8] * l[0]), xb) for l, sb, xb in zip(loc, stb, vnb)]
            for hh in heads:
                o = os_[hh]
                s_out[c, hh] = st[hh]
                t_out[c, hh] = ts[hh]
                s_scr[hh] = snew[hh]
                o_ref[rows, hsl[hh]] = o
                on = o * lax.rsqrt(jnp.mean(o * o, axis=-1, keepdims=True) + RMS_EPS) * nwv
                y_ref[rows, hsl[hh]] = _bf(on * (zs[hh] * _sigmoid(zs[hh])))
            return carry

        lax.fori_loop(0, nc, chunk, 0)

    row = lambda i: (i, 0)
    fix = lambda i: (0, 0)
    return pl.pallas_call(
        body, name="delta_fwd", grid=(nblk,),
        out_shape=(S((L, W), f32), S((L, W), bf16), S((L // CH, H, DH, DH), f32), S((L // CH, H, CH, CH), f32)),
        in_specs=[pl.BlockSpec((rb, W), row)] * 3 + [pl.BlockSpec((rb, LANE), row)] * 2
        + [pl.BlockSpec((rb, W), row), pl.BlockSpec((1, DH), fix)],
        out_specs=(pl.BlockSpec((rb, W), row), pl.BlockSpec((rb, W), row),
                   pl.BlockSpec((nc, H, DH, DH), lambda i: (i, 0, 0, 0)),
                   pl.BlockSpec((nc, H, CH, CH), lambda i: (i, 0, 0, 0))),
        scratch_shapes=[pltpu.VMEM((H, DH, DH), f32)],
        compiler_params=_cp(dimension_semantics=("arbitrary",)))(q, k, v, g, beta, z, nw)


def _delta_bwd(dy, o, z, nw, q, k, v, g, beta, s_all, t_all, *, H):
    L = q.shape[0]
    W = H * DH
    rb = TM
    nc = rb // CH
    nblk = L // rb

    def body(dy_ref, o_ref, z_ref, nw_ref, q_ref, k_ref, v_ref, g_ref, b_ref, s_ref, t_ref,
             dq_ref, dk_ref, dv_ref, dz_ref, dg_ref, db_ref, dnw_ref, ds_scr):
        @pl.when(pl.program_id(0) == 0)
        def _():
            ds_scr[...] = jnp.zeros_like(ds_scr)
            dnw_ref[...] = jnp.zeros_like(dnw_ref)

        ii = lax.broadcasted_iota(jnp.int32, (CH, CH), 0)
        jj = lax.broadcasted_iota(jnp.int32, (CH, CH), 1)
        lane = lax.broadcasted_iota(jnp.int32, (CH, LANE), 1)
        last_row = lax.broadcasted_iota(jnp.int32, (CH, 1), 0) == CH - 1
        nwv = nw_ref[...]

        def chunk(cc, carry):
            c = nc - 1 - cc
            r0 = pl.multiple_of(c * CH, CH)
            rows = pl.ds(r0, CH)
            gam = g_ref[rows, :]
            gam_t = gam.T
            bb = b_ref[rows, :]
            def head(hh):
                hs = slice(hh * DH, (hh + 1) * DH)
                gcol, grow, glast = gam[:, hh:hh + 1], gam_t[hh:hh + 1, :], gam[CH - 1:CH, hh:hh + 1]
                bcol = bb[:, hh:hh + 1]
                qh, kh, vh = q_ref[rows, hs], k_ref[rows, hs], v_ref[rows, hs]
                oh, zh, dyh = o_ref[rows, hs], z_ref[rows, hs], dy_ref[rows, hs]
                t = t_ref[c, hh]
                st = s_ref[c, hh]
                dsn = ds_scr[hh]
                rms = lax.rsqrt(jnp.mean(oh * oh, axis=-1, keepdims=True) + RMS_EPS)
                on = oh * rms
                sig = _sigmoid(zh)
                sz = zh * sig
                dz = dyh * on * nwv * (sig * (1.0 + zh * (1.0 - sig)))
                dnw = jnp.sum(dyh * on * sz, axis=0, keepdims=True)
                don = dyh * nwv * sz
                do = rms * (don - on * jnp.mean(don * on, axis=-1, keepdims=True))
                dec, eg, kb, kbg, vb, qt, kt, kk, qk = _chunk_local(qh, kh, vh, gcol, grow, glast, bcol, ii, jj)
                stb, dsnb, dob, tb = _bf(st), _bf(dsn), _bf(do), _bf(t)
                u = _dot(t, vb, HI3)
                w = _dot(t, kbg, HI3)
                dqt = _dot_nt(dob, stb)
                ds_new = _dot_tn(_bf(qt), dob)
                yield
                mm = jnp.where(ii > jj, kk * dec, 0.0)
                attn = qk * dec
                wb = _bf(w)
                vn = u - _dot(wb, stb)
                dvn = _dot_tn(_bf(attn), dob) + _dot(_bf(kt), dsnb)
                egl = jnp.exp(glast)
                ekt = jnp.exp(glast - gcol)
                yield
                vnb, dvnb = _bf(vn), _bf(dvn)
                dattn = jnp.where(ii >= jj, _dot_nt(dob, vnb), 0.0)
                dkt = _dot_nt(vnb, dsnb)
                ds_new = ds_new + egl * dsn - _dot_tn(wb, dvnb)
                dw = -_dot_nt(dvnb, stb)
                dvb = _dot_tn(tb, dvnb)
                dt_u = _dot_nt(dvnb, _bf(vb))
                dglast = egl * jnp.sum(jnp.sum(dsn * st, axis=0, keepdims=True), axis=1, keepdims=True)
                yield
                dwb = _bf(dw)
                dkbg = _dot_tn(tb, dwb)
                dt = dt_u + _dot_nt(dwb, _bf(kbg))
                yield
                x = _dot_nt(_bf(dt), tb)
                yield
                dm = jnp.where(ii > jj, -_dot_tn(tb, _bf(x)), 0.0)
                dkk = dm * dec
                dqk = dattn * dec
                e = dm * mm + dattn * attn
                dgam = jnp.sum(e, axis=1, keepdims=True) - jnp.sum(e.T, axis=1, keepdims=True)
                dkkb, dqkb, kbf = _bf(dkk), _bf(dqk), _bf(kh)
                dkb = _dot(dkkb, kbf) + dkbg * eg
                dk = _dot_tn(dkkb, _bf(kb)) + _dot_tn(dqkb, _bf(qh)) + dkt * ekt + dkb * bcol
                dq = _dot(dqkb, kbf) + dqt * eg
                yield
                dktkt = dkt * kt
                dgam = dgam + jnp.sum(dqt * qt - dktkt + dkbg * kbg, axis=1, keepdims=True)
                dglast = dglast + jnp.sum(jnp.sum(dktkt, axis=0, keepdims=True), axis=1, keepdims=True)
                dgam = dgam + jnp.where(last_row, dglast, 0.0)
                dbeta = jnp.sum(dkb * kh + dvb * vh, axis=1, keepdims=True)
                return dq, dk, dvb * bcol, dz, dgam, dbeta, dnw, ds_new

            res = [None] * H
            for h0 in range(0, H, BWD_HEAD_GROUP):
                group = range(h0, min(h0 + BWD_HEAD_GROUP, H))
                gens = {hh: head(hh) for hh in group}
                while any(res[hh] is None for hh in group):
                    for hh in group:
                        try:
                            next(gens[hh])
                        except StopIteration as stop:
                            res[hh] = stop.value
            dgam_all = jnp.zeros((CH, LANE), f32)
            dbeta_all = jnp.zeros((CH, LANE), f32)
            dnw_acc = jnp.zeros((1, DH), f32)
            for hh in range(H):
                hs = slice(hh * DH, (hh + 1) * DH)
                dq, dk, dv_, dz, dgam, dbeta, dnw, ds_new = res[hh]
                dq_ref[rows, hs] = dq
                dk_ref[rows, hs] = dk
                dv_ref[rows, hs] = dv_
                dz_ref[rows, hs] = dz
                ds_scr[hh] = ds_new
                dgam_all = dgam_all + jnp.where(lane == hh, dgam, 0.0)
                dbeta_all = dbeta_all + jnp.where(lane == hh, dbeta, 0.0)
                dnw_acc = dnw_acc + dnw
            dg_ref[rows, :] = dgam_all
            db_ref[rows, :] = dbeta_all
            dnw_ref[0:1, :] += dnw_acc
            return carry

        lax.fori_loop(0, nc, chunk, 0)

    rev = lambda i: (nblk - 1 - i, 0)
    rev4 = lambda i: (nblk - 1 - i, 0, 0, 0)
    fix = lambda i: (0, 0)
    wide = pl.BlockSpec((rb, W), rev)
    thin = pl.BlockSpec((rb, LANE), rev)
    return pl.pallas_call(
        body, name="delta_bwd", grid=(nblk,),
        out_shape=(S((L, W), f32), S((L, W), f32), S((L, W), f32), S((L, W), f32),
                   S((L, LANE), f32), S((L, LANE), f32), S((8, DH), f32)),
        in_specs=[wide, wide, wide, pl.BlockSpec((1, DH), fix), wide, wide, wide, thin, thin,
                  pl.BlockSpec((nc, H, DH, DH), rev4), pl.BlockSpec((nc, H, CH, CH), rev4)],
        out_specs=(wide, wide, wide, wide, thin, thin, pl.BlockSpec((8, DH), fix)),
        scratch_shapes=[pltpu.VMEM((H, DH, DH), f32)],
        compiler_params=_cp(dimension_semantics=("arbitrary",)))(dy, o, z, nw, q, k, v, g, beta, s_all, t_all)


def _sc_fwd(h, w_in, conv_w, w_out, g, b, *, first_row):
    L, D = h.shape
    W = w_out.shape[0]
    KW = conv_w.shape[0]
    tm = TM
    alpha = ALPHA

    def body(h_ref, win_hbm, cw_ref, wout_hbm, g_ref, b_ref, proj_ref, bu_ref, pre_ref, out_ref,
             win, wout, scr, carry, sem):
        i = pl.program_id(0)
        _load_once([(win_hbm, win), (wout_hbm, wout)], sem)

        @pl.when(i == 0)
        def _():
            carry[...] = jnp.zeros_like(carry)

        hv = h_ref[...]
        hb = _bf(hv)
        bg = _dot(hb, win[:, 0:W])
        cg = _dot(hb, win[:, W:2 * W])
        xv = _dot(hb, win[:, 2 * W:3 * W])
        proj_ref[:, 0:W] = bg
        proj_ref[:, W:2 * W] = cg
        proj_ref[:, 2 * W:3 * W] = xv
        p = cg * xv
        scr[0:8, :] = carry[...]
        scr[8:8 + tm, :] = p
        carry[...] = p[tm - 8:tm, :]
        u = jnp.zeros((tm, W), f32)
        for j in range(KW):
            u = u + cw_ref[j:j + 1, :] * scr[pl.ds(8 - (KW - 1) + j, tm), :]
        bu = _bf(bg * u)
        bu_ref[...] = bu
        pre = alpha * hv + _dot(bu, wout[...])
        pre_ref[...] = pre
        out_ref[...] = _ln_fwd(pre, g_ref[...], b_ref[...], _row_ids(i, tm, D), first_row)

    row = lambda i: (i, 0)
    fix = lambda i: (0, 0)
    return pl.pallas_call(
        body, name="sc_fwd", grid=(L // tm,),
        out_shape=(S((L, 3 * W), f32), S((L, W), bf16), S((L, D), f32), S((L, D), f32)),
        in_specs=[pl.BlockSpec((tm, D), row), ANY, pl.BlockSpec((KW, W), fix), ANY,
                  pl.BlockSpec((1, D), fix), pl.BlockSpec((1, D), fix)],
        out_specs=(pl.BlockSpec((tm, 3 * W), row), pl.BlockSpec((tm, W), row),
                   pl.BlockSpec((tm, D), row), pl.BlockSpec((tm, D), row)),
        scratch_shapes=[pltpu.VMEM((D, 3 * W), w_in.dtype), pltpu.VMEM((W, D), w_out.dtype), pltpu.VMEM((tm + 8, W), f32),
                        pltpu.VMEM((8, W), f32), pltpu.SemaphoreType.DMA((2,))],
        compiler_params=_cp(dimension_semantics=("arbitrary",)))(h, w_in, conv_w, w_out, g, b)


def _sc_bwd(dbu, proj, conv_w):
    L, W = dbu.shape
    KW = conv_w.shape[0]
    tm = TM
    nb = L // tm

    def body(dbu_ref, proj_ref, hc_ref, hx_ref, cw_ref, dproj_ref, dcw_ref, scrx, scrd, carry):
        i = pl.program_id(0)
        blk = nb - 1 - i

        @pl.when(i == 0)
        def _():
            carry[...] = jnp.zeros_like(carry)
            dcw_ref[...] = jnp.zeros_like(dcw_ref)

        bg, cg, xv = proj_ref[:, 0:W], proj_ref[:, W:2 * W], proj_ref[:, 2 * W:3 * W]
        scrx[0:8, :] = jnp.where(blk > 0, hc_ref[...] * hx_ref[...], 0.0)
        scrx[8:8 + tm, :] = cg * xv
        u = jnp.zeros((tm, W), f32)
        for j in range(KW):
            u = u + cw_ref[j:j + 1, :] * scrx[pl.ds(8 - (KW - 1) + j, tm), :]
        d = dbu_ref[...]
        dproj_ref[:, 0:W] = _bf(d * u)
        du = d * bg
        scrd[0:tm, :] = du
        scrd[tm:tm + 8, :] = carry[...]
        carry[...] = du[0:8, :]
        dp = jnp.zeros((tm, W), f32)
        for j in range(KW):
            dp = dp + cw_ref[j:j + 1, :] * scrd[pl.ds(KW - 1 - j, tm), :]
            dcw_ref[j:j + 1, :] += jnp.sum(du * scrx[pl.ds(8 - (KW - 1) + j, tm), :], axis=0, keepdims=True)
        dproj_ref[:, W:2 * W] = _bf(dp * xv)
        dproj_ref[:, 2 * W:3 * W] = _bf(dp * cg)

    rev = lambda i: (nb - 1 - i, 0)
    fix = lambda i: (0, 0)

    def halo(col):
        return pl.BlockSpec((8, W), lambda i: (jnp.maximum((nb - 1 - i) * (tm // 8) - 1, 0), col))

    return pl.pallas_call(
        body, name="sc_bwd", grid=(nb,), out_shape=(S((L, 3 * W), bf16), S((8, W), f32)),
        in_specs=[pl.BlockSpec((tm, W), rev), pl.BlockSpec((tm, 3 * W), rev), halo(1), halo(2),
                  pl.BlockSpec((KW, W), fix)],
        out_specs=(pl.BlockSpec((tm, 3 * W), rev), pl.BlockSpec((8, W), fix)),
        scratch_shapes=[pltpu.VMEM((tm + 8, W), f32), pltpu.VMEM((tm + 8, W), f32), pltpu.VMEM((8, W), f32)],
        compiler_params=_cp(dimension_semantics=("arbitrary",)))(dbu, proj, proj, proj, conv_w)


def _ffn_cols(F):
    fc = F
    for cand in (1408, 1024, 512, 256, 128):
        if F % cand == 0:
            fc = cand
            break
    return fc


def _ffn_fwd(h, w_up, conv_w, w_down, g, b, *, first_row, name):
    L, D = h.shape
    F = w_down.shape[0]
    KW = conv_w.shape[0]
    tm = TM
    fc = _ffn_cols(F)
    alpha = ALPHA

    def body(h_ref, wup_hbm, cw_ref, wdn_hbm, g_ref, b_ref, up_ref, a_ref, pre_ref, out_ref,
             wup, wdn, scr, carry, sem):
        i = pl.program_id(0)
        _load_once([(wup_hbm, wup), (wdn_hbm, wdn)], sem)

        @pl.when(i == 0)
        def _():
            carry[...] = jnp.zeros_like(carry)

        hv = h_ref[...]
        hb = _bf(hv)
        pre = alpha * hv
        for c0 in range(0, F, fc):
            cs = slice(c0, c0 + fc)
            u = _dot(hb, wup[:, cs])
            gate = _dot(hb, wup[:, F + c0:F + c0 + fc])
            up_ref[:, cs] = u
            up_ref[:, F + c0:F + c0 + fc] = gate
            scr[0:8, :] = carry[:, cs]
            scr[8:8 + tm, :] = u
            carry[:, cs] = u[tm - 8:tm, :]
            uc = jnp.zeros((tm, fc), f32)
            for j in range(KW):
                uc = uc + cw_ref[j:j + 1, cs] * scr[pl.ds(8 - (KW - 1) + j, tm), :]
            ab = _bf(uc * _sigmoid(uc) * gate)
            a_ref[:, cs] = ab
            pre = pre + _dot(ab, wdn[cs, :])
        pre_ref[...] = pre
        out_ref[...] = _ln_fwd(pre, g_ref[...], b_ref[...], _row_ids(i, tm, D), first_row)

    row = lambda i: (i, 0)
    fix = lambda i: (0, 0)
    return pl.pallas_call(
        body, name=name, grid=(L // tm,),
        out_shape=(S((L, 2 * F), f32), S((L, F), bf16), S((L, D), f32), S((L, D), f32)),
        in_specs=[pl.BlockSpec((tm, D), row), ANY, pl.BlockSpec((KW, F), fix), ANY,
                  pl.BlockSpec((1, D), fix), pl.BlockSpec((1, D), fix)],
        out_specs=(pl.BlockSpec((tm, 2 * F), row), pl.BlockSpec((tm, F), row),
                   pl.BlockSpec((tm, D), row), pl.BlockSpec((tm, D), row)),
        scratch_shapes=[pltpu.VMEM((D, 2 * F), w_up.dtype), pltpu.VMEM((F, D), w_down.dtype), pltpu.VMEM((tm + 8, fc), f32),
                        pltpu.VMEM((8, F), f32), pltpu.SemaphoreType.DMA((2,))],
        compiler_params=_cp(dimension_semantics=("arbitrary",)))(h, w_up, conv_w, w_down, g, b)


def _ffn_bwd(dpre, up, w_down, conv_w, *, name):
    L, D = dpre.shape
    F = w_down.shape[0]
    KW = conv_w.shape[0]
    tm = TM
    nb = L // tm
    fc = _ffn_cols(F)

    def body(dpre_ref, up_ref, halo_ref, wdn_hbm, cw_ref, dup_ref, dcw_ref, wdn, scrx, scrd, carry, sem):
        i = pl.program_id(0)
        blk = nb - 1 - i
        _load_once([(wdn_hbm, wdn)], sem)

        @pl.when(i == 0)
        def _():
            carry[...] = jnp.zeros_like(carry)
            dcw_ref[...] = jnp.zeros_like(dcw_ref)

        db = _bf(dpre_ref[...])
        for c0 in range(0, F, fc):
            cs = slice(c0, c0 + fc)
            da = _dot_nt(db, wdn[cs, :])
            gate = up_ref[:, F + c0:F + c0 + fc]
            scrx[0:8, :] = jnp.where(blk > 0, halo_ref[:, cs], 0.0)
            scrx[8:8 + tm, :] = up_ref[:, cs]
            uc = jnp.zeros((tm, fc), f32)
            for j in range(KW):
                uc = uc + cw_ref[j:j + 1, cs] * scrx[pl.ds(8 - (KW - 1) + j, tm), :]
            sig = _sigmoid(uc)
            dup_ref[:, F + c0:F + c0 + fc] = _bf(da * (uc * sig))
            duc = da * gate * (sig * (1.0 + uc * (1.0 - sig)))
            scrd[0:tm, :] = duc
            scrd[tm:tm + 8, :] = carry[:, cs]
            carry[:, cs] = duc[0:8, :]
            du = jnp.zeros((tm, fc), f32)
            for j in range(KW):
                du = du + cw_ref[j:j + 1, cs] * scrd[pl.ds(KW - 1 - j, tm), :]
                dcw_ref[j:j + 1, cs] += jnp.sum(duc * scrx[pl.ds(8 - (KW - 1) + j, tm), :], axis=0, keepdims=True)
            dup_ref[:, cs] = _bf(du)

    rev = lambda i: (nb - 1 - i, 0)
    fix = lambda i: (0, 0)
    return pl.pallas_call(
        body, name=name, grid=(nb,), out_shape=(S((L, 2 * F), bf16), S((8, F), f32)),
        in_specs=[pl.BlockSpec((tm, D), rev), pl.BlockSpec((tm, 2 * F), rev),
                  pl.BlockSpec((8, F), lambda i: (jnp.maximum((nb - 1 - i) * (tm // 8) - 1, 0), 0)),
                  ANY, pl.BlockSpec((KW, F), fix)],
        out_specs=(pl.BlockSpec((tm, 2 * F), rev), pl.BlockSpec((8, F), fix)),
        scratch_shapes=[pltpu.VMEM((F, D), w_down.dtype), pltpu.VMEM((tm + 8, fc), f32), pltpu.VMEM((tm + 8, fc), f32),
                        pltpu.VMEM((8, F), f32), pltpu.SemaphoreType.DMA((1,))],
        compiler_params=_cp(dimension_semantics=("arbitrary",)))(dpre, up, up, w_down, conv_w)


def _loss_head(h, target):
    L, D = h.shape
    tm = TM
    pb = PADF // tm

    def body(h_ref, t_ref, dh_ref, loss_ref):
        i = pl.program_id(0)

        @pl.when(i == 0)
        def _():
            loss_ref[...] = jnp.zeros_like(loss_ref)

        valid = i >= pb
        err = h_ref[...] - t_ref[...]
        dh_ref[...] = jnp.where(valid, err * (1.0 / D), 0.0)
        part = 0.5 * jnp.sum(jnp.sum(err * err, axis=-1, keepdims=True) * (1.0 / D), axis=0, keepdims=True)
        loss_ref[...] += jnp.where(valid, part, 0.0)

    return pl.pallas_call(
        body, name="loss_head", grid=(L // tm,), out_shape=(S((L, D), f32), S((8, LANE), f32)),
        in_specs=[pl.BlockSpec((tm, D), lambda i: (i, 0)),
                  pl.BlockSpec((tm, D), lambda i: (jnp.maximum(i - pb, 0), 0))],
        out_specs=(pl.BlockSpec((tm, D), lambda i: (i, 0)), pl.BlockSpec((8, LANE), lambda i: (0, 0))),
        compiler_params=_cp(dimension_semantics=("arbitrary",)))(h, target)


def _ln_bwd(dout, pre, g, *, first_row, name):
    L, D = pre.shape
    tm = TM

    def body(do_ref, pre_ref, g_ref, dpre_ref, dg_ref, db_ref):
        i = pl.program_id(0)

        @pl.when(i == 0)
        def _():
            dg_ref[...] = jnp.zeros_like(dg_ref)
            db_ref[...] = jnp.zeros_like(db_ref)

        pre_v = pre_ref[...]
        mu = jnp.mean(pre_v, axis=-1, keepdims=True)
        xc = pre_v - mu
        rstd = lax.rsqrt(jnp.mean(xc * xc, axis=-1, keepdims=True) + LN_EPS)
        xh = xc * rstd
        dy = jnp.where(_row_ids(i, tm, D) >= first_row, do_ref[...], 0.0)
        dg_ref[0:1, :] += jnp.sum(dy * xh, axis=0, keepdims=True)
        db_ref[0:1, :] += jnp.sum(dy, axis=0, keepdims=True)
        dxh = dy * g_ref[...]
        dpre_ref[...] = rstd * (dxh - jnp.mean(dxh, axis=-1, keepdims=True)
                                - xh * jnp.mean(dxh * xh, axis=-1, keepdims=True))

    row = lambda i: (i, 0)
    fix = lambda i: (0, 0)
    return pl.pallas_call(
        body, name=name, grid=(L // tm,), out_shape=(S((L, D), f32), S((8, D), f32), S((8, D), f32)),
        in_specs=[pl.BlockSpec((tm, D), row), pl.BlockSpec((tm, D), row), pl.BlockSpec((1, D), fix)],
        out_specs=(pl.BlockSpec((tm, D), row), pl.BlockSpec((8, D), fix), pl.BlockSpec((8, D), fix)),
        compiler_params=_cp(dimension_semantics=("arbitrary",)))(dout, pre, g)


def _adamw(g_terms, w, m, v, *, name):
    R, C = w.shape
    tr = _row_tile(R)
    n = len(g_terms)
    c1 = 1.0 - ADAM_B1 ** ADAM_STEP
    c2 = 1.0 - ADAM_B2 ** ADAM_STEP

    def body(*refs):
        g = refs[0][...].astype(f32)
        for r in refs[1:n]:
            g = g + r[...].astype(f32)
        w_ref, m_ref, v_ref, g_out, d_out, m_out, v_out = refs[n:]
        mn = ADAM_B1 * m_ref[...] + (1.0 - ADAM_B1) * g
        vn = ADAM_B2 * v_ref[...] + (1.0 - ADAM_B2) * (g * g)
        g_out[...] = g
        m_out[...] = mn
        v_out[...] = vn
        d_out[...] = -ADAM_LR * ((mn / c1) / (jnp.sqrt(vn / c2) + ADAM_EPS) + ADAM_WD * w_ref[...])

    spec = pl.BlockSpec((tr, C), lambda i: (i, 0))
    return pl.pallas_call(
        body, name=name, grid=(R // tr,), out_shape=(S((R, C), f32),) * 4,
        in_specs=[spec] * (n + 3), out_specs=(spec,) * 4,
        compiler_params=_cp(dimension_semantics=("arbitrary",)))(*g_terms, w, m, v)


def _sum_devices(x):
    n, R, C = x.shape

    def body(x_ref, o_ref):
        acc = x_ref[0]
        for d in range(1, n):
            acc = acc + x_ref[d]
        o_ref[...] = acc

    return pl.pallas_call(body, name="sum_devices", out_shape=S((R, C), f32), compiler_params=_cp())(x)


def _row_tile(R):
    for step in (16, 8):
        for t in range(256, 0, -step):
            if R % t == 0:
                return t
    return R


def _pair_add(s32, recv1, sidx, ridx, *, name):
    _, L, K, n = s32.shape
    nj = sidx.shape[0]
    tk = _row_tile(K)

    def body(sidx_ref, ridx_ref, a_ref, b_ref, o_ref):
        o_ref[...] = _bf(a_ref[...] + b_ref[...].astype(f32))

    blk = (1, 1, tk, n)
    grid_spec = pltpu.PrefetchScalarGridSpec(
        num_scalar_prefetch=2, grid=(nj, L, K // tk),
        in_specs=[pl.BlockSpec(blk, lambda j, l, i, si, ri: (si[j], l, i, 0)),
                  pl.BlockSpec(blk, lambda j, l, i, si, ri: (ri[j], l, i, 0))],
        out_specs=pl.BlockSpec(blk, lambda j, l, i, si, ri: (j, l, i, 0)))
    return pl.pallas_call(
        body, name=name, grid_spec=grid_spec, out_shape=S((nj, L, K, n), bf16),
        compiler_params=_cp(dimension_semantics=("arbitrary",) * 3))(sidx, ridx, s32, recv1)


def _adamw_shard(s32, recv1, recv2, w, m, v, idx, *, name):
    L, K, n = w.shape
    tk = _row_tile(K)
    c1 = 1.0 - ADAM_B1 ** ADAM_STEP
    c2 = 1.0 - ADAM_B2 ** ADAM_STEP

    def body(idx_ref, a_ref, b_ref, r_ref, w_ref, m_ref, v_ref, g_out, d_out, m_out, v_out):
        g = a_ref[0] + b_ref[0].astype(f32)
        for j in range(3):
            g = g + r_ref[j].astype(f32)
        mn = ADAM_B1 * m_ref[...] + (1.0 - ADAM_B1) * g
        vn = ADAM_B2 * v_ref[...] + (1.0 - ADAM_B2) * (g * g)
        g_out[...] = g
        m_out[...] = mn
        v_out[...] = vn
        d_out[...] = -ADAM_LR * ((mn / c1) / (jnp.sqrt(vn / c2) + ADAM_EPS) + ADAM_WD * w_ref[...])

    own = pl.BlockSpec((1, tk, n), lambda l, i, ix: (l, i, 0))
    grid_spec = pltpu.PrefetchScalarGridSpec(
        num_scalar_prefetch=1, grid=(L, K // tk),
        in_specs=[pl.BlockSpec((1, 1, tk, n), lambda l, i, ix: (ix[0], l, i, 0)),
                  pl.BlockSpec((1, 1, tk, n), lambda l, i, ix: (ix[1], l, i, 0)),
                  pl.BlockSpec((3, 1, tk, n), lambda l, i, ix: (0, l, i, 0)), own, own, own],
        out_specs=(own,) * 4)
    return pl.pallas_call(
        body, name=name, grid_spec=grid_spec, out_shape=(S((L, K, n), f32),) * 4,
        compiler_params=_cp(dimension_semantics=("arbitrary", "arbitrary")))(idx, s32, recv1, recv2, w, m, v)


def _col_segments(n, mapping):
    segs = []
    for p in range(N_DEV):
        lo, hi = p * n, (p + 1) * n
        out = []
        for c0, c1, e0 in mapping:
            a, b = max(lo, c0), min(hi, c1)
            if a < b:
                out.append((a - lo, e0 + (a - c0), b - a))
        segs.append(out)
    return segs


def _assemble_cols(gathered, mapping, n_out, *, name):
    _, L, K, n = gathered.shape
    tk = _row_tile(K)
    segs = _col_segments(n, mapping)
    covered = sum(w for s in segs for (_, _, w) in s)

    def body(g_ref, o_ref):
        if covered != n_out:
            o_ref[...] = jnp.zeros_like(o_ref)
        for p in range(N_DEV):
            for s0, d0, w in segs[p]:
                o_ref[0, :, d0:d0 + w] = g_ref[p, 0, :, s0:s0 + w]

    return pl.pallas_call(
        body, name=name, grid=(L, K // tk), out_shape=S((L, K, n_out), gathered.dtype),
        in_specs=[pl.BlockSpec((N_DEV, 1, tk, n), lambda l, i: (0, l, i, 0))],
        out_specs=pl.BlockSpec((1, tk, n_out), lambda l, i: (l, i, 0)),
        compiler_params=_cp(dimension_semantics=("arbitrary", "arbitrary")))(gathered)


def _split_cols(dws, mapping, n, *, name):
    L = len(dws)
    K, n_in = dws[0].shape
    tk = _row_tile(K)
    segs = _col_segments(n, mapping)

    def body(*refs):
        ins, o32, o16 = refs[:L], refs[L], refs[L + 1]
        for li in range(L):
            @pl.when(pl.program_id(0) == li)
            def _(li=li):
                for p in range(N_DEV):
                    for s0, d0, w in segs[p]:
                        val = ins[li][:, d0:d0 + w]
                        o32[p, 0, :, s0:s0 + w] = val
                        o16[p, 0, :, s0:s0 + w] = _bf(val)

    out = pl.BlockSpec((N_DEV, 1, tk, n), lambda l, i: (0, l, i, 0))
    return pl.pallas_call(
        body, name=name, grid=(L, K // tk), out_shape=(S((N_DEV, L, K, n), f32), S((N_DEV, L, K, n), bf16)),
        in_specs=[pl.BlockSpec((tk, n_in), lambda l, i: (i, 0))] * L, out_specs=(out, out),
        compiler_params=_cp(dimension_semantics=("arbitrary", "arbitrary")))(*dws)


def _split_rows(dws, k, *, name):
    L = len(dws)
    N = dws[0].shape[1]

    def body(*refs):
        ins, o32, o16 = refs[:L], refs[L], refs[L + 1]
        for li in range(L):
            @pl.when(pl.program_id(0) == li)
            def _(li=li):
                val = ins[li][...]
                o32[0, 0] = val
                o16[0, 0] = _bf(val)

    out = pl.BlockSpec((1, 1, k, N), lambda l, p: (p, l, 0, 0))
    return pl.pallas_call(
        body, name=name, grid=(L, N_DEV), out_shape=(S((N_DEV, L, k, N), f32), S((N_DEV, L, k, N), bf16)),
        in_specs=[pl.BlockSpec((k, N), lambda l, p: (p, 0))] * L, out_specs=(out, out),
        compiler_params=_cp(dimension_semantics=("arbitrary", "arbitrary")))(*dws)


def _rows_full(gathered):
    _, L, k, N = gathered.shape
    return jnp.transpose(gathered, (1, 0, 2, 3)).reshape(L, N_DEV * k, N)


def _peer(rel):
    x, y, c = lax.axis_index("x"), lax.axis_index("y"), lax.axis_index("c")
    return {"c": (x, y, 1 - c), "x": (1 - x, y, c), "y": (x, 1 - y, c), "xy": (1 - x, 1 - y, c)}[rel]


def _all_gather(xs, *, name):
    na = len(xs)

    def body(*refs):
        x_refs, out_refs = refs[:na], refs[na:2 * na]
        send_sems, recv_sems, local_sems = refs[2 * na:]
        mx, my, mc = lax.axis_index("x"), lax.axis_index("y"), lax.axis_index("c")
        me, sibling = (mx, my, mc), (mx, my, 1 - mc)
        chips = [(1 - mx, my), (mx, 1 - my), (1 - mx, 1 - my)]

        def slot(a, px, py, pc):
            return out_refs[a].at[4 * px + 2 * py + pc]

        def copy(a, kk, block, to, src=None):
            return pltpu.make_async_remote_copy(
                src_ref=slot(a, *block) if src is None else src, dst_ref=slot(a, *block),
                send_sem=send_sems.at[7 * a + kk], recv_sem=recv_sems.at[7 * a + kk], device_id=to, device_id_type=MESH)

        mine = [pltpu.make_async_copy(x_refs[a], slot(a, *me), local_sems.at[a]) for a in range(na)]
        for cp in mine:
            cp.start()
        first = []
        for a in range(na):
            first.append(copy(a, 0, me, sibling, src=x_refs[a]))
            first += [copy(a, 1 + j, me, (*chip, mc), src=x_refs[a]) for j, chip in enumerate(chips)]
        for cp in first:
            cp.start()
        passed = []
        for j, chip in enumerate(chips):
            for a in range(na):
                copy(a, 1 + j, (*chip, mc), me).wait_recv()
                fwd = copy(a, 4 + j, (*chip, mc), sibling)
                fwd.start()
                passed.append(fwd)
        for a in range(na):
            copy(a, 0, sibling, me).wait_recv()
            for j, chip in enumerate(chips):
                copy(a, 4 + j, (*chip, 1 - mc), me).wait_recv()
        for cp in first + passed:
            cp.wait_send()
        for cp in mine:
            cp.wait()

    return pl.pallas_call(
        body, name=name, out_shape=tuple(S((N_DEV,) + x.shape, x.dtype) for x in xs),
        in_specs=[ANY] * na, out_specs=(ANY,) * na,
        scratch_shapes=[pltpu.SemaphoreType.DMA((7 * na,)), pltpu.SemaphoreType.DMA((7 * na,)),
                        pltpu.SemaphoreType.DMA((na,))],
        compiler_params=pltpu.CompilerParams(has_side_effects=True))(*xs)


def _exchange(sends, n_slots, src_index, rels, *, name):
    na = len(sends)

    def body(*refs):
        send_refs, recv_refs = refs[:na], refs[na:2 * na]
        send_sems, recv_sems = refs[2 * na:]
        cps = [pltpu.make_async_remote_copy(
            src_ref=send_refs[a].at[src_index(j)], dst_ref=recv_refs[a].at[j],
            send_sem=send_sems.at[n_slots * a + j], recv_sem=recv_sems.at[n_slots * a + j],
            device_id=_peer(rels[j]), device_id_type=MESH) for a in range(na) for j in range(n_slots)]
        for cp in cps:
            cp.start()
        for cp in cps:
            cp.wait()

    return pl.pallas_call(
        body, name=name, out_shape=tuple(S((n_slots,) + s.shape[1:], s.dtype) for s in sends),
        in_specs=[ANY] * na, out_specs=(ANY,) * na,
        scratch_shapes=[pltpu.SemaphoreType.DMA((n_slots * na,)), pltpu.SemaphoreType.DMA((n_slots * na,))],
        compiler_params=pltpu.CompilerParams(has_side_effects=True))(*sends)


def _pack_small(parts, width):
    rows, offs, r = [], [], 0
    for a in parts:
        n = a.size
        nr = -(-n // width)
        flat = a.reshape(-1).astype(f32)
        if nr * width != n:
            flat = jnp.pad(flat, (0, nr * width - n))
        rows.append(flat.reshape(nr, width))
        offs.append((r, nr))
        r += nr
    buf = jnp.concatenate(rows, axis=0)
    pad = (-r) % 8
    if pad:
        buf = jnp.pad(buf, ((0, pad), (0, 0)))
    return buf, offs


def _unpack_small(buf, off, shape):
    r, nr = off
    return buf[r:r + nr].reshape(-1)[:math.prod(shape)].reshape(shape)


def _local_step(x, target, meta, wts, small):
    SEQ, D = x.shape
    n_meta = meta.shape[0]
    first_row = PADF - n_meta
    H = small["a_log"].shape[-1]
    W = H * DH
    F = wts["ffn_w_down"].shape[1]

    h0 = jnp.concatenate([jnp.zeros((first_row, D), f32), meta, x], axis=0)

    def lanes(a):
        return jnp.pad(a.reshape(1, -1), ((0, 0), (0, LANE - a.size)))

    alog, dtb = lanes(small["a_log"][0]), lanes(small["a_dt_bias"][0])
    a_conv, b_conv = small["a_conv"][0], small["b_conv"][0]
    nw = small["a_norm"][0].reshape(1, DH)
    lmg, lmb, lfg, lfb = small["ln_mix_g"], small["ln_mix_b"], small["ln_ffn_g"], small["ln_ffn_b"]

    pre_a, z, raw, q, k, v, beta, g = _gdn_in_fwd(h0, wts["a_w_in"], a_conv, alog, dtb, first_row=first_row, H=H)
    o, y, s_all, t_all = _delta_fwd(q, k, v, g, beta, z, nw, H=H)
    pre1, h1 = _out_res_ln(y, wts["a_w_out"], h0, lmg[0:1], lmb[0:1], first_row=first_row, name="gdn_out_ln")
    up0, act0, pre2, h2 = _ffn_fwd(h1, wts["ffn_w_up"][0], small["ffn_conv"][0], wts["ffn_w_down"][0],
                                   lfg[0:1], lfb[0:1], first_row=first_row, name="ffn_fwd0")
    proj_b, bu, pre3, h3 = _sc_fwd(h2, wts["b_w_in"], b_conv, wts["b_w_out"], lmg[1:2], lmb[1:2], first_row=first_row)
    up1, act1, pre4, h4 = _ffn_fwd(h3, wts["ffn_w_up"][1], small["ffn_conv"][1], wts["ffn_w_down"][1],
                                   lfg[1:2], lfb[1:2], first_row=first_row, name="ffn_fwd1")
    dh4, loss_tile = _loss_head(h4, target)

    gw, gs = {}, {}
    alpha = ALPHA

    def ffn_backward(dh_out, pre, up, act, h_in, layer, tag):
        dpre, dg, db = _ln_bwd(dh_out, pre, lfg[layer:layer + 1], first_row=first_row, name="ln_bwd_ffn" + tag)
        dup, dcw = _ffn_bwd(dpre, up, wts["ffn_w_down"][layer], small["ffn_conv"][layer], name="ffn_bwd" + tag)
        dwd = _linear_dw(act, dpre, name="dw_down" + tag)
        dwu = _linear_dw(h_in, dup, name="dw_up" + tag)
        dh_in = _linear_dx(dup, wts["ffn_w_up"][layer], dpre, alpha=alpha, out_dtype=f32, name="dx_up" + tag)
        return dh_in, dwu, dwd, dcw[0:3], dg[0], db[0]

    dh3, dwu1, dwd1, dcf1, dlfg1, dlfb1 = ffn_backward(dh4, pre4, up1, act1, h3, 1, "1")

    dpre3, dlmg1, dlmb1 = _ln_bwd(dh3, pre3, lmg[1:2], first_row=first_row, name="ln_bwd_mix1")
    dbu = _linear_dx(dpre3, wts["b_w_out"], None, alpha=0.0, out_dtype=f32, name="dx_b_out")
    gw["b_w_out"] = [_linear_dw(bu, dpre3, name="dw_b_out")]
    dproj_b, dcb = _sc_bwd(dbu, proj_b, b_conv)
    gw["b_w_in"] = [_linear_dw(h2, dproj_b, name="dw_b_in")]
    dh2 = _linear_dx(dproj_b, wts["b_w_in"], dpre3, alpha=alpha, out_dtype=f32, name="dx_b_in")

    dh1, dwu0, dwd0, dcf0, dlfg0, dlfb0 = ffn_backward(dh2, pre2, up0, act0, h1, 0, "0")

    dpre1, dlmg0, dlmb0 = _ln_bwd(dh1, pre1, lmg[0:1], first_row=first_row, name="ln_bwd_mix0")
    dy = _linear_dx(dpre1, wts["a_w_out"], None, alpha=0.0, out_dtype=f32, name="dx_a_out")
    gw["a_w_out"] = [_linear_dw(y, dpre1, name="dw_a_out")]
    dq, dk, dv, dz, dg_, dbeta, dnw = _delta_bwd(dy, o, z, nw, q, k, v, g, beta, s_all, t_all, H=H)
    dproj_a, dca, dal, ddt = _gdn_in_bwd(dq, dk, dv, dz, dg_, dbeta, pre_a, raw, a_conv, alog, dtb,
                                         first_row=first_row, H=H)
    gw["a_w_in"] = [_linear_dw(h0, dproj_a, name="dw_a_in")]
    dh0 = _linear_dx(dproj_a, wts["a_w_in"], dpre1, alpha=alpha, out_dtype=f32, name="dx_a_in")

    gw["ffn_w_up"] = [dwu0, dwu1]
    gw["ffn_w_down"] = [dwd0, dwd1]
    gs["meta"] = dh0[first_row:PADF]
    gs["a_conv"] = dca[0:a_conv.shape[0]][None]
    gs["a_log"] = dal[0:1, 0:H]
    gs["a_dt_bias"] = ddt[0:1, 0:H]
    gs["a_norm"] = dnw[0:1]
    gs["b_conv"] = dcb[0:b_conv.shape[0]][None]
    gs["ln_mix_g"] = jnp.stack([dlmg0[0], dlmg1[0]])
    gs["ln_mix_b"] = jnp.stack([dlmb0[0], dlmb1[0]])
    gs["ffn_conv"] = jnp.stack([dcf0, dcf1])
    gs["ln_ffn_g"] = jnp.stack([dlfg0, dlfg1])
    gs["ln_ffn_b"] = jnp.stack([dlfb0, dlfb1])
    return loss_tile, dh0[PADF:], gw, gs


_BIG = ("a_w_in", "a_w_out", "b_w_in", "b_w_out", "ffn_w_up", "ffn_w_down")
_BIG_COL = ("a_w_in", "b_w_in", "ffn_w_up")
_SMALL = ("meta", "a_conv", "a_log", "a_dt_bias", "a_norm", "b_conv", "ln_mix_g", "ln_mix_b",
          "ffn_conv", "ln_ffn_g", "ln_ffn_b")
_SMALL_SHARDED = ("meta", "a_conv", "b_conv", "ffn_conv")
_ORDER = ("meta", "a_w_in", "a_conv", "a_log", "a_dt_bias", "a_norm", "a_w_out", "b_w_in", "b_conv", "b_w_out",
          "ln_mix_g", "ln_mix_b", "ffn_w_up", "ffn_conv", "ffn_w_down", "ln_ffn_g", "ln_ffn_b")


def _a_w_in_map(H):
    W4 = 4 * H * DH
    return [(0, W4, 0), (W4, W4 + H, W4), (W4 + H, W4 + 2 * H, W4 + LANE)], W4 + 2 * LANE


def kernel(x, meta, a_w_in, a_conv, a_log, a_dt_bias, a_norm, a_w_out, b_w_in, b_conv, b_w_out, ln_mix_g, ln_mix_b, ffn_w_up, ffn_conv, ffn_w_down, ln_ffn_g, ln_ffn_b, loss_target, m_meta, m_a_w_in, m_a_conv, m_a_log, m_a_dt_bias, m_a_norm, m_a_w_out, m_b_w_in, m_b_conv, m_b_w_out, m_ln_mix_g, m_ln_mix_b, m_ffn_w_up, m_ffn_conv, m_ffn_w_down, m_ln_ffn_g, m_ln_ffn_b, v_meta, v_a_w_in, v_a_conv, v_a_log, v_a_dt_bias, v_a_norm, v_a_w_out, v_b_w_in, v_b_conv, v_b_w_out, v_ln_mix_g, v_ln_mix_b, v_ffn_w_up, v_ffn_conv, v_ffn_w_down, v_ln_ffn_g, v_ln_ffn_b):
    wloc = dict(meta=meta, a_w_in=a_w_in, a_conv=a_conv, a_log=a_log, a_dt_bias=a_dt_bias, a_norm=a_norm,
                a_w_out=a_w_out, b_w_in=b_w_in, b_conv=b_conv, b_w_out=b_w_out, ln_mix_g=ln_mix_g, ln_mix_b=ln_mix_b,
                ffn_w_up=ffn_w_up, ffn_conv=ffn_conv, ffn_w_down=ffn_w_down, ln_ffn_g=ln_ffn_g, ln_ffn_b=ln_ffn_b)
    mloc = dict(meta=m_meta, a_w_in=m_a_w_in, a_conv=m_a_conv, a_log=m_a_log, a_dt_bias=m_a_dt_bias, a_norm=m_a_norm,
                a_w_out=m_a_w_out, b_w_in=m_b_w_in, b_conv=m_b_conv, b_w_out=m_b_w_out, ln_mix_g=m_ln_mix_g,
                ln_mix_b=m_ln_mix_b, ffn_w_up=m_ffn_w_up, ffn_conv=m_ffn_conv, ffn_w_down=m_ffn_w_down,
                ln_ffn_g=m_ln_ffn_g, ln_ffn_b=m_ln_ffn_b)
    vloc = dict(meta=v_meta, a_w_in=v_a_w_in, a_conv=v_a_conv, a_log=v_a_log, a_dt_bias=v_a_dt_bias, a_norm=v_a_norm,
                a_w_out=v_a_w_out, b_w_in=v_b_w_in, b_conv=v_b_conv, b_w_out=v_b_w_out, ln_mix_g=v_ln_mix_g,
                ln_mix_b=v_ln_mix_b, ffn_w_up=v_ffn_w_up, ffn_conv=v_ffn_conv, ffn_w_down=v_ffn_w_down,
                ln_ffn_g=v_ln_ffn_g, ln_ffn_b=v_ln_ffn_b)
    H = a_log.shape[-1]
    mx, my, mc = lax.axis_index("x"), lax.axis_index("y"), lax.axis_index("c")
    me = 4 * mx + 2 * my + mc

    gathered = dict(zip(_BIG, _all_gather([_bf(wloc[n]) for n in _BIG], name="gather_weights")))
    a_map, a_cols = _a_w_in_map(H)
    col_maps = {"a_w_in": (a_map, a_cols)}
    for n in ("b_w_in", "ffn_w_up"):
        ncols = N_DEV * wloc[n].shape[-1]
        col_maps[n] = ([(0, ncols, 0)], ncols)
    wts = {}
    for n in _BIG:
        if n in _BIG_COL:
            wts[n] = _assemble_cols(gathered[n], *col_maps[n], name="assemble_" + n)
        else:
            wts[n] = _rows_full(gathered[n])
    for n in ("a_w_in", "a_w_out", "b_w_in", "b_w_out"):
        wts[n] = wts[n][0]

    sm_sh = [wloc[n] for n in _SMALL_SHARDED]
    sbuf, soffs = _pack_small(sm_sh, 128)
    sg = _all_gather([sbuf], name="gather_small")[0]
    small = {n: wloc[n] for n in _SMALL}
    for n, off in zip(_SMALL_SHARDED, soffs):
        sh = wloc[n].shape
        parts = jnp.stack([_unpack_small(sg[d], off, sh) for d in range(N_DEV)])
        nd = len(sh)
        small[n] = jnp.transpose(parts, tuple(range(1, nd)) + (0, nd)).reshape(sh[:-1] + (N_DEV * sh[-1],))

    loss_tile, grad_x, gw, gs = _local_step(x[0], loss_target[0], small["meta"], wts, small)

    s32, s16 = {}, {}
    for n in _BIG:
        if n in _BIG_COL:
            s32[n], s16[n] = _split_cols(gw[n], col_maps[n][0], wloc[n].shape[-1], name="split_" + n)
        else:
            s32[n], s16[n] = _split_rows(gw[n], wloc[n].shape[-2], name="split_" + n)
    recv1 = dict(zip(_BIG, _exchange([s16[n] for n in _BIG], 4, lambda j: 2 * j + (1 - lax.axis_index("c")),
                                     ["c"] * 4, name="grad_to_sibling")))
    chip = 2 * mx + my
    others = [2 * (1 - mx) + my, 2 * mx + (1 - my), 2 * (1 - mx) + (1 - my)]
    sidx = jnp.stack([2 * o + mc for o in others]).astype(jnp.int32)
    ridx = jnp.stack(others).astype(jnp.int32)
    to_send = [_pair_add(s32[n], recv1[n], sidx, ridx, name="pair_add_" + n) for n in _BIG]
    recv2 = dict(zip(_BIG, _exchange(to_send, 3, lambda j: j, ["x", "y", "xy"], name="grad_to_chips")))
    own = jnp.stack([me, chip]).astype(jnp.int32)
    big_out = {n: _adamw_shard(s32[n], recv1[n], recv2[n], wloc[n], mloc[n], vloc[n], own, name="adamw_" + n)
               for n in _BIG}

    names = list(_SMALL)
    pbuf, poffs = _pack_small([gs[n] for n in names] + [loss_tile[0:1, 0:1]], 1024)
    psum = _sum_devices(_all_gather([pbuf], name="gather_small_grads")[0])
    loss = psum[poffs[-1][0], 0]
    g_small = {}
    for n, off in zip(names, poffs[:-1]):
        full_shape = gs[n].shape
        gfull = _unpack_small(psum, off, full_shape)
        if n in _SMALL_SHARDED:
            ns = wloc[n].shape[-1]
            gfull = lax.dynamic_slice_in_dim(gfull, me * ns, ns, axis=gfull.ndim - 1)
        g_small[n] = gfull.reshape(wloc[n].shape)
    gbuf, aoffs = _pack_small([g_small[n] for n in names], 128)
    wbuf, _ = _pack_small([wloc[n] for n in names], 128)
    mbuf, _ = _pack_small([mloc[n] for n in names], 128)
    vbuf, _ = _pack_small([vloc[n] for n in names], 128)
    _, d_s, m_s, v_s = _adamw([gbuf], wbuf, mbuf, vbuf, name="adamw_small")

    grads, deltas, new_m, new_v = {}, {}, {}, {}
    for n in _BIG:
        grads[n], deltas[n], new_m[n], new_v[n] = big_out[n]
    for n, off in zip(names, aoffs):
        sh = wloc[n].shape
        grads[n] = g_small[n]
        deltas[n], new_m[n], new_v[n] = (_unpack_small(b_, off, sh) for b_ in (d_s, m_s, v_s))
    return (loss, grad_x[None], *[grads[n] for n in _ORDER], *[deltas[n] for n in _ORDER],
            *[new_m[n] for n in _ORDER], *[new_v[n] for n in _ORDER])
```

```python
import math

import jax
import jax.numpy as jnp
from jax import lax
from jax.experimental import pallas as pl
from jax.experimental.pallas import tpu as pltpu

f32, bf16 = jnp.float32, jnp.bfloat16
S = jax.ShapeDtypeStruct
HI = lax.Precision.HIGHEST
HI3 = lax.Precision.HIGH
MESH = pl.DeviceIdType.MESH

V7X_VMEM_LIMIT = 56 * 1024 * 1024
LANE = 128
DH = 128
CH = 64
PADF = 256
TM = 256
TMM = 768
N_DEV = 8
BWD_HEAD_GROUP = 8

DEPTH = 2
ALPHA = (2.0 * DEPTH) ** 0.25
LN_EPS = 1e-5
RMS_EPS = 1e-6
L2_EPS = 1e-6
ADAM_LR, ADAM_B1, ADAM_B2, ADAM_EPS, ADAM_WD, ADAM_STEP = 0.001, 0.9, 0.999, 1e-08, 0.01, 10


def _cp(**kw):
    return pltpu.CompilerParams(vmem_limit_bytes=V7X_VMEM_LIMIT, **kw)


def _bf(x):
    return x.astype(bf16)


def _dot(a, b, precision=None):
    return jnp.dot(a, b, preferred_element_type=f32, precision=precision)


def _dot_nt(a, b):
    return lax.dot_general(a, b, (((1,), (1,)), ((), ())), preferred_element_type=f32)


def _dot_tn(a, b):
    return lax.dot_general(a, b, (((0,), (0,)), ((), ())), preferred_element_type=f32)


def _sigmoid(x):
    return 1.0 / (1.0 + jnp.exp(-x))


def _load_once(pairs, sem):
    @pl.when(pl.program_id(0) == 0)
    def _():
        cps = [pltpu.make_async_copy(src, dst, sem.at[n]) for n, (src, dst) in enumerate(pairs)]
        for c in cps:
            c.start()
        for c in cps:
            c.wait()


def _row_ids(i, tm, width):
    return i * tm + lax.broadcasted_iota(jnp.int32, (tm, width), 0)


def _ln_fwd(pre, g, b, rows, first_row):
    mu = jnp.mean(pre, axis=-1, keepdims=True)
    xc = pre - mu
    var = jnp.mean(xc * xc, axis=-1, keepdims=True)
    y = xc * lax.rsqrt(var + LN_EPS) * g + b
    return jnp.where(rows >= first_row, y, 0.0)


ANY = pl.BlockSpec(memory_space=pl.ANY)


def _linear_dw(x, dy, *, name):
    L, K = x.shape
    N = dy.shape[1]
    tm = TMM if L % TMM == 0 else TM
    tn = LANE
    for d in range(N // LANE, 0, -1):
        if (N // LANE) % d == 0 and K * d * LANE * 4 <= 9 * 1024 * 1024:
            tn = d * LANE
            break

    def body(x_ref, dy_ref, o_ref):
        @pl.when(pl.program_id(1) == 0)
        def _():
            o_ref[...] = jnp.zeros_like(o_ref)
        o_ref[...] += _dot_tn(_bf(x_ref[...]), _bf(dy_ref[...]))

    return pl.pallas_call(
        body, name=name, grid=(N // tn, L // tm), out_shape=S((K, N), f32),
        in_specs=[pl.BlockSpec((tm, K), lambda j, i: (i, 0)), pl.BlockSpec((tm, tn), lambda j, i: (i, j))],
        out_specs=pl.BlockSpec((K, tn), lambda j, i: (0, j)),
        compiler_params=_cp(dimension_semantics=("arbitrary", "arbitrary")))(x, dy)


def _linear_dx(dy, w, res, *, alpha, out_dtype, name):
    L, N = dy.shape
    K = w.shape[0]
    tm = TM
    has_res = res is not None

    def body(*refs):
        if has_res:
            dy_ref, w_hbm, res_ref, o_ref, w_vmem, sem = refs
        else:
            dy_ref, w_hbm, o_ref, w_vmem, sem = refs
        _load_once([(w_hbm, w_vmem)], sem)
        acc = _dot_nt(_bf(dy_ref[...]), w_vmem[...])
        if has_res:
            acc = acc + alpha * res_ref[...]
        o_ref[...] = acc.astype(out_dtype)

    in_specs = [pl.BlockSpec((tm, N), lambda i: (i, 0)), ANY]
    args = [dy, w]
    if has_res:
        in_specs.append(pl.BlockSpec((tm, K), lambda i: (i, 0)))
        args.append(res)
    return pl.pallas_call(
        body, name=name, grid=(L // tm,), out_shape=S((L, K), out_dtype),
        in_specs=in_specs, out_specs=pl.BlockSpec((tm, K), lambda i: (i, 0)),
        scratch_shapes=[pltpu.VMEM((K, N), w.dtype), pltpu.SemaphoreType.DMA((1,))],
        compiler_params=_cp(dimension_semantics=("arbitrary",)))(*args)


def _out_res_ln(y, w, h, g, b, *, first_row, name):
    L, K = y.shape
    D = w.shape[1]
    tm = TM
    alpha = ALPHA

    def body(y_ref, w_hbm, h_ref, g_ref, b_ref, pre_ref, out_ref, w_vmem, sem):
        _load_once([(w_hbm, w_vmem)], sem)
        pre = alpha * h_ref[...] + _dot(_bf(y_ref[...]), w_vmem[...])
        pre_ref[...] = pre
        out_ref[...] = _ln_fwd(pre, g_ref[...], b_ref[...], _row_ids(pl.program_id(0), tm, D), first_row)

    row = lambda i: (i, 0)
    fix = lambda i: (0, 0)
    return pl.pallas_call(
        body, name=name, grid=(L // tm,), out_shape=(S((L, D), f32), S((L, D), f32)),
        in_specs=[pl.BlockSpec((tm, K), row), ANY, pl.BlockSpec((tm, D), row),
                  pl.BlockSpec((1, D), fix), pl.BlockSpec((1, D), fix)],
        out_specs=(pl.BlockSpec((tm, D), row), pl.BlockSpec((tm, D), row)),
        scratch_shapes=[pltpu.VMEM((K, D), w.dtype), pltpu.SemaphoreType.DMA((1,))],
        compiler_params=_cp(dimension_semantics=("arbitrary",)))(y, w, h, g, b)


def _gdn_in_fwd(h, w_full, conv_w, alog, dtb, *, first_row, H):
    L, D = h.shape
    W = H * DH
    NW = w_full.shape[1]
    KW = conv_w.shape[0]
    tm = TM

    def body(h_ref, w_hbm, cw_ref, alog_ref, dtb_ref,
             pre_ref, z_ref, raw_ref, q_ref, k_ref, v_ref, beta_ref, g_ref,
             w_vmem, scr, carry, sem):
        i = pl.program_id(0)
        _load_once([(w_hbm, w_vmem)], sem)

        @pl.when(i == 0)
        def _():
            carry[...] = jnp.zeros_like(carry)

        hb = _bf(h_ref[...])
        outs = (q_ref, k_ref, v_ref)
        for s in range(3):
            pre = _dot(hb, w_vmem[:, s * W:(s + 1) * W])
            pre_ref[:, s * W:(s + 1) * W] = pre
            scr[0:8, :] = carry[s]
            scr[8:8 + tm, :] = pre
            carry[s] = pre[tm - 8:tm, :]
            c = jnp.zeros((tm, W), f32)
            for j in range(KW):
                c = c + cw_ref[j:j + 1, s * W:(s + 1) * W] * scr[pl.ds(8 - (KW - 1) + j, tm), :]
            sl = c * _sigmoid(c)
            if s < 2:
                scale = DH ** -0.5 if s == 0 else 1.0
                for hh in range(H):
                    seg = sl[:, hh * DH:(hh + 1) * DH]
                    r = lax.rsqrt(jnp.sum(seg * seg, axis=-1, keepdims=True) + L2_EPS)
                    outs[s][:, hh * DH:(hh + 1) * DH] = seg * (r * scale)
            else:
                v_ref[...] = sl
        z_ref[...] = _dot(hb, w_vmem[:, 3 * W:4 * W])
        raw = _dot(hb, w_vmem[:, 4 * W:4 * W + 2 * LANE])
        raw_ref[...] = raw
        ok = (_row_ids(i, tm, LANE) >= first_row) & (lax.broadcasted_iota(jnp.int32, (tm, LANE), 1) < H)
        beta_ref[...] = jnp.where(ok, _sigmoid(raw[:, :LANE]), 0.0)
        a = raw[:, LANE:] + dtb_ref[...]
        sp = jnp.maximum(a, 0.0) + jnp.log(1.0 + jnp.exp(-jnp.abs(a)))
        gv = jnp.where(ok, -jnp.exp(alog_ref[...]) * sp, 0.0)
        g_ref[...] = _dot(_chunk_tri(tm, lower=True), gv, HI)

    row = lambda i: (i, 0)
    fix = lambda i: (0, 0)
    out_shape = (S((L, 3 * W), f32), S((L, W), f32), S((L, 2 * LANE), f32),
                 S((L, W), f32), S((L, W), f32), S((L, W), f32), S((L, LANE), f32), S((L, LANE), f32))
    out_specs = (pl.BlockSpec((tm, 3 * W), row), pl.BlockSpec((tm, W), row), pl.BlockSpec((tm, 2 * LANE), row),
                 pl.BlockSpec((tm, W), row), pl.BlockSpec((tm, W), row), pl.BlockSpec((tm, W), row),
                 pl.BlockSpec((tm, LANE), row), pl.BlockSpec((tm, LANE), row))
    return pl.pallas_call(
        body, name="gdn_in_fwd", grid=(L // tm,), out_shape=out_shape,
        in_specs=[pl.BlockSpec((tm, D), row), ANY, pl.BlockSpec((KW, 3 * W), fix),
                  pl.BlockSpec((1, LANE), fix), pl.BlockSpec((1, LANE), fix)],
        out_specs=out_specs,
        scratch_shapes=[pltpu.VMEM((D, NW), w_full.dtype), pltpu.VMEM((tm + 8, W), f32), pltpu.VMEM((3, 8, W), f32),
                        pltpu.SemaphoreType.DMA((1,))],
        compiler_params=_cp(dimension_semantics=("arbitrary",)))(h, w_full, conv_w, alog, dtb)


def _gdn_in_bwd(dq, dk, dv, dz, dg, dbeta, pre, raw, conv_w, alog, dtb, *, first_row, H):
    L = dq.shape[0]
    W = H * DH
    KW = conv_w.shape[0]
    tm = TM
    nb = L // tm
    NW = 4 * W + 2 * LANE

    def body(dq_ref, dk_ref, dv_ref, dz_ref, dg_ref, dbeta_ref, pre_ref, hq_ref, hk_ref, hv_ref, raw_ref,
             cw_ref, alog_ref, dtb_ref, dproj_ref, dcw_ref, dal_ref, ddt_ref, scrx, scrd, carry, tmp):
        i = pl.program_id(0)
        blk = nb - 1 - i

        @pl.when(i == 0)
        def _():
            carry[...] = jnp.zeros_like(carry)
            dcw_ref[...] = jnp.zeros_like(dcw_ref)
            dal_ref[...] = jnp.zeros_like(dal_ref)
            ddt_ref[...] = jnp.zeros_like(ddt_ref)

        halos = (hq_ref, hk_ref, hv_ref)
        douts = (dq_ref, dk_ref, dv_ref)
        for s in range(3):
            sec = slice(s * W, (s + 1) * W)
            scrx[0:8, :] = jnp.where(blk > 0, halos[s][...], 0.0)
            scrx[8:8 + tm, :] = pre_ref[:, sec]
            c = jnp.zeros((tm, W), f32)
            for j in range(KW):
                c = c + cw_ref[j:j + 1, sec] * scrx[pl.ds(8 - (KW - 1) + j, tm), :]
            sig = _sigmoid(c)
            sl = c * sig
            if s < 2:
                scale = DH ** -0.5 if s == 0 else 1.0
                for hh in range(H):
                    hs = slice(hh * DH, (hh + 1) * DH)
                    seg = sl[:, hs]
                    r = lax.rsqrt(jnp.sum(seg * seg, axis=-1, keepdims=True) + L2_EPS)
                    n = seg * r
                    dqs = douts[s][:, hs]
                    tmp[:, hs] = (scale * r) * (dqs - n * jnp.sum(n * dqs, axis=-1, keepdims=True))
                dsl = tmp[...]
            else:
                dsl = dv_ref[...]
            dc = dsl * (sig * (1.0 + c * (1.0 - sig)))
            scrd[0:tm, :] = dc
            scrd[tm:tm + 8, :] = carry[s]
            carry[s] = dc[0:8, :]
            dx = jnp.zeros((tm, W), f32)
            for j in range(KW):
                dx = dx + cw_ref[j:j + 1, sec] * scrd[pl.ds(KW - 1 - j, tm), :]
                dcw_ref[j:j + 1, sec] += jnp.sum(dc * scrx[pl.ds(8 - (KW - 1) + j, tm), :], axis=0, keepdims=True)
            dproj_ref[:, sec] = _bf(dx)
        dproj_ref[:, 3 * W:4 * W] = _bf(dz_ref[...])
        raw_v = raw_ref[...]
        ok = (_row_ids(blk, tm, LANE) >= first_row) & (lax.broadcasted_iota(jnp.int32, (tm, LANE), 1) < H)
        beta = _sigmoid(raw_v[:, :LANE])
        dbraw = jnp.where(ok, dbeta_ref[...] * beta * (1.0 - beta), 0.0)
        a = raw_v[:, LANE:] + dtb_ref[...]
        sp = jnp.maximum(a, 0.0) + jnp.log(1.0 + jnp.exp(-jnp.abs(a)))
        nea = -jnp.exp(alog_ref[...])
        dgm = jnp.where(ok, _dot(_chunk_tri(tm, lower=False), dg_ref[...], HI), 0.0)
        daraw = dgm * nea * _sigmoid(a)
        dal_ref[0:1, :] += jnp.sum(dgm * nea * sp, axis=0, keepdims=True)
        ddt_ref[0:1, :] += jnp.sum(daraw, axis=0, keepdims=True)
        dproj_ref[:, 4 * W:4 * W + LANE] = _bf(dbraw)
        dproj_ref[:, 4 * W + LANE:4 * W + 2 * LANE] = _bf(daraw)

    rev = lambda i: (nb - 1 - i, 0)
    fix = lambda i: (0, 0)

    def halo(col):
        return pl.BlockSpec((8, W), lambda i: (jnp.maximum((nb - 1 - i) * (tm // 8) - 1, 0), col))

    return pl.pallas_call(
        body, name="gdn_in_bwd", grid=(nb,),
        out_shape=(S((L, NW), bf16), S((8, 3 * W), f32), S((8, LANE), f32), S((8, LANE), f32)),
        in_specs=[pl.BlockSpec((tm, W), rev)] * 4 + [pl.BlockSpec((tm, LANE), rev)] * 2
        + [pl.BlockSpec((tm, 3 * W), rev), halo(0), halo(1), halo(2), pl.BlockSpec((tm, 2 * LANE), rev),
           pl.BlockSpec((KW, 3 * W), fix), pl.BlockSpec((1, LANE), fix), pl.BlockSpec((1, LANE), fix)],
        out_specs=(pl.BlockSpec((tm, NW), rev), pl.BlockSpec((8, 3 * W), fix),
                   pl.BlockSpec((8, LANE), fix), pl.BlockSpec((8, LANE), fix)),
        scratch_shapes=[pltpu.VMEM((tm + 8, W), f32), pltpu.VMEM((tm + 8, W), f32), pltpu.VMEM((3, 8, W), f32),
                        pltpu.VMEM((tm, W), f32)],
        compiler_params=_cp(dimension_semantics=("arbitrary",)))(
            dq, dk, dv, dz, dg, dbeta, pre, pre, pre, pre, raw, conv_w, alog, dtb)


def _chunk_tri(n, lower):
    i = lax.broadcasted_iota(jnp.int32, (n, n), 0)
    j = lax.broadcasted_iota(jnp.int32, (n, n), 1)
    sh = int(math.log2(CH))
    same = lax.shift_right_logical(i, sh) == lax.shift_right_logical(j, sh)
    return (same & ((i >= j) if lower else (j >= i))).astype(f32)


def _tri_inv_many(ms, eye):
    ts = [eye - m for m in ms]
    ps = list(ms)
    for _ in range(int(math.log2(CH)) - 1):
        ps = [_dot(p, p, HI3) for p in ps]
        ts = [t + _dot(t, p, HI3) for t, p in zip(ts, ps)]
    return ts


def _chunk_local(q, k, v, gcol, grow, glast, bcol, ii, jj):
    dec = jnp.where(ii >= jj, jnp.exp(jnp.minimum(gcol - grow, 0.0)), 0.0)
    eg = jnp.exp(gcol)
    kb = k * bcol
    kbg = kb * eg
    vb = v * bcol
    qt = q * eg
    kt = k * jnp.exp(glast - gcol)
    kk = _dot_nt(_bf(kb), _bf(k))
    qk = _dot_nt(_bf(q), _bf(k))
    return dec, eg, kb, kbg, vb, qt, kt, kk, qk


def _delta_fwd(q, k, v, g, beta, z, nw, *, H):
    L = q.shape[0]
    W = H * DH
    rb = TM
    nc = rb // CH
    nblk = L // rb

    def body(q_ref, k_ref, v_ref, g_ref, b_ref, z_ref, nw_ref, o_ref, y_ref, s_out, t_out,
             s_scr, u_s, w_s, qt_s, kt_s, at_s):
        @pl.when(pl.program_id(0) == 0)
        def _():
            s_scr[...] = jnp.zeros_like(s_scr)

        ii = lax.broadcasted_iota(jnp.int32, (CH, CH), 0)
        jj = lax.broadcasted_iota(jnp.int32, (CH, CH), 1)
        eye = (ii == jj).astype(f32)
        nwv = nw_ref[...]

        heads = range(H)
        hsl = [slice(hh * DH, (hh + 1) * DH) for hh in heads]

        def local(c, carry):
            r0 = pl.multiple_of(c * CH, CH)
            rows = pl.ds(r0, CH)
            gam = g_ref[rows, :]
            gam_t = gam.T
            bb = b_ref[rows, :]
            loc = [_chunk_local(q_ref[rows, hsl[hh]], k_ref[rows, hsl[hh]], v_ref[rows, hsl[hh]],
                                gam[:, hh:hh + 1], gam_t[hh:hh + 1, :], gam[CH - 1:CH, hh:hh + 1], bb[:, hh:hh + 1],
                                ii, jj) for hh in heads]
            ts = _tri_inv_many([jnp.where(ii > jj, l[7] * l[0], 0.0) for l in loc], eye)
            us = [_dot(t, l[4], HI3) for t, l in zip(ts, loc)]
            ws = [_dot(t, l[3], HI3) for t, l in zip(ts, loc)]
            for hh in heads:
                l = loc[hh]
                t_out[c, hh] = ts[hh]
                u_s[c, hh] = us[hh]
                w_s[c, hh] = _bf(ws[hh])
                qt_s[c, hh] = _bf(l[5])
                kt_s[c, hh] = _bf(l[6])
                at_s[c, hh] = _bf(l[8] * l[0])
            return carry

        lax.fori_loop(0, nc, local, 0)

        def scan(c, carry):
            r0 = pl.multiple_of(c * CH, CH)
            rows = pl.ds(r0, CH)
            gam = g_ref[rows, :]
            st = [s_scr[hh] for hh in heads]
            stb = [_bf(s) for s in st]
            vnb = [_bf(u_s[c, hh] - _dot(w_s[c, hh], stb[hh])) for hh in heads]
            snew = [st[hh] * jnp.exp(gam[CH - 1:CH, hh:hh + 1]) + _dot_tn(kt_s[c, hh], vnb[hh]) for hh in heads]
            os_ = [_dot(qt_s[c, hh], stb[hh]) + _dot(at_s[c, hh], vnb[hh]) for hh in heads]
            for hh in heads:
                o = os_[hh]
                s_out[c, hh] = st[hh]
                s_scr[hh] = snew[hh]
                o_ref[rows, hsl[hh]] = o
                on = o * lax.rsqrt(jnp.mean(o * o, axis=-1, keepdims=True) + RMS_EPS) * nwv
                zh = z_ref[rows, hsl[hh]]
                y_ref[rows, hsl[hh]] = _bf(on * (zh * _sigmoid(zh)))
            return carry

        lax.fori_loop(0, nc, scan, 0)

    row = lambda i: (i, 0)
    fix = lambda i: (0, 0)
    return pl.pallas_call(
        body, name="delta_fwd", grid=(nblk,),
        out_shape=(S((L, W), f32), S((L, W), bf16), S((L // CH, H, DH, DH), f32), S((L // CH, H, CH, CH), f32)),
        in_specs=[pl.BlockSpec((rb, W), row)] * 3 + [pl.BlockSpec((rb, LANE), row)] * 2
        + [pl.BlockSpec((rb, W), row), pl.BlockSpec((1, DH), fix)],
        out_specs=(pl.BlockSpec((rb, W), row), pl.BlockSpec((rb, W), row),
                   pl.BlockSpec((nc, H, DH, DH), lambda i: (i, 0, 0, 0)),
                   pl.BlockSpec((nc, H, CH, CH), lambda i: (i, 0, 0, 0))),
        scratch_shapes=[pltpu.VMEM((H, DH, DH), f32), pltpu.VMEM((nc, H, CH, DH), f32)]
        + [pltpu.VMEM((nc, H, CH, DH), bf16)] * 3 + [pltpu.VMEM((nc, H, CH, CH), bf16)],
        compiler_params=_cp(dimension_semantics=("arbitrary",)))(q, k, v, g, beta, z, nw)


def _delta_bwd(dy, o, z, nw, q, k, v, g, beta, s_all, t_all, *, H):
    L = q.shape[0]
    W = H * DH
    rb = TM
    nc = rb // CH
    nblk = L // rb

    def body(dy_ref, o_ref, z_ref, nw_ref, q_ref, k_ref, v_ref, g_ref, b_ref, s_ref, t_ref,
             dq_ref, dk_ref, dv_ref, dz_ref, dg_ref, db_ref, dnw_ref, ds_scr):
        @pl.when(pl.program_id(0) == 0)
        def _():
            ds_scr[...] = jnp.zeros_like(ds_scr)
            dnw_ref[...] = jnp.zeros_like(dnw_ref)

        ii = lax.broadcasted_iota(jnp.int32, (CH, CH), 0)
        jj = lax.broadcasted_iota(jnp.int32, (CH, CH), 1)
        lane = lax.broadcasted_iota(jnp.int32, (CH, LANE), 1)
        last_row = lax.broadcasted_iota(jnp.int32, (CH, 1), 0) == CH - 1
        nwv = nw_ref[...]

        def chunk(cc, carry):
            c = nc - 1 - cc
            r0 = pl.multiple_of(c * CH, CH)
            rows = pl.ds(r0, CH)
            gam = g_ref[rows, :]
            gam_t = gam.T
            bb = b_ref[rows, :]

            def head(hh):
                hs = slice(hh * DH, (hh + 1) * DH)
                gcol, grow, glast = gam[:, hh:hh + 1], gam_t[hh:hh + 1, :], gam[CH - 1:CH, hh:hh + 1]
                bcol = bb[:, hh:hh + 1]
                qh, kh, vh = q_ref[rows, hs], k_ref[rows, hs], v_ref[rows, hs]
                oh, zh, dyh = o_ref[rows, hs], z_ref[rows, hs], dy_ref[rows, hs]
                t = t_ref[c, hh]
                st = s_ref[c, hh]
                dsn = ds_scr[hh]
                rms = lax.rsqrt(jnp.mean(oh * oh, axis=-1, keepdims=True) + RMS_EPS)
                on = oh * rms
                sig = _sigmoid(zh)
                sz = zh * sig
                dz_ref[rows, hs] = dyh * on * nwv * (sig * (1.0 + zh * (1.0 - sig)))
                dnw = jnp.sum(dyh * on * sz, axis=0, keepdims=True)
                don = dyh * nwv * sz
                do = rms * (don - on * jnp.mean(don * on, axis=-1, keepdims=True))
                dec, eg, kb, kbg, vb, qt, kt, kk, qk = _chunk_local(qh, kh, vh, gcol, grow, glast, bcol, ii, jj)
                stb, dsnb, dob, tb = _bf(st), _bf(dsn), _bf(do), _bf(t)
                u = _dot(t, vb, HI3)
                w = _dot(t, kbg, HI3)
                dqt = _dot_nt(dob, stb)
                ds_new = _dot_tn(_bf(qt), dob)
                yield
                mm = jnp.where(ii > jj, kk * dec, 0.0)
                attn = qk * dec
                wb = _bf(w)
                vn = u - _dot(wb, stb)
                dvn = _dot_tn(_bf(attn), dob) + _dot(_bf(kt), dsnb)
                egl = jnp.exp(glast)
                ekt = jnp.exp(glast - gcol)
                yield
                vnb, dvnb = _bf(vn), _bf(dvn)
                dattn = jnp.where(ii >= jj, _dot_nt(dob, vnb), 0.0)
                dkt = _dot_nt(vnb, dsnb)
                ds_scr[hh] = ds_new + egl * dsn - _dot_tn(wb, dvnb)
                dw = -_dot_nt(dvnb, stb)
                dvb = _dot_tn(tb, dvnb)
                dv_ref[rows, hs] = dvb * bcol
                dt_u = _dot_nt(dvnb, _bf(vb))
                dglast = egl * jnp.sum(jnp.sum(dsn * st, axis=0, keepdims=True), axis=1, keepdims=True)
                yield
                dwb = _bf(dw)
                dkbg = _dot_tn(tb, dwb)
                dt = dt_u + _dot_nt(dwb, _bf(kbg))
                yield
                x = _dot_nt(_bf(dt), tb)
                yield
                dm = jnp.where(ii > jj, -_dot_tn(tb, _bf(x)), 0.0)
                dkk = dm * dec
                dqk = dattn * dec
                e = dm * mm + dattn * attn
                dgam = jnp.sum(e, axis=1, keepdims=True) - jnp.sum(e.T, axis=1, keepdims=True)
                dkkb, dqkb, kbf = _bf(dkk), _bf(dqk), _bf(kh)
                dkb = _dot(dkkb, kbf) + dkbg * eg
                dk_ref[rows, hs] = _dot_tn(dkkb, _bf(kb)) + _dot_tn(dqkb, _bf(qh)) + dkt * ekt + dkb * bcol
                dq_ref[rows, hs] = _dot(dqkb, kbf) + dqt * eg
                yield
                dktkt = dkt * kt
                dgam = dgam + jnp.sum(dqt * qt - dktkt + dkbg * kbg, axis=1, keepdims=True)
                dglast = dglast + jnp.sum(jnp.sum(dktkt, axis=0, keepdims=True), axis=1, keepdims=True)
                dgam = dgam + jnp.where(last_row, dglast, 0.0)
                dbeta = jnp.sum(dkb * kh + dvb * vh, axis=1, keepdims=True)
                return dgam, dbeta, dnw

            res = [None] * H
            for h0 in range(0, H, BWD_HEAD_GROUP):
                group = range(h0, min(h0 + BWD_HEAD_GROUP, H))
                gens = {hh: head(hh) for hh in group}
                while any(res[hh] is None for hh in group):
                    for hh in group:
                        try:
                            next(gens[hh])
                        except StopIteration as stop:
                            res[hh] = stop.value
            dgam_all = jnp.zeros((CH, LANE), f32)
            dbeta_all = jnp.zeros((CH, LANE), f32)
            dnw_acc = jnp.zeros((1, DH), f32)
            for hh in range(H):
                dgam, dbeta, dnw = res[hh]
                dgam_all = dgam_all + jnp.where(lane == hh, dgam, 0.0)
                dbeta_all = dbeta_all + jnp.where(lane == hh, dbeta, 0.0)
                dnw_acc = dnw_acc + dnw
            dg_ref[rows, :] = dgam_all
            db_ref[rows, :] = dbeta_all
            dnw_ref[0:1, :] += dnw_acc
            return carry

        lax.fori_loop(0, nc, chunk, 0)

    rev = lambda i: (nblk - 1 - i, 0)
    rev4 = lambda i: (nblk - 1 - i, 0, 0, 0)
    fix = lambda i: (0, 0)
    wide = pl.BlockSpec((rb, W), rev)
    thin = pl.BlockSpec((rb, LANE), rev)
    return pl.pallas_call(
        body, name="delta_bwd", grid=(nblk,),
        out_shape=(S((L, W), f32),) * 4 + (S((L, LANE), f32),) * 2 + (S((8, DH), f32),),
        in_specs=[wide, wide, wide, pl.BlockSpec((1, DH), fix), wide, wide, wide, thin, thin,
                  pl.BlockSpec((nc, H, DH, DH), rev4), pl.BlockSpec((nc, H, CH, CH), rev4)],
        out_specs=(wide,) * 4 + (thin, thin, pl.BlockSpec((8, DH), fix)),
        scratch_shapes=[pltpu.VMEM((H, DH, DH), f32)],
        compiler_params=_cp(dimension_semantics=("arbitrary",)))(dy, o, z, nw, q, k, v, g, beta, s_all, t_all)


def _sc_fwd(h, w_in, conv_w, w_out, g, b, *, first_row):
    L, D = h.shape
    W = w_out.shape[0]
    KW = conv_w.shape[0]
    tm = TM
    alpha = ALPHA

    def body(h_ref, win_hbm, cw_ref, wout_hbm, g_ref, b_ref, proj_ref, bu_ref, pre_ref, out_ref,
             win, wout, scr, carry, sem):
        i = pl.program_id(0)
        _load_once([(win_hbm, win), (wout_hbm, wout)], sem)

        @pl.when(i == 0)
        def _():
            carry[...] = jnp.zeros_like(carry)

        hv = h_ref[...]
        hb = _bf(hv)
        bg = _dot(hb, win[:, 0:W])
        cg = _dot(hb, win[:, W:2 * W])
        xv = _dot(hb, win[:, 2 * W:3 * W])
        proj_ref[:, 0:W] = bg
        proj_ref[:, W:2 * W] = cg
        proj_ref[:, 2 * W:3 * W] = xv
        p = cg * xv
        scr[0:8, :] = carry[...]
        scr[8:8 + tm, :] = p
        carry[...] = p[tm - 8:tm, :]
        u = jnp.zeros((tm, W), f32)
        for j in range(KW):
            u = u + cw_ref[j:j + 1, :] * scr[pl.ds(8 - (KW - 1) + j, tm), :]
        bu = _bf(bg * u)
        bu_ref[...] = bu
        pre = alpha * hv + _dot(bu, wout[...])
        pre_ref[...] = pre
        out_ref[...] = _ln_fwd(pre, g_ref[...], b_ref[...], _row_ids(i, tm, D), first_row)

    row = lambda i: (i, 0)
    fix = lambda i: (0, 0)
    return pl.pallas_call(
        body, name="sc_fwd", grid=(L // tm,),
        out_shape=(S((L, 3 * W), f32), S((L, W), bf16), S((L, D), f32), S((L, D), f32)),
        in_specs=[pl.BlockSpec((tm, D), row), ANY, pl.BlockSpec((KW, W), fix), ANY,
                  pl.BlockSpec((1, D), fix), pl.BlockSpec((1, D), fix)],
        out_specs=(pl.BlockSpec((tm, 3 * W), row), pl.BlockSpec((tm, W), row),
                   pl.BlockSpec((tm, D), row), pl.BlockSpec((tm, D), row)),
        scratch_shapes=[pltpu.VMEM((D, 3 * W), w_in.dtype), pltpu.VMEM((W, D), w_out.dtype), pltpu.VMEM((tm + 8, W), f32),
                        pltpu.VMEM((8, W), f32), pltpu.SemaphoreType.DMA((2,))],
        compiler_params=_cp(dimension_semantics=("arbitrary",)))(h, w_in, conv_w, w_out, g, b)


def _sc_bwd(dbu, proj, conv_w):
    L, W = dbu.shape
    KW = conv_w.shape[0]
    tm = TM
    nb = L // tm

    def body(dbu_ref, proj_ref, hc_ref, hx_ref, cw_ref, dproj_ref, dcw_ref, scrx, scrd, carry):
        i = pl.program_id(0)
        blk = nb - 1 - i

        @pl.when(i == 0)
        def _():
            carry[...] = jnp.zeros_like(carry)
            dcw_ref[...] = jnp.zeros_like(dcw_ref)

        bg, cg, xv = proj_ref[:, 0:W], proj_ref[:, W:2 * W], proj_ref[:, 2 * W:3 * W]
        scrx[0:8, :] = jnp.where(blk > 0, hc_ref[...] * hx_ref[...], 0.0)
        scrx[8:8 + tm, :] = cg * xv
        u = jnp.zeros((tm, W), f32)
        for j in range(KW):
            u = u + cw_ref[j:j + 1, :] * scrx[pl.ds(8 - (KW - 1) + j, tm), :]
        d = dbu_ref[...]
        dproj_ref[:, 0:W] = _bf(d * u)
        du = d * bg
        scrd[0:tm, :] = du
        scrd[tm:tm + 8, :] = carry[...]
        carry[...] = du[0:8, :]
        dp = jnp.zeros((tm, W), f32)
        for j in range(KW):
            dp = dp + cw_ref[j:j + 1, :] * scrd[pl.ds(KW - 1 - j, tm), :]
            dcw_ref[j:j + 1, :] += jnp.sum(du * scrx[pl.ds(8 - (KW - 1) + j, tm), :], axis=0, keepdims=True)
        dproj_ref[:, W:2 * W] = _bf(dp * xv)
        dproj_ref[:, 2 * W:3 * W] = _bf(dp * cg)

    rev = lambda i: (nb - 1 - i, 0)
    fix = lambda i: (0, 0)

    def halo(col):
        return pl.BlockSpec((8, W), lambda i: (jnp.maximum((nb - 1 - i) * (tm // 8) - 1, 0), col))

    return pl.pallas_call(
        body, name="sc_bwd", grid=(nb,), out_shape=(S((L, 3 * W), bf16), S((8, W), f32)),
        in_specs=[pl.BlockSpec((tm, W), rev), pl.BlockSpec((tm, 3 * W), rev), halo(1), halo(2),
                  pl.BlockSpec((KW, W), fix)],
        out_specs=(pl.BlockSpec((tm, 3 * W), rev), pl.BlockSpec((8, W), fix)),
        scratch_shapes=[pltpu.VMEM((tm + 8, W), f32), pltpu.VMEM((tm + 8, W), f32), pltpu.VMEM((8, W), f32)],
        compiler_params=_cp(dimension_semantics=("arbitrary",)))(dbu, proj, proj, proj, conv_w)


def _ffn_cols(F):
    fc = F
    for cand in (1408, 1024, 512, 256, 128):
        if F % cand == 0:
            fc = cand
            break
    return fc


def _ffn_fwd(h, w_up, conv_w, w_down, g, b, *, first_row, name):
    L, D = h.shape
    F = w_down.shape[0]
    KW = conv_w.shape[0]
    tm = TM
    fc = _ffn_cols(F)
    alpha = ALPHA

    def body(h_ref, wup_hbm, cw_ref, wdn_hbm, g_ref, b_ref, up_ref, a_ref, pre_ref, out_ref,
             wup, wdn, scr, carry, sem):
        i = pl.program_id(0)
        _load_once([(wup_hbm, wup), (wdn_hbm, wdn)], sem)

        @pl.when(i == 0)
        def _():
            carry[...] = jnp.zeros_like(carry)

        hv = h_ref[...]
        hb = _bf(hv)
        pre = alpha * hv
        for c0 in range(0, F, fc):
            cs = slice(c0, c0 + fc)
            u = _dot(hb, wup[:, cs])
            gate = _dot(hb, wup[:, F + c0:F + c0 + fc])
            up_ref[:, cs] = u
            up_ref[:, F + c0:F + c0 + fc] = gate
            scr[0:8, :] = carry[:, cs]
            scr[8:8 + tm, :] = u
            carry[:, cs] = u[tm - 8:tm, :]
            uc = jnp.zeros((tm, fc), f32)
            for j in range(KW):
                uc = uc + cw_ref[j:j + 1, cs] * scr[pl.ds(8 - (KW - 1) + j, tm), :]
            ab = _bf(uc * _sigmoid(uc) * gate)
            a_ref[:, cs] = ab
            pre = pre + _dot(ab, wdn[cs, :])
        pre_ref[...] = pre
        out_ref[...] = _ln_fwd(pre, g_ref[...], b_ref[...], _row_ids(i, tm, D), first_row)

    row = lambda i: (i, 0)
    fix = lambda i: (0, 0)
    return pl.pallas_call(
        body, name=name, grid=(L // tm,),
        out_shape=(S((L, 2 * F), f32), S((L, F), bf16), S((L, D), f32), S((L, D), f32)),
        in_specs=[pl.BlockSpec((tm, D), row), ANY, pl.BlockSpec((KW, F), fix), ANY,
                  pl.BlockSpec((1, D), fix), pl.BlockSpec((1, D), fix)],
        out_specs=(pl.BlockSpec((tm, 2 * F), row), pl.BlockSpec((tm, F), row),
                   pl.BlockSpec((tm, D), row), pl.BlockSpec((tm, D), row)),
        scratch_shapes=[pltpu.VMEM((D, 2 * F), w_up.dtype), pltpu.VMEM((F, D), w_down.dtype), pltpu.VMEM((tm + 8, fc), f32),
                        pltpu.VMEM((8, F), f32), pltpu.SemaphoreType.DMA((2,))],
        compiler_params=_cp(dimension_semantics=("arbitrary",)))(h, w_up, conv_w, w_down, g, b)


def _ffn_bwd(dpre, up, w_down, conv_w, *, name):
    L, D = dpre.shape
    F = w_down.shape[0]
    KW = conv_w.shape[0]
    tm = TM
    nb = L // tm
    fc = _ffn_cols(F)

    def body(dpre_ref, up_ref, halo_ref, wdn_hbm, cw_ref, dup_ref, dcw_ref, wdn, scrx, scrd, carry, sem):
        i = pl.program_id(0)
        blk = nb - 1 - i
        _load_once([(wdn_hbm, wdn)], sem)

        @pl.when(i == 0)
        def _():
            carry[...] = jnp.zeros_like(carry)
            dcw_ref[...] = jnp.zeros_like(dcw_ref)

        db = _bf(dpre_ref[...])
        for c0 in range(0, F, fc):
            cs = slice(c0, c0 + fc)
            da = _dot_nt(db, wdn[cs, :])
            gate = up_ref[:, F + c0:F + c0 + fc]
            scrx[0:8, :] = jnp.where(blk > 0, halo_ref[:, cs], 0.0)
            scrx[8:8 + tm, :] = up_ref[:, cs]
            uc = jnp.zeros((tm, fc), f32)
            for j in range(KW):
                uc = uc + cw_ref[j:j + 1, cs] * scrx[pl.ds(8 - (KW - 1) + j, tm), :]
            sig = _sigmoid(uc)
            dup_ref[:, F + c0:F + c0 + fc] = _bf(da * (uc * sig))
            duc = da * gate * (sig * (1.0 + uc * (1.0 - sig)))
            scrd[0:tm, :] = duc
            scrd[tm:tm + 8, :] = carry[:, cs]
            carry[:, cs] = duc[0:8, :]
            du = jnp.zeros((tm, fc), f32)
            for j in range(KW):
                du = du + cw_ref[j:j + 1, cs] * scrd[pl.ds(KW - 1 - j, tm), :]
                dcw_ref[j:j + 1, cs] += jnp.sum(duc * scrx[pl.ds(8 - (KW - 1) + j, tm), :], axis=0, keepdims=True)
            dup_ref[:, cs] = _bf(du)

    rev = lambda i: (nb - 1 - i, 0)
    fix = lambda i: (0, 0)
    return pl.pallas_call(
        body, name=name, grid=(nb,), out_shape=(S((L, 2 * F), bf16), S((8, F), f32)),
        in_specs=[pl.BlockSpec((tm, D), rev), pl.BlockSpec((tm, 2 * F), rev),
                  pl.BlockSpec((8, F), lambda i: (jnp.maximum((nb - 1 - i) * (tm // 8) - 1, 0), 0)),
                  ANY, pl.BlockSpec((KW, F), fix)],
        out_specs=(pl.BlockSpec((tm, 2 * F), rev), pl.BlockSpec((8, F), fix)),
        scratch_shapes=[pltpu.VMEM((F, D), w_down.dtype), pltpu.VMEM((tm + 8, fc), f32), pltpu.VMEM((tm + 8, fc), f32),
                        pltpu.VMEM((8, F), f32), pltpu.SemaphoreType.DMA((1,))],
        compiler_params=_cp(dimension_semantics=("arbitrary",)))(dpre, up, up, w_down, conv_w)


def _loss_head(h, target):
    L, D = h.shape
    tm = TM
    pb = PADF // tm

    def body(h_ref, t_ref, dh_ref, loss_ref):
        i = pl.program_id(0)

        @pl.when(i == 0)
        def _():
            loss_ref[...] = jnp.zeros_like(loss_ref)

        valid = i >= pb
        err = h_ref[...] - t_ref[...]
        dh_ref[...] = jnp.where(valid, err * (1.0 / D), 0.0)
        part = 0.5 * jnp.sum(jnp.sum(err * err, axis=-1, keepdims=True) * (1.0 / D), axis=0, keepdims=True)
        loss_ref[...] += jnp.where(valid, part, 0.0)

    return pl.pallas_call(
        body, name="loss_head", grid=(L // tm,), out_shape=(S((L, D), f32), S((8, LANE), f32)),
        in_specs=[pl.BlockSpec((tm, D), lambda i: (i, 0)),
                  pl.BlockSpec((tm, D), lambda i: (jnp.maximum(i - pb, 0), 0))],
        out_specs=(pl.BlockSpec((tm, D), lambda i: (i, 0)), pl.BlockSpec((8, LANE), lambda i: (0, 0))),
        compiler_params=_cp(dimension_semantics=("arbitrary",)))(h, target)


def _ln_bwd(dout, pre, g, *, first_row, name):
    L, D = pre.shape
    tm = TM

    def body(do_ref, pre_ref, g_ref, dpre_ref, dg_ref, db_ref):
        i = pl.program_id(0)

        @pl.when(i == 0)
        def _():
            dg_ref[...] = jnp.zeros_like(dg_ref)
            db_ref[...] = jnp.zeros_like(db_ref)

        pre_v = pre_ref[...]
        mu = jnp.mean(pre_v, axis=-1, keepdims=True)
        xc = pre_v - mu
        rstd = lax.rsqrt(jnp.mean(xc * xc, axis=-1, keepdims=True) + LN_EPS)
        xh = xc * rstd
        dy = jnp.where(_row_ids(i, tm, D) >= first_row, do_ref[...], 0.0)
        dg_ref[0:1, :] += jnp.sum(dy * xh, axis=0, keepdims=True)
        db_ref[0:1, :] += jnp.sum(dy, axis=0, keepdims=True)
        dxh = dy * g_ref[...]
        dpre_ref[...] = rstd * (dxh - jnp.mean(dxh, axis=-1, keepdims=True)
                                - xh * jnp.mean(dxh * xh, axis=-1, keepdims=True))

    row = lambda i: (i, 0)
    fix = lambda i: (0, 0)
    return pl.pallas_call(
        body, name=name, grid=(L // tm,), out_shape=(S((L, D), f32), S((8, D), f32), S((8, D), f32)),
        in_specs=[pl.BlockSpec((tm, D), row), pl.BlockSpec((tm, D), row), pl.BlockSpec((1, D), fix)],
        out_specs=(pl.BlockSpec((tm, D), row), pl.BlockSpec((8, D), fix), pl.BlockSpec((8, D), fix)),
        compiler_params=_cp(dimension_semantics=("arbitrary",)))(dout, pre, g)


def _adamw(g_terms, w, m, v, *, name):
    R, C = w.shape
    tr = _row_tile(R)
    n = len(g_terms)
    c1 = 1.0 - ADAM_B1 ** ADAM_STEP
    c2 = 1.0 - ADAM_B2 ** ADAM_STEP

    def body(*refs):
        g = refs[0][...].astype(f32)
        for r in refs[1:n]:
            g = g + r[...].astype(f32)
        w_ref, m_ref, v_ref, g_out, d_out, m_out, v_out = refs[n:]
        mn = ADAM_B1 * m_ref[...] + (1.0 - ADAM_B1) * g
        vn = ADAM_B2 * v_ref[...] + (1.0 - ADAM_B2) * (g * g)
        g_out[...] = g
        m_out[...] = mn
        v_out[...] = vn
        d_out[...] = -ADAM_LR * ((mn / c1) / (jnp.sqrt(vn / c2) + ADAM_EPS) + ADAM_WD * w_ref[...])

    spec = pl.BlockSpec((tr, C), lambda i: (i, 0))
    return pl.pallas_call(
        body, name=name, grid=(R // tr,), out_shape=(S((R, C), f32),) * 4,
        in_specs=[spec] * (n + 3), out_specs=(spec,) * 4,
        compiler_params=_cp(dimension_semantics=("arbitrary",)))(*g_terms, w, m, v)


def _sum_devices(x):
    n, R, C = x.shape

    def body(x_ref, o_ref):
        acc = x_ref[0]
        for d in range(1, n):
            acc = acc + x_ref[d]
        o_ref[...] = acc

    return pl.pallas_call(body, name="sum_devices", out_shape=S((R, C), f32), compiler_params=_cp())(x)


def _row_tile(R):
    for step in (16, 8):
        for t in range(256, 0, -step):
            if R % t == 0:
                return t
    return R


def _pair_add(s32, recv1, sidx, ridx, *, name):
    _, L, K, n = s32.shape
    nj = sidx.shape[0]
    tk = _row_tile(K)

    def body(sidx_ref, ridx_ref, a_ref, b_ref, o_ref):
        o_ref[...] = _bf(a_ref[...] + b_ref[...].astype(f32))

    blk = (1, 1, tk, n)
    grid_spec = pltpu.PrefetchScalarGridSpec(
        num_scalar_prefetch=2, grid=(nj, L, K // tk),
        in_specs=[pl.BlockSpec(blk, lambda j, l, i, si, ri: (si[j], l, i, 0)),
                  pl.BlockSpec(blk, lambda j, l, i, si, ri: (ri[j], l, i, 0))],
        out_specs=pl.BlockSpec(blk, lambda j, l, i, si, ri: (j, l, i, 0)))
    return pl.pallas_call(
        body, name=name, grid_spec=grid_spec, out_shape=S((nj, L, K, n), bf16),
        compiler_params=_cp(dimension_semantics=("arbitrary",) * 3))(sidx, ridx, s32, recv1)


def _adamw_shard(s32, recv1, recv2, w, m, v, idx, *, name):
    L, K, n = w.shape
    tk = _row_tile(K)
    c1 = 1.0 - ADAM_B1 ** ADAM_STEP
    c2 = 1.0 - ADAM_B2 ** ADAM_STEP

    def body(idx_ref, a_ref, b_ref, r_ref, w_ref, m_ref, v_ref, g_out, d_out, m_out, v_out):
        g = a_ref[0] + b_ref[0].astype(f32)
        for j in range(3):
            g = g + r_ref[j].astype(f32)
        mn = ADAM_B1 * m_ref[...] + (1.0 - ADAM_B1) * g
        vn = ADAM_B2 * v_ref[...] + (1.0 - ADAM_B2) * (g * g)
        g_out[...] = g
        m_out[...] = mn
        v_out[...] = vn
        d_out[...] = -ADAM_LR * ((mn / c1) / (jnp.sqrt(vn / c2) + ADAM_EPS) + ADAM_WD * w_ref[...])

    own = pl.BlockSpec((1, tk, n), lambda l, i, ix: (l, i, 0))
    grid_spec = pltpu.PrefetchScalarGridSpec(
        num_scalar_prefetch=1, grid=(L, K // tk),
        in_specs=[pl.BlockSpec((1, 1, tk, n), lambda l, i, ix: (ix[0], l, i, 0)),
                  pl.BlockSpec((1, 1, tk, n), lambda l, i, ix: (ix[1], l, i, 0)),
                  pl.BlockSpec((3, 1, tk, n), lambda l, i, ix: (0, l, i, 0)), own, own, own],
        out_specs=(own,) * 4)
    return pl.pallas_call(
        body, name=name, grid_spec=grid_spec, out_shape=(S((L, K, n), f32),) * 4,
        compiler_params=_cp(dimension_semantics=("arbitrary", "arbitrary")))(idx, s32, recv1, recv2, w, m, v)


def _adamw_direct(s32s, recvs, w, m, v, me, *, name):
    L, K, n = w.shape
    tk = _row_tile(K)
    c1 = 1.0 - ADAM_B1 ** ADAM_STEP
    c2 = 1.0 - ADAM_B2 ** ADAM_STEP

    def body(me_ref, *refs):
        own_refs, recv_refs = refs[:L], refs[L:2 * L]
        w_ref, m_ref, v_ref, g_out, d_out, m_out, v_out = refs[2 * L:]
        for li in range(L):
            @pl.when(pl.program_id(0) == li)
            def _(li=li):
                g = own_refs[li][0, 0]
                for d in range(N_DEV):
                    g = g + recv_refs[li][d, 0].astype(f32)
                mn = ADAM_B1 * m_ref[0] + (1.0 - ADAM_B1) * g
                vn = ADAM_B2 * v_ref[0] + (1.0 - ADAM_B2) * (g * g)
                g_out[0] = g
                m_out[0] = mn
                v_out[0] = vn
                d_out[0] = -ADAM_LR * ((mn / c1) / (jnp.sqrt(vn / c2) + ADAM_EPS) + ADAM_WD * w_ref[0])

    own = pl.BlockSpec((1, tk, n), lambda l, i, ix: (l, i, 0))
    grid_spec = pltpu.PrefetchScalarGridSpec(
        num_scalar_prefetch=1, grid=(L, K // tk),
        in_specs=[pl.BlockSpec((1, 1, tk, n), lambda l, i, ix: (ix[0], 0, i, 0))] * L
        + [pl.BlockSpec((N_DEV, 1, tk, n), lambda l, i, ix: (0, 0, i, 0))] * L + [own, own, own],
        out_specs=(own,) * 4)
    return pl.pallas_call(
        body, name=name, grid_spec=grid_spec, out_shape=(S((L, K, n), f32),) * 4,
        compiler_params=_cp(dimension_semantics=("arbitrary", "arbitrary")))(me, *s32s, *recvs, w, m, v)


def _col_segments(n, mapping):
    segs = []
    for p in range(N_DEV):
        lo, hi = p * n, (p + 1) * n
        out = []
        for c0, c1, e0 in mapping:
            a, b = max(lo, c0), min(hi, c1)
            if a < b:
                out.append((a - lo, e0 + (a - c0), b - a))
        segs.append(out)
    return segs


def _assemble_cols(gathered, mapping, n_out, *, name):
    _, L, K, n = gathered.shape
    tk = _row_tile(K)
    segs = _col_segments(n, mapping)
    covered = sum(w for s in segs for (_, _, w) in s)

    def body(g_ref, o_ref):
        if covered != n_out:
            o_ref[...] = jnp.zeros_like(o_ref)
        for p in range(N_DEV):
            for s0, d0, w in segs[p]:
                o_ref[0, :, d0:d0 + w] = g_ref[p, 0, :, s0:s0 + w]

    return pl.pallas_call(
        body, name=name, grid=(L, K // tk), out_shape=S((L, K, n_out), gathered.dtype),
        in_specs=[pl.BlockSpec((N_DEV, 1, tk, n), lambda l, i: (0, l, i, 0))],
        out_specs=pl.BlockSpec((1, tk, n_out), lambda l, i: (l, i, 0)),
        compiler_params=_cp(dimension_semantics=("arbitrary", "arbitrary")))(gathered)


def _split_cols(dws, mapping, n, *, name):
    L = len(dws)
    K, n_in = dws[0].shape
    tk = _row_tile(K)
    segs = _col_segments(n, mapping)

    def body(*refs):
        ins, o32, o16 = refs[:L], refs[L], refs[L + 1]
        for li in range(L):
            @pl.when(pl.program_id(0) == li)
            def _(li=li):
                for p in range(N_DEV):
                    for s0, d0, w in segs[p]:
                        val = ins[li][:, d0:d0 + w]
                        o32[p, 0, :, s0:s0 + w] = val
                        o16[p, 0, :, s0:s0 + w] = _bf(val)

    out = pl.BlockSpec((N_DEV, 1, tk, n), lambda l, i: (0, l, i, 0))
    return pl.pallas_call(
        body, name=name, grid=(L, K // tk), out_shape=(S((N_DEV, L, K, n), f32), S((N_DEV, L, K, n), bf16)),
        in_specs=[pl.BlockSpec((tk, n_in), lambda l, i: (i, 0))] * L, out_specs=(out, out),
        compiler_params=_cp(dimension_semantics=("arbitrary", "arbitrary")))(*dws)


def _split_rows(dws, k, *, name):
    L = len(dws)
    N = dws[0].shape[1]

    def body(*refs):
        ins, o32, o16 = refs[:L], refs[L], refs[L + 1]
        for li in range(L):
            @pl.when(pl.program_id(0) == li)
            def _(li=li):
                val = ins[li][...]
                o32[0, 0] = val
                o16[0, 0] = _bf(val)

    out = pl.BlockSpec((1, 1, k, N), lambda l, p: (p, l, 0, 0))
    return pl.pallas_call(
        body, name=name, grid=(L, N_DEV), out_shape=(S((N_DEV, L, k, N), f32), S((N_DEV, L, k, N), bf16)),
        in_specs=[pl.BlockSpec((k, N), lambda l, p: (p, 0))] * L, out_specs=(out, out),
        compiler_params=_cp(dimension_semantics=("arbitrary", "arbitrary")))(*dws)


def _rows_full(gathered):
    _, L, k, N = gathered.shape
    return jnp.transpose(gathered, (1, 0, 2, 3)).reshape(L, N_DEV * k, N)


def _peer(rel):
    x, y, c = lax.axis_index("x"), lax.axis_index("y"), lax.axis_index("c")
    return {"c": (x, y, 1 - c), "x": (1 - x, y, c), "y": (x, 1 - y, c), "xy": (1 - x, 1 - y, c)}[rel]


def _all_gather(xs, *, name):
    na = len(xs)

    def body(*refs):
        x_refs, out_refs = refs[:na], refs[na:2 * na]
        send_sems, recv_sems, local_sems = refs[2 * na:]
        mx, my, mc = lax.axis_index("x"), lax.axis_index("y"), lax.axis_index("c")
        me, sibling = (mx, my, mc), (mx, my, 1 - mc)
        chips = [(1 - mx, my), (mx, 1 - my), (1 - mx, 1 - my)]

        def slot(a, px, py, pc):
            return out_refs[a].at[4 * px + 2 * py + pc]

        def copy(a, kk, block, to, src=None):
            return pltpu.make_async_remote_copy(
                src_ref=slot(a, *block) if src is None else src, dst_ref=slot(a, *block),
                send_sem=send_sems.at[7 * a + kk], recv_sem=recv_sems.at[7 * a + kk], device_id=to, device_id_type=MESH)

        mine = [pltpu.make_async_copy(x_refs[a], slot(a, *me), local_sems.at[a]) for a in range(na)]
        for cp in mine:
            cp.start()
        first = []
        for a in range(na):
            first.append(copy(a, 0, me, sibling, src=x_refs[a]))
            first += [copy(a, 1 + j, me, (*chip, mc), src=x_refs[a]) for j, chip in enumerate(chips)]
        for cp in first:
            cp.start()
        passed = []
        for j, chip in enumerate(chips):
            for a in range(na):
                copy(a, 1 + j, (*chip, mc), me).wait_recv()
                fwd = copy(a, 4 + j, (*chip, mc), sibling)
                fwd.start()
                passed.append(fwd)
        for a in range(na):
            copy(a, 0, sibling, me).wait_recv()
            for j, chip in enumerate(chips):
                copy(a, 4 + j, (*chip, 1 - mc), me).wait_recv()
        for cp in first + passed:
            cp.wait_send()
        for cp in mine:
            cp.wait()

    return pl.pallas_call(
        body, name=name, out_shape=tuple(S((N_DEV,) + x.shape, x.dtype) for x in xs),
        in_specs=[ANY] * na, out_specs=(ANY,) * na,
        scratch_shapes=[pltpu.SemaphoreType.DMA((7 * na,)), pltpu.SemaphoreType.DMA((7 * na,)),
                        pltpu.SemaphoreType.DMA((na,))],
        compiler_params=pltpu.CompilerParams(has_side_effects=True))(*xs)


def _exchange(sends, n_slots, src_index, rels, *, name):
    na = len(sends)

    def body(*refs):
        send_refs, recv_refs = refs[:na], refs[na:2 * na]
        send_sems, recv_sems = refs[2 * na:]
        cps = [pltpu.make_async_remote_copy(
            src_ref=send_refs[a].at[src_index(j)], dst_ref=recv_refs[a].at[j],
            send_sem=send_sems.at[n_slots * a + j], recv_sem=recv_sems.at[n_slots * a + j],
            device_id=_peer(rels[j]), device_id_type=MESH) for a in range(na) for j in range(n_slots)]
        for cp in cps:
            cp.start()
        for cp in cps:
            cp.wait()

    return pl.pallas_call(
        body, name=name, out_shape=tuple(S((n_slots,) + s.shape[1:], s.dtype) for s in sends),
        in_specs=[ANY] * na, out_specs=(ANY,) * na,
        scratch_shapes=[pltpu.SemaphoreType.DMA((n_slots * na,)), pltpu.SemaphoreType.DMA((n_slots * na,))],
        compiler_params=pltpu.CompilerParams(has_side_effects=True))(*sends)


_FLIPS = [(fx, fy, fc) for fx in (0, 1) for fy in (0, 1) for fc in (0, 1)][1:]


def _flip_peer(flip):
    x, y, c = lax.axis_index("x"), lax.axis_index("y"), lax.axis_index("c")
    return tuple(1 - a if f else a for a, f in zip((x, y, c), flip))


def _dev_index(p):
    return 4 * p[0] + 2 * p[1] + p[2]


HBM_SPEC = pl.BlockSpec(memory_space=pltpu.HBM)
SEM_SPEC = pl.BlockSpec(memory_space=pltpu.SEMAPHORE)


def _direct_start(srcs, lands, per_peer, *, name):
    na = len(srcs)

    def body(*refs):
        src_refs, land_refs = refs[:na], refs[na:2 * na]
        send_sems, recv_sems = refs[2 * na], refs[2 * na + 1]
        token = refs[-1]
        me = _dev_index((lax.axis_index("x"), lax.axis_index("y"), lax.axis_index("c")))
        for a in range(na):
            for r, flip in enumerate(_FLIPS):
                peer = _flip_peer(flip)
                src = src_refs[a].at[_dev_index(peer)] if per_peer else src_refs[a]
                pltpu.make_async_remote_copy(
                    src_ref=src, dst_ref=land_refs[a].at[me], send_sem=send_sems.at[7 * a + r],
                    recv_sem=recv_sems.at[7 * a + r], device_id=peer, device_id_type=MESH).start()
        token[...] = jnp.zeros_like(token)

    hbm = lambda t: pltpu.with_memory_space_constraint(t, pltpu.HBM)
    out = pl.pallas_call(
        body, name=name,
        out_shape=(pltpu.SemaphoreType.DMA((7 * na,)), pltpu.SemaphoreType.DMA((7 * na,)))
        + tuple(pltpu.HBM(t.shape, t.dtype) for t in list(srcs) + list(lands)) + (S((8, LANE), f32),),
        in_specs=[HBM_SPEC] * (2 * na),
        out_specs=(SEM_SPEC, SEM_SPEC) + (HBM_SPEC,) * (2 * na) + (pl.BlockSpec(memory_space=pltpu.VMEM),),
        input_output_aliases={i: 2 + i for i in range(2 * na)},
        compiler_params=pltpu.CompilerParams(has_side_effects=pltpu.SideEffectType.DATAFLOW_SIDE_EFFECTING))(
            *[hbm(t) for t in srcs], *[hbm(t) for t in lands])
    return out[0], out[1], list(out[2:2 + na]), list(out[2 + na:2 + 2 * na]), out[-1]


def _direct_wait(send_sems, recv_sems, srcs, lands, per_peer, after, *, name):
    na = len(srcs)

    def body(*refs):
        src_refs, land_refs = refs[:na], refs[na:2 * na]
        ssem, rsem = refs[2 * na], refs[2 * na + 1]
        me = _dev_index((lax.axis_index("x"), lax.axis_index("y"), lax.axis_index("c")))
        for a in range(na):
            for r, flip in enumerate(_FLIPS):
                peer = _flip_peer(flip)
                src = src_refs[a].at[_dev_index(peer)] if per_peer else src_refs[a]
                cp = pltpu.make_async_remote_copy(
                    src_ref=src, dst_ref=land_refs[a].at[me], send_sem=ssem.at[7 * a + r],
                    recv_sem=rsem.at[7 * a + r], device_id=peer, device_id_type=MESH)
                cp.wait_send()
                cp.wait_recv()

    out = pl.pallas_call(
        body, name=name, out_shape=tuple(pltpu.HBM(t.shape, t.dtype) for t in list(srcs) + list(lands)),
        in_specs=[HBM_SPEC] * (2 * na) + [SEM_SPEC, SEM_SPEC, ANY], out_specs=(HBM_SPEC,) * (2 * na),
        input_output_aliases={i: i for i in range(2 * na)},
        compiler_params=pltpu.CompilerParams(has_side_effects=pltpu.SideEffectType.DATAFLOW_SIDE_EFFECTING))(
            *srcs, *lands, send_sems, recv_sems, after)
    return list(out[:na]), list(out[na:])


def _pack_small(parts, width):
    rows, offs, r = [], [], 0
    for a in parts:
        n = a.size
        nr = -(-n // width)
        flat = a.reshape(-1).astype(f32)
        if nr * width != n:
            flat = jnp.pad(flat, (0, nr * width - n))
        rows.append(flat.reshape(nr, width))
        offs.append((r, nr))
        r += nr
    buf = jnp.concatenate(rows, axis=0)
    pad = (-r) % 8
    if pad:
        buf = jnp.pad(buf, ((0, pad), (0, 0)))
    return buf, offs


def _unpack_small(buf, off, shape):
    r, nr = off
    return buf[r:r + nr].reshape(-1)[:math.prod(shape)].reshape(shape)


def _local_step(x, target, meta, a_w_in, small, start_token, late_weights, grads_ready):
    SEQ, D = x.shape
    n_meta = meta.shape[0]
    first_row = PADF - n_meta
    H = small["a_log"].shape[-1]

    h0 = jnp.concatenate([jnp.zeros((first_row, D), f32), meta, x], axis=0)

    def lanes(a):
        return jnp.pad(a.reshape(1, -1), ((0, 0), (0, LANE - a.size)))

    def after_token(a, token):
        return a if token is None else a + token[0:1, 0:1]

    alog, dtb = after_token(lanes(small["a_log"][0]), start_token), lanes(small["a_dt_bias"][0])
    a_conv, b_conv = small["a_conv"][0], small["b_conv"][0]
    nw = small["a_norm"][0].reshape(1, DH)
    lmg, lmb, lfg, lfb = small["ln_mix_g"], small["ln_mix_b"], small["ln_ffn_g"], small["ln_ffn_b"]

    pre_a, z, raw, q, k, v, beta, g = _gdn_in_fwd(h0, a_w_in, a_conv, alog, dtb, first_row=first_row, H=H)
    o, y, s_all, t_all = _delta_fwd(q, k, v, g, beta, z, nw, H=H)
    wts = late_weights(y)
    pre1, h1 = _out_res_ln(y, wts["a_w_out"], h0, lmg[0:1], lmb[0:1], first_row=first_row, name="gdn_out_ln")
    up0, act0, pre2, h2 = _ffn_fwd(h1, wts["ffn_w_up"][0], small["ffn_conv"][0], wts["ffn_w_down"][0],
                                   lfg[0:1], lfb[0:1], first_row=first_row, name="ffn_fwd0")
    proj_b, bu, pre3, h3 = _sc_fwd(h2, wts["b_w_in"], b_conv, wts["b_w_out"], lmg[1:2], lmb[1:2], first_row=first_row)
    up1, act1, pre4, h4 = _ffn_fwd(h3, wts["ffn_w_up"][1], small["ffn_conv"][1], wts["ffn_w_down"][1],
                                   lfg[1:2], lfb[1:2], first_row=first_row, name="ffn_fwd1")
    dh4, loss_tile = _loss_head(h4, target)

    gw, gs = {}, {}
    alpha = ALPHA

    def ffn_backward(dh_out, pre, up, act, h_in, layer, tag, token=None):
        dpre, dg, db = _ln_bwd(dh_out, pre, after_token(lfg[layer:layer + 1], token), first_row=first_row,
                               name="ln_bwd_ffn" + tag)
        dup, dcw = _ffn_bwd(dpre, up, wts["ffn_w_down"][layer], small["ffn_conv"][layer], name="ffn_bwd" + tag)
        dwd = _linear_dw(act, dpre, name="dw_down" + tag)
        dwu = _linear_dw(h_in, dup, name="dw_up" + tag)
        dh_in = _linear_dx(dup, wts["ffn_w_up"][layer], dpre, alpha=alpha, out_dtype=f32, name="dx_up" + tag)
        return dh_in, dwu, dwd, dcw[0:3], dg[0], db[0]

    dh3, dwu1, dwd1, dcf1, dlfg1, dlfb1 = ffn_backward(dh4, pre4, up1, act1, h3, 1, "1")

    dpre3, dlmg1, dlmb1 = _ln_bwd(dh3, pre3, lmg[1:2], first_row=first_row, name="ln_bwd_mix1")
    dbu = _linear_dx(dpre3, wts["b_w_out"], None, alpha=0.0, out_dtype=f32, name="dx_b_out")
    dwb_out = _linear_dw(bu, dpre3, name="dw_b_out")
    dproj_b, dcb = _sc_bwd(dbu, proj_b, b_conv)
    dwb_in = _linear_dw(h2, dproj_b, name="dw_b_in")
    dh2 = _linear_dx(dproj_b, wts["b_w_in"], dpre3, alpha=alpha, out_dtype=f32, name="dx_b_in")
    token = grads_ready("layer1", dict(ffn_w_up=dwu1, ffn_w_down=dwd1, b_w_in=dwb_in, b_w_out=dwb_out))

    dh1, dwu0, dwd0, dcf0, dlfg0, dlfb0 = ffn_backward(dh2, pre2, up0, act0, h1, 0, "0", token)
    token = grads_ready("layer0", dict(ffn_w_up=dwu0, ffn_w_down=dwd0))

    dpre1, dlmg0, dlmb0 = _ln_bwd(dh1, pre1, after_token(lmg[0:1], token), first_row=first_row, name="ln_bwd_mix0")
    dy = _linear_dx(dpre1, wts["a_w_out"], None, alpha=0.0, out_dtype=f32, name="dx_a_out")
    gw["a_w_out"] = [_linear_dw(y, dpre1, name="dw_a_out")]
    dq, dk, dv, dz, dg_, dbeta, dnw = _delta_bwd(dy, o, z, nw, q, k, v, g, beta, s_all, t_all, H=H)
    dproj_a, dca, dal, ddt = _gdn_in_bwd(dq, dk, dv, dz, dg_, dbeta, pre_a, raw, a_conv, alog, dtb,
                                         first_row=first_row, H=H)
    gw["a_w_in"] = [_linear_dw(h0, dproj_a, name="dw_a_in")]
    dh0 = _linear_dx(dproj_a, a_w_in, dpre1, alpha=alpha, out_dtype=f32, name="dx_a_in")

    gs["meta"] = dh0[first_row:PADF]
    gs["a_conv"] = dca[0:a_conv.shape[0]][None]
    gs["a_log"] = dal[0:1, 0:H]
    gs["a_dt_bias"] = ddt[0:1, 0:H]
    gs["a_norm"] = dnw[0:1]
    gs["b_conv"] = dcb[0:b_conv.shape[0]][None]
    gs["ln_mix_g"] = jnp.stack([dlmg0[0], dlmg1[0]])
    gs["ln_mix_b"] = jnp.stack([dlmb0[0], dlmb1[0]])
    gs["ffn_conv"] = jnp.stack([dcf0, dcf1])
    gs["ln_ffn_g"] = jnp.stack([dlfg0, dlfg1])
    gs["ln_ffn_b"] = jnp.stack([dlfb0, dlfb1])
    return loss_tile, dh0[PADF:], gw, gs


_BIG = ("a_w_in", "a_w_out", "b_w_in", "b_w_out", "ffn_w_up", "ffn_w_down")
_BIG_COL = ("a_w_in", "b_w_in", "ffn_w_up")
_SMALL = ("meta", "a_conv", "a_log", "a_dt_bias", "a_norm", "b_conv", "ln_mix_g", "ln_mix_b",
          "ffn_conv", "ln_ffn_g", "ln_ffn_b")
_SMALL_SHARDED = ("meta", "a_conv", "b_conv", "ffn_conv")
_ORDER = ("meta", "a_w_in", "a_conv", "a_log", "a_dt_bias", "a_norm", "a_w_out", "b_w_in", "b_conv", "b_w_out",
          "ln_mix_g", "ln_mix_b", "ffn_w_up", "ffn_conv", "ffn_w_down", "ln_ffn_g", "ln_ffn_b")


def _a_w_in_map(H):
    W4 = 4 * H * DH
    return [(0, W4, 0), (W4, W4 + H, W4), (W4 + H, W4 + 2 * H, W4 + LANE)], W4 + 2 * LANE


def kernel(x, meta, a_w_in, a_conv, a_log, a_dt_bias, a_norm, a_w_out, b_w_in, b_conv, b_w_out, ln_mix_g, ln_mix_b, ffn_w_up, ffn_conv, ffn_w_down, ln_ffn_g, ln_ffn_b, loss_target, m_meta, m_a_w_in, m_a_conv, m_a_log, m_a_dt_bias, m_a_norm, m_a_w_out, m_b_w_in, m_b_conv, m_b_w_out, m_ln_mix_g, m_ln_mix_b, m_ffn_w_up, m_ffn_conv, m_ffn_w_down, m_ln_ffn_g, m_ln_ffn_b, v_meta, v_a_w_in, v_a_conv, v_a_log, v_a_dt_bias, v_a_norm, v_a_w_out, v_b_w_in, v_b_conv, v_b_w_out, v_ln_mix_g, v_ln_mix_b, v_ffn_w_up, v_ffn_conv, v_ffn_w_down, v_ln_ffn_g, v_ln_ffn_b):
    wloc = dict(meta=meta, a_w_in=a_w_in, a_conv=a_conv, a_log=a_log, a_dt_bias=a_dt_bias, a_norm=a_norm,
                a_w_out=a_w_out, b_w_in=b_w_in, b_conv=b_conv, b_w_out=b_w_out, ln_mix_g=ln_mix_g, ln_mix_b=ln_mix_b,
                ffn_w_up=ffn_w_up, ffn_conv=ffn_conv, ffn_w_down=ffn_w_down, ln_ffn_g=ln_ffn_g, ln_ffn_b=ln_ffn_b)
    mloc = dict(meta=m_meta, a_w_in=m_a_w_in, a_conv=m_a_conv, a_log=m_a_log, a_dt_bias=m_a_dt_bias, a_norm=m_a_norm,
                a_w_out=m_a_w_out, b_w_in=m_b_w_in, b_conv=m_b_conv, b_w_out=m_b_w_out, ln_mix_g=m_ln_mix_g,
                ln_mix_b=m_ln_mix_b, ffn_w_up=m_ffn_w_up, ffn_conv=m_ffn_conv, ffn_w_down=m_ffn_w_down,
                ln_ffn_g=m_ln_ffn_g, ln_ffn_b=m_ln_ffn_b)
    vloc = dict(meta=v_meta, a_w_in=v_a_w_in, a_conv=v_a_conv, a_log=v_a_log, a_dt_bias=v_a_dt_bias, a_norm=v_a_norm,
                a_w_out=v_a_w_out, b_w_in=v_b_w_in, b_conv=v_b_conv, b_w_out=v_b_w_out, ln_mix_g=v_ln_mix_g,
                ln_mix_b=v_ln_mix_b, ffn_w_up=v_ffn_w_up, ffn_conv=v_ffn_conv, ffn_w_down=v_ffn_w_down,
                ln_ffn_g=v_ln_ffn_g, ln_ffn_b=v_ln_ffn_b)
    H = a_log.shape[-1]
    mx, my, mc = lax.axis_index("x"), lax.axis_index("y"), lax.axis_index("c")
    me = 4 * mx + 2 * my + mc

    a_map, a_cols = _a_w_in_map(H)
    col_maps = {"a_w_in": (a_map, a_cols)}
    for n in ("b_w_in", "ffn_w_up"):
        ncols = N_DEV * wloc[n].shape[-1]
        col_maps[n] = ([(0, ncols, 0)], ncols)
    sm_sh = [wloc[n] for n in _SMALL_SHARDED]
    sbuf, soffs = _pack_small(sm_sh, 128)
    g_a_w_in, sg = _all_gather([_bf(wloc["a_w_in"]), sbuf], name="gather_first")
    w_a_in = _assemble_cols(g_a_w_in, *col_maps["a_w_in"], name="assemble_a_w_in")[0]
    late = [n for n in _BIG if n != "a_w_in"]
    ssem, rsem, srcs_t, lands_t, start_token = _direct_start(
        [_bf(wloc[n]) for n in late], [lax.empty((N_DEV,) + wloc[n].shape, bf16) for n in late], False,
        name="gather_rest_start")

    def late_weights(after):
        srcs_d, landed = _direct_wait(ssem, rsem, srcs_t, lands_t, False, after, name="gather_rest_wait")
        wts = {}
        for n, own, got in zip(late, srcs_d, landed):
            full = lax.dynamic_update_index_in_dim(got, own, me, 0)
            if n in _BIG_COL:
                wts[n] = _assemble_cols(full, *col_maps[n], name="assemble_" + n)
            else:
                wts[n] = _rows_full(full)
        for n in ("a_w_out", "b_w_in", "b_w_out"):
            wts[n] = wts[n][0]
        return wts

    small = {n: wloc[n] for n in _SMALL}
    for n, off in zip(_SMALL_SHARDED, soffs):
        sh = wloc[n].shape
        parts = jnp.stack([_unpack_small(sg[d], off, sh) for d in range(N_DEV)])
        nd = len(sh)
        small[n] = jnp.transpose(parts, tuple(range(1, nd)) + (0, nd)).reshape(sh[:-1] + (N_DEV * sh[-1],))

    def split(n, dws, tag):
        if n in _BIG_COL:
            return _split_cols(dws, col_maps[n][0], wloc[n].shape[-1], name="split_" + n + tag)
        return _split_rows(dws, wloc[n].shape[-2], name="split_" + n + tag)

    sent = {}

    def grads_ready(stage, grads):
        names = sorted(grads)
        parts = [split(n, [grads[n]], "_" + stage) for n in names]
        handles = _direct_start([p[1] for p in parts], [jnp.zeros(p[1].shape, bf16) for p in parts], True,
                                name="grads_" + stage + "_start")
        sent[stage] = (names, [p[0] for p in parts], handles)
        return handles[4]

    loss_tile, grad_x, gw, gs = _local_step(x[0], loss_target[0], small["meta"], w_a_in, small, start_token,
                                            late_weights, grads_ready)

    last = ("a_w_in", "a_w_out")
    s32, s16 = {}, {}
    for n in last:
        s32[n], s16[n] = split(n, gw[n], "")
    recv1 = dict(zip(last, _exchange([s16[n] for n in last], 4, lambda j: 2 * j + (1 - lax.axis_index("c")),
                                     ["c"] * 4, name="grad_to_sibling")))
    chip = 2 * mx + my
    others = [2 * (1 - mx) + my, 2 * mx + (1 - my), 2 * (1 - mx) + (1 - my)]
    sidx = jnp.stack([2 * o + mc for o in others]).astype(jnp.int32)
    ridx = jnp.stack(others).astype(jnp.int32)
    to_send = [_pair_add(s32[n], recv1[n], sidx, ridx, name="pair_add_" + n) for n in last]
    recv2 = dict(zip(last, _exchange(to_send, 3, lambda j: j, ["x", "y", "xy"], name="grad_to_chips")))
    own = jnp.stack([me, chip]).astype(jnp.int32)
    big_out = {n: _adamw_shard(s32[n], recv1[n], recv2[n], wloc[n], mloc[n], vloc[n], own, name="adamw_" + n)
               for n in last}

    got = {}
    for stage, (names, own32, (ssem_g, rsem_g, srcs_g, lands_g, _)) in sent.items():
        _, landed = _direct_wait(ssem_g, rsem_g, srcs_g, lands_g, True, recv2["a_w_in"], name="grads_" + stage + "_wait")
        for n, o32, r in zip(names, own32, landed):
            got.setdefault(n, []).append((stage, o32, r))
    me1 = jnp.stack([me]).astype(jnp.int32)
    for n, parts in got.items():
        parts = sorted(parts, key=lambda t: t[0])
        big_out[n] = _adamw_direct([p[1] for p in parts], [p[2] for p in parts], wloc[n], mloc[n], vloc[n], me1,
                                   name="adamw_" + n)

    names = list(_SMALL)
    pbuf, poffs = _pack_small([gs[n] for n in names] + [loss_tile[0:1, 0:1]], 1024)
    psum = _sum_devices(_all_gather([pbuf], name="gather_small_grads")[0])
    loss = psum[poffs[-1][0], 0]
    g_small = {}
    for n, off in zip(names, poffs[:-1]):
        full_shape = gs[n].shape
        gfull = _unpack_small(psum, off, full_shape)
        if n in _SMALL_SHARDED:
            ns = wloc[n].shape[-1]
            gfull = lax.dynamic_slice_in_dim(gfull, me * ns, ns, axis=gfull.ndim - 1)
        g_small[n] = gfull.reshape(wloc[n].shape)
    gbuf, aoffs = _pack_small([g_small[n] for n in names], 128)
    wbuf, _ = _pack_small([wloc[n] for n in names], 128)
    mbuf, _ = _pack_small([mloc[n] for n in names], 128)
    vbuf, _ = _pack_small([vloc[n] for n in names], 128)
    _, d_s, m_s, v_s = _adamw([gbuf], wbuf, mbuf, vbuf, name="adamw_small")

    grads, deltas, new_m, new_v = {}, {}, {}, {}
    for n in _BIG:
        grads[n], deltas[n], new_m[n], new_v[n] = big_out[n]
    for n, off in zip(names, aoffs):
        sh = wloc[n].shape
        grads[n] = g_small[n]
        deltas[n], new_m[n], new_v[n] = (_unpack_small(b_, off, sh) for b_ in (d_s, m_s, v_s))
    return (loss, grad_x[None], *[grads[n] for n in _ORDER], *[deltas[n] for n in _ORDER],
            *[new_m[n] for n in _ORDER], *[new_v[n] for n in _ORDER])
```

```python
import math

import jax
import jax.numpy as jnp
from jax import lax
from jax.experimental import pallas as pl
from jax.experimental.pallas import tpu as pltpu

f32, bf16 = jnp.float32, jnp.bfloat16
S = jax.ShapeDtypeStruct
HI = lax.Precision.HIGHEST
HI3 = lax.Precision.HIGH
MESH = pl.DeviceIdType.MESH

V7X_VMEM_LIMIT = 56 * 1024 * 1024
LANE = 128
DH = 128
CH = 64
PADF = 256
TM = 256
TMM = 768
N_DEV = 8
BWD_HEAD_GROUP = 8

DEPTH = 2
ALPHA = (2.0 * DEPTH) ** 0.25
LN_EPS = 1e-5
RMS_EPS = 1e-6
L2_EPS = 1e-6
ADAM_LR, ADAM_B1, ADAM_B2, ADAM_EPS, ADAM_WD, ADAM_STEP = 0.001, 0.9, 0.999, 1e-08, 0.01, 10


def _cp(**kw):
    return pltpu.CompilerParams(vmem_limit_bytes=V7X_VMEM_LIMIT, **kw)


def _bf(x):
    return x.astype(bf16)


def _dot(a, b, precision=None):
    return jnp.dot(a, b, preferred_element_type=f32, precision=precision)


def _dot_nt(a, b):
    return lax.dot_general(a, b, (((1,), (1,)), ((), ())), preferred_element_type=f32)


def _dot_tn(a, b):
    return lax.dot_general(a, b, (((0,), (0,)), ((), ())), preferred_element_type=f32)


def _sigmoid(x):
    return 1.0 / (1.0 + jnp.exp(-x))


def _load_once(pairs, sem):
    @pl.when(pl.program_id(0) == 0)
    def _():
        cps = [pltpu.make_async_copy(src, dst, sem.at[n]) for n, (src, dst) in enumerate(pairs)]
        for c in cps:
            c.start()
        for c in cps:
            c.wait()


def _row_ids(i, tm, width):
    return i * tm + lax.broadcasted_iota(jnp.int32, (tm, width), 0)


def _ln_fwd(pre, g, b, rows, first_row):
    mu = jnp.mean(pre, axis=-1, keepdims=True)
    xc = pre - mu
    var = jnp.mean(xc * xc, axis=-1, keepdims=True)
    y = xc * lax.rsqrt(var + LN_EPS) * g + b
    return jnp.where(rows >= first_row, y, 0.0)


ANY = pl.BlockSpec(memory_space=pl.ANY)


def _taps_back(prev8, x, kw):
    xe = jnp.concatenate([prev8, x], axis=0)
    return [pltpu.roll(xe, kw - 1 - j, 0)[8:] for j in range(kw - 1)] + [x]


def _taps_ahead(x, next8, kw):
    n = x.shape[0]
    xe = jnp.concatenate([x, next8], axis=0)
    return [pltpu.roll(xe, n + 8 - (kw - 1 - j), 0)[:n] for j in range(kw - 1)] + [x]


def _conv(cw, taps):
    acc = cw[0:1, :] * taps[0]
    for j in range(1, len(taps)):
        acc = acc + cw[j:j + 1, :] * taps[j]
    return acc


def _linear_dw(x, dy, *, name):
    L, K = x.shape
    N = dy.shape[1]
    tm = TMM if L % TMM == 0 else TM
    tn = LANE
    for d in range(N // LANE, 0, -1):
        if (N // LANE) % d == 0 and K * d * LANE * 4 <= 9 * 1024 * 1024:
            tn = d * LANE
            break

    def body(x_ref, dy_ref, o_ref):
        @pl.when(pl.program_id(1) == 0)
        def _():
            o_ref[...] = jnp.zeros_like(o_ref)
        o_ref[...] += _dot_tn(_bf(x_ref[...]), _bf(dy_ref[...]))

    return pl.pallas_call(
        body, name=name, grid=(N // tn, L // tm), out_shape=S((K, N), f32),
        in_specs=[pl.BlockSpec((tm, K), lambda j, i: (i, 0)), pl.BlockSpec((tm, tn), lambda j, i: (i, j))],
        out_specs=pl.BlockSpec((K, tn), lambda j, i: (0, j)),
        compiler_params=_cp(dimension_semantics=("arbitrary", "arbitrary")))(x, dy)


def _ln_bwd_rows(dout, pre, g, rows, first_row):
    mu = jnp.mean(pre, axis=-1, keepdims=True)
    xc = pre - mu
    rstd = lax.rsqrt(jnp.mean(xc * xc, axis=-1, keepdims=True) + LN_EPS)
    xh = xc * rstd
    dy = jnp.where(rows >= first_row, dout, 0.0)
    dxh = dy * g
    dpre = rstd * (dxh - jnp.mean(dxh, axis=-1, keepdims=True) - xh * jnp.mean(dxh * xh, axis=-1, keepdims=True))
    return dpre, jnp.sum(dy * xh, axis=0, keepdims=True), jnp.sum(dy, axis=0, keepdims=True)


def _linear_dx(dy, w, res, *, alpha, name, ln=None):
    L, N = dy.shape
    K = w.shape[0]
    tm = TM
    has_res = res is not None
    first_row = ln[2] if ln else 0

    def body(*refs):
        refs = list(refs)
        dy_ref, w_hbm = refs[:2]
        res_ref = refs.pop(2) if has_res else None
        if ln:
            pre_ref, g_ref, o_ref, dg_ref, db_ref, w_vmem, sem = refs[2:]
        else:
            o_ref, w_vmem, sem = refs[2:]
        _load_once([(w_hbm, w_vmem)], sem)
        acc = _dot_nt(_bf(dy_ref[...]), w_vmem[...])
        if has_res:
            acc = acc + alpha * res_ref[...]
        if not ln:
            o_ref[...] = acc
            return
        i = pl.program_id(0)

        @pl.when(i == 0)
        def _():
            dg_ref[...] = jnp.zeros_like(dg_ref)
            db_ref[...] = jnp.zeros_like(db_ref)

        dpre, dg, db = _ln_bwd_rows(acc, pre_ref[...], g_ref[...], _row_ids(i, tm, K), first_row)
        o_ref[...] = dpre
        dg_ref[0:1, :] += dg
        db_ref[0:1, :] += db

    row = lambda i: (i, 0)
    fix = lambda i: (0, 0)
    in_specs = [pl.BlockSpec((tm, N), row), ANY]
    args = [dy, w]
    if has_res:
        in_specs.append(pl.BlockSpec((tm, K), row))
        args.append(res)
    out_shape, out_specs = S((L, K), f32), pl.BlockSpec((tm, K), row)
    if ln:
        in_specs += [pl.BlockSpec((tm, K), row), pl.BlockSpec((1, K), fix)]
        args += [ln[0], ln[1]]
        out_shape = (out_shape, S((8, K), f32), S((8, K), f32))
        out_specs = (out_specs, pl.BlockSpec((8, K), fix), pl.BlockSpec((8, K), fix))
    return pl.pallas_call(
        body, name=name, grid=(L // tm,), out_shape=out_shape, in_specs=in_specs, out_specs=out_specs,
        scratch_shapes=[pltpu.VMEM((K, N), w.dtype), pltpu.SemaphoreType.DMA((1,))],
        compiler_params=_cp(dimension_semantics=("arbitrary",)))(*args)


def _out_res_ln(y, w, h, g, b, *, first_row, name):
    L, K = y.shape
    D = w.shape[1]
    tm = TM
    alpha = ALPHA

    def body(y_ref, w_hbm, h_ref, g_ref, b_ref, pre_ref, out_ref, w_vmem, sem):
        _load_once([(w_hbm, w_vmem)], sem)
        pre = alpha * h_ref[...] + _dot(_bf(y_ref[...]), w_vmem[...])
        pre_ref[...] = pre
        out_ref[...] = _ln_fwd(pre, g_ref[...], b_ref[...], _row_ids(pl.program_id(0), tm, D), first_row)

    row = lambda i: (i, 0)
    fix = lambda i: (0, 0)
    return pl.pallas_call(
        body, name=name, grid=(L // tm,), out_shape=(S((L, D), f32), S((L, D), f32)),
        in_specs=[pl.BlockSpec((tm, K), row), ANY, pl.BlockSpec((tm, D), row),
                  pl.BlockSpec((1, D), fix), pl.BlockSpec((1, D), fix)],
        out_specs=(pl.BlockSpec((tm, D), row), pl.BlockSpec((tm, D), row)),
        scratch_shapes=[pltpu.VMEM((K, D), w.dtype), pltpu.SemaphoreType.DMA((1,))],
        compiler_params=_cp(dimension_semantics=("arbitrary",)))(y, w, h, g, b)


def _gdn_in_fwd(h, w_full, conv_w, alog, dtb, *, first_row, H):
    L, D = h.shape
    W = H * DH
    NW = w_full.shape[1]
    KW = conv_w.shape[0]
    tm = TM

    def body(h_ref, w_hbm, cw_ref, alog_ref, dtb_ref,
             pre_ref, z_ref, raw_ref, q_ref, k_ref, v_ref, beta_ref, g_ref,
             w_vmem, carry, sem):
        i = pl.program_id(0)
        _load_once([(w_hbm, w_vmem)], sem)

        @pl.when(i == 0)
        def _():
            carry[...] = jnp.zeros_like(carry)

        hb = _bf(h_ref[...])
        outs = (q_ref, k_ref, v_ref)
        for s in range(3):
            pre = _dot(hb, w_vmem[:, s * W:(s + 1) * W])
            pre_ref[:, s * W:(s + 1) * W] = pre
            c = _conv(cw_ref[:, s * W:(s + 1) * W], _taps_back(carry[s], pre, KW))
            carry[s] = pre[tm - 8:tm, :]
            sl = c * _sigmoid(c)
            if s < 2:
                scale = DH ** -0.5 if s == 0 else 1.0
                for hh in range(H):
                    seg = sl[:, hh * DH:(hh + 1) * DH]
                    r = lax.rsqrt(jnp.sum(seg * seg, axis=-1, keepdims=True) + L2_EPS)
                    outs[s][:, hh * DH:(hh + 1) * DH] = seg * (r * scale)
            else:
                v_ref[...] = sl
        z_ref[...] = _dot(hb, w_vmem[:, 3 * W:4 * W])
        raw = _dot(hb, w_vmem[:, 4 * W:4 * W + 2 * LANE])
        raw_ref[...] = raw
        ok = (_row_ids(i, tm, LANE) >= first_row) & (lax.broadcasted_iota(jnp.int32, (tm, LANE), 1) < H)
        beta_ref[...] = jnp.where(ok, _sigmoid(raw[:, :LANE]), 0.0)
        a = raw[:, LANE:] + dtb_ref[...]
        sp = jnp.maximum(a, 0.0) + jnp.log(1.0 + jnp.exp(-jnp.abs(a)))
        gv = jnp.where(ok, -jnp.exp(alog_ref[...]) * sp, 0.0)
        g_ref[...] = _dot(_chunk_tri(tm, lower=True), gv, HI)

    row = lambda i: (i, 0)
    fix = lambda i: (0, 0)
    out_shape = (S((L, 3 * W), f32), S((L, W), f32), S((L, 2 * LANE), f32),
                 S((L, W), f32), S((L, W), f32), S((L, W), f32), S((L, LANE), f32), S((L, LANE), f32))
    out_specs = (pl.BlockSpec((tm, 3 * W), row), pl.BlockSpec((tm, W), row), pl.BlockSpec((tm, 2 * LANE), row),
                 pl.BlockSpec((tm, W), row), pl.BlockSpec((tm, W), row), pl.BlockSpec((tm, W), row),
                 pl.BlockSpec((tm, LANE), row), pl.BlockSpec((tm, LANE), row))
    return pl.pallas_call(
        body, name="gdn_in_fwd", grid=(L // tm,), out_shape=out_shape,
        in_specs=[pl.BlockSpec((tm, D), row), ANY, pl.BlockSpec((KW, 3 * W), fix),
                  pl.BlockSpec((1, LANE), fix), pl.BlockSpec((1, LANE), fix)],
        out_specs=out_specs,
        scratch_shapes=[pltpu.VMEM((D, NW), w_full.dtype), pltpu.VMEM((3, 8, W), f32), pltpu.SemaphoreType.DMA((1,))],
        compiler_params=_cp(dimension_semantics=("arbitrary",)))(h, w_full, conv_w, alog, dtb)


def _gdn_in_bwd(dq, dk, dv, dz, dg, dbeta, pre, raw, conv_w, alog, dtb, *, first_row, H):
    L = dq.shape[0]
    W = H * DH
    KW = conv_w.shape[0]
    tm = TM
    nb = L // tm
    NW = 4 * W + 2 * LANE

    def body(dq_ref, dk_ref, dv_ref, dz_ref, dg_ref, dbeta_ref, pre_ref, hq_ref, hk_ref, hv_ref, raw_ref,
             cw_ref, alog_ref, dtb_ref, dproj_ref, dcw_ref, dal_ref, ddt_ref, carry, tmp):
        i = pl.program_id(0)
        blk = nb - 1 - i

        @pl.when(i == 0)
        def _():
            carry[...] = jnp.zeros_like(carry)
            dcw_ref[...] = jnp.zeros_like(dcw_ref)
            dal_ref[...] = jnp.zeros_like(dal_ref)
            ddt_ref[...] = jnp.zeros_like(ddt_ref)

        halos = (hq_ref, hk_ref, hv_ref)
        douts = (dq_ref, dk_ref, dv_ref)
        for s in range(3):
            sec = slice(s * W, (s + 1) * W)
            pre = pre_ref[:, sec]
            c = _conv(cw_ref[:, sec], _taps_back(jnp.where(blk > 0, halos[s][...], 0.0), pre, KW))
            sig = _sigmoid(c)
            sl = c * sig
            if s < 2:
                scale = DH ** -0.5 if s == 0 else 1.0
                for hh in range(H):
                    hs = slice(hh * DH, (hh + 1) * DH)
                    seg = sl[:, hs]
                    r = lax.rsqrt(jnp.sum(seg * seg, axis=-1, keepdims=True) + L2_EPS)
                    n = seg * r
                    dqs = douts[s][:, hs]
                    tmp[:, hs] = (scale * r) * (dqs - n * jnp.sum(n * dqs, axis=-1, keepdims=True))
                dsl = tmp[...]
            else:
                dsl = dv_ref[...]
            dc = dsl * (sig * (1.0 + c * (1.0 - sig)))
            ahead = _taps_ahead(dc, carry[s], KW)
            carry[s] = dc[0:8, :]
            dproj_ref[:, sec] = _bf(_conv(cw_ref[:, sec], ahead))
            for j in range(KW):
                dcw_ref[j:j + 1, sec] += jnp.sum(ahead[j] * pre, axis=0, keepdims=True)
        dproj_ref[:, 3 * W:4 * W] = _bf(dz_ref[...])
        raw_v = raw_ref[...]
        ok = (_row_ids(blk, tm, LANE) >= first_row) & (lax.broadcasted_iota(jnp.int32, (tm, LANE), 1) < H)
        beta = _sigmoid(raw_v[:, :LANE])
        dbraw = jnp.where(ok, dbeta_ref[...] * beta * (1.0 - beta), 0.0)
        a = raw_v[:, LANE:] + dtb_ref[...]
        sp = jnp.maximum(a, 0.0) + jnp.log(1.0 + jnp.exp(-jnp.abs(a)))
        nea = -jnp.exp(alog_ref[...])
        dgm = jnp.where(ok, _dot(_chunk_tri(tm, lower=False), dg_ref[...], HI), 0.0)
        daraw = dgm * nea * _sigmoid(a)
        dal_ref[0:1, :] += jnp.sum(dgm * nea * sp, axis=0, keepdims=True)
        ddt_ref[0:1, :] += jnp.sum(daraw, axis=0, keepdims=True)
        dproj_ref[:, 4 * W:4 * W + LANE] = _bf(dbraw)
        dproj_ref[:, 4 * W + LANE:4 * W + 2 * LANE] = _bf(daraw)

    rev = lambda i: (nb - 1 - i, 0)
    fix = lambda i: (0, 0)

    def halo(col):
        return pl.BlockSpec((8, W), lambda i: (jnp.maximum((nb - 1 - i) * (tm // 8) - 1, 0), col))

    return pl.pallas_call(
        body, name="gdn_in_bwd", grid=(nb,),
        out_shape=(S((L, NW), bf16), S((8, 3 * W), f32), S((8, LANE), f32), S((8, LANE), f32)),
        in_specs=[pl.BlockSpec((tm, W), rev)] * 4 + [pl.BlockSpec((tm, LANE), rev)] * 2
        + [pl.BlockSpec((tm, 3 * W), rev), halo(0), halo(1), halo(2), pl.BlockSpec((tm, 2 * LANE), rev),
           pl.BlockSpec((KW, 3 * W), fix), pl.BlockSpec((1, LANE), fix), pl.BlockSpec((1, LANE), fix)],
        out_specs=(pl.BlockSpec((tm, NW), rev), pl.BlockSpec((8, 3 * W), fix),
                   pl.BlockSpec((8, LANE), fix), pl.BlockSpec((8, LANE), fix)),
        scratch_shapes=[pltpu.VMEM((3, 8, W), f32), pltpu.VMEM((tm, W), f32)],
        compiler_params=_cp(dimension_semantics=("arbitrary",)))(
            dq, dk, dv, dz, dg, dbeta, pre, pre, pre, pre, raw, conv_w, alog, dtb)


def _chunk_tri(n, lower):
    i = lax.broadcasted_iota(jnp.int32, (n, n), 0)
    j = lax.broadcasted_iota(jnp.int32, (n, n), 1)
    sh = int(math.log2(CH))
    same = lax.shift_right_logical(i, sh) == lax.shift_right_logical(j, sh)
    return (same & ((i >= j) if lower else (j >= i))).astype(f32)


def _tri_inv_many(ms, eye):
    ts = [eye - m for m in ms]
    ps = list(ms)
    for _ in range(int(math.log2(CH)) - 1):
        ps = [_dot(p, p, HI3) for p in ps]
        ts = [t + _dot(t, p, HI3) for t, p in zip(ts, ps)]
    return ts


def _chunk_local(q, k, v, gcol, grow, glast, bcol, ii, jj):
    dec = jnp.where(ii >= jj, jnp.exp(jnp.minimum(gcol - grow, 0.0)), 0.0)
    eg = jnp.exp(gcol)
    kb = k * bcol
    kbg = kb * eg
    vb = v * bcol
    qt = q * eg
    kt = k * jnp.exp(glast - gcol)
    kk = _dot_nt(_bf(kb), _bf(k))
    qk = _dot_nt(_bf(q), _bf(k))
    return dec, eg, kb, kbg, vb, qt, kt, kk, qk


def _delta_fwd(q, k, v, g, beta, z, nw, *, H):
    L = q.shape[0]
    W = H * DH
    rb = TM
    nc = rb // CH
    nblk = L // rb

    def body(q_ref, k_ref, v_ref, g_ref, b_ref, z_ref, nw_ref, o_ref, y_ref, s_out, t_out,
             s_scr, u_s, w_s, qt_s, kt_s, at_s):
        @pl.when(pl.program_id(0) == 0)
        def _():
            s_scr[...] = jnp.zeros_like(s_scr)

        ii = lax.broadcasted_iota(jnp.int32, (CH, CH), 0)
        jj = lax.broadcasted_iota(jnp.int32, (CH, CH), 1)
        eye = (ii == jj).astype(f32)
        nwv = nw_ref[...]

        heads = range(H)
        hsl = [slice(hh * DH, (hh + 1) * DH) for hh in heads]

        def local(c, carry):
            r0 = pl.multiple_of(c * CH, CH)
            rows = pl.ds(r0, CH)
            gam = g_ref[rows, :]
            gam_t = gam.T
            bb = b_ref[rows, :]
            loc = [_chunk_local(q_ref[rows, hsl[hh]], k_ref[rows, hsl[hh]], v_ref[rows, hsl[hh]],
                                gam[:, hh:hh + 1], gam_t[hh:hh + 1, :], gam[CH - 1:CH, hh:hh + 1], bb[:, hh:hh + 1],
                                ii, jj) for hh in heads]
            ts = _tri_inv_many([jnp.where(ii > jj, l[7] * l[0], 0.0) for l in loc], eye)
            us = [_dot(t, l[4], HI3) for t, l in zip(ts, loc)]
            ws = [_dot(t, l[3], HI3) for t, l in zip(ts, loc)]
            for hh in heads:
                l = loc[hh]
                t_out[c, hh] = ts[hh]
                u_s[c, hh] = us[hh]
                w_s[c, hh] = _bf(ws[hh])
                qt_s[c, hh] = _bf(l[5])
                kt_s[c, hh] = _bf(l[6])
                at_s[c, hh] = _bf(l[8] * l[0])
            return carry

        lax.fori_loop(0, nc, local, 0)

        def scan(c, carry):
            r0 = pl.multiple_of(c * CH, CH)
            rows = pl.ds(r0, CH)
            gam = g_ref[rows, :]
            st = [s_scr[hh] for hh in heads]
            stb = [_bf(s) for s in st]
            vnb = [_bf(u_s[c, hh] - _dot(w_s[c, hh], stb[hh])) for hh in heads]
            snew = [st[hh] * jnp.exp(gam[CH - 1:CH, hh:hh + 1]) + _dot_tn(kt_s[c, hh], vnb[hh]) for hh in heads]
            os_ = [_dot(qt_s[c, hh], stb[hh]) + _dot(at_s[c, hh], vnb[hh]) for hh in heads]
            for hh in heads:
                o = os_[hh]
                s_out[c, hh] = st[hh]
                s_scr[hh] = snew[hh]
                o_ref[rows, hsl[hh]] = o
                on = o * lax.rsqrt(jnp.mean(o * o, axis=-1, keepdims=True) + RMS_EPS) * nwv
                zh = z_ref[rows, hsl[hh]]
                y_ref[rows, hsl[hh]] = _bf(on * (zh * _sigmoid(zh)))
            return carry

        lax.fori_loop(0, nc, scan, 0)

    row = lambda i: (i, 0)
    fix = lambda i: (0, 0)
    return pl.pallas_call(
        body, name="delta_fwd", grid=(nblk,),
        out_shape=(S((L, W), f32), S((L, W), bf16), S((L // CH, H, DH, DH), f32), S((L // CH, H, CH, CH), f32)),
        in_specs=[pl.BlockSpec((rb, W), row)] * 3 + [pl.BlockSpec((rb, LANE), row)] * 2
        + [pl.BlockSpec((rb, W), row), pl.BlockSpec((1, DH), fix)],
        out_specs=(pl.BlockSpec((rb, W), row), pl.BlockSpec((rb, W), row),
                   pl.BlockSpec((nc, H, DH, DH), lambda i: (i, 0, 0, 0)),
                   pl.BlockSpec((nc, H, CH, CH), lambda i: (i, 0, 0, 0))),
        scratch_shapes=[pltpu.VMEM((H, DH, DH), f32), pltpu.VMEM((nc, H, CH, DH), f32)]
        + [pltpu.VMEM((nc, H, CH, DH), bf16)] * 3 + [pltpu.VMEM((nc, H, CH, CH), bf16)],
        compiler_params=_cp(dimension_semantics=("arbitrary",)))(q, k, v, g, beta, z, nw)


def _delta_bwd(dy, o, z, nw, q, k, v, g, beta, s_all, t_all, *, H):
    L = q.shape[0]
    W = H * DH
    rb = TM
    nc = rb // CH
    nblk = L // rb

    def body(dy_ref, o_ref, z_ref, nw_ref, q_ref, k_ref, v_ref, g_ref, b_ref, s_ref, t_ref,
             dq_ref, dk_ref, dv_ref, dz_ref, dg_ref, db_ref, dnw_ref, ds_scr):
        @pl.when(pl.program_id(0) == 0)
        def _():
            ds_scr[...] = jnp.zeros_like(ds_scr)
            dnw_ref[...] = jnp.zeros_like(dnw_ref)

        ii = lax.broadcasted_iota(jnp.int32, (CH, CH), 0)
        jj = lax.broadcasted_iota(jnp.int32, (CH, CH), 1)
        lane = lax.broadcasted_iota(jnp.int32, (CH, LANE), 1)
        last_row = lax.broadcasted_iota(jnp.int32, (CH, 1), 0) == CH - 1
        nwv = nw_ref[...]

        def chunk(cc, carry):
            c = nc - 1 - cc
            r0 = pl.multiple_of(c * CH, CH)
            rows = pl.ds(r0, CH)
            gam = g_ref[rows, :]
            gam_t = gam.T
            bb = b_ref[rows, :]

            def head(hh):
                hs = slice(hh * DH, (hh + 1) * DH)
                gcol, grow, glast = gam[:, hh:hh + 1], gam_t[hh:hh + 1, :], gam[CH - 1:CH, hh:hh + 1]
                bcol = bb[:, hh:hh + 1]
                qh, kh, vh = q_ref[rows, hs], k_ref[rows, hs], v_ref[rows, hs]
                oh, zh, dyh = o_ref[rows, hs], z_ref[rows, hs], dy_ref[rows, hs]
                t = t_ref[c, hh]
                st = s_ref[c, hh]
                dsn = ds_scr[hh]
                rms = lax.rsqrt(jnp.mean(oh * oh, axis=-1, keepdims=True) + RMS_EPS)
                on = oh * rms
                sig = _sigmoid(zh)
                sz = zh * sig
                dz_ref[rows, hs] = dyh * on * nwv * (sig * (1.0 + zh * (1.0 - sig)))
                dnw = jnp.sum(dyh * on * sz, axis=0, keepdims=True)
                don = dyh * nwv * sz
                do = rms * (don - on * jnp.mean(don * on, axis=-1, keepdims=True))
                dec, eg, kb, kbg, vb, qt, kt, kk, qk = _chunk_local(qh, kh, vh, gcol, grow, glast, bcol, ii, jj)
                stb, dsnb, dob, tb = _bf(st), _bf(dsn), _bf(do), _bf(t)
                u = _dot(t, vb, HI3)
                w = _dot(t, kbg, HI3)
                dqt = _dot_nt(dob, stb)
                ds_new = _dot_tn(_bf(qt), dob)
                yield
                mm = jnp.where(ii > jj, kk * dec, 0.0)
                attn = qk * dec
                wb = _bf(w)
                vn = u - _dot(wb, stb)
                dvn = _dot_tn(_bf(attn), dob) + _dot(_bf(kt), dsnb)
                egl = jnp.exp(glast)
                ekt = jnp.exp(glast - gcol)
                yield
                vnb, dvnb = _bf(vn), _bf(dvn)
                dattn = jnp.where(ii >= jj, _dot_nt(dob, vnb), 0.0)
                dkt = _dot_nt(vnb, dsnb)
                ds_scr[hh] = ds_new + egl * dsn - _dot_tn(wb, dvnb)
                dw = -_dot_nt(dvnb, stb)
                dvb = _dot_tn(tb, dvnb)
                dv_ref[rows, hs] = dvb * bcol
                dt_u = _dot_nt(dvnb, _bf(vb))
                dglast = egl * jnp.sum(jnp.sum(dsn * st, axis=0, keepdims=True), axis=1, keepdims=True)
                yield
                dwb = _bf(dw)
                dkbg = _dot_tn(tb, dwb)
                dt = dt_u + _dot_nt(dwb, _bf(kbg))
                yield
                x = _dot_nt(_bf(dt), tb)
                yield
                dm = jnp.where(ii > jj, -_dot_tn(tb, _bf(x)), 0.0)
                dkk = dm * dec
                dqk = dattn * dec
                e = dm * mm + dattn * attn
                dgam = jnp.sum(e, axis=1, keepdims=True) - jnp.sum(e.T, axis=1, keepdims=True)
                dkkb, dqkb, kbf = _bf(dkk), _bf(dqk), _bf(kh)
                dkb = _dot(dkkb, kbf) + dkbg * eg
                dk_ref[rows, hs] = _dot_tn(dkkb, _bf(kb)) + _dot_tn(dqkb, _bf(qh)) + dkt * ekt + dkb * bcol
                dq_ref[rows, hs] = _dot(dqkb, kbf) + dqt * eg
                yield
                dktkt = dkt * kt
                dgam = dgam + jnp.sum(dqt * qt - dktkt + dkbg * kbg, axis=1, keepdims=True)
                dglast = dglast + jnp.sum(jnp.sum(dktkt, axis=0, keepdims=True), axis=1, keepdims=True)
                dgam = dgam + jnp.where(last_row, dglast, 0.0)
                dbeta = jnp.sum(dkb * kh + dvb * vh, axis=1, keepdims=True)
                return dgam, dbeta, dnw

            res = [None] * H
            for h0 in range(0, H, BWD_HEAD_GROUP):
                group = range(h0, min(h0 + BWD_HEAD_GROUP, H))
                gens = {hh: head(hh) for hh in group}
                while any(res[hh] is None for hh in group):
                    for hh in group:
                        try:
                            next(gens[hh])
                        except StopIteration as stop:
                            res[hh] = stop.value
            dgam_all = jnp.zeros((CH, LANE), f32)
            dbeta_all = jnp.zeros((CH, LANE), f32)
            dnw_acc = jnp.zeros((1, DH), f32)
            for hh in range(H):
                dgam, dbeta, dnw = res[hh]
                dgam_all = dgam_all + jnp.where(lane == hh, dgam, 0.0)
                dbeta_all = dbeta_all + jnp.where(lane == hh, dbeta, 0.0)
                dnw_acc = dnw_acc + dnw
            dg_ref[rows, :] = dgam_all
            db_ref[rows, :] = dbeta_all
            dnw_ref[0:1, :] += dnw_acc
            return carry

        lax.fori_loop(0, nc, chunk, 0)

    rev = lambda i: (nblk - 1 - i, 0)
    rev4 = lambda i: (nblk - 1 - i, 0, 0, 0)
    fix = lambda i: (0, 0)
    wide = pl.BlockSpec((rb, W), rev)
    thin = pl.BlockSpec((rb, LANE), rev)
    return pl.pallas_call(
        body, name="delta_bwd", grid=(nblk,),
        out_shape=(S((L, W), f32),) * 4 + (S((L, LANE), f32),) * 2 + (S((8, DH), f32),),
        in_specs=[wide, wide, wide, pl.BlockSpec((1, DH), fix), wide, wide, wide, thin, thin,
                  pl.BlockSpec((nc, H, DH, DH), rev4), pl.BlockSpec((nc, H, CH, CH), rev4)],
        out_specs=(wide,) * 4 + (thin, thin, pl.BlockSpec((8, DH), fix)),
        scratch_shapes=[pltpu.VMEM((H, DH, DH), f32)],
        compiler_params=_cp(dimension_semantics=("arbitrary",)))(dy, o, z, nw, q, k, v, g, beta, s_all, t_all)


def _sc_fwd(h, w_in, conv_w, w_out, g, b, *, first_row):
    L, D = h.shape
    W = w_out.shape[0]
    KW = conv_w.shape[0]
    tm = TM
    alpha = ALPHA

    def body(h_ref, win_hbm, cw_ref, wout_hbm, g_ref, b_ref, proj_ref, bu_ref, pre_ref, out_ref,
             win, wout, carry, sem):
        i = pl.program_id(0)
        _load_once([(win_hbm, win), (wout_hbm, wout)], sem)

        @pl.when(i == 0)
        def _():
            carry[...] = jnp.zeros_like(carry)

        hv = h_ref[...]
        hb = _bf(hv)
        bg = _dot(hb, win[:, 0:W])
        cg = _dot(hb, win[:, W:2 * W])
        xv = _dot(hb, win[:, 2 * W:3 * W])
        proj_ref[:, 0:W] = bg
        proj_ref[:, W:2 * W] = cg
        proj_ref[:, 2 * W:3 * W] = xv
        p = cg * xv
        u = _conv(cw_ref[...], _taps_back(carry[...], p, KW))
        carry[...] = p[tm - 8:tm, :]
        bu = _bf(bg * u)
        bu_ref[...] = bu
        pre = alpha * hv + _dot(bu, wout[...])
        pre_ref[...] = pre
        out_ref[...] = _ln_fwd(pre, g_ref[...], b_ref[...], _row_ids(i, tm, D), first_row)

    row = lambda i: (i, 0)
    fix = lambda i: (0, 0)
    return pl.pallas_call(
        body, name="sc_fwd", grid=(L // tm,),
        out_shape=(S((L, 3 * W), f32), S((L, W), bf16), S((L, D), f32), S((L, D), f32)),
        in_specs=[pl.BlockSpec((tm, D), row), ANY, pl.BlockSpec((KW, W), fix), ANY,
                  pl.BlockSpec((1, D), fix), pl.BlockSpec((1, D), fix)],
        out_specs=(pl.BlockSpec((tm, 3 * W), row), pl.BlockSpec((tm, W), row),
                   pl.BlockSpec((tm, D), row), pl.BlockSpec((tm, D), row)),
        scratch_shapes=[pltpu.VMEM((D, 3 * W), w_in.dtype), pltpu.VMEM((W, D), w_out.dtype),
                        pltpu.VMEM((8, W), f32), pltpu.SemaphoreType.DMA((2,))],
        compiler_params=_cp(dimension_semantics=("arbitrary",)))(h, w_in, conv_w, w_out, g, b)


def _sc_bwd(dbu, proj, conv_w):
    L, W = dbu.shape
    KW = conv_w.shape[0]
    tm = TM
    nb = L // tm

    def body(dbu_ref, proj_ref, hc_ref, hx_ref, cw_ref, dproj_ref, dcw_ref, carry):
        i = pl.program_id(0)
        blk = nb - 1 - i

        @pl.when(i == 0)
        def _():
            carry[...] = jnp.zeros_like(carry)
            dcw_ref[...] = jnp.zeros_like(dcw_ref)

        bg, cg, xv = proj_ref[:, 0:W], proj_ref[:, W:2 * W], proj_ref[:, 2 * W:3 * W]
        p = cg * xv
        u = _conv(cw_ref[...], _taps_back(jnp.where(blk > 0, hc_ref[...] * hx_ref[...], 0.0), p, KW))
        d = dbu_ref[...]
        dproj_ref[:, 0:W] = _bf(d * u)
        du = d * bg
        ahead = _taps_ahead(du, carry[...], KW)
        carry[...] = du[0:8, :]
        dp = _conv(cw_ref[...], ahead)
        for j in range(KW):
            dcw_ref[j:j + 1, :] += jnp.sum(ahead[j] * p, axis=0, keepdims=True)
        dproj_ref[:, W:2 * W] = _bf(dp * xv)
        dproj_ref[:, 2 * W:3 * W] = _bf(dp * cg)

    rev = lambda i: (nb - 1 - i, 0)
    fix = lambda i: (0, 0)

    def halo(col):
        return pl.BlockSpec((8, W), lambda i: (jnp.maximum((nb - 1 - i) * (tm // 8) - 1, 0), col))

    return pl.pallas_call(
        body, name="sc_bwd", grid=(nb,), out_shape=(S((L, 3 * W), bf16), S((8, W), f32)),
        in_specs=[pl.BlockSpec((tm, W), rev), pl.BlockSpec((tm, 3 * W), rev), halo(1), halo(2),
                  pl.BlockSpec((KW, W), fix)],
        out_specs=(pl.BlockSpec((tm, 3 * W), rev), pl.BlockSpec((8, W), fix)),
        scratch_shapes=[pltpu.VMEM((8, W), f32)],
        compiler_params=_cp(dimension_semantics=("arbitrary",)))(dbu, proj, proj, proj, conv_w)


def _ffn_cols(F):
    fc = F
    for cand in (1408, 1024, 512, 256, 128):
        if F % cand == 0:
            fc = cand
            break
    return fc


def _ffn_fwd(h, w_up, conv_w, w_down, g, b, *, first_row, name):
    L, D = h.shape
    F = w_down.shape[0]
    KW = conv_w.shape[0]
    tm = TM
    fc = _ffn_cols(F)
    alpha = ALPHA

    def body(h_ref, wup_hbm, cw_ref, wdn_hbm, g_ref, b_ref, up_ref, a_ref, pre_ref, out_ref,
             wup, wdn, carry, sem):
        i = pl.program_id(0)
        _load_once([(wup_hbm, wup), (wdn_hbm, wdn)], sem)

        @pl.when(i == 0)
        def _():
            carry[...] = jnp.zeros_like(carry)

        hv = h_ref[...]
        hb = _bf(hv)
        pre = alpha * hv
        for c0 in range(0, F, fc):
            cs = slice(c0, c0 + fc)
            u = _dot(hb, wup[:, cs])
            gate = _dot(hb, wup[:, F + c0:F + c0 + fc])
            up_ref[:, cs] = u
            up_ref[:, F + c0:F + c0 + fc] = gate
            uc = _conv(cw_ref[:, cs], _taps_back(carry[:, cs], u, KW))
            carry[:, cs] = u[tm - 8:tm, :]
            ab = _bf(uc * _sigmoid(uc) * gate)
            a_ref[:, cs] = ab
            pre = pre + _dot(ab, wdn[cs, :])
        pre_ref[...] = pre
        out_ref[...] = _ln_fwd(pre, g_ref[...], b_ref[...], _row_ids(i, tm, D), first_row)

    row = lambda i: (i, 0)
    fix = lambda i: (0, 0)
    return pl.pallas_call(
        body, name=name, grid=(L // tm,),
        out_shape=(S((L, 2 * F), f32), S((L, F), bf16), S((L, D), f32), S((L, D), f32)),
        in_specs=[pl.BlockSpec((tm, D), row), ANY, pl.BlockSpec((KW, F), fix), ANY,
                  pl.BlockSpec((1, D), fix), pl.BlockSpec((1, D), fix)],
        out_specs=(pl.BlockSpec((tm, 2 * F), row), pl.BlockSpec((tm, F), row),
                   pl.BlockSpec((tm, D), row), pl.BlockSpec((tm, D), row)),
        scratch_shapes=[pltpu.VMEM((D, 2 * F), w_up.dtype), pltpu.VMEM((F, D), w_down.dtype),
                        pltpu.VMEM((8, F), f32), pltpu.SemaphoreType.DMA((2,))],
        compiler_params=_cp(dimension_semantics=("arbitrary",)))(h, w_up, conv_w, w_down, g, b)


def _ffn_bwd(dpre, up, w_down, conv_w, *, name):
    L, D = dpre.shape
    F = w_down.shape[0]
    KW = conv_w.shape[0]
    tm = TM
    nb = L // tm
    fc = _ffn_cols(F)

    def body(dpre_ref, up_ref, halo_ref, wdn_hbm, cw_ref, dup_ref, dcw_ref, wdn, carry, sem):
        i = pl.program_id(0)
        blk = nb - 1 - i
        _load_once([(wdn_hbm, wdn)], sem)

        @pl.when(i == 0)
        def _():
            carry[...] = jnp.zeros_like(carry)
            dcw_ref[...] = jnp.zeros_like(dcw_ref)

        db = _bf(dpre_ref[...])
        for c0 in range(0, F, fc):
            cs = slice(c0, c0 + fc)
            da = _dot_nt(db, wdn[cs, :])
            gate = up_ref[:, F + c0:F + c0 + fc]
            u = up_ref[:, cs]
            uc = _conv(cw_ref[:, cs], _taps_back(jnp.where(blk > 0, halo_ref[:, cs], 0.0), u, KW))
            sig = _sigmoid(uc)
            dup_ref[:, F + c0:F + c0 + fc] = _bf(da * (uc * sig))
            duc = da * gate * (sig * (1.0 + uc * (1.0 - sig)))
            ahead = _taps_ahead(duc, carry[:, cs], KW)
            carry[:, cs] = duc[0:8, :]
            dup_ref[:, cs] = _bf(_conv(cw_ref[:, cs], ahead))
            for j in range(KW):
                dcw_ref[j:j + 1, cs] += jnp.sum(ahead[j] * u, axis=0, keepdims=True)

    rev = lambda i: (nb - 1 - i, 0)
    fix = lambda i: (0, 0)
    return pl.pallas_call(
        body, name=name, grid=(nb,), out_shape=(S((L, 2 * F), bf16), S((8, F), f32)),
        in_specs=[pl.BlockSpec((tm, D), rev), pl.BlockSpec((tm, 2 * F), rev),
                  pl.BlockSpec((8, F), lambda i: (jnp.maximum((nb - 1 - i) * (tm // 8) - 1, 0), 0)),
                  ANY, pl.BlockSpec((KW, F), fix)],
        out_specs=(pl.BlockSpec((tm, 2 * F), rev), pl.BlockSpec((8, F), fix)),
        scratch_shapes=[pltpu.VMEM((F, D), w_down.dtype), pltpu.VMEM((8, F), f32), pltpu.SemaphoreType.DMA((1,))],
        compiler_params=_cp(dimension_semantics=("arbitrary",)))(dpre, up, up, w_down, conv_w)


def _loss_head(h, target, pre, g, *, first_row):
    L, D = h.shape
    tm = TM
    pb = PADF // tm

    def body(h_ref, t_ref, pre_ref, g_ref, dpre_ref, dg_ref, db_ref, loss_ref):
        i = pl.program_id(0)

        @pl.when(i == 0)
        def _():
            loss_ref[...] = jnp.zeros_like(loss_ref)
            dg_ref[...] = jnp.zeros_like(dg_ref)
            db_ref[...] = jnp.zeros_like(db_ref)

        valid = i >= pb
        err = h_ref[...] - t_ref[...]
        dh = jnp.where(valid, err * (1.0 / D), 0.0)
        part = 0.5 * jnp.sum(jnp.sum(err * err, axis=-1, keepdims=True) * (1.0 / D), axis=0, keepdims=True)
        loss_ref[...] += jnp.where(valid, part, 0.0)
        dpre, dg, db = _ln_bwd_rows(dh, pre_ref[...], g_ref[...], _row_ids(i, tm, D), first_row)
        dpre_ref[...] = dpre
        dg_ref[0:1, :] += dg
        db_ref[0:1, :] += db

    row = lambda i: (i, 0)
    fix = lambda i: (0, 0)
    return pl.pallas_call(
        body, name="loss_head", grid=(L // tm,),
        out_shape=(S((L, D), f32), S((8, D), f32), S((8, D), f32), S((8, LANE), f32)),
        in_specs=[pl.BlockSpec((tm, D), row), pl.BlockSpec((tm, D), lambda i: (jnp.maximum(i - pb, 0), 0)),
                  pl.BlockSpec((tm, D), row), pl.BlockSpec((1, D), fix)],
        out_specs=(pl.BlockSpec((tm, D), row), pl.BlockSpec((8, D), fix), pl.BlockSpec((8, D), fix),
                   pl.BlockSpec((8, LANE), fix)),
        compiler_params=_cp(dimension_semantics=("arbitrary",)))(h, target, pre, g)


def _adamw(g_terms, w, m, v, *, name):
    R, C = w.shape
    tr = _row_tile(R)
    n = len(g_terms)
    c1 = 1.0 - ADAM_B1 ** ADAM_STEP
    c2 = 1.0 - ADAM_B2 ** ADAM_STEP

    def body(*refs):
        g = refs[0][...].astype(f32)
        for r in refs[1:n]:
            g = g + r[...].astype(f32)
        w_ref, m_ref, v_ref, g_out, d_out, m_out, v_out = refs[n:]
        mn = ADAM_B1 * m_ref[...] + (1.0 - ADAM_B1) * g
        vn = ADAM_B2 * v_ref[...] + (1.0 - ADAM_B2) * (g * g)
        g_out[...] = g
        m_out[...] = mn
        v_out[...] = vn
        d_out[...] = -ADAM_LR * ((mn / c1) / (jnp.sqrt(vn / c2) + ADAM_EPS) + ADAM_WD * w_ref[...])

    spec = pl.BlockSpec((tr, C), lambda i: (i, 0))
    return pl.pallas_call(
        body, name=name, grid=(R // tr,), out_shape=(S((R, C), f32),) * 4,
        in_specs=[spec] * (n + 3), out_specs=(spec,) * 4,
        compiler_params=_cp(dimension_semantics=("arbitrary",)))(*g_terms, w, m, v)


def _sum_devices(x):
    n, R, C = x.shape

    def body(x_ref, o_ref):
        acc = x_ref[0]
        for d in range(1, n):
            acc = acc + x_ref[d]
        o_ref[...] = acc

    return pl.pallas_call(body, name="sum_devices", out_shape=S((R, C), f32), compiler_params=_cp())(x)


def _row_tile(R):
    for step in (16, 8):
        for t in range(256, 0, -step):
            if R % t == 0:
                return t
    return R


def _pair_add(s32, recv1, sidx, ridx, *, name):
    _, L, K, n = s32.shape
    nj = sidx.shape[0]
    tk = _row_tile(K)

    def body(sidx_ref, ridx_ref, a_ref, b_ref, o_ref):
        o_ref[...] = _bf(a_ref[...] + b_ref[...].astype(f32))

    blk = (1, 1, tk, n)
    grid_spec = pltpu.PrefetchScalarGridSpec(
        num_scalar_prefetch=2, grid=(nj, L, K // tk),
        in_specs=[pl.BlockSpec(blk, lambda j, l, i, si, ri: (si[j], l, i, 0)),
                  pl.BlockSpec(blk, lambda j, l, i, si, ri: (ri[j], l, i, 0))],
        out_specs=pl.BlockSpec(blk, lambda j, l, i, si, ri: (j, l, i, 0)))
    return pl.pallas_call(
        body, name=name, grid_spec=grid_spec, out_shape=S((nj, L, K, n), bf16),
        compiler_params=_cp(dimension_semantics=("arbitrary",) * 3))(sidx, ridx, s32, recv1)


def _adamw_shard(s32, recv1, recv2, w, m, v, idx, *, name):
    L, K, n = w.shape
    tk = _row_tile(K)
    c1 = 1.0 - ADAM_B1 ** ADAM_STEP
    c2 = 1.0 - ADAM_B2 ** ADAM_STEP

    def body(idx_ref, a_ref, b_ref, r_ref, w_ref, m_ref, v_ref, g_out, d_out, m_out, v_out):
        g = a_ref[0] + b_ref[0].astype(f32)
        for j in range(3):
            g = g + r_ref[j].astype(f32)
        mn = ADAM_B1 * m_ref[...] + (1.0 - ADAM_B1) * g
        vn = ADAM_B2 * v_ref[...] + (1.0 - ADAM_B2) * (g * g)
        g_out[...] = g
        m_out[...] = mn
        v_out[...] = vn
        d_out[...] = -ADAM_LR * ((mn / c1) / (jnp.sqrt(vn / c2) + ADAM_EPS) + ADAM_WD * w_ref[...])

    own = pl.BlockSpec((1, tk, n), lambda l, i, ix: (l, i, 0))
    grid_spec = pltpu.PrefetchScalarGridSpec(
        num_scalar_prefetch=1, grid=(L, K // tk),
        in_specs=[pl.BlockSpec((1, 1, tk, n), lambda l, i, ix: (ix[0], l, i, 0)),
                  pl.BlockSpec((1, 1, tk, n), lambda l, i, ix: (ix[1], l, i, 0)),
                  pl.BlockSpec((3, 1, tk, n), lambda l, i, ix: (0, l, i, 0)), own, own, own],
        out_specs=(own,) * 4)
    return pl.pallas_call(
        body, name=name, grid_spec=grid_spec, out_shape=(S((L, K, n), f32),) * 4,
        compiler_params=_cp(dimension_semantics=("arbitrary", "arbitrary")))(idx, s32, recv1, recv2, w, m, v)


def _adamw_direct(s32s, recvs, w, m, v, me, *, name):
    L, K, n = w.shape
    tk = _row_tile(K)
    c1 = 1.0 - ADAM_B1 ** ADAM_STEP
    c2 = 1.0 - ADAM_B2 ** ADAM_STEP

    def body(me_ref, *refs):
        own_refs, recv_refs = refs[:L], refs[L:2 * L]
        w_ref, m_ref, v_ref, g_out, d_out, m_out, v_out = refs[2 * L:]
        for li in range(L):
            @pl.when(pl.program_id(0) == li)
            def _(li=li):
                g = own_refs[li][0, 0]
                for d in range(N_DEV):
                    g = g + recv_refs[li][d, 0].astype(f32)
                mn = ADAM_B1 * m_ref[0] + (1.0 - ADAM_B1) * g
                vn = ADAM_B2 * v_ref[0] + (1.0 - ADAM_B2) * (g * g)
                g_out[0] = g
                m_out[0] = mn
                v_out[0] = vn
                d_out[0] = -ADAM_LR * ((mn / c1) / (jnp.sqrt(vn / c2) + ADAM_EPS) + ADAM_WD * w_ref[0])

    own = pl.BlockSpec((1, tk, n), lambda l, i, ix: (l, i, 0))
    grid_spec = pltpu.PrefetchScalarGridSpec(
        num_scalar_prefetch=1, grid=(L, K // tk),
        in_specs=[pl.BlockSpec((1, 1, tk, n), lambda l, i, ix: (ix[0], 0, i, 0))] * L
        + [pl.BlockSpec((N_DEV, 1, tk, n), lambda l, i, ix: (0, 0, i, 0))] * L + [own, own, own],
        out_specs=(own,) * 4)
    return pl.pallas_call(
        body, name=name, grid_spec=grid_spec, out_shape=(S((L, K, n), f32),) * 4,
        compiler_params=_cp(dimension_semantics=("arbitrary", "arbitrary")))(me, *s32s, *recvs, w, m, v)


def _col_segments(n, mapping):
    segs = []
    for p in range(N_DEV):
        lo, hi = p * n, (p + 1) * n
        out = []
        for c0, c1, e0 in mapping:
            a, b = max(lo, c0), min(hi, c1)
            if a < b:
                out.append((a - lo, e0 + (a - c0), b - a))
        segs.append(out)
    return segs


def _assemble_cols(gathered, mapping, n_out, *, name):
    _, L, K, n = gathered.shape
    tk = _row_tile(K)
    segs = _col_segments(n, mapping)
    covered = sum(w for s in segs for (_, _, w) in s)

    def body(g_ref, o_ref):
        if covered != n_out:
            o_ref[...] = jnp.zeros_like(o_ref)
        for p in range(N_DEV):
            for s0, d0, w in segs[p]:
                o_ref[0, :, d0:d0 + w] = g_ref[p, 0, :, s0:s0 + w]

    return pl.pallas_call(
        body, name=name, grid=(L, K // tk), out_shape=S((L, K, n_out), gathered.dtype),
        in_specs=[pl.BlockSpec((N_DEV, 1, tk, n), lambda l, i: (0, l, i, 0))],
        out_specs=pl.BlockSpec((1, tk, n_out), lambda l, i: (l, i, 0)),
        compiler_params=_cp(dimension_semantics=("arbitrary", "arbitrary")))(gathered)


def _split_cols(dws, mapping, n, *, name):
    L = len(dws)
    K, n_in = dws[0].shape
    tk = _row_tile(K)
    segs = _col_segments(n, mapping)

    def body(*refs):
        ins, o32, o16 = refs[:L], refs[L], refs[L + 1]
        for li in range(L):
            @pl.when(pl.program_id(0) == li)
            def _(li=li):
                for p in range(N_DEV):
                    for s0, d0, w in segs[p]:
                        val = ins[li][:, d0:d0 + w]
                        o32[p, 0, :, s0:s0 + w] = val
                        o16[p, 0, :, s0:s0 + w] = _bf(val)

    out = pl.BlockSpec((N_DEV, 1, tk, n), lambda l, i: (0, l, i, 0))
    return pl.pallas_call(
        body, name=name, grid=(L, K // tk), out_shape=(S((N_DEV, L, K, n), f32), S((N_DEV, L, K, n), bf16)),
        in_specs=[pl.BlockSpec((tk, n_in), lambda l, i: (i, 0))] * L, out_specs=(out, out),
        compiler_params=_cp(dimension_semantics=("arbitrary", "arbitrary")))(*dws)


def _split_rows(dws, k, *, name):
    L = len(dws)
    N = dws[0].shape[1]

    def body(*refs):
        ins, o32, o16 = refs[:L], refs[L], refs[L + 1]
        for li in range(L):
            @pl.when(pl.program_id(0) == li)
            def _(li=li):
                val = ins[li][...]
                o32[0, 0] = val
                o16[0, 0] = _bf(val)

    out = pl.BlockSpec((1, 1, k, N), lambda l, p: (p, l, 0, 0))
    return pl.pallas_call(
        body, name=name, grid=(L, N_DEV), out_shape=(S((N_DEV, L, k, N), f32), S((N_DEV, L, k, N), bf16)),
        in_specs=[pl.BlockSpec((k, N), lambda l, p: (p, 0))] * L, out_specs=(out, out),
        compiler_params=_cp(dimension_semantics=("arbitrary", "arbitrary")))(*dws)


def _rows_full(gathered):
    _, L, k, N = gathered.shape
    return jnp.transpose(gathered, (1, 0, 2, 3)).reshape(L, N_DEV * k, N)


def _peer(rel):
    x, y, c = lax.axis_index("x"), lax.axis_index("y"), lax.axis_index("c")
    return {"c": (x, y, 1 - c), "x": (1 - x, y, c), "y": (x, 1 - y, c), "xy": (1 - x, 1 - y, c)}[rel]


def _all_gather(xs, *, name):
    na = len(xs)

    def body(*refs):
        x_refs, out_refs = refs[:na], refs[na:2 * na]
        send_sems, recv_sems, local_sems = refs[2 * na:]
        mx, my, mc = lax.axis_index("x"), lax.axis_index("y"), lax.axis_index("c")
        me, sibling = (mx, my, mc), (mx, my, 1 - mc)
        chips = [(1 - mx, my), (mx, 1 - my), (1 - mx, 1 - my)]

        def slot(a, px, py, pc):
            return out_refs[a].at[4 * px + 2 * py + pc]

        def copy(a, kk, block, to, src=None):
            return pltpu.make_async_remote_copy(
                src_ref=slot(a, *block) if src is None else src, dst_ref=slot(a, *block),
                send_sem=send_sems.at[7 * a + kk], recv_sem=recv_sems.at[7 * a + kk], device_id=to, device_id_type=MESH)

        mine = [pltpu.make_async_copy(x_refs[a], slot(a, *me), local_sems.at[a]) for a in range(na)]
        for cp in mine:
            cp.start()
        first = []
        for a in range(na):
            first.append(copy(a, 0, me, sibling, src=x_refs[a]))
            first += [copy(a, 1 + j, me, (*chip, mc), src=x_refs[a]) for j, chip in enumerate(chips)]
        for cp in first:
            cp.start()
        passed = []
        for j, chip in enumerate(chips):
            for a in range(na):
                copy(a, 1 + j, (*chip, mc), me).wait_recv()
                fwd = copy(a, 4 + j, (*chip, mc), sibling)
                fwd.start()
                passed.append(fwd)
        for a in range(na):
            copy(a, 0, sibling, me).wait_recv()
            for j, chip in enumerate(chips):
                copy(a, 4 + j, (*chip, 1 - mc), me).wait_recv()
        for cp in first + passed:
            cp.wait_send()
        for cp in mine:
            cp.wait()

    return pl.pallas_call(
        body, name=name, out_shape=tuple(S((N_DEV,) + x.shape, x.dtype) for x in xs),
        in_specs=[ANY] * na, out_specs=(ANY,) * na,
        scratch_shapes=[pltpu.SemaphoreType.DMA((7 * na,)), pltpu.SemaphoreType.DMA((7 * na,)),
                        pltpu.SemaphoreType.DMA((na,))],
        compiler_params=pltpu.CompilerParams(has_side_effects=True))(*xs)


def _exchange(sends, n_slots, src_index, rels, *, name):
    na = len(sends)

    def body(*refs):
        send_refs, recv_refs = refs[:na], refs[na:2 * na]
        send_sems, recv_sems = refs[2 * na:]
        cps = [pltpu.make_async_remote_copy(
            src_ref=send_refs[a].at[src_index(j)], dst_ref=recv_refs[a].at[j],
            send_sem=send_sems.at[n_slots * a + j], recv_sem=recv_sems.at[n_slots * a + j],
            device_id=_peer(rels[j]), device_id_type=MESH) for a in range(na) for j in range(n_slots)]
        for cp in cps:
            cp.start()
        for cp in cps:
            cp.wait()

    return pl.pallas_call(
        body, name=name, out_shape=tuple(S((n_slots,) + s.shape[1:], s.dtype) for s in sends),
        in_specs=[ANY] * na, out_specs=(ANY,) * na,
        scratch_shapes=[pltpu.SemaphoreType.DMA((n_slots * na,)), pltpu.SemaphoreType.DMA((n_slots * na,))],
        compiler_params=pltpu.CompilerParams(has_side_effects=True))(*sends)


_FLIPS = [(fx, fy, fc) for fx in (0, 1) for fy in (0, 1) for fc in (0, 1)][1:]


def _flip_peer(flip):
    x, y, c = lax.axis_index("x"), lax.axis_index("y"), lax.axis_index("c")
    return tuple(1 - a if f else a for a, f in zip((x, y, c), flip))


def _dev_index(p):
    return 4 * p[0] + 2 * p[1] + p[2]


HBM_SPEC = pl.BlockSpec(memory_space=pltpu.HBM)
SEM_SPEC = pl.BlockSpec(memory_space=pltpu.SEMAPHORE)


def _direct_start(srcs, lands, per_peer, *, name):
    na = len(srcs)

    def body(*refs):
        src_refs, land_refs = refs[:na], refs[na:2 * na]
        send_sems, recv_sems = refs[2 * na], refs[2 * na + 1]
        token = refs[-1]
        me = _dev_index((lax.axis_index("x"), lax.axis_index("y"), lax.axis_index("c")))
        for a in range(na):
            for r, flip in enumerate(_FLIPS):
                peer = _flip_peer(flip)
                src = src_refs[a].at[_dev_index(peer)] if per_peer else src_refs[a]
                pltpu.make_async_remote_copy(
                    src_ref=src, dst_ref=land_refs[a].at[me], send_sem=send_sems.at[7 * a + r],
                    recv_sem=recv_sems.at[7 * a + r], device_id=peer, device_id_type=MESH).start()
        token[...] = jnp.zeros_like(token)

    hbm = lambda t: pltpu.with_memory_space_constraint(t, pltpu.HBM)
    out = pl.pallas_call(
        body, name=name,
        out_shape=(pltpu.SemaphoreType.DMA((7 * na,)), pltpu.SemaphoreType.DMA((7 * na,)))
        + tuple(pltpu.HBM(t.shape, t.dtype) for t in list(srcs) + list(lands)) + (S((8, LANE), f32),),
        in_specs=[HBM_SPEC] * (2 * na),
        out_specs=(SEM_SPEC, SEM_SPEC) + (HBM_SPEC,) * (2 * na) + (pl.BlockSpec(memory_space=pltpu.VMEM),),
        input_output_aliases={i: 2 + i for i in range(2 * na)},
        compiler_params=pltpu.CompilerParams(has_side_effects=pltpu.SideEffectType.DATAFLOW_SIDE_EFFECTING))(
            *[hbm(t) for t in srcs], *[hbm(t) for t in lands])
    return out[0], out[1], list(out[2:2 + na]), list(out[2 + na:2 + 2 * na]), out[-1]


def _direct_wait(send_sems, recv_sems, srcs, lands, per_peer, after, *, name):
    na = len(srcs)

    def body(*refs):
        src_refs, land_refs = refs[:na], refs[na:2 * na]
        ssem, rsem = refs[2 * na], refs[2 * na + 1]
        me = _dev_index((lax.axis_index("x"), lax.axis_index("y"), lax.axis_index("c")))
        for a in range(na):
            for r, flip in enumerate(_FLIPS):
                peer = _flip_peer(flip)
                src = src_refs[a].at[_dev_index(peer)] if per_peer else src_refs[a]
                cp = pltpu.make_async_remote_copy(
                    src_ref=src, dst_ref=land_refs[a].at[me], send_sem=ssem.at[7 * a + r],
                    recv_sem=rsem.at[7 * a + r], device_id=peer, device_id_type=MESH)
                cp.wait_send()
                cp.wait_recv()

    out = pl.pallas_call(
        body, name=name, out_shape=tuple(pltpu.HBM(t.shape, t.dtype) for t in list(srcs) + list(lands)),
        in_specs=[HBM_SPEC] * (2 * na) + [SEM_SPEC, SEM_SPEC, ANY], out_specs=(HBM_SPEC,) * (2 * na),
        input_output_aliases={i: i for i in range(2 * na)},
        compiler_params=pltpu.CompilerParams(has_side_effects=pltpu.SideEffectType.DATAFLOW_SIDE_EFFECTING))(
            *srcs, *lands, send_sems, recv_sems, after)
    return list(out[:na]), list(out[na:])


def _pack_small(parts, width):
    rows, offs, r = [], [], 0
    for a in parts:
        n = a.size
        nr = -(-n // width)
        flat = a.reshape(-1).astype(f32)
        if nr * width != n:
            flat = jnp.pad(flat, (0, nr * width - n))
        rows.append(flat.reshape(nr, width))
        offs.append((r, nr))
        r += nr
    buf = jnp.concatenate(rows, axis=0)
    pad = (-r) % 8
    if pad:
        buf = jnp.pad(buf, ((0, pad), (0, 0)))
    return buf, offs


def _unpack_small(buf, off, shape):
    r, nr = off
    return buf[r:r + nr].reshape(-1)[:math.prod(shape)].reshape(shape)


def _local_step(x, target, meta, a_w_in, small, start_token, late_weights, grads_ready):
    SEQ, D = x.shape
    n_meta = meta.shape[0]
    first_row = PADF - n_meta
    H = small["a_log"].shape[-1]

    h0 = jnp.concatenate([jnp.zeros((first_row, D), f32), meta, x], axis=0)

    def lanes(a):
        return jnp.pad(a.reshape(1, -1), ((0, 0), (0, LANE - a.size)))

    def after_token(a, token):
        return a if token is None else a + token[0:1, 0:1]

    alog, dtb = after_token(lanes(small["a_log"][0]), start_token), lanes(small["a_dt_bias"][0])
    a_conv, b_conv = small["a_conv"][0], small["b_conv"][0]
    nw = small["a_norm"][0].reshape(1, DH)
    lmg, lmb, lfg, lfb = small["ln_mix_g"], small["ln_mix_b"], small["ln_ffn_g"], small["ln_ffn_b"]

    pre_a, z, raw, q, k, v, beta, g = _gdn_in_fwd(h0, a_w_in, a_conv, alog, dtb, first_row=first_row, H=H)
    o, y, s_all, t_all = _delta_fwd(q, k, v, g, beta, z, nw, H=H)
    wts = late_weights(y)
    pre1, h1 = _out_res_ln(y, wts["a_w_out"], h0, lmg[0:1], lmb[0:1], first_row=first_row, name="gdn_out_ln")
    up0, act0, pre2, h2 = _ffn_fwd(h1, wts["ffn_w_up"][0], small["ffn_conv"][0], wts["ffn_w_down"][0],
                                   lfg[0:1], lfb[0:1], first_row=first_row, name="ffn_fwd0")
    proj_b, bu, pre3, h3 = _sc_fwd(h2, wts["b_w_in"], b_conv, wts["b_w_out"], lmg[1:2], lmb[1:2], first_row=first_row)
    up1, act1, pre4, h4 = _ffn_fwd(h3, wts["ffn_w_up"][1], small["ffn_conv"][1], wts["ffn_w_down"][1],
                                   lfg[1:2], lfb[1:2], first_row=first_row, name="ffn_fwd1")
    gw, gs = {}, {}
    alpha = ALPHA
    dpre4, dlfg1, dlfb1, loss_tile = _loss_head(h4, target, pre4, lfg[1:2], first_row=first_row)

    def ffn_backward(dpre, up, act, h_in, layer, tag, ln_in, token=None):
        dup, dcw = _ffn_bwd(dpre, up, wts["ffn_w_down"][layer], after_token(small["ffn_conv"][layer], token),
                            name="ffn_bwd" + tag)
        dwd = _linear_dw(act, dpre, name="dw_down" + tag)
        dwu = _linear_dw(h_in, dup, name="dw_up" + tag)
        dpre_in, dg, db = _linear_dx(dup, wts["ffn_w_up"][layer], dpre, alpha=alpha, name="dx_up" + tag,
                                     ln=ln_in + (first_row,))
        return dpre_in, dg, db, dwu, dwd, dcw[0:3]

    dpre3, dlmg1, dlmb1, dwu1, dwd1, dcf1 = ffn_backward(dpre4, up1, act1, h3, 1, "1", (pre3, lmg[1:2]))

    dbu = _linear_dx(dpre3, wts["b_w_out"], None, alpha=0.0, name="dx_b_out")
    dwb_out = _linear_dw(bu, dpre3, name="dw_b_out")
    dproj_b, dcb = _sc_bwd(dbu, proj_b, b_conv)
    dwb_in = _linear_dw(h2, dproj_b, name="dw_b_in")
    dpre2, dlfg0, dlfb0 = _linear_dx(dproj_b, wts["b_w_in"], dpre3, alpha=alpha, name="dx_b_in",
                                     ln=(pre2, lfg[0:1], first_row))
    token = grads_ready("layer1", dict(ffn_w_up=dwu1, ffn_w_down=dwd1, b_w_in=dwb_in, b_w_out=dwb_out))

    dpre1, dlmg0, dlmb0, dwu0, dwd0, dcf0 = ffn_backward(dpre2, up0, act0, h1, 0, "0", (pre1, lmg[0:1]), token)
    token = grads_ready("layer0", dict(ffn_w_up=dwu0, ffn_w_down=dwd0))

    dy = _linear_dx(dpre1, wts["a_w_out"], None, alpha=0.0, name="dx_a_out")
    gw["a_w_out"] = [_linear_dw(y, dpre1, name="dw_a_out")]
    dq, dk, dv, dz, dg_, dbeta, dnw = _delta_bwd(dy, o, z, after_token(nw, token), q, k, v, g, beta, s_all, t_all, H=H)
    dproj_a, dca, dal, ddt = _gdn_in_bwd(dq, dk, dv, dz, dg_, dbeta, pre_a, raw, a_conv, alog, dtb,
                                         first_row=first_row, H=H)
    gw["a_w_in"] = [_linear_dw(h0, dproj_a, name="dw_a_in")]
    dh0 = _linear_dx(dproj_a, a_w_in, dpre1, alpha=alpha, name="dx_a_in")

    gs["meta"] = dh0[first_row:PADF]
    gs["a_conv"] = dca[0:a_conv.shape[0]][None]
    gs["a_log"] = dal[0:1, 0:H]
    gs["a_dt_bias"] = ddt[0:1, 0:H]
    gs["a_norm"] = dnw[0:1]
    gs["b_conv"] = dcb[0:b_conv.shape[0]][None]
    gs["ln_mix_g"] = jnp.stack([dlmg0[0], dlmg1[0]])
    gs["ln_mix_b"] = jnp.stack([dlmb0[0], dlmb1[0]])
    gs["ffn_conv"] = jnp.stack([dcf0, dcf1])
    gs["ln_ffn_g"] = jnp.stack([dlfg0[0], dlfg1[0]])
    gs["ln_ffn_b"] = jnp.stack([dlfb0[0], dlfb1[0]])
    return loss_tile, dh0[PADF:], gw, gs


_BIG = ("a_w_in", "a_w_out", "b_w_in", "b_w_out", "ffn_w_up", "ffn_w_down")
_BIG_COL = ("a_w_in", "b_w_in", "ffn_w_up")
_SMALL = ("meta", "a_conv", "a_log", "a_dt_bias", "a_norm", "b_conv", "ln_mix_g", "ln_mix_b",
          "ffn_conv", "ln_ffn_g", "ln_ffn_b")
_SMALL_SHARDED = ("meta", "a_conv", "b_conv", "ffn_conv")
_ORDER = ("meta", "a_w_in", "a_conv", "a_log", "a_dt_bias", "a_norm", "a_w_out", "b_w_in", "b_conv", "b_w_out",
          "ln_mix_g", "ln_mix_b", "ffn_w_up", "ffn_conv", "ffn_w_down", "ln_ffn_g", "ln_ffn_b")


def _a_w_in_map(H):
    W4 = 4 * H * DH
    return [(0, W4, 0), (W4, W4 + H, W4), (W4 + H, W4 + 2 * H, W4 + LANE)], W4 + 2 * LANE


def kernel(x, meta, a_w_in, a_conv, a_log, a_dt_bias, a_norm, a_w_out, b_w_in, b_conv, b_w_out, ln_mix_g, ln_mix_b, ffn_w_up, ffn_conv, ffn_w_down, ln_ffn_g, ln_ffn_b, loss_target, m_meta, m_a_w_in, m_a_conv, m_a_log, m_a_dt_bias, m_a_norm, m_a_w_out, m_b_w_in, m_b_conv, m_b_w_out, m_ln_mix_g, m_ln_mix_b, m_ffn_w_up, m_ffn_conv, m_ffn_w_down, m_ln_ffn_g, m_ln_ffn_b, v_meta, v_a_w_in, v_a_conv, v_a_log, v_a_dt_bias, v_a_norm, v_a_w_out, v_b_w_in, v_b_conv, v_b_w_out, v_ln_mix_g, v_ln_mix_b, v_ffn_w_up, v_ffn_conv, v_ffn_w_down, v_ln_ffn_g, v_ln_ffn_b):
    wloc = dict(meta=meta, a_w_in=a_w_in, a_conv=a_conv, a_log=a_log, a_dt_bias=a_dt_bias, a_norm=a_norm,
                a_w_out=a_w_out, b_w_in=b_w_in, b_conv=b_conv, b_w_out=b_w_out, ln_mix_g=ln_mix_g, ln_mix_b=ln_mix_b,
                ffn_w_up=ffn_w_up, ffn_conv=ffn_conv, ffn_w_down=ffn_w_down, ln_ffn_g=ln_ffn_g, ln_ffn_b=ln_ffn_b)
    mloc = dict(meta=m_meta, a_w_in=m_a_w_in, a_conv=m_a_conv, a_log=m_a_log, a_dt_bias=m_a_dt_bias, a_norm=m_a_norm,
                a_w_out=m_a_w_out, b_w_in=m_b_w_in, b_conv=m_b_conv, b_w_out=m_b_w_out, ln_mix_g=m_ln_mix_g,
                ln_mix_b=m_ln_mix_b, ffn_w_up=m_ffn_w_up, ffn_conv=m_ffn_conv, ffn_w_down=m_ffn_w_down,
                ln_ffn_g=m_ln_ffn_g, ln_ffn_b=m_ln_ffn_b)
    vloc = dict(meta=v_meta, a_w_in=v_a_w_in, a_conv=v_a_conv, a_log=v_a_log, a_dt_bias=v_a_dt_bias, a_norm=v_a_norm,
                a_w_out=v_a_w_out, b_w_in=v_b_w_in, b_conv=v_b_conv, b_w_out=v_b_w_out, ln_mix_g=v_ln_mix_g,
                ln_mix_b=v_ln_mix_b, ffn_w_up=v_ffn_w_up, ffn_conv=v_ffn_conv, ffn_w_down=v_ffn_w_down,
                ln_ffn_g=v_ln_ffn_g, ln_ffn_b=v_ln_ffn_b)
    H = a_log.shape[-1]
    mx, my, mc = lax.axis_index("x"), lax.axis_index("y"), lax.axis_index("c")
    me = 4 * mx + 2 * my + mc

    a_map, a_cols = _a_w_in_map(H)
    col_maps = {"a_w_in": (a_map, a_cols)}
    for n in ("b_w_in", "ffn_w_up"):
        ncols = N_DEV * wloc[n].shape[-1]
        col_maps[n] = ([(0, ncols, 0)], ncols)
    sm_sh = [wloc[n] for n in _SMALL_SHARDED]
    sbuf, soffs = _pack_small(sm_sh, 128)
    g_a_w_in, sg = _all_gather([_bf(wloc["a_w_in"]), sbuf], name="gather_first")
    w_a_in = _assemble_cols(g_a_w_in, *col_maps["a_w_in"], name="assemble_a_w_in")[0]
    late = [n for n in _BIG if n != "a_w_in"]
    ssem, rsem, srcs_t, lands_t, start_token = _direct_start(
        [_bf(wloc[n]) for n in late], [lax.empty((N_DEV,) + wloc[n].shape, bf16) for n in late], False,
        name="gather_rest_start")

    def late_weights(after):
        srcs_d, landed = _direct_wait(ssem, rsem, srcs_t, lands_t, False, after, name="gather_rest_wait")
        wts = {}
        for n, own, got in zip(late, srcs_d, landed):
            full = lax.dynamic_update_index_in_dim(got, own, me, 0)
            if n in _BIG_COL:
                wts[n] = _assemble_cols(full, *col_maps[n], name="assemble_" + n)
            else:
                wts[n] = _rows_full(full)
        for n in ("a_w_out", "b_w_in", "b_w_out"):
            wts[n] = wts[n][0]
        return wts

    small = {n: wloc[n] for n in _SMALL}
    for n, off in zip(_SMALL_SHARDED, soffs):
        sh = wloc[n].shape
        parts = jnp.stack([_unpack_small(sg[d], off, sh) for d in range(N_DEV)])
        nd = len(sh)
        small[n] = jnp.transpose(parts, tuple(range(1, nd)) + (0, nd)).reshape(sh[:-1] + (N_DEV * sh[-1],))

    def split(n, dws, tag):
        if n in _BIG_COL:
            return _split_cols(dws, col_maps[n][0], wloc[n].shape[-1], name="split_" + n + tag)
        return _split_rows(dws, wloc[n].shape[-2], name="split_" + n + tag)

    sent = {}

    def grads_ready(stage, grads):
        names = sorted(grads)
        parts = [split(n, [grads[n]], "_" + stage) for n in names]
        handles = _direct_start([p[1] for p in parts], [jnp.zeros(p[1].shape, bf16) for p in parts], True,
                                name="grads_" + stage + "_start")
        sent[stage] = (names, [p[0] for p in parts], handles)
        return handles[4]

    loss_tile, grad_x, gw, gs = _local_step(x[0], loss_target[0], small["meta"], w_a_in, small, start_token,
                                            late_weights, grads_ready)

    last = ("a_w_in", "a_w_out")
    s32, s16 = {}, {}
    for n in last:
        s32[n], s16[n] = split(n, gw[n], "")
    recv1 = dict(zip(last, _exchange([s16[n] for n in last], 4, lambda j: 2 * j + (1 - lax.axis_index("c")),
                                     ["c"] * 4, name="grad_to_sibling")))
    chip = 2 * mx + my
    others = [2 * (1 - mx) + my, 2 * mx + (1 - my), 2 * (1 - mx) + (1 - my)]
    sidx = jnp.stack([2 * o + mc for o in others]).astype(jnp.int32)
    ridx = jnp.stack(others).astype(jnp.int32)
    to_send = [_pair_add(s32[n], recv1[n], sidx, ridx, name="pair_add_" + n) for n in last]
    recv2 = dict(zip(last, _exchange(to_send, 3, lambda j: j, ["x", "y", "xy"], name="grad_to_chips")))
    own = jnp.stack([me, chip]).astype(jnp.int32)
    big_out = {n: _adamw_shard(s32[n], recv1[n], recv2[n], wloc[n], mloc[n], vloc[n], own, name="adamw_" + n)
               for n in last}

    got = {}
    for stage, (names, own32, (ssem_g, rsem_g, srcs_g, lands_g, _)) in sent.items():
        _, landed = _direct_wait(ssem_g, rsem_g, srcs_g, lands_g, True, recv2["a_w_in"], name="grads_" + stage + "_wait")
        for n, o32, r in zip(names, own32, landed):
            got.setdefault(n, []).append((stage, o32, r))
    me1 = jnp.stack([me]).astype(jnp.int32)
    for n, parts in got.items():
        parts = sorted(parts, key=lambda t: t[0])
        big_out[n] = _adamw_direct([p[1] for p in parts], [p[2] for p in parts], wloc[n], mloc[n], vloc[n], me1,
                                   name="adamw_" + n)

    names = list(_SMALL)
    pbuf, poffs = _pack_small([gs[n] for n in names] + [loss_tile[0:1, 0:1]], 1024)
    psum = _sum_devices(_all_gather([pbuf], name="gather_small_grads")[0])
    loss = psum[poffs[-1][0], 0]
    g_small = {}
    for n, off in zip(names, poffs[:-1]):
        full_shape = gs[n].shape
        gfull = _unpack_small(psum, off, full_shape)
        if n in _SMALL_SHARDED:
            ns = wloc[n].shape[-1]
            gfull = lax.dynamic_slice_in_dim(gfull, me * ns, ns, axis=gfull.ndim - 1)
        g_small[n] = gfull.reshape(wloc[n].shape)
    gbuf, aoffs = _pack_small([g_small[n] for n in names], 128)
    wbuf, _ = _pack_small([wloc[n] for n in names], 128)
    mbuf, _ = _pack_small([mloc[n] for n in names], 128)
    vbuf, _ = _pack_small([vloc[n] for n in names], 128)
    _, d_s, m_s, v_s = _adamw([gbuf], wbuf, mbuf, vbuf, name="adamw_small")

    grads, deltas, new_m, new_v = {}, {}, {}, {}
    for n in _BIG:
        grads[n], deltas[n], new_m[n], new_v[n] = big_out[n]
    for n, off in zip(names, aoffs):
        sh = wloc[n].shape
        grads[n] = g_small[n]
        deltas[n], new_m[n], new_v[n] = (_unpack_small(b_, off, sh) for b_ in (d_s, m_s, v_s))
    return (loss, grad_x[None], *[grads[n] for n in _ORDER], *[deltas[n] for n in _ORDER],
            *[new_m[n] for n in _ORDER], *[new_v[n] for n in _ORDER])
```

```python
import math

import jax
import jax.numpy as jnp
from jax import lax
from jax.experimental import pallas as pl
from jax.experimental.pallas import tpu as pltpu

f32, bf16 = jnp.float32, jnp.bfloat16
S = jax.ShapeDtypeStruct
HI = lax.Precision.HIGHEST
HI3 = lax.Precision.HIGH
MESH = pl.DeviceIdType.MESH

V7X_VMEM_LIMIT = 56 * 1024 * 1024
LANE = 128
DH = 128
CH = 64
PADF = 256
TM = 256
TMM = 768
N_DEV = 8
BWD_HEAD_GROUP = 4

DEPTH = 2
ALPHA = (2.0 * DEPTH) ** 0.25
LN_EPS = 1e-5
RMS_EPS = 1e-6
L2_EPS = 1e-6
ADAM_LR, ADAM_B1, ADAM_B2, ADAM_EPS, ADAM_WD, ADAM_STEP = 0.001, 0.9, 0.999, 1e-08, 0.01, 10


def _cp(**kw):
    return pltpu.CompilerParams(vmem_limit_bytes=V7X_VMEM_LIMIT, **kw)


def _bf(x):
    return x.astype(bf16)


def _dot(a, b, precision=None):
    return jnp.dot(a, b, preferred_element_type=f32, precision=precision)


def _dot_nt(a, b):
    return lax.dot_general(a, b, (((1,), (1,)), ((), ())), preferred_element_type=f32)


def _dot_tn(a, b):
    return lax.dot_general(a, b, (((0,), (0,)), ((), ())), preferred_element_type=f32)


def _sigmoid(x):
    return 1.0 / (1.0 + jnp.exp(-x))


def _load_once(pairs, sem):
    @pl.when(pl.program_id(0) == 0)
    def _():
        cps = [pltpu.make_async_copy(src, dst, sem.at[n]) for n, (src, dst) in enumerate(pairs)]
        for c in cps:
            c.start()
        for c in cps:
            c.wait()


def _row_ids(i, tm, width):
    return i * tm + lax.broadcasted_iota(jnp.int32, (tm, width), 0)


def _ln_fwd(pre, g, b, rows, first_row):
    mu = jnp.mean(pre, axis=-1, keepdims=True)
    xc = pre - mu
    var = jnp.mean(xc * xc, axis=-1, keepdims=True)
    y = xc * lax.rsqrt(var + LN_EPS) * g + b
    return jnp.where(rows >= first_row, y, 0.0)


ANY = pl.BlockSpec(memory_space=pl.ANY)


def _taps_back(prev8, x, kw):
    xe = jnp.concatenate([prev8, x], axis=0)
    return [pltpu.roll(xe, kw - 1 - j, 0)[8:] for j in range(kw - 1)] + [x]


def _taps_ahead(x, next8, kw):
    n = x.shape[0]
    xe = jnp.concatenate([x, next8], axis=0)
    return [pltpu.roll(xe, n + 8 - (kw - 1 - j), 0)[:n] for j in range(kw - 1)] + [x]


def _conv(cw, taps):
    acc = cw[0:1, :] * taps[0]
    for j in range(1, len(taps)):
        acc = acc + cw[j:j + 1, :] * taps[j]
    return acc


def _linear_dw(x, dy, *, name):
    L, K = x.shape
    N = dy.shape[1]
    tm = TMM if L % TMM == 0 else TM
    tn = LANE
    for d in range(N // LANE, 0, -1):
        if (N // LANE) % d == 0 and K * d * LANE * 4 <= 9 * 1024 * 1024:
            tn = d * LANE
            break

    def body(x_ref, dy_ref, o_ref):
        @pl.when(pl.program_id(1) == 0)
        def _():
            o_ref[...] = jnp.zeros_like(o_ref)
        o_ref[...] += _dot_tn(_bf(x_ref[...]), _bf(dy_ref[...]))

    return pl.pallas_call(
        body, name=name, grid=(N // tn, L // tm), out_shape=S((K, N), f32),
        in_specs=[pl.BlockSpec((tm, K), lambda j, i: (i, 0)), pl.BlockSpec((tm, tn), lambda j, i: (i, j))],
        out_specs=pl.BlockSpec((K, tn), lambda j, i: (0, j)),
        compiler_params=_cp(dimension_semantics=("arbitrary", "arbitrary")))(x, dy)


def _ln_bwd_rows(dout, pre, g, rows, first_row):
    mu = jnp.mean(pre, axis=-1, keepdims=True)
    xc = pre - mu
    rstd = lax.rsqrt(jnp.mean(xc * xc, axis=-1, keepdims=True) + LN_EPS)
    xh = xc * rstd
    dy = jnp.where(rows >= first_row, dout, 0.0)
    dxh = dy * g
    dpre = rstd * (dxh - jnp.mean(dxh, axis=-1, keepdims=True) - xh * jnp.mean(dxh * xh, axis=-1, keepdims=True))
    return dpre, jnp.sum(dy * xh, axis=0, keepdims=True), jnp.sum(dy, axis=0, keepdims=True)


def _linear_dx(dy, w, res, *, alpha, name, ln=None, front_rows=0):
    L, N = dy.shape
    K = w.shape[0]
    tm = TM
    has_res = res is not None
    first_row = ln[2] if ln else 0
    fb = front_rows // tm

    def body(*refs):
        refs = list(refs)
        dy_ref, w_hbm = refs[:2]
        res_ref = refs.pop(2) if has_res else None
        if ln:
            pre_ref, g_ref, o_ref, dg_ref, db_ref, w_vmem, sem = refs[2:]
        elif fb:
            o_ref, front_ref, w_vmem, sem = refs[2:]
        else:
            o_ref, w_vmem, sem = refs[2:]
        _load_once([(w_hbm, w_vmem)], sem)
        acc = _dot_nt(_bf(dy_ref[...]), w_vmem[...])
        if has_res:
            acc = acc + alpha * res_ref[...]
        if not ln:
            o_ref[...] = acc
            if fb:
                @pl.when(pl.program_id(0) < fb)
                def _():
                    front_ref[...] = acc
            return
        i = pl.program_id(0)

        @pl.when(i == 0)
        def _():
            dg_ref[...] = jnp.zeros_like(dg_ref)
            db_ref[...] = jnp.zeros_like(db_ref)

        dpre, dg, db = _ln_bwd_rows(acc, pre_ref[...], g_ref[...], _row_ids(i, tm, K), first_row)
        o_ref[...] = dpre
        dg_ref[0:1, :] += dg
        db_ref[0:1, :] += db

    row = lambda i: (i, 0)
    fix = lambda i: (0, 0)
    in_specs = [pl.BlockSpec((tm, N), row), ANY]
    args = [dy, w]
    if has_res:
        in_specs.append(pl.BlockSpec((tm, K), row))
        args.append(res)
    out_shape, out_specs = S((L, K), f32), pl.BlockSpec((tm, K), row)
    if fb:
        out_shape = (S((L - front_rows, K), f32), S((front_rows, K), f32))
        out_specs = (pl.BlockSpec((tm, K), lambda i: (jnp.maximum(i - fb, 0), 0)),
                     pl.BlockSpec((tm, K), lambda i: (jnp.minimum(i, fb - 1), 0)))
    if ln:
        in_specs += [pl.BlockSpec((tm, K), row), pl.BlockSpec((1, K), fix)]
        args += [ln[0], ln[1]]
        out_shape = (out_shape, S((8, K), f32), S((8, K), f32))
        out_specs = (out_specs, pl.BlockSpec((8, K), fix), pl.BlockSpec((8, K), fix))
    return pl.pallas_call(
        body, name=name, grid=(L // tm,), out_shape=out_shape, in_specs=in_specs, out_specs=out_specs,
        scratch_shapes=[pltpu.VMEM((K, N), w.dtype), pltpu.SemaphoreType.DMA((1,))],
        compiler_params=_cp(dimension_semantics=("arbitrary",)))(*args)


def _out_res_ln(y, w, h, g, b, *, first_row, name):
    L, K = y.shape
    D = w.shape[1]
    tm = TM
    alpha = ALPHA

    def body(y_ref, w_hbm, h_ref, g_ref, b_ref, pre_ref, out_ref, w_vmem, sem):
        _load_once([(w_hbm, w_vmem)], sem)
        pre = alpha * h_ref[...] + _dot(_bf(y_ref[...]), w_vmem[...])
        pre_ref[...] = pre
        out_ref[...] = _ln_fwd(pre, g_ref[...], b_ref[...], _row_ids(pl.program_id(0), tm, D), first_row)

    row = lambda i: (i, 0)
    fix = lambda i: (0, 0)
    return pl.pallas_call(
        body, name=name, grid=(L // tm,), out_shape=(S((L, D), f32), S((L, D), f32)),
        in_specs=[pl.BlockSpec((tm, K), row), ANY, pl.BlockSpec((tm, D), row),
                  pl.BlockSpec((1, D), fix), pl.BlockSpec((1, D), fix)],
        out_specs=(pl.BlockSpec((tm, D), row), pl.BlockSpec((tm, D), row)),
        scratch_shapes=[pltpu.VMEM((K, D), w.dtype), pltpu.SemaphoreType.DMA((1,))],
        compiler_params=_cp(dimension_semantics=("arbitrary",)))(y, w, h, g, b)


def _gdn_in_fwd(x, head, w_full, conv_w, alog, dtb, *, first_row, H):
    D = x.shape[1]
    L = PADF + x.shape[0]
    W = H * DH
    NW = w_full.shape[1]
    KW = conv_w.shape[0]
    tm = TM
    pb = PADF // tm

    def body(x_ref, head_ref, w_hbm, cw_ref, alog_ref, dtb_ref,
             h_ref, pre_ref, z_ref, raw_ref, q_ref, k_ref, v_ref, beta_ref, g_ref,
             w_vmem, carry, sem):
        i = pl.program_id(0)
        _load_once([(w_hbm, w_vmem)], sem)

        @pl.when(i == 0)
        def _():
            carry[...] = jnp.zeros_like(carry)

        hv = jnp.where(i < pb, head_ref[...], x_ref[...])
        h_ref[...] = hv
        hb = _bf(hv)
        outs = (q_ref, k_ref, v_ref)
        for s in range(3):
            pre = _dot(hb, w_vmem[:, s * W:(s + 1) * W])
            pre_ref[:, s * W:(s + 1) * W] = pre
            c = _conv(cw_ref[:, s * W:(s + 1) * W], _taps_back(carry[s], pre, KW))
            carry[s] = pre[tm - 8:tm, :]
            sl = c * _sigmoid(c)
            if s < 2:
                scale = DH ** -0.5 if s == 0 else 1.0
                for hh in range(H):
                    seg = sl[:, hh * DH:(hh + 1) * DH]
                    r = lax.rsqrt(jnp.sum(seg * seg, axis=-1, keepdims=True) + L2_EPS)
                    outs[s][:, hh * DH:(hh + 1) * DH] = seg * (r * scale)
            else:
                v_ref[...] = sl
        z_ref[...] = _dot(hb, w_vmem[:, 3 * W:4 * W])
        raw = _dot(hb, w_vmem[:, 4 * W:4 * W + 2 * LANE])
        raw_ref[...] = raw
        ok = (_row_ids(i, tm, LANE) >= first_row) & (lax.broadcasted_iota(jnp.int32, (tm, LANE), 1) < H)
        beta_ref[...] = jnp.where(ok, _sigmoid(raw[:, :LANE]), 0.0)
        a = raw[:, LANE:] + dtb_ref[...]
        sp = jnp.maximum(a, 0.0) + jnp.log(1.0 + jnp.exp(-jnp.abs(a)))
        gv = jnp.where(ok, -jnp.exp(alog_ref[...]) * sp, 0.0)
        g_ref[...] = _dot(_chunk_tri(tm, lower=True), gv, HI)

    row = lambda i: (i, 0)
    fix = lambda i: (0, 0)
    out_shape = (S((L, D), f32), S((L, 3 * W), f32), S((L, W), f32), S((L, 2 * LANE), f32),
                 S((L, W), f32), S((L, W), f32), S((L, W), f32), S((L, LANE), f32), S((L, LANE), f32))
    out_specs = (pl.BlockSpec((tm, D), row),
                 pl.BlockSpec((tm, 3 * W), row), pl.BlockSpec((tm, W), row), pl.BlockSpec((tm, 2 * LANE), row),
                 pl.BlockSpec((tm, W), row), pl.BlockSpec((tm, W), row), pl.BlockSpec((tm, W), row),
                 pl.BlockSpec((tm, LANE), row), pl.BlockSpec((tm, LANE), row))
    return pl.pallas_call(
        body, name="gdn_in_fwd", grid=(L // tm,), out_shape=out_shape,
        in_specs=[pl.BlockSpec((tm, D), lambda i: (jnp.maximum(i - pb, 0), 0)),
                  pl.BlockSpec((tm, D), lambda i: (jnp.minimum(i, pb - 1), 0)), ANY, pl.BlockSpec((KW, 3 * W), fix),
                  pl.BlockSpec((1, LANE), fix), pl.BlockSpec((1, LANE), fix)],
        out_specs=out_specs,
        scratch_shapes=[pltpu.VMEM((D, NW), w_full.dtype), pltpu.VMEM((3, 8, W), f32), pltpu.SemaphoreType.DMA((1,))],
        compiler_params=_cp(dimension_semantics=("arbitrary",)))(x, head, w_full, conv_w, alog, dtb)


def _gdn_in_bwd(dq, dk, dv, dz, dg, dbeta, pre, raw, conv_w, alog, dtb, *, first_row, H):
    L = dq.shape[0]
    W = H * DH
    KW = conv_w.shape[0]
    tm = TM
    nb = L // tm
    NW = 4 * W + 2 * LANE

    def body(dq_ref, dk_ref, dv_ref, dz_ref, dg_ref, dbeta_ref, pre_ref, hq_ref, hk_ref, hv_ref, raw_ref,
             cw_ref, alog_ref, dtb_ref, dproj_ref, dcw_ref, dal_ref, ddt_ref, carry, tmp):
        i = pl.program_id(0)
        blk = nb - 1 - i

        @pl.when(i == 0)
        def _():
            carry[...] = jnp.zeros_like(carry)
            dcw_ref[...] = jnp.zeros_like(dcw_ref)
            dal_ref[...] = jnp.zeros_like(dal_ref)
            ddt_ref[...] = jnp.zeros_like(ddt_ref)

        halos = (hq_ref, hk_ref, hv_ref)
        douts = (dq_ref, dk_ref, dv_ref)
        for s in range(3):
            sec = slice(s * W, (s + 1) * W)
            pre = pre_ref[:, sec]
            c = _conv(cw_ref[:, sec], _taps_back(jnp.where(blk > 0, halos[s][...], 0.0), pre, KW))
            sig = _sigmoid(c)
            sl = c * sig
            if s < 2:
                scale = DH ** -0.5 if s == 0 else 1.0
                for hh in range(H):
                    hs = slice(hh * DH, (hh + 1) * DH)
                    seg = sl[:, hs]
                    r = lax.rsqrt(jnp.sum(seg * seg, axis=-1, keepdims=True) + L2_EPS)
                    n = seg * r
                    dqs = douts[s][:, hs]
                    tmp[:, hs] = (scale * r) * (dqs - n * jnp.sum(n * dqs, axis=-1, keepdims=True))
                dsl = tmp[...]
            else:
                dsl = dv_ref[...]
            dc = dsl * (sig * (1.0 + c * (1.0 - sig)))
            ahead = _taps_ahead(dc, carry[s], KW)
            carry[s] = dc[0:8, :]
            dproj_ref[:, sec] = _bf(_conv(cw_ref[:, sec], ahead))
            for j in range(KW):
                dcw_ref[j:j + 1, sec] += jnp.sum(ahead[j] * pre, axis=0, keepdims=True)
        dproj_ref[:, 3 * W:4 * W] = _bf(dz_ref[...])
        raw_v = raw_ref[...]
        ok = (_row_ids(blk, tm, LANE) >= first_row) & (lax.broadcasted_iota(jnp.int32, (tm, LANE), 1) < H)
        beta = _sigmoid(raw_v[:, :LANE])
        dbraw = jnp.where(ok, dbeta_ref[...] * beta * (1.0 - beta), 0.0)
        a = raw_v[:, LANE:] + dtb_ref[...]
        sp = jnp.maximum(a, 0.0) + jnp.log(1.0 + jnp.exp(-jnp.abs(a)))
        nea = -jnp.exp(alog_ref[...])
        dgm = jnp.where(ok, _dot(_chunk_tri(tm, lower=False), dg_ref[...], HI), 0.0)
        daraw = dgm * nea * _sigmoid(a)
        dal_ref[0:1, :] += jnp.sum(dgm * nea * sp, axis=0, keepdims=True)
        ddt_ref[0:1, :] += jnp.sum(daraw, axis=0, keepdims=True)
        dproj_ref[:, 4 * W:4 * W + LANE] = _bf(dbraw)
        dproj_ref[:, 4 * W + LANE:4 * W + 2 * LANE] = _bf(daraw)

    rev = lambda i: (nb - 1 - i, 0)
    fix = lambda i: (0, 0)

    def halo(col):
        return pl.BlockSpec((8, W), lambda i: (jnp.maximum((nb - 1 - i) * (tm // 8) - 1, 0), col))

    return pl.pallas_call(
        body, name="gdn_in_bwd", grid=(nb,),
        out_shape=(S((L, NW), bf16), S((8, 3 * W), f32), S((8, LANE), f32), S((8, LANE), f32)),
        in_specs=[pl.BlockSpec((tm, W), rev)] * 4 + [pl.BlockSpec((tm, LANE), rev)] * 2
        + [pl.BlockSpec((tm, 3 * W), rev), halo(0), halo(1), halo(2), pl.BlockSpec((tm, 2 * LANE), rev),
           pl.BlockSpec((KW, 3 * W), fix), pl.BlockSpec((1, LANE), fix), pl.BlockSpec((1, LANE), fix)],
        out_specs=(pl.BlockSpec((tm, NW), rev), pl.BlockSpec((8, 3 * W), fix),
                   pl.BlockSpec((8, LANE), fix), pl.BlockSpec((8, LANE), fix)),
        scratch_shapes=[pltpu.VMEM((3, 8, W), f32), pltpu.VMEM((tm, W), f32)],
        compiler_params=_cp(dimension_semantics=("arbitrary",)))(
            dq, dk, dv, dz, dg, dbeta, pre, pre, pre, pre, raw, conv_w, alog, dtb)


def _chunk_tri(n, lower):
    i = lax.broadcasted_iota(jnp.int32, (n, n), 0)
    j = lax.broadcasted_iota(jnp.int32, (n, n), 1)
    sh = int(math.log2(CH))
    same = lax.shift_right_logical(i, sh) == lax.shift_right_logical(j, sh)
    return (same & ((i >= j) if lower else (j >= i))).astype(f32)


def _tri_inv_many(ms, eye):
    ts = [eye - m for m in ms]
    ps = list(ms)
    for _ in range(int(math.log2(CH)) - 1):
        ps = [_dot(p, p, HI3) for p in ps]
        ts = [t + _dot(t, p, HI3) for t, p in zip(ts, ps)]
    return ts


def _chunk_local(q, k, v, gcol, grow, glast, bcol, ii, jj):
    dec = jnp.where(ii >= jj, jnp.exp(jnp.minimum(gcol - grow, 0.0)), 0.0)
    eg = jnp.exp(gcol)
    kb = k * bcol
    kbg = kb * eg
    vb = v * bcol
    qt = q * eg
    kt = k * jnp.exp(glast - gcol)
    kk = _dot_nt(_bf(kb), _bf(k))
    qk = _dot_nt(_bf(q), _bf(k))
    return dec, eg, kb, kbg, vb, qt, kt, kk, qk


def _delta_fwd(q, k, v, g, beta, z, nw, *, H):
    L = q.shape[0]
    W = H * DH
    rb = TM
    nc = rb // CH
    nblk = L // rb

    def body(q_ref, k_ref, v_ref, g_ref, b_ref, z_ref, nw_ref, o_ref, y_ref, s_out, t_out, s_scr):
        @pl.when(pl.program_id(0) == 0)
        def _():
            s_scr[...] = jnp.zeros_like(s_scr)

        ii = lax.broadcasted_iota(jnp.int32, (CH, CH), 0)
        jj = lax.broadcasted_iota(jnp.int32, (CH, CH), 1)
        eye = (ii == jj).astype(f32)
        nwv = nw_ref[...]

        heads = range(H)
        hsl = [slice(hh * DH, (hh + 1) * DH) for hh in heads]

        def chunk(c, carry):
            r0 = pl.multiple_of(c * CH, CH)
            rows = pl.ds(r0, CH)
            gam = g_ref[rows, :]
            gam_t = gam.T
            bb = b_ref[rows, :]
            glast = [gam[CH - 1:CH, hh:hh + 1] for hh in heads]
            loc = [_chunk_local(q_ref[rows, hsl[hh]], k_ref[rows, hsl[hh]], v_ref[rows, hsl[hh]],
                                gam[:, hh:hh + 1], gam_t[hh:hh + 1, :], glast[hh], bb[:, hh:hh + 1], ii, jj)
                   for hh in heads]
            st = [s_scr[hh] for hh in heads]
            zs = [z_ref[rows, hsl[hh]] for hh in heads]
            ts = _tri_inv_many([jnp.where(ii > jj, l[7] * l[0], 0.0) for l in loc], eye)
            us = [_dot(t, l[4], HI3) for t, l in zip(ts, loc)]
            ws = [_dot(t, l[3], HI3) for t, l in zip(ts, loc)]
            stb = [_bf(s) for s in st]
            vn = [u - _dot(_bf(w), sb) for u, w, sb in zip(us, ws, stb)]
            vnb = [_bf(x) for x in vn]
            snew = [s * jnp.exp(gl) + _dot_tn(_bf(l[6]), xb) for s, gl, l, xb in zip(st, glast, loc, vnb)]
            os_ = [_dot(_bf(l[5]), sb) + _dot(_bf(l[8] * l[0]), xb) for l, sb, xb in zip(loc, stb, vnb)]
            for hh in heads:
                o = os_[hh]
                s_out[c, hh] = st[hh]
                t_out[c, hh] = ts[hh]
                s_scr[hh] = snew[hh]
                o_ref[rows, hsl[hh]] = o
                on = o * lax.rsqrt(jnp.mean(o * o, axis=-1, keepdims=True) + RMS_EPS) * nwv
                y_ref[rows, hsl[hh]] = _bf(on * (zs[hh] * _sigmoid(zs[hh])))
            return carry

        lax.fori_loop(0, nc, chunk, 0)

    row = lambda i: (i, 0)
    fix = lambda i: (0, 0)
    return pl.pallas_call(
        body, name="delta_fwd", grid=(nblk,),
        out_shape=(S((L, W), f32), S((L, W), bf16), S((L // CH, H, DH, DH), f32), S((L // CH, H, CH, CH), f32)),
        in_specs=[pl.BlockSpec((rb, W), row)] * 3 + [pl.BlockSpec((rb, LANE), row)] * 2
        + [pl.BlockSpec((rb, W), row), pl.BlockSpec((1, DH), fix)],
        out_specs=(pl.BlockSpec((rb, W), row), pl.BlockSpec((rb, W), row),
                   pl.BlockSpec((nc, H, DH, DH), lambda i: (i, 0, 0, 0)),
                   pl.BlockSpec((nc, H, CH, CH), lambda i: (i, 0, 0, 0))),
        scratch_shapes=[pltpu.VMEM((H, DH, DH), f32)],
        compiler_params=_cp(dimension_semantics=("arbitrary",)))(q, k, v, g, beta, z, nw)


def _delta_bwd(dy, o, z, nw, q, k, v, g, beta, s_all, t_all, *, H):
    L = q.shape[0]
    W = H * DH
    rb = TM
    nc = rb // CH
    nblk = L // rb

    def body(dy_ref, o_ref, z_ref, nw_ref, q_ref, k_ref, v_ref, g_ref, b_ref, s_ref, t_ref,
             dq_ref, dk_ref, dv_ref, dz_ref, dg_ref, db_ref, dnw_ref, ds_scr):
        @pl.when(pl.program_id(0) == 0)
        def _():
            ds_scr[...] = jnp.zeros_like(ds_scr)
            dnw_ref[...] = jnp.zeros_like(dnw_ref)

        ii = lax.broadcasted_iota(jnp.int32, (CH, CH), 0)
        jj = lax.broadcasted_iota(jnp.int32, (CH, CH), 1)
        lane = lax.broadcasted_iota(jnp.int32, (CH, LANE), 1)
        last_row = lax.broadcasted_iota(jnp.int32, (CH, 1), 0) == CH - 1
        nwv = nw_ref[...]

        def chunk(cc, carry):
            c = nc - 1 - cc
            r0 = pl.multiple_of(c * CH, CH)
            rows = pl.ds(r0, CH)
            gam = g_ref[rows, :]
            gam_t = gam.T
            bb = b_ref[rows, :]

            def head(hh):
                hs = slice(hh * DH, (hh + 1) * DH)
                gcol, grow, glast = gam[:, hh:hh + 1], gam_t[hh:hh + 1, :], gam[CH - 1:CH, hh:hh + 1]
                bcol = bb[:, hh:hh + 1]
                qh, kh, vh = q_ref[rows, hs], k_ref[rows, hs], v_ref[rows, hs]
                oh, zh, dyh = o_ref[rows, hs], z_ref[rows, hs], dy_ref[rows, hs]
                t = t_ref[c, hh]
                st = s_ref[c, hh]
                dsn = ds_scr[hh]
                rms = lax.rsqrt(jnp.mean(oh * oh, axis=-1, keepdims=True) + RMS_EPS)
                on = oh * rms
                sig = _sigmoid(zh)
                sz = zh * sig
                dz_ref[rows, hs] = dyh * on * nwv * (sig * (1.0 + zh * (1.0 - sig)))
                dnw = jnp.sum(dyh * on * sz, axis=0, keepdims=True)
                don = dyh * nwv * sz
                do = rms * (don - on * jnp.mean(don * on, axis=-1, keepdims=True))
                dec, eg, kb, kbg, vb, qt, kt, kk, qk = _chunk_local(qh, kh, vh, gcol, grow, glast, bcol, ii, jj)
                stb, dsnb, dob, tb, kbgb = _bf(st), _bf(dsn), _bf(do), _bf(t), _bf(kbg)
                r = vb - _dot(kbgb, stb)
                dqt = _dot_nt(dob, stb)
                ds_new = _dot_tn(_bf(qt), dob)
                mm = jnp.where(ii > jj, kk * dec, 0.0)
                attn = qk * dec
                yield
                vn = _dot(t, r, HI3)
                dvn = _dot_tn(_bf(attn), dob) + _dot(_bf(kt), dsnb)
                egl = jnp.exp(glast)
                ekt = jnp.exp(glast - gcol)
                yield
                vnb, dvnb = _bf(vn), _bf(dvn)
                dattn = jnp.where(ii >= jj, _dot_nt(dob, vnb), 0.0)
                dkt = _dot_nt(vnb, dsnb)
                dvb = _dot_tn(tb, dvnb)
                dt = _dot_nt(dvnb, _bf(r))
                dglast = egl * jnp.sum(jnp.sum(dsn * st, axis=0, keepdims=True), axis=1, keepdims=True)
                yield
                dvbb = _bf(dvb)
                dv_ref[rows, hs] = dvb * bcol
                ds_scr[hh] = ds_new + egl * dsn - _dot_tn(kbgb, dvbb)
                dkbg = -_dot_nt(dvbb, stb)
                x = _dot_nt(_bf(dt), tb)
                yield
                dm = jnp.where(ii > jj, -_dot_tn(tb, _bf(x)), 0.0)
                dkk = dm * dec
                dqk = dattn * dec
                e = dm * mm + dattn * attn
                dgam = jnp.sum(e, axis=1, keepdims=True) - jnp.sum(e.T, axis=1, keepdims=True)
                dkkb, dqkb, kbf = _bf(dkk), _bf(dqk), _bf(kh)
                dkb = _dot(dkkb, kbf) + dkbg * eg
                dk_ref[rows, hs] = _dot_tn(dkkb, _bf(kb)) + _dot_tn(dqkb, _bf(qh)) + dkt * ekt + dkb * bcol
                dq_ref[rows, hs] = _dot(dqkb, kbf) + dqt * eg
                yield
                dktkt = dkt * kt
                dgam = dgam + jnp.sum(dqt * qt - dktkt + dkbg * kbg, axis=1, keepdims=True)
                dglast = dglast + jnp.sum(jnp.sum(dktkt, axis=0, keepdims=True), axis=1, keepdims=True)
                dgam = dgam + jnp.where(last_row, dglast, 0.0)
                dbeta = jnp.sum(dkb * kh + dvb * vh, axis=1, keepdims=True)
                return dgam, dbeta, dnw

            res = [None] * H
            for h0 in range(0, H, BWD_HEAD_GROUP):
                group = range(h0, min(h0 + BWD_HEAD_GROUP, H))
                gens = {hh: head(hh) for hh in group}
                while any(res[hh] is None for hh in group):
                    for hh in group:
                        try:
                            next(gens[hh])
                        except StopIteration as stop:
                            res[hh] = stop.value
            dgam_all = jnp.zeros((CH, LANE), f32)
            dbeta_all = jnp.zeros((CH, LANE), f32)
            dnw_acc = jnp.zeros((1, DH), f32)
            for hh in range(H):
                dgam, dbeta, dnw = res[hh]
                dgam_all = dgam_all + jnp.where(lane == hh, dgam, 0.0)
                dbeta_all = dbeta_all + jnp.where(lane == hh, dbeta, 0.0)
                dnw_acc = dnw_acc + dnw
            dg_ref[rows, :] = dgam_all
            db_ref[rows, :] = dbeta_all
            dnw_ref[0:1, :] += dnw_acc
            return carry

        lax.fori_loop(0, nc, chunk, 0)

    rev = lambda i: (nblk - 1 - i, 0)
    rev4 = lambda i: (nblk - 1 - i, 0, 0, 0)
    fix = lambda i: (0, 0)
    wide = pl.BlockSpec((rb, W), rev)
    thin = pl.BlockSpec((rb, LANE), rev)
    return pl.pallas_call(
        body, name="delta_bwd", grid=(nblk,),
        out_shape=(S((L, W), f32),) * 4 + (S((L, LANE), f32),) * 2 + (S((8, DH), f32),),
        in_specs=[wide, wide, wide, pl.BlockSpec((1, DH), fix), wide, wide, wide, thin, thin,
                  pl.BlockSpec((nc, H, DH, DH), rev4), pl.BlockSpec((nc, H, CH, CH), rev4)],
        out_specs=(wide,) * 4 + (thin, thin, pl.BlockSpec((8, DH), fix)),
        scratch_shapes=[pltpu.VMEM((H, DH, DH), f32)],
        compiler_params=_cp(dimension_semantics=("arbitrary",)))(dy, o, z, nw, q, k, v, g, beta, s_all, t_all)


def _sc_fwd(h, w_in, conv_w, w_out, g, b, *, first_row):
    L, D = h.shape
    W = w_out.shape[0]
    KW = conv_w.shape[0]
    tm = TM
    alpha = ALPHA

    def body(h_ref, win_hbm, cw_ref, wout_hbm, g_ref, b_ref, proj_ref, bu_ref, pre_ref, out_ref,
             win, wout, carry, sem):
        i = pl.program_id(0)
        _load_once([(win_hbm, win), (wout_hbm, wout)], sem)

        @pl.when(i == 0)
        def _():
            carry[...] = jnp.zeros_like(carry)

        hv = h_ref[...]
        hb = _bf(hv)
        bg = _dot(hb, win[:, 0:W])
        cg = _dot(hb, win[:, W:2 * W])
        xv = _dot(hb, win[:, 2 * W:3 * W])
        proj_ref[:, 0:W] = bg
        proj_ref[:, W:2 * W] = cg
        proj_ref[:, 2 * W:3 * W] = xv
        p = cg * xv
        u = _conv(cw_ref[...], _taps_back(carry[...], p, KW))
        carry[...] = p[tm - 8:tm, :]
        bu = _bf(bg * u)
        bu_ref[...] = bu
        pre = alpha * hv + _dot(bu, wout[...])
        pre_ref[...] = pre
        out_ref[...] = _ln_fwd(pre, g_ref[...], b_ref[...], _row_ids(i, tm, D), first_row)

    row = lambda i: (i, 0)
    fix = lambda i: (0, 0)
    return pl.pallas_call(
        body, name="sc_fwd", grid=(L // tm,),
        out_shape=(S((L, 3 * W), f32), S((L, W), bf16), S((L, D), f32), S((L, D), f32)),
        in_specs=[pl.BlockSpec((tm, D), row), ANY, pl.BlockSpec((KW, W), fix), ANY,
                  pl.BlockSpec((1, D), fix), pl.BlockSpec((1, D), fix)],
        out_specs=(pl.BlockSpec((tm, 3 * W), row), pl.BlockSpec((tm, W), row),
                   pl.BlockSpec((tm, D), row), pl.BlockSpec((tm, D), row)),
        scratch_shapes=[pltpu.VMEM((D, 3 * W), w_in.dtype), pltpu.VMEM((W, D), w_out.dtype),
                        pltpu.VMEM((8, W), f32), pltpu.SemaphoreType.DMA((2,))],
        compiler_params=_cp(dimension_semantics=("arbitrary",)))(h, w_in, conv_w, w_out, g, b)


def _sc_bwd(dbu, proj, conv_w):
    L, W = dbu.shape
    KW = conv_w.shape[0]
    tm = TM
    nb = L // tm

    def body(dbu_ref, proj_ref, hc_ref, hx_ref, cw_ref, dproj_ref, dcw_ref, carry):
        i = pl.program_id(0)
        blk = nb - 1 - i

        @pl.when(i == 0)
        def _():
            carry[...] = jnp.zeros_like(carry)
            dcw_ref[...] = jnp.zeros_like(dcw_ref)

        bg, cg, xv = proj_ref[:, 0:W], proj_ref[:, W:2 * W], proj_ref[:, 2 * W:3 * W]
        p = cg * xv
        u = _conv(cw_ref[...], _taps_back(jnp.where(blk > 0, hc_ref[...] * hx_ref[...], 0.0), p, KW))
        d = dbu_ref[...]
        dproj_ref[:, 0:W] = _bf(d * u)
        du = d * bg
        ahead = _taps_ahead(du, carry[...], KW)
        carry[...] = du[0:8, :]
        dp = _conv(cw_ref[...], ahead)
        for j in range(KW):
            dcw_ref[j:j + 1, :] += jnp.sum(ahead[j] * p, axis=0, keepdims=True)
        dproj_ref[:, W:2 * W] = _bf(dp * xv)
        dproj_ref[:, 2 * W:3 * W] = _bf(dp * cg)

    rev = lambda i: (nb - 1 - i, 0)
    fix = lambda i: (0, 0)

    def halo(col):
        return pl.BlockSpec((8, W), lambda i: (jnp.maximum((nb - 1 - i) * (tm // 8) - 1, 0), col))

    return pl.pallas_call(
        body, name="sc_bwd", grid=(nb,), out_shape=(S((L, 3 * W), bf16), S((8, W), f32)),
        in_specs=[pl.BlockSpec((tm, W), rev), pl.BlockSpec((tm, 3 * W), rev), halo(1), halo(2),
                  pl.BlockSpec((KW, W), fix)],
        out_specs=(pl.BlockSpec((tm, 3 * W), rev), pl.BlockSpec((8, W), fix)),
        scratch_shapes=[pltpu.VMEM((8, W), f32)],
        compiler_params=_cp(dimension_semantics=("arbitrary",)))(dbu, proj, proj, proj, conv_w)


def _ffn_cols(F):
    fc = F
    for cand in (1408, 1024, 512, 256, 128):
        if F % cand == 0:
            fc = cand
            break
    return fc


def _ffn_fwd(h, w_up, conv_w, w_down, g, b, *, first_row, name):
    L, D = h.shape
    F = w_down.shape[0]
    KW = conv_w.shape[0]
    tm = TM
    fc = _ffn_cols(F)
    alpha = ALPHA

    def body(h_ref, wup_hbm, cw_ref, wdn_hbm, g_ref, b_ref, up_ref, a_ref, pre_ref, out_ref,
             wup, wdn, carry, sem):
        i = pl.program_id(0)
        _load_once([(wup_hbm, wup), (wdn_hbm, wdn)], sem)

        @pl.when(i == 0)
        def _():
            carry[...] = jnp.zeros_like(carry)

        hv = h_ref[...]
        hb = _bf(hv)
        pre = alpha * hv
        for c0 in range(0, F, fc):
            cs = slice(c0, c0 + fc)
            u = _dot(hb, wup[:, cs])
            gate = _dot(hb, wup[:, F + c0:F + c0 + fc])
            up_ref[:, cs] = u
            up_ref[:, F + c0:F + c0 + fc] = gate
            uc = _conv(cw_ref[:, cs], _taps_back(carry[:, cs], u, KW))
            carry[:, cs] = u[tm - 8:tm, :]
            ab = _bf(uc * _sigmoid(uc) * gate)
            a_ref[:, cs] = ab
            pre = pre + _dot(ab, wdn[cs, :])
        pre_ref[...] = pre
        out_ref[...] = _ln_fwd(pre, g_ref[...], b_ref[...], _row_ids(i, tm, D), first_row)

    row = lambda i: (i, 0)
    fix = lambda i: (0, 0)
    return pl.pallas_call(
        body, name=name, grid=(L // tm,),
        out_shape=(S((L, 2 * F), f32), S((L, F), bf16), S((L, D), f32), S((L, D), f32)),
        in_specs=[pl.BlockSpec((tm, D), row), ANY, pl.BlockSpec((KW, F), fix), ANY,
                  pl.BlockSpec((1, D), fix), pl.BlockSpec((1, D), fix)],
        out_specs=(pl.BlockSpec((tm, 2 * F), row), pl.BlockSpec((tm, F), row),
                   pl.BlockSpec((tm, D), row), pl.BlockSpec((tm, D), row)),
        scratch_shapes=[pltpu.VMEM((D, 2 * F), w_up.dtype), pltpu.VMEM((F, D), w_down.dtype),
                        pltpu.VMEM((8, F), f32), pltpu.SemaphoreType.DMA((2,))],
        compiler_params=_cp(dimension_semantics=("arbitrary",)))(h, w_up, conv_w, w_down, g, b)


def _ffn_bwd(dpre, up, w_down, conv_w, *, name):
    L, D = dpre.shape
    F = w_down.shape[0]
    KW = conv_w.shape[0]
    tm = TM
    nb = L // tm
    fc = _ffn_cols(F)

    def body(dpre_ref, up_ref, halo_ref, wdn_hbm, cw_ref, dup_ref, dcw_ref, wdn, carry, sem):
        i = pl.program_id(0)
        blk = nb - 1 - i
        _load_once([(wdn_hbm, wdn)], sem)

        @pl.when(i == 0)
        def _():
            carry[...] = jnp.zeros_like(carry)
            dcw_ref[...] = jnp.zeros_like(dcw_ref)

        db = _bf(dpre_ref[...])
        for c0 in range(0, F, fc):
            cs = slice(c0, c0 + fc)
            da = _dot_nt(db, wdn[cs, :])
            gate = up_ref[:, F + c0:F + c0 + fc]
            u = up_ref[:, cs]
            uc = _conv(cw_ref[:, cs], _taps_back(jnp.where(blk > 0, halo_ref[:, cs], 0.0), u, KW))
            sig = _sigmoid(uc)
            dup_ref[:, F + c0:F + c0 + fc] = _bf(da * (uc * sig))
            duc = da * gate * (sig * (1.0 + uc * (1.0 - sig)))
            ahead = _taps_ahead(duc, carry[:, cs], KW)
            carry[:, cs] = duc[0:8, :]
            dup_ref[:, cs] = _bf(_conv(cw_ref[:, cs], ahead))
            for j in range(KW):
                dcw_ref[j:j + 1, cs] += jnp.sum(ahead[j] * u, axis=0, keepdims=True)

    rev = lambda i: (nb - 1 - i, 0)
    fix = lambda i: (0, 0)
    return pl.pallas_call(
        body, name=name, grid=(nb,), out_shape=(S((L, 2 * F), bf16), S((8, F), f32)),
        in_specs=[pl.BlockSpec((tm, D), rev), pl.BlockSpec((tm, 2 * F), rev),
                  pl.BlockSpec((8, F), lambda i: (jnp.maximum((nb - 1 - i) * (tm // 8) - 1, 0), 0)),
                  ANY, pl.BlockSpec((KW, F), fix)],
        out_specs=(pl.BlockSpec((tm, 2 * F), rev), pl.BlockSpec((8, F), fix)),
        scratch_shapes=[pltpu.VMEM((F, D), w_down.dtype), pltpu.VMEM((8, F), f32), pltpu.SemaphoreType.DMA((1,))],
        compiler_params=_cp(dimension_semantics=("arbitrary",)))(dpre, up, up, w_down, conv_w)


def _loss_head(h, target, pre, g, *, first_row):
    L, D = h.shape
    tm = TM
    pb = PADF // tm

    def body(h_ref, t_ref, pre_ref, g_ref, dpre_ref, dg_ref, db_ref, loss_ref):
        i = pl.program_id(0)

        @pl.when(i == 0)
        def _():
            loss_ref[...] = jnp.zeros_like(loss_ref)
            dg_ref[...] = jnp.zeros_like(dg_ref)
            db_ref[...] = jnp.zeros_like(db_ref)

        valid = i >= pb
        err = h_ref[...] - t_ref[...]
        dh = jnp.where(valid, err * (1.0 / D), 0.0)
        part = 0.5 * jnp.sum(jnp.sum(err * err, axis=-1, keepdims=True) * (1.0 / D), axis=0, keepdims=True)
        loss_ref[...] += jnp.where(valid, part, 0.0)
        dpre, dg, db = _ln_bwd_rows(dh, pre_ref[...], g_ref[...], _row_ids(i, tm, D), first_row)
        dpre_ref[...] = dpre
        dg_ref[0:1, :] += dg
        db_ref[0:1, :] += db

    row = lambda i: (i, 0)
    fix = lambda i: (0, 0)
    return pl.pallas_call(
        body, name="loss_head", grid=(L // tm,),
        out_shape=(S((L, D), f32), S((8, D), f32), S((8, D), f32), S((8, LANE), f32)),
        in_specs=[pl.BlockSpec((tm, D), row), pl.BlockSpec((tm, D), lambda i: (jnp.maximum(i - pb, 0), 0)),
                  pl.BlockSpec((tm, D), row), pl.BlockSpec((1, D), fix)],
        out_specs=(pl.BlockSpec((tm, D), row), pl.BlockSpec((8, D), fix), pl.BlockSpec((8, D), fix),
                   pl.BlockSpec((8, LANE), fix)),
        compiler_params=_cp(dimension_semantics=("arbitrary",)))(h, target, pre, g)


def _adamw(g_terms, w, m, v, *, name):
    R, C = w.shape
    tr = _row_tile(R)
    n = len(g_terms)
    c1 = 1.0 - ADAM_B1 ** ADAM_STEP
    c2 = 1.0 - ADAM_B2 ** ADAM_STEP

    def body(*refs):
        g = refs[0][...].astype(f32)
        for r in refs[1:n]:
            g = g + r[...].astype(f32)
        w_ref, m_ref, v_ref, g_out, d_out, m_out, v_out = refs[n:]
        mn = ADAM_B1 * m_ref[...] + (1.0 - ADAM_B1) * g
        vn = ADAM_B2 * v_ref[...] + (1.0 - ADAM_B2) * (g * g)
        g_out[...] = g
        m_out[...] = mn
        v_out[...] = vn
        d_out[...] = -ADAM_LR * ((mn / c1) / (jnp.sqrt(vn / c2) + ADAM_EPS) + ADAM_WD * w_ref[...])

    spec = pl.BlockSpec((tr, C), lambda i: (i, 0))
    return pl.pallas_call(
        body, name=name, grid=(R // tr,), out_shape=(S((R, C), f32),) * 4,
        in_specs=[spec] * (n + 3), out_specs=(spec,) * 4,
        compiler_params=_cp(dimension_semantics=("arbitrary",)))(*g_terms, w, m, v)


def _sum_devices(x):
    n, R, C = x.shape

    def body(x_ref, o_ref):
        acc = x_ref[0]
        for d in range(1, n):
            acc = acc + x_ref[d]
        o_ref[...] = acc

    return pl.pallas_call(body, name="sum_devices", out_shape=S((R, C), f32), compiler_params=_cp())(x)


def _row_tile(R):
    for step in (16, 8):
        for t in range(256, 0, -step):
            if R % t == 0:
                return t
    return R


def _pair_add(s32, recv1, sidx, ridx, *, name):
    _, L, K, n = s32.shape
    nj = sidx.shape[0]
    tk = _row_tile(K)

    def body(sidx_ref, ridx_ref, a_ref, b_ref, o_ref):
        o_ref[...] = _bf(a_ref[...] + b_ref[...].astype(f32))

    blk = (1, 1, tk, n)
    grid_spec = pltpu.PrefetchScalarGridSpec(
        num_scalar_prefetch=2, grid=(nj, L, K // tk),
        in_specs=[pl.BlockSpec(blk, lambda j, l, i, si, ri: (si[j], l, i, 0)),
                  pl.BlockSpec(blk, lambda j, l, i, si, ri: (ri[j], l, i, 0))],
        out_specs=pl.BlockSpec(blk, lambda j, l, i, si, ri: (j, l, i, 0)))
    return pl.pallas_call(
        body, name=name, grid_spec=grid_spec, out_shape=S((nj, L, K, n), bf16),
        compiler_params=_cp(dimension_semantics=("arbitrary",) * 3))(sidx, ridx, s32, recv1)


def _adamw_shard(s32, recv1, recv2, w, m, v, idx, *, name):
    L, K, n = w.shape
    tk = _row_tile(K)
    c1 = 1.0 - ADAM_B1 ** ADAM_STEP
    c2 = 1.0 - ADAM_B2 ** ADAM_STEP

    def body(idx_ref, a_ref, b_ref, r_ref, w_ref, m_ref, v_ref, g_out, d_out, m_out, v_out):
        g = a_ref[0] + b_ref[0].astype(f32)
        for j in range(3):
            g = g + r_ref[j].astype(f32)
        mn = ADAM_B1 * m_ref[...] + (1.0 - ADAM_B1) * g
        vn = ADAM_B2 * v_ref[...] + (1.0 - ADAM_B2) * (g * g)
        g_out[...] = g
        m_out[...] = mn
        v_out[...] = vn
        d_out[...] = -ADAM_LR * ((mn / c1) / (jnp.sqrt(vn / c2) + ADAM_EPS) + ADAM_WD * w_ref[...])

    own = pl.BlockSpec((1, tk, n), lambda l, i, ix: (l, i, 0))
    grid_spec = pltpu.PrefetchScalarGridSpec(
        num_scalar_prefetch=1, grid=(L, K // tk),
        in_specs=[pl.BlockSpec((1, 1, tk, n), lambda l, i, ix: (ix[0], l, i, 0)),
                  pl.BlockSpec((1, 1, tk, n), lambda l, i, ix: (ix[1], l, i, 0)),
                  pl.BlockSpec((3, 1, tk, n), lambda l, i, ix: (0, l, i, 0)), own, own, own],
        out_specs=(own,) * 4)
    return pl.pallas_call(
        body, name=name, grid_spec=grid_spec, out_shape=(S((L, K, n), f32),) * 4,
        compiler_params=_cp(dimension_semantics=("arbitrary", "arbitrary")))(idx, s32, recv1, recv2, w, m, v)


def _adamw_direct(s32s, recvs, w, m, v, me, *, name):
    L, K, n = w.shape
    tk = _row_tile(K)
    c1 = 1.0 - ADAM_B1 ** ADAM_STEP
    c2 = 1.0 - ADAM_B2 ** ADAM_STEP

    def body(me_ref, *refs):
        own_refs, recv_refs = refs[:L], refs[L:2 * L]
        w_ref, m_ref, v_ref, g_out, d_out, m_out, v_out = refs[2 * L:]
        for li in range(L):
            @pl.when(pl.program_id(0) == li)
            def _(li=li):
                g = own_refs[li][0, 0]
                for d in range(N_DEV):
                    g = g + recv_refs[li][d, 0].astype(f32)
                mn = ADAM_B1 * m_ref[0] + (1.0 - ADAM_B1) * g
                vn = ADAM_B2 * v_ref[0] + (1.0 - ADAM_B2) * (g * g)
                g_out[0] = g
                m_out[0] = mn
                v_out[0] = vn
                d_out[0] = -ADAM_LR * ((mn / c1) / (jnp.sqrt(vn / c2) + ADAM_EPS) + ADAM_WD * w_ref[0])

    own = pl.BlockSpec((1, tk, n), lambda l, i, ix: (l, i, 0))
    grid_spec = pltpu.PrefetchScalarGridSpec(
        num_scalar_prefetch=1, grid=(L, K // tk),
        in_specs=[pl.BlockSpec((1, 1, tk, n), lambda l, i, ix: (ix[0], 0, i, 0))] * L
        + [pl.BlockSpec((N_DEV, 1, tk, n), lambda l, i, ix: (0, 0, i, 0))] * L + [own, own, own],
        out_specs=(own,) * 4)
    return pl.pallas_call(
        body, name=name, grid_spec=grid_spec, out_shape=(S((L, K, n), f32),) * 4,
        compiler_params=_cp(dimension_semantics=("arbitrary", "arbitrary")))(me, *s32s, *recvs, w, m, v)


def _col_segments(n, mapping):
    segs = []
    for p in range(N_DEV):
        lo, hi = p * n, (p + 1) * n
        out = []
        for c0, c1, e0 in mapping:
            a, b = max(lo, c0), min(hi, c1)
            if a < b:
                out.append((a - lo, e0 + (a - c0), b - a))
        segs.append(out)
    return segs


def _assemble_cols(gathered, mapping, n_out, *, name):
    _, L, K, n = gathered.shape
    tk = _row_tile(K)
    segs = _col_segments(n, mapping)
    covered = sum(w for s in segs for (_, _, w) in s)

    def body(g_ref, o_ref):
        if covered != n_out:
            o_ref[...] = jnp.zeros_like(o_ref)
        for p in range(N_DEV):
            for s0, d0, w in segs[p]:
                o_ref[0, :, d0:d0 + w] = g_ref[p, 0, :, s0:s0 + w]

    return pl.pallas_call(
        body, name=name, grid=(L, K // tk), out_shape=S((L, K, n_out), gathered.dtype),
        in_specs=[pl.BlockSpec((N_DEV, 1, tk, n), lambda l, i: (0, l, i, 0))],
        out_specs=pl.BlockSpec((1, tk, n_out), lambda l, i: (l, i, 0)),
        compiler_params=_cp(dimension_semantics=("arbitrary", "arbitrary")))(gathered)


def _split_cols(dws, mapping, n, *, name):
    L = len(dws)
    K, n_in = dws[0].shape
    tk = _row_tile(K)
    segs = _col_segments(n, mapping)

    def body(*refs):
        ins, o32, o16 = refs[:L], refs[L], refs[L + 1]
        for li in range(L):
            @pl.when(pl.program_id(0) == li)
            def _(li=li):
                for p in range(N_DEV):
                    for s0, d0, w in segs[p]:
                        val = ins[li][:, d0:d0 + w]
                        o32[p, 0, :, s0:s0 + w] = val
                        o16[p, 0, :, s0:s0 + w] = _bf(val)

    out = pl.BlockSpec((N_DEV, 1, tk, n), lambda l, i: (0, l, i, 0))
    return pl.pallas_call(
        body, name=name, grid=(L, K // tk), out_shape=(S((N_DEV, L, K, n), f32), S((N_DEV, L, K, n), bf16)),
        in_specs=[pl.BlockSpec((tk, n_in), lambda l, i: (i, 0))] * L, out_specs=(out, out),
        compiler_params=_cp(dimension_semantics=("arbitrary", "arbitrary")))(*dws)


def _split_rows(dws, k, *, name):
    L = len(dws)
    N = dws[0].shape[1]

    def body(*refs):
        ins, o32, o16 = refs[:L], refs[L], refs[L + 1]
        for li in range(L):
            @pl.when(pl.program_id(0) == li)
            def _(li=li):
                val = ins[li][...]
                o32[0, 0] = val
                o16[0, 0] = _bf(val)

    out = pl.BlockSpec((1, 1, k, N), lambda l, p: (p, l, 0, 0))
    return pl.pallas_call(
        body, name=name, grid=(L, N_DEV), out_shape=(S((N_DEV, L, k, N), f32), S((N_DEV, L, k, N), bf16)),
        in_specs=[pl.BlockSpec((k, N), lambda l, p: (p, 0))] * L, out_specs=(out, out),
        compiler_params=_cp(dimension_semantics=("arbitrary", "arbitrary")))(*dws)


def _rows_full(gathered):
    _, L, k, N = gathered.shape
    return jnp.transpose(gathered, (1, 0, 2, 3)).reshape(L, N_DEV * k, N)


def _peer(rel):
    x, y, c = lax.axis_index("x"), lax.axis_index("y"), lax.axis_index("c")
    return {"c": (x, y, 1 - c), "x": (1 - x, y, c), "y": (x, 1 - y, c), "xy": (1 - x, 1 - y, c)}[rel]


def _all_gather(xs, *, name):
    na = len(xs)

    def body(*refs):
        x_refs, out_refs = refs[:na], refs[na:2 * na]
        send_sems, recv_sems, local_sems = refs[2 * na:]
        mx, my, mc = lax.axis_index("x"), lax.axis_index("y"), lax.axis_index("c")
        me, sibling = (mx, my, mc), (mx, my, 1 - mc)
        chips = [(1 - mx, my), (mx, 1 - my), (1 - mx, 1 - my)]

        def slot(a, px, py, pc):
            return out_refs[a].at[4 * px + 2 * py + pc]

        def copy(a, kk, block, to, src=None):
            return pltpu.make_async_remote_copy(
                src_ref=slot(a, *block) if src is None else src, dst_ref=slot(a, *block),
                send_sem=send_sems.at[7 * a + kk], recv_sem=recv_sems.at[7 * a + kk], device_id=to, device_id_type=MESH)

        mine = [pltpu.make_async_copy(x_refs[a], slot(a, *me), local_sems.at[a]) for a in range(na)]
        for cp in mine:
            cp.start()
        first = []
        for a in range(na):
            first.append(copy(a, 0, me, sibling, src=x_refs[a]))
            first += [copy(a, 1 + j, me, (*chip, mc), src=x_refs[a]) for j, chip in enumerate(chips)]
        for cp in first:
            cp.start()
        passed = []
        for j, chip in enumerate(chips):
            for a in range(na):
                copy(a, 1 + j, (*chip, mc), me).wait_recv()
                fwd = copy(a, 4 + j, (*chip, mc), sibling)
                fwd.start()
                passed.append(fwd)
        for a in range(na):
            copy(a, 0, sibling, me).wait_recv()
            for j, chip in enumerate(chips):
                copy(a, 4 + j, (*chip, 1 - mc), me).wait_recv()
        for cp in first + passed:
            cp.wait_send()
        for cp in mine:
            cp.wait()

    return pl.pallas_call(
        body, name=name, out_shape=tuple(S((N_DEV,) + x.shape, x.dtype) for x in xs),
        in_specs=[ANY] * na, out_specs=(ANY,) * na,
        scratch_shapes=[pltpu.SemaphoreType.DMA((7 * na,)), pltpu.SemaphoreType.DMA((7 * na,)),
                        pltpu.SemaphoreType.DMA((na,))],
        compiler_params=pltpu.CompilerParams(has_side_effects=True))(*xs)


def _exchange(sends, n_slots, src_index, rels, *, name):
    na = len(sends)

    def body(*refs):
        send_refs, recv_refs = refs[:na], refs[na:2 * na]
        send_sems, recv_sems = refs[2 * na:]
        cps = [pltpu.make_async_remote_copy(
            src_ref=send_refs[a].at[src_index(j)], dst_ref=recv_refs[a].at[j],
            send_sem=send_sems.at[n_slots * a + j], recv_sem=recv_sems.at[n_slots * a + j],
            device_id=_peer(rels[j]), device_id_type=MESH) for a in range(na) for j in range(n_slots)]
        for cp in cps:
            cp.start()
        for cp in cps:
            cp.wait()

    return pl.pallas_call(
        body, name=name, out_shape=tuple(S((n_slots,) + s.shape[1:], s.dtype) for s in sends),
        in_specs=[ANY] * na, out_specs=(ANY,) * na,
        scratch_shapes=[pltpu.SemaphoreType.DMA((n_slots * na,)), pltpu.SemaphoreType.DMA((n_slots * na,))],
        compiler_params=pltpu.CompilerParams(has_side_effects=True))(*sends)


_FLIPS = [(fx, fy, fc) for fx in (0, 1) for fy in (0, 1) for fc in (0, 1)][1:]


def _flip_peer(flip):
    x, y, c = lax.axis_index("x"), lax.axis_index("y"), lax.axis_index("c")
    return tuple(1 - a if f else a for a, f in zip((x, y, c), flip))


def _dev_index(p):
    return 4 * p[0] + 2 * p[1] + p[2]


HBM_SPEC = pl.BlockSpec(memory_space=pltpu.HBM)
SEM_SPEC = pl.BlockSpec(memory_space=pltpu.SEMAPHORE)


def _direct_start(srcs, lands, per_peer, *, name):
    na = len(srcs)

    def body(*refs):
        src_refs, land_refs = refs[:na], refs[na:2 * na]
        send_sems, recv_sems = refs[2 * na], refs[2 * na + 1]
        token = refs[-1]
        me = _dev_index((lax.axis_index("x"), lax.axis_index("y"), lax.axis_index("c")))
        for a in range(na):
            for r, flip in enumerate(_FLIPS):
                peer = _flip_peer(flip)
                src = src_refs[a].at[_dev_index(peer)] if per_peer else src_refs[a]
                pltpu.make_async_remote_copy(
                    src_ref=src, dst_ref=land_refs[a].at[me], send_sem=send_sems.at[7 * a + r],
                    recv_sem=recv_sems.at[7 * a + r], device_id=peer, device_id_type=MESH).start()
        token[...] = jnp.zeros_like(token)

    hbm = lambda t: pltpu.with_memory_space_constraint(t, pltpu.HBM)
    out = pl.pallas_call(
        body, name=name,
        out_shape=(pltpu.SemaphoreType.DMA((7 * na,)), pltpu.SemaphoreType.DMA((7 * na,)))
        + tuple(pltpu.HBM(t.shape, t.dtype) for t in list(srcs) + list(lands)) + (S((8, LANE), f32),),
        in_specs=[HBM_SPEC] * (2 * na),
        out_specs=(SEM_SPEC, SEM_SPEC) + (HBM_SPEC,) * (2 * na) + (pl.BlockSpec(memory_space=pltpu.VMEM),),
        input_output_aliases={i: 2 + i for i in range(2 * na)},
        compiler_params=pltpu.CompilerParams(has_side_effects=pltpu.SideEffectType.DATAFLOW_SIDE_EFFECTING))(
            *[hbm(t) for t in srcs], *[hbm(t) for t in lands])
    return out[0], out[1], list(out[2:2 + na]), list(out[2 + na:2 + 2 * na]), out[-1]


def _direct_wait(send_sems, recv_sems, srcs, lands, per_peer, after, *, name):
    na = len(srcs)

    def body(*refs):
        src_refs, land_refs = refs[:na], refs[na:2 * na]
        ssem, rsem = refs[2 * na], refs[2 * na + 1]
        me = _dev_index((lax.axis_index("x"), lax.axis_index("y"), lax.axis_index("c")))
        for a in range(na):
            for r, flip in enumerate(_FLIPS):
                peer = _flip_peer(flip)
                src = src_refs[a].at[_dev_index(peer)] if per_peer else src_refs[a]
                cp = pltpu.make_async_remote_copy(
                    src_ref=src, dst_ref=land_refs[a].at[me], send_sem=ssem.at[7 * a + r],
                    recv_sem=rsem.at[7 * a + r], device_id=peer, device_id_type=MESH)
                cp.wait_send()
                cp.wait_recv()

    out = pl.pallas_call(
        body, name=name, out_shape=tuple(pltpu.HBM(t.shape, t.dtype) for t in list(srcs) + list(lands)),
        in_specs=[HBM_SPEC] * (2 * na) + [SEM_SPEC, SEM_SPEC, ANY], out_specs=(HBM_SPEC,) * (2 * na),
        input_output_aliases={i: i for i in range(2 * na)},
        compiler_params=pltpu.CompilerParams(has_side_effects=pltpu.SideEffectType.DATAFLOW_SIDE_EFFECTING))(
            *srcs, *lands, send_sems, recv_sems, after)
    return list(out[:na]), list(out[na:])


def _pack_small(parts, width):
    rows, offs, r = [], [], 0
    for a in parts:
        n = a.size
        nr = -(-n // width)
        flat = a.reshape(-1).astype(f32)
        if nr * width != n:
            flat = jnp.pad(flat, (0, nr * width - n))
        rows.append(flat.reshape(nr, width))
        offs.append((r, nr))
        r += nr
    buf = jnp.concatenate(rows, axis=0)
    pad = (-r) % 8
    if pad:
        buf = jnp.pad(buf, ((0, pad), (0, 0)))
    return buf, offs


def _unpack_small(buf, off, shape):
    r, nr = off
    return buf[r:r + nr].reshape(-1)[:math.prod(shape)].reshape(shape)


def _local_step(x, target, meta, a_w_in, small, start_token, late_weights, grads_ready):
    SEQ, D = x.shape
    n_meta = meta.shape[0]
    first_row = PADF - n_meta
    H = small["a_log"].shape[-1]

    head = jnp.concatenate([jnp.zeros((first_row, D), f32), meta], axis=0)

    def lanes(a):
        return jnp.pad(a.reshape(1, -1), ((0, 0), (0, LANE - a.size)))

    def after_token(a, token):
        return a if token is None else a + token[0:1, 0:1]

    alog, dtb = after_token(lanes(small["a_log"][0]), start_token), lanes(small["a_dt_bias"][0])
    a_conv, b_conv = small["a_conv"][0], small["b_conv"][0]
    nw = small["a_norm"][0].reshape(1, DH)
    lmg, lmb, lfg, lfb = small["ln_mix_g"], small["ln_mix_b"], small["ln_ffn_g"], small["ln_ffn_b"]

    h0, pre_a, z, raw, q, k, v, beta, g = _gdn_in_fwd(x, head, a_w_in, a_conv, alog, dtb, first_row=first_row, H=H)
    o, y, s_all, t_all = _delta_fwd(q, k, v, g, beta, z, nw, H=H)
    wts = late_weights(y)
    pre1, h1 = _out_res_ln(y, wts["a_w_out"], h0, lmg[0:1], lmb[0:1], first_row=first_row, name="gdn_out_ln")
    up0, act0, pre2, h2 = _ffn_fwd(h1, wts["ffn_w_up"][0], small["ffn_conv"][0], wts["ffn_w_down"][0],
                                   lfg[0:1], lfb[0:1], first_row=first_row, name="ffn_fwd0")
    proj_b, bu, pre3, h3 = _sc_fwd(h2, wts["b_w_in"], b_conv, wts["b_w_out"], lmg[1:2], lmb[1:2], first_row=first_row)
    up1, act1, pre4, h4 = _ffn_fwd(h3, wts["ffn_w_up"][1], small["ffn_conv"][1], wts["ffn_w_down"][1],
                                   lfg[1:2], lfb[1:2], first_row=first_row, name="ffn_fwd1")
    gw, gs = {}, {}
    alpha = ALPHA
    dpre4, dlfg1, dlfb1, loss_tile = _loss_head(h4, target, pre4, lfg[1:2], first_row=first_row)

    def ffn_backward(dpre, up, act, h_in, layer, tag, ln_in, token=None):
        dup, dcw = _ffn_bwd(dpre, up, wts["ffn_w_down"][layer], after_token(small["ffn_conv"][layer], token),
                            name="ffn_bwd" + tag)
        dwd = _linear_dw(act, dpre, name="dw_down" + tag)
        dwu = _linear_dw(h_in, dup, name="dw_up" + tag)
        dpre_in, dg, db = _linear_dx(dup, wts["ffn_w_up"][layer], dpre, alpha=alpha, name="dx_up" + tag,
                                     ln=ln_in + (first_row,))
        return dpre_in, dg, db, dwu, dwd, dcw[0:3]

    dpre3, dlmg1, dlmb1, dwu1, dwd1, dcf1 = ffn_backward(dpre4, up1, act1, h3, 1, "1", (pre3, lmg[1:2]))

    dbu = _linear_dx(dpre3, wts["b_w_out"], None, alpha=0.0, name="dx_b_out")
    dwb_out = _linear_dw(bu, dpre3, name="dw_b_out")
    dproj_b, dcb = _sc_bwd(dbu, proj_b, b_conv)
    dwb_in = _linear_dw(h2, dproj_b, name="dw_b_in")
    dpre2, dlfg0, dlfb0 = _linear_dx(dproj_b, wts["b_w_in"], dpre3, alpha=alpha, name="dx_b_in",
                                     ln=(pre2, lfg[0:1], first_row))
    token = grads_ready("layer1", dict(ffn_w_up=dwu1, ffn_w_down=dwd1, b_w_in=dwb_in, b_w_out=dwb_out))

    dpre1, dlmg0, dlmb0, dwu0, dwd0, dcf0 = ffn_backward(dpre2, up0, act0, h1, 0, "0", (pre1, lmg[0:1]), token)
    token = grads_ready("layer0", dict(ffn_w_up=dwu0, ffn_w_down=dwd0))

    dy = _linear_dx(dpre1, wts["a_w_out"], None, alpha=0.0, name="dx_a_out")
    gw["a_w_out"] = [_linear_dw(y, dpre1, name="dw_a_out")]
    dq, dk, dv, dz, dg_, dbeta, dnw = _delta_bwd(dy, o, z, after_token(nw, token), q, k, v, g, beta, s_all, t_all, H=H)
    dproj_a, dca, dal, ddt = _gdn_in_bwd(dq, dk, dv, dz, dg_, dbeta, pre_a, raw, a_conv, alog, dtb,
                                         first_row=first_row, H=H)
    gw["a_w_in"] = [_linear_dw(h0, dproj_a, name="dw_a_in")]
    grad_x, dhead = _linear_dx(dproj_a, a_w_in, dpre1, alpha=alpha, name="dx_a_in", front_rows=PADF)

    gs["meta"] = dhead[first_row:PADF]
    gs["a_conv"] = dca[0:a_conv.shape[0]][None]
    gs["a_log"] = dal[0:1, 0:H]
    gs["a_dt_bias"] = ddt[0:1, 0:H]
    gs["a_norm"] = dnw[0:1]
    gs["b_conv"] = dcb[0:b_conv.shape[0]][None]
    gs["ln_mix_g"] = jnp.stack([dlmg0[0], dlmg1[0]])
    gs["ln_mix_b"] = jnp.stack([dlmb0[0], dlmb1[0]])
    gs["ffn_conv"] = jnp.stack([dcf0, dcf1])
    gs["ln_ffn_g"] = jnp.stack([dlfg0[0], dlfg1[0]])
    gs["ln_ffn_b"] = jnp.stack([dlfb0[0], dlfb1[0]])
    return loss_tile, grad_x, gw, gs


_BIG = ("a_w_in", "a_w_out", "b_w_in", "b_w_out", "ffn_w_up", "ffn_w_down")
_BIG_COL = ("a_w_in", "b_w_in", "ffn_w_up")
_SMALL = ("meta", "a_conv", "a_log", "a_dt_bias", "a_norm", "b_conv", "ln_mix_g", "ln_mix_b",
          "ffn_conv", "ln_ffn_g", "ln_ffn_b")
_SMALL_SHARDED = ("meta", "a_conv", "b_conv", "ffn_conv")
_ORDER = ("meta", "a_w_in", "a_conv", "a_log", "a_dt_bias", "a_norm", "a_w_out", "b_w_in", "b_conv", "b_w_out",
          "ln_mix_g", "ln_mix_b", "ffn_w_up", "ffn_conv", "ffn_w_down", "ln_ffn_g", "ln_ffn_b")


def _a_w_in_map(H):
    W4 = 4 * H * DH
    return [(0, W4, 0), (W4, W4 + H, W4), (W4 + H, W4 + 2 * H, W4 + LANE)], W4 + 2 * LANE


def kernel(x, meta, a_w_in, a_conv, a_log, a_dt_bias, a_norm, a_w_out, b_w_in, b_conv, b_w_out, ln_mix_g, ln_mix_b, ffn_w_up, ffn_conv, ffn_w_down, ln_ffn_g, ln_ffn_b, loss_target, m_meta, m_a_w_in, m_a_conv, m_a_log, m_a_dt_bias, m_a_norm, m_a_w_out, m_b_w_in, m_b_conv, m_b_w_out, m_ln_mix_g, m_ln_mix_b, m_ffn_w_up, m_ffn_conv, m_ffn_w_down, m_ln_ffn_g, m_ln_ffn_b, v_meta, v_a_w_in, v_a_conv, v_a_log, v_a_dt_bias, v_a_norm, v_a_w_out, v_b_w_in, v_b_conv, v_b_w_out, v_ln_mix_g, v_ln_mix_b, v_ffn_w_up, v_ffn_conv, v_ffn_w_down, v_ln_ffn_g, v_ln_ffn_b):
    wloc = dict(meta=meta, a_w_in=a_w_in, a_conv=a_conv, a_log=a_log, a_dt_bias=a_dt_bias, a_norm=a_norm,
                a_w_out=a_w_out, b_w_in=b_w_in, b_conv=b_conv, b_w_out=b_w_out, ln_mix_g=ln_mix_g, ln_mix_b=ln_mix_b,
                ffn_w_up=ffn_w_up, ffn_conv=ffn_conv, ffn_w_down=ffn_w_down, ln_ffn_g=ln_ffn_g, ln_ffn_b=ln_ffn_b)
    mloc = dict(meta=m_meta, a_w_in=m_a_w_in, a_conv=m_a_conv, a_log=m_a_log, a_dt_bias=m_a_dt_bias, a_norm=m_a_norm,
                a_w_out=m_a_w_out, b_w_in=m_b_w_in, b_conv=m_b_conv, b_w_out=m_b_w_out, ln_mix_g=m_ln_mix_g,
                ln_mix_b=m_ln_mix_b, ffn_w_up=m_ffn_w_up, ffn_conv=m_ffn_conv, ffn_w_down=m_ffn_w_down,
                ln_ffn_g=m_ln_ffn_g, ln_ffn_b=m_ln_ffn_b)
    vloc = dict(meta=v_meta, a_w_in=v_a_w_in, a_conv=v_a_conv, a_log=v_a_log, a_dt_bias=v_a_dt_bias, a_norm=v_a_norm,
                a_w_out=v_a_w_out, b_w_in=v_b_w_in, b_conv=v_b_conv, b_w_out=v_b_w_out, ln_mix_g=v_ln_mix_g,
                ln_mix_b=v_ln_mix_b, ffn_w_up=v_ffn_w_up, ffn_conv=v_ffn_conv, ffn_w_down=v_ffn_w_down,
                ln_ffn_g=v_ln_ffn_g, ln_ffn_b=v_ln_ffn_b)
    H = a_log.shape[-1]
    mx, my, mc = lax.axis_index("x"), lax.axis_index("y"), lax.axis_index("c")
    me = 4 * mx + 2 * my + mc

    a_map, a_cols = _a_w_in_map(H)
    col_maps = {"a_w_in": (a_map, a_cols)}
    for n in ("b_w_in", "ffn_w_up"):
        ncols = N_DEV * wloc[n].shape[-1]
        col_maps[n] = ([(0, ncols, 0)], ncols)
    sm_sh = [wloc[n] for n in _SMALL_SHARDED]
    sbuf, soffs = _pack_small(sm_sh, 128)
    g_a_w_in, sg = _all_gather([_bf(wloc["a_w_in"]), sbuf], name="gather_first")
    w_a_in = _assemble_cols(g_a_w_in, *col_maps["a_w_in"], name="assemble_a_w_in")[0]
    late = [n for n in _BIG if n != "a_w_in"]
    ssem, rsem, srcs_t, lands_t, start_token = _direct_start(
        [_bf(wloc[n]) for n in late], [lax.empty((N_DEV,) + wloc[n].shape, bf16) for n in late], False,
        name="gather_rest_start")

    def late_weights(after):
        srcs_d, landed = _direct_wait(ssem, rsem, srcs_t, lands_t, False, after, name="gather_rest_wait")
        wts = {}
        for n, own, got in zip(late, srcs_d, landed):
            full = lax.dynamic_update_index_in_dim(got, own, me, 0)
            if n in _BIG_COL:
                wts[n] = _assemble_cols(full, *col_maps[n], name="assemble_" + n)
            else:
                wts[n] = _rows_full(full)
        for n in ("a_w_out", "b_w_in", "b_w_out"):
            wts[n] = wts[n][0]
        return wts

    small = {n: wloc[n] for n in _SMALL}
    for n, off in zip(_SMALL_SHARDED, soffs):
        sh = wloc[n].shape
        parts = jnp.stack([_unpack_small(sg[d], off, sh) for d in range(N_DEV)])
        nd = len(sh)
        small[n] = jnp.transpose(parts, tuple(range(1, nd)) + (0, nd)).reshape(sh[:-1] + (N_DEV * sh[-1],))

    def split(n, dws, tag):
        if n in _BIG_COL:
            return _split_cols(dws, col_maps[n][0], wloc[n].shape[-1], name="split_" + n + tag)
        return _split_rows(dws, wloc[n].shape[-2], name="split_" + n + tag)

    sent = {}

    def grads_ready(stage, grads):
        names = sorted(grads)
        parts = [split(n, [grads[n]], "_" + stage) for n in names]
        handles = _direct_start([p[1] for p in parts], [jnp.zeros(p[1].shape, bf16) for p in parts], True,
                                name="grads_" + stage + "_start")
        sent[stage] = (names, [p[0] for p in parts], handles)
        return handles[4]

    loss_tile, grad_x, gw, gs = _local_step(x[0], loss_target[0], small["meta"], w_a_in, small, start_token,
                                            late_weights, grads_ready)

    last = ("a_w_in", "a_w_out")
    s32, s16 = {}, {}
    for n in last:
        s32[n], s16[n] = split(n, gw[n], "")
    recv1 = dict(zip(last, _exchange([s16[n] for n in last], 4, lambda j: 2 * j + (1 - lax.axis_index("c")),
                                     ["c"] * 4, name="grad_to_sibling")))
    chip = 2 * mx + my
    others = [2 * (1 - mx) + my, 2 * mx + (1 - my), 2 * (1 - mx) + (1 - my)]
    sidx = jnp.stack([2 * o + mc for o in others]).astype(jnp.int32)
    ridx = jnp.stack(others).astype(jnp.int32)
    to_send = [_pair_add(s32[n], recv1[n], sidx, ridx, name="pair_add_" + n) for n in last]
    recv2 = dict(zip(last, _exchange(to_send, 3, lambda j: j, ["x", "y", "xy"], name="grad_to_chips")))
    own = jnp.stack([me, chip]).astype(jnp.int32)
    big_out = {n: _adamw_shard(s32[n], recv1[n], recv2[n], wloc[n], mloc[n], vloc[n], own, name="adamw_" + n)
               for n in last}

    got = {}
    for stage, (names, own32, (ssem_g, rsem_g, srcs_g, lands_g, _)) in sent.items():
        _, landed = _direct_wait(ssem_g, rsem_g, srcs_g, lands_g, True, recv2["a_w_in"], name="grads_" + stage + "_wait")
        for n, o32, r in zip(names, own32, landed):
            got.setdefault(n, []).append((stage, o32, r))
    me1 = jnp.stack([me]).astype(jnp.int32)
    for n, parts in got.items():
        parts = sorted(parts, key=lambda t: t[0])
        big_out[n] = _adamw_direct([p[1] for p in parts], [p[2] for p in parts], wloc[n], mloc[n], vloc[n], me1,
                                   name="adamw_" + n)

    names = list(_SMALL)
    pbuf, poffs = _pack_small([gs[n] for n in names] + [loss_tile[0:1, 0:1]], 1024)
    psum = _sum_devices(_all_gather([pbuf], name="gather_small_grads")[0])
    loss = psum[poffs[-1][0], 0]
    g_small = {}
    for n, off in zip(names, poffs[:-1]):
        full_shape = gs[n].shape
        gfull = _unpack_small(psum, off, full_shape)
        if n in _SMALL_SHARDED:
            ns = wloc[n].shape[-1]
            gfull = lax.dynamic_slice_in_dim(gfull, me * ns, ns, axis=gfull.ndim - 1)
        g_small[n] = gfull.reshape(wloc[n].shape)
    gbuf, aoffs = _pack_small([g_small[n] for n in names], 128)
    wbuf, _ = _pack_small([wloc[n] for n in names], 128)
    mbuf, _ = _pack_small([mloc[n] for n in names], 128)
    vbuf, _ = _pack_small([vloc[n] for n in names], 128)
    _, d_s, m_s, v_s = _adamw([gbuf], wbuf, mbuf, vbuf, name="adamw_small")

    grads, deltas, new_m, new_v = {}, {}, {}, {}
    for n in _BIG:
        grads[n], deltas[n], new_m[n], new_v[n] = big_out[n]
    for n, off in zip(names, aoffs):
        sh = wloc[n].shape
        grads[n] = g_small[n]
        deltas[n], new_m[n], new_v[n] = (_unpack_small(b_, off, sh) for b_ in (d_s, m_s, v_s))
    return (loss, grad_x[None], *[grads[n] for n in _ORDER], *[deltas[n] for n in _ORDER],
            *[new_m[n] for n in _ORDER], *[new_v[n] for n in _ORDER])
```

```python
import math

import jax
import jax.numpy as jnp
from jax import lax
from jax.experimental import pallas as pl
from jax.experimental.pallas import tpu as pltpu

f32, bf16 = jnp.float32, jnp.bfloat16
S = jax.ShapeDtypeStruct
HI = lax.Precision.HIGHEST
HI3 = lax.Precision.HIGH
MESH = pl.DeviceIdType.MESH

V7X_VMEM_LIMIT = 56 * 1024 * 1024
LANE = 128
DH = 128
CH = 64
PADF = 256
TM = 256
TMM = 768
N_DEV = 8
BWD_HEAD_GROUP = 4

DEPTH = 2
ALPHA = (2.0 * DEPTH) ** 0.25
LN_EPS = 1e-5
RMS_EPS = 1e-6
L2_EPS = 1e-6
ADAM_LR, ADAM_B1, ADAM_B2, ADAM_EPS, ADAM_WD, ADAM_STEP = 0.001, 0.9, 0.999, 1e-08, 0.01, 10


def _cp(**kw):
    return pltpu.CompilerParams(vmem_limit_bytes=V7X_VMEM_LIMIT, **kw)


def _bf(x):
    return x.astype(bf16)


def _dot(a, b, precision=None):
    return jnp.dot(a, b, preferred_element_type=f32, precision=precision)


def _dot_nt(a, b):
    return lax.dot_general(a, b, (((1,), (1,)), ((), ())), preferred_element_type=f32)


def _dot_tn(a, b):
    return lax.dot_general(a, b, (((0,), (0,)), ((), ())), preferred_element_type=f32)


def _sigmoid(x):
    return 1.0 / (1.0 + jnp.exp(-x))


def _load_once(pairs, sem):
    @pl.when(pl.program_id(0) == 0)
    def _():
        cps = [pltpu.make_async_copy(src, dst, sem.at[n]) for n, (src, dst) in enumerate(pairs)]
        for c in cps:
            c.start()
        for c in cps:
            c.wait()


def _row_ids(i, tm, width):
    return i * tm + lax.broadcasted_iota(jnp.int32, (tm, width), 0)


def _ln_fwd(pre, g, b, rows, first_row):
    mu = jnp.mean(pre, axis=-1, keepdims=True)
    xc = pre - mu
    var = jnp.mean(xc * xc, axis=-1, keepdims=True)
    y = xc * lax.rsqrt(var + LN_EPS) * g + b
    return jnp.where(rows >= first_row, y, 0.0)


ANY = pl.BlockSpec(memory_space=pl.ANY)


def _taps_back(prev8, x, kw):
    xe = jnp.concatenate([prev8, x], axis=0)
    return [pltpu.roll(xe, kw - 1 - j, 0)[8:] for j in range(kw - 1)] + [x]


def _taps_ahead(x, next8, kw):
    n = x.shape[0]
    xe = jnp.concatenate([x, next8], axis=0)
    return [pltpu.roll(xe, n + 8 - (kw - 1 - j), 0)[:n] for j in range(kw - 1)] + [x]


def _conv(cw, taps):
    acc = cw[0:1, :] * taps[0]
    for j in range(1, len(taps)):
        acc = acc + cw[j:j + 1, :] * taps[j]
    return acc


def _linear_dw(x, dy, *, name):
    L, K = x.shape
    N = dy.shape[1]
    tm = TMM if L % TMM == 0 else TM
    tn = LANE
    for d in range(N // LANE, 0, -1):
        if (N // LANE) % d == 0 and K * d * LANE * 4 <= 9 * 1024 * 1024:
            tn = d * LANE
            break

    def body(x_ref, dy_ref, o_ref):
        @pl.when(pl.program_id(1) == 0)
        def _():
            o_ref[...] = jnp.zeros_like(o_ref)
        o_ref[...] += _dot_tn(_bf(x_ref[...]), _bf(dy_ref[...]))

    return pl.pallas_call(
        body, name=name, grid=(N // tn, L // tm), out_shape=S((K, N), f32),
        in_specs=[pl.BlockSpec((tm, K), lambda j, i: (i, 0)), pl.BlockSpec((tm, tn), lambda j, i: (i, j))],
        out_specs=pl.BlockSpec((K, tn), lambda j, i: (0, j)),
        compiler_params=_cp(dimension_semantics=("arbitrary", "arbitrary")))(x, dy)


def _ln_bwd_rows(dout, pre, g, rows, first_row):
    mu = jnp.mean(pre, axis=-1, keepdims=True)
    xc = pre - mu
    rstd = lax.rsqrt(jnp.mean(xc * xc, axis=-1, keepdims=True) + LN_EPS)
    xh = xc * rstd
    dy = jnp.where(rows >= first_row, dout, 0.0)
    dxh = dy * g
    dpre = rstd * (dxh - jnp.mean(dxh, axis=-1, keepdims=True) - xh * jnp.mean(dxh * xh, axis=-1, keepdims=True))
    return dpre, jnp.sum(dy * xh, axis=0, keepdims=True), jnp.sum(dy, axis=0, keepdims=True)


def _linear_dx(dy, w, *, name):
    L, N = dy.shape
    K = w.shape[0]
    tm = TM

    def body(dy_ref, w_hbm, o_ref, w_vmem, sem):
        _load_once([(w_hbm, w_vmem)], sem)
        o_ref[...] = _dot_nt(_bf(dy_ref[...]), w_vmem[...])

    row = lambda i: (i, 0)
    return pl.pallas_call(
        body, name=name, grid=(L // tm,), out_shape=S((L, K), f32),
        in_specs=[pl.BlockSpec((tm, N), row), ANY], out_specs=pl.BlockSpec((tm, K), row),
        scratch_shapes=[pltpu.VMEM((K, N), w.dtype), pltpu.SemaphoreType.DMA((1,))],
        compiler_params=_cp(dimension_semantics=("arbitrary",)))(dy, w)


def _out_res_ln(y, w, h, g, b, *, first_row, name):
    L, K = y.shape
    D = w.shape[1]
    tm = TM
    alpha = ALPHA

    def body(y_ref, w_hbm, h_ref, g_ref, b_ref, pre_ref, out_ref, w_vmem, sem):
        _load_once([(w_hbm, w_vmem)], sem)
        pre = alpha * h_ref[...] + _dot(_bf(y_ref[...]), w_vmem[...])
        pre_ref[...] = pre
        out_ref[...] = _ln_fwd(pre, g_ref[...], b_ref[...], _row_ids(pl.program_id(0), tm, D), first_row)

    row = lambda i: (i, 0)
    fix = lambda i: (0, 0)
    return pl.pallas_call(
        body, name=name, grid=(L // tm,), out_shape=(S((L, D), f32), S((L, D), f32)),
        in_specs=[pl.BlockSpec((tm, K), row), ANY, pl.BlockSpec((tm, D), row),
                  pl.BlockSpec((1, D), fix), pl.BlockSpec((1, D), fix)],
        out_specs=(pl.BlockSpec((tm, D), row), pl.BlockSpec((tm, D), row)),
        scratch_shapes=[pltpu.VMEM((K, D), w.dtype), pltpu.SemaphoreType.DMA((1,))],
        compiler_params=_cp(dimension_semantics=("arbitrary",)))(y, w, h, g, b)


def _gdn_in_fwd(x, head, w_full, conv_w, alog, dtb, *, first_row, H):
    D = x.shape[1]
    L = PADF + x.shape[0]
    W = H * DH
    NW = w_full.shape[1]
    KW = conv_w.shape[0]
    tm = TM
    pb = PADF // tm

    def body(x_ref, head_ref, w_hbm, cw_ref, alog_ref, dtb_ref,
             h_ref, pre_ref, z_ref, raw_ref, q_ref, k_ref, v_ref, beta_ref, g_ref,
             w_vmem, carry, sem):
        i = pl.program_id(0)
        _load_once([(w_hbm, w_vmem)], sem)

        @pl.when(i == 0)
        def _():
            carry[...] = jnp.zeros_like(carry)

        hv = jnp.where(i < pb, head_ref[...], x_ref[...])
        h_ref[...] = hv
        hb = _bf(hv)
        outs = (q_ref, k_ref, v_ref)
        for s in range(3):
            pre = _dot(hb, w_vmem[:, s * W:(s + 1) * W])
            pre_ref[:, s * W:(s + 1) * W] = pre
            c = _conv(cw_ref[:, s * W:(s + 1) * W], _taps_back(carry[s], pre, KW))
            carry[s] = pre[tm - 8:tm, :]
            sl = c * _sigmoid(c)
            if s < 2:
                scale = DH ** -0.5 if s == 0 else 1.0
                for hh in range(H):
                    seg = sl[:, hh * DH:(hh + 1) * DH]
                    r = lax.rsqrt(jnp.sum(seg * seg, axis=-1, keepdims=True) + L2_EPS)
                    outs[s][:, hh * DH:(hh + 1) * DH] = seg * (r * scale)
            else:
                v_ref[...] = sl
        z_ref[...] = _dot(hb, w_vmem[:, 3 * W:4 * W])
        raw = _dot(hb, w_vmem[:, 4 * W:4 * W + 2 * LANE])
        raw_ref[...] = raw
        ok = (_row_ids(i, tm, LANE) >= first_row) & (lax.broadcasted_iota(jnp.int32, (tm, LANE), 1) < H)
        beta_ref[...] = jnp.where(ok, _sigmoid(raw[:, :LANE]), 0.0)
        a = raw[:, LANE:] + dtb_ref[...]
        sp = jnp.maximum(a, 0.0) + jnp.log(1.0 + jnp.exp(-jnp.abs(a)))
        gv = jnp.where(ok, -jnp.exp(alog_ref[...]) * sp, 0.0)
        g_ref[...] = _dot(_chunk_tri(tm, lower=True), gv, HI)

    row = lambda i: (i, 0)
    fix = lambda i: (0, 0)
    out_shape = (S((L, D), f32), S((L, 3 * W), f32), S((L, W), f32), S((L, 2 * LANE), f32),
                 S((L, W), f32), S((L, W), f32), S((L, W), f32), S((L, LANE), f32), S((L, LANE), f32))
    out_specs = (pl.BlockSpec((tm, D), row),
                 pl.BlockSpec((tm, 3 * W), row), pl.BlockSpec((tm, W), row), pl.BlockSpec((tm, 2 * LANE), row),
                 pl.BlockSpec((tm, W), row), pl.BlockSpec((tm, W), row), pl.BlockSpec((tm, W), row),
                 pl.BlockSpec((tm, LANE), row), pl.BlockSpec((tm, LANE), row))
    return pl.pallas_call(
        body, name="gdn_in_fwd", grid=(L // tm,), out_shape=out_shape,
        in_specs=[pl.BlockSpec((tm, D), lambda i: (jnp.maximum(i - pb, 0), 0)),
                  pl.BlockSpec((tm, D), lambda i: (jnp.minimum(i, pb - 1), 0)), ANY, pl.BlockSpec((KW, 3 * W), fix),
                  pl.BlockSpec((1, LANE), fix), pl.BlockSpec((1, LANE), fix)],
        out_specs=out_specs,
        scratch_shapes=[pltpu.VMEM((D, NW), w_full.dtype), pltpu.VMEM((3, 8, W), f32), pltpu.SemaphoreType.DMA((1,))],
        compiler_params=_cp(dimension_semantics=("arbitrary",)))(x, head, w_full, conv_w, alog, dtb)


def _gdn_in_bwd(dq, dk, dv, dz, dg, dbeta, pre, raw, conv_w, alog, dtb, w_full, res, *, first_row, H):
    L = dq.shape[0]
    D = res.shape[1]
    W = H * DH
    KW = conv_w.shape[0]
    tm = TM
    nb = L // tm
    NW = 4 * W + 2 * LANE
    fb = PADF // tm
    alpha = ALPHA

    def body(dq_ref, dk_ref, dv_ref, dz_ref, dg_ref, dbeta_ref, pre_ref, hq_ref, hk_ref, hv_ref, raw_ref,
             cw_ref, alog_ref, dtb_ref, w_hbm, res_ref,
             dproj_ref, dcw_ref, dal_ref, ddt_ref, dx_ref, dfront_ref, w_vmem, carry, tmp, sem):
        i = pl.program_id(0)
        blk = nb - 1 - i
        _load_once([(w_hbm, w_vmem)], sem)

        @pl.when(i == 0)
        def _():
            carry[...] = jnp.zeros_like(carry)
            dcw_ref[...] = jnp.zeros_like(dcw_ref)
            dal_ref[...] = jnp.zeros_like(dal_ref)
            ddt_ref[...] = jnp.zeros_like(ddt_ref)

        halos = (hq_ref, hk_ref, hv_ref)
        douts = (dq_ref, dk_ref, dv_ref)
        for s in range(3):
            sec = slice(s * W, (s + 1) * W)
            pre = pre_ref[:, sec]
            c = _conv(cw_ref[:, sec], _taps_back(jnp.where(blk > 0, halos[s][...], 0.0), pre, KW))
            sig = _sigmoid(c)
            sl = c * sig
            if s < 2:
                scale = DH ** -0.5 if s == 0 else 1.0
                for hh in range(H):
                    hs = slice(hh * DH, (hh + 1) * DH)
                    seg = sl[:, hs]
                    r = lax.rsqrt(jnp.sum(seg * seg, axis=-1, keepdims=True) + L2_EPS)
                    n = seg * r
                    dqs = douts[s][:, hs]
                    tmp[:, hs] = (scale * r) * (dqs - n * jnp.sum(n * dqs, axis=-1, keepdims=True))
                dsl = tmp[...]
            else:
                dsl = dv_ref[...]
            dc = dsl * (sig * (1.0 + c * (1.0 - sig)))
            ahead = _taps_ahead(dc, carry[s], KW)
            carry[s] = dc[0:8, :]
            dproj_ref[:, sec] = _bf(_conv(cw_ref[:, sec], ahead))
            for j in range(KW):
                dcw_ref[j:j + 1, sec] += jnp.sum(ahead[j] * pre, axis=0, keepdims=True)
        dproj_ref[:, 3 * W:4 * W] = _bf(dz_ref[...])
        raw_v = raw_ref[...]
        ok = (_row_ids(blk, tm, LANE) >= first_row) & (lax.broadcasted_iota(jnp.int32, (tm, LANE), 1) < H)
        beta = _sigmoid(raw_v[:, :LANE])
        dbraw = jnp.where(ok, dbeta_ref[...] * beta * (1.0 - beta), 0.0)
        a = raw_v[:, LANE:] + dtb_ref[...]
        sp = jnp.maximum(a, 0.0) + jnp.log(1.0 + jnp.exp(-jnp.abs(a)))
        nea = -jnp.exp(alog_ref[...])
        dgm = jnp.where(ok, _dot(_chunk_tri(tm, lower=False), dg_ref[...], HI), 0.0)
        daraw = dgm * nea * _sigmoid(a)
        dal_ref[0:1, :] += jnp.sum(dgm * nea * sp, axis=0, keepdims=True)
        ddt_ref[0:1, :] += jnp.sum(daraw, axis=0, keepdims=True)
        dproj_ref[:, 4 * W:4 * W + LANE] = _bf(dbraw)
        dproj_ref[:, 4 * W + LANE:4 * W + 2 * LANE] = _bf(daraw)
        dh = alpha * res_ref[...] + _dot_nt(dproj_ref[...], w_vmem[...])

        @pl.when(blk >= fb)
        def _():
            dx_ref[...] = dh

        @pl.when(blk < fb)
        def _():
            dfront_ref[...] = dh

    rev = lambda i: (nb - 1 - i, 0)
    fix = lambda i: (0, 0)

    def halo(col):
        return pl.BlockSpec((8, W), lambda i: (jnp.maximum((nb - 1 - i) * (tm // 8) - 1, 0), col))

    return pl.pallas_call(
        body, name="gdn_in_bwd", grid=(nb,),
        out_shape=(S((L, NW), bf16), S((8, 3 * W), f32), S((8, LANE), f32), S((8, LANE), f32),
                   S((L - PADF, D), f32), S((PADF, D), f32)),
        in_specs=[pl.BlockSpec((tm, W), rev)] * 4 + [pl.BlockSpec((tm, LANE), rev)] * 2
        + [pl.BlockSpec((tm, 3 * W), rev), halo(0), halo(1), halo(2), pl.BlockSpec((tm, 2 * LANE), rev),
           pl.BlockSpec((KW, 3 * W), fix), pl.BlockSpec((1, LANE), fix), pl.BlockSpec((1, LANE), fix),
           ANY, pl.BlockSpec((tm, D), rev)],
        out_specs=(pl.BlockSpec((tm, NW), rev), pl.BlockSpec((8, 3 * W), fix),
                   pl.BlockSpec((8, LANE), fix), pl.BlockSpec((8, LANE), fix),
                   pl.BlockSpec((tm, D), lambda i: (jnp.maximum(nb - 1 - i - fb, 0), 0)),
                   pl.BlockSpec((tm, D), lambda i: (jnp.minimum(nb - 1 - i, fb - 1), 0))),
        scratch_shapes=[pltpu.VMEM((D, NW), w_full.dtype), pltpu.VMEM((3, 8, W), f32), pltpu.VMEM((tm, W), f32),
                        pltpu.SemaphoreType.DMA((1,))],
        compiler_params=_cp(dimension_semantics=("arbitrary",)))(
            dq, dk, dv, dz, dg, dbeta, pre, pre, pre, pre, raw, conv_w, alog, dtb, w_full, res)


def _chunk_tri(n, lower):
    i = lax.broadcasted_iota(jnp.int32, (n, n), 0)
    j = lax.broadcasted_iota(jnp.int32, (n, n), 1)
    sh = int(math.log2(CH))
    same = lax.shift_right_logical(i, sh) == lax.shift_right_logical(j, sh)
    return (same & ((i >= j) if lower else (j >= i))).astype(f32)


def _tri_inv_many(ms, eye):
    ts = [eye - m for m in ms]
    ps = list(ms)
    for _ in range(int(math.log2(CH)) - 1):
        ps = [_dot(p, p, HI3) for p in ps]
        ts = [t + _dot(t, p, HI3) for t, p in zip(ts, ps)]
    return ts


def _chunk_local(q, k, v, gcol, grow, glast, bcol, ii, jj):
    dec = jnp.where(ii >= jj, jnp.exp(jnp.minimum(gcol - grow, 0.0)), 0.0)
    eg = jnp.exp(gcol)
    kb = k * bcol
    kbg = kb * eg
    vb = v * bcol
    qt = q * eg
    kt = k * jnp.exp(glast - gcol)
    kk = _dot_nt(_bf(kb), _bf(k))
    qk = _dot_nt(_bf(q), _bf(k))
    return dec, eg, kb, kbg, vb, qt, kt, kk, qk


def _delta_fwd(q, k, v, g, beta, z, nw, *, H):
    L = q.shape[0]
    W = H * DH
    rb = TM
    nc = rb // CH
    nblk = L // rb

    def body(q_ref, k_ref, v_ref, g_ref, b_ref, z_ref, nw_ref, o_ref, y_ref, s_out, t_out, s_scr):
        @pl.when(pl.program_id(0) == 0)
        def _():
            s_scr[...] = jnp.zeros_like(s_scr)

        ii = lax.broadcasted_iota(jnp.int32, (CH, CH), 0)
        jj = lax.broadcasted_iota(jnp.int32, (CH, CH), 1)
        eye = (ii == jj).astype(f32)
        nwv = nw_ref[...]

        heads = range(H)
        hsl = [slice(hh * DH, (hh + 1) * DH) for hh in heads]

        def chunk(c, carry):
            r0 = pl.multiple_of(c * CH, CH)
            rows = pl.ds(r0, CH)
            gam = g_ref[rows, :]
            gam_t = gam.T
            bb = b_ref[rows, :]
            glast = [gam[CH - 1:CH, hh:hh + 1] for hh in heads]
            loc = [_chunk_local(q_ref[rows, hsl[hh]], k_ref[rows, hsl[hh]], v_ref[rows, hsl[hh]],
                                gam[:, hh:hh + 1], gam_t[hh:hh + 1, :], glast[hh], bb[:, hh:hh + 1], ii, jj)
                   for hh in heads]
            st = [s_scr[hh] for hh in heads]
            zs = [z_ref[rows, hsl[hh]] for hh in heads]
            ts = _tri_inv_many([jnp.where(ii > jj, l[7] * l[0], 0.0) for l in loc], eye)
            us = [_dot(t, l[4], HI3) for t, l in zip(ts, loc)]
            ws = [_dot(t, l[3], HI3) for t, l in zip(ts, loc)]
            stb = [_bf(s) for s in st]
            vn = [u - _dot(_bf(w), sb) for u, w, sb in zip(us, ws, stb)]
            vnb = [_bf(x) for x in vn]
            snew = [s * jnp.exp(gl) + _dot_tn(_bf(l[6]), xb) for s, gl, l, xb in zip(st, glast, loc, vnb)]
            os_ = [_dot(_bf(l[5]), sb) + _dot(_bf(l[8] * l[0]), xb) for l, sb, xb in zip(loc, stb, vnb)]
            for hh in heads:
                o = os_[hh]
                s_out[c, hh] = st[hh]
                t_out[c, hh] = ts[hh]
                s_scr[hh] = snew[hh]
                o_ref[rows, hsl[hh]] = o
                on = o * lax.rsqrt(jnp.mean(o * o, axis=-1, keepdims=True) + RMS_EPS) * nwv
                y_ref[rows, hsl[hh]] = _bf(on * (zs[hh] * _sigmoid(zs[hh])))
            return carry

        lax.fori_loop(0, nc, chunk, 0)

    row = lambda i: (i, 0)
    fix = lambda i: (0, 0)
    return pl.pallas_call(
        body, name="delta_fwd", grid=(nblk,),
        out_shape=(S((L, W), f32), S((L, W), bf16), S((L // CH, H, DH, DH), f32), S((L // CH, H, CH, CH), f32)),
        in_specs=[pl.BlockSpec((rb, W), row)] * 3 + [pl.BlockSpec((rb, LANE), row)] * 2
        + [pl.BlockSpec((rb, W), row), pl.BlockSpec((1, DH), fix)],
        out_specs=(pl.BlockSpec((rb, W), row), pl.BlockSpec((rb, W), row),
                   pl.BlockSpec((nc, H, DH, DH), lambda i: (i, 0, 0, 0)),
                   pl.BlockSpec((nc, H, CH, CH), lambda i: (i, 0, 0, 0))),
        scratch_shapes=[pltpu.VMEM((H, DH, DH), f32)],
        compiler_params=_cp(dimension_semantics=("arbitrary",)))(q, k, v, g, beta, z, nw)


def _delta_bwd(dy, o, z, nw, q, k, v, g, beta, s_all, t_all, *, H):
    L = q.shape[0]
    W = H * DH
    rb = TM
    nc = rb // CH
    nblk = L // rb

    def body(dy_ref, o_ref, z_ref, nw_ref, q_ref, k_ref, v_ref, g_ref, b_ref, s_ref, t_ref,
             dq_ref, dk_ref, dv_ref, dz_ref, dg_ref, db_ref, dnw_ref, ds_scr):
        @pl.when(pl.program_id(0) == 0)
        def _():
            ds_scr[...] = jnp.zeros_like(ds_scr)
            dnw_ref[...] = jnp.zeros_like(dnw_ref)

        ii = lax.broadcasted_iota(jnp.int32, (CH, CH), 0)
        jj = lax.broadcasted_iota(jnp.int32, (CH, CH), 1)
        lane = lax.broadcasted_iota(jnp.int32, (CH, LANE), 1)
        last_row = lax.broadcasted_iota(jnp.int32, (CH, 1), 0) == CH - 1
        nwv = nw_ref[...]

        def chunk(cc, carry):
            c = nc - 1 - cc
            r0 = pl.multiple_of(c * CH, CH)
            rows = pl.ds(r0, CH)
            gam = g_ref[rows, :]
            gam_t = gam.T
            bb = b_ref[rows, :]

            def head(hh):
                hs = slice(hh * DH, (hh + 1) * DH)
                gcol, grow, glast = gam[:, hh:hh + 1], gam_t[hh:hh + 1, :], gam[CH - 1:CH, hh:hh + 1]
                bcol = bb[:, hh:hh + 1]
                qh, kh, vh = q_ref[rows, hs], k_ref[rows, hs], v_ref[rows, hs]
                oh, zh, dyh = o_ref[rows, hs], z_ref[rows, hs], dy_ref[rows, hs]
                t = t_ref[c, hh]
                st = s_ref[c, hh]
                dsn = ds_scr[hh]
                rms = lax.rsqrt(jnp.mean(oh * oh, axis=-1, keepdims=True) + RMS_EPS)
                on = oh * rms
                sig = _sigmoid(zh)
                sz = zh * sig
                dz_ref[rows, hs] = dyh * on * nwv * (sig * (1.0 + zh * (1.0 - sig)))
                dnw = jnp.sum(dyh * on * sz, axis=0, keepdims=True)
                don = dyh * nwv * sz
                do = rms * (don - on * jnp.mean(don * on, axis=-1, keepdims=True))
                dec, eg, kb, kbg, vb, qt, kt, kk, qk = _chunk_local(qh, kh, vh, gcol, grow, glast, bcol, ii, jj)
                stb, dsnb, dob, tb, kbgb = _bf(st), _bf(dsn), _bf(do), _bf(t), _bf(kbg)
                r = vb - _dot(kbgb, stb)
                dqt = _dot_nt(dob, stb)
                ds_new = _dot_tn(_bf(qt), dob)
                mm = jnp.where(ii > jj, kk * dec, 0.0)
                attn = qk * dec
                yield
                vn = _dot(t, r, HI3)
                dvn = _dot_tn(_bf(attn), dob) + _dot(_bf(kt), dsnb)
                egl = jnp.exp(glast)
                ekt = jnp.exp(glast - gcol)
                yield
                vnb, dvnb = _bf(vn), _bf(dvn)
                dattn = jnp.where(ii >= jj, _dot_nt(dob, vnb), 0.0)
                dkt = _dot_nt(vnb, dsnb)
                dvb = _dot_tn(tb, dvnb)
                dt = _dot_nt(dvnb, _bf(r))
                dglast = egl * jnp.sum(jnp.sum(dsn * st, axis=0, keepdims=True), axis=1, keepdims=True)
                yield
                dvbb = _bf(dvb)
                dv_ref[rows, hs] = dvb * bcol
                ds_scr[hh] = ds_new + egl * dsn - _dot_tn(kbgb, dvbb)
                dkbg = -_dot_nt(dvbb, stb)
                x = _dot_nt(_bf(dt), tb)
                yield
                dm = jnp.where(ii > jj, -_dot_tn(tb, _bf(x)), 0.0)
                dkk = dm * dec
                dqk = dattn * dec
                e = dm * mm + dattn * attn
                dgam = jnp.sum(e, axis=1, keepdims=True) - jnp.sum(e.T, axis=1, keepdims=True)
                dkkb, dqkb, kbf = _bf(dkk), _bf(dqk), _bf(kh)
                dkb = _dot(dkkb, kbf) + dkbg * eg
                dk_ref[rows, hs] = _dot_tn(dkkb, _bf(kb)) + _dot_tn(dqkb, _bf(qh)) + dkt * ekt + dkb * bcol
                dq_ref[rows, hs] = _dot(dqkb, kbf) + dqt * eg
                yield
                dktkt = dkt * kt
                dgam = dgam + jnp.sum(dqt * qt - dktkt + dkbg * kbg, axis=1, keepdims=True)
                dglast = dglast + jnp.sum(jnp.sum(dktkt, axis=0, keepdims=True), axis=1, keepdims=True)
                dgam = dgam + jnp.where(last_row, dglast, 0.0)
                dbeta = jnp.sum(dkb * kh + dvb * vh, axis=1, keepdims=True)
                return dgam, dbeta, dnw

            res = [None] * H
            for h0 in range(0, H, BWD_HEAD_GROUP):
                group = range(h0, min(h0 + BWD_HEAD_GROUP, H))
                gens = {hh: head(hh) for hh in group}
                while any(res[hh] is None for hh in group):
                    for hh in group:
                        try:
                            next(gens[hh])
                        except StopIteration as stop:
                            res[hh] = stop.value
            dgam_all = jnp.zeros((CH, LANE), f32)
            dbeta_all = jnp.zeros((CH, LANE), f32)
            dnw_acc = jnp.zeros((1, DH), f32)
            for hh in range(H):
                dgam, dbeta, dnw = res[hh]
                dgam_all = dgam_all + jnp.where(lane == hh, dgam, 0.0)
                dbeta_all = dbeta_all + jnp.where(lane == hh, dbeta, 0.0)
                dnw_acc = dnw_acc + dnw
            dg_ref[rows, :] = dgam_all
            db_ref[rows, :] = dbeta_all
            dnw_ref[0:1, :] += dnw_acc
            return carry

        lax.fori_loop(0, nc, chunk, 0)

    rev = lambda i: (nblk - 1 - i, 0)
    rev4 = lambda i: (nblk - 1 - i, 0, 0, 0)
    fix = lambda i: (0, 0)
    wide = pl.BlockSpec((rb, W), rev)
    thin = pl.BlockSpec((rb, LANE), rev)
    return pl.pallas_call(
        body, name="delta_bwd", grid=(nblk,),
        out_shape=(S((L, W), f32),) * 4 + (S((L, LANE), f32),) * 2 + (S((8, DH), f32),),
        in_specs=[wide, wide, wide, pl.BlockSpec((1, DH), fix), wide, wide, wide, thin, thin,
                  pl.BlockSpec((nc, H, DH, DH), rev4), pl.BlockSpec((nc, H, CH, CH), rev4)],
        out_specs=(wide,) * 4 + (thin, thin, pl.BlockSpec((8, DH), fix)),
        scratch_shapes=[pltpu.VMEM((H, DH, DH), f32)],
        compiler_params=_cp(dimension_semantics=("arbitrary",)))(dy, o, z, nw, q, k, v, g, beta, s_all, t_all)


def _sc_fwd(h, w_in, conv_w, w_out, g, b, *, first_row):
    L, D = h.shape
    W = w_out.shape[0]
    KW = conv_w.shape[0]
    tm = TM
    alpha = ALPHA

    def body(h_ref, win_hbm, cw_ref, wout_hbm, g_ref, b_ref, proj_ref, bu_ref, pre_ref, out_ref,
             win, wout, carry, sem):
        i = pl.program_id(0)
        _load_once([(win_hbm, win), (wout_hbm, wout)], sem)

        @pl.when(i == 0)
        def _():
            carry[...] = jnp.zeros_like(carry)

        hv = h_ref[...]
        hb = _bf(hv)
        bg = _dot(hb, win[:, 0:W])
        cg = _dot(hb, win[:, W:2 * W])
        xv = _dot(hb, win[:, 2 * W:3 * W])
        proj_ref[:, 0:W] = bg
        proj_ref[:, W:2 * W] = cg
        proj_ref[:, 2 * W:3 * W] = xv
        p = cg * xv
        u = _conv(cw_ref[...], _taps_back(carry[...], p, KW))
        carry[...] = p[tm - 8:tm, :]
        bu = _bf(bg * u)
        bu_ref[...] = bu
        pre = alpha * hv + _dot(bu, wout[...])
        pre_ref[...] = pre
        out_ref[...] = _ln_fwd(pre, g_ref[...], b_ref[...], _row_ids(i, tm, D), first_row)

    row = lambda i: (i, 0)
    fix = lambda i: (0, 0)
    return pl.pallas_call(
        body, name="sc_fwd", grid=(L // tm,),
        out_shape=(S((L, 3 * W), f32), S((L, W), bf16), S((L, D), f32), S((L, D), f32)),
        in_specs=[pl.BlockSpec((tm, D), row), ANY, pl.BlockSpec((KW, W), fix), ANY,
                  pl.BlockSpec((1, D), fix), pl.BlockSpec((1, D), fix)],
        out_specs=(pl.BlockSpec((tm, 3 * W), row), pl.BlockSpec((tm, W), row),
                   pl.BlockSpec((tm, D), row), pl.BlockSpec((tm, D), row)),
        scratch_shapes=[pltpu.VMEM((D, 3 * W), w_in.dtype), pltpu.VMEM((W, D), w_out.dtype),
                        pltpu.VMEM((8, W), f32), pltpu.SemaphoreType.DMA((2,))],
        compiler_params=_cp(dimension_semantics=("arbitrary",)))(h, w_in, conv_w, w_out, g, b)


def _sc_bwd(dpre, proj, conv_w, w_out, w_in, pre_in, g_in, *, first_row):
    L, D = dpre.shape
    W = w_out.shape[0]
    KW = conv_w.shape[0]
    tm = TM
    nb = L // tm
    alpha = ALPHA

    def body(dpre_ref, proj_ref, hc_ref, hx_ref, cw_ref, wout_hbm, win_hbm, pin_ref, g_ref,
             dproj_ref, dcw_ref, dpin_ref, dg_ref, db_ref, wout, win, carry, sem):
        i = pl.program_id(0)
        blk = nb - 1 - i
        _load_once([(wout_hbm, wout), (win_hbm, win)], sem)

        @pl.when(i == 0)
        def _():
            carry[...] = jnp.zeros_like(carry)
            dcw_ref[...] = jnp.zeros_like(dcw_ref)
            dg_ref[...] = jnp.zeros_like(dg_ref)
            db_ref[...] = jnp.zeros_like(db_ref)

        bg, cg, xv = proj_ref[:, 0:W], proj_ref[:, W:2 * W], proj_ref[:, 2 * W:3 * W]
        p = cg * xv
        u = _conv(cw_ref[...], _taps_back(jnp.where(blk > 0, hc_ref[...] * hx_ref[...], 0.0), p, KW))
        dpre_v = dpre_ref[...]
        d = _dot_nt(_bf(dpre_v), wout[...])
        dproj_ref[:, 0:W] = _bf(d * u)
        du = d * bg
        ahead = _taps_ahead(du, carry[...], KW)
        carry[...] = du[0:8, :]
        dp = _conv(cw_ref[...], ahead)
        for j in range(KW):
            dcw_ref[j:j + 1, :] += jnp.sum(ahead[j] * p, axis=0, keepdims=True)
        dproj_ref[:, W:2 * W] = _bf(dp * xv)
        dproj_ref[:, 2 * W:3 * W] = _bf(dp * cg)
        dh = alpha * dpre_v + _dot_nt(dproj_ref[...], win[...])
        dpin, dg, dbias = _ln_bwd_rows(dh, pin_ref[...], g_ref[...], _row_ids(blk, tm, D), first_row)
        dpin_ref[...] = dpin
        dg_ref[0:1, :] += dg
        db_ref[0:1, :] += dbias

    rev = lambda i: (nb - 1 - i, 0)
    fix = lambda i: (0, 0)

    def halo(col):
        return pl.BlockSpec((8, W), lambda i: (jnp.maximum((nb - 1 - i) * (tm // 8) - 1, 0), col))

    return pl.pallas_call(
        body, name="sc_bwd", grid=(nb,),
        out_shape=(S((L, 3 * W), bf16), S((8, W), f32), S((L, D), f32), S((8, D), f32), S((8, D), f32)),
        in_specs=[pl.BlockSpec((tm, D), rev), pl.BlockSpec((tm, 3 * W), rev), halo(1), halo(2),
                  pl.BlockSpec((KW, W), fix), ANY, ANY, pl.BlockSpec((tm, D), rev), pl.BlockSpec((1, D), fix)],
        out_specs=(pl.BlockSpec((tm, 3 * W), rev), pl.BlockSpec((8, W), fix), pl.BlockSpec((tm, D), rev),
                   pl.BlockSpec((8, D), fix), pl.BlockSpec((8, D), fix)),
        scratch_shapes=[pltpu.VMEM((W, D), w_out.dtype), pltpu.VMEM((D, 3 * W), w_in.dtype), pltpu.VMEM((8, W), f32),
                        pltpu.SemaphoreType.DMA((2,))],
        compiler_params=_cp(dimension_semantics=("arbitrary",)))(
            dpre, proj, proj, proj, conv_w, w_out, w_in, pre_in, g_in)


def _ffn_cols(F):
    fc = F
    for cand in (1408, 1024, 512, 256, 128):
        if F % cand == 0:
            fc = cand
            break
    return fc


def _ffn_fwd(h, w_up, conv_w, w_down, g, b, *, first_row, name):
    L, D = h.shape
    F = w_down.shape[0]
    KW = conv_w.shape[0]
    tm = TM
    fc = _ffn_cols(F)
    alpha = ALPHA

    def body(h_ref, wup_hbm, cw_ref, wdn_hbm, g_ref, b_ref, up_ref, a_ref, pre_ref, out_ref,
             wup, wdn, carry, sem):
        i = pl.program_id(0)
        _load_once([(wup_hbm, wup), (wdn_hbm, wdn)], sem)

        @pl.when(i == 0)
        def _():
            carry[...] = jnp.zeros_like(carry)

        hv = h_ref[...]
        hb = _bf(hv)
        pre = alpha * hv
        for c0 in range(0, F, fc):
            cs = slice(c0, c0 + fc)
            u = _dot(hb, wup[:, cs])
            gate = _dot(hb, wup[:, F + c0:F + c0 + fc])
            up_ref[:, cs] = u
            up_ref[:, F + c0:F + c0 + fc] = gate
            uc = _conv(cw_ref[:, cs], _taps_back(carry[:, cs], u, KW))
            carry[:, cs] = u[tm - 8:tm, :]
            ab = _bf(uc * _sigmoid(uc) * gate)
            a_ref[:, cs] = ab
            pre = pre + _dot(ab, wdn[cs, :])
        pre_ref[...] = pre
        out_ref[...] = _ln_fwd(pre, g_ref[...], b_ref[...], _row_ids(i, tm, D), first_row)

    row = lambda i: (i, 0)
    fix = lambda i: (0, 0)
    return pl.pallas_call(
        body, name=name, grid=(L // tm,),
        out_shape=(S((L, 2 * F), f32), S((L, F), bf16), S((L, D), f32), S((L, D), f32)),
        in_specs=[pl.BlockSpec((tm, D), row), ANY, pl.BlockSpec((KW, F), fix), ANY,
                  pl.BlockSpec((1, D), fix), pl.BlockSpec((1, D), fix)],
        out_specs=(pl.BlockSpec((tm, 2 * F), row), pl.BlockSpec((tm, F), row),
                   pl.BlockSpec((tm, D), row), pl.BlockSpec((tm, D), row)),
        scratch_shapes=[pltpu.VMEM((D, 2 * F), w_up.dtype), pltpu.VMEM((F, D), w_down.dtype),
                        pltpu.VMEM((8, F), f32), pltpu.SemaphoreType.DMA((2,))],
        compiler_params=_cp(dimension_semantics=("arbitrary",)))(h, w_up, conv_w, w_down, g, b)


def _ffn_bwd(dpre, up, w_down, conv_w, w_up, pre_in, g_in, *, first_row, name):
    L, D = dpre.shape
    F = w_down.shape[0]
    KW = conv_w.shape[0]
    tm = TM
    nb = L // tm
    fc = F
    alpha = ALPHA

    def body(dpre_ref, up_ref, halo_ref, wdn_hbm, cw_ref, wup_hbm, pin_ref, g_ref,
             dup_ref, dcw_ref, dpin_ref, dg_ref, db_ref, wdn, wup, carry, sem):
        i = pl.program_id(0)
        blk = nb - 1 - i
        _load_once([(wdn_hbm, wdn), (wup_hbm, wup)], sem)

        @pl.when(i == 0)
        def _():
            carry[...] = jnp.zeros_like(carry)
            dcw_ref[...] = jnp.zeros_like(dcw_ref)
            dg_ref[...] = jnp.zeros_like(dg_ref)
            db_ref[...] = jnp.zeros_like(db_ref)

        dpre_v = dpre_ref[...]
        db = _bf(dpre_v)
        dh = alpha * dpre_v
        for c0 in range(0, F, fc):
            cs = slice(c0, c0 + fc)
            gs_ = slice(F + c0, F + c0 + fc)
            da = _dot_nt(db, wdn[cs, :])
            gate = up_ref[:, gs_]
            u = up_ref[:, cs]
            uc = _conv(cw_ref[:, cs], _taps_back(jnp.where(blk > 0, halo_ref[:, cs], 0.0), u, KW))
            sig = _sigmoid(uc)
            dgate = _bf(da * (uc * sig))
            dup_ref[:, gs_] = dgate
            duc = da * gate * (sig * (1.0 + uc * (1.0 - sig)))
            ahead = _taps_ahead(duc, carry[:, cs], KW)
            carry[:, cs] = duc[0:8, :]
            du = _bf(_conv(cw_ref[:, cs], ahead))
            dup_ref[:, cs] = du
            for j in range(KW):
                dcw_ref[j:j + 1, cs] += jnp.sum(ahead[j] * u, axis=0, keepdims=True)
            dh = dh + _dot_nt(du, wup[:, cs]) + _dot_nt(dgate, wup[:, gs_])
        dpin, dg, dbias = _ln_bwd_rows(dh, pin_ref[...], g_ref[...], _row_ids(blk, tm, D), first_row)
        dpin_ref[...] = dpin
        dg_ref[0:1, :] += dg
        db_ref[0:1, :] += dbias

    rev = lambda i: (nb - 1 - i, 0)
    fix = lambda i: (0, 0)
    return pl.pallas_call(
        body, name=name, grid=(nb,),
        out_shape=(S((L, 2 * F), bf16), S((8, F), f32), S((L, D), f32), S((8, D), f32), S((8, D), f32)),
        in_specs=[pl.BlockSpec((tm, D), rev), pl.BlockSpec((tm, 2 * F), rev),
                  pl.BlockSpec((8, F), lambda i: (jnp.maximum((nb - 1 - i) * (tm // 8) - 1, 0), 0)),
                  ANY, pl.BlockSpec((KW, F), fix), ANY, pl.BlockSpec((tm, D), rev), pl.BlockSpec((1, D), fix)],
        out_specs=(pl.BlockSpec((tm, 2 * F), rev), pl.BlockSpec((8, F), fix), pl.BlockSpec((tm, D), rev),
                   pl.BlockSpec((8, D), fix), pl.BlockSpec((8, D), fix)),
        scratch_shapes=[pltpu.VMEM((F, D), w_down.dtype), pltpu.VMEM((D, 2 * F), w_up.dtype), pltpu.VMEM((8, F), f32),
                        pltpu.SemaphoreType.DMA((2,))],
        compiler_params=_cp(dimension_semantics=("arbitrary",)))(dpre, up, up, w_down, conv_w, w_up, pre_in, g_in)


def _loss_head(h, target, pre, g, *, first_row):
    L, D = h.shape
    tm = TM
    pb = PADF // tm

    def body(h_ref, t_ref, pre_ref, g_ref, dpre_ref, dg_ref, db_ref, loss_ref):
        i = pl.program_id(0)

        @pl.when(i == 0)
        def _():
            loss_ref[...] = jnp.zeros_like(loss_ref)
            dg_ref[...] = jnp.zeros_like(dg_ref)
            db_ref[...] = jnp.zeros_like(db_ref)

        valid = i >= pb
        err = h_ref[...] - t_ref[...]
        dh = jnp.where(valid, err * (1.0 / D), 0.0)
        part = 0.5 * jnp.sum(jnp.sum(err * err, axis=-1, keepdims=True) * (1.0 / D), axis=0, keepdims=True)
        loss_ref[...] += jnp.where(valid, part, 0.0)
        dpre, dg, db = _ln_bwd_rows(dh, pre_ref[...], g_ref[...], _row_ids(i, tm, D), first_row)
        dpre_ref[...] = dpre
        dg_ref[0:1, :] += dg
        db_ref[0:1, :] += db

    row = lambda i: (i, 0)
    fix = lambda i: (0, 0)
    return pl.pallas_call(
        body, name="loss_head", grid=(L // tm,),
        out_shape=(S((L, D), f32), S((8, D), f32), S((8, D), f32), S((8, LANE), f32)),
        in_specs=[pl.BlockSpec((tm, D), row), pl.BlockSpec((tm, D), lambda i: (jnp.maximum(i - pb, 0), 0)),
                  pl.BlockSpec((tm, D), row), pl.BlockSpec((1, D), fix)],
        out_specs=(pl.BlockSpec((tm, D), row), pl.BlockSpec((8, D), fix), pl.BlockSpec((8, D), fix),
                   pl.BlockSpec((8, LANE), fix)),
        compiler_params=_cp(dimension_semantics=("arbitrary",)))(h, target, pre, g)


def _adamw(g_terms, w, m, v, *, name):
    R, C = w.shape
    tr = _row_tile(R)
    n = len(g_terms)
    c1 = 1.0 - ADAM_B1 ** ADAM_STEP
    c2 = 1.0 - ADAM_B2 ** ADAM_STEP

    def body(*refs):
        g = refs[0][...].astype(f32)
        for r in refs[1:n]:
            g = g + r[...].astype(f32)
        w_ref, m_ref, v_ref, g_out, d_out, m_out, v_out = refs[n:]
        mn = ADAM_B1 * m_ref[...] + (1.0 - ADAM_B1) * g
        vn = ADAM_B2 * v_ref[...] + (1.0 - ADAM_B2) * (g * g)
        g_out[...] = g
        m_out[...] = mn
        v_out[...] = vn
        d_out[...] = -ADAM_LR * ((mn / c1) / (jnp.sqrt(vn / c2) + ADAM_EPS) + ADAM_WD * w_ref[...])

    spec = pl.BlockSpec((tr, C), lambda i: (i, 0))
    return pl.pallas_call(
        body, name=name, grid=(R // tr,), out_shape=(S((R, C), f32),) * 4,
        in_specs=[spec] * (n + 3), out_specs=(spec,) * 4,
        compiler_params=_cp(dimension_semantics=("arbitrary",)))(*g_terms, w, m, v)


def _sum_devices(x):
    n, R, C = x.shape

    def body(x_ref, o_ref):
        acc = x_ref[0]
        for d in range(1, n):
            acc = acc + x_ref[d]
        o_ref[...] = acc

    return pl.pallas_call(body, name="sum_devices", out_shape=S((R, C), f32), compiler_params=_cp())(x)


def _row_tile(R):
    for step in (16, 8):
        for t in range(256, 0, -step):
            if R % t == 0:
                return t
    return R


def _pair_add(s32, recv1, sidx, ridx, *, name):
    _, L, K, n = s32.shape
    nj = sidx.shape[0]
    tk = _row_tile(K)

    def body(sidx_ref, ridx_ref, a_ref, b_ref, o_ref):
        o_ref[...] = _bf(a_ref[...] + b_ref[...].astype(f32))

    blk = (1, 1, tk, n)
    grid_spec = pltpu.PrefetchScalarGridSpec(
        num_scalar_prefetch=2, grid=(nj, L, K // tk),
        in_specs=[pl.BlockSpec(blk, lambda j, l, i, si, ri: (si[j], l, i, 0)),
                  pl.BlockSpec(blk, lambda j, l, i, si, ri: (ri[j], l, i, 0))],
        out_specs=pl.BlockSpec(blk, lambda j, l, i, si, ri: (j, l, i, 0)))
    return pl.pallas_call(
        body, name=name, grid_spec=grid_spec, out_shape=S((nj, L, K, n), bf16),
        compiler_params=_cp(dimension_semantics=("arbitrary",) * 3))(sidx, ridx, s32, recv1)


def _adamw_shard(s32, recv1, recv2, w, m, v, idx, *, name):
    L, K, n = w.shape
    tk = _row_tile(K)
    c1 = 1.0 - ADAM_B1 ** ADAM_STEP
    c2 = 1.0 - ADAM_B2 ** ADAM_STEP

    def body(idx_ref, a_ref, b_ref, r_ref, w_ref, m_ref, v_ref, g_out, d_out, m_out, v_out):
        g = a_ref[0] + b_ref[0].astype(f32)
        for j in range(3):
            g = g + r_ref[j].astype(f32)
        mn = ADAM_B1 * m_ref[...] + (1.0 - ADAM_B1) * g
        vn = ADAM_B2 * v_ref[...] + (1.0 - ADAM_B2) * (g * g)
        g_out[...] = g
        m_out[...] = mn
        v_out[...] = vn
        d_out[...] = -ADAM_LR * ((mn / c1) / (jnp.sqrt(vn / c2) + ADAM_EPS) + ADAM_WD * w_ref[...])

    own = pl.BlockSpec((1, tk, n), lambda l, i, ix: (l, i, 0))
    grid_spec = pltpu.PrefetchScalarGridSpec(
        num_scalar_prefetch=1, grid=(L, K // tk),
        in_specs=[pl.BlockSpec((1, 1, tk, n), lambda l, i, ix: (ix[0], l, i, 0)),
                  pl.BlockSpec((1, 1, tk, n), lambda l, i, ix: (ix[1], l, i, 0)),
                  pl.BlockSpec((3, 1, tk, n), lambda l, i, ix: (0, l, i, 0)), own, own, own],
        out_specs=(own,) * 4)
    return pl.pallas_call(
        body, name=name, grid_spec=grid_spec, out_shape=(S((L, K, n), f32),) * 4,
        compiler_params=_cp(dimension_semantics=("arbitrary", "arbitrary")))(idx, s32, recv1, recv2, w, m, v)


def _adamw_direct(s32s, recvs, w, m, v, me, *, name):
    L, K, n = w.shape
    tk = _row_tile(K)
    c1 = 1.0 - ADAM_B1 ** ADAM_STEP
    c2 = 1.0 - ADAM_B2 ** ADAM_STEP

    def body(me_ref, *refs):
        own_refs, recv_refs = refs[:L], refs[L:2 * L]
        w_ref, m_ref, v_ref, g_out, d_out, m_out, v_out = refs[2 * L:]
        for li in range(L):
            @pl.when(pl.program_id(0) == li)
            def _(li=li):
                g = own_refs[li][0, 0]
                for d in range(N_DEV):
                    g = g + recv_refs[li][d, 0].astype(f32)
                mn = ADAM_B1 * m_ref[0] + (1.0 - ADAM_B1) * g
                vn = ADAM_B2 * v_ref[0] + (1.0 - ADAM_B2) * (g * g)
                g_out[0] = g
                m_out[0] = mn
                v_out[0] = vn
                d_out[0] = -ADAM_LR * ((mn / c1) / (jnp.sqrt(vn / c2) + ADAM_EPS) + ADAM_WD * w_ref[0])

    own = pl.BlockSpec((1, tk, n), lambda l, i, ix: (l, i, 0))
    grid_spec = pltpu.PrefetchScalarGridSpec(
        num_scalar_prefetch=1, grid=(L, K // tk),
        in_specs=[pl.BlockSpec((1, 1, tk, n), lambda l, i, ix: (ix[0], 0, i, 0))] * L
        + [pl.BlockSpec((N_DEV, 1, tk, n), lambda l, i, ix: (0, 0, i, 0))] * L + [own, own, own],
        out_specs=(own,) * 4)
    return pl.pallas_call(
        body, name=name, grid_spec=grid_spec, out_shape=(S((L, K, n), f32),) * 4,
        compiler_params=_cp(dimension_semantics=("arbitrary", "arbitrary")))(me, *s32s, *recvs, w, m, v)


def _col_segments(n, mapping):
    segs = []
    for p in range(N_DEV):
        lo, hi = p * n, (p + 1) * n
        out = []
        for c0, c1, e0 in mapping:
            a, b = max(lo, c0), min(hi, c1)
            if a < b:
                out.append((a - lo, e0 + (a - c0), b - a))
        segs.append(out)
    return segs


def _assemble_cols(gathered, mapping, n_out, *, name):
    _, L, K, n = gathered.shape
    tk = _row_tile(K)
    segs = _col_segments(n, mapping)
    covered = sum(w for s in segs for (_, _, w) in s)

    def body(g_ref, o_ref):
        if covered != n_out:
            o_ref[...] = jnp.zeros_like(o_ref)
        for p in range(N_DEV):
            for s0, d0, w in segs[p]:
                o_ref[0, :, d0:d0 + w] = g_ref[p, 0, :, s0:s0 + w]

    return pl.pallas_call(
        body, name=name, grid=(L, K // tk), out_shape=S((L, K, n_out), gathered.dtype),
        in_specs=[pl.BlockSpec((N_DEV, 1, tk, n), lambda l, i: (0, l, i, 0))],
        out_specs=pl.BlockSpec((1, tk, n_out), lambda l, i: (l, i, 0)),
        compiler_params=_cp(dimension_semantics=("arbitrary", "arbitrary")))(gathered)


def _split_cols(dws, mapping, n, *, name):
    L = len(dws)
    K, n_in = dws[0].shape
    tk = _row_tile(K)
    segs = _col_segments(n, mapping)

    def body(*refs):
        ins, o32, o16 = refs[:L], refs[L], refs[L + 1]
        for li in range(L):
            @pl.when(pl.program_id(0) == li)
            def _(li=li):
                for p in range(N_DEV):
                    for s0, d0, w in segs[p]:
                        val = ins[li][:, d0:d0 + w]
                        o32[p, 0, :, s0:s0 + w] = val
                        o16[p, 0, :, s0:s0 + w] = _bf(val)

    out = pl.BlockSpec((N_DEV, 1, tk, n), lambda l, i: (0, l, i, 0))
    return pl.pallas_call(
        body, name=name, grid=(L, K // tk), out_shape=(S((N_DEV, L, K, n), f32), S((N_DEV, L, K, n), bf16)),
        in_specs=[pl.BlockSpec((tk, n_in), lambda l, i: (i, 0))] * L, out_specs=(out, out),
        compiler_params=_cp(dimension_semantics=("arbitrary", "arbitrary")))(*dws)


def _split_rows(dws, k, *, name):
    L = len(dws)
    N = dws[0].shape[1]

    def body(*refs):
        ins, o32, o16 = refs[:L], refs[L], refs[L + 1]
        for li in range(L):
            @pl.when(pl.program_id(0) == li)
            def _(li=li):
                val = ins[li][...]
                o32[0, 0] = val
                o16[0, 0] = _bf(val)

    out = pl.BlockSpec((1, 1, k, N), lambda l, p: (p, l, 0, 0))
    return pl.pallas_call(
        body, name=name, grid=(L, N_DEV), out_shape=(S((N_DEV, L, k, N), f32), S((N_DEV, L, k, N), bf16)),
        in_specs=[pl.BlockSpec((k, N), lambda l, p: (p, 0))] * L, out_specs=(out, out),
        compiler_params=_cp(dimension_semantics=("arbitrary", "arbitrary")))(*dws)


def _rows_full(gathered):
    _, L, k, N = gathered.shape
    return jnp.transpose(gathered, (1, 0, 2, 3)).reshape(L, N_DEV * k, N)


def _peer(rel):
    x, y, c = lax.axis_index("x"), lax.axis_index("y"), lax.axis_index("c")
    return {"c": (x, y, 1 - c), "x": (1 - x, y, c), "y": (x, 1 - y, c), "xy": (1 - x, 1 - y, c)}[rel]


def _all_gather(xs, *, name):
    na = len(xs)

    def body(*refs):
        x_refs, out_refs = refs[:na], refs[na:2 * na]
        send_sems, recv_sems, local_sems = refs[2 * na:]
        mx, my, mc = lax.axis_index("x"), lax.axis_index("y"), lax.axis_index("c")
        me, sibling = (mx, my, mc), (mx, my, 1 - mc)
        chips = [(1 - mx, my), (mx, 1 - my), (1 - mx, 1 - my)]

        def slot(a, px, py, pc):
            return out_refs[a].at[4 * px + 2 * py + pc]

        def copy(a, kk, block, to, src=None):
            return pltpu.make_async_remote_copy(
                src_ref=slot(a, *block) if src is None else src, dst_ref=slot(a, *block),
                send_sem=send_sems.at[7 * a + kk], recv_sem=recv_sems.at[7 * a + kk], device_id=to, device_id_type=MESH)

        mine = [pltpu.make_async_copy(x_refs[a], slot(a, *me), local_sems.at[a]) for a in range(na)]
        for cp in mine:
            cp.start()
        first = []
        for a in range(na):
            first.append(copy(a, 0, me, sibling, src=x_refs[a]))
            first += [copy(a, 1 + j, me, (*chip, mc), src=x_refs[a]) for j, chip in enumerate(chips)]
        for cp in first:
            cp.start()
        passed = []
        for j, chip in enumerate(chips):
            for a in range(na):
                copy(a, 1 + j, (*chip, mc), me).wait_recv()
                fwd = copy(a, 4 + j, (*chip, mc), sibling)
                fwd.start()
                passed.append(fwd)
        for a in range(na):
            copy(a, 0, sibling, me).wait_recv()
            for j, chip in enumerate(chips):
                copy(a, 4 + j, (*chip, 1 - mc), me).wait_recv()
        for cp in first + passed:
            cp.wait_send()
        for cp in mine:
            cp.wait()

    return pl.pallas_call(
        body, name=name, out_shape=tuple(S((N_DEV,) + x.shape, x.dtype) for x in xs),
        in_specs=[ANY] * na, out_specs=(ANY,) * na,
        scratch_shapes=[pltpu.SemaphoreType.DMA((7 * na,)), pltpu.SemaphoreType.DMA((7 * na,)),
                        pltpu.SemaphoreType.DMA((na,))],
        compiler_params=pltpu.CompilerParams(has_side_effects=True))(*xs)


def _exchange(sends, n_slots, src_index, rels, *, name):
    na = len(sends)

    def body(*refs):
        send_refs, recv_refs = refs[:na], refs[na:2 * na]
        send_sems, recv_sems = refs[2 * na:]
        cps = [pltpu.make_async_remote_copy(
            src_ref=send_refs[a].at[src_index(j)], dst_ref=recv_refs[a].at[j],
            send_sem=send_sems.at[n_slots * a + j], recv_sem=recv_sems.at[n_slots * a + j],
            device_id=_peer(rels[j]), device_id_type=MESH) for a in range(na) for j in range(n_slots)]
        for cp in cps:
            cp.start()
        for cp in cps:
            cp.wait()

    return pl.pallas_call(
        body, name=name, out_shape=tuple(S((n_slots,) + s.shape[1:], s.dtype) for s in sends),
        in_specs=[ANY] * na, out_specs=(ANY,) * na,
        scratch_shapes=[pltpu.SemaphoreType.DMA((n_slots * na,)), pltpu.SemaphoreType.DMA((n_slots * na,))],
        compiler_params=pltpu.CompilerParams(has_side_effects=True))(*sends)


_FLIPS = [(fx, fy, fc) for fx in (0, 1) for fy in (0, 1) for fc in (0, 1)][1:]


def _flip_peer(flip):
    x, y, c = lax.axis_index("x"), lax.axis_index("y"), lax.axis_index("c")
    return tuple(1 - a if f else a for a, f in zip((x, y, c), flip))


def _dev_index(p):
    return 4 * p[0] + 2 * p[1] + p[2]


HBM_SPEC = pl.BlockSpec(memory_space=pltpu.HBM)
SEM_SPEC = pl.BlockSpec(memory_space=pltpu.SEMAPHORE)


def _direct_start(srcs, lands, per_peer, *, name):
    na = len(srcs)

    def body(*refs):
        src_refs, land_refs = refs[:na], refs[na:2 * na]
        send_sems, recv_sems = refs[2 * na], refs[2 * na + 1]
        token = refs[-1]
        me = _dev_index((lax.axis_index("x"), lax.axis_index("y"), lax.axis_index("c")))
        for a in range(na):
            for r, flip in enumerate(_FLIPS):
                peer = _flip_peer(flip)
                src = src_refs[a].at[_dev_index(peer)] if per_peer else src_refs[a]
                pltpu.make_async_remote_copy(
                    src_ref=src, dst_ref=land_refs[a].at[me], send_sem=send_sems.at[7 * a + r],
                    recv_sem=recv_sems.at[7 * a + r], device_id=peer, device_id_type=MESH).start()
        token[...] = jnp.zeros_like(token)

    hbm = lambda t: pltpu.with_memory_space_constraint(t, pltpu.HBM)
    out = pl.pallas_call(
        body, name=name,
        out_shape=(pltpu.SemaphoreType.DMA((7 * na,)), pltpu.SemaphoreType.DMA((7 * na,)))
        + tuple(pltpu.HBM(t.shape, t.dtype) for t in list(srcs) + list(lands)) + (S((8, LANE), f32),),
        in_specs=[HBM_SPEC] * (2 * na),
        out_specs=(SEM_SPEC, SEM_SPEC) + (HBM_SPEC,) * (2 * na) + (pl.BlockSpec(memory_space=pltpu.VMEM),),
        input_output_aliases={i: 2 + i for i in range(2 * na)},
        compiler_params=pltpu.CompilerParams(has_side_effects=pltpu.SideEffectType.DATAFLOW_SIDE_EFFECTING))(
            *[hbm(t) for t in srcs], *[hbm(t) for t in lands])
    return out[0], out[1], list(out[2:2 + na]), list(out[2 + na:2 + 2 * na]), out[-1]


def _direct_wait(send_sems, recv_sems, srcs, lands, per_peer, after, *, name):
    na = len(srcs)

    def body(*refs):
        src_refs, land_refs = refs[:na], refs[na:2 * na]
        ssem, rsem = refs[2 * na], refs[2 * na + 1]
        me = _dev_index((lax.axis_index("x"), lax.axis_index("y"), lax.axis_index("c")))
        for a in range(na):
            for r, flip in enumerate(_FLIPS):
                peer = _flip_peer(flip)
                src = src_refs[a].at[_dev_index(peer)] if per_peer else src_refs[a]
                cp = pltpu.make_async_remote_copy(
                    src_ref=src, dst_ref=land_refs[a].at[me], send_sem=ssem.at[7 * a + r],
                    recv_sem=rsem.at[7 * a + r], device_id=peer, device_id_type=MESH)
                cp.wait_send()
                cp.wait_recv()

    out = pl.pallas_call(
        body, name=name, out_shape=tuple(pltpu.HBM(t.shape, t.dtype) for t in list(srcs) + list(lands)),
        in_specs=[HBM_SPEC] * (2 * na) + [SEM_SPEC, SEM_SPEC, ANY], out_specs=(HBM_SPEC,) * (2 * na),
        input_output_aliases={i: i for i in range(2 * na)},
        compiler_params=pltpu.CompilerParams(has_side_effects=pltpu.SideEffectType.DATAFLOW_SIDE_EFFECTING))(
            *srcs, *lands, send_sems, recv_sems, after)
    return list(out[:na]), list(out[na:])


def _pack_small(parts, width):
    rows, offs, r = [], [], 0
    for a in parts:
        n = a.size
        nr = -(-n // width)
        flat = a.reshape(-1).astype(f32)
        if nr * width != n:
            flat = jnp.pad(flat, (0, nr * width - n))
        rows.append(flat.reshape(nr, width))
        offs.append((r, nr))
        r += nr
    buf = jnp.concatenate(rows, axis=0)
    pad = (-r) % 8
    if pad:
        buf = jnp.pad(buf, ((0, pad), (0, 0)))
    return buf, offs


def _unpack_small(buf, off, shape):
    r, nr = off
    return buf[r:r + nr].reshape(-1)[:math.prod(shape)].reshape(shape)


def _local_step(x, target, meta, a_w_in, small, start_token, late_weights, grads_ready):
    SEQ, D = x.shape
    n_meta = meta.shape[0]
    first_row = PADF - n_meta
    H = small["a_log"].shape[-1]

    head = jnp.concatenate([jnp.zeros((first_row, D), f32), meta], axis=0)

    def lanes(a):
        return jnp.pad(a.reshape(1, -1), ((0, 0), (0, LANE - a.size)))

    def after_token(a, token):
        return a if token is None else a + token[0:1, 0:1]

    alog, dtb = after_token(lanes(small["a_log"][0]), start_token), lanes(small["a_dt_bias"][0])
    a_conv, b_conv = small["a_conv"][0], small["b_conv"][0]
    nw = small["a_norm"][0].reshape(1, DH)
    lmg, lmb, lfg, lfb = small["ln_mix_g"], small["ln_mix_b"], small["ln_ffn_g"], small["ln_ffn_b"]

    h0, pre_a, z, raw, q, k, v, beta, g = _gdn_in_fwd(x, head, a_w_in, a_conv, alog, dtb, first_row=first_row, H=H)
    o, y, s_all, t_all = _delta_fwd(q, k, v, g, beta, z, nw, H=H)
    wts = late_weights(y)
    pre1, h1 = _out_res_ln(y, wts["a_w_out"], h0, lmg[0:1], lmb[0:1], first_row=first_row, name="gdn_out_ln")
    up0, act0, pre2, h2 = _ffn_fwd(h1, wts["ffn_w_up"][0], small["ffn_conv"][0], wts["ffn_w_down"][0],
                                   lfg[0:1], lfb[0:1], first_row=first_row, name="ffn_fwd0")
    proj_b, bu, pre3, h3 = _sc_fwd(h2, wts["b_w_in"], b_conv, wts["b_w_out"], lmg[1:2], lmb[1:2], first_row=first_row)
    up1, act1, pre4, h4 = _ffn_fwd(h3, wts["ffn_w_up"][1], small["ffn_conv"][1], wts["ffn_w_down"][1],
                                   lfg[1:2], lfb[1:2], first_row=first_row, name="ffn_fwd1")
    gw, gs = {}, {}
    alpha = ALPHA
    dpre4, dlfg1, dlfb1, loss_tile = _loss_head(h4, target, pre4, lfg[1:2], first_row=first_row)

    def ffn_backward(dpre, up, act, h_in, layer, tag, ln_in, token=None):
        dup, dcw, dpre_in, dg, db = _ffn_bwd(
            dpre, up, wts["ffn_w_down"][layer], after_token(small["ffn_conv"][layer], token),
            wts["ffn_w_up"][layer], ln_in[0], ln_in[1], first_row=first_row, name="ffn_bwd" + tag)
        dwd = _linear_dw(act, dpre, name="dw_down" + tag)
        dwu = _linear_dw(h_in, dup, name="dw_up" + tag)
        return dpre_in, dg, db, dwu, dwd, dcw[0:3]

    dpre3, dlmg1, dlmb1, dwu1, dwd1, dcf1 = ffn_backward(dpre4, up1, act1, h3, 1, "1", (pre3, lmg[1:2]))

    dwb_out = _linear_dw(bu, dpre3, name="dw_b_out")
    dproj_b, dcb, dpre2, dlfg0, dlfb0 = _sc_bwd(dpre3, proj_b, b_conv, wts["b_w_out"], wts["b_w_in"], pre2, lfg[0:1],
                                                first_row=first_row)
    dwb_in = _linear_dw(h2, dproj_b, name="dw_b_in")
    token = grads_ready("layer1", dict(ffn_w_up=dwu1, ffn_w_down=dwd1, b_w_in=dwb_in, b_w_out=dwb_out))

    dpre1, dlmg0, dlmb0, dwu0, dwd0, dcf0 = ffn_backward(dpre2, up0, act0, h1, 0, "0", (pre1, lmg[0:1]), token)
    token = grads_ready("layer0", dict(ffn_w_up=dwu0, ffn_w_down=dwd0))

    dy = _linear_dx(dpre1, wts["a_w_out"], name="dx_a_out")
    gw["a_w_out"] = [_linear_dw(y, dpre1, name="dw_a_out")]
    dq, dk, dv, dz, dg_, dbeta, dnw = _delta_bwd(dy, o, z, after_token(nw, token), q, k, v, g, beta, s_all, t_all, H=H)
    dproj_a, dca, dal, ddt, grad_x, dhead = _gdn_in_bwd(dq, dk, dv, dz, dg_, dbeta, pre_a, raw, a_conv, alog, dtb,
                                                        a_w_in, dpre1, first_row=first_row, H=H)
    gw["a_w_in"] = [_linear_dw(h0, dproj_a, name="dw_a_in")]

    gs["meta"] = dhead[first_row:PADF]
    gs["a_conv"] = dca[0:a_conv.shape[0]][None]
    gs["a_log"] = dal[0:1, 0:H]
    gs["a_dt_bias"] = ddt[0:1, 0:H]
    gs["a_norm"] = dnw[0:1]
    gs["b_conv"] = dcb[0:b_conv.shape[0]][None]
    gs["ln_mix_g"] = jnp.stack([dlmg0[0], dlmg1[0]])
    gs["ln_mix_b"] = jnp.stack([dlmb0[0], dlmb1[0]])
    gs["ffn_conv"] = jnp.stack([dcf0, dcf1])
    gs["ln_ffn_g"] = jnp.stack([dlfg0[0], dlfg1[0]])
    gs["ln_ffn_b"] = jnp.stack([dlfb0[0], dlfb1[0]])
    return loss_tile, grad_x, gw, gs


_BIG = ("a_w_in", "a_w_out", "b_w_in", "b_w_out", "ffn_w_up", "ffn_w_down")
_BIG_COL = ("a_w_in", "b_w_in", "ffn_w_up")
_SMALL = ("meta", "a_conv", "a_log", "a_dt_bias", "a_norm", "b_conv", "ln_mix_g", "ln_mix_b",
          "ffn_conv", "ln_ffn_g", "ln_ffn_b")
_SMALL_SHARDED = ("meta", "a_conv", "b_conv", "ffn_conv")
_ORDER = ("meta", "a_w_in", "a_conv", "a_log", "a_dt_bias", "a_norm", "a_w_out", "b_w_in", "b_conv", "b_w_out",
          "ln_mix_g", "ln_mix_b", "ffn_w_up", "ffn_conv", "ffn_w_down", "ln_ffn_g", "ln_ffn_b")


def _a_w_in_map(H):
    W4 = 4 * H * DH
    return [(0, W4, 0), (W4, W4 + H, W4), (W4 + H, W4 + 2 * H, W4 + LANE)], W4 + 2 * LANE


def kernel(x, meta, a_w_in, a_conv, a_log, a_dt_bias, a_norm, a_w_out, b_w_in, b_conv, b_w_out, ln_mix_g, ln_mix_b, ffn_w_up, ffn_conv, ffn_w_down, ln_ffn_g, ln_ffn_b, loss_target, m_meta, m_a_w_in, m_a_conv, m_a_log, m_a_dt_bias, m_a_norm, m_a_w_out, m_b_w_in, m_b_conv, m_b_w_out, m_ln_mix_g, m_ln_mix_b, m_ffn_w_up, m_ffn_conv, m_ffn_w_down, m_ln_ffn_g, m_ln_ffn_b, v_meta, v_a_w_in, v_a_conv, v_a_log, v_a_dt_bias, v_a_norm, v_a_w_out, v_b_w_in, v_b_conv, v_b_w_out, v_ln_mix_g, v_ln_mix_b, v_ffn_w_up, v_ffn_conv, v_ffn_w_down, v_ln_ffn_g, v_ln_ffn_b):
    wloc = dict(meta=meta, a_w_in=a_w_in, a_conv=a_conv, a_log=a_log, a_dt_bias=a_dt_bias, a_norm=a_norm,
                a_w_out=a_w_out, b_w_in=b_w_in, b_conv=b_conv, b_w_out=b_w_out, ln_mix_g=ln_mix_g, ln_mix_b=ln_mix_b,
                ffn_w_up=ffn_w_up, ffn_conv=ffn_conv, ffn_w_down=ffn_w_down, ln_ffn_g=ln_ffn_g, ln_ffn_b=ln_ffn_b)
    mloc = dict(meta=m_meta, a_w_in=m_a_w_in, a_conv=m_a_conv, a_log=m_a_log, a_dt_bias=m_a_dt_bias, a_norm=m_a_norm,
                a_w_out=m_a_w_out, b_w_in=m_b_w_in, b_conv=m_b_conv, b_w_out=m_b_w_out, ln_mix_g=m_ln_mix_g,
                ln_mix_b=m_ln_mix_b, ffn_w_up=m_ffn_w_up, ffn_conv=m_ffn_conv, ffn_w_down=m_ffn_w_down,
                ln_ffn_g=m_ln_ffn_g, ln_ffn_b=m_ln_ffn_b)
    vloc = dict(meta=v_meta, a_w_in=v_a_w_in, a_conv=v_a_conv, a_log=v_a_log, a_dt_bias=v_a_dt_bias, a_norm=v_a_norm,
                a_w_out=v_a_w_out, b_w_in=v_b_w_in, b_conv=v_b_conv, b_w_out=v_b_w_out, ln_mix_g=v_ln_mix_g,
                ln_mix_b=v_ln_mix_b, ffn_w_up=v_ffn_w_up, ffn_conv=v_ffn_conv, ffn_w_down=v_ffn_w_down,
                ln_ffn_g=v_ln_ffn_g, ln_ffn_b=v_ln_ffn_b)
    H = a_log.shape[-1]
    mx, my, mc = lax.axis_index("x"), lax.axis_index("y"), lax.axis_index("c")
    me = 4 * mx + 2 * my + mc

    a_map, a_cols = _a_w_in_map(H)
    col_maps = {"a_w_in": (a_map, a_cols)}
    for n in ("b_w_in", "ffn_w_up"):
        ncols = N_DEV * wloc[n].shape[-1]
        col_maps[n] = ([(0, ncols, 0)], ncols)
    sm_sh = [wloc[n] for n in _SMALL_SHARDED]
    sbuf, soffs = _pack_small(sm_sh, 128)
    g_a_w_in, sg = _all_gather([_bf(wloc["a_w_in"]), sbuf], name="gather_first")
    w_a_in = _assemble_cols(g_a_w_in, *col_maps["a_w_in"], name="assemble_a_w_in")[0]
    late = [n for n in _BIG if n != "a_w_in"]
    ssem, rsem, srcs_t, lands_t, start_token = _direct_start(
        [_bf(wloc[n]) for n in late], [lax.empty((N_DEV,) + wloc[n].shape, bf16) for n in late], False,
        name="gather_rest_start")

    def late_weights(after):
        srcs_d, landed = _direct_wait(ssem, rsem, srcs_t, lands_t, False, after, name="gather_rest_wait")
        wts = {}
        for n, own, got in zip(late, srcs_d, landed):
            full = lax.dynamic_update_index_in_dim(got, own, me, 0)
            if n in _BIG_COL:
                wts[n] = _assemble_cols(full, *col_maps[n], name="assemble_" + n)
            else:
                wts[n] = _rows_full(full)
        for n in ("a_w_out", "b_w_in", "b_w_out"):
            wts[n] = wts[n][0]
        return wts

    small = {n: wloc[n] for n in _SMALL}
    for n, off in zip(_SMALL_SHARDED, soffs):
        sh = wloc[n].shape
        parts = jnp.stack([_unpack_small(sg[d], off, sh) for d in range(N_DEV)])
        nd = len(sh)
        small[n] = jnp.transpose(parts, tuple(range(1, nd)) + (0, nd)).reshape(sh[:-1] + (N_DEV * sh[-1],))

    def split(n, dws, tag):
        if n in _BIG_COL:
            return _split_cols(dws, col_maps[n][0], wloc[n].shape[-1], name="split_" + n + tag)
        return _split_rows(dws, wloc[n].shape[-2], name="split_" + n + tag)

    sent = {}

    def grads_ready(stage, grads):
        names = sorted(grads)
        parts = [split(n, [grads[n]], "_" + stage) for n in names]
        handles = _direct_start([p[1] for p in parts], [jnp.zeros(p[1].shape, bf16) for p in parts], True,
                                name="grads_" + stage + "_start")
        sent[stage] = (names, [p[0] for p in parts], handles)
        return handles[4]

    loss_tile, grad_x, gw, gs = _local_step(x[0], loss_target[0], small["meta"], w_a_in, small, start_token,
                                            late_weights, grads_ready)

    last = ("a_w_in", "a_w_out")
    s32, s16 = {}, {}
    for n in last:
        s32[n], s16[n] = split(n, gw[n], "")
    recv1 = dict(zip(last, _exchange([s16[n] for n in last], 4, lambda j: 2 * j + (1 - lax.axis_index("c")),
                                     ["c"] * 4, name="grad_to_sibling")))
    chip = 2 * mx + my
    others = [2 * (1 - mx) + my, 2 * mx + (1 - my), 2 * (1 - mx) + (1 - my)]
    sidx = jnp.stack([2 * o + mc for o in others]).astype(jnp.int32)
    ridx = jnp.stack(others).astype(jnp.int32)
    to_send = [_pair_add(s32[n], recv1[n], sidx, ridx, name="pair_add_" + n) for n in last]
    recv2 = dict(zip(last, _exchange(to_send, 3, lambda j: j, ["x", "y", "xy"], name="grad_to_chips")))
    own = jnp.stack([me, chip]).astype(jnp.int32)
    big_out = {n: _adamw_shard(s32[n], recv1[n], recv2[n], wloc[n], mloc[n], vloc[n], own, name="adamw_" + n)
               for n in last}

    got = {}
    for stage, (names, own32, (ssem_g, rsem_g, srcs_g, lands_g, _)) in sent.items():
        _, landed = _direct_wait(ssem_g, rsem_g, srcs_g, lands_g, True, recv2["a_w_in"], name="grads_" + stage + "_wait")
        for n, o32, r in zip(names, own32, landed):
            got.setdefault(n, []).append((stage, o32, r))
    me1 = jnp.stack([me]).astype(jnp.int32)
    for n, parts in got.items():
        parts = sorted(parts, key=lambda t: t[0])
        big_out[n] = _adamw_direct([p[1] for p in parts], [p[2] for p in parts], wloc[n], mloc[n], vloc[n], me1,
                                   name="adamw_" + n)

    names = list(_SMALL)
    pbuf, poffs = _pack_small([gs[n] for n in names] + [loss_tile[0:1, 0:1]], 1024)
    psum = _sum_devices(_all_gather([pbuf], name="gather_small_grads")[0])
    loss = psum[poffs[-1][0], 0]
    g_small = {}
    for n, off in zip(names, poffs[:-1]):
        full_shape = gs[n].shape
        gfull = _unpack_small(psum, off, full_shape)
        if n in _SMALL_SHARDED:
            ns = wloc[n].shape[-1]
            gfull = lax.dynamic_slice_in_dim(gfull, me * ns, ns, axis=gfull.ndim - 1)
        g_small[n] = gfull.reshape(wloc[n].shape)
    gbuf, aoffs = _pack_small([g_small[n] for n in names], 128)
    wbuf, _ = _pack_small([wloc[n] for n in names], 128)
    mbuf, _ = _pack_small([mloc[n] for n in names], 128)
    vbuf, _ = _pack_small([vloc[n] for n in names], 128)
    _, d_s, m_s, v_s = _adamw([gbuf], wbuf, mbuf, vbuf, name="adamw_small")

    grads, deltas, new_m, new_v = {}, {}, {}, {}
    for n in _BIG:
        grads[n], deltas[n], new_m[n], new_v[n] = big_out[n]
    for n, off in zip(names, aoffs):
        sh = wloc[n].shape
        grads[n] = g_small[n]
        deltas[n], new_m[n], new_v[n] = (_unpack_small(b_, off, sh) for b_ in (d_s, m_s, v_s))
    return (loss, grad_x[None], *[grads[n] for n in _ORDER], *[deltas[n] for n in _ORDER],
            *[new_m[n] for n in _ORDER], *[new_v[n] for n in _ORDER])
```

```python
import math

import jax
import jax.numpy as jnp
from jax import lax
from jax.experimental import pallas as pl
from jax.experimental.pallas import tpu as pltpu

f32, bf16 = jnp.float32, jnp.bfloat16
S = jax.ShapeDtypeStruct
HI = lax.Precision.HIGHEST
HI3 = lax.Precision.HIGH
MESH = pl.DeviceIdType.MESH

V7X_VMEM_LIMIT = 56 * 1024 * 1024
LANE = 128
DH = 128
CH = 64
PADF = 256
TM = 256
TMM = 768
N_DEV = 8
BWD_HEAD_GROUP = 4

DEPTH = 2
ALPHA = (2.0 * DEPTH) ** 0.25
LN_EPS = 1e-5
RMS_EPS = 1e-6
L2_EPS = 1e-6
ADAM_LR, ADAM_B1, ADAM_B2, ADAM_EPS, ADAM_WD, ADAM_STEP = 0.001, 0.9, 0.999, 1e-08, 0.01, 10


def _cp(**kw):
    return pltpu.CompilerParams(vmem_limit_bytes=V7X_VMEM_LIMIT, **kw)


def _bf(x):
    return x.astype(bf16)


def _dot(a, b, precision=None):
    return jnp.dot(a, b, preferred_element_type=f32, precision=precision)


def _dot_nt(a, b):
    return lax.dot_general(a, b, (((1,), (1,)), ((), ())), preferred_element_type=f32)


def _dot_tn(a, b):
    return lax.dot_general(a, b, (((0,), (0,)), ((), ())), preferred_element_type=f32)


def _sigmoid(x):
    return 1.0 / (1.0 + jnp.exp(-x))


def _load_once(pairs, sem):
    @pl.when(pl.program_id(0) == 0)
    def _():
        cps = [pltpu.make_async_copy(src, dst, sem.at[n]) for n, (src, dst) in enumerate(pairs)]
        for c in cps:
            c.start()
        for c in cps:
            c.wait()


def _row_ids(i, tm, width):
    return i * tm + lax.broadcasted_iota(jnp.int32, (tm, width), 0)


def _ln_fwd(pre, g, b, rows, first_row):
    mu = jnp.mean(pre, axis=-1, keepdims=True)
    xc = pre - mu
    var = jnp.mean(xc * xc, axis=-1, keepdims=True)
    y = xc * lax.rsqrt(var + LN_EPS) * g + b
    return jnp.where(rows >= first_row, y, 0.0)


ANY = pl.BlockSpec(memory_space=pl.ANY)


def _taps_back(prev8, x, kw):
    xe = jnp.concatenate([prev8, x], axis=0)
    return [pltpu.roll(xe, kw - 1 - j, 0)[8:] for j in range(kw - 1)] + [x]


def _taps_ahead(x, next8, kw):
    n = x.shape[0]
    xe = jnp.concatenate([x, next8], axis=0)
    return [pltpu.roll(xe, n + 8 - (kw - 1 - j), 0)[:n] for j in range(kw - 1)] + [x]


def _conv(cw, taps):
    acc = cw[0:1, :] * taps[0]
    for j in range(1, len(taps)):
        acc = acc + cw[j:j + 1, :] * taps[j]
    return acc


def _linear_dw(x, dy, *, name):
    L, K = x.shape
    N = dy.shape[1]
    tm = TMM if L % TMM == 0 else TM
    tn = LANE
    for d in range(N // LANE, 0, -1):
        if (N // LANE) % d == 0 and K * d * LANE * 4 <= 9 * 1024 * 1024:
            tn = d * LANE
            break

    def body(x_ref, dy_ref, o_ref):
        @pl.when(pl.program_id(1) == 0)
        def _():
            o_ref[...] = jnp.zeros_like(o_ref)
        o_ref[...] += _dot_tn(_bf(x_ref[...]), _bf(dy_ref[...]))

    return pl.pallas_call(
        body, name=name, grid=(N // tn, L // tm), out_shape=S((K, N), f32),
        in_specs=[pl.BlockSpec((tm, K), lambda j, i: (i, 0)), pl.BlockSpec((tm, tn), lambda j, i: (i, j))],
        out_specs=pl.BlockSpec((K, tn), lambda j, i: (0, j)),
        compiler_params=_cp(dimension_semantics=("arbitrary", "arbitrary")))(x, dy)


def _ln_bwd_rows(dout, pre, g, rows, first_row):
    mu = jnp.mean(pre, axis=-1, keepdims=True)
    xc = pre - mu
    rstd = lax.rsqrt(jnp.mean(xc * xc, axis=-1, keepdims=True) + LN_EPS)
    xh = xc * rstd
    dy = jnp.where(rows >= first_row, dout, 0.0)
    dxh = dy * g
    dpre = rstd * (dxh - jnp.mean(dxh, axis=-1, keepdims=True) - xh * jnp.mean(dxh * xh, axis=-1, keepdims=True))
    return dpre, jnp.sum(dy * xh, axis=0, keepdims=True), jnp.sum(dy, axis=0, keepdims=True)


def _linear_dx(dy, w, *, name):
    L, N = dy.shape
    K = w.shape[0]
    tm = TM

    def body(dy_ref, w_hbm, o_ref, w_vmem, sem):
        _load_once([(w_hbm, w_vmem)], sem)
        o_ref[...] = _dot_nt(_bf(dy_ref[...]), w_vmem[...])

    row = lambda i: (i, 0)
    return pl.pallas_call(
        body, name=name, grid=(L // tm,), out_shape=S((L, K), f32),
        in_specs=[pl.BlockSpec((tm, N), row), ANY], out_specs=pl.BlockSpec((tm, K), row),
        scratch_shapes=[pltpu.VMEM((K, N), w.dtype), pltpu.SemaphoreType.DMA((1,))],
        compiler_params=_cp(dimension_semantics=("arbitrary",)))(dy, w)


def _out_res_ln(y, w, h, g, b, *, first_row, name):
    L, K = y.shape
    D = w.shape[1]
    tm = TM
    alpha = ALPHA

    def body(y_ref, w_hbm, h_ref, g_ref, b_ref, pre_ref, out_ref, w_vmem, sem):
        _load_once([(w_hbm, w_vmem)], sem)
        pre = alpha * h_ref[...] + _dot(_bf(y_ref[...]), w_vmem[...])
        pre_ref[...] = pre
        out_ref[...] = _ln_fwd(pre, g_ref[...], b_ref[...], _row_ids(pl.program_id(0), tm, D), first_row)

    row = lambda i: (i, 0)
    fix = lambda i: (0, 0)
    return pl.pallas_call(
        body, name=name, grid=(L // tm,), out_shape=(S((L, D), f32), S((L, D), f32)),
        in_specs=[pl.BlockSpec((tm, K), row), ANY, pl.BlockSpec((tm, D), row),
                  pl.BlockSpec((1, D), fix), pl.BlockSpec((1, D), fix)],
        out_specs=(pl.BlockSpec((tm, D), row), pl.BlockSpec((tm, D), row)),
        scratch_shapes=[pltpu.VMEM((K, D), w.dtype), pltpu.SemaphoreType.DMA((1,))],
        compiler_params=_cp(dimension_semantics=("arbitrary",)))(y, w, h, g, b)


def _gdn_in_fwd(x, head, w_full, conv_w, alog, dtb, *, first_row, H):
    D = x.shape[1]
    L = PADF + x.shape[0]
    W = H * DH
    NW = w_full.shape[1]
    KW = conv_w.shape[0]
    tm = TM
    pb = PADF // tm

    def body(x_ref, head_ref, w_hbm, cw_ref, alog_ref, dtb_ref,
             h_ref, pre_ref, z_ref, raw_ref, q_ref, k_ref, v_ref, beta_ref, g_ref,
             w_vmem, carry, sem):
        i = pl.program_id(0)
        _load_once([(w_hbm, w_vmem)], sem)

        @pl.when(i == 0)
        def _():
            carry[...] = jnp.zeros_like(carry)

        hv = jnp.where(i < pb, head_ref[...], x_ref[...])
        h_ref[...] = hv
        hb = _bf(hv)
        outs = (q_ref, k_ref, v_ref)
        for s in range(3):
            pre = _dot(hb, w_vmem[:, s * W:(s + 1) * W])
            pre_ref[:, s * W:(s + 1) * W] = pre
            c = _conv(cw_ref[:, s * W:(s + 1) * W], _taps_back(carry[s], pre, KW))
            carry[s] = pre[tm - 8:tm, :]
            sl = c * _sigmoid(c)
            if s < 2:
                scale = DH ** -0.5 if s == 0 else 1.0
                for hh in range(H):
                    seg = sl[:, hh * DH:(hh + 1) * DH]
                    r = lax.rsqrt(jnp.sum(seg * seg, axis=-1, keepdims=True) + L2_EPS)
                    outs[s][:, hh * DH:(hh + 1) * DH] = seg * (r * scale)
            else:
                v_ref[...] = sl
        z_ref[...] = _dot(hb, w_vmem[:, 3 * W:4 * W])
        raw = _dot(hb, w_vmem[:, 4 * W:4 * W + 2 * LANE])
        raw_ref[...] = raw
        ok = (_row_ids(i, tm, LANE) >= first_row) & (lax.broadcasted_iota(jnp.int32, (tm, LANE), 1) < H)
        beta_ref[...] = jnp.where(ok, _sigmoid(raw[:, :LANE]), 0.0)
        a = raw[:, LANE:] + dtb_ref[...]
        sp = jnp.maximum(a, 0.0) + jnp.log(1.0 + jnp.exp(-jnp.abs(a)))
        gv = jnp.where(ok, -jnp.exp(alog_ref[...]) * sp, 0.0)
        g_ref[...] = _dot(_chunk_tri(tm, lower=True), gv, HI)

    row = lambda i: (i, 0)
    fix = lambda i: (0, 0)
    out_shape = (S((L, D), f32), S((L, 3 * W), f32), S((L, W), f32), S((L, 2 * LANE), f32),
                 S((L, W), f32), S((L, W), f32), S((L, W), f32), S((L, LANE), f32), S((L, LANE), f32))
    out_specs = (pl.BlockSpec((tm, D), row),
                 pl.BlockSpec((tm, 3 * W), row), pl.BlockSpec((tm, W), row), pl.BlockSpec((tm, 2 * LANE), row),
                 pl.BlockSpec((tm, W), row), pl.BlockSpec((tm, W), row), pl.BlockSpec((tm, W), row),
                 pl.BlockSpec((tm, LANE), row), pl.BlockSpec((tm, LANE), row))
    return pl.pallas_call(
        body, name="gdn_in_fwd", grid=(L // tm,), out_shape=out_shape,
        in_specs=[pl.BlockSpec((tm, D), lambda i: (jnp.maximum(i - pb, 0), 0)),
                  pl.BlockSpec((tm, D), lambda i: (jnp.minimum(i, pb - 1), 0)), ANY, pl.BlockSpec((KW, 3 * W), fix),
                  pl.BlockSpec((1, LANE), fix), pl.BlockSpec((1, LANE), fix)],
        out_specs=out_specs,
        scratch_shapes=[pltpu.VMEM((D, NW), w_full.dtype), pltpu.VMEM((3, 8, W), f32), pltpu.SemaphoreType.DMA((1,))],
        compiler_params=_cp(dimension_semantics=("arbitrary",)))(x, head, w_full, conv_w, alog, dtb)


def _gdn_in_bwd(dq, dk, dv, dz, dg, dbeta, pre, raw, conv_w, alog, dtb, w_full, res, *, first_row, H):
    L = dq.shape[0]
    D = res.shape[1]
    W = H * DH
    KW = conv_w.shape[0]
    tm = TM
    nb = L // tm
    NW = 4 * W + 2 * LANE
    fb = PADF // tm
    alpha = ALPHA

    def body(dq_ref, dk_ref, dv_ref, dz_ref, dg_ref, dbeta_ref, pre_ref, hq_ref, hk_ref, hv_ref, raw_ref,
             cw_ref, alog_ref, dtb_ref, w_hbm, res_ref,
             dproj_ref, dcw_ref, dal_ref, ddt_ref, dx_ref, dfront_ref, w_vmem, carry, tmp, sem):
        i = pl.program_id(0)
        blk = nb - 1 - i
        _load_once([(w_hbm, w_vmem)], sem)

        @pl.when(i == 0)
        def _():
            carry[...] = jnp.zeros_like(carry)
            dcw_ref[...] = jnp.zeros_like(dcw_ref)
            dal_ref[...] = jnp.zeros_like(dal_ref)
            ddt_ref[...] = jnp.zeros_like(ddt_ref)

        halos = (hq_ref, hk_ref, hv_ref)
        douts = (dq_ref, dk_ref, dv_ref)
        for s in range(3):
            sec = slice(s * W, (s + 1) * W)
            pre = pre_ref[:, sec]
            c = _conv(cw_ref[:, sec], _taps_back(jnp.where(blk > 0, halos[s][...], 0.0), pre, KW))
            sig = _sigmoid(c)
            sl = c * sig
            if s < 2:
                scale = DH ** -0.5 if s == 0 else 1.0
                for hh in range(H):
                    hs = slice(hh * DH, (hh + 1) * DH)
                    seg = sl[:, hs]
                    r = lax.rsqrt(jnp.sum(seg * seg, axis=-1, keepdims=True) + L2_EPS)
                    n = seg * r
                    dqs = douts[s][:, hs]
                    tmp[:, hs] = (scale * r) * (dqs - n * jnp.sum(n * dqs, axis=-1, keepdims=True))
                dsl = tmp[...]
            else:
                dsl = dv_ref[...]
            dc = dsl * (sig * (1.0 + c * (1.0 - sig)))
            ahead = _taps_ahead(dc, carry[s], KW)
            carry[s] = dc[0:8, :]
            dproj_ref[:, sec] = _bf(_conv(cw_ref[:, sec], ahead))
            for j in range(KW):
                dcw_ref[j:j + 1, sec] += jnp.sum(ahead[j] * pre, axis=0, keepdims=True)
        dproj_ref[:, 3 * W:4 * W] = _bf(dz_ref[...])
        raw_v = raw_ref[...]
        ok = (_row_ids(blk, tm, LANE) >= first_row) & (lax.broadcasted_iota(jnp.int32, (tm, LANE), 1) < H)
        beta = _sigmoid(raw_v[:, :LANE])
        dbraw = jnp.where(ok, dbeta_ref[...] * beta * (1.0 - beta), 0.0)
        a = raw_v[:, LANE:] + dtb_ref[...]
        sp = jnp.maximum(a, 0.0) + jnp.log(1.0 + jnp.exp(-jnp.abs(a)))
        nea = -jnp.exp(alog_ref[...])
        dgm = jnp.where(ok, _dot(_chunk_tri(tm, lower=False), dg_ref[...], HI), 0.0)
        daraw = dgm * nea * _sigmoid(a)
        dal_ref[0:1, :] += jnp.sum(dgm * nea * sp, axis=0, keepdims=True)
        ddt_ref[0:1, :] += jnp.sum(daraw, axis=0, keepdims=True)
        dproj_ref[:, 4 * W:4 * W + LANE] = _bf(dbraw)
        dproj_ref[:, 4 * W + LANE:4 * W + 2 * LANE] = _bf(daraw)
        dh = alpha * res_ref[...] + _dot_nt(dproj_ref[...], w_vmem[...])

        @pl.when(blk >= fb)
        def _():
            dx_ref[...] = dh

        @pl.when(blk < fb)
        def _():
            dfront_ref[...] = dh

    rev = lambda i: (nb - 1 - i, 0)
    fix = lambda i: (0, 0)

    def halo(col):
        return pl.BlockSpec((8, W), lambda i: (jnp.maximum((nb - 1 - i) * (tm // 8) - 1, 0), col))

    return pl.pallas_call(
        body, name="gdn_in_bwd", grid=(nb,),
        out_shape=(S((L, NW), bf16), S((8, 3 * W), f32), S((8, LANE), f32), S((8, LANE), f32),
                   S((L - PADF, D), f32), S((PADF, D), f32)),
        in_specs=[pl.BlockSpec((tm, W), rev)] * 4 + [pl.BlockSpec((tm, LANE), rev)] * 2
        + [pl.BlockSpec((tm, 3 * W), rev), halo(0), halo(1), halo(2), pl.BlockSpec((tm, 2 * LANE), rev),
           pl.BlockSpec((KW, 3 * W), fix), pl.BlockSpec((1, LANE), fix), pl.BlockSpec((1, LANE), fix),
           ANY, pl.BlockSpec((tm, D), rev)],
        out_specs=(pl.BlockSpec((tm, NW), rev), pl.BlockSpec((8, 3 * W), fix),
                   pl.BlockSpec((8, LANE), fix), pl.BlockSpec((8, LANE), fix),
                   pl.BlockSpec((tm, D), lambda i: (jnp.maximum(nb - 1 - i - fb, 0), 0)),
                   pl.BlockSpec((tm, D), lambda i: (jnp.minimum(nb - 1 - i, fb - 1), 0))),
        scratch_shapes=[pltpu.VMEM((D, NW), w_full.dtype), pltpu.VMEM((3, 8, W), f32), pltpu.VMEM((tm, W), f32),
                        pltpu.SemaphoreType.DMA((1,))],
        compiler_params=_cp(dimension_semantics=("arbitrary",)))(
            dq, dk, dv, dz, dg, dbeta, pre, pre, pre, pre, raw, conv_w, alog, dtb, w_full, res)


def _chunk_tri(n, lower):
    i = lax.broadcasted_iota(jnp.int32, (n, n), 0)
    j = lax.broadcasted_iota(jnp.int32, (n, n), 1)
    sh = int(math.log2(CH))
    same = lax.shift_right_logical(i, sh) == lax.shift_right_logical(j, sh)
    return (same & ((i >= j) if lower else (j >= i))).astype(f32)


def _tri_inv_many(ms, eye):
    ts = [eye - m for m in ms]
    ps = list(ms)
    for _ in range(int(math.log2(CH)) - 1):
        ps = [_dot(p, p, HI3) for p in ps]
        ts = [t + _dot(t, p, HI3) for t, p in zip(ts, ps)]
    return ts


def _chunk_local(q, k, v, gcol, grow, glast, bcol, ii, jj):
    dec = jnp.where(ii >= jj, jnp.exp(jnp.minimum(gcol - grow, 0.0)), 0.0)
    eg = jnp.exp(gcol)
    kb = k * bcol
    kbg = kb * eg
    vb = v * bcol
    qt = q * eg
    kt = k * jnp.exp(glast - gcol)
    kk = _dot_nt(_bf(kb), _bf(k))
    qk = _dot_nt(_bf(q), _bf(k))
    return dec, eg, kb, kbg, vb, qt, kt, kk, qk


def _delta_fwd(q, k, v, g, beta, z, nw, *, H):
    L = q.shape[0]
    W = H * DH
    rb = TM
    nc = rb // CH
    nblk = L // rb

    def body(q_ref, k_ref, v_ref, g_ref, b_ref, z_ref, nw_ref, o_ref, y_ref, s_out, t_out, s_scr):
        @pl.when(pl.program_id(0) == 0)
        def _():
            s_scr[...] = jnp.zeros_like(s_scr)

        ii = lax.broadcasted_iota(jnp.int32, (CH, CH), 0)
        jj = lax.broadcasted_iota(jnp.int32, (CH, CH), 1)
        eye = (ii == jj).astype(f32)
        nwv = nw_ref[...]

        heads = range(H)
        hsl = [slice(hh * DH, (hh + 1) * DH) for hh in heads]

        def chunk(c, carry):
            r0 = pl.multiple_of(c * CH, CH)
            rows = pl.ds(r0, CH)
            gam = g_ref[rows, :]
            gam_t = gam.T
            bb = b_ref[rows, :]
            glast = [gam[CH - 1:CH, hh:hh + 1] for hh in heads]
            loc = [_chunk_local(q_ref[rows, hsl[hh]], k_ref[rows, hsl[hh]], v_ref[rows, hsl[hh]],
                                gam[:, hh:hh + 1], gam_t[hh:hh + 1, :], glast[hh], bb[:, hh:hh + 1], ii, jj)
                   for hh in heads]
            st = [s_scr[hh] for hh in heads]
            zs = [z_ref[rows, hsl[hh]] for hh in heads]
            ts = _tri_inv_many([jnp.where(ii > jj, l[7] * l[0], 0.0) for l in loc], eye)
            us = [_dot(t, l[4], HI3) for t, l in zip(ts, loc)]
            ws = [_dot(t, l[3], HI3) for t, l in zip(ts, loc)]
            stb = [_bf(s) for s in st]
            vn = [u - _dot(_bf(w), sb) for u, w, sb in zip(us, ws, stb)]
            vnb = [_bf(x) for x in vn]
            snew = [s * jnp.exp(gl) + _dot_tn(_bf(l[6]), xb) for s, gl, l, xb in zip(st, glast, loc, vnb)]
            os_ = [_dot(_bf(l[5]), sb) + _dot(_bf(l[8] * l[0]), xb) for l, sb, xb in zip(loc, stb, vnb)]
            for hh in heads:
                o = os_[hh]
                s_out[c, hh] = st[hh]
                t_out[c, hh] = ts[hh]
                s_scr[hh] = snew[hh]
                o_ref[rows, hsl[hh]] = o
                on = o * lax.rsqrt(jnp.mean(o * o, axis=-1, keepdims=True) + RMS_EPS) * nwv
                y_ref[rows, hsl[hh]] = _bf(on * (zs[hh] * _sigmoid(zs[hh])))
            return carry

        lax.fori_loop(0, nc, chunk, 0)

    row = lambda i: (i, 0)
    fix = lambda i: (0, 0)
    return pl.pallas_call(
        body, name="delta_fwd", grid=(nblk,),
        out_shape=(S((L, W), f32), S((L, W), bf16), S((L // CH, H, DH, DH), f32), S((L // CH, H, CH, CH), f32)),
        in_specs=[pl.BlockSpec((rb, W), row)] * 3 + [pl.BlockSpec((rb, LANE), row)] * 2
        + [pl.BlockSpec((rb, W), row), pl.BlockSpec((1, DH), fix)],
        out_specs=(pl.BlockSpec((rb, W), row), pl.BlockSpec((rb, W), row),
                   pl.BlockSpec((nc, H, DH, DH), lambda i: (i, 0, 0, 0)),
                   pl.BlockSpec((nc, H, CH, CH), lambda i: (i, 0, 0, 0))),
        scratch_shapes=[pltpu.VMEM((H, DH, DH), f32)],
        compiler_params=_cp(dimension_semantics=("arbitrary",)))(q, k, v, g, beta, z, nw)


def _delta_bwd(dy, o, z, nw, q, k, v, g, beta, s_all, t_all, *, H):
    L = q.shape[0]
    W = H * DH
    rb = TM
    nc = rb // CH
    nblk = L // rb

    def body(dy_ref, o_ref, z_ref, nw_ref, q_ref, k_ref, v_ref, g_ref, b_ref, s_ref, t_ref,
             dq_ref, dk_ref, dv_ref, dz_ref, dg_ref, db_ref, dnw_ref, ds_scr):
        @pl.when(pl.program_id(0) == 0)
        def _():
            ds_scr[...] = jnp.zeros_like(ds_scr)
            dnw_ref[...] = jnp.zeros_like(dnw_ref)

        ii = lax.broadcasted_iota(jnp.int32, (CH, CH), 0)
        jj = lax.broadcasted_iota(jnp.int32, (CH, CH), 1)
        lane = lax.broadcasted_iota(jnp.int32, (CH, LANE), 1)
        last_row = lax.broadcasted_iota(jnp.int32, (CH, 1), 0) == CH - 1
        nwv = nw_ref[...]

        def chunk(cc, carry):
            c = nc - 1 - cc
            r0 = pl.multiple_of(c * CH, CH)
            rows = pl.ds(r0, CH)
            gam = g_ref[rows, :]
            gam_t = gam.T
            bb = b_ref[rows, :]

            def head(hh):
                hs = slice(hh * DH, (hh + 1) * DH)
                gcol, grow, glast = gam[:, hh:hh + 1], gam_t[hh:hh + 1, :], gam[CH - 1:CH, hh:hh + 1]
                bcol = bb[:, hh:hh + 1]
                qh, kh, vh = q_ref[rows, hs], k_ref[rows, hs], v_ref[rows, hs]
                oh, zh, dyh = o_ref[rows, hs], z_ref[rows, hs], dy_ref[rows, hs]
                t = t_ref[c, hh]
                st = s_ref[c, hh]
                dsn = ds_scr[hh]
                rms = lax.rsqrt(jnp.mean(oh * oh, axis=-1, keepdims=True) + RMS_EPS)
                on = oh * rms
                sig = _sigmoid(zh)
                sz = zh * sig
                dz_ref[rows, hs] = dyh * on * nwv * (sig * (1.0 + zh * (1.0 - sig)))
                dnw = jnp.sum(dyh * on * sz, axis=0, keepdims=True)
                don = dyh * nwv * sz
                do = rms * (don - on * jnp.mean(don * on, axis=-1, keepdims=True))
                dec, eg, kb, kbg, vb, qt, kt, kk, qk = _chunk_local(qh, kh, vh, gcol, grow, glast, bcol, ii, jj)
                stb, dsnb, dob, tb, kbgb = _bf(st), _bf(dsn), _bf(do), _bf(t), _bf(kbg)
                r = vb - _dot(kbgb, stb)
                dqt = _dot_nt(dob, stb)
                ds_new = _dot_tn(_bf(qt), dob)
                mm = jnp.where(ii > jj, kk * dec, 0.0)
                attn = qk * dec
                yield
                vn = _dot(t, r, HI3)
                dvn = _dot_tn(_bf(attn), dob) + _dot(_bf(kt), dsnb)
                egl = jnp.exp(glast)
                ekt = jnp.exp(glast - gcol)
                yield
                vnb, dvnb = _bf(vn), _bf(dvn)
                dattn = jnp.where(ii >= jj, _dot_nt(dob, vnb), 0.0)
                dkt = _dot_nt(vnb, dsnb)
                dvb = _dot_tn(tb, dvnb)
                dt = _dot_nt(dvnb, _bf(r))
                dglast = egl * jnp.sum(jnp.sum(dsn * st, axis=0, keepdims=True), axis=1, keepdims=True)
                yield
                dvbb = _bf(dvb)
                dv_ref[rows, hs] = dvb * bcol
                ds_scr[hh] = ds_new + egl * dsn - _dot_tn(kbgb, dvbb)
                dkbg = -_dot_nt(dvbb, stb)
                x = _dot_nt(_bf(dt), tb)
                yield
                dm = jnp.where(ii > jj, -_dot_tn(tb, _bf(x)), 0.0)
                dkk = dm * dec
                dqk = dattn * dec
                e = dm * mm + dattn * attn
                dgam = jnp.sum(e, axis=1, keepdims=True) - jnp.sum(e.T, axis=1, keepdims=True)
                dkkb, dqkb, kbf = _bf(dkk), _bf(dqk), _bf(kh)
                dkb = _dot(dkkb, kbf) + dkbg * eg
                dk_ref[rows, hs] = _dot_tn(dkkb, _bf(kb)) + _dot_tn(dqkb, _bf(qh)) + dkt * ekt + dkb * bcol
                dq_ref[rows, hs] = _dot(dqkb, kbf) + dqt * eg
                yield
                dktkt = dkt * kt
                dgam = dgam + jnp.sum(dqt * qt - dktkt + dkbg * kbg, axis=1, keepdims=True)
                dglast = dglast + jnp.sum(jnp.sum(dktkt, axis=0, keepdims=True), axis=1, keepdims=True)
                dgam = dgam + jnp.where(last_row, dglast, 0.0)
                dbeta = jnp.sum(dkb * kh + dvb * vh, axis=1, keepdims=True)
                return dgam, dbeta, dnw

            res = [None] * H
            for h0 in range(0, H, BWD_HEAD_GROUP):
                group = range(h0, min(h0 + BWD_HEAD_GROUP, H))
                gens = {hh: head(hh) for hh in group}
                while any(res[hh] is None for hh in group):
                    for hh in group:
                        try:
                            next(gens[hh])
                        except StopIteration as stop:
                            res[hh] = stop.value
            dgam_all = jnp.zeros((CH, LANE), f32)
            dbeta_all = jnp.zeros((CH, LANE), f32)
            dnw_acc = jnp.zeros((1, DH), f32)
            for hh in range(H):
                dgam, dbeta, dnw = res[hh]
                dgam_all = dgam_all + jnp.where(lane == hh, dgam, 0.0)
                dbeta_all = dbeta_all + jnp.where(lane == hh, dbeta, 0.0)
                dnw_acc = dnw_acc + dnw
            dg_ref[rows, :] = dgam_all
            db_ref[rows, :] = dbeta_all
            dnw_ref[0:1, :] += dnw_acc
            return carry

        lax.fori_loop(0, nc, chunk, 0)

    rev = lambda i: (nblk - 1 - i, 0)
    rev4 = lambda i: (nblk - 1 - i, 0, 0, 0)
    fix = lambda i: (0, 0)
    wide = pl.BlockSpec((rb, W), rev)
    thin = pl.BlockSpec((rb, LANE), rev)
    return pl.pallas_call(
        body, name="delta_bwd", grid=(nblk,),
        out_shape=(S((L, W), f32),) * 4 + (S((L, LANE), f32),) * 2 + (S((8, DH), f32),),
        in_specs=[wide, wide, wide, pl.BlockSpec((1, DH), fix), wide, wide, wide, thin, thin,
                  pl.BlockSpec((nc, H, DH, DH), rev4), pl.BlockSpec((nc, H, CH, CH), rev4)],
        out_specs=(wide,) * 4 + (thin, thin, pl.BlockSpec((8, DH), fix)),
        scratch_shapes=[pltpu.VMEM((H, DH, DH), f32)],
        compiler_params=_cp(dimension_semantics=("arbitrary",)))(dy, o, z, nw, q, k, v, g, beta, s_all, t_all)


def _sc_fwd(h, w_in, conv_w, w_out, g, b, *, first_row):
    L, D = h.shape
    W = w_out.shape[0]
    KW = conv_w.shape[0]
    tm = TM
    alpha = ALPHA

    def body(h_ref, win_hbm, cw_ref, wout_hbm, g_ref, b_ref, proj_ref, bu_ref, pre_ref, out_ref,
             win, wout, carry, sem):
        i = pl.program_id(0)
        _load_once([(win_hbm, win), (wout_hbm, wout)], sem)

        @pl.when(i == 0)
        def _():
            carry[...] = jnp.zeros_like(carry)

        hv = h_ref[...]
        hb = _bf(hv)
        bg = _dot(hb, win[:, 0:W])
        cg = _dot(hb, win[:, W:2 * W])
        xv = _dot(hb, win[:, 2 * W:3 * W])
        proj_ref[:, 0:W] = bg
        proj_ref[:, W:2 * W] = cg
        proj_ref[:, 2 * W:3 * W] = xv
        p = cg * xv
        u = _conv(cw_ref[...], _taps_back(carry[...], p, KW))
        carry[...] = p[tm - 8:tm, :]
        bu = _bf(bg * u)
        bu_ref[...] = bu
        pre = alpha * hv + _dot(bu, wout[...])
        pre_ref[...] = pre
        out_ref[...] = _ln_fwd(pre, g_ref[...], b_ref[...], _row_ids(i, tm, D), first_row)

    row = lambda i: (i, 0)
    fix = lambda i: (0, 0)
    return pl.pallas_call(
        body, name="sc_fwd", grid=(L // tm,),
        out_shape=(S((L, 3 * W), f32), S((L, W), bf16), S((L, D), f32), S((L, D), f32)),
        in_specs=[pl.BlockSpec((tm, D), row), ANY, pl.BlockSpec((KW, W), fix), ANY,
                  pl.BlockSpec((1, D), fix), pl.BlockSpec((1, D), fix)],
        out_specs=(pl.BlockSpec((tm, 3 * W), row), pl.BlockSpec((tm, W), row),
                   pl.BlockSpec((tm, D), row), pl.BlockSpec((tm, D), row)),
        scratch_shapes=[pltpu.VMEM((D, 3 * W), w_in.dtype), pltpu.VMEM((W, D), w_out.dtype),
                        pltpu.VMEM((8, W), f32), pltpu.SemaphoreType.DMA((2,))],
        compiler_params=_cp(dimension_semantics=("arbitrary",)))(h, w_in, conv_w, w_out, g, b)


def _sc_bwd(dpre, proj, conv_w, w_out, w_in, pre_in, g_in, *, first_row):
    L, D = dpre.shape
    W = w_out.shape[0]
    KW = conv_w.shape[0]
    tm = TM
    nb = L // tm
    alpha = ALPHA

    def body(dpre_ref, proj_ref, hc_ref, hx_ref, cw_ref, wout_hbm, win_hbm, pin_ref, g_ref,
             dproj_ref, dcw_ref, dpin_ref, dg_ref, db_ref, wout, win, carry, sem):
        i = pl.program_id(0)
        blk = nb - 1 - i
        _load_once([(wout_hbm, wout), (win_hbm, win)], sem)

        @pl.when(i == 0)
        def _():
            carry[...] = jnp.zeros_like(carry)
            dcw_ref[...] = jnp.zeros_like(dcw_ref)
            dg_ref[...] = jnp.zeros_like(dg_ref)
            db_ref[...] = jnp.zeros_like(db_ref)

        bg, cg, xv = proj_ref[:, 0:W], proj_ref[:, W:2 * W], proj_ref[:, 2 * W:3 * W]
        p = cg * xv
        u = _conv(cw_ref[...], _taps_back(jnp.where(blk > 0, hc_ref[...] * hx_ref[...], 0.0), p, KW))
        dpre_v = dpre_ref[...]
        d = _dot_nt(_bf(dpre_v), wout[...])
        dproj_ref[:, 0:W] = _bf(d * u)
        du = d * bg
        ahead = _taps_ahead(du, carry[...], KW)
        carry[...] = du[0:8, :]
        dp = _conv(cw_ref[...], ahead)
        for j in range(KW):
            dcw_ref[j:j + 1, :] += jnp.sum(ahead[j] * p, axis=0, keepdims=True)
        dproj_ref[:, W:2 * W] = _bf(dp * xv)
        dproj_ref[:, 2 * W:3 * W] = _bf(dp * cg)
        dh = alpha * dpre_v + _dot_nt(dproj_ref[...], win[...])
        dpin, dg, dbias = _ln_bwd_rows(dh, pin_ref[...], g_ref[...], _row_ids(blk, tm, D), first_row)
        dpin_ref[...] = dpin
        dg_ref[0:1, :] += dg
        db_ref[0:1, :] += dbias

    rev = lambda i: (nb - 1 - i, 0)
    fix = lambda i: (0, 0)

    def halo(col):
        return pl.BlockSpec((8, W), lambda i: (jnp.maximum((nb - 1 - i) * (tm // 8) - 1, 0), col))

    return pl.pallas_call(
        body, name="sc_bwd", grid=(nb,),
        out_shape=(S((L, 3 * W), bf16), S((8, W), f32), S((L, D), f32), S((8, D), f32), S((8, D), f32)),
        in_specs=[pl.BlockSpec((tm, D), rev), pl.BlockSpec((tm, 3 * W), rev), halo(1), halo(2),
                  pl.BlockSpec((KW, W), fix), ANY, ANY, pl.BlockSpec((tm, D), rev), pl.BlockSpec((1, D), fix)],
        out_specs=(pl.BlockSpec((tm, 3 * W), rev), pl.BlockSpec((8, W), fix), pl.BlockSpec((tm, D), rev),
                   pl.BlockSpec((8, D), fix), pl.BlockSpec((8, D), fix)),
        scratch_shapes=[pltpu.VMEM((W, D), w_out.dtype), pltpu.VMEM((D, 3 * W), w_in.dtype), pltpu.VMEM((8, W), f32),
                        pltpu.SemaphoreType.DMA((2,))],
        compiler_params=_cp(dimension_semantics=("arbitrary",)))(
            dpre, proj, proj, proj, conv_w, w_out, w_in, pre_in, g_in)


def _ffn_cols(F):
    fc = F
    for cand in (1408, 1024, 512, 256, 128):
        if F % cand == 0:
            fc = cand
            break
    return fc


def _ffn_fwd(h, w_up, conv_w, w_down, g, b, *, first_row, name):
    L, D = h.shape
    F = w_down.shape[0]
    KW = conv_w.shape[0]
    tm = TM
    fc = _ffn_cols(F)
    alpha = ALPHA

    def body(h_ref, wup_hbm, cw_ref, wdn_hbm, g_ref, b_ref, up_ref, a_ref, pre_ref, out_ref,
             wup, wdn, carry, sem):
        i = pl.program_id(0)
        _load_once([(wup_hbm, wup), (wdn_hbm, wdn)], sem)

        @pl.when(i == 0)
        def _():
            carry[...] = jnp.zeros_like(carry)

        hv = h_ref[...]
        hb = _bf(hv)
        pre = alpha * hv
        for c0 in range(0, F, fc):
            cs = slice(c0, c0 + fc)
            u = _dot(hb, wup[:, cs])
            gate = _dot(hb, wup[:, F + c0:F + c0 + fc])
            up_ref[:, cs] = u
            up_ref[:, F + c0:F + c0 + fc] = gate
            uc = _conv(cw_ref[:, cs], _taps_back(carry[:, cs], u, KW))
            carry[:, cs] = u[tm - 8:tm, :]
            ab = _bf(uc * _sigmoid(uc) * gate)
            a_ref[:, cs] = ab
            pre = pre + _dot(ab, wdn[cs, :])
        pre_ref[...] = pre
        out_ref[...] = _ln_fwd(pre, g_ref[...], b_ref[...], _row_ids(i, tm, D), first_row)

    row = lambda i: (i, 0)
    fix = lambda i: (0, 0)
    return pl.pallas_call(
        body, name=name, grid=(L // tm,),
        out_shape=(S((L, 2 * F), f32), S((L, F), bf16), S((L, D), f32), S((L, D), f32)),
        in_specs=[pl.BlockSpec((tm, D), row), ANY, pl.BlockSpec((KW, F), fix), ANY,
                  pl.BlockSpec((1, D), fix), pl.BlockSpec((1, D), fix)],
        out_specs=(pl.BlockSpec((tm, 2 * F), row), pl.BlockSpec((tm, F), row),
                   pl.BlockSpec((tm, D), row), pl.BlockSpec((tm, D), row)),
        scratch_shapes=[pltpu.VMEM((D, 2 * F), w_up.dtype), pltpu.VMEM((F, D), w_down.dtype),
                        pltpu.VMEM((8, F), f32), pltpu.SemaphoreType.DMA((2,))],
        compiler_params=_cp(dimension_semantics=("arbitrary",)))(h, w_up, conv_w, w_down, g, b)


def _ffn_bwd(dpre, up, w_down, conv_w, w_up, pre_in, g_in, *, first_row, name):
    L, D = dpre.shape
    F = w_down.shape[0]
    KW = conv_w.shape[0]
    tm = TM
    nb = L // tm
    fc = F
    alpha = ALPHA

    def body(dpre_ref, up_ref, halo_ref, wdn_hbm, cw_ref, wup_hbm, pin_ref, g_ref,
             dup_ref, dcw_ref, dpin_ref, dg_ref, db_ref, wdn, wup, carry, sem):
        i = pl.program_id(0)
        blk = nb - 1 - i
        _load_once([(wdn_hbm, wdn), (wup_hbm, wup)], sem)

        @pl.when(i == 0)
        def _():
            carry[...] = jnp.zeros_like(carry)
            dcw_ref[...] = jnp.zeros_like(dcw_ref)
            dg_ref[...] = jnp.zeros_like(dg_ref)
            db_ref[...] = jnp.zeros_like(db_ref)

        dpre_v = dpre_ref[...]
        db = _bf(dpre_v)
        dh = alpha * dpre_v
        for c0 in range(0, F, fc):
            cs = slice(c0, c0 + fc)
            gs_ = slice(F + c0, F + c0 + fc)
            da = _dot_nt(db, wdn[cs, :])
            gate = up_ref[:, gs_]
            u = up_ref[:, cs]
            uc = _conv(cw_ref[:, cs], _taps_back(jnp.where(blk > 0, halo_ref[:, cs], 0.0), u, KW))
            sig = _sigmoid(uc)
            dgate = _bf(da * (uc * sig))
            dup_ref[:, gs_] = dgate
            duc = da * gate * (sig * (1.0 + uc * (1.0 - sig)))
            ahead = _taps_ahead(duc, carry[:, cs], KW)
            carry[:, cs] = duc[0:8, :]
            du = _bf(_conv(cw_ref[:, cs], ahead))
            dup_ref[:, cs] = du
            for j in range(KW):
                dcw_ref[j:j + 1, cs] += jnp.sum(ahead[j] * u, axis=0, keepdims=True)
            dh = dh + _dot_nt(du, wup[:, cs]) + _dot_nt(dgate, wup[:, gs_])
        dpin, dg, dbias = _ln_bwd_rows(dh, pin_ref[...], g_ref[...], _row_ids(blk, tm, D), first_row)
        dpin_ref[...] = dpin
        dg_ref[0:1, :] += dg
        db_ref[0:1, :] += dbias

    rev = lambda i: (nb - 1 - i, 0)
    fix = lambda i: (0, 0)
    return pl.pallas_call(
        body, name=name, grid=(nb,),
        out_shape=(S((L, 2 * F), bf16), S((8, F), f32), S((L, D), f32), S((8, D), f32), S((8, D), f32)),
        in_specs=[pl.BlockSpec((tm, D), rev), pl.BlockSpec((tm, 2 * F), rev),
                  pl.BlockSpec((8, F), lambda i: (jnp.maximum((nb - 1 - i) * (tm // 8) - 1, 0), 0)),
                  ANY, pl.BlockSpec((KW, F), fix), ANY, pl.BlockSpec((tm, D), rev), pl.BlockSpec((1, D), fix)],
        out_specs=(pl.BlockSpec((tm, 2 * F), rev), pl.BlockSpec((8, F), fix), pl.BlockSpec((tm, D), rev),
                   pl.BlockSpec((8, D), fix), pl.BlockSpec((8, D), fix)),
        scratch_shapes=[pltpu.VMEM((F, D), w_down.dtype), pltpu.VMEM((D, 2 * F), w_up.dtype), pltpu.VMEM((8, F), f32),
                        pltpu.SemaphoreType.DMA((2,))],
        compiler_params=_cp(dimension_semantics=("arbitrary",)))(dpre, up, up, w_down, conv_w, w_up, pre_in, g_in)


def _loss_head(h, target, pre, g, *, first_row):
    L, D = h.shape
    tm = TM
    pb = PADF // tm

    def body(h_ref, t_ref, pre_ref, g_ref, dpre_ref, dg_ref, db_ref, loss_ref):
        i = pl.program_id(0)

        @pl.when(i == 0)
        def _():
            loss_ref[...] = jnp.zeros_like(loss_ref)
            dg_ref[...] = jnp.zeros_like(dg_ref)
            db_ref[...] = jnp.zeros_like(db_ref)

        valid = i >= pb
        err = h_ref[...] - t_ref[...]
        dh = jnp.where(valid, err * (1.0 / D), 0.0)
        part = 0.5 * jnp.sum(jnp.sum(err * err, axis=-1, keepdims=True) * (1.0 / D), axis=0, keepdims=True)
        loss_ref[...] += jnp.where(valid, part, 0.0)
        dpre, dg, db = _ln_bwd_rows(dh, pre_ref[...], g_ref[...], _row_ids(i, tm, D), first_row)
        dpre_ref[...] = dpre
        dg_ref[0:1, :] += dg
        db_ref[0:1, :] += db

    row = lambda i: (i, 0)
    fix = lambda i: (0, 0)
    return pl.pallas_call(
        body, name="loss_head", grid=(L // tm,),
        out_shape=(S((L, D), f32), S((8, D), f32), S((8, D), f32), S((8, LANE), f32)),
        in_specs=[pl.BlockSpec((tm, D), row), pl.BlockSpec((tm, D), lambda i: (jnp.maximum(i - pb, 0), 0)),
                  pl.BlockSpec((tm, D), row), pl.BlockSpec((1, D), fix)],
        out_specs=(pl.BlockSpec((tm, D), row), pl.BlockSpec((8, D), fix), pl.BlockSpec((8, D), fix),
                   pl.BlockSpec((8, LANE), fix)),
        compiler_params=_cp(dimension_semantics=("arbitrary",)))(h, target, pre, g)


def _adamw(g_terms, w, m, v, *, name):
    R, C = w.shape
    tr = _row_tile(R)
    n = len(g_terms)
    c1 = 1.0 - ADAM_B1 ** ADAM_STEP
    c2 = 1.0 - ADAM_B2 ** ADAM_STEP

    def body(*refs):
        g = refs[0][...].astype(f32)
        for r in refs[1:n]:
            g = g + r[...].astype(f32)
        w_ref, m_ref, v_ref, g_out, d_out, m_out, v_out = refs[n:]
        mn = ADAM_B1 * m_ref[...] + (1.0 - ADAM_B1) * g
        vn = ADAM_B2 * v_ref[...] + (1.0 - ADAM_B2) * (g * g)
        g_out[...] = g
        m_out[...] = mn
        v_out[...] = vn
        d_out[...] = -ADAM_LR * ((mn / c1) / (jnp.sqrt(vn / c2) + ADAM_EPS) + ADAM_WD * w_ref[...])

    spec = pl.BlockSpec((tr, C), lambda i: (i, 0))
    return pl.pallas_call(
        body, name=name, grid=(R // tr,), out_shape=(S((R, C), f32),) * 4,
        in_specs=[spec] * (n + 3), out_specs=(spec,) * 4,
        compiler_params=_cp(dimension_semantics=("arbitrary",)))(*g_terms, w, m, v)


def _sum_devices(x):
    n, R, C = x.shape

    def body(x_ref, o_ref):
        acc = x_ref[0]
        for d in range(1, n):
            acc = acc + x_ref[d]
        o_ref[...] = acc

    return pl.pallas_call(body, name="sum_devices", out_shape=S((R, C), f32), compiler_params=_cp())(x)


def _row_tile(R):
    for step in (16, 8):
        for t in range(256, 0, -step):
            if R % t == 0:
                return t
    return R


def _adamw_direct(s32s, recvs, w, m, v, me, *, name):
    L, K, n = w.shape
    tk = _row_tile(K)
    c1 = 1.0 - ADAM_B1 ** ADAM_STEP
    c2 = 1.0 - ADAM_B2 ** ADAM_STEP

    def body(me_ref, *refs):
        own_refs, recv_refs = refs[:L], refs[L:2 * L]
        w_ref, m_ref, v_ref, g_out, d_out, m_out, v_out = refs[2 * L:]
        for li in range(L):
            @pl.when(pl.program_id(0) == li)
            def _(li=li):
                g = own_refs[li][0, 0]
                for d in range(N_DEV):
                    g = g + recv_refs[li][d, 0].astype(f32)
                mn = ADAM_B1 * m_ref[0] + (1.0 - ADAM_B1) * g
                vn = ADAM_B2 * v_ref[0] + (1.0 - ADAM_B2) * (g * g)
                g_out[0] = g
                m_out[0] = mn
                v_out[0] = vn
                d_out[0] = -ADAM_LR * ((mn / c1) / (jnp.sqrt(vn / c2) + ADAM_EPS) + ADAM_WD * w_ref[0])

    own = pl.BlockSpec((1, tk, n), lambda l, i, ix: (l, i, 0))
    grid_spec = pltpu.PrefetchScalarGridSpec(
        num_scalar_prefetch=1, grid=(L, K // tk),
        in_specs=[pl.BlockSpec((1, 1, tk, n), lambda l, i, ix: (ix[0], 0, i, 0))] * L
        + [pl.BlockSpec((N_DEV, 1, tk, n), lambda l, i, ix: (0, 0, i, 0))] * L + [own, own, own],
        out_specs=(own,) * 4)
    return pl.pallas_call(
        body, name=name, grid_spec=grid_spec, out_shape=(S((L, K, n), f32),) * 4,
        compiler_params=_cp(dimension_semantics=("arbitrary", "arbitrary")))(me, *s32s, *recvs, w, m, v)


def _col_segments(n, mapping):
    segs = []
    for p in range(N_DEV):
        lo, hi = p * n, (p + 1) * n
        out = []
        for c0, c1, e0 in mapping:
            a, b = max(lo, c0), min(hi, c1)
            if a < b:
                out.append((a - lo, e0 + (a - c0), b - a))
        segs.append(out)
    return segs


def _assemble_cols(gathered, mapping, n_out, *, name):
    _, L, K, n = gathered.shape
    tk = _row_tile(K)
    segs = _col_segments(n, mapping)
    covered = sum(w for s in segs for (_, _, w) in s)

    def body(g_ref, o_ref):
        if covered != n_out:
            o_ref[...] = jnp.zeros_like(o_ref)
        for p in range(N_DEV):
            for s0, d0, w in segs[p]:
                o_ref[0, :, d0:d0 + w] = g_ref[p, 0, :, s0:s0 + w]

    return pl.pallas_call(
        body, name=name, grid=(L, K // tk), out_shape=S((L, K, n_out), gathered.dtype),
        in_specs=[pl.BlockSpec((N_DEV, 1, tk, n), lambda l, i: (0, l, i, 0))],
        out_specs=pl.BlockSpec((1, tk, n_out), lambda l, i: (l, i, 0)),
        compiler_params=_cp(dimension_semantics=("arbitrary", "arbitrary")))(gathered)


def _split_cols(dws, mapping, n, *, name):
    L = len(dws)
    K, n_in = dws[0].shape
    tk = _row_tile(K)
    segs = _col_segments(n, mapping)

    def body(*refs):
        ins, o32, o16 = refs[:L], refs[L], refs[L + 1]
        for li in range(L):
            @pl.when(pl.program_id(0) == li)
            def _(li=li):
                for p in range(N_DEV):
                    for s0, d0, w in segs[p]:
                        val = ins[li][:, d0:d0 + w]
                        o32[p, 0, :, s0:s0 + w] = val
                        o16[p, 0, :, s0:s0 + w] = _bf(val)

    out = pl.BlockSpec((N_DEV, 1, tk, n), lambda l, i: (0, l, i, 0))
    return pl.pallas_call(
        body, name=name, grid=(L, K // tk), out_shape=(S((N_DEV, L, K, n), f32), S((N_DEV, L, K, n), bf16)),
        in_specs=[pl.BlockSpec((tk, n_in), lambda l, i: (i, 0))] * L, out_specs=(out, out),
        compiler_params=_cp(dimension_semantics=("arbitrary", "arbitrary")))(*dws)


def _split_rows(dws, k, *, name):
    L = len(dws)
    N = dws[0].shape[1]

    def body(*refs):
        ins, o32, o16 = refs[:L], refs[L], refs[L + 1]
        for li in range(L):
            @pl.when(pl.program_id(0) == li)
            def _(li=li):
                val = ins[li][...]
                o32[0, 0] = val
                o16[0, 0] = _bf(val)

    out = pl.BlockSpec((1, 1, k, N), lambda l, p: (p, l, 0, 0))
    return pl.pallas_call(
        body, name=name, grid=(L, N_DEV), out_shape=(S((N_DEV, L, k, N), f32), S((N_DEV, L, k, N), bf16)),
        in_specs=[pl.BlockSpec((k, N), lambda l, p: (p, 0))] * L, out_specs=(out, out),
        compiler_params=_cp(dimension_semantics=("arbitrary", "arbitrary")))(*dws)


def _rows_full(gathered):
    _, L, k, N = gathered.shape
    return jnp.transpose(gathered, (1, 0, 2, 3)).reshape(L, N_DEV * k, N)


def _all_gather(xs, *, name):
    na = len(xs)

    def body(*refs):
        x_refs, out_refs = refs[:na], refs[na:2 * na]
        send_sems, recv_sems, local_sems = refs[2 * na:]
        mx, my, mc = lax.axis_index("x"), lax.axis_index("y"), lax.axis_index("c")
        me, sibling = (mx, my, mc), (mx, my, 1 - mc)
        chips = [(1 - mx, my), (mx, 1 - my), (1 - mx, 1 - my)]

        def slot(a, px, py, pc):
            return out_refs[a].at[4 * px + 2 * py + pc]

        def copy(a, kk, block, to, src=None):
            return pltpu.make_async_remote_copy(
                src_ref=slot(a, *block) if src is None else src, dst_ref=slot(a, *block),
                send_sem=send_sems.at[7 * a + kk], recv_sem=recv_sems.at[7 * a + kk], device_id=to, device_id_type=MESH)

        mine = [pltpu.make_async_copy(x_refs[a], slot(a, *me), local_sems.at[a]) for a in range(na)]
        for cp in mine:
            cp.start()
        first = []
        for a in range(na):
            first.append(copy(a, 0, me, sibling, src=x_refs[a]))
            first += [copy(a, 1 + j, me, (*chip, mc), src=x_refs[a]) for j, chip in enumerate(chips)]
        for cp in first:
            cp.start()
        passed = []
        for j, chip in enumerate(chips):
            for a in range(na):
                copy(a, 1 + j, (*chip, mc), me).wait_recv()
                fwd = copy(a, 4 + j, (*chip, mc), sibling)
                fwd.start()
                passed.append(fwd)
        for a in range(na):
            copy(a, 0, sibling, me).wait_recv()
            for j, chip in enumerate(chips):
                copy(a, 4 + j, (*chip, 1 - mc), me).wait_recv()
        for cp in first + passed:
            cp.wait_send()
        for cp in mine:
            cp.wait()

    return pl.pallas_call(
        body, name=name, out_shape=tuple(S((N_DEV,) + x.shape, x.dtype) for x in xs),
        in_specs=[ANY] * na, out_specs=(ANY,) * na,
        scratch_shapes=[pltpu.SemaphoreType.DMA((7 * na,)), pltpu.SemaphoreType.DMA((7 * na,)),
                        pltpu.SemaphoreType.DMA((na,))],
        compiler_params=pltpu.CompilerParams(has_side_effects=True))(*xs)


_FLIPS = [(fx, fy, fc) for fx in (0, 1) for fy in (0, 1) for fc in (0, 1)][1:]


def _flip_peer(flip):
    x, y, c = lax.axis_index("x"), lax.axis_index("y"), lax.axis_index("c")
    return tuple(1 - a if f else a for a, f in zip((x, y, c), flip))


def _dev_index(p):
    return 4 * p[0] + 2 * p[1] + p[2]


HBM_SPEC = pl.BlockSpec(memory_space=pltpu.HBM)
SEM_SPEC = pl.BlockSpec(memory_space=pltpu.SEMAPHORE)


def _direct_start(srcs, lands, per_peer, *, name):
    na = len(srcs)

    def body(*refs):
        src_refs, land_refs = refs[:na], refs[na:2 * na]
        send_sems, recv_sems = refs[2 * na], refs[2 * na + 1]
        token = refs[-1]
        me = _dev_index((lax.axis_index("x"), lax.axis_index("y"), lax.axis_index("c")))
        for a in range(na):
            for r, flip in enumerate(_FLIPS):
                peer = _flip_peer(flip)
                src = src_refs[a].at[_dev_index(peer)] if per_peer else src_refs[a]
                pltpu.make_async_remote_copy(
                    src_ref=src, dst_ref=land_refs[a].at[me], send_sem=send_sems.at[7 * a + r],
                    recv_sem=recv_sems.at[7 * a + r], device_id=peer, device_id_type=MESH).start()
        token[...] = jnp.zeros_like(token)

    hbm = lambda t: pltpu.with_memory_space_constraint(t, pltpu.HBM)
    out = pl.pallas_call(
        body, name=name,
        out_shape=(pltpu.SemaphoreType.DMA((7 * na,)), pltpu.SemaphoreType.DMA((7 * na,)))
        + tuple(pltpu.HBM(t.shape, t.dtype) for t in list(srcs) + list(lands)) + (S((8, LANE), f32),),
        in_specs=[HBM_SPEC] * (2 * na),
        out_specs=(SEM_SPEC, SEM_SPEC) + (HBM_SPEC,) * (2 * na) + (pl.BlockSpec(memory_space=pltpu.VMEM),),
        input_output_aliases={i: 2 + i for i in range(2 * na)},
        compiler_params=pltpu.CompilerParams(has_side_effects=pltpu.SideEffectType.DATAFLOW_SIDE_EFFECTING))(
            *[hbm(t) for t in srcs], *[hbm(t) for t in lands])
    return out[0], out[1], list(out[2:2 + na]), list(out[2 + na:2 + 2 * na]), out[-1]


def _direct_wait(send_sems, recv_sems, srcs, lands, per_peer, after, *, name):
    na = len(srcs)

    def body(*refs):
        src_refs, land_refs = refs[:na], refs[na:2 * na]
        ssem, rsem = refs[2 * na], refs[2 * na + 1]
        me = _dev_index((lax.axis_index("x"), lax.axis_index("y"), lax.axis_index("c")))
        for a in range(na):
            for r, flip in enumerate(_FLIPS):
                peer = _flip_peer(flip)
                src = src_refs[a].at[_dev_index(peer)] if per_peer else src_refs[a]
                cp = pltpu.make_async_remote_copy(
                    src_ref=src, dst_ref=land_refs[a].at[me], send_sem=ssem.at[7 * a + r],
                    recv_sem=rsem.at[7 * a + r], device_id=peer, device_id_type=MESH)
                cp.wait_send()
                cp.wait_recv()

    out = pl.pallas_call(
        body, name=name, out_shape=tuple(pltpu.HBM(t.shape, t.dtype) for t in list(srcs) + list(lands)),
        in_specs=[HBM_SPEC] * (2 * na) + [SEM_SPEC, SEM_SPEC, ANY], out_specs=(HBM_SPEC,) * (2 * na),
        input_output_aliases={i: i for i in range(2 * na)},
        compiler_params=pltpu.CompilerParams(has_side_effects=pltpu.SideEffectType.DATAFLOW_SIDE_EFFECTING))(
            *srcs, *lands, send_sems, recv_sems, after)
    return list(out[:na]), list(out[na:])


def _pack_small(parts, width):
    rows, offs, r = [], [], 0
    for a in parts:
        n = a.size
        nr = -(-n // width)
        flat = a.reshape(-1).astype(f32)
        if nr * width != n:
            flat = jnp.pad(flat, (0, nr * width - n))
        rows.append(flat.reshape(nr, width))
        offs.append((r, nr))
        r += nr
    buf = jnp.concatenate(rows, axis=0)
    pad = (-r) % 8
    if pad:
        buf = jnp.pad(buf, ((0, pad), (0, 0)))
    return buf, offs


def _unpack_small(buf, off, shape):
    r, nr = off
    return buf[r:r + nr].reshape(-1)[:math.prod(shape)].reshape(shape)


def _local_step(x, target, meta, a_w_in, small, start_token, late_weights, grads_ready):
    SEQ, D = x.shape
    n_meta = meta.shape[0]
    first_row = PADF - n_meta
    H = small["a_log"].shape[-1]

    head = jnp.concatenate([jnp.zeros((first_row, D), f32), meta], axis=0)

    def lanes(a):
        return jnp.pad(a.reshape(1, -1), ((0, 0), (0, LANE - a.size)))

    def after_token(a, token):
        return a if token is None else a + token[0:1, 0:1]

    alog, dtb = after_token(lanes(small["a_log"][0]), start_token), lanes(small["a_dt_bias"][0])
    a_conv, b_conv = small["a_conv"][0], small["b_conv"][0]
    nw = small["a_norm"][0].reshape(1, DH)
    lmg, lmb, lfg, lfb = small["ln_mix_g"], small["ln_mix_b"], small["ln_ffn_g"], small["ln_ffn_b"]

    h0, pre_a, z, raw, q, k, v, beta, g = _gdn_in_fwd(x, head, a_w_in, a_conv, alog, dtb, first_row=first_row, H=H)
    o, y, s_all, t_all = _delta_fwd(q, k, v, g, beta, z, nw, H=H)
    wts = late_weights(y)
    pre1, h1 = _out_res_ln(y, wts["a_w_out"], h0, lmg[0:1], lmb[0:1], first_row=first_row, name="gdn_out_ln")
    up0, act0, pre2, h2 = _ffn_fwd(h1, wts["ffn_w_up"][0], small["ffn_conv"][0], wts["ffn_w_down"][0],
                                   lfg[0:1], lfb[0:1], first_row=first_row, name="ffn_fwd0")
    proj_b, bu, pre3, h3 = _sc_fwd(h2, wts["b_w_in"], b_conv, wts["b_w_out"], lmg[1:2], lmb[1:2], first_row=first_row)
    up1, act1, pre4, h4 = _ffn_fwd(h3, wts["ffn_w_up"][1], small["ffn_conv"][1], wts["ffn_w_down"][1],
                                   lfg[1:2], lfb[1:2], first_row=first_row, name="ffn_fwd1")
    gs = {}
    dpre4, dlfg1, dlfb1, loss_tile = _loss_head(h4, target, pre4, lfg[1:2], first_row=first_row)

    def ffn_backward(dpre, up, act, h_in, layer, tag, ln_in, token=None):
        dup, dcw, dpre_in, dg, db = _ffn_bwd(
            dpre, up, wts["ffn_w_down"][layer], after_token(small["ffn_conv"][layer], token),
            wts["ffn_w_up"][layer], ln_in[0], ln_in[1], first_row=first_row, name="ffn_bwd" + tag)
        dwd = _linear_dw(act, dpre, name="dw_down" + tag)
        dwu = _linear_dw(h_in, dup, name="dw_up" + tag)
        return dpre_in, dg, db, dwu, dwd, dcw[0:3]

    dpre3, dlmg1, dlmb1, dwu1, dwd1, dcf1 = ffn_backward(dpre4, up1, act1, h3, 1, "1", (pre3, lmg[1:2]))

    dwb_out = _linear_dw(bu, dpre3, name="dw_b_out")
    dproj_b, dcb, dpre2, dlfg0, dlfb0 = _sc_bwd(dpre3, proj_b, b_conv, wts["b_w_out"], wts["b_w_in"], pre2, lfg[0:1],
                                                first_row=first_row)
    dwb_in = _linear_dw(h2, dproj_b, name="dw_b_in")
    token = grads_ready("layer1", dict(ffn_w_up=dwu1, ffn_w_down=dwd1, b_w_in=dwb_in, b_w_out=dwb_out))

    dpre1, dlmg0, dlmb0, dwu0, dwd0, dcf0 = ffn_backward(dpre2, up0, act0, h1, 0, "0", (pre1, lmg[0:1]), token)
    dy = _linear_dx(dpre1, wts["a_w_out"], name="dx_a_out")
    dwa_out = _linear_dw(y, dpre1, name="dw_a_out")
    token = grads_ready("layer0", dict(ffn_w_up=dwu0, ffn_w_down=dwd0, a_w_out=dwa_out))

    dq, dk, dv, dz, dg_, dbeta, dnw = _delta_bwd(dy, o, z, after_token(nw, token), q, k, v, g, beta, s_all, t_all, H=H)
    dproj_a, dca, dal, ddt, grad_x, dhead = _gdn_in_bwd(dq, dk, dv, dz, dg_, dbeta, pre_a, raw, a_conv, alog, dtb,
                                                        a_w_in, dpre1, first_row=first_row, H=H)
    grads_ready("last", dict(a_w_in=_linear_dw(h0, dproj_a, name="dw_a_in")))

    gs["meta"] = dhead[first_row:PADF]
    gs["a_conv"] = dca[0:a_conv.shape[0]][None]
    gs["a_log"] = dal[0:1, 0:H]
    gs["a_dt_bias"] = ddt[0:1, 0:H]
    gs["a_norm"] = dnw[0:1]
    gs["b_conv"] = dcb[0:b_conv.shape[0]][None]
    gs["ln_mix_g"] = jnp.stack([dlmg0[0], dlmg1[0]])
    gs["ln_mix_b"] = jnp.stack([dlmb0[0], dlmb1[0]])
    gs["ffn_conv"] = jnp.stack([dcf0, dcf1])
    gs["ln_ffn_g"] = jnp.stack([dlfg0[0], dlfg1[0]])
    gs["ln_ffn_b"] = jnp.stack([dlfb0[0], dlfb1[0]])
    return loss_tile, grad_x, gs


_BIG = ("a_w_in", "a_w_out", "b_w_in", "b_w_out", "ffn_w_up", "ffn_w_down")
_BIG_COL = ("a_w_in", "b_w_in", "ffn_w_up")
_SMALL = ("meta", "a_conv", "a_log", "a_dt_bias", "a_norm", "b_conv", "ln_mix_g", "ln_mix_b",
          "ffn_conv", "ln_ffn_g", "ln_ffn_b")
_SMALL_SHARDED = ("meta", "a_conv", "b_conv", "ffn_conv")
_ORDER = ("meta", "a_w_in", "a_conv", "a_log", "a_dt_bias", "a_norm", "a_w_out", "b_w_in", "b_conv", "b_w_out",
          "ln_mix_g", "ln_mix_b", "ffn_w_up", "ffn_conv", "ffn_w_down", "ln_ffn_g", "ln_ffn_b")


def _a_w_in_map(H):
    W4 = 4 * H * DH
    return [(0, W4, 0), (W4, W4 + H, W4), (W4 + H, W4 + 2 * H, W4 + LANE)], W4 + 2 * LANE


def kernel(x, meta, a_w_in, a_conv, a_log, a_dt_bias, a_norm, a_w_out, b_w_in, b_conv, b_w_out, ln_mix_g, ln_mix_b, ffn_w_up, ffn_conv, ffn_w_down, ln_ffn_g, ln_ffn_b, loss_target, m_meta, m_a_w_in, m_a_conv, m_a_log, m_a_dt_bias, m_a_norm, m_a_w_out, m_b_w_in, m_b_conv, m_b_w_out, m_ln_mix_g, m_ln_mix_b, m_ffn_w_up, m_ffn_conv, m_ffn_w_down, m_ln_ffn_g, m_ln_ffn_b, v_meta, v_a_w_in, v_a_conv, v_a_log, v_a_dt_bias, v_a_norm, v_a_w_out, v_b_w_in, v_b_conv, v_b_w_out, v_ln_mix_g, v_ln_mix_b, v_ffn_w_up, v_ffn_conv, v_ffn_w_down, v_ln_ffn_g, v_ln_ffn_b):
    wloc = dict(meta=meta, a_w_in=a_w_in, a_conv=a_conv, a_log=a_log, a_dt_bias=a_dt_bias, a_norm=a_norm,
                a_w_out=a_w_out, b_w_in=b_w_in, b_conv=b_conv, b_w_out=b_w_out, ln_mix_g=ln_mix_g, ln_mix_b=ln_mix_b,
                ffn_w_up=ffn_w_up, ffn_conv=ffn_conv, ffn_w_down=ffn_w_down, ln_ffn_g=ln_ffn_g, ln_ffn_b=ln_ffn_b)
    mloc = dict(meta=m_meta, a_w_in=m_a_w_in, a_conv=m_a_conv, a_log=m_a_log, a_dt_bias=m_a_dt_bias, a_norm=m_a_norm,
                a_w_out=m_a_w_out, b_w_in=m_b_w_in, b_conv=m_b_conv, b_w_out=m_b_w_out, ln_mix_g=m_ln_mix_g,
                ln_mix_b=m_ln_mix_b, ffn_w_up=m_ffn_w_up, ffn_conv=m_ffn_conv, ffn_w_down=m_ffn_w_down,
                ln_ffn_g=m_ln_ffn_g, ln_ffn_b=m_ln_ffn_b)
    vloc = dict(meta=v_meta, a_w_in=v_a_w_in, a_conv=v_a_conv, a_log=v_a_log, a_dt_bias=v_a_dt_bias, a_norm=v_a_norm,
                a_w_out=v_a_w_out, b_w_in=v_b_w_in, b_conv=v_b_conv, b_w_out=v_b_w_out, ln_mix_g=v_ln_mix_g,
                ln_mix_b=v_ln_mix_b, ffn_w_up=v_ffn_w_up, ffn_conv=v_ffn_conv, ffn_w_down=v_ffn_w_down,
                ln_ffn_g=v_ln_ffn_g, ln_ffn_b=v_ln_ffn_b)
    H = a_log.shape[-1]
    mx, my, mc = lax.axis_index("x"), lax.axis_index("y"), lax.axis_index("c")
    me = 4 * mx + 2 * my + mc

    a_map, a_cols = _a_w_in_map(H)
    col_maps = {"a_w_in": (a_map, a_cols)}
    for n in ("b_w_in", "ffn_w_up"):
        ncols = N_DEV * wloc[n].shape[-1]
        col_maps[n] = ([(0, ncols, 0)], ncols)
    sm_sh = [wloc[n] for n in _SMALL_SHARDED]
    sbuf, soffs = _pack_small(sm_sh, 128)
    g_a_w_in, sg = _all_gather([_bf(wloc["a_w_in"]), sbuf], name="gather_first")
    w_a_in = _assemble_cols(g_a_w_in, *col_maps["a_w_in"], name="assemble_a_w_in")[0]
    late = [n for n in _BIG if n != "a_w_in"]
    ssem, rsem, srcs_t, lands_t, start_token = _direct_start(
        [_bf(wloc[n]) for n in late], [lax.empty((N_DEV,) + wloc[n].shape, bf16) for n in late], False,
        name="gather_rest_start")

    def late_weights(after):
        srcs_d, landed = _direct_wait(ssem, rsem, srcs_t, lands_t, False, after, name="gather_rest_wait")
        wts = {}
        for n, own, got in zip(late, srcs_d, landed):
            full = lax.dynamic_update_index_in_dim(got, own, me, 0)
            if n in _BIG_COL:
                wts[n] = _assemble_cols(full, *col_maps[n], name="assemble_" + n)
            else:
                wts[n] = _rows_full(full)
        for n in ("a_w_out", "b_w_in", "b_w_out"):
            wts[n] = wts[n][0]
        return wts

    small = {n: wloc[n] for n in _SMALL}
    for n, off in zip(_SMALL_SHARDED, soffs):
        sh = wloc[n].shape
        parts = jnp.stack([_unpack_small(sg[d], off, sh) for d in range(N_DEV)])
        nd = len(sh)
        small[n] = jnp.transpose(parts, tuple(range(1, nd)) + (0, nd)).reshape(sh[:-1] + (N_DEV * sh[-1],))

    def split(n, dws, tag):
        if n in _BIG_COL:
            return _split_cols(dws, col_maps[n][0], wloc[n].shape[-1], name="split_" + n + tag)
        return _split_rows(dws, wloc[n].shape[-2], name="split_" + n + tag)

    sent = {}

    def grads_ready(stage, grads):
        names = sorted(grads)
        parts = [split(n, [grads[n]], "_" + stage) for n in names]
        handles = _direct_start([p[1] for p in parts], [jnp.zeros(p[1].shape, bf16) for p in parts], True,
                                name="grads_" + stage + "_start")
        sent[stage] = (names, [p[0] for p in parts], handles)
        return handles[4]

    loss_tile, grad_x, gs = _local_step(x[0], loss_target[0], small["meta"], w_a_in, small, start_token,
                                        late_weights, grads_ready)

    def landed(stage, after):
        names, own32, (ssem_g, rsem_g, srcs_g, lands_g, _) = sent[stage]
        _, got = _direct_wait(ssem_g, rsem_g, srcs_g, lands_g, True, after, name="grads_" + stage + "_wait")
        return list(zip(names, own32, got))

    parts = {}
    for stage in ("layer0", "layer1"):
        for n, o32, r in landed(stage, grad_x):
            parts.setdefault(n, []).append((o32, r))
    me1 = jnp.stack([me]).astype(jnp.int32)
    big_out = {n: _adamw_direct([p[0] for p in ps], [p[1] for p in ps], wloc[n], mloc[n], vloc[n], me1,
                                name="adamw_" + n) for n, ps in parts.items()}
    (n, o32, r), = landed("last", big_out["ffn_w_up"][0])
    big_out[n] = _adamw_direct([o32], [r], wloc[n], mloc[n], vloc[n], me1, name="adamw_" + n)

    names = list(_SMALL)
    pbuf, poffs = _pack_small([gs[n] for n in names] + [loss_tile[0:1, 0:1]], 1024)
    psum = _sum_devices(_all_gather([pbuf], name="gather_small_grads")[0])
    loss = psum[poffs[-1][0], 0]
    g_small = {}
    for n, off in zip(names, poffs[:-1]):
        full_shape = gs[n].shape
        gfull = _unpack_small(psum, off, full_shape)
        if n in _SMALL_SHARDED:
            ns = wloc[n].shape[-1]
            gfull = lax.dynamic_slice_in_dim(gfull, me * ns, ns, axis=gfull.ndim - 1)
        g_small[n] = gfull.reshape(wloc[n].shape)
    gbuf, aoffs = _pack_small([g_small[n] for n in names], 128)
    wbuf, _ = _pack_small([wloc[n] for n in names], 128)
    mbuf, _ = _pack_small([mloc[n] for n in names], 128)
    vbuf, _ = _pack_small([vloc[n] for n in names], 128)
    _, d_s, m_s, v_s = _adamw([gbuf], wbuf, mbuf, vbuf, name="adamw_small")

    grads, deltas, new_m, new_v = {}, {}, {}, {}
    for n in _BIG:
        grads[n], deltas[n], new_m[n], new_v[n] = big_out[n]
    for n, off in zip(names, aoffs):
        sh = wloc[n].shape
        grads[n] = g_small[n]
        deltas[n], new_m[n], new_v[n] = (_unpack_small(b_, off, sh) for b_ in (d_s, m_s, v_s))
    return (loss, grad_x[None], *[grads[n] for n in _ORDER], *[deltas[n] for n in _ORDER],
            *[new_m[n] for n in _ORDER], *[new_v[n] for n in _ORDER])
```

```python
import math

import jax
import jax.numpy as jnp
from jax import lax
from jax.experimental import pallas as pl
from jax.experimental.pallas import tpu as pltpu

f32, bf16 = jnp.float32, jnp.bfloat16
S = jax.ShapeDtypeStruct
HI = lax.Precision.HIGHEST
HI3 = lax.Precision.HIGH
MESH = pl.DeviceIdType.MESH

V7X_VMEM_LIMIT = 56 * 1024 * 1024
LANE = 128
DH = 128
CH = 64
PADF = 256
TM = 256
TMM = 768
N_DEV = 8
BWD_HEAD_GROUP = 4

DEPTH = 2
ALPHA = (2.0 * DEPTH) ** 0.25
LN_EPS = 1e-5
RMS_EPS = 1e-6
L2_EPS = 1e-6
ADAM_LR, ADAM_B1, ADAM_B2, ADAM_EPS, ADAM_WD, ADAM_STEP = 0.001, 0.9, 0.999, 1e-08, 0.01, 10


def _cp(**kw):
    return pltpu.CompilerParams(vmem_limit_bytes=V7X_VMEM_LIMIT, **kw)


def _bf(x):
    return x.astype(bf16)


def _dot(a, b, precision=None):
    return jnp.dot(a, b, preferred_element_type=f32, precision=precision)


def _dot_nt(a, b):
    return lax.dot_general(a, b, (((1,), (1,)), ((), ())), preferred_element_type=f32)


def _dot_tn(a, b):
    return lax.dot_general(a, b, (((0,), (0,)), ((), ())), preferred_element_type=f32)


def _sigmoid(x):
    return 1.0 / (1.0 + jnp.exp(-x))


def _load_once(pairs, sem):
    @pl.when(pl.program_id(0) == 0)
    def _():
        cps = [pltpu.make_async_copy(src, dst, sem.at[n]) for n, (src, dst) in enumerate(pairs)]
        for c in cps:
            c.start()
        for c in cps:
            c.wait()


def _row_ids(i, tm, width):
    return i * tm + lax.broadcasted_iota(jnp.int32, (tm, width), 0)


def _ln_fwd(pre, g, b, rows, first_row):
    mu = jnp.mean(pre, axis=-1, keepdims=True)
    xc = pre - mu
    var = jnp.mean(xc * xc, axis=-1, keepdims=True)
    y = xc * lax.rsqrt(var + LN_EPS) * g + b
    return jnp.where(rows >= first_row, y, 0.0)


ANY = pl.BlockSpec(memory_space=pl.ANY)


def _taps_back(prev8, x, kw):
    xe = jnp.concatenate([prev8, x], axis=0)
    return [pltpu.roll(xe, kw - 1 - j, 0)[8:] for j in range(kw - 1)] + [x]


def _taps_ahead(x, next8, kw):
    n = x.shape[0]
    xe = jnp.concatenate([x, next8], axis=0)
    return [pltpu.roll(xe, n + 8 - (kw - 1 - j), 0)[:n] for j in range(kw - 1)] + [x]


def _conv(cw, taps):
    acc = cw[0:1, :] * taps[0]
    for j in range(1, len(taps)):
        acc = acc + cw[j:j + 1, :] * taps[j]
    return acc


def _linear_dw(x, dy, *, name):
    L, K = x.shape
    N = dy.shape[1]
    tm = TMM if L % TMM == 0 else TM
    tn = LANE
    for d in range(N // LANE, 0, -1):
        if (N // LANE) % d == 0 and K * d * LANE * 4 <= 9 * 1024 * 1024:
            tn = d * LANE
            break

    def body(x_ref, dy_ref, o_ref):
        @pl.when(pl.program_id(1) == 0)
        def _():
            o_ref[...] = jnp.zeros_like(o_ref)
        o_ref[...] += _dot_tn(_bf(x_ref[...]), _bf(dy_ref[...]))

    return pl.pallas_call(
        body, name=name, grid=(N // tn, L // tm), out_shape=S((K, N), f32),
        in_specs=[pl.BlockSpec((tm, K), lambda j, i: (i, 0)), pl.BlockSpec((tm, tn), lambda j, i: (i, j))],
        out_specs=pl.BlockSpec((K, tn), lambda j, i: (0, j)),
        compiler_params=_cp(dimension_semantics=("arbitrary", "arbitrary")))(x, dy)


def _ln_bwd_rows(dout, pre, g, rows, first_row):
    mu = jnp.mean(pre, axis=-1, keepdims=True)
    xc = pre - mu
    rstd = lax.rsqrt(jnp.mean(xc * xc, axis=-1, keepdims=True) + LN_EPS)
    xh = xc * rstd
    dy = jnp.where(rows >= first_row, dout, 0.0)
    dxh = dy * g
    dpre = rstd * (dxh - jnp.mean(dxh, axis=-1, keepdims=True) - xh * jnp.mean(dxh * xh, axis=-1, keepdims=True))
    return dpre, jnp.sum(dy * xh, axis=0, keepdims=True), jnp.sum(dy, axis=0, keepdims=True)


def _gdn_in_fwd(x, head, w_full, conv_w, alog, dtb, *, first_row, H):
    D = x.shape[1]
    L = PADF + x.shape[0]
    W = H * DH
    NW = w_full.shape[1]
    KW = conv_w.shape[0]
    tm = TM
    pb = PADF // tm

    def body(x_ref, head_ref, w_hbm, cw_ref, alog_ref, dtb_ref,
             h_ref, pre_ref, z_ref, raw_ref, q_ref, k_ref, v_ref, beta_ref, g_ref,
             w_vmem, carry, sem):
        i = pl.program_id(0)
        _load_once([(w_hbm, w_vmem)], sem)

        @pl.when(i == 0)
        def _():
            carry[...] = jnp.zeros_like(carry)

        hv = jnp.where(i < pb, head_ref[...], x_ref[...])
        h_ref[...] = hv
        hb = _bf(hv)
        outs = (q_ref, k_ref, v_ref)
        for s in range(3):
            pre = _dot(hb, w_vmem[:, s * W:(s + 1) * W])
            pre_ref[:, s * W:(s + 1) * W] = pre
            c = _conv(cw_ref[:, s * W:(s + 1) * W], _taps_back(carry[s], pre, KW))
            carry[s] = pre[tm - 8:tm, :]
            sl = c * _sigmoid(c)
            if s < 2:
                scale = DH ** -0.5 if s == 0 else 1.0
                for hh in range(H):
                    seg = sl[:, hh * DH:(hh + 1) * DH]
                    r = lax.rsqrt(jnp.sum(seg * seg, axis=-1, keepdims=True) + L2_EPS)
                    outs[s][:, hh * DH:(hh + 1) * DH] = seg * (r * scale)
            else:
                v_ref[...] = sl
        z_ref[...] = _dot(hb, w_vmem[:, 3 * W:4 * W])
        raw = _dot(hb, w_vmem[:, 4 * W:4 * W + 2 * LANE])
        raw_ref[...] = raw
        ok = (_row_ids(i, tm, LANE) >= first_row) & (lax.broadcasted_iota(jnp.int32, (tm, LANE), 1) < H)
        beta_ref[...] = jnp.where(ok, _sigmoid(raw[:, :LANE]), 0.0)
        a = raw[:, LANE:] + dtb_ref[...]
        sp = jnp.maximum(a, 0.0) + jnp.log(1.0 + jnp.exp(-jnp.abs(a)))
        gv = jnp.where(ok, -jnp.exp(alog_ref[...]) * sp, 0.0)
        g_ref[...] = _dot(_chunk_tri(tm, lower=True), gv, HI)

    row = lambda i: (i, 0)
    fix = lambda i: (0, 0)
    out_shape = (S((L, D), f32), S((L, 3 * W), f32), S((L, W), f32), S((L, 2 * LANE), f32),
                 S((L, W), f32), S((L, W), f32), S((L, W), f32), S((L, LANE), f32), S((L, LANE), f32))
    out_specs = (pl.BlockSpec((tm, D), row),
                 pl.BlockSpec((tm, 3 * W), row), pl.BlockSpec((tm, W), row), pl.BlockSpec((tm, 2 * LANE), row),
                 pl.BlockSpec((tm, W), row), pl.BlockSpec((tm, W), row), pl.BlockSpec((tm, W), row),
                 pl.BlockSpec((tm, LANE), row), pl.BlockSpec((tm, LANE), row))
    return pl.pallas_call(
        body, name="gdn_in_fwd", grid=(L // tm,), out_shape=out_shape,
        in_specs=[pl.BlockSpec((tm, D), lambda i: (jnp.maximum(i - pb, 0), 0)),
                  pl.BlockSpec((tm, D), lambda i: (jnp.minimum(i, pb - 1), 0)), ANY, pl.BlockSpec((KW, 3 * W), fix),
                  pl.BlockSpec((1, LANE), fix), pl.BlockSpec((1, LANE), fix)],
        out_specs=out_specs,
        scratch_shapes=[pltpu.VMEM((D, NW), w_full.dtype), pltpu.VMEM((3, 8, W), f32), pltpu.SemaphoreType.DMA((1,))],
        compiler_params=_cp(dimension_semantics=("arbitrary",)))(x, head, w_full, conv_w, alog, dtb)


def _gdn_in_bwd(dq, dk, dv, dz, dg, dbeta, pre, raw, conv_w, alog, dtb, w_full, res, *, first_row, H):
    L = dq.shape[0]
    D = res.shape[1]
    W = H * DH
    KW = conv_w.shape[0]
    tm = TM
    nb = L // tm
    NW = 4 * W + 2 * LANE
    fb = PADF // tm
    alpha = ALPHA

    def body(dq_ref, dk_ref, dv_ref, dz_ref, dg_ref, dbeta_ref, pre_ref, hq_ref, hk_ref, hv_ref, raw_ref,
             cw_ref, alog_ref, dtb_ref, w_hbm, res_ref,
             dproj_ref, dcw_ref, dal_ref, ddt_ref, dx_ref, dfront_ref, w_vmem, carry, tmp, sem):
        i = pl.program_id(0)
        blk = nb - 1 - i
        _load_once([(w_hbm, w_vmem)], sem)

        @pl.when(i == 0)
        def _():
            carry[...] = jnp.zeros_like(carry)
            dcw_ref[...] = jnp.zeros_like(dcw_ref)
            dal_ref[...] = jnp.zeros_like(dal_ref)
            ddt_ref[...] = jnp.zeros_like(ddt_ref)

        halos = (hq_ref, hk_ref, hv_ref)
        douts = (dq_ref, dk_ref, dv_ref)
        for s in range(3):
            sec = slice(s * W, (s + 1) * W)
            pre = pre_ref[:, sec]
            c = _conv(cw_ref[:, sec], _taps_back(jnp.where(blk > 0, halos[s][...], 0.0), pre, KW))
            sig = _sigmoid(c)
            sl = c * sig
            if s < 2:
                scale = DH ** -0.5 if s == 0 else 1.0
                for hh in range(H):
                    hs = slice(hh * DH, (hh + 1) * DH)
                    seg = sl[:, hs]
                    r = lax.rsqrt(jnp.sum(seg * seg, axis=-1, keepdims=True) + L2_EPS)
                    n = seg * r
                    dqs = douts[s][:, hs]
                    tmp[:, hs] = (scale * r) * (dqs - n * jnp.sum(n * dqs, axis=-1, keepdims=True))
                dsl = tmp[...]
            else:
                dsl = dv_ref[...]
            dc = dsl * (sig * (1.0 + c * (1.0 - sig)))
            ahead = _taps_ahead(dc, carry[s], KW)
            carry[s] = dc[0:8, :]
            dproj_ref[:, sec] = _bf(_conv(cw_ref[:, sec], ahead))
            for j in range(KW):
                dcw_ref[j:j + 1, sec] += jnp.sum(ahead[j] * pre, axis=0, keepdims=True)
        dproj_ref[:, 3 * W:4 * W] = _bf(dz_ref[...])
        raw_v = raw_ref[...]
        ok = (_row_ids(blk, tm, LANE) >= first_row) & (lax.broadcasted_iota(jnp.int32, (tm, LANE), 1) < H)
        beta = _sigmoid(raw_v[:, :LANE])
        dbraw = jnp.where(ok, dbeta_ref[...] * beta * (1.0 - beta), 0.0)
        a = raw_v[:, LANE:] + dtb_ref[...]
        sp = jnp.maximum(a, 0.0) + jnp.log(1.0 + jnp.exp(-jnp.abs(a)))
        nea = -jnp.exp(alog_ref[...])
        dgm = jnp.where(ok, _dot(_chunk_tri(tm, lower=False), dg_ref[...], HI), 0.0)
        daraw = dgm * nea * _sigmoid(a)
        dal_ref[0:1, :] += jnp.sum(dgm * nea * sp, axis=0, keepdims=True)
        ddt_ref[0:1, :] += jnp.sum(daraw, axis=0, keepdims=True)
        dproj_ref[:, 4 * W:4 * W + LANE] = _bf(dbraw)
        dproj_ref[:, 4 * W + LANE:4 * W + 2 * LANE] = _bf(daraw)
        dh = alpha * res_ref[...] + _dot_nt(dproj_ref[...], w_vmem[...])

        @pl.when(blk >= fb)
        def _():
            dx_ref[...] = dh

        @pl.when(blk < fb)
        def _():
            dfront_ref[...] = dh

    rev = lambda i: (nb - 1 - i, 0)
    fix = lambda i: (0, 0)

    def halo(col):
        return pl.BlockSpec((8, W), lambda i: (jnp.maximum((nb - 1 - i) * (tm // 8) - 1, 0), col))

    return pl.pallas_call(
        body, name="gdn_in_bwd", grid=(nb,),
        out_shape=(S((L, NW), bf16), S((8, 3 * W), f32), S((8, LANE), f32), S((8, LANE), f32),
                   S((L - PADF, D), f32), S((PADF, D), f32)),
        in_specs=[pl.BlockSpec((tm, W), rev)] * 4 + [pl.BlockSpec((tm, LANE), rev)] * 2
        + [pl.BlockSpec((tm, 3 * W), rev), halo(0), halo(1), halo(2), pl.BlockSpec((tm, 2 * LANE), rev),
           pl.BlockSpec((KW, 3 * W), fix), pl.BlockSpec((1, LANE), fix), pl.BlockSpec((1, LANE), fix),
           ANY, pl.BlockSpec((tm, D), rev)],
        out_specs=(pl.BlockSpec((tm, NW), rev), pl.BlockSpec((8, 3 * W), fix),
                   pl.BlockSpec((8, LANE), fix), pl.BlockSpec((8, LANE), fix),
                   pl.BlockSpec((tm, D), lambda i: (jnp.maximum(nb - 1 - i - fb, 0), 0)),
                   pl.BlockSpec((tm, D), lambda i: (jnp.minimum(nb - 1 - i, fb - 1), 0))),
        scratch_shapes=[pltpu.VMEM((D, NW), w_full.dtype), pltpu.VMEM((3, 8, W), f32), pltpu.VMEM((tm, W), f32),
                        pltpu.SemaphoreType.DMA((1,))],
        compiler_params=_cp(dimension_semantics=("arbitrary",)))(
            dq, dk, dv, dz, dg, dbeta, pre, pre, pre, pre, raw, conv_w, alog, dtb, w_full, res)


def _chunk_tri(n, lower):
    i = lax.broadcasted_iota(jnp.int32, (n, n), 0)
    j = lax.broadcasted_iota(jnp.int32, (n, n), 1)
    sh = int(math.log2(CH))
    same = lax.shift_right_logical(i, sh) == lax.shift_right_logical(j, sh)
    return (same & ((i >= j) if lower else (j >= i))).astype(f32)


def _tri_inv_many(ms, eye):
    ts = [eye - m for m in ms]
    ps = list(ms)
    for _ in range(int(math.log2(CH)) - 1):
        ps = [_dot(p, p, HI3) for p in ps]
        ts = [t + _dot(t, p, HI3) for t, p in zip(ts, ps)]
    return ts


def _chunk_local(q, k, v, gcol, grow, glast, bcol, ii, jj):
    dec = jnp.where(ii >= jj, jnp.exp(jnp.minimum(gcol - grow, 0.0)), 0.0)
    eg = jnp.exp(gcol)
    kb = k * bcol
    kbg = kb * eg
    vb = v * bcol
    qt = q * eg
    kt = k * jnp.exp(glast - gcol)
    kk = _dot_nt(_bf(kb), _bf(k))
    qk = _dot_nt(_bf(q), _bf(k))
    return dec, eg, kb, kbg, vb, qt, kt, kk, qk


def _delta_fwd(q, k, v, g, beta, z, nw, h, w_out, ln_g, ln_b, *, first_row, H):
    L = q.shape[0]
    W = H * DH
    D = h.shape[1]
    rb = TM
    nc = rb // CH
    nblk = L // rb
    alpha = ALPHA

    def body(q_ref, k_ref, v_ref, g_ref, b_ref, z_ref, nw_ref, h_ref, wout_hbm, lg_ref, lb_ref,
             o_ref, y_ref, s_out, t_out, pre_ref, out_ref, s_scr, wout, sem):
        _load_once([(wout_hbm, wout)], sem)

        @pl.when(pl.program_id(0) == 0)
        def _():
            s_scr[...] = jnp.zeros_like(s_scr)

        ii = lax.broadcasted_iota(jnp.int32, (CH, CH), 0)
        jj = lax.broadcasted_iota(jnp.int32, (CH, CH), 1)
        eye = (ii == jj).astype(f32)
        nwv = nw_ref[...]

        heads = range(H)
        hsl = [slice(hh * DH, (hh + 1) * DH) for hh in heads]

        def chunk(c, carry):
            r0 = pl.multiple_of(c * CH, CH)
            rows = pl.ds(r0, CH)
            gam = g_ref[rows, :]
            gam_t = gam.T
            bb = b_ref[rows, :]
            glast = [gam[CH - 1:CH, hh:hh + 1] for hh in heads]
            loc = [_chunk_local(q_ref[rows, hsl[hh]], k_ref[rows, hsl[hh]], v_ref[rows, hsl[hh]],
                                gam[:, hh:hh + 1], gam_t[hh:hh + 1, :], glast[hh], bb[:, hh:hh + 1], ii, jj)
                   for hh in heads]
            st = [s_scr[hh] for hh in heads]
            zs = [z_ref[rows, hsl[hh]] for hh in heads]
            ts = _tri_inv_many([jnp.where(ii > jj, l[7] * l[0], 0.0) for l in loc], eye)
            us = [_dot(t, l[4], HI3) for t, l in zip(ts, loc)]
            ws = [_dot(t, l[3], HI3) for t, l in zip(ts, loc)]
            stb = [_bf(s) for s in st]
            vn = [u - _dot(_bf(w), sb) for u, w, sb in zip(us, ws, stb)]
            vnb = [_bf(x) for x in vn]
            snew = [s * jnp.exp(gl) + _dot_tn(_bf(l[6]), xb) for s, gl, l, xb in zip(st, glast, loc, vnb)]
            os_ = [_dot(_bf(l[5]), sb) + _dot(_bf(l[8] * l[0]), xb) for l, sb, xb in zip(loc, stb, vnb)]
            for hh in heads:
                o = os_[hh]
                s_out[c, hh] = st[hh]
                t_out[c, hh] = ts[hh]
                s_scr[hh] = snew[hh]
                o_ref[rows, hsl[hh]] = o
                on = o * lax.rsqrt(jnp.mean(o * o, axis=-1, keepdims=True) + RMS_EPS) * nwv
                y_ref[rows, hsl[hh]] = _bf(on * (zs[hh] * _sigmoid(zs[hh])))
            return carry

        lax.fori_loop(0, nc, chunk, 0)
        pre = alpha * h_ref[...] + _dot(y_ref[...], wout[...])
        pre_ref[...] = pre
        out_ref[...] = _ln_fwd(pre, lg_ref[...], lb_ref[...], _row_ids(pl.program_id(0), rb, D), first_row)

    row = lambda i: (i, 0)
    fix = lambda i: (0, 0)
    return pl.pallas_call(
        body, name="delta_fwd", grid=(nblk,),
        out_shape=(S((L, W), f32), S((L, W), bf16), S((L // CH, H, DH, DH), f32), S((L // CH, H, CH, CH), f32),
                   S((L, D), f32), S((L, D), f32)),
        in_specs=[pl.BlockSpec((rb, W), row)] * 3 + [pl.BlockSpec((rb, LANE), row)] * 2
        + [pl.BlockSpec((rb, W), row), pl.BlockSpec((1, DH), fix), pl.BlockSpec((rb, D), row), ANY,
           pl.BlockSpec((1, D), fix), pl.BlockSpec((1, D), fix)],
        out_specs=(pl.BlockSpec((rb, W), row), pl.BlockSpec((rb, W), row),
                   pl.BlockSpec((nc, H, DH, DH), lambda i: (i, 0, 0, 0)),
                   pl.BlockSpec((nc, H, CH, CH), lambda i: (i, 0, 0, 0)),
                   pl.BlockSpec((rb, D), row), pl.BlockSpec((rb, D), row)),
        scratch_shapes=[pltpu.VMEM((H, DH, DH), f32), pltpu.VMEM((W, D), w_out.dtype), pltpu.SemaphoreType.DMA((1,))],
        compiler_params=_cp(dimension_semantics=("arbitrary",)))(q, k, v, g, beta, z, nw, h, w_out, ln_g, ln_b)


def _delta_bwd(dpre, w_out, o, z, nw, q, k, v, g, beta, s_all, t_all, *, H):
    L = q.shape[0]
    W = H * DH
    D = dpre.shape[1]
    rb = TM
    nc = rb // CH
    nblk = L // rb

    def body(dpre_ref, wout_hbm, o_ref, z_ref, nw_ref, q_ref, k_ref, v_ref, g_ref, b_ref, s_ref, t_ref,
             dq_ref, dk_ref, dv_ref, dz_ref, dg_ref, db_ref, dnw_ref, ds_scr, wout, dy_scr, sem):
        _load_once([(wout_hbm, wout)], sem)

        @pl.when(pl.program_id(0) == 0)
        def _():
            ds_scr[...] = jnp.zeros_like(ds_scr)
            dnw_ref[...] = jnp.zeros_like(dnw_ref)

        dy_scr[...] = _dot_nt(_bf(dpre_ref[...]), wout[...])

        ii = lax.broadcasted_iota(jnp.int32, (CH, CH), 0)
        jj = lax.broadcasted_iota(jnp.int32, (CH, CH), 1)
        lane = lax.broadcasted_iota(jnp.int32, (CH, LANE), 1)
        last_row = lax.broadcasted_iota(jnp.int32, (CH, 1), 0) == CH - 1
        nwv = nw_ref[...]

        def chunk(cc, carry):
            c = nc - 1 - cc
            r0 = pl.multiple_of(c * CH, CH)
            rows = pl.ds(r0, CH)
            gam = g_ref[rows, :]
            gam_t = gam.T
            bb = b_ref[rows, :]

            def head(hh):
                hs = slice(hh * DH, (hh + 1) * DH)
                gcol, grow, glast = gam[:, hh:hh + 1], gam_t[hh:hh + 1, :], gam[CH - 1:CH, hh:hh + 1]
                bcol = bb[:, hh:hh + 1]
                qh, kh, vh = q_ref[rows, hs], k_ref[rows, hs], v_ref[rows, hs]
                oh, zh, dyh = o_ref[rows, hs], z_ref[rows, hs], dy_scr[rows, hs]
                t = t_ref[c, hh]
                st = s_ref[c, hh]
                dsn = ds_scr[hh]
                rms = lax.rsqrt(jnp.mean(oh * oh, axis=-1, keepdims=True) + RMS_EPS)
                on = oh * rms
                sig = _sigmoid(zh)
                sz = zh * sig
                dz_ref[rows, hs] = dyh * on * nwv * (sig * (1.0 + zh * (1.0 - sig)))
                dnw = jnp.sum(dyh * on * sz, axis=0, keepdims=True)
                don = dyh * nwv * sz
                do = rms * (don - on * jnp.mean(don * on, axis=-1, keepdims=True))
                dec, eg, kb, kbg, vb, qt, kt, kk, qk = _chunk_local(qh, kh, vh, gcol, grow, glast, bcol, ii, jj)
                stb, dsnb, dob, tb, kbgb = _bf(st), _bf(dsn), _bf(do), _bf(t), _bf(kbg)
                r = vb - _dot(kbgb, stb)
                dqt = _dot_nt(dob, stb)
                ds_new = _dot_tn(_bf(qt), dob)
                mm = jnp.where(ii > jj, kk * dec, 0.0)
                attn = qk * dec
                yield
                vn = _dot(t, r, HI3)
                dvn = _dot_tn(_bf(attn), dob) + _dot(_bf(kt), dsnb)
                egl = jnp.exp(glast)
                ekt = jnp.exp(glast - gcol)
                yield
                vnb, dvnb = _bf(vn), _bf(dvn)
                dattn = jnp.where(ii >= jj, _dot_nt(dob, vnb), 0.0)
                dkt = _dot_nt(vnb, dsnb)
                dvb = _dot_tn(tb, dvnb)
                dt = _dot_nt(dvnb, _bf(r))
                dglast = egl * jnp.sum(jnp.sum(dsn * st, axis=0, keepdims=True), axis=1, keepdims=True)
                yield
                dvbb = _bf(dvb)
                dv_ref[rows, hs] = dvb * bcol
                ds_scr[hh] = ds_new + egl * dsn - _dot_tn(kbgb, dvbb)
                dkbg = -_dot_nt(dvbb, stb)
                x = _dot_nt(_bf(dt), tb)
                yield
                dm = jnp.where(ii > jj, -_dot_tn(tb, _bf(x)), 0.0)
                dkk = dm * dec
                dqk = dattn * dec
                e = dm * mm + dattn * attn
                dgam = jnp.sum(e, axis=1, keepdims=True) - jnp.sum(e.T, axis=1, keepdims=True)
                dkkb, dqkb, kbf = _bf(dkk), _bf(dqk), _bf(kh)
                dkb = _dot(dkkb, kbf) + dkbg * eg
                dk_ref[rows, hs] = _dot_tn(dkkb, _bf(kb)) + _dot_tn(dqkb, _bf(qh)) + dkt * ekt + dkb * bcol
                dq_ref[rows, hs] = _dot(dqkb, kbf) + dqt * eg
                yield
                dktkt = dkt * kt
                dgam = dgam + jnp.sum(dqt * qt - dktkt + dkbg * kbg, axis=1, keepdims=True)
                dglast = dglast + jnp.sum(jnp.sum(dktkt, axis=0, keepdims=True), axis=1, keepdims=True)
                dgam = dgam + jnp.where(last_row, dglast, 0.0)
                dbeta = jnp.sum(dkb * kh + dvb * vh, axis=1, keepdims=True)
                return dgam, dbeta, dnw

            res = [None] * H
            for h0 in range(0, H, BWD_HEAD_GROUP):
                group = range(h0, min(h0 + BWD_HEAD_GROUP, H))
                gens = {hh: head(hh) for hh in group}
                while any(res[hh] is None for hh in group):
                    for hh in group:
                        try:
                            next(gens[hh])
                        except StopIteration as stop:
                            res[hh] = stop.value
            dgam_all = jnp.zeros((CH, LANE), f32)
            dbeta_all = jnp.zeros((CH, LANE), f32)
            dnw_acc = jnp.zeros((1, DH), f32)
            for hh in range(H):
                dgam, dbeta, dnw = res[hh]
                dgam_all = dgam_all + jnp.where(lane == hh, dgam, 0.0)
                dbeta_all = dbeta_all + jnp.where(lane == hh, dbeta, 0.0)
                dnw_acc = dnw_acc + dnw
            dg_ref[rows, :] = dgam_all
            db_ref[rows, :] = dbeta_all
            dnw_ref[0:1, :] += dnw_acc
            return carry

        lax.fori_loop(0, nc, chunk, 0)

    rev = lambda i: (nblk - 1 - i, 0)
    rev4 = lambda i: (nblk - 1 - i, 0, 0, 0)
    fix = lambda i: (0, 0)
    wide = pl.BlockSpec((rb, W), rev)
    thin = pl.BlockSpec((rb, LANE), rev)
    return pl.pallas_call(
        body, name="delta_bwd", grid=(nblk,),
        out_shape=(S((L, W), f32),) * 4 + (S((L, LANE), f32),) * 2 + (S((8, DH), f32),),
        in_specs=[pl.BlockSpec((rb, D), rev), ANY, wide, wide, pl.BlockSpec((1, DH), fix), wide, wide, wide, thin, thin,
                  pl.BlockSpec((nc, H, DH, DH), rev4), pl.BlockSpec((nc, H, CH, CH), rev4)],
        out_specs=(wide,) * 4 + (thin, thin, pl.BlockSpec((8, DH), fix)),
        scratch_shapes=[pltpu.VMEM((H, DH, DH), f32), pltpu.VMEM((W, D), w_out.dtype), pltpu.VMEM((rb, W), f32),
                        pltpu.SemaphoreType.DMA((1,))],
        compiler_params=_cp(dimension_semantics=("arbitrary",)))(
            dpre, w_out, o, z, nw, q, k, v, g, beta, s_all, t_all)


def _sc_fwd(h, w_in, conv_w, w_out, g, b, *, first_row):
    L, D = h.shape
    W = w_out.shape[0]
    KW = conv_w.shape[0]
    tm = TM
    alpha = ALPHA

    def body(h_ref, win_hbm, cw_ref, wout_hbm, g_ref, b_ref, proj_ref, bu_ref, pre_ref, out_ref,
             win, wout, carry, sem):
        i = pl.program_id(0)
        _load_once([(win_hbm, win), (wout_hbm, wout)], sem)

        @pl.when(i == 0)
        def _():
            carry[...] = jnp.zeros_like(carry)

        hv = h_ref[...]
        hb = _bf(hv)
        bg = _dot(hb, win[:, 0:W])
        cg = _dot(hb, win[:, W:2 * W])
        xv = _dot(hb, win[:, 2 * W:3 * W])
        proj_ref[:, 0:W] = bg
        proj_ref[:, W:2 * W] = cg
        proj_ref[:, 2 * W:3 * W] = xv
        p = cg * xv
        u = _conv(cw_ref[...], _taps_back(carry[...], p, KW))
        carry[...] = p[tm - 8:tm, :]
        bu = _bf(bg * u)
        bu_ref[...] = bu
        pre = alpha * hv + _dot(bu, wout[...])
        pre_ref[...] = pre
        out_ref[...] = _ln_fwd(pre, g_ref[...], b_ref[...], _row_ids(i, tm, D), first_row)

    row = lambda i: (i, 0)
    fix = lambda i: (0, 0)
    return pl.pallas_call(
        body, name="sc_fwd", grid=(L // tm,),
        out_shape=(S((L, 3 * W), f32), S((L, W), bf16), S((L, D), f32), S((L, D), f32)),
        in_specs=[pl.BlockSpec((tm, D), row), ANY, pl.BlockSpec((KW, W), fix), ANY,
                  pl.BlockSpec((1, D), fix), pl.BlockSpec((1, D), fix)],
        out_specs=(pl.BlockSpec((tm, 3 * W), row), pl.BlockSpec((tm, W), row),
                   pl.BlockSpec((tm, D), row), pl.BlockSpec((tm, D), row)),
        scratch_shapes=[pltpu.VMEM((D, 3 * W), w_in.dtype), pltpu.VMEM((W, D), w_out.dtype),
                        pltpu.VMEM((8, W), f32), pltpu.SemaphoreType.DMA((2,))],
        compiler_params=_cp(dimension_semantics=("arbitrary",)))(h, w_in, conv_w, w_out, g, b)


def _sc_bwd(dpre, proj, conv_w, w_out, w_in, pre_in, g_in, *, first_row):
    L, D = dpre.shape
    W = w_out.shape[0]
    KW = conv_w.shape[0]
    tm = TM
    nb = L // tm
    alpha = ALPHA

    def body(dpre_ref, proj_ref, hc_ref, hx_ref, cw_ref, wout_hbm, win_hbm, pin_ref, g_ref,
             dproj_ref, dcw_ref, dpin_ref, dg_ref, db_ref, wout, win, carry, sem):
        i = pl.program_id(0)
        blk = nb - 1 - i
        _load_once([(wout_hbm, wout), (win_hbm, win)], sem)

        @pl.when(i == 0)
        def _():
            carry[...] = jnp.zeros_like(carry)
            dcw_ref[...] = jnp.zeros_like(dcw_ref)
            dg_ref[...] = jnp.zeros_like(dg_ref)
            db_ref[...] = jnp.zeros_like(db_ref)

        bg, cg, xv = proj_ref[:, 0:W], proj_ref[:, W:2 * W], proj_ref[:, 2 * W:3 * W]
        p = cg * xv
        u = _conv(cw_ref[...], _taps_back(jnp.where(blk > 0, hc_ref[...] * hx_ref[...], 0.0), p, KW))
        dpre_v = dpre_ref[...]
        d = _dot_nt(_bf(dpre_v), wout[...])
        dproj_ref[:, 0:W] = _bf(d * u)
        du = d * bg
        ahead = _taps_ahead(du, carry[...], KW)
        carry[...] = du[0:8, :]
        dp = _conv(cw_ref[...], ahead)
        for j in range(KW):
            dcw_ref[j:j + 1, :] += jnp.sum(ahead[j] * p, axis=0, keepdims=True)
        dproj_ref[:, W:2 * W] = _bf(dp * xv)
        dproj_ref[:, 2 * W:3 * W] = _bf(dp * cg)
        dh = alpha * dpre_v + _dot_nt(dproj_ref[...], win[...])
        dpin, dg, dbias = _ln_bwd_rows(dh, pin_ref[...], g_ref[...], _row_ids(blk, tm, D), first_row)
        dpin_ref[...] = dpin
        dg_ref[0:1, :] += dg
        db_ref[0:1, :] += dbias

    rev = lambda i: (nb - 1 - i, 0)
    fix = lambda i: (0, 0)

    def halo(col):
        return pl.BlockSpec((8, W), lambda i: (jnp.maximum((nb - 1 - i) * (tm // 8) - 1, 0), col))

    return pl.pallas_call(
        body, name="sc_bwd", grid=(nb,),
        out_shape=(S((L, 3 * W), bf16), S((8, W), f32), S((L, D), f32), S((8, D), f32), S((8, D), f32)),
        in_specs=[pl.BlockSpec((tm, D), rev), pl.BlockSpec((tm, 3 * W), rev), halo(1), halo(2),
                  pl.BlockSpec((KW, W), fix), ANY, ANY, pl.BlockSpec((tm, D), rev), pl.BlockSpec((1, D), fix)],
        out_specs=(pl.BlockSpec((tm, 3 * W), rev), pl.BlockSpec((8, W), fix), pl.BlockSpec((tm, D), rev),
                   pl.BlockSpec((8, D), fix), pl.BlockSpec((8, D), fix)),
        scratch_shapes=[pltpu.VMEM((W, D), w_out.dtype), pltpu.VMEM((D, 3 * W), w_in.dtype), pltpu.VMEM((8, W), f32),
                        pltpu.SemaphoreType.DMA((2,))],
        compiler_params=_cp(dimension_semantics=("arbitrary",)))(
            dpre, proj, proj, proj, conv_w, w_out, w_in, pre_in, g_in)


def _ffn_cols(F):
    fc = F
    for cand in (1408, 1024, 512, 256, 128):
        if F % cand == 0:
            fc = cand
            break
    return fc


def _ffn_fwd(h, w_up, conv_w, w_down, g, b, *, first_row, name):
    L, D = h.shape
    F = w_down.shape[0]
    KW = conv_w.shape[0]
    tm = TM
    fc = _ffn_cols(F)
    alpha = ALPHA

    def body(h_ref, wup_hbm, cw_ref, wdn_hbm, g_ref, b_ref, up_ref, a_ref, pre_ref, out_ref,
             wup, wdn, carry, sem):
        i = pl.program_id(0)
        _load_once([(wup_hbm, wup), (wdn_hbm, wdn)], sem)

        @pl.when(i == 0)
        def _():
            carry[...] = jnp.zeros_like(carry)

        hv = h_ref[...]
        hb = _bf(hv)
        pre = alpha * hv
        for c0 in range(0, F, fc):
            cs = slice(c0, c0 + fc)
            u = _dot(hb, wup[:, cs])
            gate = _dot(hb, wup[:, F + c0:F + c0 + fc])
            up_ref[:, cs] = u
            up_ref[:, F + c0:F + c0 + fc] = gate
            uc = _conv(cw_ref[:, cs], _taps_back(carry[:, cs], u, KW))
            carry[:, cs] = u[tm - 8:tm, :]
            ab = _bf(uc * _sigmoid(uc) * gate)
            a_ref[:, cs] = ab
            pre = pre + _dot(ab, wdn[cs, :])
        pre_ref[...] = pre
        out_ref[...] = _ln_fwd(pre, g_ref[...], b_ref[...], _row_ids(i, tm, D), first_row)

    row = lambda i: (i, 0)
    fix = lambda i: (0, 0)
    return pl.pallas_call(
        body, name=name, grid=(L // tm,),
        out_shape=(S((L, 2 * F), f32), S((L, F), bf16), S((L, D), f32), S((L, D), f32)),
        in_specs=[pl.BlockSpec((tm, D), row), ANY, pl.BlockSpec((KW, F), fix), ANY,
                  pl.BlockSpec((1, D), fix), pl.BlockSpec((1, D), fix)],
        out_specs=(pl.BlockSpec((tm, 2 * F), row), pl.BlockSpec((tm, F), row),
                   pl.BlockSpec((tm, D), row), pl.BlockSpec((tm, D), row)),
        scratch_shapes=[pltpu.VMEM((D, 2 * F), w_up.dtype), pltpu.VMEM((F, D), w_down.dtype),
                        pltpu.VMEM((8, F), f32), pltpu.SemaphoreType.DMA((2,))],
        compiler_params=_cp(dimension_semantics=("arbitrary",)))(h, w_up, conv_w, w_down, g, b)


def _ffn_bwd(dpre, up, w_down, conv_w, w_up, pre_in, g_in, *, first_row, name):
    L, D = dpre.shape
    F = w_down.shape[0]
    KW = conv_w.shape[0]
    tm = TM
    nb = L // tm
    fc = F
    alpha = ALPHA

    def body(dpre_ref, up_ref, halo_ref, wdn_hbm, cw_ref, wup_hbm, pin_ref, g_ref,
             dup_ref, dcw_ref, dpin_ref, dg_ref, db_ref, wdn, wup, carry, sem):
        i = pl.program_id(0)
        blk = nb - 1 - i
        _load_once([(wdn_hbm, wdn), (wup_hbm, wup)], sem)

        @pl.when(i == 0)
        def _():
            carry[...] = jnp.zeros_like(carry)
            dcw_ref[...] = jnp.zeros_like(dcw_ref)
            dg_ref[...] = jnp.zeros_like(dg_ref)
            db_ref[...] = jnp.zeros_like(db_ref)

        dpre_v = dpre_ref[...]
        db = _bf(dpre_v)
        dh = alpha * dpre_v
        for c0 in range(0, F, fc):
            cs = slice(c0, c0 + fc)
            gs_ = slice(F + c0, F + c0 + fc)
            da = _dot_nt(db, wdn[cs, :])
            gate = up_ref[:, gs_]
            u = up_ref[:, cs]
            uc = _conv(cw_ref[:, cs], _taps_back(jnp.where(blk > 0, halo_ref[:, cs], 0.0), u, KW))
            sig = _sigmoid(uc)
            dgate = _bf(da * (uc * sig))
            dup_ref[:, gs_] = dgate
            duc = da * gate * (sig * (1.0 + uc * (1.0 - sig)))
            ahead = _taps_ahead(duc, carry[:, cs], KW)
            carry[:, cs] = duc[0:8, :]
            du = _bf(_conv(cw_ref[:, cs], ahead))
            dup_ref[:, cs] = du
            for j in range(KW):
                dcw_ref[j:j + 1, cs] += jnp.sum(ahead[j] * u, axis=0, keepdims=True)
            dh = dh + _dot_nt(du, wup[:, cs]) + _dot_nt(dgate, wup[:, gs_])
        dpin, dg, dbias = _ln_bwd_rows(dh, pin_ref[...], g_ref[...], _row_ids(blk, tm, D), first_row)
        dpin_ref[...] = dpin
        dg_ref[0:1, :] += dg
        db_ref[0:1, :] += dbias

    rev = lambda i: (nb - 1 - i, 0)
    fix = lambda i: (0, 0)
    return pl.pallas_call(
        body, name=name, grid=(nb,),
        out_shape=(S((L, 2 * F), bf16), S((8, F), f32), S((L, D), f32), S((8, D), f32), S((8, D), f32)),
        in_specs=[pl.BlockSpec((tm, D), rev), pl.BlockSpec((tm, 2 * F), rev),
                  pl.BlockSpec((8, F), lambda i: (jnp.maximum((nb - 1 - i) * (tm // 8) - 1, 0), 0)),
                  ANY, pl.BlockSpec((KW, F), fix), ANY, pl.BlockSpec((tm, D), rev), pl.BlockSpec((1, D), fix)],
        out_specs=(pl.BlockSpec((tm, 2 * F), rev), pl.BlockSpec((8, F), fix), pl.BlockSpec((tm, D), rev),
                   pl.BlockSpec((8, D), fix), pl.BlockSpec((8, D), fix)),
        scratch_shapes=[pltpu.VMEM((F, D), w_down.dtype), pltpu.VMEM((D, 2 * F), w_up.dtype), pltpu.VMEM((8, F), f32),
                        pltpu.SemaphoreType.DMA((2,))],
        compiler_params=_cp(dimension_semantics=("arbitrary",)))(dpre, up, up, w_down, conv_w, w_up, pre_in, g_in)


def _loss_head(h, target, pre, g, *, first_row):
    L, D = h.shape
    tm = TM
    pb = PADF // tm

    def body(h_ref, t_ref, pre_ref, g_ref, dpre_ref, dg_ref, db_ref, loss_ref):
        i = pl.program_id(0)

        @pl.when(i == 0)
        def _():
            loss_ref[...] = jnp.zeros_like(loss_ref)
            dg_ref[...] = jnp.zeros_like(dg_ref)
            db_ref[...] = jnp.zeros_like(db_ref)

        valid = i >= pb
        err = h_ref[...] - t_ref[...]
        dh = jnp.where(valid, err * (1.0 / D), 0.0)
        part = 0.5 * jnp.sum(jnp.sum(err * err, axis=-1, keepdims=True) * (1.0 / D), axis=0, keepdims=True)
        loss_ref[...] += jnp.where(valid, part, 0.0)
        dpre, dg, db = _ln_bwd_rows(dh, pre_ref[...], g_ref[...], _row_ids(i, tm, D), first_row)
        dpre_ref[...] = dpre
        dg_ref[0:1, :] += dg
        db_ref[0:1, :] += db

    row = lambda i: (i, 0)
    fix = lambda i: (0, 0)
    return pl.pallas_call(
        body, name="loss_head", grid=(L // tm,),
        out_shape=(S((L, D), f32), S((8, D), f32), S((8, D), f32), S((8, LANE), f32)),
        in_specs=[pl.BlockSpec((tm, D), row), pl.BlockSpec((tm, D), lambda i: (jnp.maximum(i - pb, 0), 0)),
                  pl.BlockSpec((tm, D), row), pl.BlockSpec((1, D), fix)],
        out_specs=(pl.BlockSpec((tm, D), row), pl.BlockSpec((8, D), fix), pl.BlockSpec((8, D), fix),
                   pl.BlockSpec((8, LANE), fix)),
        compiler_params=_cp(dimension_semantics=("arbitrary",)))(h, target, pre, g)


def _adamw(g_terms, w, m, v, *, name):
    R, C = w.shape
    tr = _row_tile(R)
    n = len(g_terms)
    c1 = 1.0 - ADAM_B1 ** ADAM_STEP
    c2 = 1.0 - ADAM_B2 ** ADAM_STEP

    def body(*refs):
        g = refs[0][...].astype(f32)
        for r in refs[1:n]:
            g = g + r[...].astype(f32)
        w_ref, m_ref, v_ref, g_out, d_out, m_out, v_out = refs[n:]
        mn = ADAM_B1 * m_ref[...] + (1.0 - ADAM_B1) * g
        vn = ADAM_B2 * v_ref[...] + (1.0 - ADAM_B2) * (g * g)
        g_out[...] = g
        m_out[...] = mn
        v_out[...] = vn
        d_out[...] = -ADAM_LR * ((mn / c1) / (jnp.sqrt(vn / c2) + ADAM_EPS) + ADAM_WD * w_ref[...])

    spec = pl.BlockSpec((tr, C), lambda i: (i, 0))
    return pl.pallas_call(
        body, name=name, grid=(R // tr,), out_shape=(S((R, C), f32),) * 4,
        in_specs=[spec] * (n + 3), out_specs=(spec,) * 4,
        compiler_params=_cp(dimension_semantics=("arbitrary",)))(*g_terms, w, m, v)


def _sum_devices(x):
    n, R, C = x.shape

    def body(x_ref, o_ref):
        acc = x_ref[0]
        for d in range(1, n):
            acc = acc + x_ref[d]
        o_ref[...] = acc

    return pl.pallas_call(body, name="sum_devices", out_shape=S((R, C), f32), compiler_params=_cp())(x)


def _row_tile(R):
    for step in (16, 8):
        for t in range(256, 0, -step):
            if R % t == 0:
                return t
    return R


def _adamw_direct(s32s, recvs, w, m, v, me, *, name):
    L, K, n = w.shape
    tk = _row_tile(K)
    c1 = 1.0 - ADAM_B1 ** ADAM_STEP
    c2 = 1.0 - ADAM_B2 ** ADAM_STEP

    def body(me_ref, *refs):
        own_refs, recv_refs = refs[:L], refs[L:2 * L]
        w_ref, m_ref, v_ref, g_out, d_out, m_out, v_out = refs[2 * L:]
        for li in range(L):
            @pl.when(pl.program_id(0) == li)
            def _(li=li):
                g = own_refs[li][0, 0]
                for d in range(N_DEV):
                    g = g + recv_refs[li][d, 0].astype(f32)
                mn = ADAM_B1 * m_ref[0] + (1.0 - ADAM_B1) * g
                vn = ADAM_B2 * v_ref[0] + (1.0 - ADAM_B2) * (g * g)
                g_out[0] = g
                m_out[0] = mn
                v_out[0] = vn
                d_out[0] = -ADAM_LR * ((mn / c1) / (jnp.sqrt(vn / c2) + ADAM_EPS) + ADAM_WD * w_ref[0])

    own = pl.BlockSpec((1, tk, n), lambda l, i, ix: (l, i, 0))
    grid_spec = pltpu.PrefetchScalarGridSpec(
        num_scalar_prefetch=1, grid=(L, K // tk),
        in_specs=[pl.BlockSpec((1, 1, tk, n), lambda l, i, ix: (ix[0], 0, i, 0))] * L
        + [pl.BlockSpec((N_DEV, 1, tk, n), lambda l, i, ix: (0, 0, i, 0))] * L + [own, own, own],
        out_specs=(own,) * 4)
    return pl.pallas_call(
        body, name=name, grid_spec=grid_spec, out_shape=(S((L, K, n), f32),) * 4,
        compiler_params=_cp(dimension_semantics=("arbitrary", "arbitrary")))(me, *s32s, *recvs, w, m, v)


def _col_segments(n, mapping):
    segs = []
    for p in range(N_DEV):
        lo, hi = p * n, (p + 1) * n
        out = []
        for c0, c1, e0 in mapping:
            a, b = max(lo, c0), min(hi, c1)
            if a < b:
                out.append((a - lo, e0 + (a - c0), b - a))
        segs.append(out)
    return segs


def _assemble_cols(gathered, mapping, n_out, *, name):
    _, L, K, n = gathered.shape
    tk = _row_tile(K)
    segs = _col_segments(n, mapping)
    covered = sum(w for s in segs for (_, _, w) in s)

    def body(g_ref, o_ref):
        if covered != n_out:
            o_ref[...] = jnp.zeros_like(o_ref)
        for p in range(N_DEV):
            for s0, d0, w in segs[p]:
                o_ref[0, :, d0:d0 + w] = g_ref[p, 0, :, s0:s0 + w]

    return pl.pallas_call(
        body, name=name, grid=(L, K // tk), out_shape=S((L, K, n_out), gathered.dtype),
        in_specs=[pl.BlockSpec((N_DEV, 1, tk, n), lambda l, i: (0, l, i, 0))],
        out_specs=pl.BlockSpec((1, tk, n_out), lambda l, i: (l, i, 0)),
        compiler_params=_cp(dimension_semantics=("arbitrary", "arbitrary")))(gathered)


def _split_cols(dws, mapping, n, *, name):
    L = len(dws)
    K, n_in = dws[0].shape
    tk = _row_tile(K)
    segs = _col_segments(n, mapping)

    def body(*refs):
        ins, o32, o16 = refs[:L], refs[L], refs[L + 1]
        for li in range(L):
            @pl.when(pl.program_id(0) == li)
            def _(li=li):
                for p in range(N_DEV):
                    for s0, d0, w in segs[p]:
                        val = ins[li][:, d0:d0 + w]
                        o32[p, 0, :, s0:s0 + w] = val
                        o16[p, 0, :, s0:s0 + w] = _bf(val)

    out = pl.BlockSpec((N_DEV, 1, tk, n), lambda l, i: (0, l, i, 0))
    return pl.pallas_call(
        body, name=name, grid=(L, K // tk), out_shape=(S((N_DEV, L, K, n), f32), S((N_DEV, L, K, n), bf16)),
        in_specs=[pl.BlockSpec((tk, n_in), lambda l, i: (i, 0))] * L, out_specs=(out, out),
        compiler_params=_cp(dimension_semantics=("arbitrary", "arbitrary")))(*dws)


def _split_rows(dws, k, *, name):
    L = len(dws)
    N = dws[0].shape[1]

    def body(*refs):
        ins, o32, o16 = refs[:L], refs[L], refs[L + 1]
        for li in range(L):
            @pl.when(pl.program_id(0) == li)
            def _(li=li):
                val = ins[li][...]
                o32[0, 0] = val
                o16[0, 0] = _bf(val)

    out = pl.BlockSpec((1, 1, k, N), lambda l, p: (p, l, 0, 0))
    return pl.pallas_call(
        body, name=name, grid=(L, N_DEV), out_shape=(S((N_DEV, L, k, N), f32), S((N_DEV, L, k, N), bf16)),
        in_specs=[pl.BlockSpec((k, N), lambda l, p: (p, 0))] * L, out_specs=(out, out),
        compiler_params=_cp(dimension_semantics=("arbitrary", "arbitrary")))(*dws)


def _rows_full(gathered):
    _, L, k, N = gathered.shape
    return jnp.transpose(gathered, (1, 0, 2, 3)).reshape(L, N_DEV * k, N)


def _all_gather(xs, *, name):
    na = len(xs)

    def body(*refs):
        x_refs, out_refs = refs[:na], refs[na:2 * na]
        send_sems, recv_sems, local_sems = refs[2 * na:]
        mx, my, mc = lax.axis_index("x"), lax.axis_index("y"), lax.axis_index("c")
        me, sibling = (mx, my, mc), (mx, my, 1 - mc)
        chips = [(1 - mx, my), (mx, 1 - my), (1 - mx, 1 - my)]

        def slot(a, px, py, pc):
            return out_refs[a].at[4 * px + 2 * py + pc]

        def copy(a, kk, block, to, src=None):
            return pltpu.make_async_remote_copy(
                src_ref=slot(a, *block) if src is None else src, dst_ref=slot(a, *block),
                send_sem=send_sems.at[7 * a + kk], recv_sem=recv_sems.at[7 * a + kk], device_id=to, device_id_type=MESH)

        mine = [pltpu.make_async_copy(x_refs[a], slot(a, *me), local_sems.at[a]) for a in range(na)]
        for cp in mine:
            cp.start()
        first = []
        for a in range(na):
            first.append(copy(a, 0, me, sibling, src=x_refs[a]))
            first += [copy(a, 1 + j, me, (*chip, mc), src=x_refs[a]) for j, chip in enumerate(chips)]
        for cp in first:
            cp.start()
        passed = []
        for j, chip in enumerate(chips):
            for a in range(na):
                copy(a, 1 + j, (*chip, mc), me).wait_recv()
                fwd = copy(a, 4 + j, (*chip, mc), sibling)
                fwd.start()
                passed.append(fwd)
        for a in range(na):
            copy(a, 0, sibling, me).wait_recv()
            for j, chip in enumerate(chips):
                copy(a, 4 + j, (*chip, 1 - mc), me).wait_recv()
        for cp in first + passed:
            cp.wait_send()
        for cp in mine:
            cp.wait()

    return pl.pallas_call(
        body, name=name, out_shape=tuple(S((N_DEV,) + x.shape, x.dtype) for x in xs),
        in_specs=[ANY] * na, out_specs=(ANY,) * na,
        scratch_shapes=[pltpu.SemaphoreType.DMA((7 * na,)), pltpu.SemaphoreType.DMA((7 * na,)),
                        pltpu.SemaphoreType.DMA((na,))],
        compiler_params=pltpu.CompilerParams(has_side_effects=True))(*xs)


_FLIPS = [(fx, fy, fc) for fx in (0, 1) for fy in (0, 1) for fc in (0, 1)][1:]


def _flip_peer(flip):
    x, y, c = lax.axis_index("x"), lax.axis_index("y"), lax.axis_index("c")
    return tuple(1 - a if f else a for a, f in zip((x, y, c), flip))


def _dev_index(p):
    return 4 * p[0] + 2 * p[1] + p[2]


HBM_SPEC = pl.BlockSpec(memory_space=pltpu.HBM)
SEM_SPEC = pl.BlockSpec(memory_space=pltpu.SEMAPHORE)


def _direct_start(srcs, lands, per_peer, *, name):
    na = len(srcs)

    def body(*refs):
        src_refs, land_refs = refs[:na], refs[na:2 * na]
        send_sems, recv_sems = refs[2 * na], refs[2 * na + 1]
        token = refs[-1]
        me = _dev_index((lax.axis_index("x"), lax.axis_index("y"), lax.axis_index("c")))
        for a in range(na):
            for r, flip in enumerate(_FLIPS):
                peer = _flip_peer(flip)
                src = src_refs[a].at[_dev_index(peer)] if per_peer else src_refs[a]
                pltpu.make_async_remote_copy(
                    src_ref=src, dst_ref=land_refs[a].at[me], send_sem=send_sems.at[7 * a + r],
                    recv_sem=recv_sems.at[7 * a + r], device_id=peer, device_id_type=MESH).start()
        token[...] = jnp.zeros_like(token)

    hbm = lambda t: pltpu.with_memory_space_constraint(t, pltpu.HBM)
    out = pl.pallas_call(
        body, name=name,
        out_shape=(pltpu.SemaphoreType.DMA((7 * na,)), pltpu.SemaphoreType.DMA((7 * na,)))
        + tuple(pltpu.HBM(t.shape, t.dtype) for t in list(srcs) + list(lands)) + (S((8, LANE), f32),),
        in_specs=[HBM_SPEC] * (2 * na),
        out_specs=(SEM_SPEC, SEM_SPEC) + (HBM_SPEC,) * (2 * na) + (pl.BlockSpec(memory_space=pltpu.VMEM),),
        input_output_aliases={i: 2 + i for i in range(2 * na)},
        compiler_params=pltpu.CompilerParams(has_side_effects=pltpu.SideEffectType.DATAFLOW_SIDE_EFFECTING))(
            *[hbm(t) for t in srcs], *[hbm(t) for t in lands])
    return out[0], out[1], list(out[2:2 + na]), list(out[2 + na:2 + 2 * na]), out[-1]


def _direct_wait(send_sems, recv_sems, srcs, lands, per_peer, after, *, name):
    na = len(srcs)

    def body(*refs):
        src_refs, land_refs = refs[:na], refs[na:2 * na]
        ssem, rsem = refs[2 * na], refs[2 * na + 1]
        me = _dev_index((lax.axis_index("x"), lax.axis_index("y"), lax.axis_index("c")))
        for a in range(na):
            for r, flip in enumerate(_FLIPS):
                peer = _flip_peer(flip)
                src = src_refs[a].at[_dev_index(peer)] if per_peer else src_refs[a]
                cp = pltpu.make_async_remote_copy(
                    src_ref=src, dst_ref=land_refs[a].at[me], send_sem=ssem.at[7 * a + r],
                    recv_sem=rsem.at[7 * a + r], device_id=peer, device_id_type=MESH)
                cp.wait_send()
                cp.wait_recv()

    out = pl.pallas_call(
        body, name=name, out_shape=tuple(pltpu.HBM(t.shape, t.dtype) for t in list(srcs) + list(lands)),
        in_specs=[HBM_SPEC] * (2 * na) + [SEM_SPEC, SEM_SPEC, ANY], out_specs=(HBM_SPEC,) * (2 * na),
        input_output_aliases={i: i for i in range(2 * na)},
        compiler_params=pltpu.CompilerParams(has_side_effects=pltpu.SideEffectType.DATAFLOW_SIDE_EFFECTING))(
            *srcs, *lands, send_sems, recv_sems, after)
    return list(out[:na]), list(out[na:])


def _pack_small(parts, width):
    rows, offs, r = [], [], 0
    for a in parts:
        n = a.size
        nr = -(-n // width)
        flat = a.reshape(-1).astype(f32)
        if nr * width != n:
            flat = jnp.pad(flat, (0, nr * width - n))
        rows.append(flat.reshape(nr, width))
        offs.append((r, nr))
        r += nr
    buf = jnp.concatenate(rows, axis=0)
    pad = (-r) % 8
    if pad:
        buf = jnp.pad(buf, ((0, pad), (0, 0)))
    return buf, offs


def _unpack_small(buf, off, shape):
    r, nr = off
    return buf[r:r + nr].reshape(-1)[:math.prod(shape)].reshape(shape)


def _local_step(x, target, meta, a_w_in, a_w_out, small, start_token, late_weights, grads_ready):
    SEQ, D = x.shape
    n_meta = meta.shape[0]
    first_row = PADF - n_meta
    H = small["a_log"].shape[-1]

    head = jnp.concatenate([jnp.zeros((first_row, D), f32), meta], axis=0)

    def lanes(a):
        return jnp.pad(a.reshape(1, -1), ((0, 0), (0, LANE - a.size)))

    def after_token(a, token):
        return a if token is None else a + token[0:1, 0:1]

    alog, dtb = after_token(lanes(small["a_log"][0]), start_token), lanes(small["a_dt_bias"][0])
    a_conv, b_conv = small["a_conv"][0], small["b_conv"][0]
    nw = small["a_norm"][0].reshape(1, DH)
    lmg, lmb, lfg, lfb = small["ln_mix_g"], small["ln_mix_b"], small["ln_ffn_g"], small["ln_ffn_b"]

    h0, pre_a, z, raw, q, k, v, beta, g = _gdn_in_fwd(x, head, a_w_in, a_conv, alog, dtb, first_row=first_row, H=H)
    o, y, s_all, t_all, pre1, h1 = _delta_fwd(q, k, v, g, beta, z, nw, h0, a_w_out, lmg[0:1], lmb[0:1],
                                              first_row=first_row, H=H)
    wts = late_weights(h1)
    up0, act0, pre2, h2 = _ffn_fwd(h1, wts["ffn_w_up"][0], small["ffn_conv"][0], wts["ffn_w_down"][0],
                                   lfg[0:1], lfb[0:1], first_row=first_row, name="ffn_fwd0")
    proj_b, bu, pre3, h3 = _sc_fwd(h2, wts["b_w_in"], b_conv, wts["b_w_out"], lmg[1:2], lmb[1:2], first_row=first_row)
    up1, act1, pre4, h4 = _ffn_fwd(h3, wts["ffn_w_up"][1], small["ffn_conv"][1], wts["ffn_w_down"][1],
                                   lfg[1:2], lfb[1:2], first_row=first_row, name="ffn_fwd1")
    gs = {}
    dpre4, dlfg1, dlfb1, loss_tile = _loss_head(h4, target, pre4, lfg[1:2], first_row=first_row)

    def ffn_backward(dpre, up, act, h_in, layer, tag, ln_in, token=None):
        dup, dcw, dpre_in, dg, db = _ffn_bwd(
            dpre, up, wts["ffn_w_down"][layer], after_token(small["ffn_conv"][layer], token),
            wts["ffn_w_up"][layer], ln_in[0], ln_in[1], first_row=first_row, name="ffn_bwd" + tag)
        dwd = _linear_dw(act, dpre, name="dw_down" + tag)
        dwu = _linear_dw(h_in, dup, name="dw_up" + tag)
        return dpre_in, dg, db, dwu, dwd, dcw[0:3]

    dpre3, dlmg1, dlmb1, dwu1, dwd1, dcf1 = ffn_backward(dpre4, up1, act1, h3, 1, "1", (pre3, lmg[1:2]))

    dwb_out = _linear_dw(bu, dpre3, name="dw_b_out")
    dproj_b, dcb, dpre2, dlfg0, dlfb0 = _sc_bwd(dpre3, proj_b, b_conv, wts["b_w_out"], wts["b_w_in"], pre2, lfg[0:1],
                                                first_row=first_row)
    dwb_in = _linear_dw(h2, dproj_b, name="dw_b_in")
    token = grads_ready("layer1", dict(ffn_w_up=dwu1, ffn_w_down=dwd1, b_w_in=dwb_in, b_w_out=dwb_out))

    dpre1, dlmg0, dlmb0, dwu0, dwd0, dcf0 = ffn_backward(dpre2, up0, act0, h1, 0, "0", (pre1, lmg[0:1]), token)
    dwa_out = _linear_dw(y, dpre1, name="dw_a_out")
    token = grads_ready("layer0", dict(ffn_w_up=dwu0, ffn_w_down=dwd0, a_w_out=dwa_out))

    dq, dk, dv, dz, dg_, dbeta, dnw = _delta_bwd(dpre1, a_w_out, o, z, after_token(nw, token), q, k, v, g, beta,
                                                 s_all, t_all, H=H)
    dproj_a, dca, dal, ddt, grad_x, dhead = _gdn_in_bwd(dq, dk, dv, dz, dg_, dbeta, pre_a, raw, a_conv, alog, dtb,
                                                        a_w_in, dpre1, first_row=first_row, H=H)
    grads_ready("last", dict(a_w_in=_linear_dw(h0, dproj_a, name="dw_a_in")))

    gs["meta"] = dhead[first_row:PADF]
    gs["a_conv"] = dca[0:a_conv.shape[0]][None]
    gs["a_log"] = dal[0:1, 0:H]
    gs["a_dt_bias"] = ddt[0:1, 0:H]
    gs["a_norm"] = dnw[0:1]
    gs["b_conv"] = dcb[0:b_conv.shape[0]][None]
    gs["ln_mix_g"] = jnp.stack([dlmg0[0], dlmg1[0]])
    gs["ln_mix_b"] = jnp.stack([dlmb0[0], dlmb1[0]])
    gs["ffn_conv"] = jnp.stack([dcf0, dcf1])
    gs["ln_ffn_g"] = jnp.stack([dlfg0[0], dlfg1[0]])
    gs["ln_ffn_b"] = jnp.stack([dlfb0[0], dlfb1[0]])
    return loss_tile, grad_x, gs


_BIG = ("a_w_in", "a_w_out", "b_w_in", "b_w_out", "ffn_w_up", "ffn_w_down")
_BIG_COL = ("a_w_in", "b_w_in", "ffn_w_up")
_SMALL = ("meta", "a_conv", "a_log", "a_dt_bias", "a_norm", "b_conv", "ln_mix_g", "ln_mix_b",
          "ffn_conv", "ln_ffn_g", "ln_ffn_b")
_SMALL_SHARDED = ("meta", "a_conv", "b_conv", "ffn_conv")
_ORDER = ("meta", "a_w_in", "a_conv", "a_log", "a_dt_bias", "a_norm", "a_w_out", "b_w_in", "b_conv", "b_w_out",
          "ln_mix_g", "ln_mix_b", "ffn_w_up", "ffn_conv", "ffn_w_down", "ln_ffn_g", "ln_ffn_b")


def _a_w_in_map(H):
    W4 = 4 * H * DH
    return [(0, W4, 0), (W4, W4 + H, W4), (W4 + H, W4 + 2 * H, W4 + LANE)], W4 + 2 * LANE


def kernel(x, meta, a_w_in, a_conv, a_log, a_dt_bias, a_norm, a_w_out, b_w_in, b_conv, b_w_out, ln_mix_g, ln_mix_b, ffn_w_up, ffn_conv, ffn_w_down, ln_ffn_g, ln_ffn_b, loss_target, m_meta, m_a_w_in, m_a_conv, m_a_log, m_a_dt_bias, m_a_norm, m_a_w_out, m_b_w_in, m_b_conv, m_b_w_out, m_ln_mix_g, m_ln_mix_b, m_ffn_w_up, m_ffn_conv, m_ffn_w_down, m_ln_ffn_g, m_ln_ffn_b, v_meta, v_a_w_in, v_a_conv, v_a_log, v_a_dt_bias, v_a_norm, v_a_w_out, v_b_w_in, v_b_conv, v_b_w_out, v_ln_mix_g, v_ln_mix_b, v_ffn_w_up, v_ffn_conv, v_ffn_w_down, v_ln_ffn_g, v_ln_ffn_b):
    wloc = dict(meta=meta, a_w_in=a_w_in, a_conv=a_conv, a_log=a_log, a_dt_bias=a_dt_bias, a_norm=a_norm,
                a_w_out=a_w_out, b_w_in=b_w_in, b_conv=b_conv, b_w_out=b_w_out, ln_mix_g=ln_mix_g, ln_mix_b=ln_mix_b,
                ffn_w_up=ffn_w_up, ffn_conv=ffn_conv, ffn_w_down=ffn_w_down, ln_ffn_g=ln_ffn_g, ln_ffn_b=ln_ffn_b)
    mloc = dict(meta=m_meta, a_w_in=m_a_w_in, a_conv=m_a_conv, a_log=m_a_log, a_dt_bias=m_a_dt_bias, a_norm=m_a_norm,
                a_w_out=m_a_w_out, b_w_in=m_b_w_in, b_conv=m_b_conv, b_w_out=m_b_w_out, ln_mix_g=m_ln_mix_g,
                ln_mix_b=m_ln_mix_b, ffn_w_up=m_ffn_w_up, ffn_conv=m_ffn_conv, ffn_w_down=m_ffn_w_down,
                ln_ffn_g=m_ln_ffn_g, ln_ffn_b=m_ln_ffn_b)
    vloc = dict(meta=v_meta, a_w_in=v_a_w_in, a_conv=v_a_conv, a_log=v_a_log, a_dt_bias=v_a_dt_bias, a_norm=v_a_norm,
                a_w_out=v_a_w_out, b_w_in=v_b_w_in, b_conv=v_b_conv, b_w_out=v_b_w_out, ln_mix_g=v_ln_mix_g,
                ln_mix_b=v_ln_mix_b, ffn_w_up=v_ffn_w_up, ffn_conv=v_ffn_conv, ffn_w_down=v_ffn_w_down,
                ln_ffn_g=v_ln_ffn_g, ln_ffn_b=v_ln_ffn_b)
    H = a_log.shape[-1]
    mx, my, mc = lax.axis_index("x"), lax.axis_index("y"), lax.axis_index("c")
    me = 4 * mx + 2 * my + mc

    a_map, a_cols = _a_w_in_map(H)
    col_maps = {"a_w_in": (a_map, a_cols)}
    for n in ("b_w_in", "ffn_w_up"):
        ncols = N_DEV * wloc[n].shape[-1]
        col_maps[n] = ([(0, ncols, 0)], ncols)
    sm_sh = [wloc[n] for n in _SMALL_SHARDED]
    sbuf, soffs = _pack_small(sm_sh, 128)
    g_a_w_in, g_a_w_out, sg = _all_gather([_bf(wloc["a_w_in"]), _bf(wloc["a_w_out"]), sbuf], name="gather_first")
    w_a_in = _assemble_cols(g_a_w_in, *col_maps["a_w_in"], name="assemble_a_w_in")[0]
    w_a_out = _rows_full(g_a_w_out)[0]
    late = [n for n in _BIG if n not in ("a_w_in", "a_w_out")]
    ssem, rsem, srcs_t, lands_t, start_token = _direct_start(
        [_bf(wloc[n]) for n in late], [lax.empty((N_DEV,) + wloc[n].shape, bf16) for n in late], False,
        name="gather_rest_start")

    def late_weights(after):
        srcs_d, landed = _direct_wait(ssem, rsem, srcs_t, lands_t, False, after, name="gather_rest_wait")
        wts = {}
        for n, own, got in zip(late, srcs_d, landed):
            full = lax.dynamic_update_index_in_dim(got, own, me, 0)
            if n in _BIG_COL:
                wts[n] = _assemble_cols(full, *col_maps[n], name="assemble_" + n)
            else:
                wts[n] = _rows_full(full)
        for n in ("b_w_in", "b_w_out"):
            wts[n] = wts[n][0]
        return wts

    small = {n: wloc[n] for n in _SMALL}
    for n, off in zip(_SMALL_SHARDED, soffs):
        sh = wloc[n].shape
        parts = jnp.stack([_unpack_small(sg[d], off, sh) for d in range(N_DEV)])
        nd = len(sh)
        small[n] = jnp.transpose(parts, tuple(range(1, nd)) + (0, nd)).reshape(sh[:-1] + (N_DEV * sh[-1],))

    def split(n, dws, tag):
        if n in _BIG_COL:
            return _split_cols(dws, col_maps[n][0], wloc[n].shape[-1], name="split_" + n + tag)
        return _split_rows(dws, wloc[n].shape[-2], name="split_" + n + tag)

    sent = {}

    def grads_ready(stage, grads):
        names = sorted(grads)
        parts = [split(n, [grads[n]], "_" + stage) for n in names]
        handles = _direct_start([p[1] for p in parts], [jnp.zeros(p[1].shape, bf16) for p in parts], True,
                                name="grads_" + stage + "_start")
        sent[stage] = (names, [p[0] for p in parts], handles)
        return handles[4]

    loss_tile, grad_x, gs = _local_step(x[0], loss_target[0], small["meta"], w_a_in, w_a_out, small, start_token,
                                        late_weights, grads_ready)

    def landed(stage, after):
        names, own32, (ssem_g, rsem_g, srcs_g, lands_g, _) = sent[stage]
        _, got = _direct_wait(ssem_g, rsem_g, srcs_g, lands_g, True, after, name="grads_" + stage + "_wait")
        return list(zip(names, own32, got))

    parts = {}
    for stage in ("layer0", "layer1"):
        for n, o32, r in landed(stage, grad_x):
            parts.setdefault(n, []).append((o32, r))
    me1 = jnp.stack([me]).astype(jnp.int32)
    big_out = {n: _adamw_direct([p[0] for p in ps], [p[1] for p in ps], wloc[n], mloc[n], vloc[n], me1,
                                name="adamw_" + n) for n, ps in parts.items()}
    names = list(_SMALL)
    pbuf, poffs = _pack_small([gs[n] for n in names] + [loss_tile[0:1, 0:1]], 1024)
    psum = _sum_devices(_all_gather([pbuf], name="gather_small_grads")[0])
    loss = psum[poffs[-1][0], 0]
    g_small = {}
    for n, off in zip(names, poffs[:-1]):
        full_shape = gs[n].shape
        gfull = _unpack_small(psum, off, full_shape)
        if n in _SMALL_SHARDED:
            ns = wloc[n].shape[-1]
            gfull = lax.dynamic_slice_in_dim(gfull, me * ns, ns, axis=gfull.ndim - 1)
        g_small[n] = gfull.reshape(wloc[n].shape)
    gbuf, aoffs = _pack_small([g_small[n] for n in names], 128)
    wbuf, _ = _pack_small([wloc[n] for n in names], 128)
    mbuf, _ = _pack_small([mloc[n] for n in names], 128)
    vbuf, _ = _pack_small([vloc[n] for n in names], 128)
    _, d_s, m_s, v_s = _adamw([gbuf], wbuf, mbuf, vbuf, name="adamw_small")

    done = d_s[0, 0]
    for out in big_out.values():
        done = done + out[1][0, 0, 0]
    (n, o32, r), = landed("last", done.reshape(1, 1))
    big_out[n] = _adamw_direct([o32], [r], wloc[n], mloc[n], vloc[n], me1, name="adamw_" + n)

    grads, deltas, new_m, new_v = {}, {}, {}, {}
    for n in _BIG:
        grads[n], deltas[n], new_m[n], new_v[n] = big_out[n]
    for n, off in zip(names, aoffs):
        sh = wloc[n].shape
        grads[n] = g_small[n]
        deltas[n], new_m[n], new_v[n] = (_unpack_small(b_, off, sh) for b_ in (d_s, m_s, v_s))
    return (loss, grad_x[None], *[grads[n] for n in _ORDER], *[deltas[n] for n in _ORDER],
            *[new_m[n] for n in _ORDER], *[new_v[n] for n in _ORDER])
```

```python
import math

import jax
import jax.numpy as jnp
from jax import lax
from jax.experimental import pallas as pl
from jax.experimental.pallas import tpu as pltpu

f32, bf16 = jnp.float32, jnp.bfloat16
S = jax.ShapeDtypeStruct
HI = lax.Precision.HIGHEST
HI3 = lax.Precision.HIGH
MESH = pl.DeviceIdType.MESH

V7X_VMEM_LIMIT = 56 * 1024 * 1024
LANE = 128
DH = 128
CH = 64
PADF = 256
TM = 256
TMM = 768
N_DEV = 8
BWD_HEAD_GROUP = 4

DEPTH = 2
ALPHA = (2.0 * DEPTH) ** 0.25
LN_EPS = 1e-5
RMS_EPS = 1e-6
L2_EPS = 1e-6
ADAM_LR, ADAM_B1, ADAM_B2, ADAM_EPS, ADAM_WD, ADAM_STEP = 0.001, 0.9, 0.999, 1e-08, 0.01, 10


def _cp(**kw):
    return pltpu.CompilerParams(vmem_limit_bytes=V7X_VMEM_LIMIT, **kw)


def _bf(x):
    return x.astype(bf16)


def _dot(a, b, precision=None):
    return jnp.dot(a, b, preferred_element_type=f32, precision=precision)


def _dot_nt(a, b):
    return lax.dot_general(a, b, (((1,), (1,)), ((), ())), preferred_element_type=f32)


def _dot_tn(a, b):
    return lax.dot_general(a, b, (((0,), (0,)), ((), ())), preferred_element_type=f32)


def _sigmoid(x):
    return 1.0 / (1.0 + jnp.exp(-x))


def _load_once(pairs, sem):
    @pl.when(pl.program_id(0) == 0)
    def _():
        cps = [pltpu.make_async_copy(src, dst, sem.at[n]) for n, (src, dst) in enumerate(pairs)]
        for c in cps:
            c.start()
        for c in cps:
            c.wait()


def _row_ids(i, tm, width):
    return i * tm + lax.broadcasted_iota(jnp.int32, (tm, width), 0)


def _ln_fwd(pre, g, b, rows, first_row):
    mu = jnp.mean(pre, axis=-1, keepdims=True)
    xc = pre - mu
    var = jnp.mean(xc * xc, axis=-1, keepdims=True)
    y = xc * lax.rsqrt(var + LN_EPS) * g + b
    return jnp.where(rows >= first_row, y, 0.0)


ANY = pl.BlockSpec(memory_space=pl.ANY)


def _taps_back(prev8, x, kw):
    xe = jnp.concatenate([prev8, x], axis=0)
    return [pltpu.roll(xe, kw - 1 - j, 0)[8:] for j in range(kw - 1)] + [x]


def _taps_ahead(x, next8, kw):
    n = x.shape[0]
    xe = jnp.concatenate([x, next8], axis=0)
    return [pltpu.roll(xe, n + 8 - (kw - 1 - j), 0)[:n] for j in range(kw - 1)] + [x]


def _conv(cw, taps):
    acc = cw[0:1, :] * taps[0]
    for j in range(1, len(taps)):
        acc = acc + cw[j:j + 1, :] * taps[j]
    return acc


def _linear_dw(x, dy, *, name):
    L, K = x.shape
    N = dy.shape[1]
    tm = TMM if L % TMM == 0 else TM
    tn = LANE
    for d in range(N // LANE, 0, -1):
        if (N // LANE) % d == 0 and K * d * LANE * 4 <= 9 * 1024 * 1024:
            tn = d * LANE
            break

    def body(x_ref, dy_ref, o_ref):
        @pl.when(pl.program_id(1) == 0)
        def _():
            o_ref[...] = jnp.zeros_like(o_ref)
        o_ref[...] += _dot_tn(_bf(x_ref[...]), _bf(dy_ref[...]))

    return pl.pallas_call(
        body, name=name, grid=(N // tn, L // tm), out_shape=S((K, N), f32),
        in_specs=[pl.BlockSpec((tm, K), lambda j, i: (i, 0)), pl.BlockSpec((tm, tn), lambda j, i: (i, j))],
        out_specs=pl.BlockSpec((K, tn), lambda j, i: (0, j)),
        compiler_params=_cp(dimension_semantics=("arbitrary", "arbitrary")))(x, dy)


def _ln_bwd_rows(dout, pre, g, rows, first_row):
    mu = jnp.mean(pre, axis=-1, keepdims=True)
    xc = pre - mu
    rstd = lax.rsqrt(jnp.mean(xc * xc, axis=-1, keepdims=True) + LN_EPS)
    xh = xc * rstd
    dy = jnp.where(rows >= first_row, dout, 0.0)
    dxh = dy * g
    dpre = rstd * (dxh - jnp.mean(dxh, axis=-1, keepdims=True) - xh * jnp.mean(dxh * xh, axis=-1, keepdims=True))
    return dpre, jnp.sum(dy * xh, axis=0, keepdims=True), jnp.sum(dy, axis=0, keepdims=True)


def _gdn_in_fwd(x, head, w_full, conv_w, alog, dtb, *, first_row, H):
    D = x.shape[1]
    L = PADF + x.shape[0]
    W = H * DH
    NW = w_full.shape[1]
    KW = conv_w.shape[0]
    tm = TM
    pb = PADF // tm

    def body(x_ref, head_ref, w_hbm, cw_ref, alog_ref, dtb_ref,
             h_ref, pre_ref, z_ref, raw_ref, q_ref, k_ref, v_ref, beta_ref, g_ref,
             w_vmem, carry, sem):
        i = pl.program_id(0)
        _load_once([(w_hbm, w_vmem)], sem)

        @pl.when(i == 0)
        def _():
            carry[...] = jnp.zeros_like(carry)

        hv = jnp.where(i < pb, head_ref[...], x_ref[...])
        h_ref[...] = hv
        hb = _bf(hv)
        outs = (q_ref, k_ref, v_ref)
        for s in range(3):
            pre = _dot(hb, w_vmem[:, s * W:(s + 1) * W])
            pre_ref[:, s * W:(s + 1) * W] = pre
            c = _conv(cw_ref[:, s * W:(s + 1) * W], _taps_back(carry[s], pre, KW))
            carry[s] = pre[tm - 8:tm, :]
            sl = c * _sigmoid(c)
            if s < 2:
                scale = DH ** -0.5 if s == 0 else 1.0
                for hh in range(H):
                    seg = sl[:, hh * DH:(hh + 1) * DH]
                    r = lax.rsqrt(jnp.sum(seg * seg, axis=-1, keepdims=True) + L2_EPS)
                    outs[s][:, hh * DH:(hh + 1) * DH] = seg * (r * scale)
            else:
                v_ref[...] = sl
        z_ref[...] = _dot(hb, w_vmem[:, 3 * W:4 * W])
        raw = _dot(hb, w_vmem[:, 4 * W:4 * W + 2 * LANE])
        raw_ref[...] = raw
        ok = (_row_ids(i, tm, LANE) >= first_row) & (lax.broadcasted_iota(jnp.int32, (tm, LANE), 1) < H)
        beta_ref[...] = jnp.where(ok, _sigmoid(raw[:, :LANE]), 0.0)
        a = raw[:, LANE:] + dtb_ref[...]
        sp = jnp.maximum(a, 0.0) + jnp.log(1.0 + jnp.exp(-jnp.abs(a)))
        gv = jnp.where(ok, -jnp.exp(alog_ref[...]) * sp, 0.0)
        g_ref[...] = _dot(_chunk_tri(tm, lower=True), gv, HI)

    row = lambda i: (i, 0)
    fix = lambda i: (0, 0)
    out_shape = (S((L, D), f32), S((L, 3 * W), f32), S((L, W), f32), S((L, 2 * LANE), f32),
                 S((L, W), f32), S((L, W), f32), S((L, W), f32), S((L, LANE), f32), S((L, LANE), f32))
    out_specs = (pl.BlockSpec((tm, D), row),
                 pl.BlockSpec((tm, 3 * W), row), pl.BlockSpec((tm, W), row), pl.BlockSpec((tm, 2 * LANE), row),
                 pl.BlockSpec((tm, W), row), pl.BlockSpec((tm, W), row), pl.BlockSpec((tm, W), row),
                 pl.BlockSpec((tm, LANE), row), pl.BlockSpec((tm, LANE), row))
    return pl.pallas_call(
        body, name="gdn_in_fwd", grid=(L // tm,), out_shape=out_shape,
        in_specs=[pl.BlockSpec((tm, D), lambda i: (jnp.maximum(i - pb, 0), 0)),
                  pl.BlockSpec((tm, D), lambda i: (jnp.minimum(i, pb - 1), 0)), ANY, pl.BlockSpec((KW, 3 * W), fix),
                  pl.BlockSpec((1, LANE), fix), pl.BlockSpec((1, LANE), fix)],
        out_specs=out_specs,
        scratch_shapes=[pltpu.VMEM((D, NW), w_full.dtype), pltpu.VMEM((3, 8, W), f32), pltpu.SemaphoreType.DMA((1,))],
        compiler_params=_cp(dimension_semantics=("arbitrary",)))(x, head, w_full, conv_w, alog, dtb)


def _gdn_in_bwd(dq, dk, dv, dz, dg, dbeta, pre, raw, conv_w, alog, dtb, w_full, res, *, first_row, H):
    L = dq.shape[0]
    D = res.shape[1]
    W = H * DH
    KW = conv_w.shape[0]
    tm = TM
    nb = L // tm
    NW = 4 * W + 2 * LANE
    fb = PADF // tm
    alpha = ALPHA

    def body(dq_ref, dk_ref, dv_ref, dz_ref, dg_ref, dbeta_ref, pre_ref, hq_ref, hk_ref, hv_ref, raw_ref,
             cw_ref, alog_ref, dtb_ref, w_hbm, res_ref,
             dproj_ref, dcw_ref, dal_ref, ddt_ref, dx_ref, dfront_ref, w_vmem, carry, tmp, sem):
        i = pl.program_id(0)
        blk = nb - 1 - i
        _load_once([(w_hbm, w_vmem)], sem)

        @pl.when(i == 0)
        def _():
            carry[...] = jnp.zeros_like(carry)
            dcw_ref[...] = jnp.zeros_like(dcw_ref)
            dal_ref[...] = jnp.zeros_like(dal_ref)
            ddt_ref[...] = jnp.zeros_like(ddt_ref)

        halos = (hq_ref, hk_ref, hv_ref)
        douts = (dq_ref, dk_ref, dv_ref)
        for s in range(3):
            sec = slice(s * W, (s + 1) * W)
            pre = pre_ref[:, sec]
            c = _conv(cw_ref[:, sec], _taps_back(jnp.where(blk > 0, halos[s][...], 0.0), pre, KW))
            sig = _sigmoid(c)
            sl = c * sig
            if s < 2:
                scale = DH ** -0.5 if s == 0 else 1.0
                for hh in range(H):
                    hs = slice(hh * DH, (hh + 1) * DH)
                    seg = sl[:, hs]
                    r = lax.rsqrt(jnp.sum(seg * seg, axis=-1, keepdims=True) + L2_EPS)
                    n = seg * r
                    dqs = douts[s][:, hs]
                    tmp[:, hs] = (scale * r) * (dqs - n * jnp.sum(n * dqs, axis=-1, keepdims=True))
                dsl = tmp[...]
            else:
                dsl = dv_ref[...]
            dc = dsl * (sig * (1.0 + c * (1.0 - sig)))
            ahead = _taps_ahead(dc, carry[s], KW)
            carry[s] = dc[0:8, :]
            dproj_ref[:, sec] = _bf(_conv(cw_ref[:, sec], ahead))
            for j in range(KW):
                dcw_ref[j:j + 1, sec] += jnp.sum(ahead[j] * pre, axis=0, keepdims=True)
        dproj_ref[:, 3 * W:4 * W] = _bf(dz_ref[...])
        raw_v = raw_ref[...]
        ok = (_row_ids(blk, tm, LANE) >= first_row) & (lax.broadcasted_iota(jnp.int32, (tm, LANE), 1) < H)
        beta = _sigmoid(raw_v[:, :LANE])
        dbraw = jnp.where(ok, dbeta_ref[...] * beta * (1.0 - beta), 0.0)
        a = raw_v[:, LANE:] + dtb_ref[...]
        sp = jnp.maximum(a, 0.0) + jnp.log(1.0 + jnp.exp(-jnp.abs(a)))
        nea = -jnp.exp(alog_ref[...])
        dgm = jnp.where(ok, _dot(_chunk_tri(tm, lower=False), dg_ref[...], HI), 0.0)
        daraw = dgm * nea * _sigmoid(a)
        dal_ref[0:1, :] += jnp.sum(dgm * nea * sp, axis=0, keepdims=True)
        ddt_ref[0:1, :] += jnp.sum(daraw, axis=0, keepdims=True)
        dproj_ref[:, 4 * W:4 * W + LANE] = _bf(dbraw)
        dproj_ref[:, 4 * W + LANE:4 * W + 2 * LANE] = _bf(daraw)
        dh = alpha * res_ref[...] + _dot_nt(dproj_ref[...], w_vmem[...])

        @pl.when(blk >= fb)
        def _():
            dx_ref[...] = dh

        @pl.when(blk < fb)
        def _():
            dfront_ref[...] = dh

    rev = lambda i: (nb - 1 - i, 0)
    fix = lambda i: (0, 0)

    def halo(col):
        return pl.BlockSpec((8, W), lambda i: (jnp.maximum((nb - 1 - i) * (tm // 8) - 1, 0), col))

    return pl.pallas_call(
        body, name="gdn_in_bwd", grid=(nb,),
        out_shape=(S((L, NW), bf16), S((8, 3 * W), f32), S((8, LANE), f32), S((8, LANE), f32),
                   S((L - PADF, D), f32), S((PADF, D), f32)),
        in_specs=[pl.BlockSpec((tm, W), rev)] * 4 + [pl.BlockSpec((tm, LANE), rev)] * 2
        + [pl.BlockSpec((tm, 3 * W), rev), halo(0), halo(1), halo(2), pl.BlockSpec((tm, 2 * LANE), rev),
           pl.BlockSpec((KW, 3 * W), fix), pl.BlockSpec((1, LANE), fix), pl.BlockSpec((1, LANE), fix),
           ANY, pl.BlockSpec((tm, D), rev)],
        out_specs=(pl.BlockSpec((tm, NW), rev), pl.BlockSpec((8, 3 * W), fix),
                   pl.BlockSpec((8, LANE), fix), pl.BlockSpec((8, LANE), fix),
                   pl.BlockSpec((tm, D), lambda i: (jnp.maximum(nb - 1 - i - fb, 0), 0)),
                   pl.BlockSpec((tm, D), lambda i: (jnp.minimum(nb - 1 - i, fb - 1), 0))),
        scratch_shapes=[pltpu.VMEM((D, NW), w_full.dtype), pltpu.VMEM((3, 8, W), f32), pltpu.VMEM((tm, W), f32),
                        pltpu.SemaphoreType.DMA((1,))],
        compiler_params=_cp(dimension_semantics=("arbitrary",)))(
            dq, dk, dv, dz, dg, dbeta, pre, pre, pre, pre, raw, conv_w, alog, dtb, w_full, res)


def _chunk_tri(n, lower):
    i = lax.broadcasted_iota(jnp.int32, (n, n), 0)
    j = lax.broadcasted_iota(jnp.int32, (n, n), 1)
    sh = int(math.log2(CH))
    same = lax.shift_right_logical(i, sh) == lax.shift_right_logical(j, sh)
    return (same & ((i >= j) if lower else (j >= i))).astype(f32)


def _tri_inv_many(ms, eye):
    ts = [eye - m for m in ms]
    ps = list(ms)
    for _ in range(int(math.log2(CH)) - 1):
        ps = [_dot(p, p, HI3) for p in ps]
        ts = [t + _dot(t, p, HI3) for t, p in zip(ts, ps)]
    return ts


def _chunk_local(q, k, v, gcol, grow, glast, bcol, ii, jj):
    dec = jnp.where(ii >= jj, jnp.exp(jnp.minimum(gcol - grow, 0.0)), 0.0)
    eg = jnp.exp(gcol)
    kb = k * bcol
    kbg = kb * eg
    vb = v * bcol
    qt = q * eg
    kt = k * jnp.exp(glast - gcol)
    kbb, qb, kbf = _bf(kb), _bf(q), _bf(k)
    return dec, eg, kb, kbg, vb, qt, kt, _dot_nt(kbb, kbf), _dot_nt(qb, kbf), jnp.concatenate([kbb, qb], axis=0)


def _delta_fwd(q, k, v, g, beta, z, nw, h, w_out, ln_g, ln_b, *, first_row, H):
    L = q.shape[0]
    W = H * DH
    D = h.shape[1]
    rb = TM
    nc = rb // CH
    nblk = L // rb
    alpha = ALPHA

    def body(q_ref, k_ref, v_ref, g_ref, b_ref, z_ref, nw_ref, h_ref, wout_hbm, lg_ref, lb_ref,
             o_ref, y_ref, s_out, t_out, pre_ref, out_ref, s_scr, wout, sem):
        _load_once([(wout_hbm, wout)], sem)

        @pl.when(pl.program_id(0) == 0)
        def _():
            s_scr[...] = jnp.zeros_like(s_scr)

        ii = lax.broadcasted_iota(jnp.int32, (CH, CH), 0)
        jj = lax.broadcasted_iota(jnp.int32, (CH, CH), 1)
        eye = (ii == jj).astype(f32)
        nwv = nw_ref[...]

        heads = range(H)
        hsl = [slice(hh * DH, (hh + 1) * DH) for hh in heads]

        def chunk(c, carry):
            r0 = pl.multiple_of(c * CH, CH)
            rows = pl.ds(r0, CH)
            gam = g_ref[rows, :]
            gam_t = gam.T
            bb = b_ref[rows, :]
            glast = [gam[CH - 1:CH, hh:hh + 1] for hh in heads]
            loc = [_chunk_local(q_ref[rows, hsl[hh]], k_ref[rows, hsl[hh]], v_ref[rows, hsl[hh]],
                                gam[:, hh:hh + 1], gam_t[hh:hh + 1, :], glast[hh], bb[:, hh:hh + 1], ii, jj)
                   for hh in heads]
            st = [s_scr[hh] for hh in heads]
            zs = [z_ref[rows, hsl[hh]] for hh in heads]
            ts = _tri_inv_many([jnp.where(ii > jj, l[7] * l[0], 0.0) for l in loc], eye)
            us = [_dot(t, l[4], HI3) for t, l in zip(ts, loc)]
            ws = [_dot(t, l[3], HI3) for t, l in zip(ts, loc)]
            stb = [_bf(s) for s in st]
            vn = [u - _dot(_bf(w), sb) for u, w, sb in zip(us, ws, stb)]
            vnb = [_bf(x) for x in vn]
            snew = [s * jnp.exp(gl) + _dot_tn(_bf(l[6]), xb) for s, gl, l, xb in zip(st, glast, loc, vnb)]
            os_ = [_dot(_bf(l[5]), sb) + _dot(_bf(l[8] * l[0]), xb) for l, sb, xb in zip(loc, stb, vnb)]
            for hh in heads:
                o = os_[hh]
                s_out[c, hh] = st[hh]
                t_out[c, hh] = ts[hh]
                s_scr[hh] = snew[hh]
                o_ref[rows, hsl[hh]] = o
                on = o * lax.rsqrt(jnp.mean(o * o, axis=-1, keepdims=True) + RMS_EPS) * nwv
                y_ref[rows, hsl[hh]] = _bf(on * (zs[hh] * _sigmoid(zs[hh])))
            return carry

        lax.fori_loop(0, nc, chunk, 0)
        pre = alpha * h_ref[...] + _dot(y_ref[...], wout[...])
        pre_ref[...] = pre
        out_ref[...] = _ln_fwd(pre, lg_ref[...], lb_ref[...], _row_ids(pl.program_id(0), rb, D), first_row)

    row = lambda i: (i, 0)
    fix = lambda i: (0, 0)
    return pl.pallas_call(
        body, name="delta_fwd", grid=(nblk,),
        out_shape=(S((L, W), f32), S((L, W), bf16), S((L // CH, H, DH, DH), f32), S((L // CH, H, CH, CH), f32),
                   S((L, D), f32), S((L, D), f32)),
        in_specs=[pl.BlockSpec((rb, W), row)] * 3 + [pl.BlockSpec((rb, LANE), row)] * 2
        + [pl.BlockSpec((rb, W), row), pl.BlockSpec((1, DH), fix), pl.BlockSpec((rb, D), row), ANY,
           pl.BlockSpec((1, D), fix), pl.BlockSpec((1, D), fix)],
        out_specs=(pl.BlockSpec((rb, W), row), pl.BlockSpec((rb, W), row),
                   pl.BlockSpec((nc, H, DH, DH), lambda i: (i, 0, 0, 0)),
                   pl.BlockSpec((nc, H, CH, CH), lambda i: (i, 0, 0, 0)),
                   pl.BlockSpec((rb, D), row), pl.BlockSpec((rb, D), row)),
        scratch_shapes=[pltpu.VMEM((H, DH, DH), f32), pltpu.VMEM((W, D), w_out.dtype), pltpu.SemaphoreType.DMA((1,))],
        compiler_params=_cp(dimension_semantics=("arbitrary",)))(q, k, v, g, beta, z, nw, h, w_out, ln_g, ln_b)


def _delta_bwd(dpre, w_out, o, z, nw, q, k, v, g, beta, s_all, t_all, *, H):
    L = q.shape[0]
    W = H * DH
    D = dpre.shape[1]
    rb = TM
    nc = rb // CH
    nblk = L // rb

    def body(dpre_ref, wout_hbm, o_ref, z_ref, nw_ref, q_ref, k_ref, v_ref, g_ref, b_ref, s_ref, t_ref,
             dq_ref, dk_ref, dv_ref, dz_ref, dg_ref, db_ref, dnw_ref, ds_scr, wout, dy_scr, sem):
        _load_once([(wout_hbm, wout)], sem)

        @pl.when(pl.program_id(0) == 0)
        def _():
            ds_scr[...] = jnp.zeros_like(ds_scr)
            dnw_ref[...] = jnp.zeros_like(dnw_ref)

        dy_scr[...] = _dot_nt(_bf(dpre_ref[...]), wout[...])

        ii = lax.broadcasted_iota(jnp.int32, (CH, CH), 0)
        jj = lax.broadcasted_iota(jnp.int32, (CH, CH), 1)
        lane = lax.broadcasted_iota(jnp.int32, (CH, LANE), 1)
        last_row = lax.broadcasted_iota(jnp.int32, (CH, 1), 0) == CH - 1
        nwv = nw_ref[...]

        def chunk(cc, carry):
            c = nc - 1 - cc
            r0 = pl.multiple_of(c * CH, CH)
            rows = pl.ds(r0, CH)
            gam = g_ref[rows, :]
            gam_t = gam.T
            bb = b_ref[rows, :]

            def head(hh):
                hs = slice(hh * DH, (hh + 1) * DH)
                gcol, grow, glast = gam[:, hh:hh + 1], gam_t[hh:hh + 1, :], gam[CH - 1:CH, hh:hh + 1]
                bcol = bb[:, hh:hh + 1]
                qh, kh, vh = q_ref[rows, hs], k_ref[rows, hs], v_ref[rows, hs]
                oh, zh, dyh = o_ref[rows, hs], z_ref[rows, hs], dy_scr[rows, hs]
                t = t_ref[c, hh]
                st = s_ref[c, hh]
                dsn = ds_scr[hh]
                rms = lax.rsqrt(jnp.mean(oh * oh, axis=-1, keepdims=True) + RMS_EPS)
                on = oh * rms
                sig = _sigmoid(zh)
                sz = zh * sig
                dz_ref[rows, hs] = dyh * on * nwv * (sig * (1.0 + zh * (1.0 - sig)))
                dnw = jnp.sum(dyh * on * sz, axis=0, keepdims=True)
                don = dyh * nwv * sz
                do = rms * (don - on * jnp.mean(don * on, axis=-1, keepdims=True))
                dec, eg, kb, kbg, vb, qt, kt, kk, qk, kqb = _chunk_local(qh, kh, vh, gcol, grow, glast, bcol, ii, jj)
                stb, dsnb, dob, tb, kbgb = _bf(st), _bf(dsn), _bf(do), _bf(t), _bf(kbg)
                r = vb - _dot(kbgb, stb)
                mm = jnp.where(ii > jj, kk * dec, 0.0)
                attn = qk * dec
                yield
                vn = _dot(t, r, HI3)
                dvn = _dot_tn(_bf(attn), dob) + _dot(_bf(kt), dsnb)
                egl = jnp.exp(glast)
                ekt = jnp.exp(glast - gcol)
                yield
                vnb, dvnb = _bf(vn), _bf(dvn)
                dattn = jnp.where(ii >= jj, _dot_nt(dob, vnb), 0.0)
                dkt = _dot_nt(vnb, dsnb)
                dvb = _dot_tn(tb, dvnb)
                dt = _dot_nt(dvnb, _bf(r))
                dglast = egl * jnp.sum(jnp.sum(dsn * st, axis=0, keepdims=True), axis=1, keepdims=True)
                yield
                dv_ref[rows, hs] = dvb * bcol
                dod = jnp.concatenate([dob, -_bf(dvb)], axis=0)
                ds_scr[hh] = egl * dsn + _dot_tn(jnp.concatenate([_bf(qt), kbgb], axis=0), dod)
                both = _dot_nt(dod, stb)
                dqt, dkbg = both[:CH], both[CH:]
                x = _dot_nt(_bf(dt), tb)
                yield
                dm = jnp.where(ii > jj, -_dot_tn(tb, _bf(x)), 0.0)
                dkk = dm * dec
                dqk = dattn * dec
                e = dm * mm + dattn * attn
                dgam = jnp.sum(e, axis=1, keepdims=True) - jnp.sum(e.T, axis=1, keepdims=True)
                dd = _bf(jnp.concatenate([dkk, dqk], axis=0))
                both = _dot(dd, _bf(kh))
                dkb = both[:CH] + dkbg * eg
                dk_ref[rows, hs] = _dot_tn(dd, kqb) + dkt * ekt + dkb * bcol
                dq_ref[rows, hs] = both[CH:] + dqt * eg
                yield
                dktkt = dkt * kt
                dgam = dgam + jnp.sum(dqt * qt - dktkt + dkbg * kbg, axis=1, keepdims=True)
                dglast = dglast + jnp.sum(jnp.sum(dktkt, axis=0, keepdims=True), axis=1, keepdims=True)
                dgam = dgam + jnp.where(last_row, dglast, 0.0)
                dbeta = jnp.sum(dkb * kh + dvb * vh, axis=1, keepdims=True)
                return dgam, dbeta, dnw

            res = [None] * H
            for h0 in range(0, H, BWD_HEAD_GROUP):
                group = range(h0, min(h0 + BWD_HEAD_GROUP, H))
                gens = {hh: head(hh) for hh in group}
                while any(res[hh] is None for hh in group):
                    for hh in group:
                        try:
                            next(gens[hh])
                        except StopIteration as stop:
                            res[hh] = stop.value
            dgam_all = jnp.zeros((CH, LANE), f32)
            dbeta_all = jnp.zeros((CH, LANE), f32)
            dnw_acc = jnp.zeros((1, DH), f32)
            for hh in range(H):
                dgam, dbeta, dnw = res[hh]
                dgam_all = dgam_all + jnp.where(lane == hh, dgam, 0.0)
                dbeta_all = dbeta_all + jnp.where(lane == hh, dbeta, 0.0)
                dnw_acc = dnw_acc + dnw
            dg_ref[rows, :] = dgam_all
            db_ref[rows, :] = dbeta_all
            dnw_ref[0:1, :] += dnw_acc
            return carry

        lax.fori_loop(0, nc, chunk, 0)

    rev = lambda i: (nblk - 1 - i, 0)
    rev4 = lambda i: (nblk - 1 - i, 0, 0, 0)
    fix = lambda i: (0, 0)
    wide = pl.BlockSpec((rb, W), rev)
    thin = pl.BlockSpec((rb, LANE), rev)
    return pl.pallas_call(
        body, name="delta_bwd", grid=(nblk,),
        out_shape=(S((L, W), f32),) * 4 + (S((L, LANE), f32),) * 2 + (S((8, DH), f32),),
        in_specs=[pl.BlockSpec((rb, D), rev), ANY, wide, wide, pl.BlockSpec((1, DH), fix), wide, wide, wide, thin, thin,
                  pl.BlockSpec((nc, H, DH, DH), rev4), pl.BlockSpec((nc, H, CH, CH), rev4)],
        out_specs=(wide,) * 4 + (thin, thin, pl.BlockSpec((8, DH), fix)),
        scratch_shapes=[pltpu.VMEM((H, DH, DH), f32), pltpu.VMEM((W, D), w_out.dtype), pltpu.VMEM((rb, W), f32),
                        pltpu.SemaphoreType.DMA((1,))],
        compiler_params=_cp(dimension_semantics=("arbitrary",)))(
            dpre, w_out, o, z, nw, q, k, v, g, beta, s_all, t_all)


def _sc_fwd(h, w_in, conv_w, w_out, g, b, *, first_row):
    L, D = h.shape
    W = w_out.shape[0]
    KW = conv_w.shape[0]
    tm = TM
    alpha = ALPHA

    def body(h_ref, win_hbm, cw_ref, wout_hbm, g_ref, b_ref, proj_ref, bu_ref, pre_ref, out_ref,
             win, wout, carry, sem):
        i = pl.program_id(0)
        _load_once([(win_hbm, win), (wout_hbm, wout)], sem)

        @pl.when(i == 0)
        def _():
            carry[...] = jnp.zeros_like(carry)

        hv = h_ref[...]
        hb = _bf(hv)
        bg = _dot(hb, win[:, 0:W])
        cg = _dot(hb, win[:, W:2 * W])
        xv = _dot(hb, win[:, 2 * W:3 * W])
        proj_ref[:, 0:W] = bg
        proj_ref[:, W:2 * W] = cg
        proj_ref[:, 2 * W:3 * W] = xv
        p = cg * xv
        u = _conv(cw_ref[...], _taps_back(carry[...], p, KW))
        carry[...] = p[tm - 8:tm, :]
        bu = _bf(bg * u)
        bu_ref[...] = bu
        pre = alpha * hv + _dot(bu, wout[...])
        pre_ref[...] = pre
        out_ref[...] = _ln_fwd(pre, g_ref[...], b_ref[...], _row_ids(i, tm, D), first_row)

    row = lambda i: (i, 0)
    fix = lambda i: (0, 0)
    return pl.pallas_call(
        body, name="sc_fwd", grid=(L // tm,),
        out_shape=(S((L, 3 * W), f32), S((L, W), bf16), S((L, D), f32), S((L, D), f32)),
        in_specs=[pl.BlockSpec((tm, D), row), ANY, pl.BlockSpec((KW, W), fix), ANY,
                  pl.BlockSpec((1, D), fix), pl.BlockSpec((1, D), fix)],
        out_specs=(pl.BlockSpec((tm, 3 * W), row), pl.BlockSpec((tm, W), row),
                   pl.BlockSpec((tm, D), row), pl.BlockSpec((tm, D), row)),
        scratch_shapes=[pltpu.VMEM((D, 3 * W), w_in.dtype), pltpu.VMEM((W, D), w_out.dtype),
                        pltpu.VMEM((8, W), f32), pltpu.SemaphoreType.DMA((2,))],
        compiler_params=_cp(dimension_semantics=("arbitrary",)))(h, w_in, conv_w, w_out, g, b)


def _sc_bwd(dpre, proj, conv_w, w_out, w_in, pre_in, g_in, *, first_row):
    L, D = dpre.shape
    W = w_out.shape[0]
    KW = conv_w.shape[0]
    tm = TM
    nb = L // tm
    alpha = ALPHA

    def body(dpre_ref, proj_ref, hc_ref, hx_ref, cw_ref, wout_hbm, win_hbm, pin_ref, g_ref,
             dproj_ref, dcw_ref, dpin_ref, dg_ref, db_ref, wout, win, carry, sem):
        i = pl.program_id(0)
        blk = nb - 1 - i
        _load_once([(wout_hbm, wout), (win_hbm, win)], sem)

        @pl.when(i == 0)
        def _():
            carry[...] = jnp.zeros_like(carry)
            dcw_ref[...] = jnp.zeros_like(dcw_ref)
            dg_ref[...] = jnp.zeros_like(dg_ref)
            db_ref[...] = jnp.zeros_like(db_ref)

        bg, cg, xv = proj_ref[:, 0:W], proj_ref[:, W:2 * W], proj_ref[:, 2 * W:3 * W]
        p = cg * xv
        u = _conv(cw_ref[...], _taps_back(jnp.where(blk > 0, hc_ref[...] * hx_ref[...], 0.0), p, KW))
        dpre_v = dpre_ref[...]
        d = _dot_nt(_bf(dpre_v), wout[...])
        dproj_ref[:, 0:W] = _bf(d * u)
        du = d * bg
        ahead = _taps_ahead(du, carry[...], KW)
        carry[...] = du[0:8, :]
        dp = _conv(cw_ref[...], ahead)
        for j in range(KW):
            dcw_ref[j:j + 1, :] += jnp.sum(ahead[j] * p, axis=0, keepdims=True)
        dproj_ref[:, W:2 * W] = _bf(dp * xv)
        dproj_ref[:, 2 * W:3 * W] = _bf(dp * cg)
        dh = alpha * dpre_v + _dot_nt(dproj_ref[...], win[...])
        dpin, dg, dbias = _ln_bwd_rows(dh, pin_ref[...], g_ref[...], _row_ids(blk, tm, D), first_row)
        dpin_ref[...] = dpin
        dg_ref[0:1, :] += dg
        db_ref[0:1, :] += dbias

    rev = lambda i: (nb - 1 - i, 0)
    fix = lambda i: (0, 0)

    def halo(col):
        return pl.BlockSpec((8, W), lambda i: (jnp.maximum((nb - 1 - i) * (tm // 8) - 1, 0), col))

    return pl.pallas_call(
        body, name="sc_bwd", grid=(nb,),
        out_shape=(S((L, 3 * W), bf16), S((8, W), f32), S((L, D), f32), S((8, D), f32), S((8, D), f32)),
        in_specs=[pl.BlockSpec((tm, D), rev), pl.BlockSpec((tm, 3 * W), rev), halo(1), halo(2),
                  pl.BlockSpec((KW, W), fix), ANY, ANY, pl.BlockSpec((tm, D), rev), pl.BlockSpec((1, D), fix)],
        out_specs=(pl.BlockSpec((tm, 3 * W), rev), pl.BlockSpec((8, W), fix), pl.BlockSpec((tm, D), rev),
                   pl.BlockSpec((8, D), fix), pl.BlockSpec((8, D), fix)),
        scratch_shapes=[pltpu.VMEM((W, D), w_out.dtype), pltpu.VMEM((D, 3 * W), w_in.dtype), pltpu.VMEM((8, W), f32),
                        pltpu.SemaphoreType.DMA((2,))],
        compiler_params=_cp(dimension_semantics=("arbitrary",)))(
            dpre, proj, proj, proj, conv_w, w_out, w_in, pre_in, g_in)


def _ffn_cols(F):
    fc = F
    for cand in (1408, 1024, 512, 256, 128):
        if F % cand == 0:
            fc = cand
            break
    return fc


def _ffn_fwd(h, w_up, conv_w, w_down, g, b, *, first_row, name):
    L, D = h.shape
    F = w_down.shape[0]
    KW = conv_w.shape[0]
    tm = TM
    fc = _ffn_cols(F)
    alpha = ALPHA

    def body(h_ref, wup_hbm, cw_ref, wdn_hbm, g_ref, b_ref, up_ref, a_ref, pre_ref, out_ref,
             wup, wdn, carry, sem):
        i = pl.program_id(0)
        _load_once([(wup_hbm, wup), (wdn_hbm, wdn)], sem)

        @pl.when(i == 0)
        def _():
            carry[...] = jnp.zeros_like(carry)

        hv = h_ref[...]
        hb = _bf(hv)
        pre = alpha * hv
        for c0 in range(0, F, fc):
            cs = slice(c0, c0 + fc)
            u = _dot(hb, wup[:, cs])
            gate = _dot(hb, wup[:, F + c0:F + c0 + fc])
            up_ref[:, cs] = u
            up_ref[:, F + c0:F + c0 + fc] = gate
            uc = _conv(cw_ref[:, cs], _taps_back(carry[:, cs], u, KW))
            carry[:, cs] = u[tm - 8:tm, :]
            ab = _bf(uc * _sigmoid(uc) * gate)
            a_ref[:, cs] = ab
            pre = pre + _dot(ab, wdn[cs, :])
        pre_ref[...] = pre
        out_ref[...] = _ln_fwd(pre, g_ref[...], b_ref[...], _row_ids(i, tm, D), first_row)

    row = lambda i: (i, 0)
    fix = lambda i: (0, 0)
    return pl.pallas_call(
        body, name=name, grid=(L // tm,),
        out_shape=(S((L, 2 * F), f32), S((L, F), bf16), S((L, D), f32), S((L, D), f32)),
        in_specs=[pl.BlockSpec((tm, D), row), ANY, pl.BlockSpec((KW, F), fix), ANY,
                  pl.BlockSpec((1, D), fix), pl.BlockSpec((1, D), fix)],
        out_specs=(pl.BlockSpec((tm, 2 * F), row), pl.BlockSpec((tm, F), row),
                   pl.BlockSpec((tm, D), row), pl.BlockSpec((tm, D), row)),
        scratch_shapes=[pltpu.VMEM((D, 2 * F), w_up.dtype), pltpu.VMEM((F, D), w_down.dtype),
                        pltpu.VMEM((8, F), f32), pltpu.SemaphoreType.DMA((2,))],
        compiler_params=_cp(dimension_semantics=("arbitrary",)))(h, w_up, conv_w, w_down, g, b)


def _ffn_bwd(dpre, up, w_down, conv_w, w_up, pre_in, g_in, *, first_row, name):
    L, D = dpre.shape
    F = w_down.shape[0]
    KW = conv_w.shape[0]
    tm = TM
    nb = L // tm
    fc = F
    alpha = ALPHA

    def body(dpre_ref, up_ref, halo_ref, wdn_hbm, cw_ref, wup_hbm, pin_ref, g_ref,
             dup_ref, dcw_ref, dpin_ref, dg_ref, db_ref, wdn, wup, carry, sem):
        i = pl.program_id(0)
        blk = nb - 1 - i
        _load_once([(wdn_hbm, wdn), (wup_hbm, wup)], sem)

        @pl.when(i == 0)
        def _():
            carry[...] = jnp.zeros_like(carry)
            dcw_ref[...] = jnp.zeros_like(dcw_ref)
            dg_ref[...] = jnp.zeros_like(dg_ref)
            db_ref[...] = jnp.zeros_like(db_ref)

        dpre_v = dpre_ref[...]
        db = _bf(dpre_v)
        dh = alpha * dpre_v
        for c0 in range(0, F, fc):
            cs = slice(c0, c0 + fc)
            gs_ = slice(F + c0, F + c0 + fc)
            da = _dot_nt(db, wdn[cs, :])
            gate = up_ref[:, gs_]
            u = up_ref[:, cs]
            uc = _conv(cw_ref[:, cs], _taps_back(jnp.where(blk > 0, halo_ref[:, cs], 0.0), u, KW))
            sig = _sigmoid(uc)
            dgate = _bf(da * (uc * sig))
            dup_ref[:, gs_] = dgate
            duc = da * gate * (sig * (1.0 + uc * (1.0 - sig)))
            ahead = _taps_ahead(duc, carry[:, cs], KW)
            carry[:, cs] = duc[0:8, :]
            du = _bf(_conv(cw_ref[:, cs], ahead))
            dup_ref[:, cs] = du
            for j in range(KW):
                dcw_ref[j:j + 1, cs] += jnp.sum(ahead[j] * u, axis=0, keepdims=True)
            dh = dh + _dot_nt(du, wup[:, cs]) + _dot_nt(dgate, wup[:, gs_])
        dpin, dg, dbias = _ln_bwd_rows(dh, pin_ref[...], g_ref[...], _row_ids(blk, tm, D), first_row)
        dpin_ref[...] = dpin
        dg_ref[0:1, :] += dg
        db_ref[0:1, :] += dbias

    rev = lambda i: (nb - 1 - i, 0)
    fix = lambda i: (0, 0)
    return pl.pallas_call(
        body, name=name, grid=(nb,),
        out_shape=(S((L, 2 * F), bf16), S((8, F), f32), S((L, D), f32), S((8, D), f32), S((8, D), f32)),
        in_specs=[pl.BlockSpec((tm, D), rev), pl.BlockSpec((tm, 2 * F), rev),
                  pl.BlockSpec((8, F), lambda i: (jnp.maximum((nb - 1 - i) * (tm // 8) - 1, 0), 0)),
                  ANY, pl.BlockSpec((KW, F), fix), ANY, pl.BlockSpec((tm, D), rev), pl.BlockSpec((1, D), fix)],
        out_specs=(pl.BlockSpec((tm, 2 * F), rev), pl.BlockSpec((8, F), fix), pl.BlockSpec((tm, D), rev),
                   pl.BlockSpec((8, D), fix), pl.BlockSpec((8, D), fix)),
        scratch_shapes=[pltpu.VMEM((F, D), w_down.dtype), pltpu.VMEM((D, 2 * F), w_up.dtype), pltpu.VMEM((8, F), f32),
                        pltpu.SemaphoreType.DMA((2,))],
        compiler_params=_cp(dimension_semantics=("arbitrary",)))(dpre, up, up, w_down, conv_w, w_up, pre_in, g_in)


def _loss_head(h, target, pre, g, *, first_row):
    L, D = h.shape
    tm = TM
    pb = PADF // tm

    def body(h_ref, t_ref, pre_ref, g_ref, dpre_ref, dg_ref, db_ref, loss_ref):
        i = pl.program_id(0)

        @pl.when(i == 0)
        def _():
            loss_ref[...] = jnp.zeros_like(loss_ref)
            dg_ref[...] = jnp.zeros_like(dg_ref)
            db_ref[...] = jnp.zeros_like(db_ref)

        valid = i >= pb
        err = h_ref[...] - t_ref[...]
        dh = jnp.where(valid, err * (1.0 / D), 0.0)
        part = 0.5 * jnp.sum(jnp.sum(err * err, axis=-1, keepdims=True) * (1.0 / D), axis=0, keepdims=True)
        loss_ref[...] += jnp.where(valid, part, 0.0)
        dpre, dg, db = _ln_bwd_rows(dh, pre_ref[...], g_ref[...], _row_ids(i, tm, D), first_row)
        dpre_ref[...] = dpre
        dg_ref[0:1, :] += dg
        db_ref[0:1, :] += db

    row = lambda i: (i, 0)
    fix = lambda i: (0, 0)
    return pl.pallas_call(
        body, name="loss_head", grid=(L // tm,),
        out_shape=(S((L, D), f32), S((8, D), f32), S((8, D), f32), S((8, LANE), f32)),
        in_specs=[pl.BlockSpec((tm, D), row), pl.BlockSpec((tm, D), lambda i: (jnp.maximum(i - pb, 0), 0)),
                  pl.BlockSpec((tm, D), row), pl.BlockSpec((1, D), fix)],
        out_specs=(pl.BlockSpec((tm, D), row), pl.BlockSpec((8, D), fix), pl.BlockSpec((8, D), fix),
                   pl.BlockSpec((8, LANE), fix)),
        compiler_params=_cp(dimension_semantics=("arbitrary",)))(h, target, pre, g)


def _adamw(g_terms, w, m, v, *, name):
    R, C = w.shape
    tr = _row_tile(R)
    n = len(g_terms)
    c1 = 1.0 - ADAM_B1 ** ADAM_STEP
    c2 = 1.0 - ADAM_B2 ** ADAM_STEP

    def body(*refs):
        g = refs[0][...].astype(f32)
        for r in refs[1:n]:
            g = g + r[...].astype(f32)
        w_ref, m_ref, v_ref, g_out, d_out, m_out, v_out = refs[n:]
        mn = ADAM_B1 * m_ref[...] + (1.0 - ADAM_B1) * g
        vn = ADAM_B2 * v_ref[...] + (1.0 - ADAM_B2) * (g * g)
        g_out[...] = g
        m_out[...] = mn
        v_out[...] = vn
        d_out[...] = -ADAM_LR * ((mn / c1) / (jnp.sqrt(vn / c2) + ADAM_EPS) + ADAM_WD * w_ref[...])

    spec = pl.BlockSpec((tr, C), lambda i: (i, 0))
    return pl.pallas_call(
        body, name=name, grid=(R // tr,), out_shape=(S((R, C), f32),) * 4,
        in_specs=[spec] * (n + 3), out_specs=(spec,) * 4,
        compiler_params=_cp(dimension_semantics=("arbitrary",)))(*g_terms, w, m, v)


def _sum_devices(x):
    n, R, C = x.shape

    def body(x_ref, o_ref):
        acc = x_ref[0]
        for d in range(1, n):
            acc = acc + x_ref[d]
        o_ref[...] = acc

    return pl.pallas_call(body, name="sum_devices", out_shape=S((R, C), f32), compiler_params=_cp())(x)


def _row_tile(R):
    for step in (16, 8):
        for t in range(256, 0, -step):
            if R % t == 0:
                return t
    return R


def _adamw_direct(s32s, recvs, w, m, v, me, *, name):
    L, K, n = w.shape
    tk = _row_tile(K)
    c1 = 1.0 - ADAM_B1 ** ADAM_STEP
    c2 = 1.0 - ADAM_B2 ** ADAM_STEP

    def body(me_ref, *refs):
        own_refs, recv_refs = refs[:L], refs[L:2 * L]
        w_ref, m_ref, v_ref, g_out, d_out, m_out, v_out = refs[2 * L:]
        for li in range(L):
            @pl.when(pl.program_id(0) == li)
            def _(li=li):
                g = own_refs[li][0, 0]
                for d in range(N_DEV):
                    g = g + recv_refs[li][d, 0].astype(f32)
                mn = ADAM_B1 * m_ref[0] + (1.0 - ADAM_B1) * g
                vn = ADAM_B2 * v_ref[0] + (1.0 - ADAM_B2) * (g * g)
                g_out[0] = g
                m_out[0] = mn
                v_out[0] = vn
                d_out[0] = -ADAM_LR * ((mn / c1) / (jnp.sqrt(vn / c2) + ADAM_EPS) + ADAM_WD * w_ref[0])

    own = pl.BlockSpec((1, tk, n), lambda l, i, ix: (l, i, 0))
    grid_spec = pltpu.PrefetchScalarGridSpec(
        num_scalar_prefetch=1, grid=(L, K // tk),
        in_specs=[pl.BlockSpec((1, 1, tk, n), lambda l, i, ix: (ix[0], 0, i, 0))] * L
        + [pl.BlockSpec((N_DEV, 1, tk, n), lambda l, i, ix: (0, 0, i, 0))] * L + [own, own, own],
        out_specs=(own,) * 4)
    return pl.pallas_call(
        body, name=name, grid_spec=grid_spec, out_shape=(S((L, K, n), f32),) * 4,
        compiler_params=_cp(dimension_semantics=("arbitrary", "arbitrary")))(me, *s32s, *recvs, w, m, v)


def _col_segments(n, mapping):
    segs = []
    for p in range(N_DEV):
        lo, hi = p * n, (p + 1) * n
        out = []
        for c0, c1, e0 in mapping:
            a, b = max(lo, c0), min(hi, c1)
            if a < b:
                out.append((a - lo, e0 + (a - c0), b - a))
        segs.append(out)
    return segs


def _assemble_cols(gathered, mapping, n_out, *, name):
    _, L, K, n = gathered.shape
    tk = _row_tile(K)
    segs = _col_segments(n, mapping)
    covered = sum(w for s in segs for (_, _, w) in s)

    def body(g_ref, o_ref):
        if covered != n_out:
            o_ref[...] = jnp.zeros_like(o_ref)
        for p in range(N_DEV):
            for s0, d0, w in segs[p]:
                o_ref[0, :, d0:d0 + w] = g_ref[p, 0, :, s0:s0 + w]

    return pl.pallas_call(
        body, name=name, grid=(L, K // tk), out_shape=S((L, K, n_out), gathered.dtype),
        in_specs=[pl.BlockSpec((N_DEV, 1, tk, n), lambda l, i: (0, l, i, 0))],
        out_specs=pl.BlockSpec((1, tk, n_out), lambda l, i: (l, i, 0)),
        compiler_params=_cp(dimension_semantics=("arbitrary", "arbitrary")))(gathered)


def _split_cols(dws, mapping, n, *, name):
    L = len(dws)
    K, n_in = dws[0].shape
    tk = _row_tile(K)
    segs = _col_segments(n, mapping)

    def body(*refs):
        ins, o32, o16 = refs[:L], refs[L], refs[L + 1]
        for li in range(L):
            @pl.when(pl.program_id(0) == li)
            def _(li=li):
                for p in range(N_DEV):
                    for s0, d0, w in segs[p]:
                        val = ins[li][:, d0:d0 + w]
                        o32[p, 0, :, s0:s0 + w] = val
                        o16[p, 0, :, s0:s0 + w] = _bf(val)

    out = pl.BlockSpec((N_DEV, 1, tk, n), lambda l, i: (0, l, i, 0))
    return pl.pallas_call(
        body, name=name, grid=(L, K // tk), out_shape=(S((N_DEV, L, K, n), f32), S((N_DEV, L, K, n), bf16)),
        in_specs=[pl.BlockSpec((tk, n_in), lambda l, i: (i, 0))] * L, out_specs=(out, out),
        compiler_params=_cp(dimension_semantics=("arbitrary", "arbitrary")))(*dws)


def _split_rows(dws, k, *, name):
    L = len(dws)
    N = dws[0].shape[1]

    def body(*refs):
        ins, o32, o16 = refs[:L], refs[L], refs[L + 1]
        for li in range(L):
            @pl.when(pl.program_id(0) == li)
            def _(li=li):
                val = ins[li][...]
                o32[0, 0] = val
                o16[0, 0] = _bf(val)

    out = pl.BlockSpec((1, 1, k, N), lambda l, p: (p, l, 0, 0))
    return pl.pallas_call(
        body, name=name, grid=(L, N_DEV), out_shape=(S((N_DEV, L, k, N), f32), S((N_DEV, L, k, N), bf16)),
        in_specs=[pl.BlockSpec((k, N), lambda l, p: (p, 0))] * L, out_specs=(out, out),
        compiler_params=_cp(dimension_semantics=("arbitrary", "arbitrary")))(*dws)


def _rows_full(gathered):
    _, L, k, N = gathered.shape
    return jnp.transpose(gathered, (1, 0, 2, 3)).reshape(L, N_DEV * k, N)


def _all_gather(xs, *, name):
    na = len(xs)

    def body(*refs):
        x_refs, out_refs = refs[:na], refs[na:2 * na]
        send_sems, recv_sems, local_sems = refs[2 * na:]
        mx, my, mc = lax.axis_index("x"), lax.axis_index("y"), lax.axis_index("c")
        me, sibling = (mx, my, mc), (mx, my, 1 - mc)
        chips = [(1 - mx, my), (mx, 1 - my), (1 - mx, 1 - my)]

        def slot(a, px, py, pc):
            return out_refs[a].at[4 * px + 2 * py + pc]

        def copy(a, kk, block, to, src=None):
            return pltpu.make_async_remote_copy(
                src_ref=slot(a, *block) if src is None else src, dst_ref=slot(a, *block),
                send_sem=send_sems.at[7 * a + kk], recv_sem=recv_sems.at[7 * a + kk], device_id=to, device_id_type=MESH)

        mine = [pltpu.make_async_copy(x_refs[a], slot(a, *me), local_sems.at[a]) for a in range(na)]
        for cp in mine:
            cp.start()
        first = []
        for a in range(na):
            first.append(copy(a, 0, me, sibling, src=x_refs[a]))
            first += [copy(a, 1 + j, me, (*chip, mc), src=x_refs[a]) for j, chip in enumerate(chips)]
        for cp in first:
            cp.start()
        passed = []
        for j, chip in enumerate(chips):
            for a in range(na):
                copy(a, 1 + j, (*chip, mc), me).wait_recv()
                fwd = copy(a, 4 + j, (*chip, mc), sibling)
                fwd.start()
                passed.append(fwd)
        for a in range(na):
            copy(a, 0, sibling, me).wait_recv()
            for j, chip in enumerate(chips):
                copy(a, 4 + j, (*chip, 1 - mc), me).wait_recv()
        for cp in first + passed:
            cp.wait_send()
        for cp in mine:
            cp.wait()

    return pl.pallas_call(
        body, name=name, out_shape=tuple(S((N_DEV,) + x.shape, x.dtype) for x in xs),
        in_specs=[ANY] * na, out_specs=(ANY,) * na,
        scratch_shapes=[pltpu.SemaphoreType.DMA((7 * na,)), pltpu.SemaphoreType.DMA((7 * na,)),
                        pltpu.SemaphoreType.DMA((na,))],
        compiler_params=pltpu.CompilerParams(has_side_effects=True))(*xs)


_FLIPS = [(fx, fy, fc) for fx in (0, 1) for fy in (0, 1) for fc in (0, 1)][1:]


def _flip_peer(flip):
    x, y, c = lax.axis_index("x"), lax.axis_index("y"), lax.axis_index("c")
    return tuple(1 - a if f else a for a, f in zip((x, y, c), flip))


def _dev_index(p):
    return 4 * p[0] + 2 * p[1] + p[2]


HBM_SPEC = pl.BlockSpec(memory_space=pltpu.HBM)
SEM_SPEC = pl.BlockSpec(memory_space=pltpu.SEMAPHORE)


def _direct_start(srcs, lands, per_peer, *, name):
    na = len(srcs)

    def body(*refs):
        src_refs, land_refs = refs[:na], refs[na:2 * na]
        send_sems, recv_sems = refs[2 * na], refs[2 * na + 1]
        token = refs[-1]
        me = _dev_index((lax.axis_index("x"), lax.axis_index("y"), lax.axis_index("c")))
        for a in range(na):
            for r, flip in enumerate(_FLIPS):
                peer = _flip_peer(flip)
                src = src_refs[a].at[_dev_index(peer)] if per_peer else src_refs[a]
                pltpu.make_async_remote_copy(
                    src_ref=src, dst_ref=land_refs[a].at[me], send_sem=send_sems.at[7 * a + r],
                    recv_sem=recv_sems.at[7 * a + r], device_id=peer, device_id_type=MESH).start()
        token[...] = jnp.zeros_like(token)

    hbm = lambda t: pltpu.with_memory_space_constraint(t, pltpu.HBM)
    out = pl.pallas_call(
        body, name=name,
        out_shape=(pltpu.SemaphoreType.DMA((7 * na,)), pltpu.SemaphoreType.DMA((7 * na,)))
        + tuple(pltpu.HBM(t.shape, t.dtype) for t in list(srcs) + list(lands)) + (S((8, LANE), f32),),
        in_specs=[HBM_SPEC] * (2 * na),
        out_specs=(SEM_SPEC, SEM_SPEC) + (HBM_SPEC,) * (2 * na) + (pl.BlockSpec(memory_space=pltpu.VMEM),),
        input_output_aliases={i: 2 + i for i in range(2 * na)},
        compiler_params=pltpu.CompilerParams(has_side_effects=pltpu.SideEffectType.DATAFLOW_SIDE_EFFECTING))(
            *[hbm(t) for t in srcs], *[hbm(t) for t in lands])
    return out[0], out[1], list(out[2:2 + na]), list(out[2 + na:2 + 2 * na]), out[-1]


def _direct_wait(send_sems, recv_sems, srcs, lands, per_peer, after, *, name):
    na = len(srcs)

    def body(*refs):
        src_refs, land_refs = refs[:na], refs[na:2 * na]
        ssem, rsem = refs[2 * na], refs[2 * na + 1]
        me = _dev_index((lax.axis_index("x"), lax.axis_index("y"), lax.axis_index("c")))
        for a in range(na):
            for r, flip in enumerate(_FLIPS):
                peer = _flip_peer(flip)
                src = src_refs[a].at[_dev_index(peer)] if per_peer else src_refs[a]
                cp = pltpu.make_async_remote_copy(
                    src_ref=src, dst_ref=land_refs[a].at[me], send_sem=ssem.at[7 * a + r],
                    recv_sem=rsem.at[7 * a + r], device_id=peer, device_id_type=MESH)
                cp.wait_send()
                cp.wait_recv()

    out = pl.pallas_call(
        body, name=name, out_shape=tuple(pltpu.HBM(t.shape, t.dtype) for t in list(srcs) + list(lands)),
        in_specs=[HBM_SPEC] * (2 * na) + [SEM_SPEC, SEM_SPEC, ANY], out_specs=(HBM_SPEC,) * (2 * na),
        input_output_aliases={i: i for i in range(2 * na)},
        compiler_params=pltpu.CompilerParams(has_side_effects=pltpu.SideEffectType.DATAFLOW_SIDE_EFFECTING))(
            *srcs, *lands, send_sems, recv_sems, after)
    return list(out[:na]), list(out[na:])


def _pack_small(parts, width):
    rows, offs, r = [], [], 0
    for a in parts:
        n = a.size
        nr = -(-n // width)
        flat = a.reshape(-1).astype(f32)
        if nr * width != n:
            flat = jnp.pad(flat, (0, nr * width - n))
        rows.append(flat.reshape(nr, width))
        offs.append((r, nr))
        r += nr
    buf = jnp.concatenate(rows, axis=0)
    pad = (-r) % 8
    if pad:
        buf = jnp.pad(buf, ((0, pad), (0, 0)))
    return buf, offs


def _unpack_small(buf, off, shape):
    r, nr = off
    return buf[r:r + nr].reshape(-1)[:math.prod(shape)].reshape(shape)


def _local_step(x, target, meta, a_w_in, a_w_out, small, start_token, late_weights, grads_ready):
    SEQ, D = x.shape
    n_meta = meta.shape[0]
    first_row = PADF - n_meta
    H = small["a_log"].shape[-1]

    head = jnp.concatenate([jnp.zeros((first_row, D), f32), meta], axis=0)

    def lanes(a):
        return jnp.pad(a.reshape(1, -1), ((0, 0), (0, LANE - a.size)))

    def after_token(a, token):
        return a if token is None else a + token[0:1, 0:1]

    alog, dtb = after_token(lanes(small["a_log"][0]), start_token), lanes(small["a_dt_bias"][0])
    a_conv, b_conv = small["a_conv"][0], small["b_conv"][0]
    nw = small["a_norm"][0].reshape(1, DH)
    lmg, lmb, lfg, lfb = small["ln_mix_g"], small["ln_mix_b"], small["ln_ffn_g"], small["ln_ffn_b"]

    h0, pre_a, z, raw, q, k, v, beta, g = _gdn_in_fwd(x, head, a_w_in, a_conv, alog, dtb, first_row=first_row, H=H)
    o, y, s_all, t_all, pre1, h1 = _delta_fwd(q, k, v, g, beta, z, nw, h0, a_w_out, lmg[0:1], lmb[0:1],
                                              first_row=first_row, H=H)
    wts = late_weights(h1)
    up0, act0, pre2, h2 = _ffn_fwd(h1, wts["ffn_w_up"][0], small["ffn_conv"][0], wts["ffn_w_down"][0],
                                   lfg[0:1], lfb[0:1], first_row=first_row, name="ffn_fwd0")
    proj_b, bu, pre3, h3 = _sc_fwd(h2, wts["b_w_in"], b_conv, wts["b_w_out"], lmg[1:2], lmb[1:2], first_row=first_row)
    up1, act1, pre4, h4 = _ffn_fwd(h3, wts["ffn_w_up"][1], small["ffn_conv"][1], wts["ffn_w_down"][1],
                                   lfg[1:2], lfb[1:2], first_row=first_row, name="ffn_fwd1")
    gs = {}
    dpre4, dlfg1, dlfb1, loss_tile = _loss_head(h4, target, pre4, lfg[1:2], first_row=first_row)

    def ffn_backward(dpre, up, act, h_in, layer, tag, ln_in, token=None):
        dup, dcw, dpre_in, dg, db = _ffn_bwd(
            dpre, up, wts["ffn_w_down"][layer], after_token(small["ffn_conv"][layer], token),
            wts["ffn_w_up"][layer], ln_in[0], ln_in[1], first_row=first_row, name="ffn_bwd" + tag)
        dwd = _linear_dw(act, dpre, name="dw_down" + tag)
        dwu = _linear_dw(h_in, dup, name="dw_up" + tag)
        return dpre_in, dg, db, dwu, dwd, dcw[0:3]

    dpre3, dlmg1, dlmb1, dwu1, dwd1, dcf1 = ffn_backward(dpre4, up1, act1, h3, 1, "1", (pre3, lmg[1:2]))

    dwb_out = _linear_dw(bu, dpre3, name="dw_b_out")
    dproj_b, dcb, dpre2, dlfg0, dlfb0 = _sc_bwd(dpre3, proj_b, b_conv, wts["b_w_out"], wts["b_w_in"], pre2, lfg[0:1],
                                                first_row=first_row)
    dwb_in = _linear_dw(h2, dproj_b, name="dw_b_in")
    token = grads_ready("layer1", dict(ffn_w_up=dwu1, ffn_w_down=dwd1, b_w_in=dwb_in, b_w_out=dwb_out))

    dpre1, dlmg0, dlmb0, dwu0, dwd0, dcf0 = ffn_backward(dpre2, up0, act0, h1, 0, "0", (pre1, lmg[0:1]), token)
    dwa_out = _linear_dw(y, dpre1, name="dw_a_out")
    token = grads_ready("layer0", dict(ffn_w_up=dwu0, ffn_w_down=dwd0, a_w_out=dwa_out))

    dq, dk, dv, dz, dg_, dbeta, dnw = _delta_bwd(dpre1, a_w_out, o, z, after_token(nw, token), q, k, v, g, beta,
                                                 s_all, t_all, H=H)
    dproj_a, dca, dal, ddt, grad_x, dhead = _gdn_in_bwd(dq, dk, dv, dz, dg_, dbeta, pre_a, raw, a_conv, alog, dtb,
                                                        a_w_in, dpre1, first_row=first_row, H=H)
    grads_ready("last", dict(a_w_in=_linear_dw(h0, dproj_a, name="dw_a_in")))

    gs["meta"] = dhead[first_row:PADF]
    gs["a_conv"] = dca[0:a_conv.shape[0]][None]
    gs["a_log"] = dal[0:1, 0:H]
    gs["a_dt_bias"] = ddt[0:1, 0:H]
    gs["a_norm"] = dnw[0:1]
    gs["b_conv"] = dcb[0:b_conv.shape[0]][None]
    gs["ln_mix_g"] = jnp.stack([dlmg0[0], dlmg1[0]])
    gs["ln_mix_b"] = jnp.stack([dlmb0[0], dlmb1[0]])
    gs["ffn_conv"] = jnp.stack([dcf0, dcf1])
    gs["ln_ffn_g"] = jnp.stack([dlfg0[0], dlfg1[0]])
    gs["ln_ffn_b"] = jnp.stack([dlfb0[0], dlfb1[0]])
    return loss_tile, grad_x, gs


_BIG = ("a_w_in", "a_w_out", "b_w_in", "b_w_out", "ffn_w_up", "ffn_w_down")
_BIG_COL = ("a_w_in", "b_w_in", "ffn_w_up")
_SMALL = ("meta", "a_conv", "a_log", "a_dt_bias", "a_norm", "b_conv", "ln_mix_g", "ln_mix_b",
          "ffn_conv", "ln_ffn_g", "ln_ffn_b")
_SMALL_SHARDED = ("meta", "a_conv", "b_conv", "ffn_conv")
_ORDER = ("meta", "a_w_in", "a_conv", "a_log", "a_dt_bias", "a_norm", "a_w_out", "b_w_in", "b_conv", "b_w_out",
          "ln_mix_g", "ln_mix_b", "ffn_w_up", "ffn_conv", "ffn_w_down", "ln_ffn_g", "ln_ffn_b")


def _a_w_in_map(H):
    W4 = 4 * H * DH
    return [(0, W4, 0), (W4, W4 + H, W4), (W4 + H, W4 + 2 * H, W4 + LANE)], W4 + 2 * LANE


def kernel(x, meta, a_w_in, a_conv, a_log, a_dt_bias, a_norm, a_w_out, b_w_in, b_conv, b_w_out, ln_mix_g, ln_mix_b, ffn_w_up, ffn_conv, ffn_w_down, ln_ffn_g, ln_ffn_b, loss_target, m_meta, m_a_w_in, m_a_conv, m_a_log, m_a_dt_bias, m_a_norm, m_a_w_out, m_b_w_in, m_b_conv, m_b_w_out, m_ln_mix_g, m_ln_mix_b, m_ffn_w_up, m_ffn_conv, m_ffn_w_down, m_ln_ffn_g, m_ln_ffn_b, v_meta, v_a_w_in, v_a_conv, v_a_log, v_a_dt_bias, v_a_norm, v_a_w_out, v_b_w_in, v_b_conv, v_b_w_out, v_ln_mix_g, v_ln_mix_b, v_ffn_w_up, v_ffn_conv, v_ffn_w_down, v_ln_ffn_g, v_ln_ffn_b):
    wloc = dict(meta=meta, a_w_in=a_w_in, a_conv=a_conv, a_log=a_log, a_dt_bias=a_dt_bias, a_norm=a_norm,
                a_w_out=a_w_out, b_w_in=b_w_in, b_conv=b_conv, b_w_out=b_w_out, ln_mix_g=ln_mix_g, ln_mix_b=ln_mix_b,
                ffn_w_up=ffn_w_up, ffn_conv=ffn_conv, ffn_w_down=ffn_w_down, ln_ffn_g=ln_ffn_g, ln_ffn_b=ln_ffn_b)
    mloc = dict(meta=m_meta, a_w_in=m_a_w_in, a_conv=m_a_conv, a_log=m_a_log, a_dt_bias=m_a_dt_bias, a_norm=m_a_norm,
                a_w_out=m_a_w_out, b_w_in=m_b_w_in, b_conv=m_b_conv, b_w_out=m_b_w_out, ln_mix_g=m_ln_mix_g,
                ln_mix_b=m_ln_mix_b, ffn_w_up=m_ffn_w_up, ffn_conv=m_ffn_conv, ffn_w_down=m_ffn_w_down,
                ln_ffn_g=m_ln_ffn_g, ln_ffn_b=m_ln_ffn_b)
    vloc = dict(meta=v_meta, a_w_in=v_a_w_in, a_conv=v_a_conv, a_log=v_a_log, a_dt_bias=v_a_dt_bias, a_norm=v_a_norm,
                a_w_out=v_a_w_out, b_w_in=v_b_w_in, b_conv=v_b_conv, b_w_out=v_b_w_out, ln_mix_g=v_ln_mix_g,
                ln_mix_b=v_ln_mix_b, ffn_w_up=v_ffn_w_up, ffn_conv=v_ffn_conv, ffn_w_down=v_ffn_w_down,
                ln_ffn_g=v_ln_ffn_g, ln_ffn_b=v_ln_ffn_b)
    H = a_log.shape[-1]
    mx, my, mc = lax.axis_index("x"), lax.axis_index("y"), lax.axis_index("c")
    me = 4 * mx + 2 * my + mc

    a_map, a_cols = _a_w_in_map(H)
    col_maps = {"a_w_in": (a_map, a_cols)}
    for n in ("b_w_in", "ffn_w_up"):
        ncols = N_DEV * wloc[n].shape[-1]
        col_maps[n] = ([(0, ncols, 0)], ncols)
    sm_sh = [wloc[n] for n in _SMALL_SHARDED]
    sbuf, soffs = _pack_small(sm_sh, 128)
    g_a_w_in, g_a_w_out, sg = _all_gather([_bf(wloc["a_w_in"]), _bf(wloc["a_w_out"]), sbuf], name="gather_first")
    w_a_in = _assemble_cols(g_a_w_in, *col_maps["a_w_in"], name="assemble_a_w_in")[0]
    w_a_out = _rows_full(g_a_w_out)[0]
    late = [n for n in _BIG if n not in ("a_w_in", "a_w_out")]
    ssem, rsem, srcs_t, lands_t, start_token = _direct_start(
        [_bf(wloc[n]) for n in late], [lax.empty((N_DEV,) + wloc[n].shape, bf16) for n in late], False,
        name="gather_rest_start")

    def late_weights(after):
        srcs_d, landed = _direct_wait(ssem, rsem, srcs_t, lands_t, False, after, name="gather_rest_wait")
        wts = {}
        for n, own, got in zip(late, srcs_d, landed):
            full = lax.dynamic_update_index_in_dim(got, own, me, 0)
            if n in _BIG_COL:
                wts[n] = _assemble_cols(full, *col_maps[n], name="assemble_" + n)
            else:
                wts[n] = _rows_full(full)
        for n in ("b_w_in", "b_w_out"):
            wts[n] = wts[n][0]
        return wts

    small = {n: wloc[n] for n in _SMALL}
    for n, off in zip(_SMALL_SHARDED, soffs):
        sh = wloc[n].shape
        parts = jnp.stack([_unpack_small(sg[d], off, sh) for d in range(N_DEV)])
        nd = len(sh)
        small[n] = jnp.transpose(parts, tuple(range(1, nd)) + (0, nd)).reshape(sh[:-1] + (N_DEV * sh[-1],))

    def split(n, dws, tag):
        if n in _BIG_COL:
            return _split_cols(dws, col_maps[n][0], wloc[n].shape[-1], name="split_" + n + tag)
        return _split_rows(dws, wloc[n].shape[-2], name="split_" + n + tag)

    sent = {}

    def grads_ready(stage, grads):
        names = sorted(grads)
        parts = [split(n, [grads[n]], "_" + stage) for n in names]
        handles = _direct_start([p[1] for p in parts], [jnp.zeros(p[1].shape, bf16) for p in parts], True,
                                name="grads_" + stage + "_start")
        sent[stage] = (names, [p[0] for p in parts], handles)
        return handles[4]

    loss_tile, grad_x, gs = _local_step(x[0], loss_target[0], small["meta"], w_a_in, w_a_out, small, start_token,
                                        late_weights, grads_ready)

    def landed(stage, after):
        names, own32, (ssem_g, rsem_g, srcs_g, lands_g, _) = sent[stage]
        _, got = _direct_wait(ssem_g, rsem_g, srcs_g, lands_g, True, after, name="grads_" + stage + "_wait")
        return list(zip(names, own32, got))

    parts = {}
    for stage in ("layer0", "layer1"):
        for n, o32, r in landed(stage, grad_x):
            parts.setdefault(n, []).append((o32, r))
    me1 = jnp.stack([me]).astype(jnp.int32)
    big_out = {n: _adamw_direct([p[0] for p in ps], [p[1] for p in ps], wloc[n], mloc[n], vloc[n], me1,
                                name="adamw_" + n) for n, ps in parts.items()}
    names = list(_SMALL)
    pbuf, poffs = _pack_small([gs[n] for n in names] + [loss_tile[0:1, 0:1]], 1024)
    psum = _sum_devices(_all_gather([pbuf], name="gather_small_grads")[0])
    loss = psum[poffs[-1][0], 0]
    g_small = {}
    for n, off in zip(names, poffs[:-1]):
        full_shape = gs[n].shape
        gfull = _unpack_small(psum, off, full_shape)
        if n in _SMALL_SHARDED:
            ns = wloc[n].shape[-1]
            gfull = lax.dynamic_slice_in_dim(gfull, me * ns, ns, axis=gfull.ndim - 1)
        g_small[n] = gfull.reshape(wloc[n].shape)
    gbuf, aoffs = _pack_small([g_small[n] for n in names], 128)
    wbuf, _ = _pack_small([wloc[n] for n in names], 128)
    mbuf, _ = _pack_small([mloc[n] for n in names], 128)
    vbuf, _ = _pack_small([vloc[n] for n in names], 128)
    _, d_s, m_s, v_s = _adamw([gbuf], wbuf, mbuf, vbuf, name="adamw_small")

    done = d_s[0, 0]
    for out in big_out.values():
        done = done + out[1][0, 0, 0]
    (n, o32, r), = landed("last", done.reshape(1, 1))
    big_out[n] = _adamw_direct([o32], [r], wloc[n], mloc[n], vloc[n], me1, name="adamw_" + n)

    grads, deltas, new_m, new_v = {}, {}, {}, {}
    for n in _BIG:
        grads[n], deltas[n], new_m[n], new_v[n] = big_out[n]
    for n, off in zip(names, aoffs):
        sh = wloc[n].shape
        grads[n] = g_small[n]
        deltas[n], new_m[n], new_v[n] = (_unpack_small(b_, off, sh) for b_ in (d_s, m_s, v_s))
    return (loss, grad_x[None], *[grads[n] for n in _ORDER], *[deltas[n] for n in _ORDER],
            *[new_m[n] for n in _ORDER], *[new_v[n] for n in _ORDER])
```

```python
import math

import jax
import jax.numpy as jnp
from jax import lax
from jax.experimental import pallas as pl
from jax.experimental.pallas import tpu as pltpu

f32, bf16 = jnp.float32, jnp.bfloat16
S = jax.ShapeDtypeStruct
HI = lax.Precision.HIGHEST
HI3 = lax.Precision.HIGH
MESH = pl.DeviceIdType.MESH

V7X_VMEM_LIMIT = 56 * 1024 * 1024
LANE = 128
DH = 128
CH = 64
PADF = 256
TM = 256
TMM = 768
N_DEV = 8
BWD_HEAD_GROUP = 4

DEPTH = 2
ALPHA = (2.0 * DEPTH) ** 0.25
LN_EPS = 1e-5
RMS_EPS = 1e-6
L2_EPS = 1e-6
ADAM_LR, ADAM_B1, ADAM_B2, ADAM_EPS, ADAM_WD, ADAM_STEP = 0.001, 0.9, 0.999, 1e-08, 0.01, 10


def _cp(**kw):
    return pltpu.CompilerParams(vmem_limit_bytes=V7X_VMEM_LIMIT, **kw)


def _bf(x):
    return x.astype(bf16)


def _dot(a, b, precision=None):
    return jnp.dot(a, b, preferred_element_type=f32, precision=precision)


def _dot_nt(a, b):
    return lax.dot_general(a, b, (((1,), (1,)), ((), ())), preferred_element_type=f32)


def _dot_tn(a, b):
    return lax.dot_general(a, b, (((0,), (0,)), ((), ())), preferred_element_type=f32)


def _sigmoid(x):
    return 1.0 / (1.0 + jnp.exp(-x))


def _load_once(pairs, sem):
    @pl.when(pl.program_id(0) == 0)
    def _():
        cps = [pltpu.make_async_copy(src, dst, sem.at[n]) for n, (src, dst) in enumerate(pairs)]
        for c in cps:
            c.start()
        for c in cps:
            c.wait()


def _row_ids(i, tm, width):
    return i * tm + lax.broadcasted_iota(jnp.int32, (tm, width), 0)


def _ln_fwd(pre, g, b, rows, first_row):
    mu = jnp.mean(pre, axis=-1, keepdims=True)
    xc = pre - mu
    var = jnp.mean(xc * xc, axis=-1, keepdims=True)
    y = xc * lax.rsqrt(var + LN_EPS) * g + b
    return jnp.where(rows >= first_row, y, 0.0)


ANY = pl.BlockSpec(memory_space=pl.ANY)


def _taps_back(prev8, x, kw):
    xe = jnp.concatenate([prev8, x], axis=0)
    return [pltpu.roll(xe, kw - 1 - j, 0)[8:] for j in range(kw - 1)] + [x]


def _taps_ahead(x, next8, kw):
    n = x.shape[0]
    xe = jnp.concatenate([x, next8], axis=0)
    return [pltpu.roll(xe, n + 8 - (kw - 1 - j), 0)[:n] for j in range(kw - 1)] + [x]


def _conv(cw, taps):
    acc = cw[0:1, :] * taps[0]
    for j in range(1, len(taps)):
        acc = acc + cw[j:j + 1, :] * taps[j]
    return acc


def _linear_dw(x, dy, *, name):
    L, K = x.shape
    N = dy.shape[1]
    tm = TMM if L % TMM == 0 else TM
    tn = LANE
    for d in range(N // LANE, 0, -1):
        if (N // LANE) % d == 0 and K * d * LANE * 4 <= 9 * 1024 * 1024:
            tn = d * LANE
            break

    def body(x_ref, dy_ref, o_ref):
        @pl.when(pl.program_id(1) == 0)
        def _():
            o_ref[...] = jnp.zeros_like(o_ref)
        o_ref[...] += _dot_tn(_bf(x_ref[...]), _bf(dy_ref[...]))

    return pl.pallas_call(
        body, name=name, grid=(N // tn, L // tm), out_shape=S((K, N), f32),
        in_specs=[pl.BlockSpec((tm, K), lambda j, i: (i, 0)), pl.BlockSpec((tm, tn), lambda j, i: (i, j))],
        out_specs=pl.BlockSpec((K, tn), lambda j, i: (0, j)),
        compiler_params=_cp(dimension_semantics=("arbitrary", "arbitrary")))(x, dy)


def _ln_bwd_rows(dout, pre, g, rows, first_row):
    mu = jnp.mean(pre, axis=-1, keepdims=True)
    xc = pre - mu
    rstd = lax.rsqrt(jnp.mean(xc * xc, axis=-1, keepdims=True) + LN_EPS)
    xh = xc * rstd
    dy = jnp.where(rows >= first_row, dout, 0.0)
    dxh = dy * g
    dpre = rstd * (dxh - jnp.mean(dxh, axis=-1, keepdims=True) - xh * jnp.mean(dxh * xh, axis=-1, keepdims=True))
    return dpre, jnp.sum(dy * xh, axis=0, keepdims=True), jnp.sum(dy, axis=0, keepdims=True)


def _gdn_in_fwd(x, head, w_full, conv_w, alog, dtb, *, first_row, H):
    D = x.shape[1]
    L = PADF + x.shape[0]
    W = H * DH
    NW = w_full.shape[1]
    KW = conv_w.shape[0]
    tm = TM
    pb = PADF // tm

    def body(x_ref, head_ref, w_hbm, cw_ref, alog_ref, dtb_ref,
             h_ref, pre_ref, z_ref, raw_ref, q_ref, k_ref, v_ref, beta_ref, g_ref, t_ref,
             w_vmem, carry, sem):
        i = pl.program_id(0)
        _load_once([(w_hbm, w_vmem)], sem)

        @pl.when(i == 0)
        def _():
            carry[...] = jnp.zeros_like(carry)

        hv = jnp.where(i < pb, head_ref[...], x_ref[...])
        h_ref[...] = hv
        hb = _bf(hv)
        outs = (q_ref, k_ref, v_ref)

        def section(s):
            pre = _dot(hb, w_vmem[:, s * W:(s + 1) * W])
            pre_ref[:, s * W:(s + 1) * W] = pre
            c = _conv(cw_ref[:, s * W:(s + 1) * W], _taps_back(carry[s], pre, KW))
            carry[s] = pre[tm - 8:tm, :]
            sl = c * _sigmoid(c)
            if s < 2:
                scale = DH ** -0.5 if s == 0 else 1.0
                for hh in range(H):
                    seg = sl[:, hh * DH:(hh + 1) * DH]
                    r = lax.rsqrt(jnp.sum(seg * seg, axis=-1, keepdims=True) + L2_EPS)
                    outs[s][:, hh * DH:(hh + 1) * DH] = seg * (r * scale)
            else:
                v_ref[...] = sl

        raw = _dot(hb, w_vmem[:, 4 * W:4 * W + 2 * LANE])
        raw_ref[...] = raw
        ok = (_row_ids(i, tm, LANE) >= first_row) & (lax.broadcasted_iota(jnp.int32, (tm, LANE), 1) < H)
        beta = jnp.where(ok, _sigmoid(raw[:, :LANE]), 0.0)
        beta_ref[...] = beta
        a = raw[:, LANE:] + dtb_ref[...]
        sp = jnp.maximum(a, 0.0) + jnp.log(1.0 + jnp.exp(-jnp.abs(a)))
        gv = jnp.where(ok, -jnp.exp(alog_ref[...]) * sp, 0.0)
        gam = _dot(_chunk_tri(tm, lower=True), gv, HI)
        g_ref[...] = gam
        section(1)
        ii = lax.broadcasted_iota(jnp.int32, (CH, CH), 0)
        jj = lax.broadcasted_iota(jnp.int32, (CH, CH), 1)
        eye = (ii == jj).astype(f32)
        gam_t = gam.T

        def inverses(chunks):
            ms = []
            for c in chunks:
                rows = slice(c * CH, (c + 1) * CH)
                for hh in range(H):
                    kh = k_ref[rows, hh * DH:(hh + 1) * DH]
                    dec = jnp.exp(jnp.minimum(gam[rows, hh:hh + 1] - gam_t[hh:hh + 1, rows], 0.0))
                    kk = _dot_nt(_bf(kh * beta[rows, hh:hh + 1]), _bf(kh))
                    ms.append(jnp.where(ii > jj, kk * dec, 0.0))
            for n, t in enumerate(_tri_inv_many(ms, eye)):
                t_ref[chunks[n // H], n % H] = t

        nch = tm // CH
        inverses(list(range(nch // 2)))
        section(0)
        inverses(list(range(nch // 2, nch)))
        section(2)
        z_ref[...] = _dot(hb, w_vmem[:, 3 * W:4 * W])

    row = lambda i: (i, 0)
    fix = lambda i: (0, 0)
    out_shape = (S((L, D), f32), S((L, 3 * W), f32), S((L, W), f32), S((L, 2 * LANE), f32),
                 S((L, W), f32), S((L, W), f32), S((L, W), f32), S((L, LANE), f32), S((L, LANE), f32),
                 S((L // CH, H, CH, CH), f32))
    out_specs = (pl.BlockSpec((tm, D), row),
                 pl.BlockSpec((tm, 3 * W), row), pl.BlockSpec((tm, W), row), pl.BlockSpec((tm, 2 * LANE), row),
                 pl.BlockSpec((tm, W), row), pl.BlockSpec((tm, W), row), pl.BlockSpec((tm, W), row),
                 pl.BlockSpec((tm, LANE), row), pl.BlockSpec((tm, LANE), row),
                 pl.BlockSpec((tm // CH, H, CH, CH), lambda i: (i, 0, 0, 0)))
    return pl.pallas_call(
        body, name="gdn_in_fwd", grid=(L // tm,), out_shape=out_shape,
        in_specs=[pl.BlockSpec((tm, D), lambda i: (jnp.maximum(i - pb, 0), 0)),
                  pl.BlockSpec((tm, D), lambda i: (jnp.minimum(i, pb - 1), 0)), ANY, pl.BlockSpec((KW, 3 * W), fix),
                  pl.BlockSpec((1, LANE), fix), pl.BlockSpec((1, LANE), fix)],
        out_specs=out_specs,
        scratch_shapes=[pltpu.VMEM((D, NW), w_full.dtype), pltpu.VMEM((3, 8, W), f32), pltpu.SemaphoreType.DMA((1,))],
        compiler_params=_cp(dimension_semantics=("arbitrary",)))(x, head, w_full, conv_w, alog, dtb)


def _gdn_in_bwd(dq, dk, dv, dz, dg, dbeta, pre, raw, conv_w, alog, dtb, w_full, res, *, first_row, H):
    L = dq.shape[0]
    D = res.shape[1]
    W = H * DH
    KW = conv_w.shape[0]
    tm = TM
    nb = L // tm
    NW = 4 * W + 2 * LANE
    fb = PADF // tm
    alpha = ALPHA

    def body(dq_ref, dk_ref, dv_ref, dz_ref, dg_ref, dbeta_ref, pre_ref, hq_ref, hk_ref, hv_ref, raw_ref,
             cw_ref, alog_ref, dtb_ref, w_hbm, res_ref,
             dproj_ref, dcw_ref, dal_ref, ddt_ref, dx_ref, dfront_ref, w_vmem, carry, tmp, sem):
        i = pl.program_id(0)
        blk = nb - 1 - i
        _load_once([(w_hbm, w_vmem)], sem)

        @pl.when(i == 0)
        def _():
            carry[...] = jnp.zeros_like(carry)
            dcw_ref[...] = jnp.zeros_like(dcw_ref)
            dal_ref[...] = jnp.zeros_like(dal_ref)
            ddt_ref[...] = jnp.zeros_like(ddt_ref)

        halos = (hq_ref, hk_ref, hv_ref)
        douts = (dq_ref, dk_ref, dv_ref)
        for s in range(3):
            sec = slice(s * W, (s + 1) * W)
            pre = pre_ref[:, sec]
            c = _conv(cw_ref[:, sec], _taps_back(jnp.where(blk > 0, halos[s][...], 0.0), pre, KW))
            sig = _sigmoid(c)
            sl = c * sig
            if s < 2:
                scale = DH ** -0.5 if s == 0 else 1.0
                for hh in range(H):
                    hs = slice(hh * DH, (hh + 1) * DH)
                    seg = sl[:, hs]
                    r = lax.rsqrt(jnp.sum(seg * seg, axis=-1, keepdims=True) + L2_EPS)
                    n = seg * r
                    dqs = douts[s][:, hs]
                    tmp[:, hs] = (scale * r) * (dqs - n * jnp.sum(n * dqs, axis=-1, keepdims=True))
                dsl = tmp[...]
            else:
                dsl = dv_ref[...]
            dc = dsl * (sig * (1.0 + c * (1.0 - sig)))
            ahead = _taps_ahead(dc, carry[s], KW)
            carry[s] = dc[0:8, :]
            dproj_ref[:, sec] = _bf(_conv(cw_ref[:, sec], ahead))
            for j in range(KW):
                dcw_ref[j:j + 1, sec] += jnp.sum(ahead[j] * pre, axis=0, keepdims=True)
        dproj_ref[:, 3 * W:4 * W] = _bf(dz_ref[...])
        raw_v = raw_ref[...]
        ok = (_row_ids(blk, tm, LANE) >= first_row) & (lax.broadcasted_iota(jnp.int32, (tm, LANE), 1) < H)
        beta = _sigmoid(raw_v[:, :LANE])
        dbraw = jnp.where(ok, dbeta_ref[...] * beta * (1.0 - beta), 0.0)
        a = raw_v[:, LANE:] + dtb_ref[...]
        sp = jnp.maximum(a, 0.0) + jnp.log(1.0 + jnp.exp(-jnp.abs(a)))
        nea = -jnp.exp(alog_ref[...])
        dgm = jnp.where(ok, _dot(_chunk_tri(tm, lower=False), dg_ref[...], HI), 0.0)
        daraw = dgm * nea * _sigmoid(a)
        dal_ref[0:1, :] += jnp.sum(dgm * nea * sp, axis=0, keepdims=True)
        ddt_ref[0:1, :] += jnp.sum(daraw, axis=0, keepdims=True)
        dproj_ref[:, 4 * W:4 * W + LANE] = _bf(dbraw)
        dproj_ref[:, 4 * W + LANE:4 * W + 2 * LANE] = _bf(daraw)
        dh = alpha * res_ref[...] + _dot_nt(dproj_ref[...], w_vmem[...])

        @pl.when(blk >= fb)
        def _():
            dx_ref[...] = dh

        @pl.when(blk < fb)
        def _():
            dfront_ref[...] = dh

    rev = lambda i: (nb - 1 - i, 0)
    fix = lambda i: (0, 0)

    def halo(col):
        return pl.BlockSpec((8, W), lambda i: (jnp.maximum((nb - 1 - i) * (tm // 8) - 1, 0), col))

    return pl.pallas_call(
        body, name="gdn_in_bwd", grid=(nb,),
        out_shape=(S((L, NW), bf16), S((8, 3 * W), f32), S((8, LANE), f32), S((8, LANE), f32),
                   S((L - PADF, D), f32), S((PADF, D), f32)),
        in_specs=[pl.BlockSpec((tm, W), rev)] * 4 + [pl.BlockSpec((tm, LANE), rev)] * 2
        + [pl.BlockSpec((tm, 3 * W), rev), halo(0), halo(1), halo(2), pl.BlockSpec((tm, 2 * LANE), rev),
           pl.BlockSpec((KW, 3 * W), fix), pl.BlockSpec((1, LANE), fix), pl.BlockSpec((1, LANE), fix),
           ANY, pl.BlockSpec((tm, D), rev)],
        out_specs=(pl.BlockSpec((tm, NW), rev), pl.BlockSpec((8, 3 * W), fix),
                   pl.BlockSpec((8, LANE), fix), pl.BlockSpec((8, LANE), fix),
                   pl.BlockSpec((tm, D), lambda i: (jnp.maximum(nb - 1 - i - fb, 0), 0)),
                   pl.BlockSpec((tm, D), lambda i: (jnp.minimum(nb - 1 - i, fb - 1), 0))),
        scratch_shapes=[pltpu.VMEM((D, NW), w_full.dtype), pltpu.VMEM((3, 8, W), f32), pltpu.VMEM((tm, W), f32),
                        pltpu.SemaphoreType.DMA((1,))],
        compiler_params=_cp(dimension_semantics=("arbitrary",)))(
            dq, dk, dv, dz, dg, dbeta, pre, pre, pre, pre, raw, conv_w, alog, dtb, w_full, res)


def _chunk_tri(n, lower):
    i = lax.broadcasted_iota(jnp.int32, (n, n), 0)
    j = lax.broadcasted_iota(jnp.int32, (n, n), 1)
    sh = int(math.log2(CH))
    same = lax.shift_right_logical(i, sh) == lax.shift_right_logical(j, sh)
    return (same & ((i >= j) if lower else (j >= i))).astype(f32)


def _tri_inv_many(ms, eye):
    ts = [eye - m for m in ms]
    ps = list(ms)
    for level in range(int(math.log2(CH)) - 1):
        if level == 0:
            ps = [_dot(p, p, HI3) for p in ps]
            ts = [t + _dot(t, p, HI3) for t, p in zip(ts, ps)]
        else:
            ps = [_dot(_bf(p), _bf(p)) for p in ps]
            ts = [t + _dot(_bf(t), _bf(p)) for t, p in zip(ts, ps)]
    return ts


def _chunk_local(q, k, v, gcol, grow, glast, bcol, ii, jj):
    dec = jnp.where(ii >= jj, jnp.exp(jnp.minimum(gcol - grow, 0.0)), 0.0)
    eg = jnp.exp(gcol)
    kb = k * bcol
    kbg = kb * eg
    vb = v * bcol
    qt = q * eg
    kt = k * jnp.exp(glast - gcol)
    kbb, qb, kbf = _bf(kb), _bf(q), _bf(k)
    return dec, eg, kb, kbg, vb, qt, kt, _dot_nt(kbb, kbf), _dot_nt(qb, kbf), jnp.concatenate([kbb, qb], axis=0)


def _delta_fwd(q, k, v, g, beta, t_all, z, nw, h, w_out, ln_g, ln_b, *, first_row, H):
    L = q.shape[0]
    W = H * DH
    D = h.shape[1]
    rb = TM
    nc = rb // CH
    nblk = L // rb
    alpha = ALPHA

    def body(q_ref, k_ref, v_ref, g_ref, b_ref, t_ref, z_ref, nw_ref, h_ref, wout_hbm, lg_ref, lb_ref,
             o_ref, y_ref, s_out, pre_ref, out_ref, s_scr, wout, sem):
        _load_once([(wout_hbm, wout)], sem)

        @pl.when(pl.program_id(0) == 0)
        def _():
            s_scr[...] = jnp.zeros_like(s_scr)

        ii = lax.broadcasted_iota(jnp.int32, (CH, CH), 0)
        jj = lax.broadcasted_iota(jnp.int32, (CH, CH), 1)
        eye = (ii == jj).astype(f32)
        nwv = nw_ref[...]

        heads = range(H)
        hsl = [slice(hh * DH, (hh + 1) * DH) for hh in heads]

        def chunk(c, carry):
            r0 = pl.multiple_of(c * CH, CH)
            rows = pl.ds(r0, CH)
            gam = g_ref[rows, :]
            gam_t = gam.T
            bb = b_ref[rows, :]
            glast = [gam[CH - 1:CH, hh:hh + 1] for hh in heads]
            loc = [_chunk_local(q_ref[rows, hsl[hh]], k_ref[rows, hsl[hh]], v_ref[rows, hsl[hh]],
                                gam[:, hh:hh + 1], gam_t[hh:hh + 1, :], glast[hh], bb[:, hh:hh + 1], ii, jj)
                   for hh in heads]
            st = [s_scr[hh] for hh in heads]
            zs = [z_ref[rows, hsl[hh]] for hh in heads]
            ts = [t_ref[c, hh] for hh in heads]
            us = [_dot(t, l[4], HI3) for t, l in zip(ts, loc)]
            ws = [_dot(t, l[3], HI3) for t, l in zip(ts, loc)]
            stb = [_bf(s) for s in st]
            vn = [u - _dot(_bf(w), sb) for u, w, sb in zip(us, ws, stb)]
            vnb = [_bf(x) for x in vn]
            snew = [s * jnp.exp(gl) + _dot_tn(_bf(l[6]), xb) for s, gl, l, xb in zip(st, glast, loc, vnb)]
            os_ = [_dot(_bf(l[5]), sb) + _dot(_bf(l[8] * l[0]), xb) for l, sb, xb in zip(loc, stb, vnb)]
            for hh in heads:
                o = os_[hh]
                s_out[c, hh] = st[hh]
                s_scr[hh] = snew[hh]
                o_ref[rows, hsl[hh]] = o
                on = o * lax.rsqrt(jnp.mean(o * o, axis=-1, keepdims=True) + RMS_EPS) * nwv
                y_ref[rows, hsl[hh]] = _bf(on * (zs[hh] * _sigmoid(zs[hh])))
            return carry

        lax.fori_loop(0, nc, chunk, 0)
        pre = alpha * h_ref[...] + _dot(y_ref[...], wout[...])
        pre_ref[...] = pre
        out_ref[...] = _ln_fwd(pre, lg_ref[...], lb_ref[...], _row_ids(pl.program_id(0), rb, D), first_row)

    row = lambda i: (i, 0)
    fix = lambda i: (0, 0)
    return pl.pallas_call(
        body, name="delta_fwd", grid=(nblk,),
        out_shape=(S((L, W), f32), S((L, W), bf16), S((L // CH, H, DH, DH), f32), S((L, D), f32), S((L, D), f32)),
        in_specs=[pl.BlockSpec((rb, W), row)] * 3 + [pl.BlockSpec((rb, LANE), row)] * 2
        + [pl.BlockSpec((nc, H, CH, CH), lambda i: (i, 0, 0, 0)),
           pl.BlockSpec((rb, W), row), pl.BlockSpec((1, DH), fix), pl.BlockSpec((rb, D), row), ANY,
           pl.BlockSpec((1, D), fix), pl.BlockSpec((1, D), fix)],
        out_specs=(pl.BlockSpec((rb, W), row), pl.BlockSpec((rb, W), row),
                   pl.BlockSpec((nc, H, DH, DH), lambda i: (i, 0, 0, 0)),
                   pl.BlockSpec((rb, D), row), pl.BlockSpec((rb, D), row)),
        scratch_shapes=[pltpu.VMEM((H, DH, DH), f32), pltpu.VMEM((W, D), w_out.dtype), pltpu.SemaphoreType.DMA((1,))],
        compiler_params=_cp(dimension_semantics=("arbitrary",)))(q, k, v, g, beta, t_all, z, nw, h, w_out, ln_g, ln_b)


def _delta_bwd(dpre, w_out, o, z, nw, q, k, v, g, beta, s_all, t_all, *, H):
    L = q.shape[0]
    W = H * DH
    D = dpre.shape[1]
    rb = TM
    nc = rb // CH
    nblk = L // rb

    def body(dpre_ref, wout_hbm, o_ref, z_ref, nw_ref, q_ref, k_ref, v_ref, g_ref, b_ref, s_ref, t_ref,
             dq_ref, dk_ref, dv_ref, dz_ref, dg_ref, db_ref, dnw_ref, ds_scr, wout, dy_scr, sem):
        _load_once([(wout_hbm, wout)], sem)

        @pl.when(pl.program_id(0) == 0)
        def _():
            ds_scr[...] = jnp.zeros_like(ds_scr)
            dnw_ref[...] = jnp.zeros_like(dnw_ref)

        dy_scr[...] = _dot_nt(_bf(dpre_ref[...]), wout[...])

        ii = lax.broadcasted_iota(jnp.int32, (CH, CH), 0)
        jj = lax.broadcasted_iota(jnp.int32, (CH, CH), 1)
        lane = lax.broadcasted_iota(jnp.int32, (CH, LANE), 1)
        last_row = lax.broadcasted_iota(jnp.int32, (CH, 1), 0) == CH - 1
        nwv = nw_ref[...]

        def chunk(cc, carry):
            c = nc - 1 - cc
            r0 = pl.multiple_of(c * CH, CH)
            rows = pl.ds(r0, CH)
            gam = g_ref[rows, :]
            gam_t = gam.T
            bb = b_ref[rows, :]

            def head(hh):
                hs = slice(hh * DH, (hh + 1) * DH)
                gcol, grow, glast = gam[:, hh:hh + 1], gam_t[hh:hh + 1, :], gam[CH - 1:CH, hh:hh + 1]
                bcol = bb[:, hh:hh + 1]
                qh, kh, vh = q_ref[rows, hs], k_ref[rows, hs], v_ref[rows, hs]
                oh, zh, dyh = o_ref[rows, hs], z_ref[rows, hs], dy_scr[rows, hs]
                t = t_ref[c, hh]
                st = s_ref[c, hh]
                dsn = ds_scr[hh]
                rms = lax.rsqrt(jnp.mean(oh * oh, axis=-1, keepdims=True) + RMS_EPS)
                on = oh * rms
                sig = _sigmoid(zh)
                sz = zh * sig
                dz_ref[rows, hs] = dyh * on * nwv * (sig * (1.0 + zh * (1.0 - sig)))
                dnw = jnp.sum(dyh * on * sz, axis=0, keepdims=True)
                don = dyh * nwv * sz
                do = rms * (don - on * jnp.mean(don * on, axis=-1, keepdims=True))
                dec, eg, kb, kbg, vb, qt, kt, kk, qk, kqb = _chunk_local(qh, kh, vh, gcol, grow, glast, bcol, ii, jj)
                stb, dsnb, dob, tb, kbgb = _bf(st), _bf(dsn), _bf(do), _bf(t), _bf(kbg)
                r = vb - _dot(kbgb, stb)
                mm = jnp.where(ii > jj, kk * dec, 0.0)
                attn = qk * dec
                yield
                vn = _dot(t, r, HI3)
                dvn = _dot_tn(_bf(attn), dob) + _dot(_bf(kt), dsnb)
                egl = jnp.exp(glast)
                ekt = jnp.exp(glast - gcol)
                yield
                vnb, dvnb = _bf(vn), _bf(dvn)
                dattn = jnp.where(ii >= jj, _dot_nt(dob, vnb), 0.0)
                dkt = _dot_nt(vnb, dsnb)
                dvb = _dot_tn(tb, dvnb)
                dt = _dot_nt(dvnb, _bf(r))
                dglast = egl * jnp.sum(jnp.sum(dsn * st, axis=0, keepdims=True), axis=1, keepdims=True)
                yield
                dv_ref[rows, hs] = dvb * bcol
                dod = jnp.concatenate([dob, -_bf(dvb)], axis=0)
                ds_scr[hh] = egl * dsn + _dot_tn(jnp.concatenate([_bf(qt), kbgb], axis=0), dod)
                both = _dot_nt(dod, stb)
                dqt, dkbg = both[:CH], both[CH:]
                x = _dot_nt(_bf(dt), tb)
                yield
                dm = jnp.where(ii > jj, -_dot_tn(tb, _bf(x)), 0.0)
                dkk = dm * dec
                dqk = dattn * dec
                e = dm * mm + dattn * attn
                dgam = jnp.sum(e, axis=1, keepdims=True) - jnp.sum(e.T, axis=1, keepdims=True)
                dd = _bf(jnp.concatenate([dkk, dqk], axis=0))
                both = _dot(dd, _bf(kh))
                dkb = both[:CH] + dkbg * eg
                dk_ref[rows, hs] = _dot_tn(dd, kqb) + dkt * ekt + dkb * bcol
                dq_ref[rows, hs] = both[CH:] + dqt * eg
                yield
                dktkt = dkt * kt
                dgam = dgam + jnp.sum(dqt * qt - dktkt + dkbg * kbg, axis=1, keepdims=True)
                dglast = dglast + jnp.sum(jnp.sum(dktkt, axis=0, keepdims=True), axis=1, keepdims=True)
                dgam = dgam + jnp.where(last_row, dglast, 0.0)
                dbeta = jnp.sum(dkb * kh + dvb * vh, axis=1, keepdims=True)
                return dgam, dbeta, dnw

            res = [None] * H
            for h0 in range(0, H, BWD_HEAD_GROUP):
                group = range(h0, min(h0 + BWD_HEAD_GROUP, H))
                gens = {hh: head(hh) for hh in group}
                while any(res[hh] is None for hh in group):
                    for hh in group:
                        try:
                            next(gens[hh])
                        except StopIteration as stop:
                            res[hh] = stop.value
            dgam_all = jnp.zeros((CH, LANE), f32)
            dbeta_all = jnp.zeros((CH, LANE), f32)
            dnw_acc = jnp.zeros((1, DH), f32)
            for hh in range(H):
                dgam, dbeta, dnw = res[hh]
                dgam_all = dgam_all + jnp.where(lane == hh, dgam, 0.0)
                dbeta_all = dbeta_all + jnp.where(lane == hh, dbeta, 0.0)
                dnw_acc = dnw_acc + dnw
            dg_ref[rows, :] = dgam_all
            db_ref[rows, :] = dbeta_all
            dnw_ref[0:1, :] += dnw_acc
            return carry

        lax.fori_loop(0, nc, chunk, 0)

    rev = lambda i: (nblk - 1 - i, 0)
    rev4 = lambda i: (nblk - 1 - i, 0, 0, 0)
    fix = lambda i: (0, 0)
    wide = pl.BlockSpec((rb, W), rev)
    thin = pl.BlockSpec((rb, LANE), rev)
    return pl.pallas_call(
        body, name="delta_bwd", grid=(nblk,),
        out_shape=(S((L, W), f32),) * 4 + (S((L, LANE), f32),) * 2 + (S((8, DH), f32),),
        in_specs=[pl.BlockSpec((rb, D), rev), ANY, wide, wide, pl.BlockSpec((1, DH), fix), wide, wide, wide, thin, thin,
                  pl.BlockSpec((nc, H, DH, DH), rev4), pl.BlockSpec((nc, H, CH, CH), rev4)],
        out_specs=(wide,) * 4 + (thin, thin, pl.BlockSpec((8, DH), fix)),
        scratch_shapes=[pltpu.VMEM((H, DH, DH), f32), pltpu.VMEM((W, D), w_out.dtype), pltpu.VMEM((rb, W), f32),
                        pltpu.SemaphoreType.DMA((1,))],
        compiler_params=_cp(dimension_semantics=("arbitrary",)))(
            dpre, w_out, o, z, nw, q, k, v, g, beta, s_all, t_all)


def _sc_fwd(h, w_in, conv_w, w_out, g, b, *, first_row):
    L, D = h.shape
    W = w_out.shape[0]
    KW = conv_w.shape[0]
    tm = TM
    alpha = ALPHA

    def body(h_ref, win_hbm, cw_ref, wout_hbm, g_ref, b_ref, proj_ref, bu_ref, pre_ref, out_ref,
             win, wout, carry, sem):
        i = pl.program_id(0)
        _load_once([(win_hbm, win), (wout_hbm, wout)], sem)

        @pl.when(i == 0)
        def _():
            carry[...] = jnp.zeros_like(carry)

        hv = h_ref[...]
        hb = _bf(hv)
        bg = _dot(hb, win[:, 0:W])
        cg = _dot(hb, win[:, W:2 * W])
        xv = _dot(hb, win[:, 2 * W:3 * W])
        proj_ref[:, 0:W] = bg
        proj_ref[:, W:2 * W] = cg
        proj_ref[:, 2 * W:3 * W] = xv
        p = cg * xv
        u = _conv(cw_ref[...], _taps_back(carry[...], p, KW))
        carry[...] = p[tm - 8:tm, :]
        bu = _bf(bg * u)
        bu_ref[...] = bu
        pre = alpha * hv + _dot(bu, wout[...])
        pre_ref[...] = pre
        out_ref[...] = _ln_fwd(pre, g_ref[...], b_ref[...], _row_ids(i, tm, D), first_row)

    row = lambda i: (i, 0)
    fix = lambda i: (0, 0)
    return pl.pallas_call(
        body, name="sc_fwd", grid=(L // tm,),
        out_shape=(S((L, 3 * W), f32), S((L, W), bf16), S((L, D), f32), S((L, D), f32)),
        in_specs=[pl.BlockSpec((tm, D), row), ANY, pl.BlockSpec((KW, W), fix), ANY,
                  pl.BlockSpec((1, D), fix), pl.BlockSpec((1, D), fix)],
        out_specs=(pl.BlockSpec((tm, 3 * W), row), pl.BlockSpec((tm, W), row),
                   pl.BlockSpec((tm, D), row), pl.BlockSpec((tm, D), row)),
        scratch_shapes=[pltpu.VMEM((D, 3 * W), w_in.dtype), pltpu.VMEM((W, D), w_out.dtype),
                        pltpu.VMEM((8, W), f32), pltpu.SemaphoreType.DMA((2,))],
        compiler_params=_cp(dimension_semantics=("arbitrary",)))(h, w_in, conv_w, w_out, g, b)


def _sc_bwd(dpre, proj, conv_w, w_out, w_in, pre_in, g_in, *, first_row):
    L, D = dpre.shape
    W = w_out.shape[0]
    KW = conv_w.shape[0]
    tm = TM
    nb = L // tm
    alpha = ALPHA

    def body(dpre_ref, proj_ref, hc_ref, hx_ref, cw_ref, wout_hbm, win_hbm, pin_ref, g_ref,
             dproj_ref, dcw_ref, dpin_ref, dg_ref, db_ref, wout, win, carry, sem):
        i = pl.program_id(0)
        blk = nb - 1 - i
        _load_once([(wout_hbm, wout), (win_hbm, win)], sem)

        @pl.when(i == 0)
        def _():
            carry[...] = jnp.zeros_like(carry)
            dcw_ref[...] = jnp.zeros_like(dcw_ref)
            dg_ref[...] = jnp.zeros_like(dg_ref)
            db_ref[...] = jnp.zeros_like(db_ref)

        bg, cg, xv = proj_ref[:, 0:W], proj_ref[:, W:2 * W], proj_ref[:, 2 * W:3 * W]
        p = cg * xv
        u = _conv(cw_ref[...], _taps_back(jnp.where(blk > 0, hc_ref[...] * hx_ref[...], 0.0), p, KW))
        dpre_v = dpre_ref[...]
        d = _dot_nt(_bf(dpre_v), wout[...])
        dproj_ref[:, 0:W] = _bf(d * u)
        du = d * bg
        ahead = _taps_ahead(du, carry[...], KW)
        carry[...] = du[0:8, :]
        dp = _conv(cw_ref[...], ahead)
        for j in range(KW):
            dcw_ref[j:j + 1, :] += jnp.sum(ahead[j] * p, axis=0, keepdims=True)
        dproj_ref[:, W:2 * W] = _bf(dp * xv)
        dproj_ref[:, 2 * W:3 * W] = _bf(dp * cg)
        dh = alpha * dpre_v + _dot_nt(dproj_ref[...], win[...])
        dpin, dg, dbias = _ln_bwd_rows(dh, pin_ref[...], g_ref[...], _row_ids(blk, tm, D), first_row)
        dpin_ref[...] = dpin
        dg_ref[0:1, :] += dg
        db_ref[0:1, :] += dbias

    rev = lambda i: (nb - 1 - i, 0)
    fix = lambda i: (0, 0)

    def halo(col):
        return pl.BlockSpec((8, W), lambda i: (jnp.maximum((nb - 1 - i) * (tm // 8) - 1, 0), col))

    return pl.pallas_call(
        body, name="sc_bwd", grid=(nb,),
        out_shape=(S((L, 3 * W), bf16), S((8, W), f32), S((L, D), f32), S((8, D), f32), S((8, D), f32)),
        in_specs=[pl.BlockSpec((tm, D), rev), pl.BlockSpec((tm, 3 * W), rev), halo(1), halo(2),
                  pl.BlockSpec((KW, W), fix), ANY, ANY, pl.BlockSpec((tm, D), rev), pl.BlockSpec((1, D), fix)],
        out_specs=(pl.BlockSpec((tm, 3 * W), rev), pl.BlockSpec((8, W), fix), pl.BlockSpec((tm, D), rev),
                   pl.BlockSpec((8, D), fix), pl.BlockSpec((8, D), fix)),
        scratch_shapes=[pltpu.VMEM((W, D), w_out.dtype), pltpu.VMEM((D, 3 * W), w_in.dtype), pltpu.VMEM((8, W), f32),
                        pltpu.SemaphoreType.DMA((2,))],
        compiler_params=_cp(dimension_semantics=("arbitrary",)))(
            dpre, proj, proj, proj, conv_w, w_out, w_in, pre_in, g_in)


def _ffn_cols(F):
    fc = F
    for cand in (1408, 1024, 512, 256, 128):
        if F % cand == 0:
            fc = cand
            break
    return fc


def _ffn_fwd(h, w_up, conv_w, w_down, g, b, *, first_row, name):
    L, D = h.shape
    F = w_down.shape[0]
    KW = conv_w.shape[0]
    tm = TM
    fc = _ffn_cols(F)
    alpha = ALPHA

    def body(h_ref, wup_hbm, cw_ref, wdn_hbm, g_ref, b_ref, up_ref, a_ref, pre_ref, out_ref,
             wup, wdn, carry, sem):
        i = pl.program_id(0)
        _load_once([(wup_hbm, wup), (wdn_hbm, wdn)], sem)

        @pl.when(i == 0)
        def _():
            carry[...] = jnp.zeros_like(carry)

        hv = h_ref[...]
        hb = _bf(hv)
        pre = alpha * hv
        for c0 in range(0, F, fc):
            cs = slice(c0, c0 + fc)
            u = _dot(hb, wup[:, cs])
            gate = _dot(hb, wup[:, F + c0:F + c0 + fc])
            up_ref[:, cs] = u
            up_ref[:, F + c0:F + c0 + fc] = gate
            uc = _conv(cw_ref[:, cs], _taps_back(carry[:, cs], u, KW))
            carry[:, cs] = u[tm - 8:tm, :]
            ab = _bf(uc * _sigmoid(uc) * gate)
            a_ref[:, cs] = ab
            pre = pre + _dot(ab, wdn[cs, :])
        pre_ref[...] = pre
        out_ref[...] = _ln_fwd(pre, g_ref[...], b_ref[...], _row_ids(i, tm, D), first_row)

    row = lambda i: (i, 0)
    fix = lambda i: (0, 0)
    return pl.pallas_call(
        body, name=name, grid=(L // tm,),
        out_shape=(S((L, 2 * F), f32), S((L, F), bf16), S((L, D), f32), S((L, D), f32)),
        in_specs=[pl.BlockSpec((tm, D), row), ANY, pl.BlockSpec((KW, F), fix), ANY,
                  pl.BlockSpec((1, D), fix), pl.BlockSpec((1, D), fix)],
        out_specs=(pl.BlockSpec((tm, 2 * F), row), pl.BlockSpec((tm, F), row),
                   pl.BlockSpec((tm, D), row), pl.BlockSpec((tm, D), row)),
        scratch_shapes=[pltpu.VMEM((D, 2 * F), w_up.dtype), pltpu.VMEM((F, D), w_down.dtype),
                        pltpu.VMEM((8, F), f32), pltpu.SemaphoreType.DMA((2,))],
        compiler_params=_cp(dimension_semantics=("arbitrary",)))(h, w_up, conv_w, w_down, g, b)


def _ffn_bwd(dpre, up, w_down, conv_w, w_up, pre_in, g_in, *, first_row, name):
    L, D = dpre.shape
    F = w_down.shape[0]
    KW = conv_w.shape[0]
    tm = TM
    nb = L // tm
    fc = F
    alpha = ALPHA

    def body(dpre_ref, up_ref, halo_ref, wdn_hbm, cw_ref, wup_hbm, pin_ref, g_ref,
             dup_ref, dcw_ref, dpin_ref, dg_ref, db_ref, wdn, wup, carry, sem):
        i = pl.program_id(0)
        blk = nb - 1 - i
        _load_once([(wdn_hbm, wdn), (wup_hbm, wup)], sem)

        @pl.when(i == 0)
        def _():
            carry[...] = jnp.zeros_like(carry)
            dcw_ref[...] = jnp.zeros_like(dcw_ref)
            dg_ref[...] = jnp.zeros_like(dg_ref)
            db_ref[...] = jnp.zeros_like(db_ref)

        dpre_v = dpre_ref[...]
        db = _bf(dpre_v)
        dh = alpha * dpre_v
        for c0 in range(0, F, fc):
            cs = slice(c0, c0 + fc)
            gs_ = slice(F + c0, F + c0 + fc)
            da = _dot_nt(db, wdn[cs, :])
            gate = up_ref[:, gs_]
            u = up_ref[:, cs]
            uc = _conv(cw_ref[:, cs], _taps_back(jnp.where(blk > 0, halo_ref[:, cs], 0.0), u, KW))
            sig = _sigmoid(uc)
            dgate = _bf(da * (uc * sig))
            dup_ref[:, gs_] = dgate
            duc = da * gate * (sig * (1.0 + uc * (1.0 - sig)))
            ahead = _taps_ahead(duc, carry[:, cs], KW)
            carry[:, cs] = duc[0:8, :]
            du = _bf(_conv(cw_ref[:, cs], ahead))
            dup_ref[:, cs] = du
            for j in range(KW):
                dcw_ref[j:j + 1, cs] += jnp.sum(ahead[j] * u, axis=0, keepdims=True)
            dh = dh + _dot_nt(du, wup[:, cs]) + _dot_nt(dgate, wup[:, gs_])
        dpin, dg, dbias = _ln_bwd_rows(dh, pin_ref[...], g_ref[...], _row_ids(blk, tm, D), first_row)
        dpin_ref[...] = dpin
        dg_ref[0:1, :] += dg
        db_ref[0:1, :] += dbias

    rev = lambda i: (nb - 1 - i, 0)
    fix = lambda i: (0, 0)
    return pl.pallas_call(
        body, name=name, grid=(nb,),
        out_shape=(S((L, 2 * F), bf16), S((8, F), f32), S((L, D), f32), S((8, D), f32), S((8, D), f32)),
        in_specs=[pl.BlockSpec((tm, D), rev), pl.BlockSpec((tm, 2 * F), rev),
                  pl.BlockSpec((8, F), lambda i: (jnp.maximum((nb - 1 - i) * (tm // 8) - 1, 0), 0)),
                  ANY, pl.BlockSpec((KW, F), fix), ANY, pl.BlockSpec((tm, D), rev), pl.BlockSpec((1, D), fix)],
        out_specs=(pl.BlockSpec((tm, 2 * F), rev), pl.BlockSpec((8, F), fix), pl.BlockSpec((tm, D), rev),
                   pl.BlockSpec((8, D), fix), pl.BlockSpec((8, D), fix)),
        scratch_shapes=[pltpu.VMEM((F, D), w_down.dtype), pltpu.VMEM((D, 2 * F), w_up.dtype), pltpu.VMEM((8, F), f32),
                        pltpu.SemaphoreType.DMA((2,))],
        compiler_params=_cp(dimension_semantics=("arbitrary",)))(dpre, up, up, w_down, conv_w, w_up, pre_in, g_in)


def _loss_head(h, target, pre, g, *, first_row):
    L, D = h.shape
    tm = TM
    pb = PADF // tm

    def body(h_ref, t_ref, pre_ref, g_ref, dpre_ref, dg_ref, db_ref, loss_ref):
        i = pl.program_id(0)

        @pl.when(i == 0)
        def _():
            loss_ref[...] = jnp.zeros_like(loss_ref)
            dg_ref[...] = jnp.zeros_like(dg_ref)
            db_ref[...] = jnp.zeros_like(db_ref)

        valid = i >= pb
        err = h_ref[...] - t_ref[...]
        dh = jnp.where(valid, err * (1.0 / D), 0.0)
        part = 0.5 * jnp.sum(jnp.sum(err * err, axis=-1, keepdims=True) * (1.0 / D), axis=0, keepdims=True)
        loss_ref[...] += jnp.where(valid, part, 0.0)
        dpre, dg, db = _ln_bwd_rows(dh, pre_ref[...], g_ref[...], _row_ids(i, tm, D), first_row)
        dpre_ref[...] = dpre
        dg_ref[0:1, :] += dg
        db_ref[0:1, :] += db

    row = lambda i: (i, 0)
    fix = lambda i: (0, 0)
    return pl.pallas_call(
        body, name="loss_head", grid=(L // tm,),
        out_shape=(S((L, D), f32), S((8, D), f32), S((8, D), f32), S((8, LANE), f32)),
        in_specs=[pl.BlockSpec((tm, D), row), pl.BlockSpec((tm, D), lambda i: (jnp.maximum(i - pb, 0), 0)),
                  pl.BlockSpec((tm, D), row), pl.BlockSpec((1, D), fix)],
        out_specs=(pl.BlockSpec((tm, D), row), pl.BlockSpec((8, D), fix), pl.BlockSpec((8, D), fix),
                   pl.BlockSpec((8, LANE), fix)),
        compiler_params=_cp(dimension_semantics=("arbitrary",)))(h, target, pre, g)


def _adamw(g_terms, w, m, v, *, name):
    R, C = w.shape
    tr = _row_tile(R)
    n = len(g_terms)
    c1 = 1.0 - ADAM_B1 ** ADAM_STEP
    c2 = 1.0 - ADAM_B2 ** ADAM_STEP

    def body(*refs):
        g = refs[0][...].astype(f32)
        for r in refs[1:n]:
            g = g + r[...].astype(f32)
        w_ref, m_ref, v_ref, g_out, d_out, m_out, v_out = refs[n:]
        mn = ADAM_B1 * m_ref[...] + (1.0 - ADAM_B1) * g
        vn = ADAM_B2 * v_ref[...] + (1.0 - ADAM_B2) * (g * g)
        g_out[...] = g
        m_out[...] = mn
        v_out[...] = vn
        d_out[...] = -ADAM_LR * ((mn / c1) / (jnp.sqrt(vn / c2) + ADAM_EPS) + ADAM_WD * w_ref[...])

    spec = pl.BlockSpec((tr, C), lambda i: (i, 0))
    return pl.pallas_call(
        body, name=name, grid=(R // tr,), out_shape=(S((R, C), f32),) * 4,
        in_specs=[spec] * (n + 3), out_specs=(spec,) * 4,
        compiler_params=_cp(dimension_semantics=("arbitrary",)))(*g_terms, w, m, v)


def _sum_devices(x):
    n, R, C = x.shape

    def body(x_ref, o_ref):
        acc = x_ref[0]
        for d in range(1, n):
            acc = acc + x_ref[d]
        o_ref[...] = acc

    return pl.pallas_call(body, name="sum_devices", out_shape=S((R, C), f32), compiler_params=_cp())(x)


def _row_tile(R):
    for step in (16, 8):
        for t in range(256, 0, -step):
            if R % t == 0:
                return t
    return R


def _adamw_direct(s32s, recvs, w, m, v, me, *, name):
    L, K, n = w.shape
    tk = _row_tile(K)
    c1 = 1.0 - ADAM_B1 ** ADAM_STEP
    c2 = 1.0 - ADAM_B2 ** ADAM_STEP

    def body(me_ref, *refs):
        own_refs, recv_refs = refs[:L], refs[L:2 * L]
        w_ref, m_ref, v_ref, g_out, d_out, m_out, v_out = refs[2 * L:]
        for li in range(L):
            @pl.when(pl.program_id(0) == li)
            def _(li=li):
                g = own_refs[li][0, 0]
                for d in range(N_DEV):
                    g = g + recv_refs[li][d, 0].astype(f32)
                mn = ADAM_B1 * m_ref[0] + (1.0 - ADAM_B1) * g
                vn = ADAM_B2 * v_ref[0] + (1.0 - ADAM_B2) * (g * g)
                g_out[0] = g
                m_out[0] = mn
                v_out[0] = vn
                d_out[0] = -ADAM_LR * ((mn / c1) / (jnp.sqrt(vn / c2) + ADAM_EPS) + ADAM_WD * w_ref[0])

    own = pl.BlockSpec((1, tk, n), lambda l, i, ix: (l, i, 0))
    grid_spec = pltpu.PrefetchScalarGridSpec(
        num_scalar_prefetch=1, grid=(L, K // tk),
        in_specs=[pl.BlockSpec((1, 1, tk, n), lambda l, i, ix: (ix[0], 0, i, 0))] * L
        + [pl.BlockSpec((N_DEV, 1, tk, n), lambda l, i, ix: (0, 0, i, 0))] * L + [own, own, own],
        out_specs=(own,) * 4)
    return pl.pallas_call(
        body, name=name, grid_spec=grid_spec, out_shape=(S((L, K, n), f32),) * 4,
        compiler_params=_cp(dimension_semantics=("arbitrary", "arbitrary")))(me, *s32s, *recvs, w, m, v)


def _col_segments(n, mapping):
    segs = []
    for p in range(N_DEV):
        lo, hi = p * n, (p + 1) * n
        out = []
        for c0, c1, e0 in mapping:
            a, b = max(lo, c0), min(hi, c1)
            if a < b:
                out.append((a - lo, e0 + (a - c0), b - a))
        segs.append(out)
    return segs


def _assemble_cols(gathered, mapping, n_out, *, name):
    _, L, K, n = gathered.shape
    tk = _row_tile(K)
    segs = _col_segments(n, mapping)
    covered = sum(w for s in segs for (_, _, w) in s)

    def body(g_ref, o_ref):
        if covered != n_out:
            o_ref[...] = jnp.zeros_like(o_ref)
        for p in range(N_DEV):
            for s0, d0, w in segs[p]:
                o_ref[0, :, d0:d0 + w] = g_ref[p, 0, :, s0:s0 + w]

    return pl.pallas_call(
        body, name=name, grid=(L, K // tk), out_shape=S((L, K, n_out), gathered.dtype),
        in_specs=[pl.BlockSpec((N_DEV, 1, tk, n), lambda l, i: (0, l, i, 0))],
        out_specs=pl.BlockSpec((1, tk, n_out), lambda l, i: (l, i, 0)),
        compiler_params=_cp(dimension_semantics=("arbitrary", "arbitrary")))(gathered)


def _split_cols(dws, mapping, n, *, name):
    L = len(dws)
    K, n_in = dws[0].shape
    tk = _row_tile(K)
    segs = _col_segments(n, mapping)

    def body(*refs):
        ins, o32, o16 = refs[:L], refs[L], refs[L + 1]
        for li in range(L):
            @pl.when(pl.program_id(0) == li)
            def _(li=li):
                for p in range(N_DEV):
                    for s0, d0, w in segs[p]:
                        val = ins[li][:, d0:d0 + w]
                        o32[p, 0, :, s0:s0 + w] = val
                        o16[p, 0, :, s0:s0 + w] = _bf(val)

    out = pl.BlockSpec((N_DEV, 1, tk, n), lambda l, i: (0, l, i, 0))
    return pl.pallas_call(
        body, name=name, grid=(L, K // tk), out_shape=(S((N_DEV, L, K, n), f32), S((N_DEV, L, K, n), bf16)),
        in_specs=[pl.BlockSpec((tk, n_in), lambda l, i: (i, 0))] * L, out_specs=(out, out),
        compiler_params=_cp(dimension_semantics=("arbitrary", "arbitrary")))(*dws)


def _split_rows(dws, k, *, name):
    L = len(dws)
    N = dws[0].shape[1]

    def body(*refs):
        ins, o32, o16 = refs[:L], refs[L], refs[L + 1]
        for li in range(L):
            @pl.when(pl.program_id(0) == li)
            def _(li=li):
                val = ins[li][...]
                o32[0, 0] = val
                o16[0, 0] = _bf(val)

    out = pl.BlockSpec((1, 1, k, N), lambda l, p: (p, l, 0, 0))
    return pl.pallas_call(
        body, name=name, grid=(L, N_DEV), out_shape=(S((N_DEV, L, k, N), f32), S((N_DEV, L, k, N), bf16)),
        in_specs=[pl.BlockSpec((k, N), lambda l, p: (p, 0))] * L, out_specs=(out, out),
        compiler_params=_cp(dimension_semantics=("arbitrary", "arbitrary")))(*dws)


def _rows_full(gathered):
    _, L, k, N = gathered.shape
    return jnp.transpose(gathered, (1, 0, 2, 3)).reshape(L, N_DEV * k, N)


def _all_gather(xs, *, name):
    na = len(xs)

    def body(*refs):
        x_refs, out_refs = refs[:na], refs[na:2 * na]
        send_sems, recv_sems, local_sems = refs[2 * na:]
        mx, my, mc = lax.axis_index("x"), lax.axis_index("y"), lax.axis_index("c")
        me, sibling = (mx, my, mc), (mx, my, 1 - mc)
        chips = [(1 - mx, my), (mx, 1 - my), (1 - mx, 1 - my)]

        def slot(a, px, py, pc):
            return out_refs[a].at[4 * px + 2 * py + pc]

        def copy(a, kk, block, to, src=None):
            return pltpu.make_async_remote_copy(
                src_ref=slot(a, *block) if src is None else src, dst_ref=slot(a, *block),
                send_sem=send_sems.at[7 * a + kk], recv_sem=recv_sems.at[7 * a + kk], device_id=to, device_id_type=MESH)

        mine = [pltpu.make_async_copy(x_refs[a], slot(a, *me), local_sems.at[a]) for a in range(na)]
        for cp in mine:
            cp.start()
        first = []
        for a in range(na):
            first.append(copy(a, 0, me, sibling, src=x_refs[a]))
            first += [copy(a, 1 + j, me, (*chip, mc), src=x_refs[a]) for j, chip in enumerate(chips)]
        for cp in first:
            cp.start()
        passed = []
        for j, chip in enumerate(chips):
            for a in range(na):
                copy(a, 1 + j, (*chip, mc), me).wait_recv()
                fwd = copy(a, 4 + j, (*chip, mc), sibling)
                fwd.start()
                passed.append(fwd)
        for a in range(na):
            copy(a, 0, sibling, me).wait_recv()
            for j, chip in enumerate(chips):
                copy(a, 4 + j, (*chip, 1 - mc), me).wait_recv()
        for cp in first + passed:
            cp.wait_send()
        for cp in mine:
            cp.wait()

    return pl.pallas_call(
        body, name=name, out_shape=tuple(S((N_DEV,) + x.shape, x.dtype) for x in xs),
        in_specs=[ANY] * na, out_specs=(ANY,) * na,
        scratch_shapes=[pltpu.SemaphoreType.DMA((7 * na,)), pltpu.SemaphoreType.DMA((7 * na,)),
                        pltpu.SemaphoreType.DMA((na,))],
        compiler_params=pltpu.CompilerParams(has_side_effects=True))(*xs)


_FLIPS = [(fx, fy, fc) for fx in (0, 1) for fy in (0, 1) for fc in (0, 1)][1:]


def _flip_peer(flip):
    x, y, c = lax.axis_index("x"), lax.axis_index("y"), lax.axis_index("c")
    return tuple(1 - a if f else a for a, f in zip((x, y, c), flip))


def _dev_index(p):
    return 4 * p[0] + 2 * p[1] + p[2]


HBM_SPEC = pl.BlockSpec(memory_space=pltpu.HBM)
SEM_SPEC = pl.BlockSpec(memory_space=pltpu.SEMAPHORE)


def _direct_start(srcs, lands, per_peer, *, name):
    na = len(srcs)

    def body(*refs):
        src_refs, land_refs = refs[:na], refs[na:2 * na]
        send_sems, recv_sems = refs[2 * na], refs[2 * na + 1]
        token = refs[-1]
        me = _dev_index((lax.axis_index("x"), lax.axis_index("y"), lax.axis_index("c")))
        for a in range(na):
            for r, flip in enumerate(_FLIPS):
                peer = _flip_peer(flip)
                src = src_refs[a].at[_dev_index(peer)] if per_peer else src_refs[a]
                pltpu.make_async_remote_copy(
                    src_ref=src, dst_ref=land_refs[a].at[me], send_sem=send_sems.at[7 * a + r],
                    recv_sem=recv_sems.at[7 * a + r], device_id=peer, device_id_type=MESH).start()
        token[...] = jnp.zeros_like(token)

    hbm = lambda t: pltpu.with_memory_space_constraint(t, pltpu.HBM)
    out = pl.pallas_call(
        body, name=name,
        out_shape=(pltpu.SemaphoreType.DMA((7 * na,)), pltpu.SemaphoreType.DMA((7 * na,)))
        + tuple(pltpu.HBM(t.shape, t.dtype) for t in list(srcs) + list(lands)) + (S((8, LANE), f32),),
        in_specs=[HBM_SPEC] * (2 * na),
        out_specs=(SEM_SPEC, SEM_SPEC) + (HBM_SPEC,) * (2 * na) + (pl.BlockSpec(memory_space=pltpu.VMEM),),
        input_output_aliases={i: 2 + i for i in range(2 * na)},
        compiler_params=pltpu.CompilerParams(has_side_effects=pltpu.SideEffectType.DATAFLOW_SIDE_EFFECTING))(
            *[hbm(t) for t in srcs], *[hbm(t) for t in lands])
    return out[0], out[1], list(out[2:2 + na]), list(out[2 + na:2 + 2 * na]), out[-1]


def _direct_wait(send_sems, recv_sems, srcs, lands, per_peer, after, *, name):
    na = len(srcs)

    def body(*refs):
        src_refs, land_refs = refs[:na], refs[na:2 * na]
        ssem, rsem = refs[2 * na], refs[2 * na + 1]
        me = _dev_index((lax.axis_index("x"), lax.axis_index("y"), lax.axis_index("c")))
        for a in range(na):
            for r, flip in enumerate(_FLIPS):
                peer = _flip_peer(flip)
                src = src_refs[a].at[_dev_index(peer)] if per_peer else src_refs[a]
                cp = pltpu.make_async_remote_copy(
                    src_ref=src, dst_ref=land_refs[a].at[me], send_sem=ssem.at[7 * a + r],
                    recv_sem=rsem.at[7 * a + r], device_id=peer, device_id_type=MESH)
                cp.wait_send()
                cp.wait_recv()

    out = pl.pallas_call(
        body, name=name, out_shape=tuple(pltpu.HBM(t.shape, t.dtype) for t in list(srcs) + list(lands)),
        in_specs=[HBM_SPEC] * (2 * na) + [SEM_SPEC, SEM_SPEC, ANY], out_specs=(HBM_SPEC,) * (2 * na),
        input_output_aliases={i: i for i in range(2 * na)},
        compiler_params=pltpu.CompilerParams(has_side_effects=pltpu.SideEffectType.DATAFLOW_SIDE_EFFECTING))(
            *srcs, *lands, send_sems, recv_sems, after)
    return list(out[:na]), list(out[na:])


def _pack_small(parts, width):
    rows, offs, r = [], [], 0
    for a in parts:
        n = a.size
        nr = -(-n // width)
        flat = a.reshape(-1).astype(f32)
        if nr * width != n:
            flat = jnp.pad(flat, (0, nr * width - n))
        rows.append(flat.reshape(nr, width))
        offs.append((r, nr))
        r += nr
    buf = jnp.concatenate(rows, axis=0)
    pad = (-r) % 8
    if pad:
        buf = jnp.pad(buf, ((0, pad), (0, 0)))
    return buf, offs


def _unpack_small(buf, off, shape):
    r, nr = off
    return buf[r:r + nr].reshape(-1)[:math.prod(shape)].reshape(shape)


def _local_step(x, target, meta, a_w_in, a_w_out, small, start_token, late_weights, grads_ready):
    SEQ, D = x.shape
    n_meta = meta.shape[0]
    first_row = PADF - n_meta
    H = small["a_log"].shape[-1]

    head = jnp.concatenate([jnp.zeros((first_row, D), f32), meta], axis=0)

    def lanes(a):
        return jnp.pad(a.reshape(1, -1), ((0, 0), (0, LANE - a.size)))

    def after_token(a, token):
        return a if token is None else a + token[0:1, 0:1]

    alog, dtb = after_token(lanes(small["a_log"][0]), start_token), lanes(small["a_dt_bias"][0])
    a_conv, b_conv = small["a_conv"][0], small["b_conv"][0]
    nw = small["a_norm"][0].reshape(1, DH)
    lmg, lmb, lfg, lfb = small["ln_mix_g"], small["ln_mix_b"], small["ln_ffn_g"], small["ln_ffn_b"]

    h0, pre_a, z, raw, q, k, v, beta, g, t_all = _gdn_in_fwd(x, head, a_w_in, a_conv, alog, dtb,
                                                             first_row=first_row, H=H)
    o, y, s_all, pre1, h1 = _delta_fwd(q, k, v, g, beta, t_all, z, nw, h0, a_w_out, lmg[0:1], lmb[0:1],
                                       first_row=first_row, H=H)
    wts = late_weights(h1)
    up0, act0, pre2, h2 = _ffn_fwd(h1, wts["ffn_w_up"][0], small["ffn_conv"][0], wts["ffn_w_down"][0],
                                   lfg[0:1], lfb[0:1], first_row=first_row, name="ffn_fwd0")
    proj_b, bu, pre3, h3 = _sc_fwd(h2, wts["b_w_in"], b_conv, wts["b_w_out"], lmg[1:2], lmb[1:2], first_row=first_row)
    up1, act1, pre4, h4 = _ffn_fwd(h3, wts["ffn_w_up"][1], small["ffn_conv"][1], wts["ffn_w_down"][1],
                                   lfg[1:2], lfb[1:2], first_row=first_row, name="ffn_fwd1")
    gs = {}
    dpre4, dlfg1, dlfb1, loss_tile = _loss_head(h4, target, pre4, lfg[1:2], first_row=first_row)

    def ffn_backward(dpre, up, act, h_in, layer, tag, ln_in, token=None):
        dup, dcw, dpre_in, dg, db = _ffn_bwd(
            dpre, up, wts["ffn_w_down"][layer], after_token(small["ffn_conv"][layer], token),
            wts["ffn_w_up"][layer], ln_in[0], ln_in[1], first_row=first_row, name="ffn_bwd" + tag)
        dwd = _linear_dw(act, dpre, name="dw_down" + tag)
        dwu = _linear_dw(h_in, dup, name="dw_up" + tag)
        return dpre_in, dg, db, dwu, dwd, dcw[0:3]

    dpre3, dlmg1, dlmb1, dwu1, dwd1, dcf1 = ffn_backward(dpre4, up1, act1, h3, 1, "1", (pre3, lmg[1:2]))

    dwb_out = _linear_dw(bu, dpre3, name="dw_b_out")
    dproj_b, dcb, dpre2, dlfg0, dlfb0 = _sc_bwd(dpre3, proj_b, b_conv, wts["b_w_out"], wts["b_w_in"], pre2, lfg[0:1],
                                                first_row=first_row)
    dwb_in = _linear_dw(h2, dproj_b, name="dw_b_in")
    token = grads_ready("layer1", dict(ffn_w_up=dwu1, ffn_w_down=dwd1, b_w_in=dwb_in, b_w_out=dwb_out))

    dpre1, dlmg0, dlmb0, dwu0, dwd0, dcf0 = ffn_backward(dpre2, up0, act0, h1, 0, "0", (pre1, lmg[0:1]), token)
    dwa_out = _linear_dw(y, dpre1, name="dw_a_out")
    token = grads_ready("layer0", dict(ffn_w_up=dwu0, ffn_w_down=dwd0, a_w_out=dwa_out))

    dq, dk, dv, dz, dg_, dbeta, dnw = _delta_bwd(dpre1, a_w_out, o, z, after_token(nw, token), q, k, v, g, beta,
                                                 s_all, t_all, H=H)
    dproj_a, dca, dal, ddt, grad_x, dhead = _gdn_in_bwd(dq, dk, dv, dz, dg_, dbeta, pre_a, raw, a_conv, alog, dtb,
                                                        a_w_in, dpre1, first_row=first_row, H=H)
    grads_ready("last", dict(a_w_in=_linear_dw(h0, dproj_a, name="dw_a_in")))

    gs["meta"] = dhead[first_row:PADF]
    gs["a_conv"] = dca[0:a_conv.shape[0]][None]
    gs["a_log"] = dal[0:1, 0:H]
    gs["a_dt_bias"] = ddt[0:1, 0:H]
    gs["a_norm"] = dnw[0:1]
    gs["b_conv"] = dcb[0:b_conv.shape[0]][None]
    gs["ln_mix_g"] = jnp.stack([dlmg0[0], dlmg1[0]])
    gs["ln_mix_b"] = jnp.stack([dlmb0[0], dlmb1[0]])
    gs["ffn_conv"] = jnp.stack([dcf0, dcf1])
    gs["ln_ffn_g"] = jnp.stack([dlfg0[0], dlfg1[0]])
    gs["ln_ffn_b"] = jnp.stack([dlfb0[0], dlfb1[0]])
    return loss_tile, grad_x, gs


_BIG = ("a_w_in", "a_w_out", "b_w_in", "b_w_out", "ffn_w_up", "ffn_w_down")
_BIG_COL = ("a_w_in", "b_w_in", "ffn_w_up")
_SMALL = ("meta", "a_conv", "a_log", "a_dt_bias", "a_norm", "b_conv", "ln_mix_g", "ln_mix_b",
          "ffn_conv", "ln_ffn_g", "ln_ffn_b")
_SMALL_SHARDED = ("meta", "a_conv", "b_conv", "ffn_conv")
_ORDER = ("meta", "a_w_in", "a_conv", "a_log", "a_dt_bias", "a_norm", "a_w_out", "b_w_in", "b_conv", "b_w_out",
          "ln_mix_g", "ln_mix_b", "ffn_w_up", "ffn_conv", "ffn_w_down", "ln_ffn_g", "ln_ffn_b")


def _a_w_in_map(H):
    W4 = 4 * H * DH
    return [(0, W4, 0), (W4, W4 + H, W4), (W4 + H, W4 + 2 * H, W4 + LANE)], W4 + 2 * LANE


def kernel(x, meta, a_w_in, a_conv, a_log, a_dt_bias, a_norm, a_w_out, b_w_in, b_conv, b_w_out, ln_mix_g, ln_mix_b, ffn_w_up, ffn_conv, ffn_w_down, ln_ffn_g, ln_ffn_b, loss_target, m_meta, m_a_w_in, m_a_conv, m_a_log, m_a_dt_bias, m_a_norm, m_a_w_out, m_b_w_in, m_b_conv, m_b_w_out, m_ln_mix_g, m_ln_mix_b, m_ffn_w_up, m_ffn_conv, m_ffn_w_down, m_ln_ffn_g, m_ln_ffn_b, v_meta, v_a_w_in, v_a_conv, v_a_log, v_a_dt_bias, v_a_norm, v_a_w_out, v_b_w_in, v_b_conv, v_b_w_out, v_ln_mix_g, v_ln_mix_b, v_ffn_w_up, v_ffn_conv, v_ffn_w_down, v_ln_ffn_g, v_ln_ffn_b):
    wloc = dict(meta=meta, a_w_in=a_w_in, a_conv=a_conv, a_log=a_log, a_dt_bias=a_dt_bias, a_norm=a_norm,
                a_w_out=a_w_out, b_w_in=b_w_in, b_conv=b_conv, b_w_out=b_w_out, ln_mix_g=ln_mix_g, ln_mix_b=ln_mix_b,
                ffn_w_up=ffn_w_up, ffn_conv=ffn_conv, ffn_w_down=ffn_w_down, ln_ffn_g=ln_ffn_g, ln_ffn_b=ln_ffn_b)
    mloc = dict(meta=m_meta, a_w_in=m_a_w_in, a_conv=m_a_conv, a_log=m_a_log, a_dt_bias=m_a_dt_bias, a_norm=m_a_norm,
                a_w_out=m_a_w_out, b_w_in=m_b_w_in, b_conv=m_b_conv, b_w_out=m_b_w_out, ln_mix_g=m_ln_mix_g,
                ln_mix_b=m_ln_mix_b, ffn_w_up=m_ffn_w_up, ffn_conv=m_ffn_conv, ffn_w_down=m_ffn_w_down,
                ln_ffn_g=m_ln_ffn_g, ln_ffn_b=m_ln_ffn_b)
    vloc = dict(meta=v_meta, a_w_in=v_a_w_in, a_conv=v_a_conv, a_log=v_a_log, a_dt_bias=v_a_dt_bias, a_norm=v_a_norm,
                a_w_out=v_a_w_out, b_w_in=v_b_w_in, b_conv=v_b_conv, b_w_out=v_b_w_out, ln_mix_g=v_ln_mix_g,
                ln_mix_b=v_ln_mix_b, ffn_w_up=v_ffn_w_up, ffn_conv=v_ffn_conv, ffn_w_down=v_ffn_w_down,
                ln_ffn_g=v_ln_ffn_g, ln_ffn_b=v_ln_ffn_b)
    H = a_log.shape[-1]
    mx, my, mc = lax.axis_index("x"), lax.axis_index("y"), lax.axis_index("c")
    me = 4 * mx + 2 * my + mc

    a_map, a_cols = _a_w_in_map(H)
    col_maps = {"a_w_in": (a_map, a_cols)}
    for n in ("b_w_in", "ffn_w_up"):
        ncols = N_DEV * wloc[n].shape[-1]
        col_maps[n] = ([(0, ncols, 0)], ncols)
    sm_sh = [wloc[n] for n in _SMALL_SHARDED]
    sbuf, soffs = _pack_small(sm_sh, 128)
    g_a_w_in, g_a_w_out, sg = _all_gather([_bf(wloc["a_w_in"]), _bf(wloc["a_w_out"]), sbuf], name="gather_first")
    w_a_in = _assemble_cols(g_a_w_in, *col_maps["a_w_in"], name="assemble_a_w_in")[0]
    w_a_out = _rows_full(g_a_w_out)[0]
    late = [n for n in _BIG if n not in ("a_w_in", "a_w_out")]
    ssem, rsem, srcs_t, lands_t, start_token = _direct_start(
        [_bf(wloc[n]) for n in late], [lax.empty((N_DEV,) + wloc[n].shape, bf16) for n in late], False,
        name="gather_rest_start")

    def late_weights(after):
        srcs_d, landed = _direct_wait(ssem, rsem, srcs_t, lands_t, False, after, name="gather_rest_wait")
        wts = {}
        for n, own, got in zip(late, srcs_d, landed):
            full = lax.dynamic_update_index_in_dim(got, own, me, 0)
            if n in _BIG_COL:
                wts[n] = _assemble_cols(full, *col_maps[n], name="assemble_" + n)
            else:
                wts[n] = _rows_full(full)
        for n in ("b_w_in", "b_w_out"):
            wts[n] = wts[n][0]
        return wts

    small = {n: wloc[n] for n in _SMALL}
    for n, off in zip(_SMALL_SHARDED, soffs):
        sh = wloc[n].shape
        parts = jnp.stack([_unpack_small(sg[d], off, sh) for d in range(N_DEV)])
        nd = len(sh)
        small[n] = jnp.transpose(parts, tuple(range(1, nd)) + (0, nd)).reshape(sh[:-1] + (N_DEV * sh[-1],))

    def split(n, dws, tag):
        if n in _BIG_COL:
            return _split_cols(dws, col_maps[n][0], wloc[n].shape[-1], name="split_" + n + tag)
        return _split_rows(dws, wloc[n].shape[-2], name="split_" + n + tag)

    sent = {}

    def grads_ready(stage, grads):
        names = sorted(grads)
        parts = [split(n, [grads[n]], "_" + stage) for n in names]
        handles = _direct_start([p[1] for p in parts], [jnp.zeros(p[1].shape, bf16) for p in parts], True,
                                name="grads_" + stage + "_start")
        sent[stage] = (names, [p[0] for p in parts], handles)
        return handles[4]

    loss_tile, grad_x, gs = _local_step(x[0], loss_target[0], small["meta"], w_a_in, w_a_out, small, start_token,
                                        late_weights, grads_ready)

    def landed(stage, after):
        names, own32, (ssem_g, rsem_g, srcs_g, lands_g, _) = sent[stage]
        _, got = _direct_wait(ssem_g, rsem_g, srcs_g, lands_g, True, after, name="grads_" + stage + "_wait")
        return list(zip(names, own32, got))

    parts = {}
    for stage in ("layer0", "layer1"):
        for n, o32, r in landed(stage, grad_x):
            parts.setdefault(n, []).append((o32, r))
    me1 = jnp.stack([me]).astype(jnp.int32)
    big_out = {n: _adamw_direct([p[0] for p in ps], [p[1] for p in ps], wloc[n], mloc[n], vloc[n], me1,
                                name="adamw_" + n) for n, ps in parts.items()}
    names = list(_SMALL)
    pbuf, poffs = _pack_small([gs[n] for n in names] + [loss_tile[0:1, 0:1]], 1024)
    psum = _sum_devices(_all_gather([pbuf], name="gather_small_grads")[0])
    loss = psum[poffs[-1][0], 0]
    g_small = {}
    for n, off in zip(names, poffs[:-1]):
        full_shape = gs[n].shape
        gfull = _unpack_small(psum, off, full_shape)
        if n in _SMALL_SHARDED:
            ns = wloc[n].shape[-1]
            gfull = lax.dynamic_slice_in_dim(gfull, me * ns, ns, axis=gfull.ndim - 1)
        g_small[n] = gfull.reshape(wloc[n].shape)
    gbuf, aoffs = _pack_small([g_small[n] for n in names], 128)
    wbuf, _ = _pack_small([wloc[n] for n in names], 128)
    mbuf, _ = _pack_small([mloc[n] for n in names], 128)
    vbuf, _ = _pack_small([vloc[n] for n in names], 128)
    _, d_s, m_s, v_s = _adamw([gbuf], wbuf, mbuf, vbuf, name="adamw_small")

    done = d_s[0, 0]
    for out in big_out.values():
        done = done + out[1][0, 0, 0]
    (n, o32, r), = landed("last", done.reshape(1, 1))
    big_out[n] = _adamw_direct([o32], [r], wloc[n], mloc[n], vloc[n], me1, name="adamw_" + n)

    grads, deltas, new_m, new_v = {}, {}, {}, {}
    for n in _BIG:
        grads[n], deltas[n], new_m[n], new_v[n] = big_out[n]
    for n, off in zip(names, aoffs):
        sh = wloc[n].shape
        grads[n] = g_small[n]
        deltas[n], new_m[n], new_v[n] = (_unpack_small(b_, off, sh) for b_ in (d_s, m_s, v_s))
    return (loss, grad_x[None], *[grads[n] for n in _ORDER], *[deltas[n] for n in _ORDER],
            *[new_m[n] for n in _ORDER], *[new_v[n] for n in _ORDER])
```

```python
import math

import jax
import jax.numpy as jnp
from jax import lax
from jax.experimental import pallas as pl
from jax.experimental.pallas import tpu as pltpu

f32, bf16 = jnp.float32, jnp.bfloat16
S = jax.ShapeDtypeStruct
HI = lax.Precision.HIGHEST
HI3 = lax.Precision.HIGH
MESH = pl.DeviceIdType.MESH

V7X_VMEM_LIMIT = 56 * 1024 * 1024
LANE = 128
DH = 128
CH = 64
PADF = 256
TM = 256
TMM = 768
N_DEV = 8
BWD_HEAD_GROUP = 4

DEPTH = 2
ALPHA = (2.0 * DEPTH) ** 0.25
LN_EPS = 1e-5
RMS_EPS = 1e-6
L2_EPS = 1e-6
ADAM_LR, ADAM_B1, ADAM_B2, ADAM_EPS, ADAM_WD, ADAM_STEP = 0.001, 0.9, 0.999, 1e-08, 0.01, 10


def _cp(**kw):
    return pltpu.CompilerParams(vmem_limit_bytes=V7X_VMEM_LIMIT, **kw)


def _bf(x):
    return x.astype(bf16)


def _dot(a, b, precision=None):
    return jnp.dot(a, b, preferred_element_type=f32, precision=precision)


def _dot_nt(a, b):
    return lax.dot_general(a, b, (((1,), (1,)), ((), ())), preferred_element_type=f32)


def _dot_tn(a, b):
    return lax.dot_general(a, b, (((0,), (0,)), ((), ())), preferred_element_type=f32)


def _sigmoid(x):
    return 1.0 / (1.0 + jnp.exp(-x))


def _load_once(pairs, sem):
    @pl.when(pl.program_id(0) == 0)
    def _():
        cps = [pltpu.make_async_copy(src, dst, sem.at[n]) for n, (src, dst) in enumerate(pairs)]
        for c in cps:
            c.start()
        for c in cps:
            c.wait()


def _row_ids(i, tm, width):
    return i * tm + lax.broadcasted_iota(jnp.int32, (tm, width), 0)


def _ln_fwd(pre, g, b, rows, first_row):
    mu = jnp.mean(pre, axis=-1, keepdims=True)
    xc = pre - mu
    var = jnp.mean(xc * xc, axis=-1, keepdims=True)
    y = xc * lax.rsqrt(var + LN_EPS) * g + b
    return jnp.where(rows >= first_row, y, 0.0)


ANY = pl.BlockSpec(memory_space=pl.ANY)


def _taps_back(prev8, x, kw):
    xe = jnp.concatenate([prev8, x], axis=0)
    return [pltpu.roll(xe, kw - 1 - j, 0)[8:] for j in range(kw - 1)] + [x]


def _taps_ahead(x, next8, kw):
    n = x.shape[0]
    xe = jnp.concatenate([x, next8], axis=0)
    return [pltpu.roll(xe, n + 8 - (kw - 1 - j), 0)[:n] for j in range(kw - 1)] + [x]


def _conv(cw, taps):
    acc = cw[0:1, :] * taps[0]
    for j in range(1, len(taps)):
        acc = acc + cw[j:j + 1, :] * taps[j]
    return acc


def _linear_dw(x, dy, *, name, after=None):
    L, K = x.shape
    N = dy.shape[1]
    tm = TMM if L % TMM == 0 else TM
    tn = LANE
    for d in range(N // LANE, 0, -1):
        if (N // LANE) % d == 0 and K * d * LANE * 4 <= 9 * 1024 * 1024:
            tn = d * LANE
            break

    def body(x_ref, dy_ref, *rest):
        o_ref = rest[-1]

        @pl.when(pl.program_id(1) == 0)
        def _():
            o_ref[...] = jnp.zeros_like(o_ref)
        o_ref[...] += _dot_tn(_bf(x_ref[...]), _bf(dy_ref[...]))

    in_specs = [pl.BlockSpec((tm, K), lambda j, i: (i, 0)), pl.BlockSpec((tm, tn), lambda j, i: (i, j))]
    args = [x, dy]
    if after is not None:
        in_specs.append(pl.BlockSpec(after.shape, lambda j, i: (0, 0)))
        args.append(after)
    return pl.pallas_call(
        body, name=name, grid=(N // tn, L // tm), out_shape=S((K, N), f32),
        in_specs=in_specs, out_specs=pl.BlockSpec((K, tn), lambda j, i: (0, j)),
        compiler_params=_cp(dimension_semantics=("arbitrary", "arbitrary")))(*args)


def _ln_bwd_rows(dout, pre, g, rows, first_row):
    mu = jnp.mean(pre, axis=-1, keepdims=True)
    xc = pre - mu
    rstd = lax.rsqrt(jnp.mean(xc * xc, axis=-1, keepdims=True) + LN_EPS)
    xh = xc * rstd
    dy = jnp.where(rows >= first_row, dout, 0.0)
    dxh = dy * g
    dpre = rstd * (dxh - jnp.mean(dxh, axis=-1, keepdims=True) - xh * jnp.mean(dxh * xh, axis=-1, keepdims=True))
    return dpre, jnp.sum(dy * xh, axis=0, keepdims=True), jnp.sum(dy, axis=0, keepdims=True)


def _gdn_in_fwd(x, head, w_full, conv_w, alog, dtb, *, first_row, H):
    D = x.shape[1]
    L = PADF + x.shape[0]
    W = H * DH
    NW = w_full.shape[1]
    KW = conv_w.shape[0]
    tm = TM
    pb = PADF // tm

    def body(x_ref, head_ref, w_hbm, cw_ref, alog_ref, dtb_ref,
             h_ref, pre_ref, z_ref, raw_ref, q_ref, k_ref, v_ref, beta_ref, g_ref, t_ref,
             w_vmem, carry, sem):
        i = pl.program_id(0)
        _load_once([(w_hbm, w_vmem)], sem)

        @pl.when(i == 0)
        def _():
            carry[...] = jnp.zeros_like(carry)

        hv = jnp.where(i < pb, head_ref[...], x_ref[...])
        h_ref[...] = hv
        hb = _bf(hv)
        outs = (q_ref, k_ref, v_ref)

        def section(s):
            pre = _dot(hb, w_vmem[:, s * W:(s + 1) * W])
            pre_ref[:, s * W:(s + 1) * W] = pre
            c = _conv(cw_ref[:, s * W:(s + 1) * W], _taps_back(carry[s], pre, KW))
            carry[s] = pre[tm - 8:tm, :]
            sl = c * _sigmoid(c)
            if s < 2:
                scale = DH ** -0.5 if s == 0 else 1.0
                for hh in range(H):
                    seg = sl[:, hh * DH:(hh + 1) * DH]
                    r = lax.rsqrt(jnp.sum(seg * seg, axis=-1, keepdims=True) + L2_EPS)
                    outs[s][:, hh * DH:(hh + 1) * DH] = seg * (r * scale)
            else:
                v_ref[...] = sl

        raw = _dot(hb, w_vmem[:, 4 * W:4 * W + 2 * LANE])
        raw_ref[...] = raw
        ok = (_row_ids(i, tm, LANE) >= first_row) & (lax.broadcasted_iota(jnp.int32, (tm, LANE), 1) < H)
        beta = jnp.where(ok, _sigmoid(raw[:, :LANE]), 0.0)
        beta_ref[...] = beta
        a = raw[:, LANE:] + dtb_ref[...]
        sp = jnp.maximum(a, 0.0) + jnp.log(1.0 + jnp.exp(-jnp.abs(a)))
        gv = jnp.where(ok, -jnp.exp(alog_ref[...]) * sp, 0.0)
        gam = _dot(_chunk_tri(tm, lower=True), gv, HI)
        g_ref[...] = gam
        section(1)
        ii = lax.broadcasted_iota(jnp.int32, (CH, CH), 0)
        jj = lax.broadcasted_iota(jnp.int32, (CH, CH), 1)
        eye = (ii == jj).astype(f32)
        gam_t = gam.T

        def inverses(chunks):
            ms = []
            for c in chunks:
                rows = slice(c * CH, (c + 1) * CH)
                for hh in range(H):
                    kh = k_ref[rows, hh * DH:(hh + 1) * DH]
                    dec = jnp.exp(jnp.minimum(gam[rows, hh:hh + 1] - gam_t[hh:hh + 1, rows], 0.0))
                    kk = _dot_nt(_bf(kh * beta[rows, hh:hh + 1]), _bf(kh))
                    ms.append(jnp.where(ii > jj, kk * dec, 0.0))
            for n, t in enumerate(_tri_inv_many(ms, eye)):
                t_ref[chunks[n // H], n % H] = t

        nch = tm // CH
        inverses(list(range(nch // 2)))
        section(0)
        inverses(list(range(nch // 2, nch)))
        section(2)
        z_ref[...] = _dot(hb, w_vmem[:, 3 * W:4 * W])

    row = lambda i: (i, 0)
    fix = lambda i: (0, 0)
    out_shape = (S((L, D), f32), S((L, 3 * W), f32), S((L, W), f32), S((L, 2 * LANE), f32),
                 S((L, W), f32), S((L, W), f32), S((L, W), f32), S((L, LANE), f32), S((L, LANE), f32),
                 S((L // CH, H, CH, CH), f32))
    out_specs = (pl.BlockSpec((tm, D), row),
                 pl.BlockSpec((tm, 3 * W), row), pl.BlockSpec((tm, W), row), pl.BlockSpec((tm, 2 * LANE), row),
                 pl.BlockSpec((tm, W), row), pl.BlockSpec((tm, W), row), pl.BlockSpec((tm, W), row),
                 pl.BlockSpec((tm, LANE), row), pl.BlockSpec((tm, LANE), row),
                 pl.BlockSpec((tm // CH, H, CH, CH), lambda i: (i, 0, 0, 0)))
    return pl.pallas_call(
        body, name="gdn_in_fwd", grid=(L // tm,), out_shape=out_shape,
        in_specs=[pl.BlockSpec((tm, D), lambda i: (jnp.maximum(i - pb, 0), 0)),
                  pl.BlockSpec((tm, D), lambda i: (jnp.minimum(i, pb - 1), 0)), ANY, pl.BlockSpec((KW, 3 * W), fix),
                  pl.BlockSpec((1, LANE), fix), pl.BlockSpec((1, LANE), fix)],
        out_specs=out_specs,
        scratch_shapes=[pltpu.VMEM((D, NW), w_full.dtype), pltpu.VMEM((3, 8, W), f32), pltpu.SemaphoreType.DMA((1,))],
        compiler_params=_cp(dimension_semantics=("arbitrary",)))(x, head, w_full, conv_w, alog, dtb)


def _gdn_in_bwd(dq, dk, dv, dz, dg, dbeta, pre, raw, conv_w, alog, dtb, w_full, res, *, first_row, H):
    L = dq.shape[0]
    D = res.shape[1]
    W = H * DH
    KW = conv_w.shape[0]
    tm = TM
    nb = L // tm
    NW = 4 * W + 2 * LANE
    fb = PADF // tm
    alpha = ALPHA

    def body(dq_ref, dk_ref, dv_ref, dz_ref, dg_ref, dbeta_ref, pre_ref, hq_ref, hk_ref, hv_ref, raw_ref,
             cw_ref, alog_ref, dtb_ref, w_hbm, res_ref,
             dproj_ref, dcw_ref, dal_ref, ddt_ref, dx_ref, dfront_ref, w_vmem, carry, tmp, sem):
        i = pl.program_id(0)
        blk = nb - 1 - i
        _load_once([(w_hbm, w_vmem)], sem)

        @pl.when(i == 0)
        def _():
            carry[...] = jnp.zeros_like(carry)
            dcw_ref[...] = jnp.zeros_like(dcw_ref)
            dal_ref[...] = jnp.zeros_like(dal_ref)
            ddt_ref[...] = jnp.zeros_like(ddt_ref)

        halos = (hq_ref, hk_ref, hv_ref)
        douts = (dq_ref, dk_ref, dv_ref)
        for s in range(3):
            sec = slice(s * W, (s + 1) * W)
            pre = pre_ref[:, sec]
            c = _conv(cw_ref[:, sec], _taps_back(jnp.where(blk > 0, halos[s][...], 0.0), pre, KW))
            sig = _sigmoid(c)
            sl = c * sig
            if s < 2:
                scale = DH ** -0.5 if s == 0 else 1.0
                for hh in range(H):
                    hs = slice(hh * DH, (hh + 1) * DH)
                    seg = sl[:, hs]
                    r = lax.rsqrt(jnp.sum(seg * seg, axis=-1, keepdims=True) + L2_EPS)
                    n = seg * r
                    dqs = douts[s][:, hs]
                    tmp[:, hs] = (scale * r) * (dqs - n * jnp.sum(n * dqs, axis=-1, keepdims=True))
                dsl = tmp[...]
            else:
                dsl = dv_ref[...]
            dc = dsl * (sig * (1.0 + c * (1.0 - sig)))
            ahead = _taps_ahead(dc, carry[s], KW)
            carry[s] = dc[0:8, :]
            dproj_ref[:, sec] = _bf(_conv(cw_ref[:, sec], ahead))
            for j in range(KW):
                dcw_ref[j:j + 1, sec] += jnp.sum(ahead[j] * pre, axis=0, keepdims=True)
        dproj_ref[:, 3 * W:4 * W] = _bf(dz_ref[...])
        raw_v = raw_ref[...]
        ok = (_row_ids(blk, tm, LANE) >= first_row) & (lax.broadcasted_iota(jnp.int32, (tm, LANE), 1) < H)
        beta = _sigmoid(raw_v[:, :LANE])
        dbraw = jnp.where(ok, dbeta_ref[...] * beta * (1.0 - beta), 0.0)
        a = raw_v[:, LANE:] + dtb_ref[...]
        sp = jnp.maximum(a, 0.0) + jnp.log(1.0 + jnp.exp(-jnp.abs(a)))
        nea = -jnp.exp(alog_ref[...])
        dgm = jnp.where(ok, _dot(_chunk_tri(tm, lower=False), dg_ref[...], HI), 0.0)
        daraw = dgm * nea * _sigmoid(a)
        dal_ref[0:1, :] += jnp.sum(dgm * nea * sp, axis=0, keepdims=True)
        ddt_ref[0:1, :] += jnp.sum(daraw, axis=0, keepdims=True)
        dproj_ref[:, 4 * W:4 * W + LANE] = _bf(dbraw)
        dproj_ref[:, 4 * W + LANE:4 * W + 2 * LANE] = _bf(daraw)
        dh = alpha * res_ref[...] + _dot_nt(dproj_ref[...], w_vmem[...])

        @pl.when(blk >= fb)
        def _():
            dx_ref[...] = dh

        @pl.when(blk < fb)
        def _():
            dfront_ref[...] = dh

    rev = lambda i: (nb - 1 - i, 0)
    fix = lambda i: (0, 0)

    def halo(col):
        return pl.BlockSpec((8, W), lambda i: (jnp.maximum((nb - 1 - i) * (tm // 8) - 1, 0), col))

    return pl.pallas_call(
        body, name="gdn_in_bwd", grid=(nb,),
        out_shape=(S((L, NW), bf16), S((8, 3 * W), f32), S((8, LANE), f32), S((8, LANE), f32),
                   S((L - PADF, D), f32), S((PADF, D), f32)),
        in_specs=[pl.BlockSpec((tm, W), rev)] * 4 + [pl.BlockSpec((tm, LANE), rev)] * 2
        + [pl.BlockSpec((tm, 3 * W), rev), halo(0), halo(1), halo(2), pl.BlockSpec((tm, 2 * LANE), rev),
           pl.BlockSpec((KW, 3 * W), fix), pl.BlockSpec((1, LANE), fix), pl.BlockSpec((1, LANE), fix),
           ANY, pl.BlockSpec((tm, D), rev)],
        out_specs=(pl.BlockSpec((tm, NW), rev), pl.BlockSpec((8, 3 * W), fix),
                   pl.BlockSpec((8, LANE), fix), pl.BlockSpec((8, LANE), fix),
                   pl.BlockSpec((tm, D), lambda i: (jnp.maximum(nb - 1 - i - fb, 0), 0)),
                   pl.BlockSpec((tm, D), lambda i: (jnp.minimum(nb - 1 - i, fb - 1), 0))),
        scratch_shapes=[pltpu.VMEM((D, NW), w_full.dtype), pltpu.VMEM((3, 8, W), f32), pltpu.VMEM((tm, W), f32),
                        pltpu.SemaphoreType.DMA((1,))],
        compiler_params=_cp(dimension_semantics=("arbitrary",)))(
            dq, dk, dv, dz, dg, dbeta, pre, pre, pre, pre, raw, conv_w, alog, dtb, w_full, res)


def _chunk_tri(n, lower):
    i = lax.broadcasted_iota(jnp.int32, (n, n), 0)
    j = lax.broadcasted_iota(jnp.int32, (n, n), 1)
    sh = int(math.log2(CH))
    same = lax.shift_right_logical(i, sh) == lax.shift_right_logical(j, sh)
    return (same & ((i >= j) if lower else (j >= i))).astype(f32)


def _tri_inv_many(ms, eye):
    ts = [eye - m for m in ms]
    ps = list(ms)
    for level in range(int(math.log2(CH)) - 1):
        if level == 0:
            ps = [_dot(p, p, HI3) for p in ps]
            ts = [t + _dot(t, p, HI3) for t, p in zip(ts, ps)]
        else:
            ps = [_dot(_bf(p), _bf(p)) for p in ps]
            ts = [t + _dot(_bf(t), _bf(p)) for t, p in zip(ts, ps)]
    return ts


def _chunk_local(q, k, v, gcol, grow, glast, bcol, ii, jj):
    dec = jnp.where(ii >= jj, jnp.exp(jnp.minimum(gcol - grow, 0.0)), 0.0)
    eg = jnp.exp(gcol)
    kb = k * bcol
    kbg = kb * eg
    vb = v * bcol
    qt = q * eg
    kt = k * jnp.exp(glast - gcol)
    kbb, qb, kbf = _bf(kb), _bf(q), _bf(k)
    return dec, eg, kb, kbg, vb, qt, kt, _dot_nt(kbb, kbf), _dot_nt(qb, kbf), jnp.concatenate([kbb, qb], axis=0)


def _delta_fwd(q, k, v, g, beta, t_all, z, nw, h, w_out, ln_g, ln_b, *, first_row, H):
    L = q.shape[0]
    W = H * DH
    D = h.shape[1]
    rb = TM
    nc = rb // CH
    nblk = L // rb
    alpha = ALPHA

    def body(q_ref, k_ref, v_ref, g_ref, b_ref, t_ref, z_ref, nw_ref, h_ref, wout_hbm, lg_ref, lb_ref,
             o_ref, y_ref, s_out, pre_ref, out_ref, s_scr, wout, sem):
        _load_once([(wout_hbm, wout)], sem)

        @pl.when(pl.program_id(0) == 0)
        def _():
            s_scr[...] = jnp.zeros_like(s_scr)

        ii = lax.broadcasted_iota(jnp.int32, (CH, CH), 0)
        jj = lax.broadcasted_iota(jnp.int32, (CH, CH), 1)
        eye = (ii == jj).astype(f32)
        nwv = nw_ref[...]

        heads = range(H)
        hsl = [slice(hh * DH, (hh + 1) * DH) for hh in heads]

        def chunk(c, carry):
            r0 = pl.multiple_of(c * CH, CH)
            rows = pl.ds(r0, CH)
            gam = g_ref[rows, :]
            gam_t = gam.T
            bb = b_ref[rows, :]
            glast = [gam[CH - 1:CH, hh:hh + 1] for hh in heads]
            loc = [_chunk_local(q_ref[rows, hsl[hh]], k_ref[rows, hsl[hh]], v_ref[rows, hsl[hh]],
                                gam[:, hh:hh + 1], gam_t[hh:hh + 1, :], glast[hh], bb[:, hh:hh + 1], ii, jj)
                   for hh in heads]
            st = [s_scr[hh] for hh in heads]
            zs = [z_ref[rows, hsl[hh]] for hh in heads]
            ts = [t_ref[c, hh] for hh in heads]
            us = [_dot(t, l[4], HI3) for t, l in zip(ts, loc)]
            ws = [_dot(t, l[3], HI3) for t, l in zip(ts, loc)]
            stb = [_bf(s) for s in st]
            vn = [u - _dot(_bf(w), sb) for u, w, sb in zip(us, ws, stb)]
            vnb = [_bf(x) for x in vn]
            snew = [s * jnp.exp(gl) + _dot_tn(_bf(l[6]), xb) for s, gl, l, xb in zip(st, glast, loc, vnb)]
            os_ = [_dot(_bf(l[5]), sb) + _dot(_bf(l[8] * l[0]), xb) for l, sb, xb in zip(loc, stb, vnb)]
            for hh in heads:
                o = os_[hh]
                s_out[c, hh] = st[hh]
                s_scr[hh] = snew[hh]
                o_ref[rows, hsl[hh]] = o
                on = o * lax.rsqrt(jnp.mean(o * o, axis=-1, keepdims=True) + RMS_EPS) * nwv
                y_ref[rows, hsl[hh]] = _bf(on * (zs[hh] * _sigmoid(zs[hh])))
            return carry

        lax.fori_loop(0, nc, chunk, 0)
        pre = alpha * h_ref[...] + _dot(y_ref[...], wout[...])
        pre_ref[...] = pre
        out_ref[...] = _ln_fwd(pre, lg_ref[...], lb_ref[...], _row_ids(pl.program_id(0), rb, D), first_row)

    row = lambda i: (i, 0)
    fix = lambda i: (0, 0)
    return pl.pallas_call(
        body, name="delta_fwd", grid=(nblk,),
        out_shape=(S((L, W), f32), S((L, W), bf16), S((L // CH, H, DH, DH), f32), S((L, D), f32), S((L, D), f32)),
        in_specs=[pl.BlockSpec((rb, W), row)] * 3 + [pl.BlockSpec((rb, LANE), row)] * 2
        + [pl.BlockSpec((nc, H, CH, CH), lambda i: (i, 0, 0, 0)),
           pl.BlockSpec((rb, W), row), pl.BlockSpec((1, DH), fix), pl.BlockSpec((rb, D), row), ANY,
           pl.BlockSpec((1, D), fix), pl.BlockSpec((1, D), fix)],
        out_specs=(pl.BlockSpec((rb, W), row), pl.BlockSpec((rb, W), row),
                   pl.BlockSpec((nc, H, DH, DH), lambda i: (i, 0, 0, 0)),
                   pl.BlockSpec((rb, D), row), pl.BlockSpec((rb, D), row)),
        scratch_shapes=[pltpu.VMEM((H, DH, DH), f32), pltpu.VMEM((W, D), w_out.dtype), pltpu.SemaphoreType.DMA((1,))],
        compiler_params=_cp(dimension_semantics=("arbitrary",)))(q, k, v, g, beta, t_all, z, nw, h, w_out, ln_g, ln_b)


def _delta_bwd(dpre, w_out, o, z, nw, q, k, v, g, beta, s_all, t_all, *, H):
    L = q.shape[0]
    W = H * DH
    D = dpre.shape[1]
    rb = TM
    nc = rb // CH
    nblk = L // rb

    def body(dpre_ref, wout_hbm, o_ref, z_ref, nw_ref, q_ref, k_ref, v_ref, g_ref, b_ref, s_ref, t_ref,
             dq_ref, dk_ref, dv_ref, dz_ref, dg_ref, db_ref, dnw_ref, ds_scr, wout, dy_scr, sem):
        _load_once([(wout_hbm, wout)], sem)

        @pl.when(pl.program_id(0) == 0)
        def _():
            ds_scr[...] = jnp.zeros_like(ds_scr)
            dnw_ref[...] = jnp.zeros_like(dnw_ref)

        dy_scr[...] = _dot_nt(_bf(dpre_ref[...]), wout[...])

        ii = lax.broadcasted_iota(jnp.int32, (CH, CH), 0)
        jj = lax.broadcasted_iota(jnp.int32, (CH, CH), 1)
        lane = lax.broadcasted_iota(jnp.int32, (CH, LANE), 1)
        last_row = lax.broadcasted_iota(jnp.int32, (CH, 1), 0) == CH - 1
        nwv = nw_ref[...]

        def chunk(cc, carry):
            c = nc - 1 - cc
            r0 = pl.multiple_of(c * CH, CH)
            rows = pl.ds(r0, CH)
            gam = g_ref[rows, :]
            gam_t = gam.T
            bb = b_ref[rows, :]

            def head(hh):
                hs = slice(hh * DH, (hh + 1) * DH)
                gcol, grow, glast = gam[:, hh:hh + 1], gam_t[hh:hh + 1, :], gam[CH - 1:CH, hh:hh + 1]
                bcol = bb[:, hh:hh + 1]
                qh, kh, vh = q_ref[rows, hs], k_ref[rows, hs], v_ref[rows, hs]
                oh, zh, dyh = o_ref[rows, hs], z_ref[rows, hs], dy_scr[rows, hs]
                t = t_ref[c, hh]
                st = s_ref[c, hh]
                dsn = ds_scr[hh]
                rms = lax.rsqrt(jnp.mean(oh * oh, axis=-1, keepdims=True) + RMS_EPS)
                on = oh * rms
                sig = _sigmoid(zh)
                sz = zh * sig
                dz_ref[rows, hs] = dyh * on * nwv * (sig * (1.0 + zh * (1.0 - sig)))
                dnw = jnp.sum(dyh * on * sz, axis=0, keepdims=True)
                don = dyh * nwv * sz
                do = rms * (don - on * jnp.mean(don * on, axis=-1, keepdims=True))
                dec, eg, kb, kbg, vb, qt, kt, kk, qk, kqb = _chunk_local(qh, kh, vh, gcol, grow, glast, bcol, ii, jj)
                stb, dsnb, dob, tb, kbgb = _bf(st), _bf(dsn), _bf(do), _bf(t), _bf(kbg)
                r = vb - _dot(kbgb, stb)
                mm = jnp.where(ii > jj, kk * dec, 0.0)
                attn = qk * dec
                yield
                vn = _dot(t, r, HI3)
                dvn = _dot_tn(_bf(attn), dob) + _dot(_bf(kt), dsnb)
                egl = jnp.exp(glast)
                ekt = jnp.exp(glast - gcol)
                yield
                vnb, dvnb = _bf(vn), _bf(dvn)
                dattn = jnp.where(ii >= jj, _dot_nt(dob, vnb), 0.0)
                dkt = _dot_nt(vnb, dsnb)
                dvb = _dot_tn(tb, dvnb)
                dt = _dot_nt(dvnb, _bf(r))
                dglast = egl * jnp.sum(jnp.sum(dsn * st, axis=0, keepdims=True), axis=1, keepdims=True)
                yield
                dv_ref[rows, hs] = dvb * bcol
                dod = jnp.concatenate([dob, -_bf(dvb)], axis=0)
                ds_scr[hh] = egl * dsn + _dot_tn(jnp.concatenate([_bf(qt), kbgb], axis=0), dod)
                both = _dot_nt(dod, stb)
                dqt, dkbg = both[:CH], both[CH:]
                x = _dot_nt(_bf(dt), tb)
                yield
                dm = jnp.where(ii > jj, -_dot_tn(tb, _bf(x)), 0.0)
                dkk = dm * dec
                dqk = dattn * dec
                e = dm * mm + dattn * attn
                dgam = jnp.sum(e, axis=1, keepdims=True) - jnp.sum(e.T, axis=1, keepdims=True)
                dd = _bf(jnp.concatenate([dkk, dqk], axis=0))
                both = _dot(dd, _bf(kh))
                dkb = both[:CH] + dkbg * eg
                dk_ref[rows, hs] = _dot_tn(dd, kqb) + dkt * ekt + dkb * bcol
                dq_ref[rows, hs] = both[CH:] + dqt * eg
                yield
                dktkt = dkt * kt
                dgam = dgam + jnp.sum(dqt * qt - dktkt + dkbg * kbg, axis=1, keepdims=True)
                dglast = dglast + jnp.sum(jnp.sum(dktkt, axis=0, keepdims=True), axis=1, keepdims=True)
                dgam = dgam + jnp.where(last_row, dglast, 0.0)
                dbeta = jnp.sum(dkb * kh + dvb * vh, axis=1, keepdims=True)
                return dgam, dbeta, dnw

            res = [None] * H
            for h0 in range(0, H, BWD_HEAD_GROUP):
                group = range(h0, min(h0 + BWD_HEAD_GROUP, H))
                gens = {hh: head(hh) for hh in group}
                while any(res[hh] is None for hh in group):
                    for hh in group:
                        try:
                            next(gens[hh])
                        except StopIteration as stop:
                            res[hh] = stop.value
            dgam_all = jnp.zeros((CH, LANE), f32)
            dbeta_all = jnp.zeros((CH, LANE), f32)
            dnw_acc = jnp.zeros((1, DH), f32)
            for hh in range(H):
                dgam, dbeta, dnw = res[hh]
                dgam_all = dgam_all + jnp.where(lane == hh, dgam, 0.0)
                dbeta_all = dbeta_all + jnp.where(lane == hh, dbeta, 0.0)
                dnw_acc = dnw_acc + dnw
            dg_ref[rows, :] = dgam_all
            db_ref[rows, :] = dbeta_all
            dnw_ref[0:1, :] += dnw_acc
            return carry

        lax.fori_loop(0, nc, chunk, 0)

    rev = lambda i: (nblk - 1 - i, 0)
    rev4 = lambda i: (nblk - 1 - i, 0, 0, 0)
    fix = lambda i: (0, 0)
    wide = pl.BlockSpec((rb, W), rev)
    thin = pl.BlockSpec((rb, LANE), rev)
    return pl.pallas_call(
        body, name="delta_bwd", grid=(nblk,),
        out_shape=(S((L, W), f32),) * 4 + (S((L, LANE), f32),) * 2 + (S((8, DH), f32),),
        in_specs=[pl.BlockSpec((rb, D), rev), ANY, wide, wide, pl.BlockSpec((1, DH), fix), wide, wide, wide, thin, thin,
                  pl.BlockSpec((nc, H, DH, DH), rev4), pl.BlockSpec((nc, H, CH, CH), rev4)],
        out_specs=(wide,) * 4 + (thin, thin, pl.BlockSpec((8, DH), fix)),
        scratch_shapes=[pltpu.VMEM((H, DH, DH), f32), pltpu.VMEM((W, D), w_out.dtype), pltpu.VMEM((rb, W), f32),
                        pltpu.SemaphoreType.DMA((1,))],
        compiler_params=_cp(dimension_semantics=("arbitrary",)))(
            dpre, w_out, o, z, nw, q, k, v, g, beta, s_all, t_all)


def _sc_fwd(h, w_in, conv_w, w_out, g, b, *, first_row):
    L, D = h.shape
    W = w_out.shape[0]
    KW = conv_w.shape[0]
    tm = TM
    alpha = ALPHA

    def body(h_ref, win_hbm, cw_ref, wout_hbm, g_ref, b_ref, proj_ref, bu_ref, pre_ref, out_ref,
             win, wout, carry, sem):
        i = pl.program_id(0)
        _load_once([(win_hbm, win), (wout_hbm, wout)], sem)

        @pl.when(i == 0)
        def _():
            carry[...] = jnp.zeros_like(carry)

        hv = h_ref[...]
        hb = _bf(hv)
        bg = _dot(hb, win[:, 0:W])
        cg = _dot(hb, win[:, W:2 * W])
        xv = _dot(hb, win[:, 2 * W:3 * W])
        proj_ref[:, 0:W] = bg
        proj_ref[:, W:2 * W] = cg
        proj_ref[:, 2 * W:3 * W] = xv
        p = cg * xv
        u = _conv(cw_ref[...], _taps_back(carry[...], p, KW))
        carry[...] = p[tm - 8:tm, :]
        bu = _bf(bg * u)
        bu_ref[...] = bu
        pre = alpha * hv + _dot(bu, wout[...])
        pre_ref[...] = pre
        out_ref[...] = _ln_fwd(pre, g_ref[...], b_ref[...], _row_ids(i, tm, D), first_row)

    row = lambda i: (i, 0)
    fix = lambda i: (0, 0)
    return pl.pallas_call(
        body, name="sc_fwd", grid=(L // tm,),
        out_shape=(S((L, 3 * W), f32), S((L, W), bf16), S((L, D), f32), S((L, D), f32)),
        in_specs=[pl.BlockSpec((tm, D), row), ANY, pl.BlockSpec((KW, W), fix), ANY,
                  pl.BlockSpec((1, D), fix), pl.BlockSpec((1, D), fix)],
        out_specs=(pl.BlockSpec((tm, 3 * W), row), pl.BlockSpec((tm, W), row),
                   pl.BlockSpec((tm, D), row), pl.BlockSpec((tm, D), row)),
        scratch_shapes=[pltpu.VMEM((D, 3 * W), w_in.dtype), pltpu.VMEM((W, D), w_out.dtype),
                        pltpu.VMEM((8, W), f32), pltpu.SemaphoreType.DMA((2,))],
        compiler_params=_cp(dimension_semantics=("arbitrary",)))(h, w_in, conv_w, w_out, g, b)


def _sc_bwd(dpre, proj, conv_w, w_out, w_in, pre_in, g_in, *, first_row):
    L, D = dpre.shape
    W = w_out.shape[0]
    KW = conv_w.shape[0]
    tm = TM
    nb = L // tm
    alpha = ALPHA

    def body(dpre_ref, proj_ref, hc_ref, hx_ref, cw_ref, wout_hbm, win_hbm, pin_ref, g_ref,
             dproj_ref, dcw_ref, dpin_ref, dg_ref, db_ref, wout, win, carry, sem):
        i = pl.program_id(0)
        blk = nb - 1 - i
        _load_once([(wout_hbm, wout), (win_hbm, win)], sem)

        @pl.when(i == 0)
        def _():
            carry[...] = jnp.zeros_like(carry)
            dcw_ref[...] = jnp.zeros_like(dcw_ref)
            dg_ref[...] = jnp.zeros_like(dg_ref)
            db_ref[...] = jnp.zeros_like(db_ref)

        bg, cg, xv = proj_ref[:, 0:W], proj_ref[:, W:2 * W], proj_ref[:, 2 * W:3 * W]
        p = cg * xv
        u = _conv(cw_ref[...], _taps_back(jnp.where(blk > 0, hc_ref[...] * hx_ref[...], 0.0), p, KW))
        dpre_v = dpre_ref[...]
        d = _dot_nt(_bf(dpre_v), wout[...])
        dproj_ref[:, 0:W] = _bf(d * u)
        du = d * bg
        ahead = _taps_ahead(du, carry[...], KW)
        carry[...] = du[0:8, :]
        dp = _conv(cw_ref[...], ahead)
        for j in range(KW):
            dcw_ref[j:j + 1, :] += jnp.sum(ahead[j] * p, axis=0, keepdims=True)
        dproj_ref[:, W:2 * W] = _bf(dp * xv)
        dproj_ref[:, 2 * W:3 * W] = _bf(dp * cg)
        dh = alpha * dpre_v + _dot_nt(dproj_ref[...], win[...])
        dpin, dg, dbias = _ln_bwd_rows(dh, pin_ref[...], g_ref[...], _row_ids(blk, tm, D), first_row)
        dpin_ref[...] = dpin
        dg_ref[0:1, :] += dg
        db_ref[0:1, :] += dbias

    rev = lambda i: (nb - 1 - i, 0)
    fix = lambda i: (0, 0)

    def halo(col):
        return pl.BlockSpec((8, W), lambda i: (jnp.maximum((nb - 1 - i) * (tm // 8) - 1, 0), col))

    return pl.pallas_call(
        body, name="sc_bwd", grid=(nb,),
        out_shape=(S((L, 3 * W), bf16), S((8, W), f32), S((L, D), f32), S((8, D), f32), S((8, D), f32)),
        in_specs=[pl.BlockSpec((tm, D), rev), pl.BlockSpec((tm, 3 * W), rev), halo(1), halo(2),
                  pl.BlockSpec((KW, W), fix), ANY, ANY, pl.BlockSpec((tm, D), rev), pl.BlockSpec((1, D), fix)],
        out_specs=(pl.BlockSpec((tm, 3 * W), rev), pl.BlockSpec((8, W), fix), pl.BlockSpec((tm, D), rev),
                   pl.BlockSpec((8, D), fix), pl.BlockSpec((8, D), fix)),
        scratch_shapes=[pltpu.VMEM((W, D), w_out.dtype), pltpu.VMEM((D, 3 * W), w_in.dtype), pltpu.VMEM((8, W), f32),
                        pltpu.SemaphoreType.DMA((2,))],
        compiler_params=_cp(dimension_semantics=("arbitrary",)))(
            dpre, proj, proj, proj, conv_w, w_out, w_in, pre_in, g_in)


def _ffn_cols(F):
    fc = F
    for cand in (1408, 1024, 512, 256, 128):
        if F % cand == 0:
            fc = cand
            break
    return fc


def _ffn_fwd(h, w_up, conv_w, w_down, g, b, *, first_row, name):
    L, D = h.shape
    F = w_down.shape[0]
    KW = conv_w.shape[0]
    tm = TM
    fc = _ffn_cols(F)
    alpha = ALPHA

    def body(h_ref, wup_hbm, cw_ref, wdn_hbm, g_ref, b_ref, up_ref, a_ref, pre_ref, out_ref,
             wup, wdn, carry, sem):
        i = pl.program_id(0)
        _load_once([(wup_hbm, wup), (wdn_hbm, wdn)], sem)

        @pl.when(i == 0)
        def _():
            carry[...] = jnp.zeros_like(carry)

        hv = h_ref[...]
        hb = _bf(hv)
        pre = alpha * hv
        for c0 in range(0, F, fc):
            cs = slice(c0, c0 + fc)
            u = _dot(hb, wup[:, cs])
            gate = _dot(hb, wup[:, F + c0:F + c0 + fc])
            up_ref[:, cs] = u
            up_ref[:, F + c0:F + c0 + fc] = gate
            uc = _conv(cw_ref[:, cs], _taps_back(carry[:, cs], u, KW))
            carry[:, cs] = u[tm - 8:tm, :]
            ab = _bf(uc * _sigmoid(uc) * gate)
            a_ref[:, cs] = ab
            pre = pre + _dot(ab, wdn[cs, :])
        pre_ref[...] = pre
        out_ref[...] = _ln_fwd(pre, g_ref[...], b_ref[...], _row_ids(i, tm, D), first_row)

    row = lambda i: (i, 0)
    fix = lambda i: (0, 0)
    return pl.pallas_call(
        body, name=name, grid=(L // tm,),
        out_shape=(S((L, 2 * F), f32), S((L, F), bf16), S((L, D), f32), S((L, D), f32)),
        in_specs=[pl.BlockSpec((tm, D), row), ANY, pl.BlockSpec((KW, F), fix), ANY,
                  pl.BlockSpec((1, D), fix), pl.BlockSpec((1, D), fix)],
        out_specs=(pl.BlockSpec((tm, 2 * F), row), pl.BlockSpec((tm, F), row),
                   pl.BlockSpec((tm, D), row), pl.BlockSpec((tm, D), row)),
        scratch_shapes=[pltpu.VMEM((D, 2 * F), w_up.dtype), pltpu.VMEM((F, D), w_down.dtype),
                        pltpu.VMEM((8, F), f32), pltpu.SemaphoreType.DMA((2,))],
        compiler_params=_cp(dimension_semantics=("arbitrary",)))(h, w_up, conv_w, w_down, g, b)


def _ffn_bwd(dpre, up, w_down, conv_w, w_up, pre_in, g_in, *, first_row, name):
    L, D = dpre.shape
    F = w_down.shape[0]
    KW = conv_w.shape[0]
    tm = TM
    nb = L // tm
    fc = F
    alpha = ALPHA

    def body(dpre_ref, up_ref, halo_ref, wdn_hbm, cw_ref, wup_hbm, pin_ref, g_ref,
             dup_ref, dcw_ref, dpin_ref, dg_ref, db_ref, wdn, wup, carry, sem):
        i = pl.program_id(0)
        blk = nb - 1 - i
        _load_once([(wdn_hbm, wdn), (wup_hbm, wup)], sem)

        @pl.when(i == 0)
        def _():
            carry[...] = jnp.zeros_like(carry)
            dcw_ref[...] = jnp.zeros_like(dcw_ref)
            dg_ref[...] = jnp.zeros_like(dg_ref)
            db_ref[...] = jnp.zeros_like(db_ref)

        dpre_v = dpre_ref[...]
        db = _bf(dpre_v)
        dh = alpha * dpre_v
        for c0 in range(0, F, fc):
            cs = slice(c0, c0 + fc)
            gs_ = slice(F + c0, F + c0 + fc)
            da = _dot_nt(db, wdn[cs, :])
            gate = up_ref[:, gs_]
            u = up_ref[:, cs]
            uc = _conv(cw_ref[:, cs], _taps_back(jnp.where(blk > 0, halo_ref[:, cs], 0.0), u, KW))
            sig = _sigmoid(uc)
            dgate = _bf(da * (uc * sig))
            dup_ref[:, gs_] = dgate
            duc = da * gate * (sig * (1.0 + uc * (1.0 - sig)))
            ahead = _taps_ahead(duc, carry[:, cs], KW)
            carry[:, cs] = duc[0:8, :]
            du = _bf(_conv(cw_ref[:, cs], ahead))
            dup_ref[:, cs] = du
            for j in range(KW):
                dcw_ref[j:j + 1, cs] += jnp.sum(ahead[j] * u, axis=0, keepdims=True)
            dh = dh + _dot_nt(du, wup[:, cs]) + _dot_nt(dgate, wup[:, gs_])
        dpin, dg, dbias = _ln_bwd_rows(dh, pin_ref[...], g_ref[...], _row_ids(blk, tm, D), first_row)
        dpin_ref[...] = dpin
        dg_ref[0:1, :] += dg
        db_ref[0:1, :] += dbias

    rev = lambda i: (nb - 1 - i, 0)
    fix = lambda i: (0, 0)
    return pl.pallas_call(
        body, name=name, grid=(nb,),
        out_shape=(S((L, 2 * F), bf16), S((8, F), f32), S((L, D), f32), S((8, D), f32), S((8, D), f32)),
        in_specs=[pl.BlockSpec((tm, D), rev), pl.BlockSpec((tm, 2 * F), rev),
                  pl.BlockSpec((8, F), lambda i: (jnp.maximum((nb - 1 - i) * (tm // 8) - 1, 0), 0)),
                  ANY, pl.BlockSpec((KW, F), fix), ANY, pl.BlockSpec((tm, D), rev), pl.BlockSpec((1, D), fix)],
        out_specs=(pl.BlockSpec((tm, 2 * F), rev), pl.BlockSpec((8, F), fix), pl.BlockSpec((tm, D), rev),
                   pl.BlockSpec((8, D), fix), pl.BlockSpec((8, D), fix)),
        scratch_shapes=[pltpu.VMEM((F, D), w_down.dtype), pltpu.VMEM((D, 2 * F), w_up.dtype), pltpu.VMEM((8, F), f32),
                        pltpu.SemaphoreType.DMA((2,))],
        compiler_params=_cp(dimension_semantics=("arbitrary",)))(dpre, up, up, w_down, conv_w, w_up, pre_in, g_in)


def _loss_head(h, target, pre, g, *, first_row):
    L, D = h.shape
    tm = TM
    pb = PADF // tm

    def body(h_ref, t_ref, pre_ref, g_ref, dpre_ref, dg_ref, db_ref, loss_ref):
        i = pl.program_id(0)

        @pl.when(i == 0)
        def _():
            loss_ref[...] = jnp.zeros_like(loss_ref)
            dg_ref[...] = jnp.zeros_like(dg_ref)
            db_ref[...] = jnp.zeros_like(db_ref)

        valid = i >= pb
        err = h_ref[...] - t_ref[...]
        dh = jnp.where(valid, err * (1.0 / D), 0.0)
        part = 0.5 * jnp.sum(jnp.sum(err * err, axis=-1, keepdims=True) * (1.0 / D), axis=0, keepdims=True)
        loss_ref[...] += jnp.where(valid, part, 0.0)
        dpre, dg, db = _ln_bwd_rows(dh, pre_ref[...], g_ref[...], _row_ids(i, tm, D), first_row)
        dpre_ref[...] = dpre
        dg_ref[0:1, :] += dg
        db_ref[0:1, :] += db

    row = lambda i: (i, 0)
    fix = lambda i: (0, 0)
    return pl.pallas_call(
        body, name="loss_head", grid=(L // tm,),
        out_shape=(S((L, D), f32), S((8, D), f32), S((8, D), f32), S((8, LANE), f32)),
        in_specs=[pl.BlockSpec((tm, D), row), pl.BlockSpec((tm, D), lambda i: (jnp.maximum(i - pb, 0), 0)),
                  pl.BlockSpec((tm, D), row), pl.BlockSpec((1, D), fix)],
        out_specs=(pl.BlockSpec((tm, D), row), pl.BlockSpec((8, D), fix), pl.BlockSpec((8, D), fix),
                   pl.BlockSpec((8, LANE), fix)),
        compiler_params=_cp(dimension_semantics=("arbitrary",)))(h, target, pre, g)


def _adamw(g_terms, w, m, v, *, name):
    R, C = w.shape
    tr = _row_tile(R)
    n = len(g_terms)
    c1 = 1.0 - ADAM_B1 ** ADAM_STEP
    c2 = 1.0 - ADAM_B2 ** ADAM_STEP

    def body(*refs):
        g = refs[0][...].astype(f32)
        for r in refs[1:n]:
            g = g + r[...].astype(f32)
        w_ref, m_ref, v_ref, g_out, d_out, m_out, v_out = refs[n:]
        mn = ADAM_B1 * m_ref[...] + (1.0 - ADAM_B1) * g
        vn = ADAM_B2 * v_ref[...] + (1.0 - ADAM_B2) * (g * g)
        g_out[...] = g
        m_out[...] = mn
        v_out[...] = vn
        d_out[...] = -ADAM_LR * ((mn / c1) / (jnp.sqrt(vn / c2) + ADAM_EPS) + ADAM_WD * w_ref[...])

    spec = pl.BlockSpec((tr, C), lambda i: (i, 0))
    return pl.pallas_call(
        body, name=name, grid=(R // tr,), out_shape=(S((R, C), f32),) * 4,
        in_specs=[spec] * (n + 3), out_specs=(spec,) * 4,
        compiler_params=_cp(dimension_semantics=("arbitrary",)))(*g_terms, w, m, v)


def _sum_devices(x):
    n, R, C = x.shape

    def body(x_ref, o_ref):
        acc = x_ref[0]
        for d in range(1, n):
            acc = acc + x_ref[d]
        o_ref[...] = acc

    return pl.pallas_call(body, name="sum_devices", out_shape=S((R, C), f32), compiler_params=_cp())(x)


def _row_tile(R):
    for step in (16, 8):
        for t in range(256, 0, -step):
            if R % t == 0:
                return t
    return R


def _adamw_direct(s32s, recvs, w, m, v, me, *, name):
    L, K, n = w.shape
    tk = _row_tile(K)
    c1 = 1.0 - ADAM_B1 ** ADAM_STEP
    c2 = 1.0 - ADAM_B2 ** ADAM_STEP

    def body(me_ref, *refs):
        own_refs, recv_refs = refs[:L], refs[L:2 * L]
        w_ref, m_ref, v_ref, g_out, d_out, m_out, v_out = refs[2 * L:]
        for li in range(L):
            @pl.when(pl.program_id(0) == li)
            def _(li=li):
                g = own_refs[li][0, 0]
                for d in range(N_DEV):
                    g = g + recv_refs[li][d, 0].astype(f32)
                mn = ADAM_B1 * m_ref[0] + (1.0 - ADAM_B1) * g
                vn = ADAM_B2 * v_ref[0] + (1.0 - ADAM_B2) * (g * g)
                g_out[0] = g
                m_out[0] = mn
                v_out[0] = vn
                d_out[0] = -ADAM_LR * ((mn / c1) / (jnp.sqrt(vn / c2) + ADAM_EPS) + ADAM_WD * w_ref[0])

    own = pl.BlockSpec((1, tk, n), lambda l, i, ix: (l, i, 0))
    grid_spec = pltpu.PrefetchScalarGridSpec(
        num_scalar_prefetch=1, grid=(L, K // tk),
        in_specs=[pl.BlockSpec((1, 1, tk, n), lambda l, i, ix: (ix[0], 0, i, 0))] * L
        + [pl.BlockSpec((N_DEV, 1, tk, n), lambda l, i, ix: (0, 0, i, 0))] * L + [own, own, own],
        out_specs=(own,) * 4)
    return pl.pallas_call(
        body, name=name, grid_spec=grid_spec, out_shape=(S((L, K, n), f32),) * 4,
        compiler_params=_cp(dimension_semantics=("arbitrary", "arbitrary")))(me, *s32s, *recvs, w, m, v)


def _col_segments(n, mapping):
    segs = []
    for p in range(N_DEV):
        lo, hi = p * n, (p + 1) * n
        out = []
        for c0, c1, e0 in mapping:
            a, b = max(lo, c0), min(hi, c1)
            if a < b:
                out.append((a - lo, e0 + (a - c0), b - a))
        segs.append(out)
    return segs


def _assemble_cols(gathered, mapping, n_out, *, name):
    _, L, K, n = gathered.shape
    tk = _row_tile(K)
    segs = _col_segments(n, mapping)
    covered = sum(w for s in segs for (_, _, w) in s)

    def body(g_ref, o_ref):
        if covered != n_out:
            o_ref[...] = jnp.zeros_like(o_ref)
        for p in range(N_DEV):
            for s0, d0, w in segs[p]:
                o_ref[0, :, d0:d0 + w] = g_ref[p, 0, :, s0:s0 + w]

    return pl.pallas_call(
        body, name=name, grid=(L, K // tk), out_shape=S((L, K, n_out), gathered.dtype),
        in_specs=[pl.BlockSpec((N_DEV, 1, tk, n), lambda l, i: (0, l, i, 0))],
        out_specs=pl.BlockSpec((1, tk, n_out), lambda l, i: (l, i, 0)),
        compiler_params=_cp(dimension_semantics=("arbitrary", "arbitrary")))(gathered)


def _split_cols(dws, mapping, n, *, name):
    L = len(dws)
    K, n_in = dws[0].shape
    tk = _row_tile(K)
    segs = _col_segments(n, mapping)

    def body(*refs):
        ins, o32, o16 = refs[:L], refs[L], refs[L + 1]
        for li in range(L):
            @pl.when(pl.program_id(0) == li)
            def _(li=li):
                for p in range(N_DEV):
                    for s0, d0, w in segs[p]:
                        val = ins[li][:, d0:d0 + w]
                        o32[p, 0, :, s0:s0 + w] = val
                        o16[p, 0, :, s0:s0 + w] = _bf(val)

    out = pl.BlockSpec((N_DEV, 1, tk, n), lambda l, i: (0, l, i, 0))
    return pl.pallas_call(
        body, name=name, grid=(L, K // tk), out_shape=(S((N_DEV, L, K, n), f32), S((N_DEV, L, K, n), bf16)),
        in_specs=[pl.BlockSpec((tk, n_in), lambda l, i: (i, 0))] * L, out_specs=(out, out),
        compiler_params=_cp(dimension_semantics=("arbitrary", "arbitrary")))(*dws)


def _split_rows(dws, k, *, name):
    L = len(dws)
    N = dws[0].shape[1]

    def body(*refs):
        ins, o32, o16 = refs[:L], refs[L], refs[L + 1]
        for li in range(L):
            @pl.when(pl.program_id(0) == li)
            def _(li=li):
                val = ins[li][...]
                o32[0, 0] = val
                o16[0, 0] = _bf(val)

    out = pl.BlockSpec((1, 1, k, N), lambda l, p: (p, l, 0, 0))
    return pl.pallas_call(
        body, name=name, grid=(L, N_DEV), out_shape=(S((N_DEV, L, k, N), f32), S((N_DEV, L, k, N), bf16)),
        in_specs=[pl.BlockSpec((k, N), lambda l, p: (p, 0))] * L, out_specs=(out, out),
        compiler_params=_cp(dimension_semantics=("arbitrary", "arbitrary")))(*dws)


def _rows_full(gathered):
    _, L, k, N = gathered.shape
    return jnp.transpose(gathered, (1, 0, 2, 3)).reshape(L, N_DEV * k, N)


def _all_gather(xs, *, name):
    na = len(xs)

    def body(*refs):
        x_refs, out_refs = refs[:na], refs[na:2 * na]
        send_sems, recv_sems, local_sems = refs[2 * na:]
        mx, my, mc = lax.axis_index("x"), lax.axis_index("y"), lax.axis_index("c")
        me, sibling = (mx, my, mc), (mx, my, 1 - mc)
        chips = [(1 - mx, my), (mx, 1 - my), (1 - mx, 1 - my)]

        def slot(a, px, py, pc):
            return out_refs[a].at[4 * px + 2 * py + pc]

        def copy(a, kk, block, to, src=None):
            return pltpu.make_async_remote_copy(
                src_ref=slot(a, *block) if src is None else src, dst_ref=slot(a, *block),
                send_sem=send_sems.at[7 * a + kk], recv_sem=recv_sems.at[7 * a + kk], device_id=to, device_id_type=MESH)

        mine = [pltpu.make_async_copy(x_refs[a], slot(a, *me), local_sems.at[a]) for a in range(na)]
        for cp in mine:
            cp.start()
        first = []
        for a in range(na):
            first.append(copy(a, 0, me, sibling, src=x_refs[a]))
            first += [copy(a, 1 + j, me, (*chip, mc), src=x_refs[a]) for j, chip in enumerate(chips)]
        for cp in first:
            cp.start()
        passed = []
        for j, chip in enumerate(chips):
            for a in range(na):
                copy(a, 1 + j, (*chip, mc), me).wait_recv()
                fwd = copy(a, 4 + j, (*chip, mc), sibling)
                fwd.start()
                passed.append(fwd)
        for a in range(na):
            copy(a, 0, sibling, me).wait_recv()
            for j, chip in enumerate(chips):
                copy(a, 4 + j, (*chip, 1 - mc), me).wait_recv()
        for cp in first + passed:
            cp.wait_send()
        for cp in mine:
            cp.wait()

    return pl.pallas_call(
        body, name=name, out_shape=tuple(S((N_DEV,) + x.shape, x.dtype) for x in xs),
        in_specs=[ANY] * na, out_specs=(ANY,) * na,
        scratch_shapes=[pltpu.SemaphoreType.DMA((7 * na,)), pltpu.SemaphoreType.DMA((7 * na,)),
                        pltpu.SemaphoreType.DMA((na,))],
        compiler_params=pltpu.CompilerParams(has_side_effects=True))(*xs)


_FLIPS = [(fx, fy, fc) for fx in (0, 1) for fy in (0, 1) for fc in (0, 1)][1:]


def _flip_peer(flip):
    x, y, c = lax.axis_index("x"), lax.axis_index("y"), lax.axis_index("c")
    return tuple(1 - a if f else a for a, f in zip((x, y, c), flip))


def _dev_index(p):
    return 4 * p[0] + 2 * p[1] + p[2]


HBM_SPEC = pl.BlockSpec(memory_space=pltpu.HBM)
SEM_SPEC = pl.BlockSpec(memory_space=pltpu.SEMAPHORE)


def _direct_start(srcs, lands, per_peer, *, name):
    na = len(srcs)

    def body(*refs):
        src_refs, land_refs = refs[:na], refs[na:2 * na]
        send_sems, recv_sems = refs[2 * na], refs[2 * na + 1]
        token = refs[-1]
        me = _dev_index((lax.axis_index("x"), lax.axis_index("y"), lax.axis_index("c")))
        for a in range(na):
            for r, flip in enumerate(_FLIPS):
                peer = _flip_peer(flip)
                src = src_refs[a].at[_dev_index(peer)] if per_peer else src_refs[a]
                pltpu.make_async_remote_copy(
                    src_ref=src, dst_ref=land_refs[a].at[me], send_sem=send_sems.at[7 * a + r],
                    recv_sem=recv_sems.at[7 * a + r], device_id=peer, device_id_type=MESH).start()
        token[...] = jnp.zeros_like(token)

    hbm = lambda t: pltpu.with_memory_space_constraint(t, pltpu.HBM)
    out = pl.pallas_call(
        body, name=name,
        out_shape=(pltpu.SemaphoreType.DMA((7 * na,)), pltpu.SemaphoreType.DMA((7 * na,)))
        + tuple(pltpu.HBM(t.shape, t.dtype) for t in list(srcs) + list(lands)) + (S((8, LANE), f32),),
        in_specs=[HBM_SPEC] * (2 * na),
        out_specs=(SEM_SPEC, SEM_SPEC) + (HBM_SPEC,) * (2 * na) + (pl.BlockSpec(memory_space=pltpu.VMEM),),
        input_output_aliases={i: 2 + i for i in range(2 * na)},
        compiler_params=pltpu.CompilerParams(has_side_effects=pltpu.SideEffectType.DATAFLOW_SIDE_EFFECTING))(
            *[hbm(t) for t in srcs], *[hbm(t) for t in lands])
    return out[0], out[1], list(out[2:2 + na]), list(out[2 + na:2 + 2 * na]), out[-1]


def _direct_wait(send_sems, recv_sems, srcs, lands, per_peer, after, *, name):
    na = len(srcs)

    def body(*refs):
        src_refs, land_refs = refs[:na], refs[na:2 * na]
        ssem, rsem = refs[2 * na], refs[2 * na + 1]
        me = _dev_index((lax.axis_index("x"), lax.axis_index("y"), lax.axis_index("c")))
        for a in range(na):
            for r, flip in enumerate(_FLIPS):
                peer = _flip_peer(flip)
                src = src_refs[a].at[_dev_index(peer)] if per_peer else src_refs[a]
                cp = pltpu.make_async_remote_copy(
                    src_ref=src, dst_ref=land_refs[a].at[me], send_sem=ssem.at[7 * a + r],
                    recv_sem=rsem.at[7 * a + r], device_id=peer, device_id_type=MESH)
                cp.wait_send()
                cp.wait_recv()

    out = pl.pallas_call(
        body, name=name, out_shape=tuple(pltpu.HBM(t.shape, t.dtype) for t in list(srcs) + list(lands)),
        in_specs=[HBM_SPEC] * (2 * na) + [SEM_SPEC, SEM_SPEC, ANY], out_specs=(HBM_SPEC,) * (2 * na),
        input_output_aliases={i: i for i in range(2 * na)},
        compiler_params=pltpu.CompilerParams(has_side_effects=pltpu.SideEffectType.DATAFLOW_SIDE_EFFECTING))(
            *srcs, *lands, send_sems, recv_sems, after)
    return list(out[:na]), list(out[na:])


def _pack_small(parts, width):
    rows, offs, r = [], [], 0
    for a in parts:
        n = a.size
        nr = -(-n // width)
        flat = a.reshape(-1).astype(f32)
        if nr * width != n:
            flat = jnp.pad(flat, (0, nr * width - n))
        rows.append(flat.reshape(nr, width))
        offs.append((r, nr))
        r += nr
    buf = jnp.concatenate(rows, axis=0)
    pad = (-r) % 8
    if pad:
        buf = jnp.pad(buf, ((0, pad), (0, 0)))
    return buf, offs


def _unpack_small(buf, off, shape):
    r, nr = off
    return buf[r:r + nr].reshape(-1)[:math.prod(shape)].reshape(shape)


def _local_step(x, target, meta, a_w_in, a_w_out, small, start_token, late_weights, grads_ready):
    SEQ, D = x.shape
    n_meta = meta.shape[0]
    first_row = PADF - n_meta
    H = small["a_log"].shape[-1]

    head = jnp.concatenate([jnp.zeros((first_row, D), f32), meta], axis=0)

    def lanes(a):
        return jnp.pad(a.reshape(1, -1), ((0, 0), (0, LANE - a.size)))

    def after_token(a, token):
        return a if token is None else a + token[0:1, 0:1]

    alog, dtb = after_token(lanes(small["a_log"][0]), start_token), lanes(small["a_dt_bias"][0])
    a_conv, b_conv = small["a_conv"][0], small["b_conv"][0]
    nw = small["a_norm"][0].reshape(1, DH)
    lmg, lmb, lfg, lfb = small["ln_mix_g"], small["ln_mix_b"], small["ln_ffn_g"], small["ln_ffn_b"]

    h0, pre_a, z, raw, q, k, v, beta, g, t_all = _gdn_in_fwd(x, head, a_w_in, a_conv, alog, dtb,
                                                             first_row=first_row, H=H)
    o, y, s_all, pre1, h1 = _delta_fwd(q, k, v, g, beta, t_all, z, nw, h0, a_w_out, lmg[0:1], lmb[0:1],
                                       first_row=first_row, H=H)
    wts = late_weights(h1)
    up0, act0, pre2, h2 = _ffn_fwd(h1, wts["ffn_w_up"][0], small["ffn_conv"][0], wts["ffn_w_down"][0],
                                   lfg[0:1], lfb[0:1], first_row=first_row, name="ffn_fwd0")
    proj_b, bu, pre3, h3 = _sc_fwd(h2, wts["b_w_in"], b_conv, wts["b_w_out"], lmg[1:2], lmb[1:2], first_row=first_row)
    up1, act1, pre4, h4 = _ffn_fwd(h3, wts["ffn_w_up"][1], small["ffn_conv"][1], wts["ffn_w_down"][1],
                                   lfg[1:2], lfb[1:2], first_row=first_row, name="ffn_fwd1")
    gs = {}
    dpre4, dlfg1, dlfb1, loss_tile = _loss_head(h4, target, pre4, lfg[1:2], first_row=first_row)

    def ffn_backward(dpre, up, act, h_in, layer, tag, ln_in, token=None):
        dup, dcw, dpre_in, dg, db = _ffn_bwd(
            dpre, up, wts["ffn_w_down"][layer], after_token(small["ffn_conv"][layer], token),
            wts["ffn_w_up"][layer], ln_in[0], ln_in[1], first_row=first_row, name="ffn_bwd" + tag)
        dwd = _linear_dw(act, dpre, name="dw_down" + tag)
        dwu = _linear_dw(h_in, dup, name="dw_up" + tag)
        return dpre_in, dg, db, dwu, dwd, dcw[0:3]

    dpre3, dlmg1, dlmb1, dwu1, dwd1, dcf1 = ffn_backward(dpre4, up1, act1, h3, 1, "1", (pre3, lmg[1:2]))

    dproj_b, dcb, dpre2, dlfg0, dlfb0 = _sc_bwd(dpre3, proj_b, b_conv, wts["b_w_out"], wts["b_w_in"], pre2, lfg[0:1],
                                                first_row=first_row)
    dwb_in = _linear_dw(h2, dproj_b, name="dw_b_in")
    token = grads_ready("layer1", dict(ffn_w_up=dwu1, ffn_w_down=dwd1, b_w_in=dwb_in))

    dpre1, dlmg0, dlmb0, dwu0, dwd0, dcf0 = ffn_backward(dpre2, up0, act0, h1, 0, "0", (pre1, lmg[0:1]), token)
    token = grads_ready("layer0", dict(ffn_w_up=dwu0, ffn_w_down=dwd0))

    dq, dk, dv, dz, dg_, dbeta, dnw = _delta_bwd(dpre1, a_w_out, o, z, after_token(nw, token), q, k, v, g, beta,
                                                 s_all, t_all, H=H)
    dproj_a, dca, dal, ddt, grad_x, dhead = _gdn_in_bwd(dq, dk, dv, dz, dg_, dbeta, pre_a, raw, a_conv, alog, dtb,
                                                        a_w_in, dpre1, first_row=first_row, H=H)
    token = grads_ready("last", dict(a_w_in=_linear_dw(h0, dproj_a, name="dw_a_in")))
    grads_ready("tail", dict(a_w_out=_linear_dw(y, dpre1, name="dw_a_out", after=token),
                             b_w_out=_linear_dw(bu, dpre3, name="dw_b_out", after=token)))

    gs["meta"] = dhead[first_row:PADF]
    gs["a_conv"] = dca[0:a_conv.shape[0]][None]
    gs["a_log"] = dal[0:1, 0:H]
    gs["a_dt_bias"] = ddt[0:1, 0:H]
    gs["a_norm"] = dnw[0:1]
    gs["b_conv"] = dcb[0:b_conv.shape[0]][None]
    gs["ln_mix_g"] = jnp.stack([dlmg0[0], dlmg1[0]])
    gs["ln_mix_b"] = jnp.stack([dlmb0[0], dlmb1[0]])
    gs["ffn_conv"] = jnp.stack([dcf0, dcf1])
    gs["ln_ffn_g"] = jnp.stack([dlfg0[0], dlfg1[0]])
    gs["ln_ffn_b"] = jnp.stack([dlfb0[0], dlfb1[0]])
    return loss_tile, grad_x, gs


_BIG = ("a_w_in", "a_w_out", "b_w_in", "b_w_out", "ffn_w_up", "ffn_w_down")
_BIG_COL = ("a_w_in", "b_w_in", "ffn_w_up")
_SMALL = ("meta", "a_conv", "a_log", "a_dt_bias", "a_norm", "b_conv", "ln_mix_g", "ln_mix_b",
          "ffn_conv", "ln_ffn_g", "ln_ffn_b")
_SMALL_SHARDED = ("meta", "a_conv", "b_conv", "ffn_conv")
_ORDER = ("meta", "a_w_in", "a_conv", "a_log", "a_dt_bias", "a_norm", "a_w_out", "b_w_in", "b_conv", "b_w_out",
          "ln_mix_g", "ln_mix_b", "ffn_w_up", "ffn_conv", "ffn_w_down", "ln_ffn_g", "ln_ffn_b")


def _a_w_in_map(H):
    W4 = 4 * H * DH
    return [(0, W4, 0), (W4, W4 + H, W4), (W4 + H, W4 + 2 * H, W4 + LANE)], W4 + 2 * LANE


def kernel(x, meta, a_w_in, a_conv, a_log, a_dt_bias, a_norm, a_w_out, b_w_in, b_conv, b_w_out, ln_mix_g, ln_mix_b, ffn_w_up, ffn_conv, ffn_w_down, ln_ffn_g, ln_ffn_b, loss_target, m_meta, m_a_w_in, m_a_conv, m_a_log, m_a_dt_bias, m_a_norm, m_a_w_out, m_b_w_in, m_b_conv, m_b_w_out, m_ln_mix_g, m_ln_mix_b, m_ffn_w_up, m_ffn_conv, m_ffn_w_down, m_ln_ffn_g, m_ln_ffn_b, v_meta, v_a_w_in, v_a_conv, v_a_log, v_a_dt_bias, v_a_norm, v_a_w_out, v_b_w_in, v_b_conv, v_b_w_out, v_ln_mix_g, v_ln_mix_b, v_ffn_w_up, v_ffn_conv, v_ffn_w_down, v_ln_ffn_g, v_ln_ffn_b):
    wloc = dict(meta=meta, a_w_in=a_w_in, a_conv=a_conv, a_log=a_log, a_dt_bias=a_dt_bias, a_norm=a_norm,
                a_w_out=a_w_out, b_w_in=b_w_in, b_conv=b_conv, b_w_out=b_w_out, ln_mix_g=ln_mix_g, ln_mix_b=ln_mix_b,
                ffn_w_up=ffn_w_up, ffn_conv=ffn_conv, ffn_w_down=ffn_w_down, ln_ffn_g=ln_ffn_g, ln_ffn_b=ln_ffn_b)
    mloc = dict(meta=m_meta, a_w_in=m_a_w_in, a_conv=m_a_conv, a_log=m_a_log, a_dt_bias=m_a_dt_bias, a_norm=m_a_norm,
                a_w_out=m_a_w_out, b_w_in=m_b_w_in, b_conv=m_b_conv, b_w_out=m_b_w_out, ln_mix_g=m_ln_mix_g,
                ln_mix_b=m_ln_mix_b, ffn_w_up=m_ffn_w_up, ffn_conv=m_ffn_conv, ffn_w_down=m_ffn_w_down,
                ln_ffn_g=m_ln_ffn_g, ln_ffn_b=m_ln_ffn_b)
    vloc = dict(meta=v_meta, a_w_in=v_a_w_in, a_conv=v_a_conv, a_log=v_a_log, a_dt_bias=v_a_dt_bias, a_norm=v_a_norm,
                a_w_out=v_a_w_out, b_w_in=v_b_w_in, b_conv=v_b_conv, b_w_out=v_b_w_out, ln_mix_g=v_ln_mix_g,
                ln_mix_b=v_ln_mix_b, ffn_w_up=v_ffn_w_up, ffn_conv=v_ffn_conv, ffn_w_down=v_ffn_w_down,
                ln_ffn_g=v_ln_ffn_g, ln_ffn_b=v_ln_ffn_b)
    H = a_log.shape[-1]
    mx, my, mc = lax.axis_index("x"), lax.axis_index("y"), lax.axis_index("c")
    me = 4 * mx + 2 * my + mc

    a_map, a_cols = _a_w_in_map(H)
    col_maps = {"a_w_in": (a_map, a_cols)}
    for n in ("b_w_in", "ffn_w_up"):
        ncols = N_DEV * wloc[n].shape[-1]
        col_maps[n] = ([(0, ncols, 0)], ncols)
    sm_sh = [wloc[n] for n in _SMALL_SHARDED]
    sbuf, soffs = _pack_small(sm_sh, 128)
    g_a_w_in, g_a_w_out, sg = _all_gather([_bf(wloc["a_w_in"]), _bf(wloc["a_w_out"]), sbuf], name="gather_first")
    w_a_in = _assemble_cols(g_a_w_in, *col_maps["a_w_in"], name="assemble_a_w_in")[0]
    w_a_out = _rows_full(g_a_w_out)[0]
    late = [n for n in _BIG if n not in ("a_w_in", "a_w_out")]
    ssem, rsem, srcs_t, lands_t, start_token = _direct_start(
        [_bf(wloc[n]) for n in late], [lax.empty((N_DEV,) + wloc[n].shape, bf16) for n in late], False,
        name="gather_rest_start")

    def late_weights(after):
        srcs_d, landed = _direct_wait(ssem, rsem, srcs_t, lands_t, False, after, name="gather_rest_wait")
        wts = {}
        for n, own, got in zip(late, srcs_d, landed):
            full = lax.dynamic_update_index_in_dim(got, own, me, 0)
            if n in _BIG_COL:
                wts[n] = _assemble_cols(full, *col_maps[n], name="assemble_" + n)
            else:
                wts[n] = _rows_full(full)
        for n in ("b_w_in", "b_w_out"):
            wts[n] = wts[n][0]
        return wts

    small = {n: wloc[n] for n in _SMALL}
    for n, off in zip(_SMALL_SHARDED, soffs):
        sh = wloc[n].shape
        parts = jnp.stack([_unpack_small(sg[d], off, sh) for d in range(N_DEV)])
        nd = len(sh)
        small[n] = jnp.transpose(parts, tuple(range(1, nd)) + (0, nd)).reshape(sh[:-1] + (N_DEV * sh[-1],))

    def split(n, dws, tag):
        if n in _BIG_COL:
            return _split_cols(dws, col_maps[n][0], wloc[n].shape[-1], name="split_" + n + tag)
        return _split_rows(dws, wloc[n].shape[-2], name="split_" + n + tag)

    sent = {}

    def grads_ready(stage, grads):
        names = sorted(grads)
        parts = [split(n, [grads[n]], "_" + stage) for n in names]
        handles = _direct_start([p[1] for p in parts], [jnp.zeros(p[1].shape, bf16) for p in parts], True,
                                name="grads_" + stage + "_start")
        sent[stage] = (names, [p[0] for p in parts], handles)
        return handles[4]

    loss_tile, grad_x, gs = _local_step(x[0], loss_target[0], small["meta"], w_a_in, w_a_out, small, start_token,
                                        late_weights, grads_ready)

    def landed(stage, after):
        names, own32, (ssem_g, rsem_g, srcs_g, lands_g, _) = sent[stage]
        _, got = _direct_wait(ssem_g, rsem_g, srcs_g, lands_g, True, after, name="grads_" + stage + "_wait")
        return list(zip(names, own32, got))

    parts = {}
    for stage in ("layer0", "layer1"):
        for n, o32, r in landed(stage, grad_x):
            parts.setdefault(n, []).append((o32, r))
    me1 = jnp.stack([me]).astype(jnp.int32)
    big_out = {n: _adamw_direct([p[0] for p in ps], [p[1] for p in ps], wloc[n], mloc[n], vloc[n], me1,
                                name="adamw_" + n) for n, ps in parts.items()}
    names = list(_SMALL)
    pbuf, poffs = _pack_small([gs[n] for n in names] + [loss_tile[0:1, 0:1]], 1024)
    psum = _sum_devices(_all_gather([pbuf], name="gather_small_grads")[0])
    loss = psum[poffs[-1][0], 0]
    g_small = {}
    for n, off in zip(names, poffs[:-1]):
        full_shape = gs[n].shape
        gfull = _unpack_small(psum, off, full_shape)
        if n in _SMALL_SHARDED:
            ns = wloc[n].shape[-1]
            gfull = lax.dynamic_slice_in_dim(gfull, me * ns, ns, axis=gfull.ndim - 1)
        g_small[n] = gfull.reshape(wloc[n].shape)
    gbuf, aoffs = _pack_small([g_small[n] for n in names], 128)
    wbuf, _ = _pack_small([wloc[n] for n in names], 128)
    mbuf, _ = _pack_small([mloc[n] for n in names], 128)
    vbuf, _ = _pack_small([vloc[n] for n in names], 128)
    _, d_s, m_s, v_s = _adamw([gbuf], wbuf, mbuf, vbuf, name="adamw_small")

    done = d_s[0, 0]
    for out in big_out.values():
        done = done + out[1][0, 0, 0]
    for stage in ("last", "tail"):
        for n, o32, r in landed(stage, done.reshape(1, 1)):
            big_out[n] = _adamw_direct([o32], [r], wloc[n], mloc[n], vloc[n], me1, name="adamw_" + n)

    grads, deltas, new_m, new_v = {}, {}, {}, {}
    for n in _BIG:
        grads[n], deltas[n], new_m[n], new_v[n] = big_out[n]
    for n, off in zip(names, aoffs):
        sh = wloc[n].shape
        grads[n] = g_small[n]
        deltas[n], new_m[n], new_v[n] = (_unpack_small(b_, off, sh) for b_ in (d_s, m_s, v_s))
    return (loss, grad_x[None], *[grads[n] for n in _ORDER], *[deltas[n] for n in _ORDER],
            *[new_m[n] for n in _ORDER], *[new_v[n] for n in _ORDER])
```

```python
import math

import jax
import jax.numpy as jnp
from jax import lax
from jax.experimental import pallas as pl
from jax.experimental.pallas import tpu as pltpu

f32, bf16 = jnp.float32, jnp.bfloat16
S = jax.ShapeDtypeStruct
HI = lax.Precision.HIGHEST
HI3 = lax.Precision.HIGH
MESH = pl.DeviceIdType.MESH

V7X_VMEM_LIMIT = 56 * 1024 * 1024
LANE = 128
DH = 128
CH = 64
PADF = 256
TM = 256
TMM = 768
N_DEV = 8
BWD_HEAD_GROUP = 4

DEPTH = 2
ALPHA = (2.0 * DEPTH) ** 0.25
LN_EPS = 1e-5
RMS_EPS = 1e-6
L2_EPS = 1e-6
ADAM_LR, ADAM_B1, ADAM_B2, ADAM_EPS, ADAM_WD, ADAM_STEP = 0.001, 0.9, 0.999, 1e-08, 0.01, 10


def _cp(**kw):
    return pltpu.CompilerParams(vmem_limit_bytes=V7X_VMEM_LIMIT, **kw)


def _bf(x):
    return x.astype(bf16)


def _dot(a, b, precision=None):
    return jnp.dot(a, b, preferred_element_type=f32, precision=precision)


def _dot_nt(a, b):
    return lax.dot_general(a, b, (((1,), (1,)), ((), ())), preferred_element_type=f32)


def _dot_tn(a, b):
    return lax.dot_general(a, b, (((0,), (0,)), ((), ())), preferred_element_type=f32)


def _sigmoid(x):
    return 1.0 / (1.0 + jnp.exp(-x))


def _load_once(pairs, sem):
    @pl.when(pl.program_id(0) == 0)
    def _():
        cps = [pltpu.make_async_copy(src, dst, sem.at[n]) for n, (src, dst) in enumerate(pairs)]
        for c in cps:
            c.start()
        for c in cps:
            c.wait()


def _row_ids(i, tm, width):
    return i * tm + lax.broadcasted_iota(jnp.int32, (tm, width), 0)


def _ln_fwd(pre, g, b, rows, first_row):
    mu = jnp.mean(pre, axis=-1, keepdims=True)
    xc = pre - mu
    var = jnp.mean(xc * xc, axis=-1, keepdims=True)
    y = xc * lax.rsqrt(var + LN_EPS) * g + b
    return jnp.where(rows >= first_row, y, 0.0)


ANY = pl.BlockSpec(memory_space=pl.ANY)


def _taps_back(prev8, x, kw):
    xe = jnp.concatenate([prev8, x], axis=0)
    return [pltpu.roll(xe, kw - 1 - j, 0)[8:] for j in range(kw - 1)] + [x]


def _taps_ahead(x, next8, kw):
    n = x.shape[0]
    xe = jnp.concatenate([x, next8], axis=0)
    return [pltpu.roll(xe, n + 8 - (kw - 1 - j), 0)[:n] for j in range(kw - 1)] + [x]


def _conv(cw, taps):
    acc = cw[0:1, :] * taps[0]
    for j in range(1, len(taps)):
        acc = acc + cw[j:j + 1, :] * taps[j]
    return acc


def _linear_dw(x, dy, *, name, after=None):
    L, K = x.shape
    N = dy.shape[1]
    tm = TMM if L % TMM == 0 else TM
    tn = LANE
    for d in range(N // LANE, 0, -1):
        if (N // LANE) % d == 0 and K * d * LANE * 4 <= 9 * 1024 * 1024:
            tn = d * LANE
            break

    def body(x_ref, dy_ref, *rest):
        o_ref = rest[-1]

        @pl.when(pl.program_id(1) == 0)
        def _():
            o_ref[...] = jnp.zeros_like(o_ref)
        o_ref[...] += _dot_tn(_bf(x_ref[...]), _bf(dy_ref[...]))

    in_specs = [pl.BlockSpec((tm, K), lambda j, i: (i, 0)), pl.BlockSpec((tm, tn), lambda j, i: (i, j))]
    args = [x, dy]
    if after is not None:
        in_specs.append(pl.BlockSpec(after.shape, lambda j, i: (0, 0)))
        args.append(after)
    return pl.pallas_call(
        body, name=name, grid=(N // tn, L // tm), out_shape=S((K, N), f32),
        in_specs=in_specs, out_specs=pl.BlockSpec((K, tn), lambda j, i: (0, j)),
        compiler_params=_cp(dimension_semantics=("arbitrary", "arbitrary")))(*args)


def _ln_bwd_rows(dout, pre, g, rows, first_row):
    mu = jnp.mean(pre, axis=-1, keepdims=True)
    xc = pre - mu
    rstd = lax.rsqrt(jnp.mean(xc * xc, axis=-1, keepdims=True) + LN_EPS)
    xh = xc * rstd
    dy = jnp.where(rows >= first_row, dout, 0.0)
    dxh = dy * g
    dpre = rstd * (dxh - jnp.mean(dxh, axis=-1, keepdims=True) - xh * jnp.mean(dxh * xh, axis=-1, keepdims=True))
    return dpre, jnp.sum(dy * xh, axis=0, keepdims=True), jnp.sum(dy, axis=0, keepdims=True)


def _gdn_in_fwd(x, head, w_full, conv_w, alog, dtb, *, first_row, H):
    D = x.shape[1]
    L = PADF + x.shape[0]
    W = H * DH
    NW = w_full.shape[1]
    KW = conv_w.shape[0]
    tm = TM
    pb = PADF // tm

    def body(x_ref, head_ref, w_hbm, cw_ref, alog_ref, dtb_ref,
             h_ref, pre_ref, z_ref, raw_ref, q_ref, k_ref, v_ref, beta_ref, g_ref, t_ref,
             w_vmem, carry, sem):
        i = pl.program_id(0)
        _load_once([(w_hbm, w_vmem)], sem)

        @pl.when(i == 0)
        def _():
            carry[...] = jnp.zeros_like(carry)

        hv = jnp.where(i < pb, head_ref[...], x_ref[...])
        h_ref[...] = hv
        hb = _bf(hv)
        outs = (q_ref, k_ref, v_ref)

        def section(s):
            pre = _dot(hb, w_vmem[:, s * W:(s + 1) * W])
            pre_ref[:, s * W:(s + 1) * W] = pre
            c = _conv(cw_ref[:, s * W:(s + 1) * W], _taps_back(carry[s], pre, KW))
            carry[s] = pre[tm - 8:tm, :]
            sl = c * _sigmoid(c)
            if s < 2:
                scale = DH ** -0.5 if s == 0 else 1.0
                for hh in range(H):
                    seg = sl[:, hh * DH:(hh + 1) * DH]
                    r = lax.rsqrt(jnp.sum(seg * seg, axis=-1, keepdims=True) + L2_EPS)
                    outs[s][:, hh * DH:(hh + 1) * DH] = seg * (r * scale)
            else:
                v_ref[...] = sl

        raw = _dot(hb, w_vmem[:, 4 * W:4 * W + 2 * LANE])
        raw_ref[...] = raw
        ok = (_row_ids(i, tm, LANE) >= first_row) & (lax.broadcasted_iota(jnp.int32, (tm, LANE), 1) < H)
        beta = jnp.where(ok, _sigmoid(raw[:, :LANE]), 0.0)
        beta_ref[...] = beta
        a = raw[:, LANE:] + dtb_ref[...]
        sp = jnp.maximum(a, 0.0) + jnp.log(1.0 + jnp.exp(-jnp.abs(a)))
        gv = jnp.where(ok, -jnp.exp(alog_ref[...]) * sp, 0.0)
        gam = _dot(_chunk_tri(tm, lower=True), gv, HI)
        g_ref[...] = gam
        section(1)
        ii = lax.broadcasted_iota(jnp.int32, (CH, CH), 0)
        jj = lax.broadcasted_iota(jnp.int32, (CH, CH), 1)
        eye = (ii == jj).astype(f32)
        gam_t = gam.T

        def inverses(chunks):
            ms = []
            for c in chunks:
                rows = slice(c * CH, (c + 1) * CH)
                for hh in range(H):
                    kh = k_ref[rows, hh * DH:(hh + 1) * DH]
                    dec = jnp.exp(jnp.minimum(gam[rows, hh:hh + 1] - gam_t[hh:hh + 1, rows], 0.0))
                    kk = _dot_nt(_bf(kh * beta[rows, hh:hh + 1]), _bf(kh))
                    ms.append(jnp.where(ii > jj, kk * dec, 0.0))
            for n, t in enumerate(_tri_inv_many(ms, eye)):
                t_ref[chunks[n // H], n % H] = t

        nch = tm // CH
        inverses(list(range(nch // 2)))
        section(0)
        inverses(list(range(nch // 2, nch)))
        section(2)
        z_ref[...] = _dot(hb, w_vmem[:, 3 * W:4 * W])

    row = lambda i: (i, 0)
    fix = lambda i: (0, 0)
    out_shape = (S((L, D), f32), S((L, 3 * W), f32), S((L, W), f32), S((L, 2 * LANE), f32),
                 S((L, W), f32), S((L, W), f32), S((L, W), f32), S((L, LANE), f32), S((L, LANE), f32),
                 S((L // CH, H, CH, CH), f32))
    out_specs = (pl.BlockSpec((tm, D), row),
                 pl.BlockSpec((tm, 3 * W), row), pl.BlockSpec((tm, W), row), pl.BlockSpec((tm, 2 * LANE), row),
                 pl.BlockSpec((tm, W), row), pl.BlockSpec((tm, W), row), pl.BlockSpec((tm, W), row),
                 pl.BlockSpec((tm, LANE), row), pl.BlockSpec((tm, LANE), row),
                 pl.BlockSpec((tm // CH, H, CH, CH), lambda i: (i, 0, 0, 0)))
    return pl.pallas_call(
        body, name="gdn_in_fwd", grid=(L // tm,), out_shape=out_shape,
        in_specs=[pl.BlockSpec((tm, D), lambda i: (jnp.maximum(i - pb, 0), 0)),
                  pl.BlockSpec((tm, D), lambda i: (jnp.minimum(i, pb - 1), 0)), ANY, pl.BlockSpec((KW, 3 * W), fix),
                  pl.BlockSpec((1, LANE), fix), pl.BlockSpec((1, LANE), fix)],
        out_specs=out_specs,
        scratch_shapes=[pltpu.VMEM((D, NW), w_full.dtype), pltpu.VMEM((3, 8, W), f32), pltpu.SemaphoreType.DMA((1,))],
        compiler_params=_cp(dimension_semantics=("arbitrary",)))(x, head, w_full, conv_w, alog, dtb)


def _gdn_in_bwd(dq, dk, dv, dz, dg, dbeta, pre, raw, conv_w, alog, dtb, w_full, res, *, first_row, H):
    L = dq.shape[0]
    D = res.shape[1]
    W = H * DH
    KW = conv_w.shape[0]
    tm = TM
    nb = L // tm
    NW = 4 * W + 2 * LANE
    fb = PADF // tm
    alpha = ALPHA

    def body(dq_ref, dk_ref, dv_ref, dz_ref, dg_ref, dbeta_ref, pre_ref, hq_ref, hk_ref, hv_ref, raw_ref,
             cw_ref, alog_ref, dtb_ref, w_hbm, res_ref,
             dproj_ref, dcw_ref, dal_ref, ddt_ref, dx_ref, dfront_ref, w_vmem, carry, tmp, sem):
        i = pl.program_id(0)
        blk = nb - 1 - i
        _load_once([(w_hbm, w_vmem)], sem)

        @pl.when(i == 0)
        def _():
            carry[...] = jnp.zeros_like(carry)
            dcw_ref[...] = jnp.zeros_like(dcw_ref)
            dal_ref[...] = jnp.zeros_like(dal_ref)
            ddt_ref[...] = jnp.zeros_like(ddt_ref)

        halos = (hq_ref, hk_ref, hv_ref)
        douts = (dq_ref, dk_ref, dv_ref)
        for s in range(3):
            sec = slice(s * W, (s + 1) * W)
            pre = pre_ref[:, sec]
            c = _conv(cw_ref[:, sec], _taps_back(jnp.where(blk > 0, halos[s][...], 0.0), pre, KW))
            sig = _sigmoid(c)
            sl = c * sig
            if s < 2:
                scale = DH ** -0.5 if s == 0 else 1.0
                for hh in range(H):
                    hs = slice(hh * DH, (hh + 1) * DH)
                    seg = sl[:, hs]
                    r = lax.rsqrt(jnp.sum(seg * seg, axis=-1, keepdims=True) + L2_EPS)
                    n = seg * r
                    dqs = douts[s][:, hs]
                    tmp[:, hs] = (scale * r) * (dqs - n * jnp.sum(n * dqs, axis=-1, keepdims=True))
                dsl = tmp[...]
            else:
                dsl = dv_ref[...]
            dc = dsl * (sig * (1.0 + c * (1.0 - sig)))
            ahead = _taps_ahead(dc, carry[s], KW)
            carry[s] = dc[0:8, :]
            dproj_ref[:, sec] = _bf(_conv(cw_ref[:, sec], ahead))
            for j in range(KW):
                dcw_ref[j:j + 1, sec] += jnp.sum(ahead[j] * pre, axis=0, keepdims=True)
        dproj_ref[:, 3 * W:4 * W] = _bf(dz_ref[...])
        raw_v = raw_ref[...]
        ok = (_row_ids(blk, tm, LANE) >= first_row) & (lax.broadcasted_iota(jnp.int32, (tm, LANE), 1) < H)
        beta = _sigmoid(raw_v[:, :LANE])
        dbraw = jnp.where(ok, dbeta_ref[...] * beta * (1.0 - beta), 0.0)
        a = raw_v[:, LANE:] + dtb_ref[...]
        sp = jnp.maximum(a, 0.0) + jnp.log(1.0 + jnp.exp(-jnp.abs(a)))
        nea = -jnp.exp(alog_ref[...])
        dgm = jnp.where(ok, _dot(_chunk_tri(tm, lower=False), dg_ref[...], HI), 0.0)
        daraw = dgm * nea * _sigmoid(a)
        dal_ref[0:1, :] += jnp.sum(dgm * nea * sp, axis=0, keepdims=True)
        ddt_ref[0:1, :] += jnp.sum(daraw, axis=0, keepdims=True)
        dproj_ref[:, 4 * W:4 * W + LANE] = _bf(dbraw)
        dproj_ref[:, 4 * W + LANE:4 * W + 2 * LANE] = _bf(daraw)
        dh = alpha * res_ref[...] + _dot_nt(dproj_ref[...], w_vmem[...])

        @pl.when(blk >= fb)
        def _():
            dx_ref[...] = dh

        @pl.when(blk < fb)
        def _():
            dfront_ref[...] = dh

    rev = lambda i: (nb - 1 - i, 0)
    fix = lambda i: (0, 0)

    def halo(col):
        return pl.BlockSpec((8, W), lambda i: (jnp.maximum((nb - 1 - i) * (tm // 8) - 1, 0), col))

    return pl.pallas_call(
        body, name="gdn_in_bwd", grid=(nb,),
        out_shape=(S((L, NW), bf16), S((8, 3 * W), f32), S((8, LANE), f32), S((8, LANE), f32),
                   S((L - PADF, D), f32), S((PADF, D), f32)),
        in_specs=[pl.BlockSpec((tm, W), rev)] * 4 + [pl.BlockSpec((tm, LANE), rev)] * 2
        + [pl.BlockSpec((tm, 3 * W), rev), halo(0), halo(1), halo(2), pl.BlockSpec((tm, 2 * LANE), rev),
           pl.BlockSpec((KW, 3 * W), fix), pl.BlockSpec((1, LANE), fix), pl.BlockSpec((1, LANE), fix),
           ANY, pl.BlockSpec((tm, D), rev)],
        out_specs=(pl.BlockSpec((tm, NW), rev), pl.BlockSpec((8, 3 * W), fix),
                   pl.BlockSpec((8, LANE), fix), pl.BlockSpec((8, LANE), fix),
                   pl.BlockSpec((tm, D), lambda i: (jnp.maximum(nb - 1 - i - fb, 0), 0)),
                   pl.BlockSpec((tm, D), lambda i: (jnp.minimum(nb - 1 - i, fb - 1), 0))),
        scratch_shapes=[pltpu.VMEM((D, NW), w_full.dtype), pltpu.VMEM((3, 8, W), f32), pltpu.VMEM((tm, W), f32),
                        pltpu.SemaphoreType.DMA((1,))],
        compiler_params=_cp(dimension_semantics=("arbitrary",)))(
            dq, dk, dv, dz, dg, dbeta, pre, pre, pre, pre, raw, conv_w, alog, dtb, w_full, res)


def _chunk_tri(n, lower):
    i = lax.broadcasted_iota(jnp.int32, (n, n), 0)
    j = lax.broadcasted_iota(jnp.int32, (n, n), 1)
    sh = int(math.log2(CH))
    same = lax.shift_right_logical(i, sh) == lax.shift_right_logical(j, sh)
    return (same & ((i >= j) if lower else (j >= i))).astype(f32)


def _tri_inv_many(ms, eye):
    ts = [eye - m for m in ms]
    ps = list(ms)
    for level in range(int(math.log2(CH)) - 1):
        if level == 0:
            ps = [_dot(p, p, HI3) for p in ps]
            ts = [t + _dot(t, p, HI3) for t, p in zip(ts, ps)]
        else:
            ps = [_dot(_bf(p), _bf(p)) for p in ps]
            ts = [t + _dot(_bf(t), _bf(p)) for t, p in zip(ts, ps)]
    return ts


def _chunk_local(q, k, v, gcol, grow, glast, bcol, ii, jj):
    dec = jnp.where(ii >= jj, jnp.exp(jnp.minimum(gcol - grow, 0.0)), 0.0)
    eg = jnp.exp(gcol)
    kb = k * bcol
    kbg = kb * eg
    vb = v * bcol
    qt = q * eg
    kt = k * jnp.exp(glast - gcol)
    kbb, qb, kbf = _bf(kb), _bf(q), _bf(k)
    return dec, eg, kb, kbg, vb, qt, kt, _dot_nt(kbb, kbf), _dot_nt(qb, kbf), jnp.concatenate([kbb, qb], axis=0)


def _delta_fwd(q, k, v, g, beta, t_all, z, nw, h, w_out, ln_g, ln_b, *, first_row, H):
    L = q.shape[0]
    W = H * DH
    D = h.shape[1]
    rb = TM
    nc = rb // CH
    nblk = L // rb
    alpha = ALPHA

    def body(q_ref, k_ref, v_ref, g_ref, b_ref, t_ref, z_ref, nw_ref, h_ref, wout_hbm, lg_ref, lb_ref,
             o_ref, y_ref, s_out, pre_ref, out_ref, s_scr, wout, sem):
        _load_once([(wout_hbm, wout)], sem)

        @pl.when(pl.program_id(0) == 0)
        def _():
            s_scr[...] = jnp.zeros_like(s_scr)

        ii = lax.broadcasted_iota(jnp.int32, (CH, CH), 0)
        jj = lax.broadcasted_iota(jnp.int32, (CH, CH), 1)
        eye = (ii == jj).astype(f32)
        nwv = nw_ref[...]

        heads = range(H)
        hsl = [slice(hh * DH, (hh + 1) * DH) for hh in heads]

        def chunk(c, carry):
            r0 = pl.multiple_of(c * CH, CH)
            rows = pl.ds(r0, CH)
            gam = g_ref[rows, :]
            gam_t = gam.T
            bb = b_ref[rows, :]
            glast = [gam[CH - 1:CH, hh:hh + 1] for hh in heads]
            loc = [_chunk_local(q_ref[rows, hsl[hh]], k_ref[rows, hsl[hh]], v_ref[rows, hsl[hh]],
                                gam[:, hh:hh + 1], gam_t[hh:hh + 1, :], glast[hh], bb[:, hh:hh + 1], ii, jj)
                   for hh in heads]
            st = [s_scr[hh] for hh in heads]
            zs = [z_ref[rows, hsl[hh]] for hh in heads]
            ts = [t_ref[c, hh] for hh in heads]
            us = [_dot(t, l[4], HI3) for t, l in zip(ts, loc)]
            ws = [_dot(t, l[3], HI3) for t, l in zip(ts, loc)]
            stb = [_bf(s) for s in st]
            vn = [u - _dot(_bf(w), sb) for u, w, sb in zip(us, ws, stb)]
            vnb = [_bf(x) for x in vn]
            snew = [s * jnp.exp(gl) + _dot_tn(_bf(l[6]), xb) for s, gl, l, xb in zip(st, glast, loc, vnb)]
            os_ = [_dot(_bf(l[5]), sb) + _dot(_bf(l[8] * l[0]), xb) for l, sb, xb in zip(loc, stb, vnb)]
            for hh in heads:
                o = os_[hh]
                s_out[c, hh] = st[hh]
                s_scr[hh] = snew[hh]
                o_ref[rows, hsl[hh]] = o
                on = o * lax.rsqrt(jnp.mean(o * o, axis=-1, keepdims=True) + RMS_EPS) * nwv
                y_ref[rows, hsl[hh]] = _bf(on * (zs[hh] * _sigmoid(zs[hh])))
            return carry

        lax.fori_loop(0, nc, chunk, 0)
        pre = alpha * h_ref[...] + _dot(y_ref[...], wout[...])
        pre_ref[...] = pre
        out_ref[...] = _ln_fwd(pre, lg_ref[...], lb_ref[...], _row_ids(pl.program_id(0), rb, D), first_row)

    row = lambda i: (i, 0)
    fix = lambda i: (0, 0)
    return pl.pallas_call(
        body, name="delta_fwd", grid=(nblk,),
        out_shape=(S((L, W), f32), S((L, W), bf16), S((L // CH, H, DH, DH), f32), S((L, D), f32), S((L, D), f32)),
        in_specs=[pl.BlockSpec((rb, W), row)] * 3 + [pl.BlockSpec((rb, LANE), row)] * 2
        + [pl.BlockSpec((nc, H, CH, CH), lambda i: (i, 0, 0, 0)),
           pl.BlockSpec((rb, W), row), pl.BlockSpec((1, DH), fix), pl.BlockSpec((rb, D), row), ANY,
           pl.BlockSpec((1, D), fix), pl.BlockSpec((1, D), fix)],
        out_specs=(pl.BlockSpec((rb, W), row), pl.BlockSpec((rb, W), row),
                   pl.BlockSpec((nc, H, DH, DH), lambda i: (i, 0, 0, 0)),
                   pl.BlockSpec((rb, D), row), pl.BlockSpec((rb, D), row)),
        scratch_shapes=[pltpu.VMEM((H, DH, DH), f32), pltpu.VMEM((W, D), w_out.dtype), pltpu.SemaphoreType.DMA((1,))],
        compiler_params=_cp(dimension_semantics=("arbitrary",)))(q, k, v, g, beta, t_all, z, nw, h, w_out, ln_g, ln_b)


def _delta_bwd(dpre, w_out, o, z, nw, q, k, v, g, beta, s_all, t_all, *, H):
    L = q.shape[0]
    W = H * DH
    D = dpre.shape[1]
    rb = TM
    nc = rb // CH
    nblk = L // rb

    def body(dpre_ref, wout_hbm, o_ref, z_ref, nw_ref, q_ref, k_ref, v_ref, g_ref, b_ref, s_ref, t_ref,
             dq_ref, dk_ref, dv_ref, dz_ref, dg_ref, db_ref, dnw_ref, ds_scr, wout, dy_scr, sem):
        _load_once([(wout_hbm, wout)], sem)

        @pl.when(pl.program_id(0) == 0)
        def _():
            ds_scr[...] = jnp.zeros_like(ds_scr)
            dnw_ref[...] = jnp.zeros_like(dnw_ref)

        dy_scr[...] = _dot_nt(_bf(dpre_ref[...]), wout[...])

        ii = lax.broadcasted_iota(jnp.int32, (CH, CH), 0)
        jj = lax.broadcasted_iota(jnp.int32, (CH, CH), 1)
        lane = lax.broadcasted_iota(jnp.int32, (CH, LANE), 1)
        last_row = lax.broadcasted_iota(jnp.int32, (CH, 1), 0) == CH - 1
        nwv = nw_ref[...]

        def chunk(cc, carry):
            c = nc - 1 - cc
            r0 = pl.multiple_of(c * CH, CH)
            rows = pl.ds(r0, CH)
            gam = g_ref[rows, :]
            gam_t = gam.T
            bb = b_ref[rows, :]

            def head(hh):
                hs = slice(hh * DH, (hh + 1) * DH)
                gcol, grow, glast = gam[:, hh:hh + 1], gam_t[hh:hh + 1, :], gam[CH - 1:CH, hh:hh + 1]
                bcol = bb[:, hh:hh + 1]
                qh, kh, vh = q_ref[rows, hs], k_ref[rows, hs], v_ref[rows, hs]
                oh, zh, dyh = o_ref[rows, hs], z_ref[rows, hs], dy_scr[rows, hs]
                t = t_ref[c, hh]
                st = s_ref[c, hh]
                dsn = ds_scr[hh]
                rms = lax.rsqrt(jnp.mean(oh * oh, axis=-1, keepdims=True) + RMS_EPS)
                on = oh * rms
                sig = _sigmoid(zh)
                sz = zh * sig
                dz_ref[rows, hs] = dyh * on * nwv * (sig * (1.0 + zh * (1.0 - sig)))
                dnw = jnp.sum(dyh * on * sz, axis=0, keepdims=True)
                don = dyh * nwv * sz
                do = rms * (don - on * jnp.mean(don * on, axis=-1, keepdims=True))
                dec, eg, kb, kbg, vb, qt, kt, kk, qk, kqb = _chunk_local(qh, kh, vh, gcol, grow, glast, bcol, ii, jj)
                stb, dsnb, dob, tb, kbgb = _bf(st), _bf(dsn), _bf(do), _bf(t), _bf(kbg)
                r = vb - _dot(kbgb, stb)
                mm = jnp.where(ii > jj, kk * dec, 0.0)
                attn = qk * dec
                yield
                vn = _dot(t, r, HI3)
                dvn = _dot_tn(_bf(attn), dob) + _dot(_bf(kt), dsnb)
                egl = jnp.exp(glast)
                ekt = jnp.exp(glast - gcol)
                yield
                vnb, dvnb = _bf(vn), _bf(dvn)
                dattn = jnp.where(ii >= jj, _dot_nt(dob, vnb), 0.0)
                dkt = _dot_nt(vnb, dsnb)
                dvb = _dot_tn(tb, dvnb)
                dt = _dot_nt(dvnb, _bf(r))
                dglast = egl * jnp.sum(jnp.sum(dsn * st, axis=0, keepdims=True), axis=1, keepdims=True)
                yield
                dv_ref[rows, hs] = dvb * bcol
                dod = jnp.concatenate([dob, -_bf(dvb)], axis=0)
                ds_scr[hh] = egl * dsn + _dot_tn(jnp.concatenate([_bf(qt), kbgb], axis=0), dod)
                both = _dot_nt(dod, stb)
                dqt, dkbg = both[:CH], both[CH:]
                x = _dot_nt(_bf(dt), tb)
                yield
                dm = jnp.where(ii > jj, -_dot_tn(tb, _bf(x)), 0.0)
                dkk = dm * dec
                dqk = dattn * dec
                e = dm * mm + dattn * attn
                dgam = jnp.sum(e, axis=1, keepdims=True) - jnp.sum(e.T, axis=1, keepdims=True)
                dd = _bf(jnp.concatenate([dkk, dqk], axis=0))
                both = _dot(dd, _bf(kh))
                dkb = both[:CH] + dkbg * eg
                dk_ref[rows, hs] = _dot_tn(dd, kqb) + dkt * ekt + dkb * bcol
                dq_ref[rows, hs] = both[CH:] + dqt * eg
                yield
                dktkt = dkt * kt
                dgam = dgam + jnp.sum(dqt * qt - dktkt + dkbg * kbg, axis=1, keepdims=True)
                dglast = dglast + jnp.sum(jnp.sum(dktkt, axis=0, keepdims=True), axis=1, keepdims=True)
                dgam = dgam + jnp.where(last_row, dglast, 0.0)
                dbeta = jnp.sum(dkb * kh + dvb * vh, axis=1, keepdims=True)
                return dgam, dbeta, dnw

            res = [None] * H
            for h0 in range(0, H, BWD_HEAD_GROUP):
                group = range(h0, min(h0 + BWD_HEAD_GROUP, H))
                gens = {hh: head(hh) for hh in group}
                while any(res[hh] is None for hh in group):
                    for hh in group:
                        try:
                            next(gens[hh])
                        except StopIteration as stop:
                            res[hh] = stop.value
            dgam_all = jnp.zeros((CH, LANE), f32)
            dbeta_all = jnp.zeros((CH, LANE), f32)
            dnw_acc = jnp.zeros((1, DH), f32)
            for hh in range(H):
                dgam, dbeta, dnw = res[hh]
                dgam_all = dgam_all + jnp.where(lane == hh, dgam, 0.0)
                dbeta_all = dbeta_all + jnp.where(lane == hh, dbeta, 0.0)
                dnw_acc = dnw_acc + dnw
            dg_ref[rows, :] = dgam_all
            db_ref[rows, :] = dbeta_all
            dnw_ref[0:1, :] += dnw_acc
            return carry

        lax.fori_loop(0, nc, chunk, 0)

    rev = lambda i: (nblk - 1 - i, 0)
    rev4 = lambda i: (nblk - 1 - i, 0, 0, 0)
    fix = lambda i: (0, 0)
    wide = pl.BlockSpec((rb, W), rev)
    thin = pl.BlockSpec((rb, LANE), rev)
    return pl.pallas_call(
        body, name="delta_bwd", grid=(nblk,),
        out_shape=(S((L, W), f32),) * 4 + (S((L, LANE), f32),) * 2 + (S((8, DH), f32),),
        in_specs=[pl.BlockSpec((rb, D), rev), ANY, wide, wide, pl.BlockSpec((1, DH), fix), wide, wide, wide, thin, thin,
                  pl.BlockSpec((nc, H, DH, DH), rev4), pl.BlockSpec((nc, H, CH, CH), rev4)],
        out_specs=(wide,) * 4 + (thin, thin, pl.BlockSpec((8, DH), fix)),
        scratch_shapes=[pltpu.VMEM((H, DH, DH), f32), pltpu.VMEM((W, D), w_out.dtype), pltpu.VMEM((rb, W), f32),
                        pltpu.SemaphoreType.DMA((1,))],
        compiler_params=_cp(dimension_semantics=("arbitrary",)))(
            dpre, w_out, o, z, nw, q, k, v, g, beta, s_all, t_all)


def _sc_fwd(h, w_in, conv_w, w_out, g, b, *, first_row):
    L, D = h.shape
    W = w_out.shape[0]
    KW = conv_w.shape[0]
    tm = TM
    alpha = ALPHA

    def body(h_ref, win_hbm, cw_ref, wout_hbm, g_ref, b_ref, proj_ref, bu_ref, pre_ref, out_ref,
             win, wout, carry, sem):
        i = pl.program_id(0)
        _load_once([(win_hbm, win), (wout_hbm, wout)], sem)

        @pl.when(i == 0)
        def _():
            carry[...] = jnp.zeros_like(carry)

        hv = h_ref[...]
        hb = _bf(hv)
        bg = _dot(hb, win[:, 0:W])
        cg = _dot(hb, win[:, W:2 * W])
        xv = _dot(hb, win[:, 2 * W:3 * W])
        proj_ref[:, 0:W] = bg
        proj_ref[:, W:2 * W] = cg
        proj_ref[:, 2 * W:3 * W] = xv
        p = cg * xv
        u = _conv(cw_ref[...], _taps_back(carry[...], p, KW))
        carry[...] = p[tm - 8:tm, :]
        bu = _bf(bg * u)
        bu_ref[...] = bu
        pre = alpha * hv + _dot(bu, wout[...])
        pre_ref[...] = pre
        out_ref[...] = _ln_fwd(pre, g_ref[...], b_ref[...], _row_ids(i, tm, D), first_row)

    row = lambda i: (i, 0)
    fix = lambda i: (0, 0)
    return pl.pallas_call(
        body, name="sc_fwd", grid=(L // tm,),
        out_shape=(S((L, 3 * W), f32), S((L, W), bf16), S((L, D), f32), S((L, D), f32)),
        in_specs=[pl.BlockSpec((tm, D), row), ANY, pl.BlockSpec((KW, W), fix), ANY,
                  pl.BlockSpec((1, D), fix), pl.BlockSpec((1, D), fix)],
        out_specs=(pl.BlockSpec((tm, 3 * W), row), pl.BlockSpec((tm, W), row),
                   pl.BlockSpec((tm, D), row), pl.BlockSpec((tm, D), row)),
        scratch_shapes=[pltpu.VMEM((D, 3 * W), w_in.dtype), pltpu.VMEM((W, D), w_out.dtype),
                        pltpu.VMEM((8, W), f32), pltpu.SemaphoreType.DMA((2,))],
        compiler_params=_cp(dimension_semantics=("arbitrary",)))(h, w_in, conv_w, w_out, g, b)


def _sc_bwd(dpre, proj, conv_w, w_out, w_in, pre_in, g_in, *, first_row):
    L, D = dpre.shape
    W = w_out.shape[0]
    KW = conv_w.shape[0]
    tm = TM
    nb = L // tm
    alpha = ALPHA

    def body(dpre_ref, proj_ref, hc_ref, hx_ref, cw_ref, wout_hbm, win_hbm, pin_ref, g_ref,
             dproj_ref, dcw_ref, dpin_ref, dg_ref, db_ref, wout, win, carry, sem):
        i = pl.program_id(0)
        blk = nb - 1 - i
        _load_once([(wout_hbm, wout), (win_hbm, win)], sem)

        @pl.when(i == 0)
        def _():
            carry[...] = jnp.zeros_like(carry)
            dcw_ref[...] = jnp.zeros_like(dcw_ref)
            dg_ref[...] = jnp.zeros_like(dg_ref)
            db_ref[...] = jnp.zeros_like(db_ref)

        bg, cg, xv = proj_ref[:, 0:W], proj_ref[:, W:2 * W], proj_ref[:, 2 * W:3 * W]
        p = cg * xv
        u = _conv(cw_ref[...], _taps_back(jnp.where(blk > 0, hc_ref[...] * hx_ref[...], 0.0), p, KW))
        dpre_v = dpre_ref[...]
        d = _dot_nt(_bf(dpre_v), wout[...])
        dproj_ref[:, 0:W] = _bf(d * u)
        du = d * bg
        ahead = _taps_ahead(du, carry[...], KW)
        carry[...] = du[0:8, :]
        dp = _conv(cw_ref[...], ahead)
        for j in range(KW):
            dcw_ref[j:j + 1, :] += jnp.sum(ahead[j] * p, axis=0, keepdims=True)
        dproj_ref[:, W:2 * W] = _bf(dp * xv)
        dproj_ref[:, 2 * W:3 * W] = _bf(dp * cg)
        dh = alpha * dpre_v + _dot_nt(dproj_ref[...], win[...])
        dpin, dg, dbias = _ln_bwd_rows(dh, pin_ref[...], g_ref[...], _row_ids(blk, tm, D), first_row)
        dpin_ref[...] = dpin
        dg_ref[0:1, :] += dg
        db_ref[0:1, :] += dbias

    rev = lambda i: (nb - 1 - i, 0)
    fix = lambda i: (0, 0)

    def halo(col):
        return pl.BlockSpec((8, W), lambda i: (jnp.maximum((nb - 1 - i) * (tm // 8) - 1, 0), col))

    return pl.pallas_call(
        body, name="sc_bwd", grid=(nb,),
        out_shape=(S((L, 3 * W), bf16), S((8, W), f32), S((L, D), f32), S((8, D), f32), S((8, D), f32)),
        in_specs=[pl.BlockSpec((tm, D), rev), pl.BlockSpec((tm, 3 * W), rev), halo(1), halo(2),
                  pl.BlockSpec((KW, W), fix), ANY, ANY, pl.BlockSpec((tm, D), rev), pl.BlockSpec((1, D), fix)],
        out_specs=(pl.BlockSpec((tm, 3 * W), rev), pl.BlockSpec((8, W), fix), pl.BlockSpec((tm, D), rev),
                   pl.BlockSpec((8, D), fix), pl.BlockSpec((8, D), fix)),
        scratch_shapes=[pltpu.VMEM((W, D), w_out.dtype), pltpu.VMEM((D, 3 * W), w_in.dtype), pltpu.VMEM((8, W), f32),
                        pltpu.SemaphoreType.DMA((2,))],
        compiler_params=_cp(dimension_semantics=("arbitrary",)))(
            dpre, proj, proj, proj, conv_w, w_out, w_in, pre_in, g_in)


def _ffn_cols(F):
    fc = F
    for cand in (1408, 1024, 512, 256, 128):
        if F % cand == 0:
            fc = cand
            break
    return fc


def _ffn_weight_copies(wup_hbm, wdn_hbm, wup, wdn, layer):
    k = wdn_hbm.shape[2]
    return [(wup_hbm.at[layer], wup)] + [(wdn_hbm.at[p, layer], wdn.at[pl.ds(p * k, k), :]) for p in range(N_DEV)]


def _ffn_fwd(h, w_up, conv_w, w_down, g, b, *, layer, first_row, name):
    L, D = h.shape
    F = N_DEV * w_down.shape[2]
    KW = conv_w.shape[0]
    tm = TM
    fc = _ffn_cols(F)
    alpha = ALPHA

    def body(h_ref, wup_hbm, cw_ref, wdn_hbm, g_ref, b_ref, up_ref, a_ref, pre_ref, out_ref,
             wup, wdn, carry, sem):
        i = pl.program_id(0)
        _load_once(_ffn_weight_copies(wup_hbm, wdn_hbm, wup, wdn, layer), sem)

        @pl.when(i == 0)
        def _():
            carry[...] = jnp.zeros_like(carry)

        hv = h_ref[...]
        hb = _bf(hv)
        pre = alpha * hv
        for c0 in range(0, F, fc):
            cs = slice(c0, c0 + fc)
            u = _dot(hb, wup[:, cs])
            gate = _dot(hb, wup[:, F + c0:F + c0 + fc])
            up_ref[:, cs] = u
            up_ref[:, F + c0:F + c0 + fc] = gate
            uc = _conv(cw_ref[:, cs], _taps_back(carry[:, cs], u, KW))
            carry[:, cs] = u[tm - 8:tm, :]
            ab = _bf(uc * _sigmoid(uc) * gate)
            a_ref[:, cs] = ab
            pre = pre + _dot(ab, wdn[cs, :])
        pre_ref[...] = pre
        out_ref[...] = _ln_fwd(pre, g_ref[...], b_ref[...], _row_ids(i, tm, D), first_row)

    row = lambda i: (i, 0)
    fix = lambda i: (0, 0)
    return pl.pallas_call(
        body, name=name, grid=(L // tm,),
        out_shape=(S((L, 2 * F), f32), S((L, F), bf16), S((L, D), f32), S((L, D), f32)),
        in_specs=[pl.BlockSpec((tm, D), row), ANY, pl.BlockSpec((KW, F), fix), ANY,
                  pl.BlockSpec((1, D), fix), pl.BlockSpec((1, D), fix)],
        out_specs=(pl.BlockSpec((tm, 2 * F), row), pl.BlockSpec((tm, F), row),
                   pl.BlockSpec((tm, D), row), pl.BlockSpec((tm, D), row)),
        scratch_shapes=[pltpu.VMEM((D, 2 * F), w_up.dtype), pltpu.VMEM((F, D), w_down.dtype),
                        pltpu.VMEM((8, F), f32), pltpu.SemaphoreType.DMA((1 + N_DEV,))],
        compiler_params=_cp(dimension_semantics=("arbitrary",)))(h, w_up, conv_w, w_down, g, b)


def _ffn_bwd(dpre, up, w_down, conv_w, w_up, pre_in, g_in, *, layer, first_row, name):
    L, D = dpre.shape
    F = N_DEV * w_down.shape[2]
    KW = conv_w.shape[0]
    tm = TM
    nb = L // tm
    fc = F
    alpha = ALPHA

    def body(dpre_ref, up_ref, halo_ref, wdn_hbm, cw_ref, wup_hbm, pin_ref, g_ref,
             dup_ref, dcw_ref, dpin_ref, dg_ref, db_ref, wdn, wup, carry, sem):
        i = pl.program_id(0)
        blk = nb - 1 - i
        _load_once(_ffn_weight_copies(wup_hbm, wdn_hbm, wup, wdn, layer), sem)

        @pl.when(i == 0)
        def _():
            carry[...] = jnp.zeros_like(carry)
            dcw_ref[...] = jnp.zeros_like(dcw_ref)
            dg_ref[...] = jnp.zeros_like(dg_ref)
            db_ref[...] = jnp.zeros_like(db_ref)

        dpre_v = dpre_ref[...]
        db = _bf(dpre_v)
        dh = alpha * dpre_v
        for c0 in range(0, F, fc):
            cs = slice(c0, c0 + fc)
            gs_ = slice(F + c0, F + c0 + fc)
            da = _dot_nt(db, wdn[cs, :])
            gate = up_ref[:, gs_]
            u = up_ref[:, cs]
            uc = _conv(cw_ref[:, cs], _taps_back(jnp.where(blk > 0, halo_ref[:, cs], 0.0), u, KW))
            sig = _sigmoid(uc)
            dgate = _bf(da * (uc * sig))
            dup_ref[:, gs_] = dgate
            duc = da * gate * (sig * (1.0 + uc * (1.0 - sig)))
            ahead = _taps_ahead(duc, carry[:, cs], KW)
            carry[:, cs] = duc[0:8, :]
            du = _bf(_conv(cw_ref[:, cs], ahead))
            dup_ref[:, cs] = du
            for j in range(KW):
                dcw_ref[j:j + 1, cs] += jnp.sum(ahead[j] * u, axis=0, keepdims=True)
            dh = dh + _dot_nt(du, wup[:, cs]) + _dot_nt(dgate, wup[:, gs_])
        dpin, dg, dbias = _ln_bwd_rows(dh, pin_ref[...], g_ref[...], _row_ids(blk, tm, D), first_row)
        dpin_ref[...] = dpin
        dg_ref[0:1, :] += dg
        db_ref[0:1, :] += dbias

    rev = lambda i: (nb - 1 - i, 0)
    fix = lambda i: (0, 0)
    return pl.pallas_call(
        body, name=name, grid=(nb,),
        out_shape=(S((L, 2 * F), bf16), S((8, F), f32), S((L, D), f32), S((8, D), f32), S((8, D), f32)),
        in_specs=[pl.BlockSpec((tm, D), rev), pl.BlockSpec((tm, 2 * F), rev),
                  pl.BlockSpec((8, F), lambda i: (jnp.maximum((nb - 1 - i) * (tm // 8) - 1, 0), 0)),
                  ANY, pl.BlockSpec((KW, F), fix), ANY, pl.BlockSpec((tm, D), rev), pl.BlockSpec((1, D), fix)],
        out_specs=(pl.BlockSpec((tm, 2 * F), rev), pl.BlockSpec((8, F), fix), pl.BlockSpec((tm, D), rev),
                   pl.BlockSpec((8, D), fix), pl.BlockSpec((8, D), fix)),
        scratch_shapes=[pltpu.VMEM((F, D), w_down.dtype), pltpu.VMEM((D, 2 * F), w_up.dtype), pltpu.VMEM((8, F), f32),
                        pltpu.SemaphoreType.DMA((1 + N_DEV,))],
        compiler_params=_cp(dimension_semantics=("arbitrary",)))(dpre, up, up, w_down, conv_w, w_up, pre_in, g_in)


def _loss_head(h, target, pre, g, *, first_row):
    L, D = h.shape
    tm = TM
    pb = PADF // tm

    def body(h_ref, t_ref, pre_ref, g_ref, dpre_ref, dg_ref, db_ref, loss_ref):
        i = pl.program_id(0)

        @pl.when(i == 0)
        def _():
            loss_ref[...] = jnp.zeros_like(loss_ref)
            dg_ref[...] = jnp.zeros_like(dg_ref)
            db_ref[...] = jnp.zeros_like(db_ref)

        valid = i >= pb
        err = h_ref[...] - t_ref[...]
        dh = jnp.where(valid, err * (1.0 / D), 0.0)
        part = 0.5 * jnp.sum(jnp.sum(err * err, axis=-1, keepdims=True) * (1.0 / D), axis=0, keepdims=True)
        loss_ref[...] += jnp.where(valid, part, 0.0)
        dpre, dg, db = _ln_bwd_rows(dh, pre_ref[...], g_ref[...], _row_ids(i, tm, D), first_row)
        dpre_ref[...] = dpre
        dg_ref[0:1, :] += dg
        db_ref[0:1, :] += db

    row = lambda i: (i, 0)
    fix = lambda i: (0, 0)
    return pl.pallas_call(
        body, name="loss_head", grid=(L // tm,),
        out_shape=(S((L, D), f32), S((8, D), f32), S((8, D), f32), S((8, LANE), f32)),
        in_specs=[pl.BlockSpec((tm, D), row), pl.BlockSpec((tm, D), lambda i: (jnp.maximum(i - pb, 0), 0)),
                  pl.BlockSpec((tm, D), row), pl.BlockSpec((1, D), fix)],
        out_specs=(pl.BlockSpec((tm, D), row), pl.BlockSpec((8, D), fix), pl.BlockSpec((8, D), fix),
                   pl.BlockSpec((8, LANE), fix)),
        compiler_params=_cp(dimension_semantics=("arbitrary",)))(h, target, pre, g)


def _adamw(g_terms, w, m, v, *, name):
    R, C = w.shape
    tr = _row_tile(R)
    n = len(g_terms)
    c1 = 1.0 - ADAM_B1 ** ADAM_STEP
    c2 = 1.0 - ADAM_B2 ** ADAM_STEP

    def body(*refs):
        g = refs[0][...].astype(f32)
        for r in refs[1:n]:
            g = g + r[...].astype(f32)
        w_ref, m_ref, v_ref, g_out, d_out, m_out, v_out = refs[n:]
        mn = ADAM_B1 * m_ref[...] + (1.0 - ADAM_B1) * g
        vn = ADAM_B2 * v_ref[...] + (1.0 - ADAM_B2) * (g * g)
        g_out[...] = g
        m_out[...] = mn
        v_out[...] = vn
        d_out[...] = -ADAM_LR * ((mn / c1) / (jnp.sqrt(vn / c2) + ADAM_EPS) + ADAM_WD * w_ref[...])

    spec = pl.BlockSpec((tr, C), lambda i: (i, 0))
    return pl.pallas_call(
        body, name=name, grid=(R // tr,), out_shape=(S((R, C), f32),) * 4,
        in_specs=[spec] * (n + 3), out_specs=(spec,) * 4,
        compiler_params=_cp(dimension_semantics=("arbitrary",)))(*g_terms, w, m, v)


def _sum_devices(x):
    n, R, C = x.shape

    def body(x_ref, o_ref):
        acc = x_ref[0]
        for d in range(1, n):
            acc = acc + x_ref[d]
        o_ref[...] = acc

    return pl.pallas_call(body, name="sum_devices", out_shape=S((R, C), f32), compiler_params=_cp())(x)


def _row_tile(R):
    for step in (16, 8):
        for t in range(256, 0, -step):
            if R % t == 0:
                return t
    return R


def _adamw_direct(s32s, recvs, w, m, v, me, *, name):
    L, K, n = w.shape
    tk = _row_tile(K)
    c1 = 1.0 - ADAM_B1 ** ADAM_STEP
    c2 = 1.0 - ADAM_B2 ** ADAM_STEP

    def body(me_ref, *refs):
        own_refs, recv_refs = refs[:L], refs[L:2 * L]
        w_ref, m_ref, v_ref, g_out, d_out, m_out, v_out = refs[2 * L:]
        for li in range(L):
            @pl.when(pl.program_id(0) == li)
            def _(li=li):
                g = own_refs[li][0, 0]
                for d in range(N_DEV):
                    g = g + recv_refs[li][d, 0].astype(f32)
                mn = ADAM_B1 * m_ref[0] + (1.0 - ADAM_B1) * g
                vn = ADAM_B2 * v_ref[0] + (1.0 - ADAM_B2) * (g * g)
                g_out[0] = g
                m_out[0] = mn
                v_out[0] = vn
                d_out[0] = -ADAM_LR * ((mn / c1) / (jnp.sqrt(vn / c2) + ADAM_EPS) + ADAM_WD * w_ref[0])

    own = pl.BlockSpec((1, tk, n), lambda l, i, ix: (l, i, 0))
    grid_spec = pltpu.PrefetchScalarGridSpec(
        num_scalar_prefetch=1, grid=(L, K // tk),
        in_specs=[pl.BlockSpec((1, 1, tk, n), lambda l, i, ix: (ix[0], 0, i, 0))] * L
        + [pl.BlockSpec((N_DEV, 1, tk, n), lambda l, i, ix: (0, 0, i, 0))] * L + [own, own, own],
        out_specs=(own,) * 4)
    return pl.pallas_call(
        body, name=name, grid_spec=grid_spec, out_shape=(S((L, K, n), f32),) * 4,
        compiler_params=_cp(dimension_semantics=("arbitrary", "arbitrary")))(me, *s32s, *recvs, w, m, v)


def _col_segments(n, mapping):
    segs = []
    for p in range(N_DEV):
        lo, hi = p * n, (p + 1) * n
        out = []
        for c0, c1, e0 in mapping:
            a, b = max(lo, c0), min(hi, c1)
            if a < b:
                out.append((a - lo, e0 + (a - c0), b - a))
        segs.append(out)
    return segs


def _assemble_cols(gathered, mapping, n_out, *, name):
    _, L, K, n = gathered.shape
    tk = _row_tile(K)
    segs = _col_segments(n, mapping)
    covered = sum(w for s in segs for (_, _, w) in s)

    def body(g_ref, o_ref):
        if covered != n_out:
            o_ref[...] = jnp.zeros_like(o_ref)
        for p in range(N_DEV):
            for s0, d0, w in segs[p]:
                o_ref[0, :, d0:d0 + w] = g_ref[p, 0, :, s0:s0 + w]

    return pl.pallas_call(
        body, name=name, grid=(L, K // tk), out_shape=S((L, K, n_out), gathered.dtype),
        in_specs=[pl.BlockSpec((N_DEV, 1, tk, n), lambda l, i: (0, l, i, 0))],
        out_specs=pl.BlockSpec((1, tk, n_out), lambda l, i: (l, i, 0)),
        compiler_params=_cp(dimension_semantics=("arbitrary", "arbitrary")))(gathered)


def _split_cols(dws, mapping, n, *, name):
    L = len(dws)
    K, n_in = dws[0].shape
    tk = _row_tile(K)
    segs = _col_segments(n, mapping)

    def body(*refs):
        ins, o32, o16 = refs[:L], refs[L], refs[L + 1]
        for li in range(L):
            @pl.when(pl.program_id(0) == li)
            def _(li=li):
                for p in range(N_DEV):
                    for s0, d0, w in segs[p]:
                        val = ins[li][:, d0:d0 + w]
                        o32[p, 0, :, s0:s0 + w] = val
                        o16[p, 0, :, s0:s0 + w] = _bf(val)

    out = pl.BlockSpec((N_DEV, 1, tk, n), lambda l, i: (0, l, i, 0))
    return pl.pallas_call(
        body, name=name, grid=(L, K // tk), out_shape=(S((N_DEV, L, K, n), f32), S((N_DEV, L, K, n), bf16)),
        in_specs=[pl.BlockSpec((tk, n_in), lambda l, i: (i, 0))] * L, out_specs=(out, out),
        compiler_params=_cp(dimension_semantics=("arbitrary", "arbitrary")))(*dws)


def _split_rows(dws, k, *, name):
    L = len(dws)
    N = dws[0].shape[1]

    def body(*refs):
        ins, o32, o16 = refs[:L], refs[L], refs[L + 1]
        for li in range(L):
            @pl.when(pl.program_id(0) == li)
            def _(li=li):
                val = ins[li][...]
                o32[0, 0] = val
                o16[0, 0] = _bf(val)

    out = pl.BlockSpec((1, 1, k, N), lambda l, p: (p, l, 0, 0))
    return pl.pallas_call(
        body, name=name, grid=(L, N_DEV), out_shape=(S((N_DEV, L, k, N), f32), S((N_DEV, L, k, N), bf16)),
        in_specs=[pl.BlockSpec((k, N), lambda l, p: (p, 0))] * L, out_specs=(out, out),
        compiler_params=_cp(dimension_semantics=("arbitrary", "arbitrary")))(*dws)


def _rows_full(gathered):
    _, L, k, N = gathered.shape
    return jnp.transpose(gathered, (1, 0, 2, 3)).reshape(L, N_DEV * k, N)


def _all_gather(xs, *, name):
    na = len(xs)

    def body(*refs):
        x_refs, out_refs = refs[:na], refs[na:2 * na]
        send_sems, recv_sems, local_sems = refs[2 * na:]
        mx, my, mc = lax.axis_index("x"), lax.axis_index("y"), lax.axis_index("c")
        me, sibling = (mx, my, mc), (mx, my, 1 - mc)
        chips = [(1 - mx, my), (mx, 1 - my), (1 - mx, 1 - my)]

        def slot(a, px, py, pc):
            return out_refs[a].at[4 * px + 2 * py + pc]

        def copy(a, kk, block, to, src=None):
            return pltpu.make_async_remote_copy(
                src_ref=slot(a, *block) if src is None else src, dst_ref=slot(a, *block),
                send_sem=send_sems.at[7 * a + kk], recv_sem=recv_sems.at[7 * a + kk], device_id=to, device_id_type=MESH)

        mine = [pltpu.make_async_copy(x_refs[a], slot(a, *me), local_sems.at[a]) for a in range(na)]
        for cp in mine:
            cp.start()
        first = []
        for a in range(na):
            first.append(copy(a, 0, me, sibling, src=x_refs[a]))
            first += [copy(a, 1 + j, me, (*chip, mc), src=x_refs[a]) for j, chip in enumerate(chips)]
        for cp in first:
            cp.start()
        passed = []
        for j, chip in enumerate(chips):
            for a in range(na):
                copy(a, 1 + j, (*chip, mc), me).wait_recv()
                fwd = copy(a, 4 + j, (*chip, mc), sibling)
                fwd.start()
                passed.append(fwd)
        for a in range(na):
            copy(a, 0, sibling, me).wait_recv()
            for j, chip in enumerate(chips):
                copy(a, 4 + j, (*chip, 1 - mc), me).wait_recv()
        for cp in first + passed:
            cp.wait_send()
        for cp in mine:
            cp.wait()

    return pl.pallas_call(
        body, name=name, out_shape=tuple(S((N_DEV,) + x.shape, x.dtype) for x in xs),
        in_specs=[ANY] * na, out_specs=(ANY,) * na,
        scratch_shapes=[pltpu.SemaphoreType.DMA((7 * na,)), pltpu.SemaphoreType.DMA((7 * na,)),
                        pltpu.SemaphoreType.DMA((na,))],
        compiler_params=pltpu.CompilerParams(has_side_effects=True))(*xs)


_FLIPS = [(fx, fy, fc) for fx in (0, 1) for fy in (0, 1) for fc in (0, 1)][1:]


def _flip_peer(flip):
    x, y, c = lax.axis_index("x"), lax.axis_index("y"), lax.axis_index("c")
    return tuple(1 - a if f else a for a, f in zip((x, y, c), flip))


def _dev_index(p):
    return 4 * p[0] + 2 * p[1] + p[2]


HBM_SPEC = pl.BlockSpec(memory_space=pltpu.HBM)
SEM_SPEC = pl.BlockSpec(memory_space=pltpu.SEMAPHORE)


def _direct_start(srcs, lands, per_peer, *, name):
    na = len(srcs)

    def body(*refs):
        src_refs, land_refs = refs[:na], refs[na:2 * na]
        send_sems, recv_sems = refs[2 * na], refs[2 * na + 1]
        token = refs[-1]
        me = _dev_index((lax.axis_index("x"), lax.axis_index("y"), lax.axis_index("c")))
        for a in range(na):
            for r, flip in enumerate(_FLIPS):
                peer = _flip_peer(flip)
                src = src_refs[a].at[_dev_index(peer)] if per_peer else src_refs[a]
                pltpu.make_async_remote_copy(
                    src_ref=src, dst_ref=land_refs[a].at[me], send_sem=send_sems.at[7 * a + r],
                    recv_sem=recv_sems.at[7 * a + r], device_id=peer, device_id_type=MESH).start()
        token[...] = jnp.zeros_like(token)

    hbm = lambda t: pltpu.with_memory_space_constraint(t, pltpu.HBM)
    out = pl.pallas_call(
        body, name=name,
        out_shape=(pltpu.SemaphoreType.DMA((7 * na,)), pltpu.SemaphoreType.DMA((7 * na,)))
        + tuple(pltpu.HBM(t.shape, t.dtype) for t in list(srcs) + list(lands)) + (S((8, LANE), f32),),
        in_specs=[HBM_SPEC] * (2 * na),
        out_specs=(SEM_SPEC, SEM_SPEC) + (HBM_SPEC,) * (2 * na) + (pl.BlockSpec(memory_space=pltpu.VMEM),),
        input_output_aliases={i: 2 + i for i in range(2 * na)},
        compiler_params=pltpu.CompilerParams(has_side_effects=pltpu.SideEffectType.DATAFLOW_SIDE_EFFECTING))(
            *[hbm(t) for t in srcs], *[hbm(t) for t in lands])
    return out[0], out[1], list(out[2:2 + na]), list(out[2 + na:2 + 2 * na]), out[-1]


def _direct_wait(send_sems, recv_sems, srcs, lands, per_peer, after, *, name):
    na = len(srcs)

    def body(*refs):
        src_refs, land_refs = refs[:na], refs[na:2 * na]
        ssem, rsem = refs[2 * na], refs[2 * na + 1]
        me = _dev_index((lax.axis_index("x"), lax.axis_index("y"), lax.axis_index("c")))
        for a in range(na):
            for r, flip in enumerate(_FLIPS):
                peer = _flip_peer(flip)
                src = src_refs[a].at[_dev_index(peer)] if per_peer else src_refs[a]
                cp = pltpu.make_async_remote_copy(
                    src_ref=src, dst_ref=land_refs[a].at[me], send_sem=ssem.at[7 * a + r],
                    recv_sem=rsem.at[7 * a + r], device_id=peer, device_id_type=MESH)
                cp.wait_send()
                cp.wait_recv()

    out = pl.pallas_call(
        body, name=name, out_shape=tuple(pltpu.HBM(t.shape, t.dtype) for t in list(srcs) + list(lands)),
        in_specs=[HBM_SPEC] * (2 * na) + [SEM_SPEC, SEM_SPEC, ANY], out_specs=(HBM_SPEC,) * (2 * na),
        input_output_aliases={i: i for i in range(2 * na)},
        compiler_params=pltpu.CompilerParams(has_side_effects=pltpu.SideEffectType.DATAFLOW_SIDE_EFFECTING))(
            *srcs, *lands, send_sems, recv_sems, after)
    return list(out[:na]), list(out[na:])


def _pack_small(parts, width):
    rows, offs, r = [], [], 0
    for a in parts:
        n = a.size
        nr = -(-n // width)
        flat = a.reshape(-1).astype(f32)
        if nr * width != n:
            flat = jnp.pad(flat, (0, nr * width - n))
        rows.append(flat.reshape(nr, width))
        offs.append((r, nr))
        r += nr
    buf = jnp.concatenate(rows, axis=0)
    pad = (-r) % 8
    if pad:
        buf = jnp.pad(buf, ((0, pad), (0, 0)))
    return buf, offs


def _unpack_small(buf, off, shape):
    r, nr = off
    return buf[r:r + nr].reshape(-1)[:math.prod(shape)].reshape(shape)


def _local_step(x, target, meta, a_w_in, a_w_out, small, start_token, late_weights, grads_ready):
    SEQ, D = x.shape
    n_meta = meta.shape[0]
    first_row = PADF - n_meta
    H = small["a_log"].shape[-1]

    head = jnp.concatenate([jnp.zeros((first_row, D), f32), meta], axis=0)

    def lanes(a):
        return jnp.pad(a.reshape(1, -1), ((0, 0), (0, LANE - a.size)))

    def after_token(a, token):
        return a if token is None else a + token[0:1, 0:1]

    alog, dtb = after_token(lanes(small["a_log"][0]), start_token), lanes(small["a_dt_bias"][0])
    a_conv, b_conv = small["a_conv"][0], small["b_conv"][0]
    nw = small["a_norm"][0].reshape(1, DH)
    lmg, lmb, lfg, lfb = small["ln_mix_g"], small["ln_mix_b"], small["ln_ffn_g"], small["ln_ffn_b"]

    h0, pre_a, z, raw, q, k, v, beta, g, t_all = _gdn_in_fwd(x, head, a_w_in, a_conv, alog, dtb,
                                                             first_row=first_row, H=H)
    o, y, s_all, pre1, h1 = _delta_fwd(q, k, v, g, beta, t_all, z, nw, h0, a_w_out, lmg[0:1], lmb[0:1],
                                       first_row=first_row, H=H)
    wts = late_weights(h1)
    up0, act0, pre2, h2 = _ffn_fwd(h1, wts["ffn_w_up"], small["ffn_conv"][0], wts["ffn_w_down"],
                                   lfg[0:1], lfb[0:1], layer=0, first_row=first_row, name="ffn_fwd0")
    proj_b, bu, pre3, h3 = _sc_fwd(h2, wts["b_w_in"], b_conv, wts["b_w_out"], lmg[1:2], lmb[1:2], first_row=first_row)
    up1, act1, pre4, h4 = _ffn_fwd(h3, wts["ffn_w_up"], small["ffn_conv"][1], wts["ffn_w_down"],
                                   lfg[1:2], lfb[1:2], layer=1, first_row=first_row, name="ffn_fwd1")
    gs = {}
    dpre4, dlfg1, dlfb1, loss_tile = _loss_head(h4, target, pre4, lfg[1:2], first_row=first_row)

    def ffn_backward(dpre, up, act, h_in, layer, tag, ln_in, token=None):
        dup, dcw, dpre_in, dg, db = _ffn_bwd(
            dpre, up, wts["ffn_w_down"], after_token(small["ffn_conv"][layer], token),
            wts["ffn_w_up"], ln_in[0], ln_in[1], layer=layer, first_row=first_row, name="ffn_bwd" + tag)
        dwd = _linear_dw(act, dpre, name="dw_down" + tag)
        dwu = _linear_dw(h_in, dup, name="dw_up" + tag)
        return dpre_in, dg, db, dwu, dwd, dcw[0:3]

    dpre3, dlmg1, dlmb1, dwu1, dwd1, dcf1 = ffn_backward(dpre4, up1, act1, h3, 1, "1", (pre3, lmg[1:2]))

    dproj_b, dcb, dpre2, dlfg0, dlfb0 = _sc_bwd(dpre3, proj_b, b_conv, wts["b_w_out"], wts["b_w_in"], pre2, lfg[0:1],
                                                first_row=first_row)
    dwb_in = _linear_dw(h2, dproj_b, name="dw_b_in")
    token = grads_ready("layer1", dict(ffn_w_up=dwu1, ffn_w_down=dwd1, b_w_in=dwb_in))

    dpre1, dlmg0, dlmb0, dwu0, dwd0, dcf0 = ffn_backward(dpre2, up0, act0, h1, 0, "0", (pre1, lmg[0:1]), token)
    token = grads_ready("layer0", dict(ffn_w_up=dwu0, ffn_w_down=dwd0))

    dq, dk, dv, dz, dg_, dbeta, dnw = _delta_bwd(dpre1, a_w_out, o, z, after_token(nw, token), q, k, v, g, beta,
                                                 s_all, t_all, H=H)
    dproj_a, dca, dal, ddt, grad_x, dhead = _gdn_in_bwd(dq, dk, dv, dz, dg_, dbeta, pre_a, raw, a_conv, alog, dtb,
                                                        a_w_in, dpre1, first_row=first_row, H=H)
    token = grads_ready("last", dict(a_w_in=_linear_dw(h0, dproj_a, name="dw_a_in")))
    grads_ready("tail", dict(a_w_out=_linear_dw(y, dpre1, name="dw_a_out", after=token),
                             b_w_out=_linear_dw(bu, dpre3, name="dw_b_out", after=token)))

    gs["meta"] = dhead[first_row:PADF]
    gs["a_conv"] = dca[0:a_conv.shape[0]][None]
    gs["a_log"] = dal[0:1, 0:H]
    gs["a_dt_bias"] = ddt[0:1, 0:H]
    gs["a_norm"] = dnw[0:1]
    gs["b_conv"] = dcb[0:b_conv.shape[0]][None]
    gs["ln_mix_g"] = jnp.stack([dlmg0[0], dlmg1[0]])
    gs["ln_mix_b"] = jnp.stack([dlmb0[0], dlmb1[0]])
    gs["ffn_conv"] = jnp.stack([dcf0, dcf1])
    gs["ln_ffn_g"] = jnp.stack([dlfg0[0], dlfg1[0]])
    gs["ln_ffn_b"] = jnp.stack([dlfb0[0], dlfb1[0]])
    return loss_tile, grad_x, gs


_BIG = ("a_w_in", "a_w_out", "b_w_in", "b_w_out", "ffn_w_up", "ffn_w_down")
_BIG_COL = ("a_w_in", "b_w_in", "ffn_w_up")
_SMALL = ("meta", "a_conv", "a_log", "a_dt_bias", "a_norm", "b_conv", "ln_mix_g", "ln_mix_b",
          "ffn_conv", "ln_ffn_g", "ln_ffn_b")
_SMALL_SHARDED = ("meta", "a_conv", "b_conv", "ffn_conv")
_ORDER = ("meta", "a_w_in", "a_conv", "a_log", "a_dt_bias", "a_norm", "a_w_out", "b_w_in", "b_conv", "b_w_out",
          "ln_mix_g", "ln_mix_b", "ffn_w_up", "ffn_conv", "ffn_w_down", "ln_ffn_g", "ln_ffn_b")


def _a_w_in_map(H):
    W4 = 4 * H * DH
    return [(0, W4, 0), (W4, W4 + H, W4), (W4 + H, W4 + 2 * H, W4 + LANE)], W4 + 2 * LANE


def kernel(x, meta, a_w_in, a_conv, a_log, a_dt_bias, a_norm, a_w_out, b_w_in, b_conv, b_w_out, ln_mix_g, ln_mix_b, ffn_w_up, ffn_conv, ffn_w_down, ln_ffn_g, ln_ffn_b, loss_target, m_meta, m_a_w_in, m_a_conv, m_a_log, m_a_dt_bias, m_a_norm, m_a_w_out, m_b_w_in, m_b_conv, m_b_w_out, m_ln_mix_g, m_ln_mix_b, m_ffn_w_up, m_ffn_conv, m_ffn_w_down, m_ln_ffn_g, m_ln_ffn_b, v_meta, v_a_w_in, v_a_conv, v_a_log, v_a_dt_bias, v_a_norm, v_a_w_out, v_b_w_in, v_b_conv, v_b_w_out, v_ln_mix_g, v_ln_mix_b, v_ffn_w_up, v_ffn_conv, v_ffn_w_down, v_ln_ffn_g, v_ln_ffn_b):
    wloc = dict(meta=meta, a_w_in=a_w_in, a_conv=a_conv, a_log=a_log, a_dt_bias=a_dt_bias, a_norm=a_norm,
                a_w_out=a_w_out, b_w_in=b_w_in, b_conv=b_conv, b_w_out=b_w_out, ln_mix_g=ln_mix_g, ln_mix_b=ln_mix_b,
                ffn_w_up=ffn_w_up, ffn_conv=ffn_conv, ffn_w_down=ffn_w_down, ln_ffn_g=ln_ffn_g, ln_ffn_b=ln_ffn_b)
    mloc = dict(meta=m_meta, a_w_in=m_a_w_in, a_conv=m_a_conv, a_log=m_a_log, a_dt_bias=m_a_dt_bias, a_norm=m_a_norm,
                a_w_out=m_a_w_out, b_w_in=m_b_w_in, b_conv=m_b_conv, b_w_out=m_b_w_out, ln_mix_g=m_ln_mix_g,
                ln_mix_b=m_ln_mix_b, ffn_w_up=m_ffn_w_up, ffn_conv=m_ffn_conv, ffn_w_down=m_ffn_w_down,
                ln_ffn_g=m_ln_ffn_g, ln_ffn_b=m_ln_ffn_b)
    vloc = dict(meta=v_meta, a_w_in=v_a_w_in, a_conv=v_a_conv, a_log=v_a_log, a_dt_bias=v_a_dt_bias, a_norm=v_a_norm,
                a_w_out=v_a_w_out, b_w_in=v_b_w_in, b_conv=v_b_conv, b_w_out=v_b_w_out, ln_mix_g=v_ln_mix_g,
                ln_mix_b=v_ln_mix_b, ffn_w_up=v_ffn_w_up, ffn_conv=v_ffn_conv, ffn_w_down=v_ffn_w_down,
                ln_ffn_g=v_ln_ffn_g, ln_ffn_b=v_ln_ffn_b)
    H = a_log.shape[-1]
    mx, my, mc = lax.axis_index("x"), lax.axis_index("y"), lax.axis_index("c")
    me = 4 * mx + 2 * my + mc

    a_map, a_cols = _a_w_in_map(H)
    col_maps = {"a_w_in": (a_map, a_cols)}
    for n in ("b_w_in", "ffn_w_up"):
        ncols = N_DEV * wloc[n].shape[-1]
        col_maps[n] = ([(0, ncols, 0)], ncols)
    sm_sh = [wloc[n] for n in _SMALL_SHARDED]
    sbuf, soffs = _pack_small(sm_sh, 128)
    g_a_w_in, g_a_w_out, sg = _all_gather([_bf(wloc["a_w_in"]), _bf(wloc["a_w_out"]), sbuf], name="gather_first")
    w_a_in = _assemble_cols(g_a_w_in, *col_maps["a_w_in"], name="assemble_a_w_in")[0]
    w_a_out = _rows_full(g_a_w_out)[0]
    late = [n for n in _BIG if n not in ("a_w_in", "a_w_out")]
    ssem, rsem, srcs_t, lands_t, start_token = _direct_start(
        [_bf(wloc[n]) for n in late], [lax.empty((N_DEV,) + wloc[n].shape, bf16) for n in late], False,
        name="gather_rest_start")

    def late_weights(after):
        srcs_d, landed = _direct_wait(ssem, rsem, srcs_t, lands_t, False, after, name="gather_rest_wait")
        wts = {}
        for n, own, got in zip(late, srcs_d, landed):
            full = lax.dynamic_update_index_in_dim(got, own, me, 0)
            if n in _BIG_COL:
                wts[n] = _assemble_cols(full, *col_maps[n], name="assemble_" + n)
            elif n == "ffn_w_down":
                wts[n] = full
            else:
                wts[n] = _rows_full(full)
        for n in ("b_w_in", "b_w_out"):
            wts[n] = wts[n][0]
        return wts

    small = {n: wloc[n] for n in _SMALL}
    for n, off in zip(_SMALL_SHARDED, soffs):
        sh = wloc[n].shape
        parts = jnp.stack([_unpack_small(sg[d], off, sh) for d in range(N_DEV)])
        nd = len(sh)
        small[n] = jnp.transpose(parts, tuple(range(1, nd)) + (0, nd)).reshape(sh[:-1] + (N_DEV * sh[-1],))

    def split(n, dws, tag):
        if n in _BIG_COL:
            return _split_cols(dws, col_maps[n][0], wloc[n].shape[-1], name="split_" + n + tag)
        return _split_rows(dws, wloc[n].shape[-2], name="split_" + n + tag)

    sent = {}

    def grads_ready(stage, grads):
        names = sorted(grads)
        parts = [split(n, [grads[n]], "_" + stage) for n in names]
        handles = _direct_start([p[1] for p in parts], [jnp.zeros(p[1].shape, bf16) for p in parts], True,
                                name="grads_" + stage + "_start")
        sent[stage] = (names, [p[0] for p in parts], handles)
        return handles[4]

    loss_tile, grad_x, gs = _local_step(x[0], loss_target[0], small["meta"], w_a_in, w_a_out, small, start_token,
                                        late_weights, grads_ready)

    def landed(stage, after):
        names, own32, (ssem_g, rsem_g, srcs_g, lands_g, _) = sent[stage]
        _, got = _direct_wait(ssem_g, rsem_g, srcs_g, lands_g, True, after, name="grads_" + stage + "_wait")
        return list(zip(names, own32, got))

    parts = {}
    for stage in ("layer0", "layer1"):
        for n, o32, r in landed(stage, grad_x):
            parts.setdefault(n, []).append((o32, r))
    me1 = jnp.stack([me]).astype(jnp.int32)
    big_out = {n: _adamw_direct([p[0] for p in ps], [p[1] for p in ps], wloc[n], mloc[n], vloc[n], me1,
                                name="adamw_" + n) for n, ps in parts.items()}
    names = list(_SMALL)
    pbuf, poffs = _pack_small([gs[n] for n in names] + [loss_tile[0:1, 0:1]], 1024)
    psum = _sum_devices(_all_gather([pbuf], name="gather_small_grads")[0])
    loss = psum[poffs[-1][0], 0]
    g_small = {}
    for n, off in zip(names, poffs[:-1]):
        full_shape = gs[n].shape
        gfull = _unpack_small(psum, off, full_shape)
        if n in _SMALL_SHARDED:
            ns = wloc[n].shape[-1]
            gfull = lax.dynamic_slice_in_dim(gfull, me * ns, ns, axis=gfull.ndim - 1)
        g_small[n] = gfull.reshape(wloc[n].shape)
    gbuf, aoffs = _pack_small([g_small[n] for n in names], 128)
    wbuf, _ = _pack_small([wloc[n] for n in names], 128)
    mbuf, _ = _pack_small([mloc[n] for n in names], 128)
    vbuf, _ = _pack_small([vloc[n] for n in names], 128)
    _, d_s, m_s, v_s = _adamw([gbuf], wbuf, mbuf, vbuf, name="adamw_small")

    done = d_s[0, 0]
    for out in big_out.values():
        done = done + out[1][0, 0, 0]
    for stage in ("last", "tail"):
        for n, o32, r in landed(stage, done.reshape(1, 1)):
            big_out[n] = _adamw_direct([o32], [r], wloc[n], mloc[n], vloc[n], me1, name="adamw_" + n)

    grads, deltas, new_m, new_v = {}, {}, {}, {}
    for n in _BIG:
        grads[n], deltas[n], new_m[n], new_v[n] = big_out[n]
    for n, off in zip(names, aoffs):
        sh = wloc[n].shape
        grads[n] = g_small[n]
        deltas[n], new_m[n], new_v[n] = (_unpack_small(b_, off, sh) for b_ in (d_s, m_s, v_s))
    return (loss, grad_x[None], *[grads[n] for n in _ORDER], *[deltas[n] for n in _ORDER],
            *[new_m[n] for n in _ORDER], *[new_v[n] for n in _ORDER])
```

```python
import math

import jax
import jax.numpy as jnp
from jax import lax
from jax.experimental import pallas as pl
from jax.experimental.pallas import tpu as pltpu

f32, bf16 = jnp.float32, jnp.bfloat16
S = jax.ShapeDtypeStruct
HI = lax.Precision.HIGHEST
MESH = pl.DeviceIdType.MESH

V7X_VMEM_LIMIT = 56 * 1024 * 1024
LANE = 128
DH = 128
CH = 64
PADF = 256
TM = 256
TMM = 768
N_DEV = 8
BWD_HEAD_GROUP = 4
BWD_GROUP_LAG = 2

DEPTH = 2
ALPHA = (2.0 * DEPTH) ** 0.25
LN_EPS = 1e-5
RMS_EPS = 1e-6
L2_EPS = 1e-6
ADAM_LR, ADAM_B1, ADAM_B2, ADAM_EPS, ADAM_WD, ADAM_STEP = 0.001, 0.9, 0.999, 1e-08, 0.01, 10


def _cp(**kw):
    return pltpu.CompilerParams(vmem_limit_bytes=V7X_VMEM_LIMIT, **kw)


def _bf(x):
    return x.astype(bf16)


def _dot(a, b, precision=None):
    return jnp.dot(a, b, preferred_element_type=f32, precision=precision)


def _dot_nt(a, b):
    return lax.dot_general(a, b, (((1,), (1,)), ((), ())), preferred_element_type=f32)


def _dot_tn(a, b):
    return lax.dot_general(a, b, (((0,), (0,)), ((), ())), preferred_element_type=f32)


def _sigmoid(x):
    return 1.0 / (1.0 + jnp.exp(-x))


def _load_once(pairs, sem):
    @pl.when(pl.program_id(0) == 0)
    def _():
        cps = [pltpu.make_async_copy(src, dst, sem.at[n]) for n, (src, dst) in enumerate(pairs)]
        for c in cps:
            c.start()
        for c in cps:
            c.wait()


def _row_ids(i, tm, width):
    return i * tm + lax.broadcasted_iota(jnp.int32, (tm, width), 0)


def _ln_fwd(pre, g, b, rows, first_row):
    mu = jnp.mean(pre, axis=-1, keepdims=True)
    xc = pre - mu
    var = jnp.mean(xc * xc, axis=-1, keepdims=True)
    y = xc * lax.rsqrt(var + LN_EPS) * g + b
    return jnp.where(rows >= first_row, y, 0.0)


ANY = pl.BlockSpec(memory_space=pl.ANY)


def _taps_back(prev8, x, kw):
    xe = jnp.concatenate([prev8, x], axis=0)
    return [pltpu.roll(xe, kw - 1 - j, 0)[8:] for j in range(kw - 1)] + [x]


def _taps_ahead(x, next8, kw):
    n = x.shape[0]
    xe = jnp.concatenate([x, next8], axis=0)
    return [pltpu.roll(xe, n + 8 - (kw - 1 - j), 0)[:n] for j in range(kw - 1)] + [x]


def _conv(cw, taps):
    acc = cw[0:1, :] * taps[0]
    for j in range(1, len(taps)):
        acc = acc + cw[j:j + 1, :] * taps[j]
    return acc


def _linear_dw(x, dy, *, name, after=None):
    L, K = x.shape
    N = dy.shape[1]
    tm = TMM if L % TMM == 0 else TM
    tn = LANE
    for d in range(N // LANE, 0, -1):
        if (N // LANE) % d == 0 and K * d * LANE * 4 <= 9 * 1024 * 1024:
            tn = d * LANE
            break

    def body(x_ref, dy_ref, *rest):
        o_ref = rest[-1]

        @pl.when(pl.program_id(1) == 0)
        def _():
            o_ref[...] = jnp.zeros_like(o_ref)
        o_ref[...] += _dot_tn(_bf(x_ref[...]), _bf(dy_ref[...]))

    in_specs = [pl.BlockSpec((tm, K), lambda j, i: (i, 0)), pl.BlockSpec((tm, tn), lambda j, i: (i, j))]
    args = [x, dy]
    if after is not None:
        in_specs.append(pl.BlockSpec(after.shape, lambda j, i: (0, 0)))
        args.append(after)
    return pl.pallas_call(
        body, name=name, grid=(N // tn, L // tm), out_shape=S((K, N), f32),
        in_specs=in_specs, out_specs=pl.BlockSpec((K, tn), lambda j, i: (0, j)),
        compiler_params=_cp(dimension_semantics=("arbitrary", "arbitrary")))(*args)


def _ln_bwd_rows(dout, pre, g, rows, first_row):
    mu = jnp.mean(pre, axis=-1, keepdims=True)
    xc = pre - mu
    rstd = lax.rsqrt(jnp.mean(xc * xc, axis=-1, keepdims=True) + LN_EPS)
    xh = xc * rstd
    dy = jnp.where(rows >= first_row, dout, 0.0)
    dxh = dy * g
    dpre = rstd * (dxh - jnp.mean(dxh, axis=-1, keepdims=True) - xh * jnp.mean(dxh * xh, axis=-1, keepdims=True))
    return dpre, jnp.sum(dy * xh, axis=0, keepdims=True), jnp.sum(dy, axis=0, keepdims=True)


def _gdn_in_fwd(x, head, w_full, conv_w, alog, dtb, *, first_row, H):
    D = x.shape[1]
    L = PADF + x.shape[0]
    W = H * DH
    NW = w_full.shape[1]
    KW = conv_w.shape[0]
    tm = TM
    pb = PADF // tm

    def body(x_ref, head_ref, w_hbm, cw_ref, alog_ref, dtb_ref,
             h_ref, pre_ref, z_ref, raw_ref, q_ref, k_ref, v_ref, beta_ref, g_ref, t_ref,
             w_vmem, carry, sem):
        i = pl.program_id(0)
        _load_once([(w_hbm, w_vmem)], sem)

        @pl.when(i == 0)
        def _():
            carry[...] = jnp.zeros_like(carry)

        hv = jnp.where(i < pb, head_ref[...], x_ref[...])
        h_ref[...] = hv
        hb = _bf(hv)
        outs = (q_ref, k_ref, v_ref)

        def section(s):
            pre = _dot(hb, w_vmem[:, s * W:(s + 1) * W])
            pre_ref[:, s * W:(s + 1) * W] = pre
            c = _conv(cw_ref[:, s * W:(s + 1) * W], _taps_back(carry[s], pre, KW))
            carry[s] = pre[tm - 8:tm, :]
            sl = c * _sigmoid(c)
            if s < 2:
                scale = DH ** -0.5 if s == 0 else 1.0
                for hh in range(H):
                    seg = sl[:, hh * DH:(hh + 1) * DH]
                    r = lax.rsqrt(jnp.sum(seg * seg, axis=-1, keepdims=True) + L2_EPS)
                    outs[s][:, hh * DH:(hh + 1) * DH] = seg * (r * scale)
            else:
                v_ref[...] = sl

        raw = _dot(hb, w_vmem[:, 4 * W:4 * W + 2 * LANE])
        raw_ref[...] = raw
        ok = (_row_ids(i, tm, LANE) >= first_row) & (lax.broadcasted_iota(jnp.int32, (tm, LANE), 1) < H)
        beta = jnp.where(ok, _sigmoid(raw[:, :LANE]), 0.0)
        beta_ref[...] = beta
        a = raw[:, LANE:] + dtb_ref[...]
        sp = jnp.maximum(a, 0.0) + jnp.log(1.0 + jnp.exp(-jnp.abs(a)))
        gv = jnp.where(ok, -jnp.exp(alog_ref[...]) * sp, 0.0)
        gam = _dot(_chunk_tri(tm, lower=True), gv, HI)
        g_ref[...] = gam
        section(1)
        ii = lax.broadcasted_iota(jnp.int32, (CH, CH), 0)
        jj = lax.broadcasted_iota(jnp.int32, (CH, CH), 1)
        eye = (ii == jj).astype(f32)
        gam_t = gam.T

        def inverses(chunks):
            ms = []
            for c in chunks:
                rows = slice(c * CH, (c + 1) * CH)
                for hh in range(H):
                    kh = k_ref[rows, hh * DH:(hh + 1) * DH]
                    dec = jnp.exp(jnp.minimum(gam[rows, hh:hh + 1] - gam_t[hh:hh + 1, rows], 0.0))
                    kk = _dot_nt(_bf(kh * beta[rows, hh:hh + 1]), _bf(kh))
                    ms.append(jnp.where(ii > jj, kk * dec, 0.0))
            for n, t in enumerate(_tri_inv_many(ms, eye)):
                t_ref[chunks[n // H], n % H] = t

        nch = tm // CH
        inverses(list(range(nch // 2)))
        section(0)
        inverses(list(range(nch // 2, nch)))
        section(2)
        z_ref[...] = _dot(hb, w_vmem[:, 3 * W:4 * W])

    row = lambda i: (i, 0)
    fix = lambda i: (0, 0)
    out_shape = (S((L, D), f32), S((L, 3 * W), f32), S((L, W), f32), S((L, 2 * LANE), f32),
                 S((L, W), f32), S((L, W), f32), S((L, W), f32), S((L, LANE), f32), S((L, LANE), f32),
                 S((L // CH, H, CH, CH), f32))
    out_specs = (pl.BlockSpec((tm, D), row),
                 pl.BlockSpec((tm, 3 * W), row), pl.BlockSpec((tm, W), row), pl.BlockSpec((tm, 2 * LANE), row),
                 pl.BlockSpec((tm, W), row), pl.BlockSpec((tm, W), row), pl.BlockSpec((tm, W), row),
                 pl.BlockSpec((tm, LANE), row), pl.BlockSpec((tm, LANE), row),
                 pl.BlockSpec((tm // CH, H, CH, CH), lambda i: (i, 0, 0, 0)))
    return pl.pallas_call(
        body, name="gdn_in_fwd", grid=(L // tm,), out_shape=out_shape,
        in_specs=[pl.BlockSpec((tm, D), lambda i: (jnp.maximum(i - pb, 0), 0)),
                  pl.BlockSpec((tm, D), lambda i: (jnp.minimum(i, pb - 1), 0)), ANY, pl.BlockSpec((KW, 3 * W), fix),
                  pl.BlockSpec((1, LANE), fix), pl.BlockSpec((1, LANE), fix)],
        out_specs=out_specs,
        scratch_shapes=[pltpu.VMEM((D, NW), w_full.dtype), pltpu.VMEM((3, 8, W), f32), pltpu.SemaphoreType.DMA((1,))],
        compiler_params=_cp(dimension_semantics=("arbitrary",)))(x, head, w_full, conv_w, alog, dtb)


def _gdn_in_bwd(dq, dk, dv, dz, dg, dbeta, pre, raw, conv_w, alog, dtb, w_full, res, *, first_row, H):
    L = dq.shape[0]
    D = res.shape[1]
    W = H * DH
    KW = conv_w.shape[0]
    tm = TM
    nb = L // tm
    NW = 4 * W + 2 * LANE
    fb = PADF // tm
    alpha = ALPHA

    def body(dq_ref, dk_ref, dv_ref, dz_ref, dg_ref, dbeta_ref, pre_ref, hq_ref, hk_ref, hv_ref, raw_ref,
             cw_ref, alog_ref, dtb_ref, w_hbm, res_ref,
             dproj_ref, dcw_ref, dal_ref, ddt_ref, dx_ref, dfront_ref, w_vmem, carry, tmp, sem):
        i = pl.program_id(0)
        blk = nb - 1 - i
        _load_once([(w_hbm, w_vmem)], sem)

        @pl.when(i == 0)
        def _():
            carry[...] = jnp.zeros_like(carry)
            dcw_ref[...] = jnp.zeros_like(dcw_ref)
            dal_ref[...] = jnp.zeros_like(dal_ref)
            ddt_ref[...] = jnp.zeros_like(ddt_ref)

        halos = (hq_ref, hk_ref, hv_ref)
        douts = (dq_ref, dk_ref, dv_ref)
        for s in range(3):
            sec = slice(s * W, (s + 1) * W)
            pre = pre_ref[:, sec]
            c = _conv(cw_ref[:, sec], _taps_back(jnp.where(blk > 0, halos[s][...], 0.0), pre, KW))
            sig = _sigmoid(c)
            sl = c * sig
            if s < 2:
                scale = DH ** -0.5 if s == 0 else 1.0
                for hh in range(H):
                    hs = slice(hh * DH, (hh + 1) * DH)
                    seg = sl[:, hs]
                    r = lax.rsqrt(jnp.sum(seg * seg, axis=-1, keepdims=True) + L2_EPS)
                    n = seg * r
                    dqs = douts[s][:, hs]
                    tmp[:, hs] = (scale * r) * (dqs - n * jnp.sum(n * dqs, axis=-1, keepdims=True))
                dsl = tmp[...]
            else:
                dsl = dv_ref[...]
            dc = dsl * (sig * (1.0 + c * (1.0 - sig)))
            ahead = _taps_ahead(dc, carry[s], KW)
            carry[s] = dc[0:8, :]
            dproj_ref[:, sec] = _bf(_conv(cw_ref[:, sec], ahead))
            for j in range(KW):
                dcw_ref[j:j + 1, sec] += jnp.sum(ahead[j] * pre, axis=0, keepdims=True)
        dproj_ref[:, 3 * W:4 * W] = _bf(dz_ref[...])
        raw_v = raw_ref[...]
        ok = (_row_ids(blk, tm, LANE) >= first_row) & (lax.broadcasted_iota(jnp.int32, (tm, LANE), 1) < H)
        beta = _sigmoid(raw_v[:, :LANE])
        dbraw = jnp.where(ok, dbeta_ref[...] * beta * (1.0 - beta), 0.0)
        a = raw_v[:, LANE:] + dtb_ref[...]
        sp = jnp.maximum(a, 0.0) + jnp.log(1.0 + jnp.exp(-jnp.abs(a)))
        nea = -jnp.exp(alog_ref[...])
        dgm = jnp.where(ok, _dot(_chunk_tri(tm, lower=False), dg_ref[...], HI), 0.0)
        daraw = dgm * nea * _sigmoid(a)
        dal_ref[0:1, :] += jnp.sum(dgm * nea * sp, axis=0, keepdims=True)
        ddt_ref[0:1, :] += jnp.sum(daraw, axis=0, keepdims=True)
        dproj_ref[:, 4 * W:4 * W + LANE] = _bf(dbraw)
        dproj_ref[:, 4 * W + LANE:4 * W + 2 * LANE] = _bf(daraw)
        dh = alpha * res_ref[...] + _dot_nt(dproj_ref[...], w_vmem[...])

        @pl.when(blk >= fb)
        def _():
            dx_ref[...] = dh

        @pl.when(blk < fb)
        def _():
            dfront_ref[...] = dh

    rev = lambda i: (nb - 1 - i, 0)
    fix = lambda i: (0, 0)

    def halo(col):
        return pl.BlockSpec((8, W), lambda i: (jnp.maximum((nb - 1 - i) * (tm // 8) - 1, 0), col))

    return pl.pallas_call(
        body, name="gdn_in_bwd", grid=(nb,),
        out_shape=(S((L, NW), bf16), S((8, 3 * W), f32), S((8, LANE), f32), S((8, LANE), f32),
                   S((L - PADF, D), f32), S((PADF, D), f32)),
        in_specs=[pl.BlockSpec((tm, W), rev)] * 4 + [pl.BlockSpec((tm, LANE), rev)] * 2
        + [pl.BlockSpec((tm, 3 * W), rev), halo(0), halo(1), halo(2), pl.BlockSpec((tm, 2 * LANE), rev),
           pl.BlockSpec((KW, 3 * W), fix), pl.BlockSpec((1, LANE), fix), pl.BlockSpec((1, LANE), fix),
           ANY, pl.BlockSpec((tm, D), rev)],
        out_specs=(pl.BlockSpec((tm, NW), rev), pl.BlockSpec((8, 3 * W), fix),
                   pl.BlockSpec((8, LANE), fix), pl.BlockSpec((8, LANE), fix),
                   pl.BlockSpec((tm, D), lambda i: (jnp.maximum(nb - 1 - i - fb, 0), 0)),
                   pl.BlockSpec((tm, D), lambda i: (jnp.minimum(nb - 1 - i, fb - 1), 0))),
        scratch_shapes=[pltpu.VMEM((D, NW), w_full.dtype), pltpu.VMEM((3, 8, W), f32), pltpu.VMEM((tm, W), f32),
                        pltpu.SemaphoreType.DMA((1,))],
        compiler_params=_cp(dimension_semantics=("arbitrary",)))(
            dq, dk, dv, dz, dg, dbeta, pre, pre, pre, pre, raw, conv_w, alog, dtb, w_full, res)


def _chunk_tri(n, lower):
    i = lax.broadcasted_iota(jnp.int32, (n, n), 0)
    j = lax.broadcasted_iota(jnp.int32, (n, n), 1)
    sh = int(math.log2(CH))
    same = lax.shift_right_logical(i, sh) == lax.shift_right_logical(j, sh)
    return (same & ((i >= j) if lower else (j >= i))).astype(f32)


def _tri_inv_many(ms, eye):
    ts = [eye - m for m in ms]
    ps = list(ms)
    for _ in range(int(math.log2(CH)) - 1):
        pb = [_bf(p) for p in ps]
        ps = [_dot(p, p) for p in pb]
        ts = [t + _dot(_bf(t), _bf(p)) for t, p in zip(ts, ps)]
    return ts


def _chunk_local(q, k, v, gcol, grow, glast, bcol, ii, jj):
    dec = jnp.where(ii >= jj, jnp.exp(jnp.minimum(gcol - grow, 0.0)), 0.0)
    eg = jnp.exp(gcol)
    kb = k * bcol
    kbg = kb * eg
    vb = v * bcol
    qt = q * eg
    kt = k * jnp.exp(glast - gcol)
    kbb, qb, kbf = _bf(kb), _bf(q), _bf(k)
    return dec, eg, kb, kbg, vb, qt, kt, _dot_nt(kbb, kbf), _dot_nt(qb, kbf), jnp.concatenate([kbb, qb], axis=0)


def _delta_fwd(q, k, v, g, beta, t_all, z, nw, h, w_out, ln_g, ln_b, *, first_row, H):
    L = q.shape[0]
    W = H * DH
    D = h.shape[1]
    rb = TM
    nc = rb // CH
    nblk = L // rb
    alpha = ALPHA

    def body(q_ref, k_ref, v_ref, g_ref, b_ref, t_ref, z_ref, nw_ref, h_ref, wout_hbm, lg_ref, lb_ref,
             o_ref, y_ref, s_out, pre_ref, out_ref, s_scr, wout, sem):
        _load_once([(wout_hbm, wout)], sem)

        @pl.when(pl.program_id(0) == 0)
        def _():
            s_scr[...] = jnp.zeros_like(s_scr)

        ii = lax.broadcasted_iota(jnp.int32, (CH, CH), 0)
        jj = lax.broadcasted_iota(jnp.int32, (CH, CH), 1)
        eye = (ii == jj).astype(f32)
        nwv = nw_ref[...]

        heads = range(H)
        hsl = [slice(hh * DH, (hh + 1) * DH) for hh in heads]

        def chunk(c, carry):
            r0 = pl.multiple_of(c * CH, CH)
            rows = pl.ds(r0, CH)
            gam = g_ref[rows, :]
            gam_t = gam.T
            bb = b_ref[rows, :]
            glast = [gam[CH - 1:CH, hh:hh + 1] for hh in heads]
            loc = [_chunk_local(q_ref[rows, hsl[hh]], k_ref[rows, hsl[hh]], v_ref[rows, hsl[hh]],
                                gam[:, hh:hh + 1], gam_t[hh:hh + 1, :], glast[hh], bb[:, hh:hh + 1], ii, jj)
                   for hh in heads]
            st = [s_scr[hh] for hh in heads]
            zs = [z_ref[rows, hsl[hh]] for hh in heads]
            ts = [_bf(t_ref[c, hh]) for hh in heads]
            us = [_dot(t, _bf(l[4])) for t, l in zip(ts, loc)]
            ws = [_dot(t, _bf(l[3])) for t, l in zip(ts, loc)]
            stb = [_bf(s) for s in st]
            vn = [u - _dot(_bf(w), sb) for u, w, sb in zip(us, ws, stb)]
            vnb = [_bf(x) for x in vn]
            snew = [s * jnp.exp(gl) + _dot_tn(_bf(l[6]), xb) for s, gl, l, xb in zip(st, glast, loc, vnb)]
            os_ = [_dot(_bf(l[5]), sb) + _dot(_bf(l[8] * l[0]), xb) for l, sb, xb in zip(loc, stb, vnb)]
            for hh in heads:
                o = os_[hh]
                s_out[c, hh] = st[hh]
                s_scr[hh] = snew[hh]
                o_ref[rows, hsl[hh]] = o
                on = o * lax.rsqrt(jnp.mean(o * o, axis=-1, keepdims=True) + RMS_EPS) * nwv
                y_ref[rows, hsl[hh]] = _bf(on * (zs[hh] * _sigmoid(zs[hh])))
            return carry

        lax.fori_loop(0, nc, chunk, 0)
        pre = alpha * h_ref[...] + _dot(y_ref[...], wout[...])
        pre_ref[...] = pre
        out_ref[...] = _ln_fwd(pre, lg_ref[...], lb_ref[...], _row_ids(pl.program_id(0), rb, D), first_row)

    row = lambda i: (i, 0)
    fix = lambda i: (0, 0)
    return pl.pallas_call(
        body, name="delta_fwd", grid=(nblk,),
        out_shape=(S((L, W), f32), S((L, W), bf16), S((L // CH, H, DH, DH), f32), S((L, D), f32), S((L, D), f32)),
        in_specs=[pl.BlockSpec((rb, W), row)] * 3 + [pl.BlockSpec((rb, LANE), row)] * 2
        + [pl.BlockSpec((nc, H, CH, CH), lambda i: (i, 0, 0, 0)),
           pl.BlockSpec((rb, W), row), pl.BlockSpec((1, DH), fix), pl.BlockSpec((rb, D), row), ANY,
           pl.BlockSpec((1, D), fix), pl.BlockSpec((1, D), fix)],
        out_specs=(pl.BlockSpec((rb, W), row), pl.BlockSpec((rb, W), row),
                   pl.BlockSpec((nc, H, DH, DH), lambda i: (i, 0, 0, 0)),
                   pl.BlockSpec((rb, D), row), pl.BlockSpec((rb, D), row)),
        scratch_shapes=[pltpu.VMEM((H, DH, DH), f32), pltpu.VMEM((W, D), w_out.dtype), pltpu.SemaphoreType.DMA((1,))],
        compiler_params=_cp(dimension_semantics=("arbitrary",)))(q, k, v, g, beta, t_all, z, nw, h, w_out, ln_g, ln_b)


def _delta_bwd(dpre, w_out, o, z, nw, q, k, v, g, beta, s_all, t_all, *, H):
    L = q.shape[0]
    W = H * DH
    D = dpre.shape[1]
    rb = TM
    nc = rb // CH
    nblk = L // rb

    def body(dpre_ref, wout_hbm, o_ref, z_ref, nw_ref, q_ref, k_ref, v_ref, g_ref, b_ref, s_ref, t_ref,
             dq_ref, dk_ref, dv_ref, dz_ref, dg_ref, db_ref, dnw_ref, ds_scr, wout, dy_scr, sem):
        _load_once([(wout_hbm, wout)], sem)

        @pl.when(pl.program_id(0) == 0)
        def _():
            ds_scr[...] = jnp.zeros_like(ds_scr)
            dnw_ref[...] = jnp.zeros_like(dnw_ref)

        dy_scr[...] = _dot_nt(_bf(dpre_ref[...]), wout[...])

        ii = lax.broadcasted_iota(jnp.int32, (CH, CH), 0)
        jj = lax.broadcasted_iota(jnp.int32, (CH, CH), 1)
        lane = lax.broadcasted_iota(jnp.int32, (CH, LANE), 1)
        last_row = lax.broadcasted_iota(jnp.int32, (CH, 1), 0) == CH - 1
        nwv = nw_ref[...]

        def chunk(cc, carry):
            c = nc - 1 - cc
            r0 = pl.multiple_of(c * CH, CH)
            rows = pl.ds(r0, CH)
            gam = g_ref[rows, :]
            gam_t = gam.T
            bb = b_ref[rows, :]

            def head(hh):
                hs = slice(hh * DH, (hh + 1) * DH)
                gcol, grow, glast = gam[:, hh:hh + 1], gam_t[hh:hh + 1, :], gam[CH - 1:CH, hh:hh + 1]
                bcol = bb[:, hh:hh + 1]
                qh, kh, vh = q_ref[rows, hs], k_ref[rows, hs], v_ref[rows, hs]
                oh, zh, dyh = o_ref[rows, hs], z_ref[rows, hs], dy_scr[rows, hs]
                t = t_ref[c, hh]
                st = s_ref[c, hh]
                dsn = ds_scr[hh]
                rms = lax.rsqrt(jnp.mean(oh * oh, axis=-1, keepdims=True) + RMS_EPS)
                on = oh * rms
                sig = _sigmoid(zh)
                sz = zh * sig
                dz_ref[rows, hs] = dyh * on * nwv * (sig * (1.0 + zh * (1.0 - sig)))
                dnw = jnp.sum(dyh * on * sz, axis=0, keepdims=True)
                don = dyh * nwv * sz
                do = rms * (don - on * jnp.mean(don * on, axis=-1, keepdims=True))
                dec, eg, kb, kbg, vb, qt, kt, kk, qk, kqb = _chunk_local(qh, kh, vh, gcol, grow, glast, bcol, ii, jj)
                stb, dsnb, dob, tb, kbgb = _bf(st), _bf(dsn), _bf(do), _bf(t), _bf(kbg)
                r = vb - _dot(kbgb, stb)
                mm = jnp.where(ii > jj, kk * dec, 0.0)
                attn = qk * dec
                yield
                rbf = _bf(r)
                vn = _dot(tb, rbf)
                dvn = _dot_tn(_bf(attn), dob) + _dot(_bf(kt), dsnb)
                egl = jnp.exp(glast)
                ekt = jnp.exp(glast - gcol)
                yield
                vnb, dvnb = _bf(vn), _bf(dvn)
                dattn = jnp.where(ii >= jj, _dot_nt(dob, vnb), 0.0)
                dkt = _dot_nt(vnb, dsnb)
                dvb = _dot_tn(tb, dvnb)
                dt = _dot_nt(dvnb, rbf)
                dglast = egl * jnp.sum(jnp.sum(dsn * st, axis=0, keepdims=True), axis=1, keepdims=True)
                yield
                dv_ref[rows, hs] = dvb * bcol
                dod = jnp.concatenate([dob, -_bf(dvb)], axis=0)
                ds_scr[hh] = egl * dsn + _dot_tn(jnp.concatenate([_bf(qt), kbgb], axis=0), dod)
                both = _dot_nt(dod, stb)
                dqt, dkbg = both[:CH], both[CH:]
                x = _dot_nt(_bf(dt), tb)
                yield
                dm = jnp.where(ii > jj, -_dot_tn(tb, _bf(x)), 0.0)
                dkk = dm * dec
                dqk = dattn * dec
                e = dm * mm + dattn * attn
                dgam = jnp.sum(e, axis=1, keepdims=True) - jnp.sum(e.T, axis=1, keepdims=True)
                dd = _bf(jnp.concatenate([dkk, dqk], axis=0))
                both = _dot(dd, _bf(kh))
                dkb = both[:CH] + dkbg * eg
                dk_ref[rows, hs] = _dot_tn(dd, kqb) + dkt * ekt + dkb * bcol
                dq_ref[rows, hs] = both[CH:] + dqt * eg
                yield
                dktkt = dkt * kt
                dgam = dgam + jnp.sum(dqt * qt - dktkt + dkbg * kbg, axis=1, keepdims=True)
                dglast = dglast + jnp.sum(jnp.sum(dktkt, axis=0, keepdims=True), axis=1, keepdims=True)
                dgam = dgam + jnp.where(last_row, dglast, 0.0)
                dbeta = jnp.sum(dkb * kh + dvb * vh, axis=1, keepdims=True)
                return dgam, dbeta, dnw

            res = [None] * H
            gens = [head(hh) for hh in range(H)]
            step = 0
            while any(r is None for r in res):
                for hh in range(H):
                    if res[hh] is None and step >= (hh // BWD_HEAD_GROUP) * BWD_GROUP_LAG:
                        try:
                            next(gens[hh])
                        except StopIteration as stop:
                            res[hh] = stop.value
                step += 1
            dgam_all = jnp.zeros((CH, LANE), f32)
            dbeta_all = jnp.zeros((CH, LANE), f32)
            dnw_acc = jnp.zeros((1, DH), f32)
            for hh in range(H):
                dgam, dbeta, dnw = res[hh]
                dgam_all = dgam_all + jnp.where(lane == hh, dgam, 0.0)
                dbeta_all = dbeta_all + jnp.where(lane == hh, dbeta, 0.0)
                dnw_acc = dnw_acc + dnw
            dg_ref[rows, :] = dgam_all
            db_ref[rows, :] = dbeta_all
            dnw_ref[0:1, :] += dnw_acc
            return carry

        lax.fori_loop(0, nc, chunk, 0)

    rev = lambda i: (nblk - 1 - i, 0)
    rev4 = lambda i: (nblk - 1 - i, 0, 0, 0)
    fix = lambda i: (0, 0)
    wide = pl.BlockSpec((rb, W), rev)
    thin = pl.BlockSpec((rb, LANE), rev)
    return pl.pallas_call(
        body, name="delta_bwd", grid=(nblk,),
        out_shape=(S((L, W), f32),) * 4 + (S((L, LANE), f32),) * 2 + (S((8, DH), f32),),
        in_specs=[pl.BlockSpec((rb, D), rev), ANY, wide, wide, pl.BlockSpec((1, DH), fix), wide, wide, wide, thin, thin,
                  pl.BlockSpec((nc, H, DH, DH), rev4), pl.BlockSpec((nc, H, CH, CH), rev4)],
        out_specs=(wide,) * 4 + (thin, thin, pl.BlockSpec((8, DH), fix)),
        scratch_shapes=[pltpu.VMEM((H, DH, DH), f32), pltpu.VMEM((W, D), w_out.dtype), pltpu.VMEM((rb, W), f32),
                        pltpu.SemaphoreType.DMA((1,))],
        compiler_params=_cp(dimension_semantics=("arbitrary",)))(
            dpre, w_out, o, z, nw, q, k, v, g, beta, s_all, t_all)


def _sc_fwd(h, w_in, conv_w, w_out, g, b, *, first_row):
    L, D = h.shape
    W = w_out.shape[0]
    KW = conv_w.shape[0]
    tm = TM
    alpha = ALPHA

    def body(h_ref, win_hbm, cw_ref, wout_hbm, g_ref, b_ref, proj_ref, bu_ref, pre_ref, out_ref,
             win, wout, carry, sem):
        i = pl.program_id(0)
        _load_once([(win_hbm, win), (wout_hbm, wout)], sem)

        @pl.when(i == 0)
        def _():
            carry[...] = jnp.zeros_like(carry)

        hv = h_ref[...]
        hb = _bf(hv)
        bg = _dot(hb, win[:, 0:W])
        cg = _dot(hb, win[:, W:2 * W])
        xv = _dot(hb, win[:, 2 * W:3 * W])
        proj_ref[:, 0:W] = bg
        proj_ref[:, W:2 * W] = cg
        proj_ref[:, 2 * W:3 * W] = xv
        p = cg * xv
        u = _conv(cw_ref[...], _taps_back(carry[...], p, KW))
        carry[...] = p[tm - 8:tm, :]
        bu = _bf(bg * u)
        bu_ref[...] = bu
        pre = alpha * hv + _dot(bu, wout[...])
        pre_ref[...] = pre
        out_ref[...] = _ln_fwd(pre, g_ref[...], b_ref[...], _row_ids(i, tm, D), first_row)

    row = lambda i: (i, 0)
    fix = lambda i: (0, 0)
    return pl.pallas_call(
        body, name="sc_fwd", grid=(L // tm,),
        out_shape=(S((L, 3 * W), f32), S((L, W), bf16), S((L, D), f32), S((L, D), f32)),
        in_specs=[pl.BlockSpec((tm, D), row), ANY, pl.BlockSpec((KW, W), fix), ANY,
                  pl.BlockSpec((1, D), fix), pl.BlockSpec((1, D), fix)],
        out_specs=(pl.BlockSpec((tm, 3 * W), row), pl.BlockSpec((tm, W), row),
                   pl.BlockSpec((tm, D), row), pl.BlockSpec((tm, D), row)),
        scratch_shapes=[pltpu.VMEM((D, 3 * W), w_in.dtype), pltpu.VMEM((W, D), w_out.dtype),
                        pltpu.VMEM((8, W), f32), pltpu.SemaphoreType.DMA((2,))],
        compiler_params=_cp(dimension_semantics=("arbitrary",)))(h, w_in, conv_w, w_out, g, b)


def _sc_bwd(dpre, proj, conv_w, w_out, w_in, pre_in, g_in, *, first_row):
    L, D = dpre.shape
    W = w_out.shape[0]
    KW = conv_w.shape[0]
    tm = TM
    nb = L // tm
    alpha = ALPHA

    def body(dpre_ref, proj_ref, hc_ref, hx_ref, cw_ref, wout_hbm, win_hbm, pin_ref, g_ref,
             dproj_ref, dcw_ref, dpin_ref, dg_ref, db_ref, wout, win, carry, sem):
        i = pl.program_id(0)
        blk = nb - 1 - i
        _load_once([(wout_hbm, wout), (win_hbm, win)], sem)

        @pl.when(i == 0)
        def _():
            carry[...] = jnp.zeros_like(carry)
            dcw_ref[...] = jnp.zeros_like(dcw_ref)
            dg_ref[...] = jnp.zeros_like(dg_ref)
            db_ref[...] = jnp.zeros_like(db_ref)

        bg, cg, xv = proj_ref[:, 0:W], proj_ref[:, W:2 * W], proj_ref[:, 2 * W:3 * W]
        p = cg * xv
        u = _conv(cw_ref[...], _taps_back(jnp.where(blk > 0, hc_ref[...] * hx_ref[...], 0.0), p, KW))
        dpre_v = dpre_ref[...]
        d = _dot_nt(_bf(dpre_v), wout[...])
        dproj_ref[:, 0:W] = _bf(d * u)
        du = d * bg
        ahead = _taps_ahead(du, carry[...], KW)
        carry[...] = du[0:8, :]
        dp = _conv(cw_ref[...], ahead)
        for j in range(KW):
            dcw_ref[j:j + 1, :] += jnp.sum(ahead[j] * p, axis=0, keepdims=True)
        dproj_ref[:, W:2 * W] = _bf(dp * xv)
        dproj_ref[:, 2 * W:3 * W] = _bf(dp * cg)
        dh = alpha * dpre_v + _dot_nt(dproj_ref[...], win[...])
        dpin, dg, dbias = _ln_bwd_rows(dh, pin_ref[...], g_ref[...], _row_ids(blk, tm, D), first_row)
        dpin_ref[...] = dpin
        dg_ref[0:1, :] += dg
        db_ref[0:1, :] += dbias

    rev = lambda i: (nb - 1 - i, 0)
    fix = lambda i: (0, 0)

    def halo(col):
        return pl.BlockSpec((8, W), lambda i: (jnp.maximum((nb - 1 - i) * (tm // 8) - 1, 0), col))

    return pl.pallas_call(
        body, name="sc_bwd", grid=(nb,),
        out_shape=(S((L, 3 * W), bf16), S((8, W), f32), S((L, D), f32), S((8, D), f32), S((8, D), f32)),
        in_specs=[pl.BlockSpec((tm, D), rev), pl.BlockSpec((tm, 3 * W), rev), halo(1), halo(2),
                  pl.BlockSpec((KW, W), fix), ANY, ANY, pl.BlockSpec((tm, D), rev), pl.BlockSpec((1, D), fix)],
        out_specs=(pl.BlockSpec((tm, 3 * W), rev), pl.BlockSpec((8, W), fix), pl.BlockSpec((tm, D), rev),
                   pl.BlockSpec((8, D), fix), pl.BlockSpec((8, D), fix)),
        scratch_shapes=[pltpu.VMEM((W, D), w_out.dtype), pltpu.VMEM((D, 3 * W), w_in.dtype), pltpu.VMEM((8, W), f32),
                        pltpu.SemaphoreType.DMA((2,))],
        compiler_params=_cp(dimension_semantics=("arbitrary",)))(
            dpre, proj, proj, proj, conv_w, w_out, w_in, pre_in, g_in)


def _ffn_cols(F):
    fc = F
    for cand in (1408, 1024, 512, 256, 128):
        if F % cand == 0:
            fc = cand
            break
    return fc


def _ffn_weight_copies(wup_hbm, wdn_hbm, wup, wdn, layer):
    k = wdn_hbm.shape[2]
    return [(wup_hbm.at[layer], wup)] + [(wdn_hbm.at[p, layer], wdn.at[pl.ds(p * k, k), :]) for p in range(N_DEV)]


def _ffn_fwd(h, w_up, conv_w, w_down, g, b, *, layer, first_row, name):
    L, D = h.shape
    F = N_DEV * w_down.shape[2]
    KW = conv_w.shape[0]
    tm = TM
    fc = _ffn_cols(F)
    alpha = ALPHA

    def body(h_ref, wup_hbm, cw_ref, wdn_hbm, g_ref, b_ref, up_ref, a_ref, pre_ref, out_ref,
             wup, wdn, carry, sem):
        i = pl.program_id(0)
        _load_once(_ffn_weight_copies(wup_hbm, wdn_hbm, wup, wdn, layer), sem)

        @pl.when(i == 0)
        def _():
            carry[...] = jnp.zeros_like(carry)

        hv = h_ref[...]
        hb = _bf(hv)
        pre = alpha * hv
        for c0 in range(0, F, fc):
            cs = slice(c0, c0 + fc)
            u = _dot(hb, wup[:, cs])
            gate = _dot(hb, wup[:, F + c0:F + c0 + fc])
            up_ref[:, cs] = u
            up_ref[:, F + c0:F + c0 + fc] = gate
            uc = _conv(cw_ref[:, cs], _taps_back(carry[:, cs], u, KW))
            carry[:, cs] = u[tm - 8:tm, :]
            ab = _bf(uc * _sigmoid(uc) * gate)
            a_ref[:, cs] = ab
            pre = pre + _dot(ab, wdn[cs, :])
        pre_ref[...] = pre
        out_ref[...] = _ln_fwd(pre, g_ref[...], b_ref[...], _row_ids(i, tm, D), first_row)

    row = lambda i: (i, 0)
    fix = lambda i: (0, 0)
    return pl.pallas_call(
        body, name=name, grid=(L // tm,),
        out_shape=(S((L, 2 * F), f32), S((L, F), bf16), S((L, D), f32), S((L, D), f32)),
        in_specs=[pl.BlockSpec((tm, D), row), ANY, pl.BlockSpec((KW, F), fix), ANY,
                  pl.BlockSpec((1, D), fix), pl.BlockSpec((1, D), fix)],
        out_specs=(pl.BlockSpec((tm, 2 * F), row), pl.BlockSpec((tm, F), row),
                   pl.BlockSpec((tm, D), row), pl.BlockSpec((tm, D), row)),
        scratch_shapes=[pltpu.VMEM((D, 2 * F), w_up.dtype), pltpu.VMEM((F, D), w_down.dtype),
                        pltpu.VMEM((8, F), f32), pltpu.SemaphoreType.DMA((1 + N_DEV,))],
        compiler_params=_cp(dimension_semantics=("arbitrary",)))(h, w_up, conv_w, w_down, g, b)


def _ffn_bwd(dpre, up, w_down, conv_w, w_up, pre_in, g_in, *, layer, first_row, name):
    L, D = dpre.shape
    F = N_DEV * w_down.shape[2]
    KW = conv_w.shape[0]
    tm = TM
    nb = L // tm
    fc = F
    alpha = ALPHA

    def body(dpre_ref, up_ref, halo_ref, wdn_hbm, cw_ref, wup_hbm, pin_ref, g_ref,
             dup_ref, dcw_ref, dpin_ref, dg_ref, db_ref, wdn, wup, carry, sem):
        i = pl.program_id(0)
        blk = nb - 1 - i
        _load_once(_ffn_weight_copies(wup_hbm, wdn_hbm, wup, wdn, layer), sem)

        @pl.when(i == 0)
        def _():
            carry[...] = jnp.zeros_like(carry)
            dcw_ref[...] = jnp.zeros_like(dcw_ref)
            dg_ref[...] = jnp.zeros_like(dg_ref)
            db_ref[...] = jnp.zeros_like(db_ref)

        dpre_v = dpre_ref[...]
        db = _bf(dpre_v)
        dh = alpha * dpre_v
        for c0 in range(0, F, fc):
            cs = slice(c0, c0 + fc)
            gs_ = slice(F + c0, F + c0 + fc)
            da = _dot_nt(db, wdn[cs, :])
            gate = up_ref[:, gs_]
            u = up_ref[:, cs]
            uc = _conv(cw_ref[:, cs], _taps_back(jnp.where(blk > 0, halo_ref[:, cs], 0.0), u, KW))
            sig = _sigmoid(uc)
            dgate = _bf(da * (uc * sig))
            dup_ref[:, gs_] = dgate
            duc = da * gate * (sig * (1.0 + uc * (1.0 - sig)))
            ahead = _taps_ahead(duc, carry[:, cs], KW)
            carry[:, cs] = duc[0:8, :]
            du = _bf(_conv(cw_ref[:, cs], ahead))
            dup_ref[:, cs] = du
            for j in range(KW):
                dcw_ref[j:j + 1, cs] += jnp.sum(ahead[j] * u, axis=0, keepdims=True)
            dh = dh + _dot_nt(du, wup[:, cs]) + _dot_nt(dgate, wup[:, gs_])
        dpin, dg, dbias = _ln_bwd_rows(dh, pin_ref[...], g_ref[...], _row_ids(blk, tm, D), first_row)
        dpin_ref[...] = dpin
        dg_ref[0:1, :] += dg
        db_ref[0:1, :] += dbias

    rev = lambda i: (nb - 1 - i, 0)
    fix = lambda i: (0, 0)
    return pl.pallas_call(
        body, name=name, grid=(nb,),
        out_shape=(S((L, 2 * F), bf16), S((8, F), f32), S((L, D), f32), S((8, D), f32), S((8, D), f32)),
        in_specs=[pl.BlockSpec((tm, D), rev), pl.BlockSpec((tm, 2 * F), rev),
                  pl.BlockSpec((8, F), lambda i: (jnp.maximum((nb - 1 - i) * (tm // 8) - 1, 0), 0)),
                  ANY, pl.BlockSpec((KW, F), fix), ANY, pl.BlockSpec((tm, D), rev), pl.BlockSpec((1, D), fix)],
        out_specs=(pl.BlockSpec((tm, 2 * F), rev), pl.BlockSpec((8, F), fix), pl.BlockSpec((tm, D), rev),
                   pl.BlockSpec((8, D), fix), pl.BlockSpec((8, D), fix)),
        scratch_shapes=[pltpu.VMEM((F, D), w_down.dtype), pltpu.VMEM((D, 2 * F), w_up.dtype), pltpu.VMEM((8, F), f32),
                        pltpu.SemaphoreType.DMA((1 + N_DEV,))],
        compiler_params=_cp(dimension_semantics=("arbitrary",)))(dpre, up, up, w_down, conv_w, w_up, pre_in, g_in)


def _loss_head(h, target, pre, g, *, first_row):
    L, D = h.shape
    tm = TM
    pb = PADF // tm

    def body(h_ref, t_ref, pre_ref, g_ref, dpre_ref, dg_ref, db_ref, loss_ref):
        i = pl.program_id(0)

        @pl.when(i == 0)
        def _():
            loss_ref[...] = jnp.zeros_like(loss_ref)
            dg_ref[...] = jnp.zeros_like(dg_ref)
            db_ref[...] = jnp.zeros_like(db_ref)

        valid = i >= pb
        err = h_ref[...] - t_ref[...]
        dh = jnp.where(valid, err * (1.0 / D), 0.0)
        part = 0.5 * jnp.sum(jnp.sum(err * err, axis=-1, keepdims=True) * (1.0 / D), axis=0, keepdims=True)
        loss_ref[...] += jnp.where(valid, part, 0.0)
        dpre, dg, db = _ln_bwd_rows(dh, pre_ref[...], g_ref[...], _row_ids(i, tm, D), first_row)
        dpre_ref[...] = dpre
        dg_ref[0:1, :] += dg
        db_ref[0:1, :] += db

    row = lambda i: (i, 0)
    fix = lambda i: (0, 0)
    return pl.pallas_call(
        body, name="loss_head", grid=(L // tm,),
        out_shape=(S((L, D), f32), S((8, D), f32), S((8, D), f32), S((8, LANE), f32)),
        in_specs=[pl.BlockSpec((tm, D), row), pl.BlockSpec((tm, D), lambda i: (jnp.maximum(i - pb, 0), 0)),
                  pl.BlockSpec((tm, D), row), pl.BlockSpec((1, D), fix)],
        out_specs=(pl.BlockSpec((tm, D), row), pl.BlockSpec((8, D), fix), pl.BlockSpec((8, D), fix),
                   pl.BlockSpec((8, LANE), fix)),
        compiler_params=_cp(dimension_semantics=("arbitrary",)))(h, target, pre, g)


def _adamw(g_terms, w, m, v, *, name):
    R, C = w.shape
    tr = _row_tile(R)
    n = len(g_terms)
    c1 = 1.0 - ADAM_B1 ** ADAM_STEP
    c2 = 1.0 - ADAM_B2 ** ADAM_STEP

    def body(*refs):
        g = refs[0][...].astype(f32)
        for r in refs[1:n]:
            g = g + r[...].astype(f32)
        w_ref, m_ref, v_ref, g_out, d_out, m_out, v_out = refs[n:]
        mn = ADAM_B1 * m_ref[...] + (1.0 - ADAM_B1) * g
        vn = ADAM_B2 * v_ref[...] + (1.0 - ADAM_B2) * (g * g)
        g_out[...] = g
        m_out[...] = mn
        v_out[...] = vn
        d_out[...] = -ADAM_LR * ((mn / c1) / (jnp.sqrt(vn / c2) + ADAM_EPS) + ADAM_WD * w_ref[...])

    spec = pl.BlockSpec((tr, C), lambda i: (i, 0))
    return pl.pallas_call(
        body, name=name, grid=(R // tr,), out_shape=(S((R, C), f32),) * 4,
        in_specs=[spec] * (n + 3), out_specs=(spec,) * 4,
        compiler_params=_cp(dimension_semantics=("arbitrary",)))(*g_terms, w, m, v)


def _sum_devices(x):
    n, R, C = x.shape

    def body(x_ref, o_ref):
        acc = x_ref[0]
        for d in range(1, n):
            acc = acc + x_ref[d]
        o_ref[...] = acc

    return pl.pallas_call(body, name="sum_devices", out_shape=S((R, C), f32), compiler_params=_cp())(x)


def _row_tile(R):
    for step in (16, 8):
        for t in range(256, 0, -step):
            if R % t == 0:
                return t
    return R


def _adamw_direct(s32s, recvs, w, m, v, me, *, name):
    L, K, n = w.shape
    tk = _row_tile(K)
    c1 = 1.0 - ADAM_B1 ** ADAM_STEP
    c2 = 1.0 - ADAM_B2 ** ADAM_STEP

    def body(me_ref, *refs):
        own_refs, recv_refs = refs[:L], refs[L:2 * L]
        w_ref, m_ref, v_ref, g_out, d_out, m_out, v_out = refs[2 * L:]
        for li in range(L):
            @pl.when(pl.program_id(0) == li)
            def _(li=li):
                g = own_refs[li][0, 0]
                for d in range(N_DEV):
                    g = g + recv_refs[li][d, 0].astype(f32)
                mn = ADAM_B1 * m_ref[0] + (1.0 - ADAM_B1) * g
                vn = ADAM_B2 * v_ref[0] + (1.0 - ADAM_B2) * (g * g)
                g_out[0] = g
                m_out[0] = mn
                v_out[0] = vn
                d_out[0] = -ADAM_LR * ((mn / c1) / (jnp.sqrt(vn / c2) + ADAM_EPS) + ADAM_WD * w_ref[0])

    own = pl.BlockSpec((1, tk, n), lambda l, i, ix: (l, i, 0))
    grid_spec = pltpu.PrefetchScalarGridSpec(
        num_scalar_prefetch=1, grid=(L, K // tk),
        in_specs=[pl.BlockSpec((1, 1, tk, n), lambda l, i, ix: (ix[0], 0, i, 0))] * L
        + [pl.BlockSpec((N_DEV, 1, tk, n), lambda l, i, ix: (0, 0, i, 0))] * L + [own, own, own],
        out_specs=(own,) * 4)
    return pl.pallas_call(
        body, name=name, grid_spec=grid_spec, out_shape=(S((L, K, n), f32),) * 4,
        compiler_params=_cp(dimension_semantics=("arbitrary", "arbitrary")))(me, *s32s, *recvs, w, m, v)


def _col_segments(n, mapping):
    segs = []
    for p in range(N_DEV):
        lo, hi = p * n, (p + 1) * n
        out = []
        for c0, c1, e0 in mapping:
            a, b = max(lo, c0), min(hi, c1)
            if a < b:
                out.append((a - lo, e0 + (a - c0), b - a))
        segs.append(out)
    return segs


def _assemble_cols(gathered, mapping, n_out, *, name):
    _, L, K, n = gathered.shape
    tk = _row_tile(K)
    segs = _col_segments(n, mapping)
    covered = sum(w for s in segs for (_, _, w) in s)

    def body(g_ref, o_ref):
        if covered != n_out:
            o_ref[...] = jnp.zeros_like(o_ref)
        for p in range(N_DEV):
            for s0, d0, w in segs[p]:
                o_ref[0, :, d0:d0 + w] = g_ref[p, 0, :, s0:s0 + w]

    return pl.pallas_call(
        body, name=name, grid=(L, K // tk), out_shape=S((L, K, n_out), gathered.dtype),
        in_specs=[pl.BlockSpec((N_DEV, 1, tk, n), lambda l, i: (0, l, i, 0))],
        out_specs=pl.BlockSpec((1, tk, n_out), lambda l, i: (l, i, 0)),
        compiler_params=_cp(dimension_semantics=("arbitrary", "arbitrary")))(gathered)


def _split_cols(dws, mapping, n, *, name):
    L = len(dws)
    K, n_in = dws[0].shape
    tk = _row_tile(K)
    segs = _col_segments(n, mapping)

    def body(*refs):
        ins, o32, o16 = refs[:L], refs[L], refs[L + 1]
        for li in range(L):
            @pl.when(pl.program_id(0) == li)
            def _(li=li):
                for p in range(N_DEV):
                    for s0, d0, w in segs[p]:
                        val = ins[li][:, d0:d0 + w]
                        o32[p, 0, :, s0:s0 + w] = val
                        o16[p, 0, :, s0:s0 + w] = _bf(val)

    out = pl.BlockSpec((N_DEV, 1, tk, n), lambda l, i: (0, l, i, 0))
    return pl.pallas_call(
        body, name=name, grid=(L, K // tk), out_shape=(S((N_DEV, L, K, n), f32), S((N_DEV, L, K, n), bf16)),
        in_specs=[pl.BlockSpec((tk, n_in), lambda l, i: (i, 0))] * L, out_specs=(out, out),
        compiler_params=_cp(dimension_semantics=("arbitrary", "arbitrary")))(*dws)


def _split_rows(dws, k, *, name):
    L = len(dws)
    N = dws[0].shape[1]

    def body(*refs):
        ins, o32, o16 = refs[:L], refs[L], refs[L + 1]
        for li in range(L):
            @pl.when(pl.program_id(0) == li)
            def _(li=li):
                val = ins[li][...]
                o32[0, 0] = val
                o16[0, 0] = _bf(val)

    out = pl.BlockSpec((1, 1, k, N), lambda l, p: (p, l, 0, 0))
    return pl.pallas_call(
        body, name=name, grid=(L, N_DEV), out_shape=(S((N_DEV, L, k, N), f32), S((N_DEV, L, k, N), bf16)),
        in_specs=[pl.BlockSpec((k, N), lambda l, p: (p, 0))] * L, out_specs=(out, out),
        compiler_params=_cp(dimension_semantics=("arbitrary", "arbitrary")))(*dws)


def _rows_full(gathered):
    _, L, k, N = gathered.shape
    return jnp.transpose(gathered, (1, 0, 2, 3)).reshape(L, N_DEV * k, N)


def _all_gather(xs, *, name):
    na = len(xs)

    def body(*refs):
        x_refs, out_refs = refs[:na], refs[na:2 * na]
        send_sems, recv_sems, local_sems = refs[2 * na:]
        mx, my, mc = lax.axis_index("x"), lax.axis_index("y"), lax.axis_index("c")
        me, sibling = (mx, my, mc), (mx, my, 1 - mc)
        chips = [(1 - mx, my), (mx, 1 - my), (1 - mx, 1 - my)]

        def slot(a, px, py, pc):
            return out_refs[a].at[4 * px + 2 * py + pc]

        def copy(a, kk, block, to, src=None):
            return pltpu.make_async_remote_copy(
                src_ref=slot(a, *block) if src is None else src, dst_ref=slot(a, *block),
                send_sem=send_sems.at[7 * a + kk], recv_sem=recv_sems.at[7 * a + kk], device_id=to, device_id_type=MESH)

        mine = [pltpu.make_async_copy(x_refs[a], slot(a, *me), local_sems.at[a]) for a in range(na)]
        for cp in mine:
            cp.start()
        first = []
        for a in range(na):
            first.append(copy(a, 0, me, sibling, src=x_refs[a]))
            first += [copy(a, 1 + j, me, (*chip, mc), src=x_refs[a]) for j, chip in enumerate(chips)]
        for cp in first:
            cp.start()
        passed = []
        for j, chip in enumerate(chips):
            for a in range(na):
                copy(a, 1 + j, (*chip, mc), me).wait_recv()
                fwd = copy(a, 4 + j, (*chip, mc), sibling)
                fwd.start()
                passed.append(fwd)
        for a in range(na):
            copy(a, 0, sibling, me).wait_recv()
            for j, chip in enumerate(chips):
                copy(a, 4 + j, (*chip, 1 - mc), me).wait_recv()
        for cp in first + passed:
            cp.wait_send()
        for cp in mine:
            cp.wait()

    return pl.pallas_call(
        body, name=name, out_shape=tuple(S((N_DEV,) + x.shape, x.dtype) for x in xs),
        in_specs=[ANY] * na, out_specs=(ANY,) * na,
        scratch_shapes=[pltpu.SemaphoreType.DMA((7 * na,)), pltpu.SemaphoreType.DMA((7 * na,)),
                        pltpu.SemaphoreType.DMA((na,))],
        compiler_params=pltpu.CompilerParams(has_side_effects=True))(*xs)


_FLIPS = [(fx, fy, fc) for fx in (0, 1) for fy in (0, 1) for fc in (0, 1)][1:]


def _flip_peer(flip):
    x, y, c = lax.axis_index("x"), lax.axis_index("y"), lax.axis_index("c")
    return tuple(1 - a if f else a for a, f in zip((x, y, c), flip))


def _dev_index(p):
    return 4 * p[0] + 2 * p[1] + p[2]


HBM_SPEC = pl.BlockSpec(memory_space=pltpu.HBM)
SEM_SPEC = pl.BlockSpec(memory_space=pltpu.SEMAPHORE)


def _direct_start(srcs, lands, per_peer, *, name):
    na = len(srcs)

    def body(*refs):
        src_refs, land_refs = refs[:na], refs[na:2 * na]
        send_sems, recv_sems = refs[2 * na], refs[2 * na + 1]
        token = refs[-1]
        me = _dev_index((lax.axis_index("x"), lax.axis_index("y"), lax.axis_index("c")))
        for a in range(na):
            for r, flip in enumerate(_FLIPS):
                peer = _flip_peer(flip)
                src = src_refs[a].at[_dev_index(peer)] if per_peer else src_refs[a]
                pltpu.make_async_remote_copy(
                    src_ref=src, dst_ref=land_refs[a].at[me], send_sem=send_sems.at[7 * a + r],
                    recv_sem=recv_sems.at[7 * a + r], device_id=peer, device_id_type=MESH).start()
        token[...] = jnp.zeros_like(token)

    hbm = lambda t: pltpu.with_memory_space_constraint(t, pltpu.HBM)
    out = pl.pallas_call(
        body, name=name,
        out_shape=(pltpu.SemaphoreType.DMA((7 * na,)), pltpu.SemaphoreType.DMA((7 * na,)))
        + tuple(pltpu.HBM(t.shape, t.dtype) for t in list(srcs) + list(lands)) + (S((8, LANE), f32),),
        in_specs=[HBM_SPEC] * (2 * na),
        out_specs=(SEM_SPEC, SEM_SPEC) + (HBM_SPEC,) * (2 * na) + (pl.BlockSpec(memory_space=pltpu.VMEM),),
        input_output_aliases={i: 2 + i for i in range(2 * na)},
        compiler_params=pltpu.CompilerParams(has_side_effects=pltpu.SideEffectType.DATAFLOW_SIDE_EFFECTING))(
            *[hbm(t) for t in srcs], *[hbm(t) for t in lands])
    return out[0], out[1], list(out[2:2 + na]), list(out[2 + na:2 + 2 * na]), out[-1]


def _direct_wait(send_sems, recv_sems, srcs, lands, per_peer, after, *, name):
    na = len(srcs)

    def body(*refs):
        src_refs, land_refs = refs[:na], refs[na:2 * na]
        ssem, rsem = refs[2 * na], refs[2 * na + 1]
        me = _dev_index((lax.axis_index("x"), lax.axis_index("y"), lax.axis_index("c")))
        for a in range(na):
            for r, flip in enumerate(_FLIPS):
                peer = _flip_peer(flip)
                src = src_refs[a].at[_dev_index(peer)] if per_peer else src_refs[a]
                cp = pltpu.make_async_remote_copy(
                    src_ref=src, dst_ref=land_refs[a].at[me], send_sem=ssem.at[7 * a + r],
                    recv_sem=rsem.at[7 * a + r], device_id=peer, device_id_type=MESH)
                cp.wait_send()
                cp.wait_recv()

    out = pl.pallas_call(
        body, name=name, out_shape=tuple(pltpu.HBM(t.shape, t.dtype) for t in list(srcs) + list(lands)),
        in_specs=[HBM_SPEC] * (2 * na) + [SEM_SPEC, SEM_SPEC, ANY], out_specs=(HBM_SPEC,) * (2 * na),
        input_output_aliases={i: i for i in range(2 * na)},
        compiler_params=pltpu.CompilerParams(has_side_effects=pltpu.SideEffectType.DATAFLOW_SIDE_EFFECTING))(
            *srcs, *lands, send_sems, recv_sems, after)
    return list(out[:na]), list(out[na:])


def _pack_small(parts, width):
    rows, offs, r = [], [], 0
    for a in parts:
        n = a.size
        nr = -(-n // width)
        flat = a.reshape(-1).astype(f32)
        if nr * width != n:
            flat = jnp.pad(flat, (0, nr * width - n))
        rows.append(flat.reshape(nr, width))
        offs.append((r, nr))
        r += nr
    buf = jnp.concatenate(rows, axis=0)
    pad = (-r) % 8
    if pad:
        buf = jnp.pad(buf, ((0, pad), (0, 0)))
    return buf, offs


def _unpack_small(buf, off, shape):
    r, nr = off
    return buf[r:r + nr].reshape(-1)[:math.prod(shape)].reshape(shape)


def _local_step(x, target, meta, a_w_in, a_w_out, small, start_token, late_weights, grads_ready):
    SEQ, D = x.shape
    n_meta = meta.shape[0]
    first_row = PADF - n_meta
    H = small["a_log"].shape[-1]

    head = jnp.concatenate([jnp.zeros((first_row, D), f32), meta], axis=0)

    def lanes(a):
        return jnp.pad(a.reshape(1, -1), ((0, 0), (0, LANE - a.size)))

    def after_token(a, token):
        return a if token is None else a + token[0:1, 0:1]

    alog, dtb = after_token(lanes(small["a_log"][0]), start_token), lanes(small["a_dt_bias"][0])
    a_conv, b_conv = small["a_conv"][0], small["b_conv"][0]
    nw = small["a_norm"][0].reshape(1, DH)
    lmg, lmb, lfg, lfb = small["ln_mix_g"], small["ln_mix_b"], small["ln_ffn_g"], small["ln_ffn_b"]

    h0, pre_a, z, raw, q, k, v, beta, g, t_all = _gdn_in_fwd(x, head, a_w_in, a_conv, alog, dtb,
                                                             first_row=first_row, H=H)
    o, y, s_all, pre1, h1 = _delta_fwd(q, k, v, g, beta, t_all, z, nw, h0, a_w_out, lmg[0:1], lmb[0:1],
                                       first_row=first_row, H=H)
    wts = late_weights(h1)
    up0, act0, pre2, h2 = _ffn_fwd(h1, wts["ffn_w_up"], small["ffn_conv"][0], wts["ffn_w_down"],
                                   lfg[0:1], lfb[0:1], layer=0, first_row=first_row, name="ffn_fwd0")
    proj_b, bu, pre3, h3 = _sc_fwd(h2, wts["b_w_in"], b_conv, wts["b_w_out"], lmg[1:2], lmb[1:2], first_row=first_row)
    up1, act1, pre4, h4 = _ffn_fwd(h3, wts["ffn_w_up"], small["ffn_conv"][1], wts["ffn_w_down"],
                                   lfg[1:2], lfb[1:2], layer=1, first_row=first_row, name="ffn_fwd1")
    gs = {}
    dpre4, dlfg1, dlfb1, loss_tile = _loss_head(h4, target, pre4, lfg[1:2], first_row=first_row)

    def ffn_backward(dpre, up, act, h_in, layer, tag, ln_in, token=None):
        dup, dcw, dpre_in, dg, db = _ffn_bwd(
            dpre, up, wts["ffn_w_down"], after_token(small["ffn_conv"][layer], token),
            wts["ffn_w_up"], ln_in[0], ln_in[1], layer=layer, first_row=first_row, name="ffn_bwd" + tag)
        dwd = _linear_dw(act, dpre, name="dw_down" + tag)
        dwu = _linear_dw(h_in, dup, name="dw_up" + tag)
        return dpre_in, dg, db, dwu, dwd, dcw[0:3]

    dpre3, dlmg1, dlmb1, dwu1, dwd1, dcf1 = ffn_backward(dpre4, up1, act1, h3, 1, "1", (pre3, lmg[1:2]))

    dproj_b, dcb, dpre2, dlfg0, dlfb0 = _sc_bwd(dpre3, proj_b, b_conv, wts["b_w_out"], wts["b_w_in"], pre2, lfg[0:1],
                                                first_row=first_row)
    dwb_in = _linear_dw(h2, dproj_b, name="dw_b_in")
    token = grads_ready("layer1", dict(ffn_w_up=dwu1, ffn_w_down=dwd1, b_w_in=dwb_in))

    dpre1, dlmg0, dlmb0, dwu0, dwd0, dcf0 = ffn_backward(dpre2, up0, act0, h1, 0, "0", (pre1, lmg[0:1]), token)
    token = grads_ready("layer0", dict(ffn_w_up=dwu0, ffn_w_down=dwd0))

    dq, dk, dv, dz, dg_, dbeta, dnw = _delta_bwd(dpre1, a_w_out, o, z, after_token(nw, token), q, k, v, g, beta,
                                                 s_all, t_all, H=H)
    dproj_a, dca, dal, ddt, grad_x, dhead = _gdn_in_bwd(dq, dk, dv, dz, dg_, dbeta, pre_a, raw, a_conv, alog, dtb,
                                                        a_w_in, dpre1, first_row=first_row, H=H)
    token = grads_ready("last", dict(a_w_in=_linear_dw(h0, dproj_a, name="dw_a_in")))
    grads_ready("tail", dict(a_w_out=_linear_dw(y, dpre1, name="dw_a_out", after=token),
                             b_w_out=_linear_dw(bu, dpre3, name="dw_b_out", after=token)))

    gs["meta"] = dhead[first_row:PADF]
    gs["a_conv"] = dca[0:a_conv.shape[0]][None]
    gs["a_log"] = dal[0:1, 0:H]
    gs["a_dt_bias"] = ddt[0:1, 0:H]
    gs["a_norm"] = dnw[0:1]
    gs["b_conv"] = dcb[0:b_conv.shape[0]][None]
    gs["ln_mix_g"] = jnp.stack([dlmg0[0], dlmg1[0]])
    gs["ln_mix_b"] = jnp.stack([dlmb0[0], dlmb1[0]])
    gs["ffn_conv"] = jnp.stack([dcf0, dcf1])
    gs["ln_ffn_g"] = jnp.stack([dlfg0[0], dlfg1[0]])
    gs["ln_ffn_b"] = jnp.stack([dlfb0[0], dlfb1[0]])
    return loss_tile, grad_x, gs


_BIG = ("a_w_in", "a_w_out", "b_w_in", "b_w_out", "ffn_w_up", "ffn_w_down")
_BIG_COL = ("a_w_in", "b_w_in", "ffn_w_up")
_SMALL = ("meta", "a_conv", "a_log", "a_dt_bias", "a_norm", "b_conv", "ln_mix_g", "ln_mix_b",
          "ffn_conv", "ln_ffn_g", "ln_ffn_b")
_SMALL_SHARDED = ("meta", "a_conv", "b_conv", "ffn_conv")
_ORDER = ("meta", "a_w_in", "a_conv", "a_log", "a_dt_bias", "a_norm", "a_w_out", "b_w_in", "b_conv", "b_w_out",
          "ln_mix_g", "ln_mix_b", "ffn_w_up", "ffn_conv", "ffn_w_down", "ln_ffn_g", "ln_ffn_b")


def _a_w_in_map(H):
    W4 = 4 * H * DH
    return [(0, W4, 0), (W4, W4 + H, W4), (W4 + H, W4 + 2 * H, W4 + LANE)], W4 + 2 * LANE


def kernel(x, meta, a_w_in, a_conv, a_log, a_dt_bias, a_norm, a_w_out, b_w_in, b_conv, b_w_out, ln_mix_g, ln_mix_b, ffn_w_up, ffn_conv, ffn_w_down, ln_ffn_g, ln_ffn_b, loss_target, m_meta, m_a_w_in, m_a_conv, m_a_log, m_a_dt_bias, m_a_norm, m_a_w_out, m_b_w_in, m_b_conv, m_b_w_out, m_ln_mix_g, m_ln_mix_b, m_ffn_w_up, m_ffn_conv, m_ffn_w_down, m_ln_ffn_g, m_ln_ffn_b, v_meta, v_a_w_in, v_a_conv, v_a_log, v_a_dt_bias, v_a_norm, v_a_w_out, v_b_w_in, v_b_conv, v_b_w_out, v_ln_mix_g, v_ln_mix_b, v_ffn_w_up, v_ffn_conv, v_ffn_w_down, v_ln_ffn_g, v_ln_ffn_b):
    wloc = dict(meta=meta, a_w_in=a_w_in, a_conv=a_conv, a_log=a_log, a_dt_bias=a_dt_bias, a_norm=a_norm,
                a_w_out=a_w_out, b_w_in=b_w_in, b_conv=b_conv, b_w_out=b_w_out, ln_mix_g=ln_mix_g, ln_mix_b=ln_mix_b,
                ffn_w_up=ffn_w_up, ffn_conv=ffn_conv, ffn_w_down=ffn_w_down, ln_ffn_g=ln_ffn_g, ln_ffn_b=ln_ffn_b)
    mloc = dict(meta=m_meta, a_w_in=m_a_w_in, a_conv=m_a_conv, a_log=m_a_log, a_dt_bias=m_a_dt_bias, a_norm=m_a_norm,
                a_w_out=m_a_w_out, b_w_in=m_b_w_in, b_conv=m_b_conv, b_w_out=m_b_w_out, ln_mix_g=m_ln_mix_g,
                ln_mix_b=m_ln_mix_b, ffn_w_up=m_ffn_w_up, ffn_conv=m_ffn_conv, ffn_w_down=m_ffn_w_down,
                ln_ffn_g=m_ln_ffn_g, ln_ffn_b=m_ln_ffn_b)
    vloc = dict(meta=v_meta, a_w_in=v_a_w_in, a_conv=v_a_conv, a_log=v_a_log, a_dt_bias=v_a_dt_bias, a_norm=v_a_norm,
                a_w_out=v_a_w_out, b_w_in=v_b_w_in, b_conv=v_b_conv, b_w_out=v_b_w_out, ln_mix_g=v_ln_mix_g,
                ln_mix_b=v_ln_mix_b, ffn_w_up=v_ffn_w_up, ffn_conv=v_ffn_conv, ffn_w_down=v_ffn_w_down,
                ln_ffn_g=v_ln_ffn_g, ln_ffn_b=v_ln_ffn_b)
    H = a_log.shape[-1]
    mx, my, mc = lax.axis_index("x"), lax.axis_index("y"), lax.axis_index("c")
    me = 4 * mx + 2 * my + mc

    a_map, a_cols = _a_w_in_map(H)
    col_maps = {"a_w_in": (a_map, a_cols)}
    for n in ("b_w_in", "ffn_w_up"):
        ncols = N_DEV * wloc[n].shape[-1]
        col_maps[n] = ([(0, ncols, 0)], ncols)
    sm_sh = [wloc[n] for n in _SMALL_SHARDED]
    sbuf, soffs = _pack_small(sm_sh, 128)
    g_a_w_in, g_a_w_out, sg = _all_gather([_bf(wloc["a_w_in"]), _bf(wloc["a_w_out"]), sbuf], name="gather_first")
    w_a_in = _assemble_cols(g_a_w_in, *col_maps["a_w_in"], name="assemble_a_w_in")[0]
    w_a_out = _rows_full(g_a_w_out)[0]
    late = [n for n in _BIG if n not in ("a_w_in", "a_w_out")]
    ssem, rsem, srcs_t, lands_t, start_token = _direct_start(
        [_bf(wloc[n]) for n in late], [lax.empty((N_DEV,) + wloc[n].shape, bf16) for n in late], False,
        name="gather_rest_start")

    def late_weights(after):
        srcs_d, landed = _direct_wait(ssem, rsem, srcs_t, lands_t, False, after, name="gather_rest_wait")
        wts = {}
        for n, own, got in zip(late, srcs_d, landed):
            full = lax.dynamic_update_index_in_dim(got, own, me, 0)
            if n in _BIG_COL:
                wts[n] = _assemble_cols(full, *col_maps[n], name="assemble_" + n)
            elif n == "ffn_w_down":
                wts[n] = full
            else:
                wts[n] = _rows_full(full)
        for n in ("b_w_in", "b_w_out"):
            wts[n] = wts[n][0]
        return wts

    small = {n: wloc[n] for n in _SMALL}
    for n, off in zip(_SMALL_SHARDED, soffs):
        sh = wloc[n].shape
        parts = jnp.stack([_unpack_small(sg[d], off, sh) for d in range(N_DEV)])
        nd = len(sh)
        small[n] = jnp.transpose(parts, tuple(range(1, nd)) + (0, nd)).reshape(sh[:-1] + (N_DEV * sh[-1],))

    def split(n, dws, tag):
        if n in _BIG_COL:
            return _split_cols(dws, col_maps[n][0], wloc[n].shape[-1], name="split_" + n + tag)
        return _split_rows(dws, wloc[n].shape[-2], name="split_" + n + tag)

    sent = {}

    def grads_ready(stage, grads):
        names = sorted(grads)
        parts = [split(n, [grads[n]], "_" + stage) for n in names]
        handles = _direct_start([p[1] for p in parts], [jnp.zeros(p[1].shape, bf16) for p in parts], True,
                                name="grads_" + stage + "_start")
        sent[stage] = (names, [p[0] for p in parts], handles)
        return handles[4]

    loss_tile, grad_x, gs = _local_step(x[0], loss_target[0], small["meta"], w_a_in, w_a_out, small, start_token,
                                        late_weights, grads_ready)

    def landed(stage, after):
        names, own32, (ssem_g, rsem_g, srcs_g, lands_g, _) = sent[stage]
        _, got = _direct_wait(ssem_g, rsem_g, srcs_g, lands_g, True, after, name="grads_" + stage + "_wait")
        return list(zip(names, own32, got))

    parts = {}
    for stage in ("layer0", "layer1"):
        for n, o32, r in landed(stage, grad_x):
            parts.setdefault(n, []).append((o32, r))
    me1 = jnp.stack([me]).astype(jnp.int32)
    big_out = {n: _adamw_direct([p[0] for p in ps], [p[1] for p in ps], wloc[n], mloc[n], vloc[n], me1,
                                name="adamw_" + n) for n, ps in parts.items()}
    names = list(_SMALL)
    pbuf, poffs = _pack_small([gs[n] for n in names] + [loss_tile[0:1, 0:1]], 1024)
    psum = _sum_devices(_all_gather([pbuf], name="gather_small_grads")[0])
    loss = psum[poffs[-1][0], 0]
    g_small = {}
    for n, off in zip(names, poffs[:-1]):
        full_shape = gs[n].shape
        gfull = _unpack_small(psum, off, full_shape)
        if n in _SMALL_SHARDED:
            ns = wloc[n].shape[-1]
            gfull = lax.dynamic_slice_in_dim(gfull, me * ns, ns, axis=gfull.ndim - 1)
        g_small[n] = gfull.reshape(wloc[n].shape)
    gbuf, aoffs = _pack_small([g_small[n] for n in names], 128)
    wbuf, _ = _pack_small([wloc[n] for n in names], 128)
    mbuf, _ = _pack_small([mloc[n] for n in names], 128)
    vbuf, _ = _pack_small([vloc[n] for n in names], 128)
    _, d_s, m_s, v_s = _adamw([gbuf], wbuf, mbuf, vbuf, name="adamw_small")

    done = d_s[0, 0]
    for out in big_out.values():
        done = done + out[1][0, 0, 0]
    for stage in ("last", "tail"):
        for n, o32, r in landed(stage, done.reshape(1, 1)):
            big_out[n] = _adamw_direct([o32], [r], wloc[n], mloc[n], vloc[n], me1, name="adamw_" + n)

    grads, deltas, new_m, new_v = {}, {}, {}, {}
    for n in _BIG:
        grads[n], deltas[n], new_m[n], new_v[n] = big_out[n]
    for n, off in zip(names, aoffs):
        sh = wloc[n].shape
        grads[n] = g_small[n]
        deltas[n], new_m[n], new_v[n] = (_unpack_small(b_, off, sh) for b_ in (d_s, m_s, v_s))
    return (loss, grad_x[None], *[grads[n] for n in _ORDER], *[deltas[n] for n in _ORDER],
            *[new_m[n] for n in _ORDER], *[new_v[n] for n in _ORDER])
```

```python
import math

import jax
import jax.numpy as jnp
from jax import lax
from jax.experimental import pallas as pl
from jax.experimental.pallas import tpu as pltpu

f32, bf16 = jnp.float32, jnp.bfloat16
S = jax.ShapeDtypeStruct
HI = lax.Precision.HIGHEST
MESH = pl.DeviceIdType.MESH

V7X_VMEM_LIMIT = 56 * 1024 * 1024
LANE = 128
DH = 128
CH = 64
PADF = 256
TM = 256
TMM = 768
N_DEV = 8
BWD_HEAD_GROUP = 4
FFN_BWD_COLS = 256
BWD_GROUP_LAG = 2

DEPTH = 2
ALPHA = (2.0 * DEPTH) ** 0.25
LN_EPS = 1e-5
RMS_EPS = 1e-6
L2_EPS = 1e-6
ADAM_LR, ADAM_B1, ADAM_B2, ADAM_EPS, ADAM_WD, ADAM_STEP = 0.001, 0.9, 0.999, 1e-08, 0.01, 10


def _cp(**kw):
    return pltpu.CompilerParams(vmem_limit_bytes=V7X_VMEM_LIMIT, **kw)


def _bf(x):
    return x.astype(bf16)


def _dot(a, b, precision=None):
    return jnp.dot(a, b, preferred_element_type=f32, precision=precision)


def _dot_nt(a, b):
    return lax.dot_general(a, b, (((1,), (1,)), ((), ())), preferred_element_type=f32)


def _dot_tn(a, b):
    return lax.dot_general(a, b, (((0,), (0,)), ((), ())), preferred_element_type=f32)


def _sigmoid(x):
    return 1.0 / (1.0 + jnp.exp(-x))


def _load_once(pairs, sem):
    @pl.when(pl.program_id(0) == 0)
    def _():
        cps = [pltpu.make_async_copy(src, dst, sem.at[n]) for n, (src, dst) in enumerate(pairs)]
        for c in cps:
            c.start()
        for c in cps:
            c.wait()


def _row_ids(i, tm, width):
    return i * tm + lax.broadcasted_iota(jnp.int32, (tm, width), 0)


def _ln_fwd(pre, g, b, rows, first_row):
    mu = jnp.mean(pre, axis=-1, keepdims=True)
    xc = pre - mu
    var = jnp.mean(xc * xc, axis=-1, keepdims=True)
    y = xc * lax.rsqrt(var + LN_EPS) * g + b
    return jnp.where(rows >= first_row, y, 0.0)


ANY = pl.BlockSpec(memory_space=pl.ANY)


def _taps_back(prev8, x, kw):
    xe = jnp.concatenate([prev8, x], axis=0)
    return [pltpu.roll(xe, kw - 1 - j, 0)[8:] for j in range(kw - 1)] + [x]


def _taps_ahead(x, next8, kw):
    n = x.shape[0]
    xe = jnp.concatenate([x, next8], axis=0)
    return [pltpu.roll(xe, n + 8 - (kw - 1 - j), 0)[:n] for j in range(kw - 1)] + [x]


def _conv(cw, taps):
    acc = cw[0:1, :] * taps[0]
    for j in range(1, len(taps)):
        acc = acc + cw[j:j + 1, :] * taps[j]
    return acc


def _linear_dw(x, dy, *, name, after=None):
    L, K = x.shape
    N = dy.shape[1]
    tm = TMM if L % TMM == 0 else TM
    tn = LANE
    for d in range(N // LANE, 0, -1):
        if (N // LANE) % d == 0 and K * d * LANE * 4 <= 9 * 1024 * 1024:
            tn = d * LANE
            break

    def body(x_ref, dy_ref, *rest):
        o_ref = rest[-1]

        @pl.when(pl.program_id(1) == 0)
        def _():
            o_ref[...] = jnp.zeros_like(o_ref)
        o_ref[...] += _dot_tn(_bf(x_ref[...]), _bf(dy_ref[...]))

    in_specs = [pl.BlockSpec((tm, K), lambda j, i: (i, 0)), pl.BlockSpec((tm, tn), lambda j, i: (i, j))]
    args = [x, dy]
    if after is not None:
        in_specs.append(pl.BlockSpec(after.shape, lambda j, i: (0, 0)))
        args.append(after)
    return pl.pallas_call(
        body, name=name, grid=(N // tn, L // tm), out_shape=S((K, N), f32),
        in_specs=in_specs, out_specs=pl.BlockSpec((K, tn), lambda j, i: (0, j)),
        compiler_params=_cp(dimension_semantics=("arbitrary", "arbitrary")))(*args)


def _ln_bwd_rows(dout, pre, g, rows, first_row):
    mu = jnp.mean(pre, axis=-1, keepdims=True)
    xc = pre - mu
    rstd = lax.rsqrt(jnp.mean(xc * xc, axis=-1, keepdims=True) + LN_EPS)
    xh = xc * rstd
    dy = jnp.where(rows >= first_row, dout, 0.0)
    dxh = dy * g
    dpre = rstd * (dxh - jnp.mean(dxh, axis=-1, keepdims=True) - xh * jnp.mean(dxh * xh, axis=-1, keepdims=True))
    return dpre, jnp.sum(dy * xh, axis=0, keepdims=True), jnp.sum(dy, axis=0, keepdims=True)


def _gdn_in_fwd(x, head, w_full, conv_w, alog, dtb, *, first_row, H):
    D = x.shape[1]
    L = PADF + x.shape[0]
    W = H * DH
    NW = w_full.shape[1]
    KW = conv_w.shape[0]
    tm = TM
    pb = PADF // tm

    def body(x_ref, head_ref, w_hbm, cw_ref, alog_ref, dtb_ref,
             h_ref, pre_ref, z_ref, raw_ref, q_ref, k_ref, v_ref, beta_ref, g_ref, t_ref,
             w_vmem, carry, sem):
        i = pl.program_id(0)
        _load_once([(w_hbm, w_vmem)], sem)

        @pl.when(i == 0)
        def _():
            carry[...] = jnp.zeros_like(carry)

        hv = jnp.where(i < pb, head_ref[...], x_ref[...])
        h_ref[...] = hv
        hb = _bf(hv)
        outs = (q_ref, k_ref, v_ref)

        def section(s):
            pre = _dot(hb, w_vmem[:, s * W:(s + 1) * W])
            pre_ref[:, s * W:(s + 1) * W] = pre
            c = _conv(cw_ref[:, s * W:(s + 1) * W], _taps_back(carry[s], pre, KW))
            carry[s] = pre[tm - 8:tm, :]
            sl = c * _sigmoid(c)
            if s < 2:
                scale = DH ** -0.5 if s == 0 else 1.0
                for hh in range(H):
                    seg = sl[:, hh * DH:(hh + 1) * DH]
                    r = lax.rsqrt(jnp.sum(seg * seg, axis=-1, keepdims=True) + L2_EPS)
                    outs[s][:, hh * DH:(hh + 1) * DH] = seg * (r * scale)
            else:
                v_ref[...] = sl

        raw = _dot(hb, w_vmem[:, 4 * W:4 * W + 2 * LANE])
        raw_ref[...] = raw
        ok = (_row_ids(i, tm, LANE) >= first_row) & (lax.broadcasted_iota(jnp.int32, (tm, LANE), 1) < H)
        beta = jnp.where(ok, _sigmoid(raw[:, :LANE]), 0.0)
        beta_ref[...] = beta
        a = raw[:, LANE:] + dtb_ref[...]
        sp = jnp.maximum(a, 0.0) + jnp.log(1.0 + jnp.exp(-jnp.abs(a)))
        gv = jnp.where(ok, -jnp.exp(alog_ref[...]) * sp, 0.0)
        gam = _dot(_chunk_tri(tm, lower=True), gv, HI)
        g_ref[...] = gam
        section(1)
        ii = lax.broadcasted_iota(jnp.int32, (CH, CH), 0)
        jj = lax.broadcasted_iota(jnp.int32, (CH, CH), 1)
        eye = (ii == jj).astype(f32)
        gam_t = gam.T

        def inverses(chunks):
            ms = []
            for c in chunks:
                rows = slice(c * CH, (c + 1) * CH)
                for hh in range(H):
                    kh = k_ref[rows, hh * DH:(hh + 1) * DH]
                    dec = jnp.exp(jnp.minimum(gam[rows, hh:hh + 1] - gam_t[hh:hh + 1, rows], 0.0))
                    kk = _dot_nt(_bf(kh * beta[rows, hh:hh + 1]), _bf(kh))
                    ms.append(jnp.where(ii > jj, kk * dec, 0.0))
            for n, t in enumerate(_tri_inv_many(ms, eye)):
                t_ref[chunks[n // H], n % H] = t

        nch = tm // CH
        inverses(list(range(nch // 2)))
        section(0)
        inverses(list(range(nch // 2, nch)))
        section(2)
        z_ref[...] = _dot(hb, w_vmem[:, 3 * W:4 * W])

    row = lambda i: (i, 0)
    fix = lambda i: (0, 0)
    out_shape = (S((L, D), f32), S((L, 3 * W), f32), S((L, W), f32), S((L, 2 * LANE), f32),
                 S((L, W), f32), S((L, W), f32), S((L, W), f32), S((L, LANE), f32), S((L, LANE), f32),
                 S((L // CH, H, CH, CH), f32))
    out_specs = (pl.BlockSpec((tm, D), row),
                 pl.BlockSpec((tm, 3 * W), row), pl.BlockSpec((tm, W), row), pl.BlockSpec((tm, 2 * LANE), row),
                 pl.BlockSpec((tm, W), row), pl.BlockSpec((tm, W), row), pl.BlockSpec((tm, W), row),
                 pl.BlockSpec((tm, LANE), row), pl.BlockSpec((tm, LANE), row),
                 pl.BlockSpec((tm // CH, H, CH, CH), lambda i: (i, 0, 0, 0)))
    return pl.pallas_call(
        body, name="gdn_in_fwd", grid=(L // tm,), out_shape=out_shape,
        in_specs=[pl.BlockSpec((tm, D), lambda i: (jnp.maximum(i - pb, 0), 0)),
                  pl.BlockSpec((tm, D), lambda i: (jnp.minimum(i, pb - 1), 0)), ANY, pl.BlockSpec((KW, 3 * W), fix),
                  pl.BlockSpec((1, LANE), fix), pl.BlockSpec((1, LANE), fix)],
        out_specs=out_specs,
        scratch_shapes=[pltpu.VMEM((D, NW), w_full.dtype), pltpu.VMEM((3, 8, W), f32), pltpu.SemaphoreType.DMA((1,))],
        compiler_params=_cp(dimension_semantics=("arbitrary",)))(x, head, w_full, conv_w, alog, dtb)


def _gdn_in_bwd(dq, dk, dv, dz, dg, dbeta, pre, raw, conv_w, alog, dtb, w_full, res, *, first_row, H):
    L = dq.shape[0]
    D = res.shape[1]
    W = H * DH
    KW = conv_w.shape[0]
    tm = TM
    nb = L // tm
    NW = 4 * W + 2 * LANE
    fb = PADF // tm
    alpha = ALPHA

    def body(dq_ref, dk_ref, dv_ref, dz_ref, dg_ref, dbeta_ref, pre_ref, hq_ref, hk_ref, hv_ref, raw_ref,
             cw_ref, alog_ref, dtb_ref, w_hbm, res_ref,
             dproj_ref, dcw_ref, dal_ref, ddt_ref, dx_ref, dfront_ref, w_vmem, carry, tmp, sem):
        i = pl.program_id(0)
        blk = nb - 1 - i
        _load_once([(w_hbm, w_vmem)], sem)

        @pl.when(i == 0)
        def _():
            carry[...] = jnp.zeros_like(carry)
            dcw_ref[...] = jnp.zeros_like(dcw_ref)
            dal_ref[...] = jnp.zeros_like(dal_ref)
            ddt_ref[...] = jnp.zeros_like(ddt_ref)

        halos = (hq_ref, hk_ref, hv_ref)
        douts = (dq_ref, dk_ref, dv_ref)
        for s in range(3):
            sec = slice(s * W, (s + 1) * W)
            pre = pre_ref[:, sec]
            c = _conv(cw_ref[:, sec], _taps_back(jnp.where(blk > 0, halos[s][...], 0.0), pre, KW))
            sig = _sigmoid(c)
            sl = c * sig
            if s < 2:
                scale = DH ** -0.5 if s == 0 else 1.0
                for hh in range(H):
                    hs = slice(hh * DH, (hh + 1) * DH)
                    seg = sl[:, hs]
                    r = lax.rsqrt(jnp.sum(seg * seg, axis=-1, keepdims=True) + L2_EPS)
                    n = seg * r
                    dqs = douts[s][:, hs]
                    tmp[:, hs] = (scale * r) * (dqs - n * jnp.sum(n * dqs, axis=-1, keepdims=True))
                dsl = tmp[...]
            else:
                dsl = dv_ref[...]
            dc = dsl * (sig * (1.0 + c * (1.0 - sig)))
            ahead = _taps_ahead(dc, carry[s], KW)
            carry[s] = dc[0:8, :]
            dproj_ref[:, sec] = _bf(_conv(cw_ref[:, sec], ahead))
            for j in range(KW):
                dcw_ref[j:j + 1, sec] += jnp.sum(ahead[j] * pre, axis=0, keepdims=True)
        dproj_ref[:, 3 * W:4 * W] = _bf(dz_ref[...])
        raw_v = raw_ref[...]
        ok = (_row_ids(blk, tm, LANE) >= first_row) & (lax.broadcasted_iota(jnp.int32, (tm, LANE), 1) < H)
        beta = _sigmoid(raw_v[:, :LANE])
        dbraw = jnp.where(ok, dbeta_ref[...] * beta * (1.0 - beta), 0.0)
        a = raw_v[:, LANE:] + dtb_ref[...]
        sp = jnp.maximum(a, 0.0) + jnp.log(1.0 + jnp.exp(-jnp.abs(a)))
        nea = -jnp.exp(alog_ref[...])
        dgm = jnp.where(ok, _dot(_chunk_tri(tm, lower=False), dg_ref[...], HI), 0.0)
        daraw = dgm * nea * _sigmoid(a)
        dal_ref[0:1, :] += jnp.sum(dgm * nea * sp, axis=0, keepdims=True)
        ddt_ref[0:1, :] += jnp.sum(daraw, axis=0, keepdims=True)
        dproj_ref[:, 4 * W:4 * W + LANE] = _bf(dbraw)
        dproj_ref[:, 4 * W + LANE:4 * W + 2 * LANE] = _bf(daraw)
        dh = alpha * res_ref[...] + _dot_nt(dproj_ref[...], w_vmem[...])

        @pl.when(blk >= fb)
        def _():
            dx_ref[...] = dh

        @pl.when(blk < fb)
        def _():
            dfront_ref[...] = dh

    rev = lambda i: (nb - 1 - i, 0)
    fix = lambda i: (0, 0)

    def halo(col):
        return pl.BlockSpec((8, W), lambda i: (jnp.maximum((nb - 1 - i) * (tm // 8) - 1, 0), col))

    return pl.pallas_call(
        body, name="gdn_in_bwd", grid=(nb,),
        out_shape=(S((L, NW), bf16), S((8, 3 * W), f32), S((8, LANE), f32), S((8, LANE), f32),
                   S((L - PADF, D), f32), S((PADF, D), f32)),
        in_specs=[pl.BlockSpec((tm, W), rev)] * 4 + [pl.BlockSpec((tm, LANE), rev)] * 2
        + [pl.BlockSpec((tm, 3 * W), rev), halo(0), halo(1), halo(2), pl.BlockSpec((tm, 2 * LANE), rev),
           pl.BlockSpec((KW, 3 * W), fix), pl.BlockSpec((1, LANE), fix), pl.BlockSpec((1, LANE), fix),
           ANY, pl.BlockSpec((tm, D), rev)],
        out_specs=(pl.BlockSpec((tm, NW), rev), pl.BlockSpec((8, 3 * W), fix),
                   pl.BlockSpec((8, LANE), fix), pl.BlockSpec((8, LANE), fix),
                   pl.BlockSpec((tm, D), lambda i: (jnp.maximum(nb - 1 - i - fb, 0), 0)),
                   pl.BlockSpec((tm, D), lambda i: (jnp.minimum(nb - 1 - i, fb - 1), 0))),
        scratch_shapes=[pltpu.VMEM((D, NW), w_full.dtype), pltpu.VMEM((3, 8, W), f32), pltpu.VMEM((tm, W), f32),
                        pltpu.SemaphoreType.DMA((1,))],
        compiler_params=_cp(dimension_semantics=("arbitrary",)))(
            dq, dk, dv, dz, dg, dbeta, pre, pre, pre, pre, raw, conv_w, alog, dtb, w_full, res)


def _chunk_tri(n, lower):
    i = lax.broadcasted_iota(jnp.int32, (n, n), 0)
    j = lax.broadcasted_iota(jnp.int32, (n, n), 1)
    sh = int(math.log2(CH))
    same = lax.shift_right_logical(i, sh) == lax.shift_right_logical(j, sh)
    return (same & ((i >= j) if lower else (j >= i))).astype(f32)


def _tri_inv_many(ms, eye):
    ts = [eye - m for m in ms]
    ps = list(ms)
    for _ in range(int(math.log2(CH)) - 1):
        pb = [_bf(p) for p in ps]
        ps = [_dot(p, p) for p in pb]
        ts = [t + _dot(_bf(t), _bf(p)) for t, p in zip(ts, ps)]
    return ts


def _chunk_local(q, k, v, gcol, grow, glast, bcol, ii, jj):
    dec = jnp.where(ii >= jj, jnp.exp(jnp.minimum(gcol - grow, 0.0)), 0.0)
    eg = jnp.exp(gcol)
    kb = k * bcol
    kbg = kb * eg
    vb = v * bcol
    qt = q * eg
    kt = k * jnp.exp(glast - gcol)
    kbb, qb, kbf = _bf(kb), _bf(q), _bf(k)
    return dec, eg, kb, kbg, vb, qt, kt, _dot_nt(kbb, kbf), _dot_nt(qb, kbf), jnp.concatenate([kbb, qb], axis=0)


def _delta_fwd(q, k, v, g, beta, t_all, z, nw, h, w_out, ln_g, ln_b, *, first_row, H):
    L = q.shape[0]
    W = H * DH
    D = h.shape[1]
    rb = TM
    nc = rb // CH
    nblk = L // rb
    alpha = ALPHA

    def body(q_ref, k_ref, v_ref, g_ref, b_ref, t_ref, z_ref, nw_ref, h_ref, wout_hbm, lg_ref, lb_ref,
             o_ref, y_ref, s_out, pre_ref, out_ref, s_scr, wout, sem):
        _load_once([(wout_hbm, wout)], sem)

        @pl.when(pl.program_id(0) == 0)
        def _():
            s_scr[...] = jnp.zeros_like(s_scr)

        ii = lax.broadcasted_iota(jnp.int32, (CH, CH), 0)
        jj = lax.broadcasted_iota(jnp.int32, (CH, CH), 1)
        eye = (ii == jj).astype(f32)
        nwv = nw_ref[...]

        heads = range(H)
        hsl = [slice(hh * DH, (hh + 1) * DH) for hh in heads]

        def chunk(c, carry):
            r0 = pl.multiple_of(c * CH, CH)
            rows = pl.ds(r0, CH)
            gam = g_ref[rows, :]
            gam_t = gam.T
            bb = b_ref[rows, :]
            glast = [gam[CH - 1:CH, hh:hh + 1] for hh in heads]
            loc = [_chunk_local(q_ref[rows, hsl[hh]], k_ref[rows, hsl[hh]], v_ref[rows, hsl[hh]],
                                gam[:, hh:hh + 1], gam_t[hh:hh + 1, :], glast[hh], bb[:, hh:hh + 1], ii, jj)
                   for hh in heads]
            st = [s_scr[hh] for hh in heads]
            zs = [z_ref[rows, hsl[hh]] for hh in heads]
            ts = [_bf(t_ref[c, hh]) for hh in heads]
            us = [_dot(t, _bf(l[4])) for t, l in zip(ts, loc)]
            ws = [_dot(t, _bf(l[3])) for t, l in zip(ts, loc)]
            stb = [_bf(s) for s in st]
            vn = [u - _dot(_bf(w), sb) for u, w, sb in zip(us, ws, stb)]
            vnb = [_bf(x) for x in vn]
            snew = [s * jnp.exp(gl) + _dot_tn(_bf(l[6]), xb) for s, gl, l, xb in zip(st, glast, loc, vnb)]
            os_ = [_dot(_bf(l[5]), sb) + _dot(_bf(l[8] * l[0]), xb) for l, sb, xb in zip(loc, stb, vnb)]
            for hh in heads:
                o = os_[hh]
                s_out[c, hh] = st[hh]
                s_scr[hh] = snew[hh]
                o_ref[rows, hsl[hh]] = o
                on = o * lax.rsqrt(jnp.mean(o * o, axis=-1, keepdims=True) + RMS_EPS) * nwv
                y_ref[rows, hsl[hh]] = _bf(on * (zs[hh] * _sigmoid(zs[hh])))
            return carry

        lax.fori_loop(0, nc, chunk, 0)
        pre = alpha * h_ref[...] + _dot(y_ref[...], wout[...])
        pre_ref[...] = pre
        out_ref[...] = _ln_fwd(pre, lg_ref[...], lb_ref[...], _row_ids(pl.program_id(0), rb, D), first_row)

    row = lambda i: (i, 0)
    fix = lambda i: (0, 0)
    return pl.pallas_call(
        body, name="delta_fwd", grid=(nblk,),
        out_shape=(S((L, W), f32), S((L, W), bf16), S((L // CH, H, DH, DH), f32), S((L, D), f32), S((L, D), f32)),
        in_specs=[pl.BlockSpec((rb, W), row)] * 3 + [pl.BlockSpec((rb, LANE), row)] * 2
        + [pl.BlockSpec((nc, H, CH, CH), lambda i: (i, 0, 0, 0)),
           pl.BlockSpec((rb, W), row), pl.BlockSpec((1, DH), fix), pl.BlockSpec((rb, D), row), ANY,
           pl.BlockSpec((1, D), fix), pl.BlockSpec((1, D), fix)],
        out_specs=(pl.BlockSpec((rb, W), row), pl.BlockSpec((rb, W), row),
                   pl.BlockSpec((nc, H, DH, DH), lambda i: (i, 0, 0, 0)),
                   pl.BlockSpec((rb, D), row), pl.BlockSpec((rb, D), row)),
        scratch_shapes=[pltpu.VMEM((H, DH, DH), f32), pltpu.VMEM((W, D), w_out.dtype), pltpu.SemaphoreType.DMA((1,))],
        compiler_params=_cp(dimension_semantics=("arbitrary",)))(q, k, v, g, beta, t_all, z, nw, h, w_out, ln_g, ln_b)


def _delta_bwd(dpre, w_out, o, z, nw, q, k, v, g, beta, s_all, t_all, *, H):
    L = q.shape[0]
    W = H * DH
    D = dpre.shape[1]
    rb = TM
    nc = rb // CH
    nblk = L // rb

    def body(dpre_ref, wout_hbm, o_ref, z_ref, nw_ref, q_ref, k_ref, v_ref, g_ref, b_ref, s_ref, t_ref,
             dq_ref, dk_ref, dv_ref, dz_ref, dg_ref, db_ref, dnw_ref, ds_scr, wout, dy_scr, sem):
        _load_once([(wout_hbm, wout)], sem)

        @pl.when(pl.program_id(0) == 0)
        def _():
            ds_scr[...] = jnp.zeros_like(ds_scr)
            dnw_ref[...] = jnp.zeros_like(dnw_ref)

        dy_scr[...] = _dot_nt(_bf(dpre_ref[...]), wout[...])

        ii = lax.broadcasted_iota(jnp.int32, (CH, CH), 0)
        jj = lax.broadcasted_iota(jnp.int32, (CH, CH), 1)
        lane = lax.broadcasted_iota(jnp.int32, (CH, LANE), 1)
        last_row = lax.broadcasted_iota(jnp.int32, (CH, 1), 0) == CH - 1
        nwv = nw_ref[...]

        def chunk(cc, carry):
            c = nc - 1 - cc
            r0 = pl.multiple_of(c * CH, CH)
            rows = pl.ds(r0, CH)
            gam = g_ref[rows, :]
            gam_t = gam.T
            bb = b_ref[rows, :]

            def head(hh):
                hs = slice(hh * DH, (hh + 1) * DH)
                gcol, grow, glast = gam[:, hh:hh + 1], gam_t[hh:hh + 1, :], gam[CH - 1:CH, hh:hh + 1]
                bcol = bb[:, hh:hh + 1]
                qh, kh, vh = q_ref[rows, hs], k_ref[rows, hs], v_ref[rows, hs]
                oh, zh, dyh = o_ref[rows, hs], z_ref[rows, hs], dy_scr[rows, hs]
                t = t_ref[c, hh]
                st = s_ref[c, hh]
                dsn = ds_scr[hh]
                rms = lax.rsqrt(jnp.mean(oh * oh, axis=-1, keepdims=True) + RMS_EPS)
                on = oh * rms
                sig = _sigmoid(zh)
                sz = zh * sig
                dz_ref[rows, hs] = dyh * on * nwv * (sig * (1.0 + zh * (1.0 - sig)))
                dnw = jnp.sum(dyh * on * sz, axis=0, keepdims=True)
                don = dyh * nwv * sz
                do = rms * (don - on * jnp.mean(don * on, axis=-1, keepdims=True))
                dec, eg, kb, kbg, vb, qt, kt, kk, qk, kqb = _chunk_local(qh, kh, vh, gcol, grow, glast, bcol, ii, jj)
                stb, dsnb, dob, tb, kbgb = _bf(st), _bf(dsn), _bf(do), _bf(t), _bf(kbg)
                r = vb - _dot(kbgb, stb)
                mm = jnp.where(ii > jj, kk * dec, 0.0)
                attn = qk * dec
                yield
                rbf = _bf(r)
                vn = _dot(tb, rbf)
                dvn = _dot_tn(_bf(attn), dob) + _dot(_bf(kt), dsnb)
                egl = jnp.exp(glast)
                ekt = jnp.exp(glast - gcol)
                yield
                vnb, dvnb = _bf(vn), _bf(dvn)
                dattn = jnp.where(ii >= jj, _dot_nt(dob, vnb), 0.0)
                dkt = _dot_nt(vnb, dsnb)
                dvb = _dot_tn(tb, dvnb)
                dt = _dot_nt(dvnb, rbf)
                dglast = egl * jnp.sum(jnp.sum(dsn * st, axis=0, keepdims=True), axis=1, keepdims=True)
                yield
                dv_ref[rows, hs] = dvb * bcol
                dod = jnp.concatenate([dob, -_bf(dvb)], axis=0)
                ds_scr[hh] = egl * dsn + _dot_tn(jnp.concatenate([_bf(qt), kbgb], axis=0), dod)
                both = _dot_nt(dod, stb)
                dqt, dkbg = both[:CH], both[CH:]
                x = _dot_nt(_bf(dt), tb)
                yield
                dm = jnp.where(ii > jj, -_dot_tn(tb, _bf(x)), 0.0)
                dkk = dm * dec
                dqk = dattn * dec
                e = dm * mm + dattn * attn
                dgam = jnp.sum(e, axis=1, keepdims=True) - jnp.sum(e.T, axis=1, keepdims=True)
                dd = _bf(jnp.concatenate([dkk, dqk], axis=0))
                both = _dot(dd, _bf(kh))
                dkb = both[:CH] + dkbg * eg
                dk_ref[rows, hs] = _dot_tn(dd, kqb) + dkt * ekt + dkb * bcol
                dq_ref[rows, hs] = both[CH:] + dqt * eg
                yield
                dktkt = dkt * kt
                dgam = dgam + jnp.sum(dqt * qt - dktkt + dkbg * kbg, axis=1, keepdims=True)
                dglast = dglast + jnp.sum(jnp.sum(dktkt, axis=0, keepdims=True), axis=1, keepdims=True)
                dgam = dgam + jnp.where(last_row, dglast, 0.0)
                dbeta = jnp.sum(dkb * kh + dvb * vh, axis=1, keepdims=True)
                return dgam, dbeta, dnw

            res = [None] * H
            gens = [head(hh) for hh in range(H)]
            step = 0
            while any(r is None for r in res):
                for hh in range(H):
                    if res[hh] is None and step >= (hh // BWD_HEAD_GROUP) * BWD_GROUP_LAG:
                        try:
                            next(gens[hh])
                        except StopIteration as stop:
                            res[hh] = stop.value
                step += 1
            dgam_all = jnp.zeros((CH, LANE), f32)
            dbeta_all = jnp.zeros((CH, LANE), f32)
            dnw_acc = jnp.zeros((1, DH), f32)
            for hh in range(H):
                dgam, dbeta, dnw = res[hh]
                dgam_all = dgam_all + jnp.where(lane == hh, dgam, 0.0)
                dbeta_all = dbeta_all + jnp.where(lane == hh, dbeta, 0.0)
                dnw_acc = dnw_acc + dnw
            dg_ref[rows, :] = dgam_all
            db_ref[rows, :] = dbeta_all
            dnw_ref[0:1, :] += dnw_acc
            return carry

        lax.fori_loop(0, nc, chunk, 0)

    rev = lambda i: (nblk - 1 - i, 0)
    rev4 = lambda i: (nblk - 1 - i, 0, 0, 0)
    fix = lambda i: (0, 0)
    wide = pl.BlockSpec((rb, W), rev)
    thin = pl.BlockSpec((rb, LANE), rev)
    return pl.pallas_call(
        body, name="delta_bwd", grid=(nblk,),
        out_shape=(S((L, W), f32),) * 4 + (S((L, LANE), f32),) * 2 + (S((8, DH), f32),),
        in_specs=[pl.BlockSpec((rb, D), rev), ANY, wide, wide, pl.BlockSpec((1, DH), fix), wide, wide, wide, thin, thin,
                  pl.BlockSpec((nc, H, DH, DH), rev4), pl.BlockSpec((nc, H, CH, CH), rev4)],
        out_specs=(wide,) * 4 + (thin, thin, pl.BlockSpec((8, DH), fix)),
        scratch_shapes=[pltpu.VMEM((H, DH, DH), f32), pltpu.VMEM((W, D), w_out.dtype), pltpu.VMEM((rb, W), f32),
                        pltpu.SemaphoreType.DMA((1,))],
        compiler_params=_cp(dimension_semantics=("arbitrary",)))(
            dpre, w_out, o, z, nw, q, k, v, g, beta, s_all, t_all)


def _sc_fwd(h, w_in, conv_w, w_out, g, b, *, first_row):
    L, D = h.shape
    W = w_out.shape[0]
    KW = conv_w.shape[0]
    tm = TM
    alpha = ALPHA

    def body(h_ref, win_hbm, cw_ref, wout_hbm, g_ref, b_ref, proj_ref, bu_ref, pre_ref, out_ref,
             win, wout, carry, sem):
        i = pl.program_id(0)
        _load_once([(win_hbm, win), (wout_hbm, wout)], sem)

        @pl.when(i == 0)
        def _():
            carry[...] = jnp.zeros_like(carry)

        hv = h_ref[...]
        hb = _bf(hv)
        bg = _dot(hb, win[:, 0:W])
        cg = _dot(hb, win[:, W:2 * W])
        xv = _dot(hb, win[:, 2 * W:3 * W])
        proj_ref[:, 0:W] = bg
        proj_ref[:, W:2 * W] = cg
        proj_ref[:, 2 * W:3 * W] = xv
        p = cg * xv
        u = _conv(cw_ref[...], _taps_back(carry[...], p, KW))
        carry[...] = p[tm - 8:tm, :]
        bu = _bf(bg * u)
        bu_ref[...] = bu
        pre = alpha * hv + _dot(bu, wout[...])
        pre_ref[...] = pre
        out_ref[...] = _ln_fwd(pre, g_ref[...], b_ref[...], _row_ids(i, tm, D), first_row)

    row = lambda i: (i, 0)
    fix = lambda i: (0, 0)
    return pl.pallas_call(
        body, name="sc_fwd", grid=(L // tm,),
        out_shape=(S((L, 3 * W), f32), S((L, W), bf16), S((L, D), f32), S((L, D), f32)),
        in_specs=[pl.BlockSpec((tm, D), row), ANY, pl.BlockSpec((KW, W), fix), ANY,
                  pl.BlockSpec((1, D), fix), pl.BlockSpec((1, D), fix)],
        out_specs=(pl.BlockSpec((tm, 3 * W), row), pl.BlockSpec((tm, W), row),
                   pl.BlockSpec((tm, D), row), pl.BlockSpec((tm, D), row)),
        scratch_shapes=[pltpu.VMEM((D, 3 * W), w_in.dtype), pltpu.VMEM((W, D), w_out.dtype),
                        pltpu.VMEM((8, W), f32), pltpu.SemaphoreType.DMA((2,))],
        compiler_params=_cp(dimension_semantics=("arbitrary",)))(h, w_in, conv_w, w_out, g, b)


def _sc_bwd(dpre, proj, conv_w, w_out, w_in, pre_in, g_in, *, first_row):
    L, D = dpre.shape
    W = w_out.shape[0]
    KW = conv_w.shape[0]
    tm = TM
    nb = L // tm
    alpha = ALPHA

    def body(dpre_ref, proj_ref, hc_ref, hx_ref, cw_ref, wout_hbm, win_hbm, pin_ref, g_ref,
             dproj_ref, dcw_ref, dpin_ref, dg_ref, db_ref, wout, win, carry, sem):
        i = pl.program_id(0)
        blk = nb - 1 - i
        _load_once([(wout_hbm, wout), (win_hbm, win)], sem)

        @pl.when(i == 0)
        def _():
            carry[...] = jnp.zeros_like(carry)
            dcw_ref[...] = jnp.zeros_like(dcw_ref)
            dg_ref[...] = jnp.zeros_like(dg_ref)
            db_ref[...] = jnp.zeros_like(db_ref)

        bg, cg, xv = proj_ref[:, 0:W], proj_ref[:, W:2 * W], proj_ref[:, 2 * W:3 * W]
        p = cg * xv
        u = _conv(cw_ref[...], _taps_back(jnp.where(blk > 0, hc_ref[...] * hx_ref[...], 0.0), p, KW))
        dpre_v = dpre_ref[...]
        d = _dot_nt(_bf(dpre_v), wout[...])
        dproj_ref[:, 0:W] = _bf(d * u)
        du = d * bg
        ahead = _taps_ahead(du, carry[...], KW)
        carry[...] = du[0:8, :]
        dp = _conv(cw_ref[...], ahead)
        for j in range(KW):
            dcw_ref[j:j + 1, :] += jnp.sum(ahead[j] * p, axis=0, keepdims=True)
        dproj_ref[:, W:2 * W] = _bf(dp * xv)
        dproj_ref[:, 2 * W:3 * W] = _bf(dp * cg)
        dh = alpha * dpre_v + _dot_nt(dproj_ref[...], win[...])
        dpin, dg, dbias = _ln_bwd_rows(dh, pin_ref[...], g_ref[...], _row_ids(blk, tm, D), first_row)
        dpin_ref[...] = dpin
        dg_ref[0:1, :] += dg
        db_ref[0:1, :] += dbias

    rev = lambda i: (nb - 1 - i, 0)
    fix = lambda i: (0, 0)

    def halo(col):
        return pl.BlockSpec((8, W), lambda i: (jnp.maximum((nb - 1 - i) * (tm // 8) - 1, 0), col))

    return pl.pallas_call(
        body, name="sc_bwd", grid=(nb,),
        out_shape=(S((L, 3 * W), bf16), S((8, W), f32), S((L, D), f32), S((8, D), f32), S((8, D), f32)),
        in_specs=[pl.BlockSpec((tm, D), rev), pl.BlockSpec((tm, 3 * W), rev), halo(1), halo(2),
                  pl.BlockSpec((KW, W), fix), ANY, ANY, pl.BlockSpec((tm, D), rev), pl.BlockSpec((1, D), fix)],
        out_specs=(pl.BlockSpec((tm, 3 * W), rev), pl.BlockSpec((8, W), fix), pl.BlockSpec((tm, D), rev),
                   pl.BlockSpec((8, D), fix), pl.BlockSpec((8, D), fix)),
        scratch_shapes=[pltpu.VMEM((W, D), w_out.dtype), pltpu.VMEM((D, 3 * W), w_in.dtype), pltpu.VMEM((8, W), f32),
                        pltpu.SemaphoreType.DMA((2,))],
        compiler_params=_cp(dimension_semantics=("arbitrary",)))(
            dpre, proj, proj, proj, conv_w, w_out, w_in, pre_in, g_in)


def _ffn_cols(F):
    fc = F
    for cand in (1408, 1024, 512, 256, 128):
        if F % cand == 0:
            fc = cand
            break
    return fc


def _ffn_weight_copies(wup_hbm, wdn_hbm, wup, wdn, layer):
    k = wdn_hbm.shape[2]
    return [(wup_hbm.at[layer], wup)] + [(wdn_hbm.at[p, layer], wdn.at[pl.ds(p * k, k), :]) for p in range(N_DEV)]


def _ffn_fwd(h, w_up, conv_w, w_down, g, b, *, layer, first_row, name):
    L, D = h.shape
    F = N_DEV * w_down.shape[2]
    KW = conv_w.shape[0]
    tm = TM
    fc = _ffn_cols(F)
    alpha = ALPHA

    def body(h_ref, wup_hbm, cw_ref, wdn_hbm, g_ref, b_ref, up_ref, a_ref, pre_ref, out_ref,
             wup, wdn, carry, sem):
        i = pl.program_id(0)
        _load_once(_ffn_weight_copies(wup_hbm, wdn_hbm, wup, wdn, layer), sem)

        @pl.when(i == 0)
        def _():
            carry[...] = jnp.zeros_like(carry)

        hv = h_ref[...]
        hb = _bf(hv)
        pre = alpha * hv
        for c0 in range(0, F, fc):
            cs = slice(c0, c0 + fc)
            u = _dot(hb, wup[:, cs])
            gate = _dot(hb, wup[:, F + c0:F + c0 + fc])
            up_ref[:, cs] = u
            up_ref[:, F + c0:F + c0 + fc] = gate
            uc = _conv(cw_ref[:, cs], _taps_back(carry[:, cs], u, KW))
            carry[:, cs] = u[tm - 8:tm, :]
            ab = _bf(uc * _sigmoid(uc) * gate)
            a_ref[:, cs] = ab
            pre = pre + _dot(ab, wdn[cs, :])
        pre_ref[...] = pre
        out_ref[...] = _ln_fwd(pre, g_ref[...], b_ref[...], _row_ids(i, tm, D), first_row)

    row = lambda i: (i, 0)
    fix = lambda i: (0, 0)
    return pl.pallas_call(
        body, name=name, grid=(L // tm,),
        out_shape=(S((L, 2 * F), f32), S((L, F), bf16), S((L, D), f32), S((L, D), f32)),
        in_specs=[pl.BlockSpec((tm, D), row), ANY, pl.BlockSpec((KW, F), fix), ANY,
                  pl.BlockSpec((1, D), fix), pl.BlockSpec((1, D), fix)],
        out_specs=(pl.BlockSpec((tm, 2 * F), row), pl.BlockSpec((tm, F), row),
                   pl.BlockSpec((tm, D), row), pl.BlockSpec((tm, D), row)),
        scratch_shapes=[pltpu.VMEM((D, 2 * F), w_up.dtype), pltpu.VMEM((F, D), w_down.dtype),
                        pltpu.VMEM((8, F), f32), pltpu.SemaphoreType.DMA((1 + N_DEV,))],
        compiler_params=_cp(dimension_semantics=("arbitrary",)))(h, w_up, conv_w, w_down, g, b)


def _ffn_bwd(dpre, up, w_down, conv_w, w_up, pre_in, g_in, *, layer, first_row, name):
    L, D = dpre.shape
    F = N_DEV * w_down.shape[2]
    KW = conv_w.shape[0]
    tm = TM
    nb = L // tm
    fc = FFN_BWD_COLS
    assert F % fc == 0
    alpha = ALPHA

    def body(dpre1_ref, up_ref, halo_ref, wdn_hbm, cw_ref, wup_hbm, dpre3_ref, pin_ref, g_ref,
             dup_ref, dcw_ref, dpin_ref, dg_ref, db_ref, wdn, wup, carry, da_scr, dup_scr, sem):
        i = pl.program_id(0)
        _load_once(_ffn_weight_copies(wup_hbm, wdn_hbm, wup, wdn, layer), sem)

        @pl.when(i == 0)
        def _():
            carry[...] = jnp.zeros_like(carry)
            da_scr[...] = jnp.zeros_like(da_scr)
            dup_scr[...] = jnp.zeros_like(dup_scr)
            dcw_ref[...] = jnp.zeros_like(dcw_ref)
            dg_ref[...] = jnp.zeros_like(dg_ref)
            db_ref[...] = jnp.zeros_like(db_ref)

        blk2 = jnp.clip(nb - i, 0, nb - 1)
        in2 = (i >= 1) & (i <= nb)
        blk3 = jnp.minimum(nb + 1 - i, nb - 1)
        in3 = i >= 2
        dpre1 = _bf(dpre1_ref[...])
        dh = alpha * dpre3_ref[...]
        for c0 in range(0, F, fc):
            cs = slice(c0, c0 + fc)
            gs_ = slice(F + c0, F + c0 + fc)
            dh = dh + _dot_nt(dup_scr[:, cs], wup[:, cs]) + _dot_nt(dup_scr[:, gs_], wup[:, gs_])
            da = da_scr[:, cs]
            da_scr[:, cs] = _dot_nt(dpre1, wdn[cs, :])
            gate = up_ref[:, gs_]
            u = up_ref[:, cs]
            uc = _conv(cw_ref[:, cs], _taps_back(jnp.where(blk2 > 0, halo_ref[:, cs], 0.0), u, KW))
            sig = _sigmoid(uc)
            dgate = _bf(da * (uc * sig))
            duc = da * gate * (sig * (1.0 + uc * (1.0 - sig)))
            ahead = _taps_ahead(duc, carry[:, cs], KW)
            carry[:, cs] = jnp.where(in2 & (i < nb), duc[0:8, :], carry[:, cs])
            du = _bf(_conv(cw_ref[:, cs], ahead))
            dup_ref[:, cs] = du
            dup_ref[:, gs_] = dgate
            dup_scr[:, cs] = du
            dup_scr[:, gs_] = dgate
            for j in range(KW):
                dcw_ref[j:j + 1, cs] += jnp.where(in2, jnp.sum(ahead[j] * u, axis=0, keepdims=True), 0.0)
        dpin, dg, dbias = _ln_bwd_rows(dh, pin_ref[...], g_ref[...], _row_ids(blk3, tm, D), first_row)
        dpin_ref[...] = dpin
        dg_ref[0:1, :] += jnp.where(in3, dg, 0.0)
        db_ref[0:1, :] += jnp.where(in3, dbias, 0.0)

    st1 = lambda i: (jnp.maximum(nb - 1 - i, 0), 0)
    st2 = lambda i: (jnp.clip(nb - i, 0, nb - 1), 0)
    st3 = lambda i: (jnp.minimum(nb + 1 - i, nb - 1), 0)
    fix = lambda i: (0, 0)
    return pl.pallas_call(
        body, name=name, grid=(nb + 2,),
        out_shape=(S((L, 2 * F), bf16), S((8, F), f32), S((L, D), f32), S((8, D), f32), S((8, D), f32)),
        in_specs=[pl.BlockSpec((tm, D), st1), pl.BlockSpec((tm, 2 * F), st2),
                  pl.BlockSpec((8, F), lambda i: (jnp.maximum(jnp.clip(nb - i, 0, nb - 1) * (tm // 8) - 1, 0), 0)),
                  ANY, pl.BlockSpec((KW, F), fix), ANY, pl.BlockSpec((tm, D), st3), pl.BlockSpec((tm, D), st3),
                  pl.BlockSpec((1, D), fix)],
        out_specs=(pl.BlockSpec((tm, 2 * F), st2), pl.BlockSpec((8, F), fix), pl.BlockSpec((tm, D), st3),
                   pl.BlockSpec((8, D), fix), pl.BlockSpec((8, D), fix)),
        scratch_shapes=[pltpu.VMEM((F, D), w_down.dtype), pltpu.VMEM((D, 2 * F), w_up.dtype), pltpu.VMEM((8, F), f32),
                        pltpu.VMEM((tm, F), f32), pltpu.VMEM((tm, 2 * F), bf16), pltpu.SemaphoreType.DMA((1 + N_DEV,))],
        compiler_params=_cp(dimension_semantics=("arbitrary",)))(
            dpre, up, up, w_down, conv_w, w_up, dpre, pre_in, g_in)


def _loss_head(h, target, pre, g, *, first_row):
    L, D = h.shape
    tm = TM
    pb = PADF // tm

    def body(h_ref, t_ref, pre_ref, g_ref, dpre_ref, dg_ref, db_ref, loss_ref):
        i = pl.program_id(0)

        @pl.when(i == 0)
        def _():
            loss_ref[...] = jnp.zeros_like(loss_ref)
            dg_ref[...] = jnp.zeros_like(dg_ref)
            db_ref[...] = jnp.zeros_like(db_ref)

        valid = i >= pb
        err = h_ref[...] - t_ref[...]
        dh = jnp.where(valid, err * (1.0 / D), 0.0)
        part = 0.5 * jnp.sum(jnp.sum(err * err, axis=-1, keepdims=True) * (1.0 / D), axis=0, keepdims=True)
        loss_ref[...] += jnp.where(valid, part, 0.0)
        dpre, dg, db = _ln_bwd_rows(dh, pre_ref[...], g_ref[...], _row_ids(i, tm, D), first_row)
        dpre_ref[...] = dpre
        dg_ref[0:1, :] += dg
        db_ref[0:1, :] += db

    row = lambda i: (i, 0)
    fix = lambda i: (0, 0)
    return pl.pallas_call(
        body, name="loss_head", grid=(L // tm,),
        out_shape=(S((L, D), f32), S((8, D), f32), S((8, D), f32), S((8, LANE), f32)),
        in_specs=[pl.BlockSpec((tm, D), row), pl.BlockSpec((tm, D), lambda i: (jnp.maximum(i - pb, 0), 0)),
                  pl.BlockSpec((tm, D), row), pl.BlockSpec((1, D), fix)],
        out_specs=(pl.BlockSpec((tm, D), row), pl.BlockSpec((8, D), fix), pl.BlockSpec((8, D), fix),
                   pl.BlockSpec((8, LANE), fix)),
        compiler_params=_cp(dimension_semantics=("arbitrary",)))(h, target, pre, g)


def _adamw(g_terms, w, m, v, *, name):
    R, C = w.shape
    tr = _row_tile(R)
    n = len(g_terms)
    c1 = 1.0 - ADAM_B1 ** ADAM_STEP
    c2 = 1.0 - ADAM_B2 ** ADAM_STEP

    def body(*refs):
        g = refs[0][...].astype(f32)
        for r in refs[1:n]:
            g = g + r[...].astype(f32)
        w_ref, m_ref, v_ref, g_out, d_out, m_out, v_out = refs[n:]
        mn = ADAM_B1 * m_ref[...] + (1.0 - ADAM_B1) * g
        vn = ADAM_B2 * v_ref[...] + (1.0 - ADAM_B2) * (g * g)
        g_out[...] = g
        m_out[...] = mn
        v_out[...] = vn
        d_out[...] = -ADAM_LR * ((mn / c1) / (jnp.sqrt(vn / c2) + ADAM_EPS) + ADAM_WD * w_ref[...])

    spec = pl.BlockSpec((tr, C), lambda i: (i, 0))
    return pl.pallas_call(
        body, name=name, grid=(R // tr,), out_shape=(S((R, C), f32),) * 4,
        in_specs=[spec] * (n + 3), out_specs=(spec,) * 4,
        compiler_params=_cp(dimension_semantics=("arbitrary",)))(*g_terms, w, m, v)


def _sum_devices(x):
    n, R, C = x.shape

    def body(x_ref, o_ref):
        acc = x_ref[0]
        for d in range(1, n):
            acc = acc + x_ref[d]
        o_ref[...] = acc

    return pl.pallas_call(body, name="sum_devices", out_shape=S((R, C), f32), compiler_params=_cp())(x)


def _row_tile(R):
    for step in (16, 8):
        for t in range(256, 0, -step):
            if R % t == 0:
                return t
    return R


def _adamw_direct(s32s, recvs, w, m, v, me, *, name):
    L, K, n = w.shape
    tk = _row_tile(K)
    c1 = 1.0 - ADAM_B1 ** ADAM_STEP
    c2 = 1.0 - ADAM_B2 ** ADAM_STEP

    def body(me_ref, *refs):
        own_refs, recv_refs = refs[:L], refs[L:2 * L]
        w_ref, m_ref, v_ref, g_out, d_out, m_out, v_out = refs[2 * L:]
        for li in range(L):
            @pl.when(pl.program_id(0) == li)
            def _(li=li):
                g = own_refs[li][0, 0]
                for d in range(N_DEV):
                    g = g + recv_refs[li][d, 0].astype(f32)
                mn = ADAM_B1 * m_ref[0] + (1.0 - ADAM_B1) * g
                vn = ADAM_B2 * v_ref[0] + (1.0 - ADAM_B2) * (g * g)
                g_out[0] = g
                m_out[0] = mn
                v_out[0] = vn
                d_out[0] = -ADAM_LR * ((mn / c1) / (jnp.sqrt(vn / c2) + ADAM_EPS) + ADAM_WD * w_ref[0])

    own = pl.BlockSpec((1, tk, n), lambda l, i, ix: (l, i, 0))
    grid_spec = pltpu.PrefetchScalarGridSpec(
        num_scalar_prefetch=1, grid=(L, K // tk),
        in_specs=[pl.BlockSpec((1, 1, tk, n), lambda l, i, ix: (ix[0], 0, i, 0))] * L
        + [pl.BlockSpec((N_DEV, 1, tk, n), lambda l, i, ix: (0, 0, i, 0))] * L + [own, own, own],
        out_specs=(own,) * 4)
    return pl.pallas_call(
        body, name=name, grid_spec=grid_spec, out_shape=(S((L, K, n), f32),) * 4,
        compiler_params=_cp(dimension_semantics=("arbitrary", "arbitrary")))(me, *s32s, *recvs, w, m, v)


def _col_segments(n, mapping):
    segs = []
    for p in range(N_DEV):
        lo, hi = p * n, (p + 1) * n
        out = []
        for c0, c1, e0 in mapping:
            a, b = max(lo, c0), min(hi, c1)
            if a < b:
                out.append((a - lo, e0 + (a - c0), b - a))
        segs.append(out)
    return segs


def _assemble_cols(gathered, mapping, n_out, *, name):
    _, L, K, n = gathered.shape
    tk = _row_tile(K)
    segs = _col_segments(n, mapping)
    covered = sum(w for s in segs for (_, _, w) in s)

    def body(g_ref, o_ref):
        if covered != n_out:
            o_ref[...] = jnp.zeros_like(o_ref)
        for p in range(N_DEV):
            for s0, d0, w in segs[p]:
                o_ref[0, :, d0:d0 + w] = g_ref[p, 0, :, s0:s0 + w]

    return pl.pallas_call(
        body, name=name, grid=(L, K // tk), out_shape=S((L, K, n_out), gathered.dtype),
        in_specs=[pl.BlockSpec((N_DEV, 1, tk, n), lambda l, i: (0, l, i, 0))],
        out_specs=pl.BlockSpec((1, tk, n_out), lambda l, i: (l, i, 0)),
        compiler_params=_cp(dimension_semantics=("arbitrary", "arbitrary")))(gathered)


def _split_cols(dws, mapping, n, *, name):
    L = len(dws)
    K, n_in = dws[0].shape
    tk = _row_tile(K)
    segs = _col_segments(n, mapping)

    def body(*refs):
        ins, o32, o16 = refs[:L], refs[L], refs[L + 1]
        for li in range(L):
            @pl.when(pl.program_id(0) == li)
            def _(li=li):
                for p in range(N_DEV):
                    for s0, d0, w in segs[p]:
                        val = ins[li][:, d0:d0 + w]
                        o32[p, 0, :, s0:s0 + w] = val
                        o16[p, 0, :, s0:s0 + w] = _bf(val)

    out = pl.BlockSpec((N_DEV, 1, tk, n), lambda l, i: (0, l, i, 0))
    return pl.pallas_call(
        body, name=name, grid=(L, K // tk), out_shape=(S((N_DEV, L, K, n), f32), S((N_DEV, L, K, n), bf16)),
        in_specs=[pl.BlockSpec((tk, n_in), lambda l, i: (i, 0))] * L, out_specs=(out, out),
        compiler_params=_cp(dimension_semantics=("arbitrary", "arbitrary")))(*dws)


def _split_rows(dws, k, *, name):
    L = len(dws)
    N = dws[0].shape[1]

    def body(*refs):
        ins, o32, o16 = refs[:L], refs[L], refs[L + 1]
        for li in range(L):
            @pl.when(pl.program_id(0) == li)
            def _(li=li):
                val = ins[li][...]
                o32[0, 0] = val
                o16[0, 0] = _bf(val)

    out = pl.BlockSpec((1, 1, k, N), lambda l, p: (p, l, 0, 0))
    return pl.pallas_call(
        body, name=name, grid=(L, N_DEV), out_shape=(S((N_DEV, L, k, N), f32), S((N_DEV, L, k, N), bf16)),
        in_specs=[pl.BlockSpec((k, N), lambda l, p: (p, 0))] * L, out_specs=(out, out),
        compiler_params=_cp(dimension_semantics=("arbitrary", "arbitrary")))(*dws)


def _rows_full(gathered):
    _, L, k, N = gathered.shape
    return jnp.transpose(gathered, (1, 0, 2, 3)).reshape(L, N_DEV * k, N)


def _all_gather(xs, *, name):
    na = len(xs)

    def body(*refs):
        x_refs, out_refs = refs[:na], refs[na:2 * na]
        send_sems, recv_sems, local_sems = refs[2 * na:]
        mx, my, mc = lax.axis_index("x"), lax.axis_index("y"), lax.axis_index("c")
        me, sibling = (mx, my, mc), (mx, my, 1 - mc)
        chips = [(1 - mx, my), (mx, 1 - my), (1 - mx, 1 - my)]

        def slot(a, px, py, pc):
            return out_refs[a].at[4 * px + 2 * py + pc]

        def copy(a, kk, block, to, src=None):
            return pltpu.make_async_remote_copy(
                src_ref=slot(a, *block) if src is None else src, dst_ref=slot(a, *block),
                send_sem=send_sems.at[7 * a + kk], recv_sem=recv_sems.at[7 * a + kk], device_id=to, device_id_type=MESH)

        mine = [pltpu.make_async_copy(x_refs[a], slot(a, *me), local_sems.at[a]) for a in range(na)]
        for cp in mine:
            cp.start()
        first = []
        for a in range(na):
            first.append(copy(a, 0, me, sibling, src=x_refs[a]))
            first += [copy(a, 1 + j, me, (*chip, mc), src=x_refs[a]) for j, chip in enumerate(chips)]
        for cp in first:
            cp.start()
        passed = []
        for j, chip in enumerate(chips):
            for a in range(na):
                copy(a, 1 + j, (*chip, mc), me).wait_recv()
                fwd = copy(a, 4 + j, (*chip, mc), sibling)
                fwd.start()
                passed.append(fwd)
        for a in range(na):
            copy(a, 0, sibling, me).wait_recv()
            for j, chip in enumerate(chips):
                copy(a, 4 + j, (*chip, 1 - mc), me).wait_recv()
        for cp in first + passed:
            cp.wait_send()
        for cp in mine:
            cp.wait()

    return pl.pallas_call(
        body, name=name, out_shape=tuple(S((N_DEV,) + x.shape, x.dtype) for x in xs),
        in_specs=[ANY] * na, out_specs=(ANY,) * na,
        scratch_shapes=[pltpu.SemaphoreType.DMA((7 * na,)), pltpu.SemaphoreType.DMA((7 * na,)),
                        pltpu.SemaphoreType.DMA((na,))],
        compiler_params=pltpu.CompilerParams(has_side_effects=True))(*xs)


_FLIPS = [(fx, fy, fc) for fx in (0, 1) for fy in (0, 1) for fc in (0, 1)][1:]


def _flip_peer(flip):
    x, y, c = lax.axis_index("x"), lax.axis_index("y"), lax.axis_index("c")
    return tuple(1 - a if f else a for a, f in zip((x, y, c), flip))


def _dev_index(p):
    return 4 * p[0] + 2 * p[1] + p[2]


HBM_SPEC = pl.BlockSpec(memory_space=pltpu.HBM)
SEM_SPEC = pl.BlockSpec(memory_space=pltpu.SEMAPHORE)


def _direct_start(srcs, lands, per_peer, *, name):
    na = len(srcs)

    def body(*refs):
        src_refs, land_refs = refs[:na], refs[na:2 * na]
        send_sems, recv_sems = refs[2 * na], refs[2 * na + 1]
        token = refs[-1]
        me = _dev_index((lax.axis_index("x"), lax.axis_index("y"), lax.axis_index("c")))
        for a in range(na):
            for r, flip in enumerate(_FLIPS):
                peer = _flip_peer(flip)
                src = src_refs[a].at[_dev_index(peer)] if per_peer else src_refs[a]
                pltpu.make_async_remote_copy(
                    src_ref=src, dst_ref=land_refs[a].at[me], send_sem=send_sems.at[7 * a + r],
                    recv_sem=recv_sems.at[7 * a + r], device_id=peer, device_id_type=MESH).start()
        token[...] = jnp.zeros_like(token)

    hbm = lambda t: pltpu.with_memory_space_constraint(t, pltpu.HBM)
    out = pl.pallas_call(
        body, name=name,
        out_shape=(pltpu.SemaphoreType.DMA((7 * na,)), pltpu.SemaphoreType.DMA((7 * na,)))
        + tuple(pltpu.HBM(t.shape, t.dtype) for t in list(srcs) + list(lands)) + (S((8, LANE), f32),),
        in_specs=[HBM_SPEC] * (2 * na),
        out_specs=(SEM_SPEC, SEM_SPEC) + (HBM_SPEC,) * (2 * na) + (pl.BlockSpec(memory_space=pltpu.VMEM),),
        input_output_aliases={i: 2 + i for i in range(2 * na)},
        compiler_params=pltpu.CompilerParams(has_side_effects=pltpu.SideEffectType.DATAFLOW_SIDE_EFFECTING))(
            *[hbm(t) for t in srcs], *[hbm(t) for t in lands])
    return out[0], out[1], list(out[2:2 + na]), list(out[2 + na:2 + 2 * na]), out[-1]


def _direct_wait(send_sems, recv_sems, srcs, lands, per_peer, after, *, name):
    na = len(srcs)

    def body(*refs):
        src_refs, land_refs = refs[:na], refs[na:2 * na]
        ssem, rsem = refs[2 * na], refs[2 * na + 1]
        me = _dev_index((lax.axis_index("x"), lax.axis_index("y"), lax.axis_index("c")))
        for a in range(na):
            for r, flip in enumerate(_FLIPS):
                peer = _flip_peer(flip)
                src = src_refs[a].at[_dev_index(peer)] if per_peer else src_refs[a]
                cp = pltpu.make_async_remote_copy(
                    src_ref=src, dst_ref=land_refs[a].at[me], send_sem=ssem.at[7 * a + r],
                    recv_sem=rsem.at[7 * a + r], device_id=peer, device_id_type=MESH)
                cp.wait_send()
                cp.wait_recv()

    out = pl.pallas_call(
        body, name=name, out_shape=tuple(pltpu.HBM(t.shape, t.dtype) for t in list(srcs) + list(lands)),
        in_specs=[HBM_SPEC] * (2 * na) + [SEM_SPEC, SEM_SPEC, ANY], out_specs=(HBM_SPEC,) * (2 * na),
        input_output_aliases={i: i for i in range(2 * na)},
        compiler_params=pltpu.CompilerParams(has_side_effects=pltpu.SideEffectType.DATAFLOW_SIDE_EFFECTING))(
            *srcs, *lands, send_sems, recv_sems, after)
    return list(out[:na]), list(out[na:])


def _pack_small(parts, width):
    rows, offs, r = [], [], 0
    for a in parts:
        n = a.size
        nr = -(-n // width)
        flat = a.reshape(-1).astype(f32)
        if nr * width != n:
            flat = jnp.pad(flat, (0, nr * width - n))
        rows.append(flat.reshape(nr, width))
        offs.append((r, nr))
        r += nr
    buf = jnp.concatenate(rows, axis=0)
    pad = (-r) % 8
    if pad:
        buf = jnp.pad(buf, ((0, pad), (0, 0)))
    return buf, offs


def _unpack_small(buf, off, shape):
    r, nr = off
    return buf[r:r + nr].reshape(-1)[:math.prod(shape)].reshape(shape)


def _local_step(x, target, meta, a_w_in, a_w_out, small, start_token, late_weights, grads_ready):
    SEQ, D = x.shape
    n_meta = meta.shape[0]
    first_row = PADF - n_meta
    H = small["a_log"].shape[-1]

    head = jnp.concatenate([jnp.zeros((first_row, D), f32), meta], axis=0)

    def lanes(a):
        return jnp.pad(a.reshape(1, -1), ((0, 0), (0, LANE - a.size)))

    def after_token(a, token):
        return a if token is None else a + token[0:1, 0:1]

    alog, dtb = after_token(lanes(small["a_log"][0]), start_token), lanes(small["a_dt_bias"][0])
    a_conv, b_conv = small["a_conv"][0], small["b_conv"][0]
    nw = small["a_norm"][0].reshape(1, DH)
    lmg, lmb, lfg, lfb = small["ln_mix_g"], small["ln_mix_b"], small["ln_ffn_g"], small["ln_ffn_b"]

    h0, pre_a, z, raw, q, k, v, beta, g, t_all = _gdn_in_fwd(x, head, a_w_in, a_conv, alog, dtb,
                                                             first_row=first_row, H=H)
    o, y, s_all, pre1, h1 = _delta_fwd(q, k, v, g, beta, t_all, z, nw, h0, a_w_out, lmg[0:1], lmb[0:1],
                                       first_row=first_row, H=H)
    wts = late_weights(h1)
    up0, act0, pre2, h2 = _ffn_fwd(h1, wts["ffn_w_up"], small["ffn_conv"][0], wts["ffn_w_down"],
                                   lfg[0:1], lfb[0:1], layer=0, first_row=first_row, name="ffn_fwd0")
    proj_b, bu, pre3, h3 = _sc_fwd(h2, wts["b_w_in"], b_conv, wts["b_w_out"], lmg[1:2], lmb[1:2], first_row=first_row)
    up1, act1, pre4, h4 = _ffn_fwd(h3, wts["ffn_w_up"], small["ffn_conv"][1], wts["ffn_w_down"],
                                   lfg[1:2], lfb[1:2], layer=1, first_row=first_row, name="ffn_fwd1")
    gs = {}
    dpre4, dlfg1, dlfb1, loss_tile = _loss_head(h4, target, pre4, lfg[1:2], first_row=first_row)

    def ffn_backward(dpre, up, act, h_in, layer, tag, ln_in, token=None):
        dup, dcw, dpre_in, dg, db = _ffn_bwd(
            dpre, up, wts["ffn_w_down"], after_token(small["ffn_conv"][layer], token),
            wts["ffn_w_up"], ln_in[0], ln_in[1], layer=layer, first_row=first_row, name="ffn_bwd" + tag)
        dwd = _linear_dw(act, dpre, name="dw_down" + tag)
        dwu = _linear_dw(h_in, dup, name="dw_up" + tag)
        return dpre_in, dg, db, dwu, dwd, dcw[0:3]

    dpre3, dlmg1, dlmb1, dwu1, dwd1, dcf1 = ffn_backward(dpre4, up1, act1, h3, 1, "1", (pre3, lmg[1:2]))

    dproj_b, dcb, dpre2, dlfg0, dlfb0 = _sc_bwd(dpre3, proj_b, b_conv, wts["b_w_out"], wts["b_w_in"], pre2, lfg[0:1],
                                                first_row=first_row)
    dwb_in = _linear_dw(h2, dproj_b, name="dw_b_in")
    token = grads_ready("layer1", dict(ffn_w_up=dwu1, ffn_w_down=dwd1, b_w_in=dwb_in))

    dpre1, dlmg0, dlmb0, dwu0, dwd0, dcf0 = ffn_backward(dpre2, up0, act0, h1, 0, "0", (pre1, lmg[0:1]), token)
    token = grads_ready("layer0", dict(ffn_w_up=dwu0, ffn_w_down=dwd0))

    dq, dk, dv, dz, dg_, dbeta, dnw = _delta_bwd(dpre1, a_w_out, o, z, after_token(nw, token), q, k, v, g, beta,
                                                 s_all, t_all, H=H)
    dproj_a, dca, dal, ddt, grad_x, dhead = _gdn_in_bwd(dq, dk, dv, dz, dg_, dbeta, pre_a, raw, a_conv, alog, dtb,
                                                        a_w_in, dpre1, first_row=first_row, H=H)
    token = grads_ready("last", dict(a_w_in=_linear_dw(h0, dproj_a, name="dw_a_in")))
    grads_ready("tail", dict(a_w_out=_linear_dw(y, dpre1, name="dw_a_out", after=token),
                             b_w_out=_linear_dw(bu, dpre3, name="dw_b_out", after=token)))

    gs["meta"] = dhead[first_row:PADF]
    gs["a_conv"] = dca[0:a_conv.shape[0]][None]
    gs["a_log"] = dal[0:1, 0:H]
    gs["a_dt_bias"] = ddt[0:1, 0:H]
    gs["a_norm"] = dnw[0:1]
    gs["b_conv"] = dcb[0:b_conv.shape[0]][None]
    gs["ln_mix_g"] = jnp.stack([dlmg0[0], dlmg1[0]])
    gs["ln_mix_b"] = jnp.stack([dlmb0[0], dlmb1[0]])
    gs["ffn_conv"] = jnp.stack([dcf0, dcf1])
    gs["ln_ffn_g"] = jnp.stack([dlfg0[0], dlfg1[0]])
    gs["ln_ffn_b"] = jnp.stack([dlfb0[0], dlfb1[0]])
    return loss_tile, grad_x, gs


_BIG = ("a_w_in", "a_w_out", "b_w_in", "b_w_out", "ffn_w_up", "ffn_w_down")
_BIG_COL = ("a_w_in", "b_w_in", "ffn_w_up")
_SMALL = ("meta", "a_conv", "a_log", "a_dt_bias", "a_norm", "b_conv", "ln_mix_g", "ln_mix_b",
          "ffn_conv", "ln_ffn_g", "ln_ffn_b")
_SMALL_SHARDED = ("meta", "a_conv", "b_conv", "ffn_conv")
_ORDER = ("meta", "a_w_in", "a_conv", "a_log", "a_dt_bias", "a_norm", "a_w_out", "b_w_in", "b_conv", "b_w_out",
          "ln_mix_g", "ln_mix_b", "ffn_w_up", "ffn_conv", "ffn_w_down", "ln_ffn_g", "ln_ffn_b")


def _a_w_in_map(H):
    W4 = 4 * H * DH
    return [(0, W4, 0), (W4, W4 + H, W4), (W4 + H, W4 + 2 * H, W4 + LANE)], W4 + 2 * LANE


def kernel(x, meta, a_w_in, a_conv, a_log, a_dt_bias, a_norm, a_w_out, b_w_in, b_conv, b_w_out, ln_mix_g, ln_mix_b, ffn_w_up, ffn_conv, ffn_w_down, ln_ffn_g, ln_ffn_b, loss_target, m_meta, m_a_w_in, m_a_conv, m_a_log, m_a_dt_bias, m_a_norm, m_a_w_out, m_b_w_in, m_b_conv, m_b_w_out, m_ln_mix_g, m_ln_mix_b, m_ffn_w_up, m_ffn_conv, m_ffn_w_down, m_ln_ffn_g, m_ln_ffn_b, v_meta, v_a_w_in, v_a_conv, v_a_log, v_a_dt_bias, v_a_norm, v_a_w_out, v_b_w_in, v_b_conv, v_b_w_out, v_ln_mix_g, v_ln_mix_b, v_ffn_w_up, v_ffn_conv, v_ffn_w_down, v_ln_ffn_g, v_ln_ffn_b):
    wloc = dict(meta=meta, a_w_in=a_w_in, a_conv=a_conv, a_log=a_log, a_dt_bias=a_dt_bias, a_norm=a_norm,
                a_w_out=a_w_out, b_w_in=b_w_in, b_conv=b_conv, b_w_out=b_w_out, ln_mix_g=ln_mix_g, ln_mix_b=ln_mix_b,
                ffn_w_up=ffn_w_up, ffn_conv=ffn_conv, ffn_w_down=ffn_w_down, ln_ffn_g=ln_ffn_g, ln_ffn_b=ln_ffn_b)
    mloc = dict(meta=m_meta, a_w_in=m_a_w_in, a_conv=m_a_conv, a_log=m_a_log, a_dt_bias=m_a_dt_bias, a_norm=m_a_norm,
                a_w_out=m_a_w_out, b_w_in=m_b_w_in, b_conv=m_b_conv, b_w_out=m_b_w_out, ln_mix_g=m_ln_mix_g,
                ln_mix_b=m_ln_mix_b, ffn_w_up=m_ffn_w_up, ffn_conv=m_ffn_conv, ffn_w_down=m_ffn_w_down,
                ln_ffn_g=m_ln_ffn_g, ln_ffn_b=m_ln_ffn_b)
    vloc = dict(meta=v_meta, a_w_in=v_a_w_in, a_conv=v_a_conv, a_log=v_a_log, a_dt_bias=v_a_dt_bias, a_norm=v_a_norm,
                a_w_out=v_a_w_out, b_w_in=v_b_w_in, b_conv=v_b_conv, b_w_out=v_b_w_out, ln_mix_g=v_ln_mix_g,
                ln_mix_b=v_ln_mix_b, ffn_w_up=v_ffn_w_up, ffn_conv=v_ffn_conv, ffn_w_down=v_ffn_w_down,
                ln_ffn_g=v_ln_ffn_g, ln_ffn_b=v_ln_ffn_b)
    H = a_log.shape[-1]
    mx, my, mc = lax.axis_index("x"), lax.axis_index("y"), lax.axis_index("c")
    me = 4 * mx + 2 * my + mc

    a_map, a_cols = _a_w_in_map(H)
    col_maps = {"a_w_in": (a_map, a_cols)}
    for n in ("b_w_in", "ffn_w_up"):
        ncols = N_DEV * wloc[n].shape[-1]
        col_maps[n] = ([(0, ncols, 0)], ncols)
    sm_sh = [wloc[n] for n in _SMALL_SHARDED]
    sbuf, soffs = _pack_small(sm_sh, 128)
    g_a_w_in, g_a_w_out, sg = _all_gather([_bf(wloc["a_w_in"]), _bf(wloc["a_w_out"]), sbuf], name="gather_first")
    w_a_in = _assemble_cols(g_a_w_in, *col_maps["a_w_in"], name="assemble_a_w_in")[0]
    w_a_out = _rows_full(g_a_w_out)[0]
    late = [n for n in _BIG if n not in ("a_w_in", "a_w_out")]
    ssem, rsem, srcs_t, lands_t, start_token = _direct_start(
        [_bf(wloc[n]) for n in late], [lax.empty((N_DEV,) + wloc[n].shape, bf16) for n in late], False,
        name="gather_rest_start")

    def late_weights(after):
        srcs_d, landed = _direct_wait(ssem, rsem, srcs_t, lands_t, False, after, name="gather_rest_wait")
        wts = {}
        for n, own, got in zip(late, srcs_d, landed):
            full = lax.dynamic_update_index_in_dim(got, own, me, 0)
            if n in _BIG_COL:
                wts[n] = _assemble_cols(full, *col_maps[n], name="assemble_" + n)
            elif n == "ffn_w_down":
                wts[n] = full
            else:
                wts[n] = _rows_full(full)
        for n in ("b_w_in", "b_w_out"):
            wts[n] = wts[n][0]
        return wts

    small = {n: wloc[n] for n in _SMALL}
    for n, off in zip(_SMALL_SHARDED, soffs):
        sh = wloc[n].shape
        parts = jnp.stack([_unpack_small(sg[d], off, sh) for d in range(N_DEV)])
        nd = len(sh)
        small[n] = jnp.transpose(parts, tuple(range(1, nd)) + (0, nd)).reshape(sh[:-1] + (N_DEV * sh[-1],))

    def split(n, dws, tag):
        if n in _BIG_COL:
            return _split_cols(dws, col_maps[n][0], wloc[n].shape[-1], name="split_" + n + tag)
        return _split_rows(dws, wloc[n].shape[-2], name="split_" + n + tag)

    sent = {}

    def grads_ready(stage, grads):
        names = sorted(grads)
        parts = [split(n, [grads[n]], "_" + stage) for n in names]
        handles = _direct_start([p[1] for p in parts], [jnp.zeros(p[1].shape, bf16) for p in parts], True,
                                name="grads_" + stage + "_start")
        sent[stage] = (names, [p[0] for p in parts], handles)
        return handles[4]

    loss_tile, grad_x, gs = _local_step(x[0], loss_target[0], small["meta"], w_a_in, w_a_out, small, start_token,
                                        late_weights, grads_ready)

    def landed(stage, after):
        names, own32, (ssem_g, rsem_g, srcs_g, lands_g, _) = sent[stage]
        _, got = _direct_wait(ssem_g, rsem_g, srcs_g, lands_g, True, after, name="grads_" + stage + "_wait")
        return list(zip(names, own32, got))

    parts = {}
    for stage in ("layer0", "layer1"):
        for n, o32, r in landed(stage, grad_x):
            parts.setdefault(n, []).append((o32, r))
    me1 = jnp.stack([me]).astype(jnp.int32)
    big_out = {n: _adamw_direct([p[0] for p in ps], [p[1] for p in ps], wloc[n], mloc[n], vloc[n], me1,
                                name="adamw_" + n) for n, ps in parts.items()}
    names = list(_SMALL)
    pbuf, poffs = _pack_small([gs[n] for n in names] + [loss_tile[0:1, 0:1]], 1024)
    psum = _sum_devices(_all_gather([pbuf], name="gather_small_grads")[0])
    loss = psum[poffs[-1][0], 0]
    g_small = {}
    for n, off in zip(names, poffs[:-1]):
        full_shape = gs[n].shape
        gfull = _unpack_small(psum, off, full_shape)
        if n in _SMALL_SHARDED:
            ns = wloc[n].shape[-1]
            gfull = lax.dynamic_slice_in_dim(gfull, me * ns, ns, axis=gfull.ndim - 1)
        g_small[n] = gfull.reshape(wloc[n].shape)
    gbuf, aoffs = _pack_small([g_small[n] for n in names], 128)
    wbuf, _ = _pack_small([wloc[n] for n in names], 128)
    mbuf, _ = _pack_small([mloc[n] for n in names], 128)
    vbuf, _ = _pack_small([vloc[n] for n in names], 128)
    _, d_s, m_s, v_s = _adamw([gbuf], wbuf, mbuf, vbuf, name="adamw_small")

    done = d_s[0, 0]
    for out in big_out.values():
        done = done + out[1][0, 0, 0]
    for stage in ("last", "tail"):
        for n, o32, r in landed(stage, done.reshape(1, 1)):
            big_out[n] = _adamw_direct([o32], [r], wloc[n], mloc[n], vloc[n], me1, name="adamw_" + n)

    grads, deltas, new_m, new_v = {}, {}, {}, {}
    for n in _BIG:
        grads[n], deltas[n], new_m[n], new_v[n] = big_out[n]
    for n, off in zip(names, aoffs):
        sh = wloc[n].shape
        grads[n] = g_small[n]
        deltas[n], new_m[n], new_v[n] = (_unpack_small(b_, off, sh) for b_ in (d_s, m_s, v_s))
    return (loss, grad_x[None], *[grads[n] for n in _ORDER], *[deltas[n] for n in _ORDER],
            *[new_m[n] for n in _ORDER], *[new_v[n] for n in _ORDER])
```

```python
import math

import jax
import jax.numpy as jnp
from jax import lax
from jax.experimental import pallas as pl
from jax.experimental.pallas import tpu as pltpu

f32, bf16 = jnp.float32, jnp.bfloat16
S = jax.ShapeDtypeStruct
HI = lax.Precision.HIGHEST
MESH = pl.DeviceIdType.MESH

V7X_VMEM_LIMIT = 56 * 1024 * 1024
LANE = 128
DH = 128
CH = 64
PADF = 256
TM = 256
TMM = 768
N_DEV = 8
BWD_HEAD_GROUP = 4
FFN_BWD_COLS = 256
BWD_GROUP_LAG = 2

DEPTH = 2
ALPHA = (2.0 * DEPTH) ** 0.25
LN_EPS = 1e-5
RMS_EPS = 1e-6
L2_EPS = 1e-6
ADAM_LR, ADAM_B1, ADAM_B2, ADAM_EPS, ADAM_WD, ADAM_STEP = 0.001, 0.9, 0.999, 1e-08, 0.01, 10


def _cp(**kw):
    return pltpu.CompilerParams(vmem_limit_bytes=V7X_VMEM_LIMIT, **kw)


def _bf(x):
    return x.astype(bf16)


def _dot(a, b, precision=None):
    return jnp.dot(a, b, preferred_element_type=f32, precision=precision)


def _dot_nt(a, b):
    return lax.dot_general(a, b, (((1,), (1,)), ((), ())), preferred_element_type=f32)


def _dot_tn(a, b):
    return lax.dot_general(a, b, (((0,), (0,)), ((), ())), preferred_element_type=f32)


def _sigmoid(x):
    return 1.0 / (1.0 + jnp.exp(-x))


def _load_once(pairs, sem):
    @pl.when(pl.program_id(0) == 0)
    def _():
        cps = [pltpu.make_async_copy(src, dst, sem.at[n]) for n, (src, dst) in enumerate(pairs)]
        for c in cps:
            c.start()
        for c in cps:
            c.wait()


def _row_ids(i, tm, width):
    return i * tm + lax.broadcasted_iota(jnp.int32, (tm, width), 0)


def _ln_fwd(pre, g, b, rows, first_row):
    mu = jnp.mean(pre, axis=-1, keepdims=True)
    xc = pre - mu
    var = jnp.mean(xc * xc, axis=-1, keepdims=True)
    y = xc * lax.rsqrt(var + LN_EPS) * g + b
    return jnp.where(rows >= first_row, y, 0.0)


ANY = pl.BlockSpec(memory_space=pl.ANY)


def _taps_back(prev8, x, kw):
    xe = jnp.concatenate([prev8, x], axis=0)
    return [pltpu.roll(xe, kw - 1 - j, 0)[8:] for j in range(kw - 1)] + [x]


def _taps_ahead(x, next8, kw):
    n = x.shape[0]
    xe = jnp.concatenate([x, next8], axis=0)
    return [pltpu.roll(xe, n + 8 - (kw - 1 - j), 0)[:n] for j in range(kw - 1)] + [x]


def _conv(cw, taps):
    acc = cw[0:1, :] * taps[0]
    for j in range(1, len(taps)):
        acc = acc + cw[j:j + 1, :] * taps[j]
    return acc


def _linear_dw(x, dy, *, name, after=None):
    L, K = x.shape
    N = dy.shape[1]
    tm = TMM if L % TMM == 0 else TM
    tn = LANE
    for d in range(N // LANE, 0, -1):
        if (N // LANE) % d == 0 and K * d * LANE * 4 <= 9 * 1024 * 1024:
            tn = d * LANE
            break

    def body(x_ref, dy_ref, *rest):
        o_ref = rest[-1]

        @pl.when(pl.program_id(1) == 0)
        def _():
            o_ref[...] = jnp.zeros_like(o_ref)
        o_ref[...] += _dot_tn(_bf(x_ref[...]), _bf(dy_ref[...]))

    in_specs = [pl.BlockSpec((tm, K), lambda j, i: (i, 0)), pl.BlockSpec((tm, tn), lambda j, i: (i, j))]
    args = [x, dy]
    if after is not None:
        in_specs.append(pl.BlockSpec(after.shape, lambda j, i: (0, 0)))
        args.append(after)
    return pl.pallas_call(
        body, name=name, grid=(N // tn, L // tm), out_shape=S((K, N), f32),
        in_specs=in_specs, out_specs=pl.BlockSpec((K, tn), lambda j, i: (0, j)),
        compiler_params=_cp(dimension_semantics=("arbitrary", "arbitrary")))(*args)


def _ln_bwd_rows(dout, pre, g, rows, first_row):
    mu = jnp.mean(pre, axis=-1, keepdims=True)
    xc = pre - mu
    rstd = lax.rsqrt(jnp.mean(xc * xc, axis=-1, keepdims=True) + LN_EPS)
    xh = xc * rstd
    dy = jnp.where(rows >= first_row, dout, 0.0)
    dxh = dy * g
    dpre = rstd * (dxh - jnp.mean(dxh, axis=-1, keepdims=True) - xh * jnp.mean(dxh * xh, axis=-1, keepdims=True))
    return dpre, jnp.sum(dy * xh, axis=0, keepdims=True), jnp.sum(dy, axis=0, keepdims=True)


def _gdn_in_fwd(x, head, w_full, conv_w, alog, dtb, *, first_row, H):
    D = x.shape[1]
    L = PADF + x.shape[0]
    W = H * DH
    NW = w_full.shape[1]
    KW = conv_w.shape[0]
    tm = TM
    pb = PADF // tm

    def body(x_ref, head_ref, w_hbm, cw_ref, alog_ref, dtb_ref,
             h_ref, pre_ref, z_ref, raw_ref, q_ref, k_ref, v_ref, beta_ref, g_ref, t_ref,
             w_vmem, carry, sem):
        i = pl.program_id(0)
        _load_once([(w_hbm, w_vmem)], sem)

        @pl.when(i == 0)
        def _():
            carry[...] = jnp.zeros_like(carry)

        hv = jnp.where(i < pb, head_ref[...], x_ref[...])
        h_ref[...] = hv
        hb = _bf(hv)
        outs = (q_ref, k_ref, v_ref)

        def section(s):
            pre = _dot(hb, w_vmem[:, s * W:(s + 1) * W])
            pre_ref[:, s * W:(s + 1) * W] = pre
            c = _conv(cw_ref[:, s * W:(s + 1) * W], _taps_back(carry[s], pre, KW))
            carry[s] = pre[tm - 8:tm, :]
            sl = c * _sigmoid(c)
            if s < 2:
                scale = DH ** -0.5 if s == 0 else 1.0
                for hh in range(H):
                    seg = sl[:, hh * DH:(hh + 1) * DH]
                    r = lax.rsqrt(jnp.sum(seg * seg, axis=-1, keepdims=True) + L2_EPS)
                    outs[s][:, hh * DH:(hh + 1) * DH] = seg * (r * scale)
            else:
                v_ref[...] = sl

        raw = _dot(hb, w_vmem[:, 4 * W:4 * W + 2 * LANE])
        raw_ref[...] = raw
        ok = (_row_ids(i, tm, LANE) >= first_row) & (lax.broadcasted_iota(jnp.int32, (tm, LANE), 1) < H)
        beta = jnp.where(ok, _sigmoid(raw[:, :LANE]), 0.0)
        beta_ref[...] = beta
        a = raw[:, LANE:] + dtb_ref[...]
        sp = jnp.maximum(a, 0.0) + jnp.log(1.0 + jnp.exp(-jnp.abs(a)))
        gv = jnp.where(ok, -jnp.exp(alog_ref[...]) * sp, 0.0)
        gam = _dot(_chunk_tri(tm, lower=True), gv, HI)
        g_ref[...] = gam
        section(1)
        ii = lax.broadcasted_iota(jnp.int32, (CH, CH), 0)
        jj = lax.broadcasted_iota(jnp.int32, (CH, CH), 1)
        eye = (ii == jj).astype(f32)
        gam_t = gam.T

        def inverses(chunks):
            ms = []
            for c in chunks:
                rows = slice(c * CH, (c + 1) * CH)
                for hh in range(H):
                    kh = k_ref[rows, hh * DH:(hh + 1) * DH]
                    dec = jnp.exp(jnp.minimum(gam[rows, hh:hh + 1] - gam_t[hh:hh + 1, rows], 0.0))
                    kk = _dot_nt(_bf(kh * beta[rows, hh:hh + 1]), _bf(kh))
                    ms.append(jnp.where(ii > jj, kk * dec, 0.0))
            for n, t in enumerate(_tri_inv_many(ms, eye)):
                t_ref[chunks[n // H], n % H] = t

        nch = tm // CH
        inverses(list(range(nch // 2)))
        section(0)
        inverses(list(range(nch // 2, nch)))
        section(2)
        z_ref[...] = _dot(hb, w_vmem[:, 3 * W:4 * W])

    row = lambda i: (i, 0)
    fix = lambda i: (0, 0)
    out_shape = (S((L, D), f32), S((L, 3 * W), f32), S((L, W), f32), S((L, 2 * LANE), f32),
                 S((L, W), f32), S((L, W), f32), S((L, W), f32), S((L, LANE), f32), S((L, LANE), f32),
                 S((L // CH, H, CH, CH), f32))
    out_specs = (pl.BlockSpec((tm, D), row),
                 pl.BlockSpec((tm, 3 * W), row), pl.BlockSpec((tm, W), row), pl.BlockSpec((tm, 2 * LANE), row),
                 pl.BlockSpec((tm, W), row), pl.BlockSpec((tm, W), row), pl.BlockSpec((tm, W), row),
                 pl.BlockSpec((tm, LANE), row), pl.BlockSpec((tm, LANE), row),
                 pl.BlockSpec((tm // CH, H, CH, CH), lambda i: (i, 0, 0, 0)))
    return pl.pallas_call(
        body, name="gdn_in_fwd", grid=(L // tm,), out_shape=out_shape,
        in_specs=[pl.BlockSpec((tm, D), lambda i: (jnp.maximum(i - pb, 0), 0)),
                  pl.BlockSpec((tm, D), lambda i: (jnp.minimum(i, pb - 1), 0)), ANY, pl.BlockSpec((KW, 3 * W), fix),
                  pl.BlockSpec((1, LANE), fix), pl.BlockSpec((1, LANE), fix)],
        out_specs=out_specs,
        scratch_shapes=[pltpu.VMEM((D, NW), w_full.dtype), pltpu.VMEM((3, 8, W), f32), pltpu.SemaphoreType.DMA((1,))],
        compiler_params=_cp(dimension_semantics=("arbitrary",)))(x, head, w_full, conv_w, alog, dtb)


def _gdn_in_bwd(dq, dk, dv, dz, dg, dbeta, pre, raw, conv_w, alog, dtb, w_full, res, *, first_row, H):
    L = dq.shape[0]
    D = res.shape[1]
    W = H * DH
    KW = conv_w.shape[0]
    tm = TM
    nb = L // tm
    NW = 4 * W + 2 * LANE
    fb = PADF // tm
    alpha = ALPHA

    def body(dq_ref, dk_ref, dv_ref, dz_ref, dg_ref, dbeta_ref, pre_ref, hq_ref, hk_ref, hv_ref, raw_ref,
             cw_ref, alog_ref, dtb_ref, w_hbm, res_ref,
             dproj_ref, dcw_ref, dal_ref, ddt_ref, dx_ref, dfront_ref, w_vmem, carry, tmp, sem):
        i = pl.program_id(0)
        blk = nb - 1 - i
        _load_once([(w_hbm, w_vmem)], sem)

        @pl.when(i == 0)
        def _():
            carry[...] = jnp.zeros_like(carry)
            dcw_ref[...] = jnp.zeros_like(dcw_ref)
            dal_ref[...] = jnp.zeros_like(dal_ref)
            ddt_ref[...] = jnp.zeros_like(ddt_ref)

        halos = (hq_ref, hk_ref, hv_ref)
        douts = (dq_ref, dk_ref, dv_ref)
        for s in range(3):
            sec = slice(s * W, (s + 1) * W)
            pre = pre_ref[:, sec]
            c = _conv(cw_ref[:, sec], _taps_back(jnp.where(blk > 0, halos[s][...], 0.0), pre, KW))
            sig = _sigmoid(c)
            sl = c * sig
            if s < 2:
                scale = DH ** -0.5 if s == 0 else 1.0
                for hh in range(H):
                    hs = slice(hh * DH, (hh + 1) * DH)
                    seg = sl[:, hs]
                    r = lax.rsqrt(jnp.sum(seg * seg, axis=-1, keepdims=True) + L2_EPS)
                    n = seg * r
                    dqs = douts[s][:, hs]
                    tmp[:, hs] = (scale * r) * (dqs - n * jnp.sum(n * dqs, axis=-1, keepdims=True))
                dsl = tmp[...]
            else:
                dsl = dv_ref[...]
            dc = dsl * (sig * (1.0 + c * (1.0 - sig)))
            ahead = _taps_ahead(dc, carry[s], KW)
            carry[s] = dc[0:8, :]
            dproj_ref[:, sec] = _bf(_conv(cw_ref[:, sec], ahead))
            for j in range(KW):
                dcw_ref[j:j + 1, sec] += jnp.sum(ahead[j] * pre, axis=0, keepdims=True)
        dproj_ref[:, 3 * W:4 * W] = _bf(dz_ref[...])
        raw_v = raw_ref[...]
        ok = (_row_ids(blk, tm, LANE) >= first_row) & (lax.broadcasted_iota(jnp.int32, (tm, LANE), 1) < H)
        beta = _sigmoid(raw_v[:, :LANE])
        dbraw = jnp.where(ok, dbeta_ref[...] * beta * (1.0 - beta), 0.0)
        a = raw_v[:, LANE:] + dtb_ref[...]
        sp = jnp.maximum(a, 0.0) + jnp.log(1.0 + jnp.exp(-jnp.abs(a)))
        nea = -jnp.exp(alog_ref[...])
        dgm = jnp.where(ok, _dot(_chunk_tri(tm, lower=False), dg_ref[...], HI), 0.0)
        daraw = dgm * nea * _sigmoid(a)
        dal_ref[0:1, :] += jnp.sum(dgm * nea * sp, axis=0, keepdims=True)
        ddt_ref[0:1, :] += jnp.sum(daraw, axis=0, keepdims=True)
        dproj_ref[:, 4 * W:4 * W + LANE] = _bf(dbraw)
        dproj_ref[:, 4 * W + LANE:4 * W + 2 * LANE] = _bf(daraw)
        dh = alpha * res_ref[...] + _dot_nt(dproj_ref[...], w_vmem[...])

        @pl.when(blk >= fb)
        def _():
            dx_ref[...] = dh

        @pl.when(blk < fb)
        def _():
            dfront_ref[...] = dh

    rev = lambda i: (nb - 1 - i, 0)
    fix = lambda i: (0, 0)

    def halo(col):
        return pl.BlockSpec((8, W), lambda i: (jnp.maximum((nb - 1 - i) * (tm // 8) - 1, 0), col))

    return pl.pallas_call(
        body, name="gdn_in_bwd", grid=(nb,),
        out_shape=(S((L, NW), bf16), S((8, 3 * W), f32), S((8, LANE), f32), S((8, LANE), f32),
                   S((L - PADF, D), f32), S((PADF, D), f32)),
        in_specs=[pl.BlockSpec((tm, W), rev)] * 4 + [pl.BlockSpec((tm, LANE), rev)] * 2
        + [pl.BlockSpec((tm, 3 * W), rev), halo(0), halo(1), halo(2), pl.BlockSpec((tm, 2 * LANE), rev),
           pl.BlockSpec((KW, 3 * W), fix), pl.BlockSpec((1, LANE), fix), pl.BlockSpec((1, LANE), fix),
           ANY, pl.BlockSpec((tm, D), rev)],
        out_specs=(pl.BlockSpec((tm, NW), rev), pl.BlockSpec((8, 3 * W), fix),
                   pl.BlockSpec((8, LANE), fix), pl.BlockSpec((8, LANE), fix),
                   pl.BlockSpec((tm, D), lambda i: (jnp.maximum(nb - 1 - i - fb, 0), 0)),
                   pl.BlockSpec((tm, D), lambda i: (jnp.minimum(nb - 1 - i, fb - 1), 0))),
        scratch_shapes=[pltpu.VMEM((D, NW), w_full.dtype), pltpu.VMEM((3, 8, W), f32), pltpu.VMEM((tm, W), f32),
                        pltpu.SemaphoreType.DMA((1,))],
        compiler_params=_cp(dimension_semantics=("arbitrary",)))(
            dq, dk, dv, dz, dg, dbeta, pre, pre, pre, pre, raw, conv_w, alog, dtb, w_full, res)


def _chunk_tri(n, lower):
    i = lax.broadcasted_iota(jnp.int32, (n, n), 0)
    j = lax.broadcasted_iota(jnp.int32, (n, n), 1)
    sh = int(math.log2(CH))
    same = lax.shift_right_logical(i, sh) == lax.shift_right_logical(j, sh)
    return (same & ((i >= j) if lower else (j >= i))).astype(f32)


def _tri_inv_many(ms, eye):
    ts = [eye - m for m in ms]
    ps = list(ms)
    for _ in range(int(math.log2(CH)) - 1):
        pb = [_bf(p) for p in ps]
        ps = [_dot(p, p) for p in pb]
        ts = [t + _dot(_bf(t), _bf(p)) for t, p in zip(ts, ps)]
    return ts


def _chunk_local(q, k, v, gcol, grow, glast, bcol, ii, jj):
    dec = jnp.where(ii >= jj, jnp.exp(jnp.minimum(gcol - grow, 0.0)), 0.0)
    eg = jnp.exp(gcol)
    kb = k * bcol
    kbg = kb * eg
    vb = v * bcol
    qt = q * eg
    kt = k * jnp.exp(glast - gcol)
    kbb, qb, kbf = _bf(kb), _bf(q), _bf(k)
    return dec, eg, kb, kbg, vb, qt, kt, _dot_nt(kbb, kbf), _dot_nt(qb, kbf), jnp.concatenate([kbb, qb], axis=0)


def _delta_fwd(q, k, v, g, beta, t_all, z, nw, h, w_out, ln_g, ln_b, *, first_row, H):
    L = q.shape[0]
    W = H * DH
    D = h.shape[1]
    rb = TM
    nc = rb // CH
    nblk = L // rb
    alpha = ALPHA

    def body(q_ref, k_ref, v_ref, g_ref, b_ref, t_ref, z_ref, nw_ref, h_ref, wout_hbm, lg_ref, lb_ref,
             o_ref, y_ref, s_out, pre_ref, out_ref, s_scr, wout, sem):
        _load_once([(wout_hbm, wout)], sem)

        @pl.when(pl.program_id(0) == 0)
        def _():
            s_scr[...] = jnp.zeros_like(s_scr)

        ii = lax.broadcasted_iota(jnp.int32, (CH, CH), 0)
        jj = lax.broadcasted_iota(jnp.int32, (CH, CH), 1)
        eye = (ii == jj).astype(f32)
        nwv = nw_ref[...]

        heads = range(H)
        hsl = [slice(hh * DH, (hh + 1) * DH) for hh in heads]

        def chunk(c, carry):
            r0 = pl.multiple_of(c * CH, CH)
            rows = pl.ds(r0, CH)
            gam = g_ref[rows, :]
            gam_t = gam.T
            bb = b_ref[rows, :]
            glast = [gam[CH - 1:CH, hh:hh + 1] for hh in heads]
            loc = [_chunk_local(q_ref[rows, hsl[hh]], k_ref[rows, hsl[hh]], v_ref[rows, hsl[hh]],
                                gam[:, hh:hh + 1], gam_t[hh:hh + 1, :], glast[hh], bb[:, hh:hh + 1], ii, jj)
                   for hh in heads]
            st = [s_scr[hh] for hh in heads]
            zs = [z_ref[rows, hsl[hh]] for hh in heads]
            ts = [_bf(t_ref[c, hh]) for hh in heads]
            us = [_dot(t, _bf(l[4])) for t, l in zip(ts, loc)]
            ws = [_dot(t, _bf(l[3])) for t, l in zip(ts, loc)]
            stb = [_bf(s) for s in st]
            vn = [u - _dot(_bf(w), sb) for u, w, sb in zip(us, ws, stb)]
            vnb = [_bf(x) for x in vn]
            snew = [s * jnp.exp(gl) + _dot_tn(_bf(l[6]), xb) for s, gl, l, xb in zip(st, glast, loc, vnb)]
            os_ = [_dot(_bf(l[5]), sb) + _dot(_bf(l[8] * l[0]), xb) for l, sb, xb in zip(loc, stb, vnb)]
            for hh in heads:
                o = os_[hh]
                s_out[c, hh] = st[hh]
                s_scr[hh] = snew[hh]
                o_ref[rows, hsl[hh]] = o
                on = o * lax.rsqrt(jnp.mean(o * o, axis=-1, keepdims=True) + RMS_EPS) * nwv
                y_ref[rows, hsl[hh]] = _bf(on * (zs[hh] * _sigmoid(zs[hh])))
            return carry

        lax.fori_loop(0, nc, chunk, 0)
        pre = alpha * h_ref[...] + _dot(y_ref[...], wout[...])
        pre_ref[...] = pre
        out_ref[...] = _ln_fwd(pre, lg_ref[...], lb_ref[...], _row_ids(pl.program_id(0), rb, D), first_row)

    row = lambda i: (i, 0)
    fix = lambda i: (0, 0)
    return pl.pallas_call(
        body, name="delta_fwd", grid=(nblk,),
        out_shape=(S((L, W), f32), S((L, W), bf16), S((L // CH, H, DH, DH), f32), S((L, D), f32), S((L, D), f32)),
        in_specs=[pl.BlockSpec((rb, W), row)] * 3 + [pl.BlockSpec((rb, LANE), row)] * 2
        + [pl.BlockSpec((nc, H, CH, CH), lambda i: (i, 0, 0, 0)),
           pl.BlockSpec((rb, W), row), pl.BlockSpec((1, DH), fix), pl.BlockSpec((rb, D), row), ANY,
           pl.BlockSpec((1, D), fix), pl.BlockSpec((1, D), fix)],
        out_specs=(pl.BlockSpec((rb, W), row), pl.BlockSpec((rb, W), row),
                   pl.BlockSpec((nc, H, DH, DH), lambda i: (i, 0, 0, 0)),
                   pl.BlockSpec((rb, D), row), pl.BlockSpec((rb, D), row)),
        scratch_shapes=[pltpu.VMEM((H, DH, DH), f32), pltpu.VMEM((W, D), w_out.dtype), pltpu.SemaphoreType.DMA((1,))],
        compiler_params=_cp(dimension_semantics=("arbitrary",)))(q, k, v, g, beta, t_all, z, nw, h, w_out, ln_g, ln_b)


def _delta_bwd(dpre, w_out, o, z, nw, q, k, v, g, beta, s_all, t_all, *, H):
    L = q.shape[0]
    W = H * DH
    D = dpre.shape[1]
    rb = TM
    nc = rb // CH
    nblk = L // rb

    def body(dpre_ref, wout_hbm, o_ref, z_ref, nw_ref, q_ref, k_ref, v_ref, g_ref, b_ref, s_ref, t_ref,
             dq_ref, dk_ref, dv_ref, dz_ref, dg_ref, db_ref, dnw_ref, ds_scr, wout, dy_scr, sem):
        _load_once([(wout_hbm, wout)], sem)

        @pl.when(pl.program_id(0) == 0)
        def _():
            ds_scr[...] = jnp.zeros_like(ds_scr)
            dnw_ref[...] = jnp.zeros_like(dnw_ref)

        dy_scr[...] = _dot_nt(_bf(dpre_ref[...]), wout[...])

        ii = lax.broadcasted_iota(jnp.int32, (CH, CH), 0)
        jj = lax.broadcasted_iota(jnp.int32, (CH, CH), 1)
        lane = lax.broadcasted_iota(jnp.int32, (CH, LANE), 1)
        last_row = lax.broadcasted_iota(jnp.int32, (CH, 1), 0) == CH - 1
        nwv = nw_ref[...]

        def chunk(cc, carry):
            c = nc - 1 - cc
            r0 = pl.multiple_of(c * CH, CH)
            rows = pl.ds(r0, CH)
            gam = g_ref[rows, :]
            gam_t = gam.T
            bb = b_ref[rows, :]

            def head(hh):
                hs = slice(hh * DH, (hh + 1) * DH)
                gcol, grow, glast = gam[:, hh:hh + 1], gam_t[hh:hh + 1, :], gam[CH - 1:CH, hh:hh + 1]
                bcol = bb[:, hh:hh + 1]
                qh, kh, vh = q_ref[rows, hs], k_ref[rows, hs], v_ref[rows, hs]
                oh, zh, dyh = o_ref[rows, hs], z_ref[rows, hs], dy_scr[rows, hs]
                t = t_ref[c, hh]
                st = s_ref[c, hh]
                dsn = ds_scr[hh]
                rms = lax.rsqrt(jnp.mean(oh * oh, axis=-1, keepdims=True) + RMS_EPS)
                on = oh * rms
                sig = _sigmoid(zh)
                sz = zh * sig
                dz_ref[rows, hs] = dyh * on * nwv * (sig * (1.0 + zh * (1.0 - sig)))
                dnw = jnp.sum(dyh * on * sz, axis=0, keepdims=True)
                don = dyh * nwv * sz
                do = rms * (don - on * jnp.mean(don * on, axis=-1, keepdims=True))
                dec, eg, kb, kbg, vb, qt, kt, kk, qk, kqb = _chunk_local(qh, kh, vh, gcol, grow, glast, bcol, ii, jj)
                stb, dsnb, dob, tb, kbgb = _bf(st), _bf(dsn), _bf(do), _bf(t), _bf(kbg)
                r = vb - _dot(kbgb, stb)
                mm = jnp.where(ii > jj, kk * dec, 0.0)
                attn = qk * dec
                yield
                rbf = _bf(r)
                vn = _dot(tb, rbf)
                dvn = _dot_tn(_bf(attn), dob) + _dot(_bf(kt), dsnb)
                egl = jnp.exp(glast)
                ekt = jnp.exp(glast - gcol)
                yield
                vnb, dvnb = _bf(vn), _bf(dvn)
                dattn = jnp.where(ii >= jj, _dot_nt(dob, vnb), 0.0)
                dkt = _dot_nt(vnb, dsnb)
                dvb = _dot_tn(tb, dvnb)
                dt = _dot_nt(dvnb, rbf)
                dglast = egl * jnp.sum(jnp.sum(dsn * st, axis=0, keepdims=True), axis=1, keepdims=True)
                yield
                dv_ref[rows, hs] = dvb * bcol
                dod = jnp.concatenate([dob, -_bf(dvb)], axis=0)
                ds_scr[hh] = egl * dsn + _dot_tn(jnp.concatenate([_bf(qt), kbgb], axis=0), dod)
                both = _dot_nt(dod, stb)
                dqt, dkbg = both[:CH], both[CH:]
                x = _dot_nt(_bf(dt), tb)
                yield
                dm = jnp.where(ii > jj, -_dot_tn(tb, _bf(x)), 0.0)
                dkk = dm * dec
                dqk = dattn * dec
                e = dm * mm + dattn * attn
                dgam = jnp.sum(e, axis=1, keepdims=True) - jnp.sum(e.T, axis=1, keepdims=True)
                dd = _bf(jnp.concatenate([dkk, dqk], axis=0))
                both = _dot(dd, _bf(kh))
                dkb = both[:CH] + dkbg * eg
                dk_ref[rows, hs] = _dot_tn(dd, kqb) + dkt * ekt + dkb * bcol
                dq_ref[rows, hs] = both[CH:] + dqt * eg
                yield
                dktkt = dkt * kt
                dgam = dgam + jnp.sum(dqt * qt - dktkt + dkbg * kbg, axis=1, keepdims=True)
                dglast = dglast + jnp.sum(jnp.sum(dktkt, axis=0, keepdims=True), axis=1, keepdims=True)
                dgam = dgam + jnp.where(last_row, dglast, 0.0)
                dbeta = jnp.sum(dkb * kh + dvb * vh, axis=1, keepdims=True)
                return dgam, dbeta, dnw

            res = [None] * H
            gens = [head(hh) for hh in range(H)]
            step = 0
            while any(r is None for r in res):
                for hh in range(H):
                    if res[hh] is None and step >= (hh // BWD_HEAD_GROUP) * BWD_GROUP_LAG:
                        try:
                            next(gens[hh])
                        except StopIteration as stop:
                            res[hh] = stop.value
                step += 1
            dgam_all = jnp.zeros((CH, LANE), f32)
            dbeta_all = jnp.zeros((CH, LANE), f32)
            dnw_acc = jnp.zeros((1, DH), f32)
            for hh in range(H):
                dgam, dbeta, dnw = res[hh]
                dgam_all = dgam_all + jnp.where(lane == hh, dgam, 0.0)
                dbeta_all = dbeta_all + jnp.where(lane == hh, dbeta, 0.0)
                dnw_acc = dnw_acc + dnw
            dg_ref[rows, :] = dgam_all
            db_ref[rows, :] = dbeta_all
            dnw_ref[0:1, :] += dnw_acc
            return carry

        lax.fori_loop(0, nc, chunk, 0)

    rev = lambda i: (nblk - 1 - i, 0)
    rev4 = lambda i: (nblk - 1 - i, 0, 0, 0)
    fix = lambda i: (0, 0)
    wide = pl.BlockSpec((rb, W), rev)
    thin = pl.BlockSpec((rb, LANE), rev)
    return pl.pallas_call(
        body, name="delta_bwd", grid=(nblk,),
        out_shape=(S((L, W), f32),) * 4 + (S((L, LANE), f32),) * 2 + (S((8, DH), f32),),
        in_specs=[pl.BlockSpec((rb, D), rev), ANY, wide, wide, pl.BlockSpec((1, DH), fix), wide, wide, wide, thin, thin,
                  pl.BlockSpec((nc, H, DH, DH), rev4), pl.BlockSpec((nc, H, CH, CH), rev4)],
        out_specs=(wide,) * 4 + (thin, thin, pl.BlockSpec((8, DH), fix)),
        scratch_shapes=[pltpu.VMEM((H, DH, DH), f32), pltpu.VMEM((W, D), w_out.dtype), pltpu.VMEM((rb, W), f32),
                        pltpu.SemaphoreType.DMA((1,))],
        compiler_params=_cp(dimension_semantics=("arbitrary",)))(
            dpre, w_out, o, z, nw, q, k, v, g, beta, s_all, t_all)


def _sc_fwd(h, w_in, conv_w, w_out, g, b, *, first_row):
    L, D = h.shape
    W = w_out.shape[0]
    KW = conv_w.shape[0]
    tm = TM
    alpha = ALPHA

    def body(h_ref, win_hbm, cw_ref, wout_hbm, g_ref, b_ref, proj_ref, bu_ref, pre_ref, out_ref,
             win, wout, carry, sem):
        i = pl.program_id(0)
        _load_once([(win_hbm, win), (wout_hbm, wout)], sem)

        @pl.when(i == 0)
        def _():
            carry[...] = jnp.zeros_like(carry)

        hv = h_ref[...]
        hb = _bf(hv)
        bg = _dot(hb, win[:, 0:W])
        cg = _dot(hb, win[:, W:2 * W])
        xv = _dot(hb, win[:, 2 * W:3 * W])
        proj_ref[:, 0:W] = bg
        proj_ref[:, W:2 * W] = cg
        proj_ref[:, 2 * W:3 * W] = xv
        p = cg * xv
        u = _conv(cw_ref[...], _taps_back(carry[...], p, KW))
        carry[...] = p[tm - 8:tm, :]
        bu = _bf(bg * u)
        bu_ref[...] = bu
        pre = alpha * hv + _dot(bu, wout[...])
        pre_ref[...] = pre
        out_ref[...] = _ln_fwd(pre, g_ref[...], b_ref[...], _row_ids(i, tm, D), first_row)

    row = lambda i: (i, 0)
    fix = lambda i: (0, 0)
    return pl.pallas_call(
        body, name="sc_fwd", grid=(L // tm,),
        out_shape=(S((L, 3 * W), f32), S((L, W), bf16), S((L, D), f32), S((L, D), f32)),
        in_specs=[pl.BlockSpec((tm, D), row), ANY, pl.BlockSpec((KW, W), fix), ANY,
                  pl.BlockSpec((1, D), fix), pl.BlockSpec((1, D), fix)],
        out_specs=(pl.BlockSpec((tm, 3 * W), row), pl.BlockSpec((tm, W), row),
                   pl.BlockSpec((tm, D), row), pl.BlockSpec((tm, D), row)),
        scratch_shapes=[pltpu.VMEM((D, 3 * W), w_in.dtype), pltpu.VMEM((W, D), w_out.dtype),
                        pltpu.VMEM((8, W), f32), pltpu.SemaphoreType.DMA((2,))],
        compiler_params=_cp(dimension_semantics=("arbitrary",)))(h, w_in, conv_w, w_out, g, b)


def _sc_bwd(dpre, proj, conv_w, w_out, w_in, pre_in, g_in, *, first_row):
    L, D = dpre.shape
    W = w_out.shape[0]
    KW = conv_w.shape[0]
    tm = TM
    nb = L // tm
    alpha = ALPHA

    def body(dpre_ref, proj_ref, hc_ref, hx_ref, cw_ref, wout_hbm, win_hbm, pin_ref, g_ref,
             dproj_ref, dcw_ref, dpin_ref, dg_ref, db_ref, wout, win, carry, sem):
        i = pl.program_id(0)
        blk = nb - 1 - i
        _load_once([(wout_hbm, wout), (win_hbm, win)], sem)

        @pl.when(i == 0)
        def _():
            carry[...] = jnp.zeros_like(carry)
            dcw_ref[...] = jnp.zeros_like(dcw_ref)
            dg_ref[...] = jnp.zeros_like(dg_ref)
            db_ref[...] = jnp.zeros_like(db_ref)

        bg, cg, xv = proj_ref[:, 0:W], proj_ref[:, W:2 * W], proj_ref[:, 2 * W:3 * W]
        p = cg * xv
        u = _conv(cw_ref[...], _taps_back(jnp.where(blk > 0, hc_ref[...] * hx_ref[...], 0.0), p, KW))
        dpre_v = dpre_ref[...]
        d = _dot_nt(_bf(dpre_v), wout[...])
        dproj_ref[:, 0:W] = _bf(d * u)
        du = d * bg
        ahead = _taps_ahead(du, carry[...], KW)
        carry[...] = du[0:8, :]
        dp = _conv(cw_ref[...], ahead)
        for j in range(KW):
            dcw_ref[j:j + 1, :] += jnp.sum(ahead[j] * p, axis=0, keepdims=True)
        dproj_ref[:, W:2 * W] = _bf(dp * xv)
        dproj_ref[:, 2 * W:3 * W] = _bf(dp * cg)
        dh = alpha * dpre_v + _dot_nt(dproj_ref[...], win[...])
        dpin, dg, dbias = _ln_bwd_rows(dh, pin_ref[...], g_ref[...], _row_ids(blk, tm, D), first_row)
        dpin_ref[...] = dpin
        dg_ref[0:1, :] += dg
        db_ref[0:1, :] += dbias

    rev = lambda i: (nb - 1 - i, 0)
    fix = lambda i: (0, 0)

    def halo(col):
        return pl.BlockSpec((8, W), lambda i: (jnp.maximum((nb - 1 - i) * (tm // 8) - 1, 0), col))

    return pl.pallas_call(
        body, name="sc_bwd", grid=(nb,),
        out_shape=(S((L, 3 * W), bf16), S((8, W), f32), S((L, D), f32), S((8, D), f32), S((8, D), f32)),
        in_specs=[pl.BlockSpec((tm, D), rev), pl.BlockSpec((tm, 3 * W), rev), halo(1), halo(2),
                  pl.BlockSpec((KW, W), fix), ANY, ANY, pl.BlockSpec((tm, D), rev), pl.BlockSpec((1, D), fix)],
        out_specs=(pl.BlockSpec((tm, 3 * W), rev), pl.BlockSpec((8, W), fix), pl.BlockSpec((tm, D), rev),
                   pl.BlockSpec((8, D), fix), pl.BlockSpec((8, D), fix)),
        scratch_shapes=[pltpu.VMEM((W, D), w_out.dtype), pltpu.VMEM((D, 3 * W), w_in.dtype), pltpu.VMEM((8, W), f32),
                        pltpu.SemaphoreType.DMA((2,))],
        compiler_params=_cp(dimension_semantics=("arbitrary",)))(
            dpre, proj, proj, proj, conv_w, w_out, w_in, pre_in, g_in)


def _ffn_cols(F):
    fc = F
    for cand in (1408, 1024, 512, 256, 128):
        if F % cand == 0:
            fc = cand
            break
    return fc


def _ffn_weight_copies(wup_hbm, wdn_hbm, wup, wdn, layer):
    k = wdn_hbm.shape[2]
    return [(wup_hbm.at[layer], wup)] + [(wdn_hbm.at[p, layer], wdn.at[pl.ds(p * k, k), :]) for p in range(N_DEV)]


def _ffn_fwd(h, w_up, conv_w, w_down, g, b, *, layer, first_row, name):
    L, D = h.shape
    F = N_DEV * w_down.shape[2]
    KW = conv_w.shape[0]
    tm = TM
    fc = _ffn_cols(F)
    alpha = ALPHA

    def body(h_ref, wup_hbm, cw_ref, wdn_hbm, g_ref, b_ref, up_ref, a_ref, pre_ref, out_ref,
             wup, wdn, carry, sem):
        i = pl.program_id(0)
        _load_once(_ffn_weight_copies(wup_hbm, wdn_hbm, wup, wdn, layer), sem)

        @pl.when(i == 0)
        def _():
            carry[...] = jnp.zeros_like(carry)

        hv = h_ref[...]
        hb = _bf(hv)
        pre = alpha * hv
        for c0 in range(0, F, fc):
            cs = slice(c0, c0 + fc)
            u = _dot(hb, wup[:, cs])
            gate = _dot(hb, wup[:, F + c0:F + c0 + fc])
            up_ref[:, cs] = u
            up_ref[:, F + c0:F + c0 + fc] = gate
            uc = _conv(cw_ref[:, cs], _taps_back(carry[:, cs], u, KW))
            carry[:, cs] = u[tm - 8:tm, :]
            ab = _bf(uc * _sigmoid(uc) * gate)
            a_ref[:, cs] = ab
            pre = pre + _dot(ab, wdn[cs, :])
        pre_ref[...] = pre
        out_ref[...] = _ln_fwd(pre, g_ref[...], b_ref[...], _row_ids(i, tm, D), first_row)

    row = lambda i: (i, 0)
    fix = lambda i: (0, 0)
    return pl.pallas_call(
        body, name=name, grid=(L // tm,),
        out_shape=(S((L, 2 * F), f32), S((L, F), bf16), S((L, D), f32), S((L, D), f32)),
        in_specs=[pl.BlockSpec((tm, D), row), ANY, pl.BlockSpec((KW, F), fix), ANY,
                  pl.BlockSpec((1, D), fix), pl.BlockSpec((1, D), fix)],
        out_specs=(pl.BlockSpec((tm, 2 * F), row), pl.BlockSpec((tm, F), row),
                   pl.BlockSpec((tm, D), row), pl.BlockSpec((tm, D), row)),
        scratch_shapes=[pltpu.VMEM((D, 2 * F), w_up.dtype), pltpu.VMEM((F, D), w_down.dtype),
                        pltpu.VMEM((8, F), f32), pltpu.SemaphoreType.DMA((1 + N_DEV,))],
        compiler_params=_cp(dimension_semantics=("arbitrary",)))(h, w_up, conv_w, w_down, g, b)


def _ffn_bwd(dpre, up, w_down, conv_w, w_up, pre_in, g_in, *, layer, first_row, name):
    L, D = dpre.shape
    F = N_DEV * w_down.shape[2]
    KW = conv_w.shape[0]
    tm = TM
    nb = L // tm
    fc = FFN_BWD_COLS
    assert F % fc == 0
    alpha = ALPHA

    def body(dpre1_ref, u_top, u_bot, gate_top, gate_bot, halo_ref, wdn_hbm, cw_ref, wup_hbm, dpre3_ref, pin_ref, g_ref,
             dup_ref, dcw_ref, dpin_ref, dg_ref, db_ref, wdn, wup, carry, da_scr, dup_scr, sem):
        i = pl.program_id(0)
        _load_once(_ffn_weight_copies(wup_hbm, wdn_hbm, wup, wdn, layer), sem)

        @pl.when(i == 0)
        def _():
            carry[...] = jnp.zeros_like(carry)
            da_scr[...] = jnp.zeros_like(da_scr)
            dup_scr[...] = jnp.zeros_like(dup_scr)
            dcw_ref[...] = jnp.zeros_like(dcw_ref)
            dg_ref[...] = jnp.zeros_like(dg_ref)
            db_ref[...] = jnp.zeros_like(db_ref)

        blk2 = jnp.clip(nb - i, 0, nb - 1)
        in2 = (i >= 1) & (i <= nb)
        blk3 = jnp.minimum(nb + 1 - i, nb - 1)
        in3 = i >= 2
        dpre1 = _bf(dpre1_ref[...])
        dh = alpha * dpre3_ref[...]
        for c0 in range(0, F, fc):
            cs = slice(c0, c0 + fc)
            gs_ = slice(F + c0, F + c0 + fc)
            dh = dh + _dot_nt(dup_scr[:, cs], wup[:, cs]) + _dot_nt(dup_scr[:, gs_], wup[:, gs_])
            da = da_scr[:, cs]
            da_scr[:, cs] = _dot_nt(dpre1, wdn[cs, :])
            gate = jnp.concatenate([gate_top[:, cs], gate_bot[:, cs]], axis=0)
            u = jnp.concatenate([u_top[:, cs], u_bot[:, cs]], axis=0)
            uc = _conv(cw_ref[:, cs], _taps_back(jnp.where(blk2 > 0, halo_ref[:, cs], 0.0), u, KW))
            sig = _sigmoid(uc)
            dgate = _bf(da * (uc * sig))
            duc = da * gate * (sig * (1.0 + uc * (1.0 - sig)))
            ahead = _taps_ahead(duc, carry[:, cs], KW)
            carry[:, cs] = jnp.where(in2 & (i < nb), duc[0:8, :], carry[:, cs])
            du = _bf(_conv(cw_ref[:, cs], ahead))
            dup_ref[:, cs] = du
            dup_ref[:, gs_] = dgate
            dup_scr[:, cs] = du
            dup_scr[:, gs_] = dgate
            for j in range(KW):
                dcw_ref[j:j + 1, cs] += jnp.where(in2, jnp.sum(ahead[j] * u, axis=0, keepdims=True), 0.0)
        dpin, dg, dbias = _ln_bwd_rows(dh, pin_ref[...], g_ref[...], _row_ids(blk3, tm, D), first_row)
        dpin_ref[...] = dpin
        dg_ref[0:1, :] += jnp.where(in3, dg, 0.0)
        db_ref[0:1, :] += jnp.where(in3, dbias, 0.0)

    st1 = lambda i: (jnp.maximum(nb - 1 - i, 0), 0)
    st2 = lambda i: (jnp.clip(nb - i, 0, nb - 1), 0)
    st3 = lambda i: (jnp.minimum(nb + 1 - i, nb - 1), 0)
    fix = lambda i: (0, 0)
    up_quarter = lambda half, part: pl.BlockSpec((tm // 2, F), lambda i: (2 * st2(i)[0] + half, part))
    return pl.pallas_call(
        body, name=name, grid=(nb + 2,),
        out_shape=(S((L, 2 * F), bf16), S((8, F), f32), S((L, D), f32), S((8, D), f32), S((8, D), f32)),
        in_specs=[pl.BlockSpec((tm, D), st1), up_quarter(0, 0), up_quarter(1, 0), up_quarter(0, 1), up_quarter(1, 1),
                  pl.BlockSpec((8, F), lambda i: (jnp.maximum(jnp.clip(nb - i, 0, nb - 1) * (tm // 8) - 1, 0), 0)),
                  ANY, pl.BlockSpec((KW, F), fix), ANY, pl.BlockSpec((tm, D), st3), pl.BlockSpec((tm, D), st3),
                  pl.BlockSpec((1, D), fix)],
        out_specs=(pl.BlockSpec((tm, 2 * F), st2), pl.BlockSpec((8, F), fix), pl.BlockSpec((tm, D), st3),
                   pl.BlockSpec((8, D), fix), pl.BlockSpec((8, D), fix)),
        scratch_shapes=[pltpu.VMEM((F, D), w_down.dtype), pltpu.VMEM((D, 2 * F), w_up.dtype), pltpu.VMEM((8, F), f32),
                        pltpu.VMEM((tm, F), f32), pltpu.VMEM((tm, 2 * F), bf16), pltpu.SemaphoreType.DMA((1 + N_DEV,))],
        compiler_params=_cp(dimension_semantics=("arbitrary",)))(
            dpre, up, up, up, up, up, w_down, conv_w, w_up, dpre, pre_in, g_in)


def _loss_head(h, target, pre, g, *, first_row):
    L, D = h.shape
    tm = TM
    pb = PADF // tm

    def body(h_ref, t_ref, pre_ref, g_ref, dpre_ref, dg_ref, db_ref, loss_ref):
        i = pl.program_id(0)

        @pl.when(i == 0)
        def _():
            loss_ref[...] = jnp.zeros_like(loss_ref)
            dg_ref[...] = jnp.zeros_like(dg_ref)
            db_ref[...] = jnp.zeros_like(db_ref)

        valid = i >= pb
        err = h_ref[...] - t_ref[...]
        dh = jnp.where(valid, err * (1.0 / D), 0.0)
        part = 0.5 * jnp.sum(jnp.sum(err * err, axis=-1, keepdims=True) * (1.0 / D), axis=0, keepdims=True)
        loss_ref[...] += jnp.where(valid, part, 0.0)
        dpre, dg, db = _ln_bwd_rows(dh, pre_ref[...], g_ref[...], _row_ids(i, tm, D), first_row)
        dpre_ref[...] = dpre
        dg_ref[0:1, :] += dg
        db_ref[0:1, :] += db

    row = lambda i: (i, 0)
    fix = lambda i: (0, 0)
    return pl.pallas_call(
        body, name="loss_head", grid=(L // tm,),
        out_shape=(S((L, D), f32), S((8, D), f32), S((8, D), f32), S((8, LANE), f32)),
        in_specs=[pl.BlockSpec((tm, D), row), pl.BlockSpec((tm, D), lambda i: (jnp.maximum(i - pb, 0), 0)),
                  pl.BlockSpec((tm, D), row), pl.BlockSpec((1, D), fix)],
        out_specs=(pl.BlockSpec((tm, D), row), pl.BlockSpec((8, D), fix), pl.BlockSpec((8, D), fix),
                   pl.BlockSpec((8, LANE), fix)),
        compiler_params=_cp(dimension_semantics=("arbitrary",)))(h, target, pre, g)


def _adamw(g_terms, w, m, v, *, name):
    R, C = w.shape
    tr = _row_tile(R)
    n = len(g_terms)
    c1 = 1.0 - ADAM_B1 ** ADAM_STEP
    c2 = 1.0 - ADAM_B2 ** ADAM_STEP

    def body(*refs):
        g = refs[0][...].astype(f32)
        for r in refs[1:n]:
            g = g + r[...].astype(f32)
        w_ref, m_ref, v_ref, g_out, d_out, m_out, v_out = refs[n:]
        mn = ADAM_B1 * m_ref[...] + (1.0 - ADAM_B1) * g
        vn = ADAM_B2 * v_ref[...] + (1.0 - ADAM_B2) * (g * g)
        g_out[...] = g
        m_out[...] = mn
        v_out[...] = vn
        d_out[...] = -ADAM_LR * ((mn / c1) / (jnp.sqrt(vn / c2) + ADAM_EPS) + ADAM_WD * w_ref[...])

    spec = pl.BlockSpec((tr, C), lambda i: (i, 0))
    return pl.pallas_call(
        body, name=name, grid=(R // tr,), out_shape=(S((R, C), f32),) * 4,
        in_specs=[spec] * (n + 3), out_specs=(spec,) * 4,
        compiler_params=_cp(dimension_semantics=("arbitrary",)))(*g_terms, w, m, v)


def _sum_devices(x):
    n, R, C = x.shape

    def body(x_ref, o_ref):
        acc = x_ref[0]
        for d in range(1, n):
            acc = acc + x_ref[d]
        o_ref[...] = acc

    return pl.pallas_call(body, name="sum_devices", out_shape=S((R, C), f32), compiler_params=_cp())(x)


def _row_tile(R):
    for step in (16, 8):
        for t in range(256, 0, -step):
            if R % t == 0:
                return t
    return R


def _adamw_direct(s32s, recvs, w, m, v, me, *, name):
    L, K, n = w.shape
    tk = _row_tile(K)
    c1 = 1.0 - ADAM_B1 ** ADAM_STEP
    c2 = 1.0 - ADAM_B2 ** ADAM_STEP

    def body(me_ref, *refs):
        own_refs, recv_refs = refs[:L], refs[L:2 * L]
        w_ref, m_ref, v_ref, g_out, d_out, m_out, v_out = refs[2 * L:]
        for li in range(L):
            @pl.when(pl.program_id(0) == li)
            def _(li=li):
                g = own_refs[li][0, 0]
                for d in range(N_DEV):
                    g = g + recv_refs[li][d, 0].astype(f32)
                mn = ADAM_B1 * m_ref[0] + (1.0 - ADAM_B1) * g
                vn = ADAM_B2 * v_ref[0] + (1.0 - ADAM_B2) * (g * g)
                g_out[0] = g
                m_out[0] = mn
                v_out[0] = vn
                d_out[0] = -ADAM_LR * ((mn / c1) / (jnp.sqrt(vn / c2) + ADAM_EPS) + ADAM_WD * w_ref[0])

    own = pl.BlockSpec((1, tk, n), lambda l, i, ix: (l, i, 0))
    grid_spec = pltpu.PrefetchScalarGridSpec(
        num_scalar_prefetch=1, grid=(L, K // tk),
        in_specs=[pl.BlockSpec((1, 1, tk, n), lambda l, i, ix: (ix[0], 0, i, 0))] * L
        + [pl.BlockSpec((N_DEV, 1, tk, n), lambda l, i, ix: (0, 0, i, 0))] * L + [own, own, own],
        out_specs=(own,) * 4)
    return pl.pallas_call(
        body, name=name, grid_spec=grid_spec, out_shape=(S((L, K, n), f32),) * 4,
        compiler_params=_cp(dimension_semantics=("arbitrary", "arbitrary")))(me, *s32s, *recvs, w, m, v)


def _col_segments(n, mapping):
    segs = []
    for p in range(N_DEV):
        lo, hi = p * n, (p + 1) * n
        out = []
        for c0, c1, e0 in mapping:
            a, b = max(lo, c0), min(hi, c1)
            if a < b:
                out.append((a - lo, e0 + (a - c0), b - a))
        segs.append(out)
    return segs


def _assemble_cols(gathered, mapping, n_out, *, name):
    _, L, K, n = gathered.shape
    tk = _row_tile(K)
    segs = _col_segments(n, mapping)
    covered = sum(w for s in segs for (_, _, w) in s)

    def body(g_ref, o_ref):
        if covered != n_out:
            o_ref[...] = jnp.zeros_like(o_ref)
        for p in range(N_DEV):
            for s0, d0, w in segs[p]:
                o_ref[0, :, d0:d0 + w] = g_ref[p, 0, :, s0:s0 + w]

    return pl.pallas_call(
        body, name=name, grid=(L, K // tk), out_shape=S((L, K, n_out), gathered.dtype),
        in_specs=[pl.BlockSpec((N_DEV, 1, tk, n), lambda l, i: (0, l, i, 0))],
        out_specs=pl.BlockSpec((1, tk, n_out), lambda l, i: (l, i, 0)),
        compiler_params=_cp(dimension_semantics=("arbitrary", "arbitrary")))(gathered)


def _split_cols(dws, mapping, n, *, name):
    L = len(dws)
    K, n_in = dws[0].shape
    tk = _row_tile(K)
    segs = _col_segments(n, mapping)

    def body(*refs):
        ins, o32, o16 = refs[:L], refs[L], refs[L + 1]
        for li in range(L):
            @pl.when(pl.program_id(0) == li)
            def _(li=li):
                for p in range(N_DEV):
                    for s0, d0, w in segs[p]:
                        val = ins[li][:, d0:d0 + w]
                        o32[p, 0, :, s0:s0 + w] = val
                        o16[p, 0, :, s0:s0 + w] = _bf(val)

    out = pl.BlockSpec((N_DEV, 1, tk, n), lambda l, i: (0, l, i, 0))
    return pl.pallas_call(
        body, name=name, grid=(L, K // tk), out_shape=(S((N_DEV, L, K, n), f32), S((N_DEV, L, K, n), bf16)),
        in_specs=[pl.BlockSpec((tk, n_in), lambda l, i: (i, 0))] * L, out_specs=(out, out),
        compiler_params=_cp(dimension_semantics=("arbitrary", "arbitrary")))(*dws)


def _split_rows(dws, k, *, name):
    L = len(dws)
    N = dws[0].shape[1]

    def body(*refs):
        ins, o32, o16 = refs[:L], refs[L], refs[L + 1]
        for li in range(L):
            @pl.when(pl.program_id(0) == li)
            def _(li=li):
                val = ins[li][...]
                o32[0, 0] = val
                o16[0, 0] = _bf(val)

    out = pl.BlockSpec((1, 1, k, N), lambda l, p: (p, l, 0, 0))
    return pl.pallas_call(
        body, name=name, grid=(L, N_DEV), out_shape=(S((N_DEV, L, k, N), f32), S((N_DEV, L, k, N), bf16)),
        in_specs=[pl.BlockSpec((k, N), lambda l, p: (p, 0))] * L, out_specs=(out, out),
        compiler_params=_cp(dimension_semantics=("arbitrary", "arbitrary")))(*dws)


def _rows_full(gathered):
    _, L, k, N = gathered.shape
    return jnp.transpose(gathered, (1, 0, 2, 3)).reshape(L, N_DEV * k, N)


def _all_gather(xs, *, name):
    na = len(xs)

    def body(*refs):
        x_refs, out_refs = refs[:na], refs[na:2 * na]
        send_sems, recv_sems, local_sems = refs[2 * na:]
        mx, my, mc = lax.axis_index("x"), lax.axis_index("y"), lax.axis_index("c")
        me, sibling = (mx, my, mc), (mx, my, 1 - mc)
        chips = [(1 - mx, my), (mx, 1 - my), (1 - mx, 1 - my)]

        def slot(a, px, py, pc):
            return out_refs[a].at[4 * px + 2 * py + pc]

        def copy(a, kk, block, to, src=None):
            return pltpu.make_async_remote_copy(
                src_ref=slot(a, *block) if src is None else src, dst_ref=slot(a, *block),
                send_sem=send_sems.at[7 * a + kk], recv_sem=recv_sems.at[7 * a + kk], device_id=to, device_id_type=MESH)

        mine = [pltpu.make_async_copy(x_refs[a], slot(a, *me), local_sems.at[a]) for a in range(na)]
        for cp in mine:
            cp.start()
        first = []
        for a in range(na):
            first.append(copy(a, 0, me, sibling, src=x_refs[a]))
            first += [copy(a, 1 + j, me, (*chip, mc), src=x_refs[a]) for j, chip in enumerate(chips)]
        for cp in first:
            cp.start()
        passed = []
        for j, chip in enumerate(chips):
            for a in range(na):
                copy(a, 1 + j, (*chip, mc), me).wait_recv()
                fwd = copy(a, 4 + j, (*chip, mc), sibling)
                fwd.start()
                passed.append(fwd)
        for a in range(na):
            copy(a, 0, sibling, me).wait_recv()
            for j, chip in enumerate(chips):
                copy(a, 4 + j, (*chip, 1 - mc), me).wait_recv()
        for cp in first + passed:
            cp.wait_send()
        for cp in mine:
            cp.wait()

    return pl.pallas_call(
        body, name=name, out_shape=tuple(S((N_DEV,) + x.shape, x.dtype) for x in xs),
        in_specs=[ANY] * na, out_specs=(ANY,) * na,
        scratch_shapes=[pltpu.SemaphoreType.DMA((7 * na,)), pltpu.SemaphoreType.DMA((7 * na,)),
                        pltpu.SemaphoreType.DMA((na,))],
        compiler_params=pltpu.CompilerParams(has_side_effects=True))(*xs)


_FLIPS = [(fx, fy, fc) for fx in (0, 1) for fy in (0, 1) for fc in (0, 1)][1:]


def _flip_peer(flip):
    x, y, c = lax.axis_index("x"), lax.axis_index("y"), lax.axis_index("c")
    return tuple(1 - a if f else a for a, f in zip((x, y, c), flip))


def _dev_index(p):
    return 4 * p[0] + 2 * p[1] + p[2]


HBM_SPEC = pl.BlockSpec(memory_space=pltpu.HBM)
SEM_SPEC = pl.BlockSpec(memory_space=pltpu.SEMAPHORE)


def _direct_start(srcs, lands, per_peer, *, name):
    na = len(srcs)

    def body(*refs):
        src_refs, land_refs = refs[:na], refs[na:2 * na]
        send_sems, recv_sems = refs[2 * na], refs[2 * na + 1]
        token = refs[-1]
        me = _dev_index((lax.axis_index("x"), lax.axis_index("y"), lax.axis_index("c")))
        for a in range(na):
            for r, flip in enumerate(_FLIPS):
                peer = _flip_peer(flip)
                src = src_refs[a].at[_dev_index(peer)] if per_peer else src_refs[a]
                pltpu.make_async_remote_copy(
                    src_ref=src, dst_ref=land_refs[a].at[me], send_sem=send_sems.at[7 * a + r],
                    recv_sem=recv_sems.at[7 * a + r], device_id=peer, device_id_type=MESH).start()
        token[...] = jnp.zeros_like(token)

    hbm = lambda t: pltpu.with_memory_space_constraint(t, pltpu.HBM)
    out = pl.pallas_call(
        body, name=name,
        out_shape=(pltpu.SemaphoreType.DMA((7 * na,)), pltpu.SemaphoreType.DMA((7 * na,)))
        + tuple(pltpu.HBM(t.shape, t.dtype) for t in list(srcs) + list(lands)) + (S((8, LANE), f32),),
        in_specs=[HBM_SPEC] * (2 * na),
        out_specs=(SEM_SPEC, SEM_SPEC) + (HBM_SPEC,) * (2 * na) + (pl.BlockSpec(memory_space=pltpu.VMEM),),
        input_output_aliases={i: 2 + i for i in range(2 * na)},
        compiler_params=pltpu.CompilerParams(has_side_effects=pltpu.SideEffectType.DATAFLOW_SIDE_EFFECTING))(
            *[hbm(t) for t in srcs], *[hbm(t) for t in lands])
    return out[0], out[1], list(out[2:2 + na]), list(out[2 + na:2 + 2 * na]), out[-1]


def _direct_wait(send_sems, recv_sems, srcs, lands, per_peer, after, *, name):
    na = len(srcs)

    def body(*refs):
        src_refs, land_refs = refs[:na], refs[na:2 * na]
        ssem, rsem = refs[2 * na], refs[2 * na + 1]
        me = _dev_index((lax.axis_index("x"), lax.axis_index("y"), lax.axis_index("c")))
        for a in range(na):
            for r, flip in enumerate(_FLIPS):
                peer = _flip_peer(flip)
                src = src_refs[a].at[_dev_index(peer)] if per_peer else src_refs[a]
                cp = pltpu.make_async_remote_copy(
                    src_ref=src, dst_ref=land_refs[a].at[me], send_sem=ssem.at[7 * a + r],
                    recv_sem=rsem.at[7 * a + r], device_id=peer, device_id_type=MESH)
                cp.wait_send()
                cp.wait_recv()

    out = pl.pallas_call(
        body, name=name, out_shape=tuple(pltpu.HBM(t.shape, t.dtype) for t in list(srcs) + list(lands)),
        in_specs=[HBM_SPEC] * (2 * na) + [SEM_SPEC, SEM_SPEC, ANY], out_specs=(HBM_SPEC,) * (2 * na),
        input_output_aliases={i: i for i in range(2 * na)},
        compiler_params=pltpu.CompilerParams(has_side_effects=pltpu.SideEffectType.DATAFLOW_SIDE_EFFECTING))(
            *srcs, *lands, send_sems, recv_sems, after)
    return list(out[:na]), list(out[na:])


def _pack_small(parts, width):
    rows, offs, r = [], [], 0
    for a in parts:
        n = a.size
        nr = -(-n // width)
        flat = a.reshape(-1).astype(f32)
        if nr * width != n:
            flat = jnp.pad(flat, (0, nr * width - n))
        rows.append(flat.reshape(nr, width))
        offs.append((r, nr))
        r += nr
    buf = jnp.concatenate(rows, axis=0)
    pad = (-r) % 8
    if pad:
        buf = jnp.pad(buf, ((0, pad), (0, 0)))
    return buf, offs


def _unpack_small(buf, off, shape):
    r, nr = off
    return buf[r:r + nr].reshape(-1)[:math.prod(shape)].reshape(shape)


def _local_step(x, target, meta, a_w_in, a_w_out, small, start_token, late_weights, grads_ready):
    SEQ, D = x.shape
    n_meta = meta.shape[0]
    first_row = PADF - n_meta
    H = small["a_log"].shape[-1]

    head = jnp.concatenate([jnp.zeros((first_row, D), f32), meta], axis=0)

    def lanes(a):
        return jnp.pad(a.reshape(1, -1), ((0, 0), (0, LANE - a.size)))

    def after_token(a, token):
        return a if token is None else a + token[0:1, 0:1]

    alog, dtb = after_token(lanes(small["a_log"][0]), start_token), lanes(small["a_dt_bias"][0])
    a_conv, b_conv = small["a_conv"][0], small["b_conv"][0]
    nw = small["a_norm"][0].reshape(1, DH)
    lmg, lmb, lfg, lfb = small["ln_mix_g"], small["ln_mix_b"], small["ln_ffn_g"], small["ln_ffn_b"]

    h0, pre_a, z, raw, q, k, v, beta, g, t_all = _gdn_in_fwd(x, head, a_w_in, a_conv, alog, dtb,
                                                             first_row=first_row, H=H)
    o, y, s_all, pre1, h1 = _delta_fwd(q, k, v, g, beta, t_all, z, nw, h0, a_w_out, lmg[0:1], lmb[0:1],
                                       first_row=first_row, H=H)
    wts = late_weights(h1)
    up0, act0, pre2, h2 = _ffn_fwd(h1, wts["ffn_w_up"], small["ffn_conv"][0], wts["ffn_w_down"],
                                   lfg[0:1], lfb[0:1], layer=0, first_row=first_row, name="ffn_fwd0")
    proj_b, bu, pre3, h3 = _sc_fwd(h2, wts["b_w_in"], b_conv, wts["b_w_out"], lmg[1:2], lmb[1:2], first_row=first_row)
    up1, act1, pre4, h4 = _ffn_fwd(h3, wts["ffn_w_up"], small["ffn_conv"][1], wts["ffn_w_down"],
                                   lfg[1:2], lfb[1:2], layer=1, first_row=first_row, name="ffn_fwd1")
    gs = {}
    dpre4, dlfg1, dlfb1, loss_tile = _loss_head(h4, target, pre4, lfg[1:2], first_row=first_row)

    def ffn_backward(dpre, up, act, h_in, layer, tag, ln_in, token=None):
        dup, dcw, dpre_in, dg, db = _ffn_bwd(
            dpre, up, wts["ffn_w_down"], after_token(small["ffn_conv"][layer], token),
            wts["ffn_w_up"], ln_in[0], ln_in[1], layer=layer, first_row=first_row, name="ffn_bwd" + tag)
        dwd = _linear_dw(act, dpre, name="dw_down" + tag)
        dwu = _linear_dw(h_in, dup, name="dw_up" + tag)
        return dpre_in, dg, db, dwu, dwd, dcw[0:3]

    dpre3, dlmg1, dlmb1, dwu1, dwd1, dcf1 = ffn_backward(dpre4, up1, act1, h3, 1, "1", (pre3, lmg[1:2]))

    dproj_b, dcb, dpre2, dlfg0, dlfb0 = _sc_bwd(dpre3, proj_b, b_conv, wts["b_w_out"], wts["b_w_in"], pre2, lfg[0:1],
                                                first_row=first_row)
    dwb_in = _linear_dw(h2, dproj_b, name="dw_b_in")
    token = grads_ready("layer1", dict(ffn_w_up=dwu1, ffn_w_down=dwd1, b_w_in=dwb_in))

    dpre1, dlmg0, dlmb0, dwu0, dwd0, dcf0 = ffn_backward(dpre2, up0, act0, h1, 0, "0", (pre1, lmg[0:1]), token)
    token = grads_ready("layer0", dict(ffn_w_up=dwu0, ffn_w_down=dwd0))

    dq, dk, dv, dz, dg_, dbeta, dnw = _delta_bwd(dpre1, a_w_out, o, z, after_token(nw, token), q, k, v, g, beta,
                                                 s_all, t_all, H=H)
    dproj_a, dca, dal, ddt, grad_x, dhead = _gdn_in_bwd(dq, dk, dv, dz, dg_, dbeta, pre_a, raw, a_conv, alog, dtb,
                                                        a_w_in, dpre1, first_row=first_row, H=H)
    token = grads_ready("last", dict(a_w_in=_linear_dw(h0, dproj_a, name="dw_a_in")))
    grads_ready("tail", dict(a_w_out=_linear_dw(y, dpre1, name="dw_a_out", after=token),
                             b_w_out=_linear_dw(bu, dpre3, name="dw_b_out", after=token)))

    gs["meta"] = dhead[first_row:PADF]
    gs["a_conv"] = dca[0:a_conv.shape[0]][None]
    gs["a_log"] = dal[0:1, 0:H]
    gs["a_dt_bias"] = ddt[0:1, 0:H]
    gs["a_norm"] = dnw[0:1]
    gs["b_conv"] = dcb[0:b_conv.shape[0]][None]
    gs["ln_mix_g"] = jnp.stack([dlmg0[0], dlmg1[0]])
    gs["ln_mix_b"] = jnp.stack([dlmb0[0], dlmb1[0]])
    gs["ffn_conv"] = jnp.stack([dcf0, dcf1])
    gs["ln_ffn_g"] = jnp.stack([dlfg0[0], dlfg1[0]])
    gs["ln_ffn_b"] = jnp.stack([dlfb0[0], dlfb1[0]])
    return loss_tile, grad_x, gs


_BIG = ("a_w_in", "a_w_out", "b_w_in", "b_w_out", "ffn_w_up", "ffn_w_down")
_BIG_COL = ("a_w_in", "b_w_in", "ffn_w_up")
_SMALL = ("meta", "a_conv", "a_log", "a_dt_bias", "a_norm", "b_conv", "ln_mix_g", "ln_mix_b",
          "ffn_conv", "ln_ffn_g", "ln_ffn_b")
_SMALL_SHARDED = ("meta", "a_conv", "b_conv", "ffn_conv")
_ORDER = ("meta", "a_w_in", "a_conv", "a_log", "a_dt_bias", "a_norm", "a_w_out", "b_w_in", "b_conv", "b_w_out",
          "ln_mix_g", "ln_mix_b", "ffn_w_up", "ffn_conv", "ffn_w_down", "ln_ffn_g", "ln_ffn_b")


def _a_w_in_map(H):
    W4 = 4 * H * DH
    return [(0, W4, 0), (W4, W4 + H, W4), (W4 + H, W4 + 2 * H, W4 + LANE)], W4 + 2 * LANE


def kernel(x, meta, a_w_in, a_conv, a_log, a_dt_bias, a_norm, a_w_out, b_w_in, b_conv, b_w_out, ln_mix_g, ln_mix_b, ffn_w_up, ffn_conv, ffn_w_down, ln_ffn_g, ln_ffn_b, loss_target, m_meta, m_a_w_in, m_a_conv, m_a_log, m_a_dt_bias, m_a_norm, m_a_w_out, m_b_w_in, m_b_conv, m_b_w_out, m_ln_mix_g, m_ln_mix_b, m_ffn_w_up, m_ffn_conv, m_ffn_w_down, m_ln_ffn_g, m_ln_ffn_b, v_meta, v_a_w_in, v_a_conv, v_a_log, v_a_dt_bias, v_a_norm, v_a_w_out, v_b_w_in, v_b_conv, v_b_w_out, v_ln_mix_g, v_ln_mix_b, v_ffn_w_up, v_ffn_conv, v_ffn_w_down, v_ln_ffn_g, v_ln_ffn_b):
    wloc = dict(meta=meta, a_w_in=a_w_in, a_conv=a_conv, a_log=a_log, a_dt_bias=a_dt_bias, a_norm=a_norm,
                a_w_out=a_w_out, b_w_in=b_w_in, b_conv=b_conv, b_w_out=b_w_out, ln_mix_g=ln_mix_g, ln_mix_b=ln_mix_b,
                ffn_w_up=ffn_w_up, ffn_conv=ffn_conv, ffn_w_down=ffn_w_down, ln_ffn_g=ln_ffn_g, ln_ffn_b=ln_ffn_b)
    mloc = dict(meta=m_meta, a_w_in=m_a_w_in, a_conv=m_a_conv, a_log=m_a_log, a_dt_bias=m_a_dt_bias, a_norm=m_a_norm,
                a_w_out=m_a_w_out, b_w_in=m_b_w_in, b_conv=m_b_conv, b_w_out=m_b_w_out, ln_mix_g=m_ln_mix_g,
                ln_mix_b=m_ln_mix_b, ffn_w_up=m_ffn_w_up, ffn_conv=m_ffn_conv, ffn_w_down=m_ffn_w_down,
                ln_ffn_g=m_ln_ffn_g, ln_ffn_b=m_ln_ffn_b)
    vloc = dict(meta=v_meta, a_w_in=v_a_w_in, a_conv=v_a_conv, a_log=v_a_log, a_dt_bias=v_a_dt_bias, a_norm=v_a_norm,
                a_w_out=v_a_w_out, b_w_in=v_b_w_in, b_conv=v_b_conv, b_w_out=v_b_w_out, ln_mix_g=v_ln_mix_g,
                ln_mix_b=v_ln_mix_b, ffn_w_up=v_ffn_w_up, ffn_conv=v_ffn_conv, ffn_w_down=v_ffn_w_down,
                ln_ffn_g=v_ln_ffn_g, ln_ffn_b=v_ln_ffn_b)
    H = a_log.shape[-1]
    mx, my, mc = lax.axis_index("x"), lax.axis_index("y"), lax.axis_index("c")
    me = 4 * mx + 2 * my + mc

    a_map, a_cols = _a_w_in_map(H)
    col_maps = {"a_w_in": (a_map, a_cols)}
    for n in ("b_w_in", "ffn_w_up"):
        ncols = N_DEV * wloc[n].shape[-1]
        col_maps[n] = ([(0, ncols, 0)], ncols)
    sm_sh = [wloc[n] for n in _SMALL_SHARDED]
    sbuf, soffs = _pack_small(sm_sh, 128)
    g_a_w_in, g_a_w_out, sg = _all_gather([_bf(wloc["a_w_in"]), _bf(wloc["a_w_out"]), sbuf], name="gather_first")
    w_a_in = _assemble_cols(g_a_w_in, *col_maps["a_w_in"], name="assemble_a_w_in")[0]
    w_a_out = _rows_full(g_a_w_out)[0]
    late = [n for n in _BIG if n not in ("a_w_in", "a_w_out")]
    ssem, rsem, srcs_t, lands_t, start_token = _direct_start(
        [_bf(wloc[n]) for n in late], [lax.empty((N_DEV,) + wloc[n].shape, bf16) for n in late], False,
        name="gather_rest_start")

    def late_weights(after):
        srcs_d, landed = _direct_wait(ssem, rsem, srcs_t, lands_t, False, after, name="gather_rest_wait")
        wts = {}
        for n, own, got in zip(late, srcs_d, landed):
            full = lax.dynamic_update_index_in_dim(got, own, me, 0)
            if n in _BIG_COL:
                wts[n] = _assemble_cols(full, *col_maps[n], name="assemble_" + n)
            elif n == "ffn_w_down":
                wts[n] = full
            else:
                wts[n] = _rows_full(full)
        for n in ("b_w_in", "b_w_out"):
            wts[n] = wts[n][0]
        return wts

    small = {n: wloc[n] for n in _SMALL}
    for n, off in zip(_SMALL_SHARDED, soffs):
        sh = wloc[n].shape
        parts = jnp.stack([_unpack_small(sg[d], off, sh) for d in range(N_DEV)])
        nd = len(sh)
        small[n] = jnp.transpose(parts, tuple(range(1, nd)) + (0, nd)).reshape(sh[:-1] + (N_DEV * sh[-1],))

    def split(n, dws, tag):
        if n in _BIG_COL:
            return _split_cols(dws, col_maps[n][0], wloc[n].shape[-1], name="split_" + n + tag)
        return _split_rows(dws, wloc[n].shape[-2], name="split_" + n + tag)

    sent = {}

    def grads_ready(stage, grads):
        names = sorted(grads)
        parts = [split(n, [grads[n]], "_" + stage) for n in names]
        handles = _direct_start([p[1] for p in parts], [jnp.zeros(p[1].shape, bf16) for p in parts], True,
                                name="grads_" + stage + "_start")
        sent[stage] = (names, [p[0] for p in parts], handles)
        return handles[4]

    loss_tile, grad_x, gs = _local_step(x[0], loss_target[0], small["meta"], w_a_in, w_a_out, small, start_token,
                                        late_weights, grads_ready)

    def landed(stage, after):
        names, own32, (ssem_g, rsem_g, srcs_g, lands_g, _) = sent[stage]
        _, got = _direct_wait(ssem_g, rsem_g, srcs_g, lands_g, True, after, name="grads_" + stage + "_wait")
        return list(zip(names, own32, got))

    parts = {}
    for stage in ("layer0", "layer1"):
        for n, o32, r in landed(stage, grad_x):
            parts.setdefault(n, []).append((o32, r))
    me1 = jnp.stack([me]).astype(jnp.int32)
    big_out = {n: _adamw_direct([p[0] for p in ps], [p[1] for p in ps], wloc[n], mloc[n], vloc[n], me1,
                                name="adamw_" + n) for n, ps in parts.items()}
    names = list(_SMALL)
    pbuf, poffs = _pack_small([gs[n] for n in names] + [loss_tile[0:1, 0:1]], 1024)
    psum = _sum_devices(_all_gather([pbuf], name="gather_small_grads")[0])
    loss = psum[poffs[-1][0], 0]
    g_small = {}
    for n, off in zip(names, poffs[:-1]):
        full_shape = gs[n].shape
        gfull = _unpack_small(psum, off, full_shape)
        if n in _SMALL_SHARDED:
            ns = wloc[n].shape[-1]
            gfull = lax.dynamic_slice_in_dim(gfull, me * ns, ns, axis=gfull.ndim - 1)
        g_small[n] = gfull.reshape(wloc[n].shape)
    gbuf, aoffs = _pack_small([g_small[n] for n in names], 128)
    wbuf, _ = _pack_small([wloc[n] for n in names], 128)
    mbuf, _ = _pack_small([mloc[n] for n in names], 128)
    vbuf, _ = _pack_small([vloc[n] for n in names], 128)
    _, d_s, m_s, v_s = _adamw([gbuf], wbuf, mbuf, vbuf, name="adamw_small")

    done = d_s[0, 0]
    for out in big_out.values():
        done = done + out[1][0, 0, 0]
    for stage in ("last", "tail"):
        for n, o32, r in landed(stage, done.reshape(1, 1)):
            big_out[n] = _adamw_direct([o32], [r], wloc[n], mloc[n], vloc[n], me1, name="adamw_" + n)

    grads, deltas, new_m, new_v = {}, {}, {}, {}
    for n in _BIG:
        grads[n], deltas[n], new_m[n], new_v[n] = big_out[n]
    for n, off in zip(names, aoffs):
        sh = wloc[n].shape
        grads[n] = g_small[n]
        deltas[n], new_m[n], new_v[n] = (_unpack_small(b_, off, sh) for b_ in (d_s, m_s, v_s))
    return (loss, grad_x[None], *[grads[n] for n in _ORDER], *[deltas[n] for n in _ORDER],
            *[new_m[n] for n in _ORDER], *[new_v[n] for n in _ORDER])
```

```python
import math

import jax
import jax.numpy as jnp
from jax import lax
from jax.experimental import pallas as pl
from jax.experimental.pallas import tpu as pltpu

f32, bf16 = jnp.float32, jnp.bfloat16
S = jax.ShapeDtypeStruct
HI = lax.Precision.HIGHEST
MESH = pl.DeviceIdType.MESH

V7X_VMEM_LIMIT = 56 * 1024 * 1024
LANE = 128
DH = 128
CH = 64
PADF = 256
TM = 256
TMM = 768
N_DEV = 8
BWD_HEAD_GROUP = 4
BWD_GROUP_LAG = 3

DEPTH = 2
ALPHA = (2.0 * DEPTH) ** 0.25
LN_EPS = 1e-5
RMS_EPS = 1e-6
L2_EPS = 1e-6
ADAM_LR, ADAM_B1, ADAM_B2, ADAM_EPS, ADAM_WD, ADAM_STEP = 0.001, 0.9, 0.999, 1e-08, 0.01, 10


def _cp(**kw):
    return pltpu.CompilerParams(vmem_limit_bytes=V7X_VMEM_LIMIT, **kw)


def _bf(x):
    return x.astype(bf16)


def _dot(a, b, precision=None):
    return jnp.dot(a, b, preferred_element_type=f32, precision=precision)


def _dot_nt(a, b):
    return lax.dot_general(a, b, (((1,), (1,)), ((), ())), preferred_element_type=f32)


def _dot_tn(a, b):
    return lax.dot_general(a, b, (((0,), (0,)), ((), ())), preferred_element_type=f32)


def _sigmoid(x):
    return 1.0 / (1.0 + jnp.exp(-x))


def _load_once(pairs, sem):
    @pl.when(pl.program_id(0) == 0)
    def _():
        cps = [pltpu.make_async_copy(src, dst, sem.at[n]) for n, (src, dst) in enumerate(pairs)]
        for c in cps:
            c.start()
        for c in cps:
            c.wait()


def _row_ids(i, tm, width):
    return i * tm + lax.broadcasted_iota(jnp.int32, (tm, width), 0)


def _ln_fwd(pre, g, b, rows, first_row):
    mu = jnp.mean(pre, axis=-1, keepdims=True)
    xc = pre - mu
    var = jnp.mean(xc * xc, axis=-1, keepdims=True)
    y = xc * lax.rsqrt(var + LN_EPS) * g + b
    return jnp.where(rows >= first_row, y, 0.0)


ANY = pl.BlockSpec(memory_space=pl.ANY)


def _taps_back(prev8, x, kw):
    xe = jnp.concatenate([prev8, x], axis=0)
    return [pltpu.roll(xe, kw - 1 - j, 0)[8:] for j in range(kw - 1)] + [x]


def _taps_ahead(x, next8, kw):
    n = x.shape[0]
    xe = jnp.concatenate([x, next8], axis=0)
    return [pltpu.roll(xe, n + 8 - (kw - 1 - j), 0)[:n] for j in range(kw - 1)] + [x]


def _conv(cw, taps):
    acc = cw[0:1, :] * taps[0]
    for j in range(1, len(taps)):
        acc = acc + cw[j:j + 1, :] * taps[j]
    return acc


def _linear_dw(x, dy, *, name, after=None):
    L, K = x.shape
    N = dy.shape[1]
    tm = TMM if L % TMM == 0 else TM
    tn = LANE
    for d in range(N // LANE, 0, -1):
        if (N // LANE) % d == 0 and K * d * LANE * 4 <= 9 * 1024 * 1024:
            tn = d * LANE
            break

    def body(x_ref, dy_ref, *rest):
        o_ref = rest[-1]

        @pl.when(pl.program_id(1) == 0)
        def _():
            o_ref[...] = jnp.zeros_like(o_ref)
        o_ref[...] += _dot_tn(_bf(x_ref[...]), _bf(dy_ref[...]))

    in_specs = [pl.BlockSpec((tm, K), lambda j, i: (i, 0)), pl.BlockSpec((tm, tn), lambda j, i: (i, j))]
    args = [x, dy]
    if after is not None:
        in_specs.append(pl.BlockSpec(after.shape, lambda j, i: (0, 0)))
        args.append(after)
    return pl.pallas_call(
        body, name=name, grid=(N // tn, L // tm), out_shape=S((K, N), f32),
        in_specs=in_specs, out_specs=pl.BlockSpec((K, tn), lambda j, i: (0, j)),
        compiler_params=_cp(dimension_semantics=("arbitrary", "arbitrary")))(*args)


def _ln_bwd_rows(dout, pre, g, rows, first_row):
    mu = jnp.mean(pre, axis=-1, keepdims=True)
    xc = pre - mu
    rstd = lax.rsqrt(jnp.mean(xc * xc, axis=-1, keepdims=True) + LN_EPS)
    xh = xc * rstd
    dy = jnp.where(rows >= first_row, dout, 0.0)
    dxh = dy * g
    dpre = rstd * (dxh - jnp.mean(dxh, axis=-1, keepdims=True) - xh * jnp.mean(dxh * xh, axis=-1, keepdims=True))
    return dpre, jnp.sum(dy * xh, axis=0, keepdims=True), jnp.sum(dy, axis=0, keepdims=True)


def _gdn_in_fwd(x, head, w_full, conv_w, alog, dtb, *, first_row, H):
    D = x.shape[1]
    L = PADF + x.shape[0]
    W = H * DH
    NW = w_full.shape[1]
    KW = conv_w.shape[0]
    tm = TM
    pb = PADF // tm

    def body(x_ref, head_ref, w_hbm, cw_ref, alog_ref, dtb_ref,
             h_ref, pre_ref, z_ref, raw_ref, q_ref, k_ref, v_ref, beta_ref, g_ref, t_ref,
             w_vmem, carry, sem):
        i = pl.program_id(0)
        _load_once([(w_hbm, w_vmem)], sem)

        @pl.when(i == 0)
        def _():
            carry[...] = jnp.zeros_like(carry)

        hv = jnp.where(i < pb, head_ref[...], x_ref[...])
        h_ref[...] = hv
        hb = _bf(hv)
        outs = (q_ref, k_ref, v_ref)

        def section(s):
            pre = _dot(hb, w_vmem[:, s * W:(s + 1) * W])
            pre_ref[:, s * W:(s + 1) * W] = pre
            c = _conv(cw_ref[:, s * W:(s + 1) * W], _taps_back(carry[s], pre, KW))
            carry[s] = pre[tm - 8:tm, :]
            sl = c * _sigmoid(c)
            if s < 2:
                scale = DH ** -0.5 if s == 0 else 1.0
                for hh in range(H):
                    seg = sl[:, hh * DH:(hh + 1) * DH]
                    r = lax.rsqrt(jnp.sum(seg * seg, axis=-1, keepdims=True) + L2_EPS)
                    outs[s][:, hh * DH:(hh + 1) * DH] = seg * (r * scale)
            else:
                v_ref[...] = sl

        raw = _dot(hb, w_vmem[:, 4 * W:4 * W + 2 * LANE])
        raw_ref[...] = raw
        ok = (_row_ids(i, tm, LANE) >= first_row) & (lax.broadcasted_iota(jnp.int32, (tm, LANE), 1) < H)
        beta = jnp.where(ok, _sigmoid(raw[:, :LANE]), 0.0)
        beta_ref[...] = beta
        a = raw[:, LANE:] + dtb_ref[...]
        sp = jnp.maximum(a, 0.0) + jnp.log(1.0 + jnp.exp(-jnp.abs(a)))
        gv = jnp.where(ok, -jnp.exp(alog_ref[...]) * sp, 0.0)
        gam = _dot(_chunk_tri(tm, lower=True), gv, HI)
        g_ref[...] = gam
        section(1)
        ii = lax.broadcasted_iota(jnp.int32, (CH, CH), 0)
        jj = lax.broadcasted_iota(jnp.int32, (CH, CH), 1)
        eye = (ii == jj).astype(f32)
        gam_t = gam.T

        def inverses(chunks):
            ms = []
            for c in chunks:
                rows = slice(c * CH, (c + 1) * CH)
                for hh in range(H):
                    kh = k_ref[rows, hh * DH:(hh + 1) * DH]
                    dec = jnp.exp(jnp.minimum(gam[rows, hh:hh + 1] - gam_t[hh:hh + 1, rows], 0.0))
                    kk = _dot_nt(_bf(kh * beta[rows, hh:hh + 1]), _bf(kh))
                    ms.append(jnp.where(ii > jj, kk * dec, 0.0))
            for n, t in enumerate(_tri_inv_many(ms, eye)):
                t_ref[chunks[n // H], n % H] = t

        nch = tm // CH
        inverses(list(range(nch // 2)))
        section(0)
        inverses(list(range(nch // 2, nch)))
        section(2)
        z_ref[...] = _dot(hb, w_vmem[:, 3 * W:4 * W])

    row = lambda i: (i, 0)
    fix = lambda i: (0, 0)
    out_shape = (S((L, D), f32), S((L, 3 * W), f32), S((L, W), f32), S((L, 2 * LANE), f32),
                 S((L, W), f32), S((L, W), f32), S((L, W), f32), S((L, LANE), f32), S((L, LANE), f32),
                 S((L // CH, H, CH, CH), f32))
    out_specs = (pl.BlockSpec((tm, D), row),
                 pl.BlockSpec((tm, 3 * W), row), pl.BlockSpec((tm, W), row), pl.BlockSpec((tm, 2 * LANE), row),
                 pl.BlockSpec((tm, W), row), pl.BlockSpec((tm, W), row), pl.BlockSpec((tm, W), row),
                 pl.BlockSpec((tm, LANE), row), pl.BlockSpec((tm, LANE), row),
                 pl.BlockSpec((tm // CH, H, CH, CH), lambda i: (i, 0, 0, 0)))
    return pl.pallas_call(
        body, name="gdn_in_fwd", grid=(L // tm,), out_shape=out_shape,
        in_specs=[pl.BlockSpec((tm, D), lambda i: (jnp.maximum(i - pb, 0), 0)),
                  pl.BlockSpec((tm, D), lambda i: (jnp.minimum(i, pb - 1), 0)), ANY, pl.BlockSpec((KW, 3 * W), fix),
                  pl.BlockSpec((1, LANE), fix), pl.BlockSpec((1, LANE), fix)],
        out_specs=out_specs,
        scratch_shapes=[pltpu.VMEM((D, NW), w_full.dtype), pltpu.VMEM((3, 8, W), f32), pltpu.SemaphoreType.DMA((1,))],
        compiler_params=_cp(dimension_semantics=("arbitrary",)))(x, head, w_full, conv_w, alog, dtb)


def _gdn_in_bwd(dq, dk, dv, dz, dg, dbeta, pre, raw, conv_w, alog, dtb, w_full, res, *, first_row, H):
    L = dq.shape[0]
    D = res.shape[1]
    W = H * DH
    KW = conv_w.shape[0]
    tm = TM
    nb = L // tm
    NW = 4 * W + 2 * LANE
    fb = PADF // tm
    alpha = ALPHA

    def body(dq_ref, dk_ref, dv_ref, dz_ref, dg_ref, dbeta_ref, pre_ref, hq_ref, hk_ref, hv_ref, raw_ref,
             cw_ref, alog_ref, dtb_ref, w_hbm, res_ref,
             dproj_ref, dcw_ref, dal_ref, ddt_ref, dx_ref, dfront_ref, w_vmem, carry, tmp, sem):
        i = pl.program_id(0)
        blk = nb - 1 - i
        _load_once([(w_hbm, w_vmem)], sem)

        @pl.when(i == 0)
        def _():
            carry[...] = jnp.zeros_like(carry)
            dcw_ref[...] = jnp.zeros_like(dcw_ref)
            dal_ref[...] = jnp.zeros_like(dal_ref)
            ddt_ref[...] = jnp.zeros_like(ddt_ref)

        halos = (hq_ref, hk_ref, hv_ref)
        douts = (dq_ref, dk_ref, dv_ref)
        for s in range(3):
            sec = slice(s * W, (s + 1) * W)
            pre = pre_ref[:, sec]
            c = _conv(cw_ref[:, sec], _taps_back(jnp.where(blk > 0, halos[s][...], 0.0), pre, KW))
            sig = _sigmoid(c)
            sl = c * sig
            if s < 2:
                scale = DH ** -0.5 if s == 0 else 1.0
                for hh in range(H):
                    hs = slice(hh * DH, (hh + 1) * DH)
                    seg = sl[:, hs]
                    r = lax.rsqrt(jnp.sum(seg * seg, axis=-1, keepdims=True) + L2_EPS)
                    n = seg * r
                    dqs = douts[s][:, hs]
                    tmp[:, hs] = (scale * r) * (dqs - n * jnp.sum(n * dqs, axis=-1, keepdims=True))
                dsl = tmp[...]
            else:
                dsl = dv_ref[...]
            dc = dsl * (sig * (1.0 + c * (1.0 - sig)))
            ahead = _taps_ahead(dc, carry[s], KW)
            carry[s] = dc[0:8, :]
            dproj_ref[:, sec] = _bf(_conv(cw_ref[:, sec], ahead))
            for j in range(KW):
                dcw_ref[j:j + 1, sec] += jnp.sum(ahead[j] * pre, axis=0, keepdims=True)
        dproj_ref[:, 3 * W:4 * W] = _bf(dz_ref[...])
        raw_v = raw_ref[...]
        ok = (_row_ids(blk, tm, LANE) >= first_row) & (lax.broadcasted_iota(jnp.int32, (tm, LANE), 1) < H)
        beta = _sigmoid(raw_v[:, :LANE])
        dbraw = jnp.where(ok, dbeta_ref[...] * beta * (1.0 - beta), 0.0)
        a = raw_v[:, LANE:] + dtb_ref[...]
        sp = jnp.maximum(a, 0.0) + jnp.log(1.0 + jnp.exp(-jnp.abs(a)))
        nea = -jnp.exp(alog_ref[...])
        dgm = jnp.where(ok, _dot(_chunk_tri(tm, lower=False), dg_ref[...], HI), 0.0)
        daraw = dgm * nea * _sigmoid(a)
        dal_ref[0:1, :] += jnp.sum(dgm * nea * sp, axis=0, keepdims=True)
        ddt_ref[0:1, :] += jnp.sum(daraw, axis=0, keepdims=True)
        dproj_ref[:, 4 * W:4 * W + LANE] = _bf(dbraw)
        dproj_ref[:, 4 * W + LANE:4 * W + 2 * LANE] = _bf(daraw)
        dh = alpha * res_ref[...] + _dot_nt(dproj_ref[...], w_vmem[...])

        @pl.when(blk >= fb)
        def _():
            dx_ref[...] = dh

        @pl.when(blk < fb)
        def _():
            dfront_ref[...] = dh

    rev = lambda i: (nb - 1 - i, 0)
    fix = lambda i: (0, 0)

    def halo(col):
        return pl.BlockSpec((8, W), lambda i: (jnp.maximum((nb - 1 - i) * (tm // 8) - 1, 0), col))

    return pl.pallas_call(
        body, name="gdn_in_bwd", grid=(nb,),
        out_shape=(S((L, NW), bf16), S((8, 3 * W), f32), S((8, LANE), f32), S((8, LANE), f32),
                   S((L - PADF, D), f32), S((PADF, D), f32)),
        in_specs=[pl.BlockSpec((tm, W), rev)] * 4 + [pl.BlockSpec((tm, LANE), rev)] * 2
        + [pl.BlockSpec((tm, 3 * W), rev), halo(0), halo(1), halo(2), pl.BlockSpec((tm, 2 * LANE), rev),
           pl.BlockSpec((KW, 3 * W), fix), pl.BlockSpec((1, LANE), fix), pl.BlockSpec((1, LANE), fix),
           ANY, pl.BlockSpec((tm, D), rev)],
        out_specs=(pl.BlockSpec((tm, NW), rev), pl.BlockSpec((8, 3 * W), fix),
                   pl.BlockSpec((8, LANE), fix), pl.BlockSpec((8, LANE), fix),
                   pl.BlockSpec((tm, D), lambda i: (jnp.maximum(nb - 1 - i - fb, 0), 0)),
                   pl.BlockSpec((tm, D), lambda i: (jnp.minimum(nb - 1 - i, fb - 1), 0))),
        scratch_shapes=[pltpu.VMEM((D, NW), w_full.dtype), pltpu.VMEM((3, 8, W), f32), pltpu.VMEM((tm, W), f32),
                        pltpu.SemaphoreType.DMA((1,))],
        compiler_params=_cp(dimension_semantics=("arbitrary",)))(
            dq, dk, dv, dz, dg, dbeta, pre, pre, pre, pre, raw, conv_w, alog, dtb, w_full, res)


def _chunk_tri(n, lower):
    i = lax.broadcasted_iota(jnp.int32, (n, n), 0)
    j = lax.broadcasted_iota(jnp.int32, (n, n), 1)
    sh = int(math.log2(CH))
    same = lax.shift_right_logical(i, sh) == lax.shift_right_logical(j, sh)
    return (same & ((i >= j) if lower else (j >= i))).astype(f32)


def _tri_inv_many(ms, eye):
    ts = [eye - m for m in ms]
    ps = list(ms)
    for _ in range(int(math.log2(CH)) - 1):
        pb = [_bf(p) for p in ps]
        ps = [_dot(p, p) for p in pb]
        ts = [t + _dot(_bf(t), _bf(p)) for t, p in zip(ts, ps)]
    return ts


def _chunk_local(q, k, v, gcol, grow, glast, bcol, ii, jj):
    dec = jnp.where(ii >= jj, jnp.exp(jnp.minimum(gcol - grow, 0.0)), 0.0)
    eg = jnp.exp(gcol)
    kb = k * bcol
    kbg = kb * eg
    vb = v * bcol
    qt = q * eg
    kt = k * jnp.exp(glast - gcol)
    kbb, qb, kbf = _bf(kb), _bf(q), _bf(k)
    return dec, eg, kb, kbg, vb, qt, kt, _dot_nt(kbb, kbf), _dot_nt(qb, kbf), jnp.concatenate([kbb, qb], axis=0)


def _delta_fwd(q, k, v, g, beta, t_all, z, nw, h, w_out, ln_g, ln_b, *, first_row, H):
    L = q.shape[0]
    W = H * DH
    D = h.shape[1]
    rb = TM
    nc = rb // CH
    nblk = L // rb
    alpha = ALPHA

    def body(q_ref, k_ref, v_ref, g_ref, b_ref, t_ref, z_ref, nw_ref, h_ref, wout_hbm, lg_ref, lb_ref,
             o_ref, y_ref, s_out, pre_ref, out_ref, s_scr, wout, sem):
        _load_once([(wout_hbm, wout)], sem)

        @pl.when(pl.program_id(0) == 0)
        def _():
            s_scr[...] = jnp.zeros_like(s_scr)

        ii = lax.broadcasted_iota(jnp.int32, (CH, CH), 0)
        jj = lax.broadcasted_iota(jnp.int32, (CH, CH), 1)
        eye = (ii == jj).astype(f32)
        nwv = nw_ref[...]

        heads = range(H)
        hsl = [slice(hh * DH, (hh + 1) * DH) for hh in heads]

        def chunk(c, carry):
            r0 = pl.multiple_of(c * CH, CH)
            rows = pl.ds(r0, CH)
            gam = g_ref[rows, :]
            gam_t = gam.T
            bb = b_ref[rows, :]
            glast = [gam[CH - 1:CH, hh:hh + 1] for hh in heads]
            loc = [_chunk_local(q_ref[rows, hsl[hh]], k_ref[rows, hsl[hh]], v_ref[rows, hsl[hh]],
                                gam[:, hh:hh + 1], gam_t[hh:hh + 1, :], glast[hh], bb[:, hh:hh + 1], ii, jj)
                   for hh in heads]
            st = [s_scr[hh] for hh in heads]
            zs = [z_ref[rows, hsl[hh]] for hh in heads]
            ts = [_bf(t_ref[c, hh]) for hh in heads]
            us = [_dot(t, _bf(l[4])) for t, l in zip(ts, loc)]
            ws = [_dot(t, _bf(l[3])) for t, l in zip(ts, loc)]
            stb = [_bf(s) for s in st]
            vn = [u - _dot(_bf(w), sb) for u, w, sb in zip(us, ws, stb)]
            vnb = [_bf(x) for x in vn]
            snew = [s * jnp.exp(gl) + _dot_tn(_bf(l[6]), xb) for s, gl, l, xb in zip(st, glast, loc, vnb)]
            os_ = [_dot(_bf(l[5]), sb) + _dot(_bf(l[8] * l[0]), xb) for l, sb, xb in zip(loc, stb, vnb)]
            for hh in heads:
                o = os_[hh]
                s_out[c, hh] = st[hh]
                s_scr[hh] = snew[hh]
                o_ref[rows, hsl[hh]] = o
                on = o * lax.rsqrt(jnp.mean(o * o, axis=-1, keepdims=True) + RMS_EPS) * nwv
                y_ref[rows, hsl[hh]] = _bf(on * (zs[hh] * _sigmoid(zs[hh])))
            return carry

        lax.fori_loop(0, nc, chunk, 0)
        pre = alpha * h_ref[...] + _dot(y_ref[...], wout[...])
        pre_ref[...] = pre
        out_ref[...] = _ln_fwd(pre, lg_ref[...], lb_ref[...], _row_ids(pl.program_id(0), rb, D), first_row)

    row = lambda i: (i, 0)
    fix = lambda i: (0, 0)
    return pl.pallas_call(
        body, name="delta_fwd", grid=(nblk,),
        out_shape=(S((L, W), f32), S((L, W), bf16), S((L // CH, H, DH, DH), f32), S((L, D), f32), S((L, D), f32)),
        in_specs=[pl.BlockSpec((rb, W), row)] * 3 + [pl.BlockSpec((rb, LANE), row)] * 2
        + [pl.BlockSpec((nc, H, CH, CH), lambda i: (i, 0, 0, 0)),
           pl.BlockSpec((rb, W), row), pl.BlockSpec((1, DH), fix), pl.BlockSpec((rb, D), row), ANY,
           pl.BlockSpec((1, D), fix), pl.BlockSpec((1, D), fix)],
        out_specs=(pl.BlockSpec((rb, W), row), pl.BlockSpec((rb, W), row),
                   pl.BlockSpec((nc, H, DH, DH), lambda i: (i, 0, 0, 0)),
                   pl.BlockSpec((rb, D), row), pl.BlockSpec((rb, D), row)),
        scratch_shapes=[pltpu.VMEM((H, DH, DH), f32), pltpu.VMEM((W, D), w_out.dtype), pltpu.SemaphoreType.DMA((1,))],
        compiler_params=_cp(dimension_semantics=("arbitrary",)))(q, k, v, g, beta, t_all, z, nw, h, w_out, ln_g, ln_b)


def _delta_bwd(dpre, w_out, o, z, nw, q, k, v, g, beta, s_all, t_all, *, H):
    L = q.shape[0]
    W = H * DH
    D = dpre.shape[1]
    rb = TM
    nc = rb // CH
    nblk = L // rb

    def body(dpre_ref, wout_hbm, o_ref, z_ref, nw_ref, q_ref, k_ref, v_ref, g_ref, b_ref, s_ref, t_ref,
             dq_ref, dk_ref, dv_ref, dz_ref, dg_ref, db_ref, dnw_ref, ds_scr, wout, dy_scr, sem):
        _load_once([(wout_hbm, wout)], sem)

        @pl.when(pl.program_id(0) == 0)
        def _():
            ds_scr[...] = jnp.zeros_like(ds_scr)
            dnw_ref[...] = jnp.zeros_like(dnw_ref)

        dy_scr[...] = _dot_nt(_bf(dpre_ref[...]), wout[...])

        ii = lax.broadcasted_iota(jnp.int32, (CH, CH), 0)
        jj = lax.broadcasted_iota(jnp.int32, (CH, CH), 1)
        lane = lax.broadcasted_iota(jnp.int32, (CH, LANE), 1)
        last_row = lax.broadcasted_iota(jnp.int32, (CH, 1), 0) == CH - 1
        nwv = nw_ref[...]

        def chunk(cc, carry):
            c = nc - 1 - cc
            r0 = pl.multiple_of(c * CH, CH)
            rows = pl.ds(r0, CH)
            gam = g_ref[rows, :]
            gam_t = gam.T
            bb = b_ref[rows, :]

            def head(hh):
                hs = slice(hh * DH, (hh + 1) * DH)
                gcol, grow, glast = gam[:, hh:hh + 1], gam_t[hh:hh + 1, :], gam[CH - 1:CH, hh:hh + 1]
                bcol = bb[:, hh:hh + 1]
                qh, kh, vh = q_ref[rows, hs], k_ref[rows, hs], v_ref[rows, hs]
                oh, zh, dyh = o_ref[rows, hs], z_ref[rows, hs], dy_scr[rows, hs]
                t = t_ref[c, hh]
                st = s_ref[c, hh]
                dsn = ds_scr[hh]
                rms = lax.rsqrt(jnp.mean(oh * oh, axis=-1, keepdims=True) + RMS_EPS)
                on = oh * rms
                sig = _sigmoid(zh)
                sz = zh * sig
                dz_ref[rows, hs] = dyh * on * nwv * (sig * (1.0 + zh * (1.0 - sig)))
                dnw = jnp.sum(dyh * on * sz, axis=0, keepdims=True)
                don = dyh * nwv * sz
                do = rms * (don - on * jnp.mean(don * on, axis=-1, keepdims=True))
                dec, eg, kb, kbg, vb, qt, kt, kk, qk, kqb = _chunk_local(qh, kh, vh, gcol, grow, glast, bcol, ii, jj)
                stb, dsnb, dob, tb, kbgb = _bf(st), _bf(dsn), _bf(do), _bf(t), _bf(kbg)
                r = vb - _dot(kbgb, stb)
                mm = jnp.where(ii > jj, kk * dec, 0.0)
                attn = qk * dec
                yield
                rbf = _bf(r)
                vn = _dot(tb, rbf)
                dvn = _dot_tn(_bf(attn), dob) + _dot(_bf(kt), dsnb)
                egl = jnp.exp(glast)
                ekt = jnp.exp(glast - gcol)
                yield
                vnb, dvnb = _bf(vn), _bf(dvn)
                dattn = jnp.where(ii >= jj, _dot_nt(dob, vnb), 0.0)
                dkt = _dot_nt(vnb, dsnb)
                dvb = _dot_tn(tb, dvnb)
                dt = _dot_nt(dvnb, rbf)
                dglast = egl * jnp.sum(jnp.sum(dsn * st, axis=0, keepdims=True), axis=1, keepdims=True)
                yield
                dv_ref[rows, hs] = dvb * bcol
                dod = jnp.concatenate([dob, -_bf(dvb)], axis=0)
                ds_scr[hh] = egl * dsn + _dot_tn(jnp.concatenate([_bf(qt), kbgb], axis=0), dod)
                both = _dot_nt(dod, stb)
                dqt, dkbg = both[:CH], both[CH:]
                x = _dot_nt(_bf(dt), tb)
                yield
                dm = jnp.where(ii > jj, -_dot_tn(tb, _bf(x)), 0.0)
                dkk = dm * dec
                dqk = dattn * dec
                e = dm * mm + dattn * attn
                dgam = jnp.sum(e, axis=1, keepdims=True) - jnp.sum(e.T, axis=1, keepdims=True)
                dd = _bf(jnp.concatenate([dkk, dqk], axis=0))
                both = _dot(dd, _bf(kh))
                dkb = both[:CH] + dkbg * eg
                yield
                dk_ref[rows, hs] = _dot_tn(dd, kqb) + dkt * ekt + dkb * bcol
                dq_ref[rows, hs] = both[CH:] + dqt * eg
                yield
                dktkt = dkt * kt
                dgam = dgam + jnp.sum(dqt * qt - dktkt + dkbg * kbg, axis=1, keepdims=True)
                dglast = dglast + jnp.sum(jnp.sum(dktkt, axis=0, keepdims=True), axis=1, keepdims=True)
                dgam = dgam + jnp.where(last_row, dglast, 0.0)
                dbeta = jnp.sum(dkb * kh + dvb * vh, axis=1, keepdims=True)
                return dgam, dbeta, dnw

            res = [None] * H
            gens = [head(hh) for hh in range(H)]
            step = 0
            while any(r is None for r in res):
                for hh in range(H):
                    if res[hh] is None and step >= (hh // BWD_HEAD_GROUP) * BWD_GROUP_LAG:
                        try:
                            next(gens[hh])
                        except StopIteration as stop:
                            res[hh] = stop.value
                step += 1
            dgam_all = jnp.zeros((CH, LANE), f32)
            dbeta_all = jnp.zeros((CH, LANE), f32)
            dnw_acc = jnp.zeros((1, DH), f32)
            for hh in range(H):
                dgam, dbeta, dnw = res[hh]
                dgam_all = dgam_all + jnp.where(lane == hh, dgam, 0.0)
                dbeta_all = dbeta_all + jnp.where(lane == hh, dbeta, 0.0)
                dnw_acc = dnw_acc + dnw
            dg_ref[rows, :] = dgam_all
            db_ref[rows, :] = dbeta_all
            dnw_ref[0:1, :] += dnw_acc
            return carry

        lax.fori_loop(0, nc, chunk, 0)

    rev = lambda i: (nblk - 1 - i, 0)
    rev4 = lambda i: (nblk - 1 - i, 0, 0, 0)
    fix = lambda i: (0, 0)
    wide = pl.BlockSpec((rb, W), rev)
    thin = pl.BlockSpec((rb, LANE), rev)
    return pl.pallas_call(
        body, name="delta_bwd", grid=(nblk,),
        out_shape=(S((L, W), f32),) * 4 + (S((L, LANE), f32),) * 2 + (S((8, DH), f32),),
        in_specs=[pl.BlockSpec((rb, D), rev), ANY, wide, wide, pl.BlockSpec((1, DH), fix), wide, wide, wide, thin, thin,
                  pl.BlockSpec((nc, H, DH, DH), rev4), pl.BlockSpec((nc, H, CH, CH), rev4)],
        out_specs=(wide,) * 4 + (thin, thin, pl.BlockSpec((8, DH), fix)),
        scratch_shapes=[pltpu.VMEM((H, DH, DH), f32), pltpu.VMEM((W, D), w_out.dtype), pltpu.VMEM((rb, W), f32),
                        pltpu.SemaphoreType.DMA((1,))],
        compiler_params=_cp(dimension_semantics=("arbitrary",)))(
            dpre, w_out, o, z, nw, q, k, v, g, beta, s_all, t_all)


def _sc_fwd(h, w_in, conv_w, w_out, g, b, *, first_row):
    L, D = h.shape
    W = w_out.shape[0]
    KW = conv_w.shape[0]
    tm = TM
    alpha = ALPHA

    def body(h_ref, win_hbm, cw_ref, wout_hbm, g_ref, b_ref, proj_ref, bu_ref, pre_ref, out_ref,
             win, wout, carry, sem):
        i = pl.program_id(0)
        _load_once([(win_hbm, win), (wout_hbm, wout)], sem)

        @pl.when(i == 0)
        def _():
            carry[...] = jnp.zeros_like(carry)

        hv = h_ref[...]
        hb = _bf(hv)
        bg = _dot(hb, win[:, 0:W])
        cg = _dot(hb, win[:, W:2 * W])
        xv = _dot(hb, win[:, 2 * W:3 * W])
        proj_ref[:, 0:W] = bg
        proj_ref[:, W:2 * W] = cg
        proj_ref[:, 2 * W:3 * W] = xv
        p = cg * xv
        u = _conv(cw_ref[...], _taps_back(carry[...], p, KW))
        carry[...] = p[tm - 8:tm, :]
        bu = _bf(bg * u)
        bu_ref[...] = bu
        pre = alpha * hv + _dot(bu, wout[...])
        pre_ref[...] = pre
        out_ref[...] = _ln_fwd(pre, g_ref[...], b_ref[...], _row_ids(i, tm, D), first_row)

    row = lambda i: (i, 0)
    fix = lambda i: (0, 0)
    return pl.pallas_call(
        body, name="sc_fwd", grid=(L // tm,),
        out_shape=(S((L, 3 * W), f32), S((L, W), bf16), S((L, D), f32), S((L, D), f32)),
        in_specs=[pl.BlockSpec((tm, D), row), ANY, pl.BlockSpec((KW, W), fix), ANY,
                  pl.BlockSpec((1, D), fix), pl.BlockSpec((1, D), fix)],
        out_specs=(pl.BlockSpec((tm, 3 * W), row), pl.BlockSpec((tm, W), row),
                   pl.BlockSpec((tm, D), row), pl.BlockSpec((tm, D), row)),
        scratch_shapes=[pltpu.VMEM((D, 3 * W), w_in.dtype), pltpu.VMEM((W, D), w_out.dtype),
                        pltpu.VMEM((8, W), f32), pltpu.SemaphoreType.DMA((2,))],
        compiler_params=_cp(dimension_semantics=("arbitrary",)))(h, w_in, conv_w, w_out, g, b)


def _sc_bwd(dpre, proj, conv_w, w_out, w_in, pre_in, g_in, *, first_row):
    L, D = dpre.shape
    W = w_out.shape[0]
    KW = conv_w.shape[0]
    tm = TM
    nb = L // tm
    alpha = ALPHA

    def body(dpre_ref, proj_ref, hc_ref, hx_ref, cw_ref, wout_hbm, win_hbm, pin_ref, g_ref,
             dproj_ref, dcw_ref, dpin_ref, dg_ref, db_ref, wout, win, carry, sem):
        i = pl.program_id(0)
        blk = nb - 1 - i
        _load_once([(wout_hbm, wout), (win_hbm, win)], sem)

        @pl.when(i == 0)
        def _():
            carry[...] = jnp.zeros_like(carry)
            dcw_ref[...] = jnp.zeros_like(dcw_ref)
            dg_ref[...] = jnp.zeros_like(dg_ref)
            db_ref[...] = jnp.zeros_like(db_ref)

        bg, cg, xv = proj_ref[:, 0:W], proj_ref[:, W:2 * W], proj_ref[:, 2 * W:3 * W]
        p = cg * xv
        u = _conv(cw_ref[...], _taps_back(jnp.where(blk > 0, hc_ref[...] * hx_ref[...], 0.0), p, KW))
        dpre_v = dpre_ref[...]
        d = _dot_nt(_bf(dpre_v), wout[...])
        dproj_ref[:, 0:W] = _bf(d * u)
        du = d * bg
        ahead = _taps_ahead(du, carry[...], KW)
        carry[...] = du[0:8, :]
        dp = _conv(cw_ref[...], ahead)
        for j in range(KW):
            dcw_ref[j:j + 1, :] += jnp.sum(ahead[j] * p, axis=0, keepdims=True)
        dproj_ref[:, W:2 * W] = _bf(dp * xv)
        dproj_ref[:, 2 * W:3 * W] = _bf(dp * cg)
        dh = alpha * dpre_v + _dot_nt(dproj_ref[...], win[...])
        dpin, dg, dbias = _ln_bwd_rows(dh, pin_ref[...], g_ref[...], _row_ids(blk, tm, D), first_row)
        dpin_ref[...] = dpin
        dg_ref[0:1, :] += dg
        db_ref[0:1, :] += dbias

    rev = lambda i: (nb - 1 - i, 0)
    fix = lambda i: (0, 0)

    def halo(col):
        return pl.BlockSpec((8, W), lambda i: (jnp.maximum((nb - 1 - i) * (tm // 8) - 1, 0), col))

    return pl.pallas_call(
        body, name="sc_bwd", grid=(nb,),
        out_shape=(S((L, 3 * W), bf16), S((8, W), f32), S((L, D), f32), S((8, D), f32), S((8, D), f32)),
        in_specs=[pl.BlockSpec((tm, D), rev), pl.BlockSpec((tm, 3 * W), rev), halo(1), halo(2),
                  pl.BlockSpec((KW, W), fix), ANY, ANY, pl.BlockSpec((tm, D), rev), pl.BlockSpec((1, D), fix)],
        out_specs=(pl.BlockSpec((tm, 3 * W), rev), pl.BlockSpec((8, W), fix), pl.BlockSpec((tm, D), rev),
                   pl.BlockSpec((8, D), fix), pl.BlockSpec((8, D), fix)),
        scratch_shapes=[pltpu.VMEM((W, D), w_out.dtype), pltpu.VMEM((D, 3 * W), w_in.dtype), pltpu.VMEM((8, W), f32),
                        pltpu.SemaphoreType.DMA((2,))],
        compiler_params=_cp(dimension_semantics=("arbitrary",)))(
            dpre, proj, proj, proj, conv_w, w_out, w_in, pre_in, g_in)


def _ffn_cols(F):
    fc = F
    for cand in (1408, 1024, 512, 256, 128):
        if F % cand == 0:
            fc = cand
            break
    return fc


def _ffn_weight_copies(wup_hbm, wdn_hbm, wup, wdn, layer):
    k = wdn_hbm.shape[2]
    return [(wup_hbm.at[layer], wup)] + [(wdn_hbm.at[p, layer], wdn.at[pl.ds(p * k, k), :]) for p in range(N_DEV)]


def _ffn_fwd(h, w_up, conv_w, w_down, g, b, *, layer, first_row, name):
    L, D = h.shape
    F = N_DEV * w_down.shape[2]
    KW = conv_w.shape[0]
    tm = TM
    fc = _ffn_cols(F)
    alpha = ALPHA

    def body(h_ref, wup_hbm, cw_ref, wdn_hbm, g_ref, b_ref, up_ref, a_ref, pre_ref, out_ref,
             wup, wdn, carry, sem):
        i = pl.program_id(0)
        _load_once(_ffn_weight_copies(wup_hbm, wdn_hbm, wup, wdn, layer), sem)

        @pl.when(i == 0)
        def _():
            carry[...] = jnp.zeros_like(carry)

        hv = h_ref[...]
        hb = _bf(hv)
        pre = alpha * hv
        for c0 in range(0, F, fc):
            cs = slice(c0, c0 + fc)
            u = _dot(hb, wup[:, cs])
            gate = _dot(hb, wup[:, F + c0:F + c0 + fc])
            up_ref[:, cs] = u
            up_ref[:, F + c0:F + c0 + fc] = gate
            uc = _conv(cw_ref[:, cs], _taps_back(carry[:, cs], u, KW))
            carry[:, cs] = u[tm - 8:tm, :]
            ab = _bf(uc * _sigmoid(uc) * gate)
            a_ref[:, cs] = ab
            pre = pre + _dot(ab, wdn[cs, :])
        pre_ref[...] = pre
        out_ref[...] = _ln_fwd(pre, g_ref[...], b_ref[...], _row_ids(i, tm, D), first_row)

    row = lambda i: (i, 0)
    fix = lambda i: (0, 0)
    return pl.pallas_call(
        body, name=name, grid=(L // tm,),
        out_shape=(S((L, 2 * F), f32), S((L, F), bf16), S((L, D), f32), S((L, D), f32)),
        in_specs=[pl.BlockSpec((tm, D), row), ANY, pl.BlockSpec((KW, F), fix), ANY,
                  pl.BlockSpec((1, D), fix), pl.BlockSpec((1, D), fix)],
        out_specs=(pl.BlockSpec((tm, 2 * F), row), pl.BlockSpec((tm, F), row),
                   pl.BlockSpec((tm, D), row), pl.BlockSpec((tm, D), row)),
        scratch_shapes=[pltpu.VMEM((D, 2 * F), w_up.dtype), pltpu.VMEM((F, D), w_down.dtype),
                        pltpu.VMEM((8, F), f32), pltpu.SemaphoreType.DMA((1 + N_DEV,))],
        compiler_params=_cp(dimension_semantics=("arbitrary",)))(h, w_up, conv_w, w_down, g, b)


def _ffn_bwd(dpre, up, w_down, conv_w, w_up, pre_in, g_in, *, layer, first_row, name):
    L, D = dpre.shape
    F = N_DEV * w_down.shape[2]
    KW = conv_w.shape[0]
    tm = TM
    nb = L // tm
    fc = F
    alpha = ALPHA

    def body(dpre_ref, up_ref, halo_ref, wdn_hbm, cw_ref, wup_hbm, pin_ref, g_ref,
             dup_ref, dcw_ref, dpin_ref, dg_ref, db_ref, wdn, wup, carry, sem):
        i = pl.program_id(0)
        blk = nb - 1 - i
        _load_once(_ffn_weight_copies(wup_hbm, wdn_hbm, wup, wdn, layer), sem)

        @pl.when(i == 0)
        def _():
            carry[...] = jnp.zeros_like(carry)
            dcw_ref[...] = jnp.zeros_like(dcw_ref)
            dg_ref[...] = jnp.zeros_like(dg_ref)
            db_ref[...] = jnp.zeros_like(db_ref)

        dpre_v = dpre_ref[...]
        db = _bf(dpre_v)
        dh = alpha * dpre_v
        for c0 in range(0, F, fc):
            cs = slice(c0, c0 + fc)
            gs_ = slice(F + c0, F + c0 + fc)
            da = _dot_nt(db, wdn[cs, :])
            gate = up_ref[:, gs_]
            u = up_ref[:, cs]
            uc = _conv(cw_ref[:, cs], _taps_back(jnp.where(blk > 0, halo_ref[:, cs], 0.0), u, KW))
            sig = _sigmoid(uc)
            dgate = _bf(da * (uc * sig))
            dup_ref[:, gs_] = dgate
            duc = da * gate * (sig * (1.0 + uc * (1.0 - sig)))
            ahead = _taps_ahead(duc, carry[:, cs], KW)
            carry[:, cs] = duc[0:8, :]
            du = _bf(_conv(cw_ref[:, cs], ahead))
            dup_ref[:, cs] = du
            for j in range(KW):
                dcw_ref[j:j + 1, cs] += jnp.sum(ahead[j] * u, axis=0, keepdims=True)
            dh = dh + _dot_nt(du, wup[:, cs]) + _dot_nt(dgate, wup[:, gs_])
        dpin, dg, dbias = _ln_bwd_rows(dh, pin_ref[...], g_ref[...], _row_ids(blk, tm, D), first_row)
        dpin_ref[...] = dpin
        dg_ref[0:1, :] += dg
        db_ref[0:1, :] += dbias

    rev = lambda i: (nb - 1 - i, 0)
    fix = lambda i: (0, 0)
    return pl.pallas_call(
        body, name=name, grid=(nb,),
        out_shape=(S((L, 2 * F), bf16), S((8, F), f32), S((L, D), f32), S((8, D), f32), S((8, D), f32)),
        in_specs=[pl.BlockSpec((tm, D), rev), pl.BlockSpec((tm, 2 * F), rev),
                  pl.BlockSpec((8, F), lambda i: (jnp.maximum((nb - 1 - i) * (tm // 8) - 1, 0), 0)),
                  ANY, pl.BlockSpec((KW, F), fix), ANY, pl.BlockSpec((tm, D), rev), pl.BlockSpec((1, D), fix)],
        out_specs=(pl.BlockSpec((tm, 2 * F), rev), pl.BlockSpec((8, F), fix), pl.BlockSpec((tm, D), rev),
                   pl.BlockSpec((8, D), fix), pl.BlockSpec((8, D), fix)),
        scratch_shapes=[pltpu.VMEM((F, D), w_down.dtype), pltpu.VMEM((D, 2 * F), w_up.dtype), pltpu.VMEM((8, F), f32),
                        pltpu.SemaphoreType.DMA((1 + N_DEV,))],
        compiler_params=_cp(dimension_semantics=("arbitrary",)))(dpre, up, up, w_down, conv_w, w_up, pre_in, g_in)


def _loss_head(h, target, pre, g, *, first_row):
    L, D = h.shape
    tm = TM
    pb = PADF // tm

    def body(h_ref, t_ref, pre_ref, g_ref, dpre_ref, dg_ref, db_ref, loss_ref):
        i = pl.program_id(0)

        @pl.when(i == 0)
        def _():
            loss_ref[...] = jnp.zeros_like(loss_ref)
            dg_ref[...] = jnp.zeros_like(dg_ref)
            db_ref[...] = jnp.zeros_like(db_ref)

        valid = i >= pb
        err = h_ref[...] - t_ref[...]
        dh = jnp.where(valid, err * (1.0 / D), 0.0)
        part = 0.5 * jnp.sum(jnp.sum(err * err, axis=-1, keepdims=True) * (1.0 / D), axis=0, keepdims=True)
        loss_ref[...] += jnp.where(valid, part, 0.0)
        dpre, dg, db = _ln_bwd_rows(dh, pre_ref[...], g_ref[...], _row_ids(i, tm, D), first_row)
        dpre_ref[...] = dpre
        dg_ref[0:1, :] += dg
        db_ref[0:1, :] += db

    row = lambda i: (i, 0)
    fix = lambda i: (0, 0)
    return pl.pallas_call(
        body, name="loss_head", grid=(L // tm,),
        out_shape=(S((L, D), f32), S((8, D), f32), S((8, D), f32), S((8, LANE), f32)),
        in_specs=[pl.BlockSpec((tm, D), row), pl.BlockSpec((tm, D), lambda i: (jnp.maximum(i - pb, 0), 0)),
                  pl.BlockSpec((tm, D), row), pl.BlockSpec((1, D), fix)],
        out_specs=(pl.BlockSpec((tm, D), row), pl.BlockSpec((8, D), fix), pl.BlockSpec((8, D), fix),
                   pl.BlockSpec((8, LANE), fix)),
        compiler_params=_cp(dimension_semantics=("arbitrary",)))(h, target, pre, g)


def _adamw(g_terms, w, m, v, *, name):
    R, C = w.shape
    tr = _row_tile(R)
    n = len(g_terms)
    c1 = 1.0 - ADAM_B1 ** ADAM_STEP
    c2 = 1.0 - ADAM_B2 ** ADAM_STEP

    def body(*refs):
        g = refs[0][...].astype(f32)
        for r in refs[1:n]:
            g = g + r[...].astype(f32)
        w_ref, m_ref, v_ref, g_out, d_out, m_out, v_out = refs[n:]
        mn = ADAM_B1 * m_ref[...] + (1.0 - ADAM_B1) * g
        vn = ADAM_B2 * v_ref[...] + (1.0 - ADAM_B2) * (g * g)
        g_out[...] = g
        m_out[...] = mn
        v_out[...] = vn
        d_out[...] = -ADAM_LR * ((mn / c1) / (jnp.sqrt(vn / c2) + ADAM_EPS) + ADAM_WD * w_ref[...])

    spec = pl.BlockSpec((tr, C), lambda i: (i, 0))
    return pl.pallas_call(
        body, name=name, grid=(R // tr,), out_shape=(S((R, C), f32),) * 4,
        in_specs=[spec] * (n + 3), out_specs=(spec,) * 4,
        compiler_params=_cp(dimension_semantics=("arbitrary",)))(*g_terms, w, m, v)


def _sum_devices(x):
    n, R, C = x.shape

    def body(x_ref, o_ref):
        acc = x_ref[0]
        for d in range(1, n):
            acc = acc + x_ref[d]
        o_ref[...] = acc

    return pl.pallas_call(body, name="sum_devices", out_shape=S((R, C), f32), compiler_params=_cp())(x)


def _row_tile(R):
    for step in (16, 8):
        for t in range(256, 0, -step):
            if R % t == 0:
                return t
    return R


def _adamw_direct(s32s, recvs, w, m, v, me, *, name):
    L, K, n = w.shape
    tk = _row_tile(K)
    c1 = 1.0 - ADAM_B1 ** ADAM_STEP
    c2 = 1.0 - ADAM_B2 ** ADAM_STEP

    def body(me_ref, *refs):
        own_refs, recv_refs = refs[:L], refs[L:2 * L]
        w_ref, m_ref, v_ref, g_out, d_out, m_out, v_out = refs[2 * L:]
        for li in range(L):
            @pl.when(pl.program_id(0) == li)
            def _(li=li):
                g = own_refs[li][0, 0]
                for d in range(N_DEV):
                    g = g + recv_refs[li][d, 0].astype(f32)
                mn = ADAM_B1 * m_ref[0] + (1.0 - ADAM_B1) * g
                vn = ADAM_B2 * v_ref[0] + (1.0 - ADAM_B2) * (g * g)
                g_out[0] = g
                m_out[0] = mn
                v_out[0] = vn
                d_out[0] = -ADAM_LR * ((mn / c1) / (jnp.sqrt(vn / c2) + ADAM_EPS) + ADAM_WD * w_ref[0])

    own = pl.BlockSpec((1, tk, n), lambda l, i, ix: (l, i, 0))
    grid_spec = pltpu.PrefetchScalarGridSpec(
        num_scalar_prefetch=1, grid=(L, K // tk),
        in_specs=[pl.BlockSpec((1, 1, tk, n), lambda l, i, ix: (ix[0], 0, i, 0))] * L
        + [pl.BlockSpec((N_DEV, 1, tk, n), lambda l, i, ix: (0, 0, i, 0))] * L + [own, own, own],
        out_specs=(own,) * 4)
    return pl.pallas_call(
        body, name=name, grid_spec=grid_spec, out_shape=(S((L, K, n), f32),) * 4,
        compiler_params=_cp(dimension_semantics=("arbitrary", "arbitrary")))(me, *s32s, *recvs, w, m, v)


def _col_segments(n, mapping):
    segs = []
    for p in range(N_DEV):
        lo, hi = p * n, (p + 1) * n
        out = []
        for c0, c1, e0 in mapping:
            a, b = max(lo, c0), min(hi, c1)
            if a < b:
                out.append((a - lo, e0 + (a - c0), b - a))
        segs.append(out)
    return segs


def _assemble_cols(gathered, mapping, n_out, *, name):
    _, L, K, n = gathered.shape
    tk = _row_tile(K)
    segs = _col_segments(n, mapping)
    covered = sum(w for s in segs for (_, _, w) in s)

    def body(g_ref, o_ref):
        if covered != n_out:
            o_ref[...] = jnp.zeros_like(o_ref)
        for p in range(N_DEV):
            for s0, d0, w in segs[p]:
                o_ref[0, :, d0:d0 + w] = g_ref[p, 0, :, s0:s0 + w]

    return pl.pallas_call(
        body, name=name, grid=(L, K // tk), out_shape=S((L, K, n_out), gathered.dtype),
        in_specs=[pl.BlockSpec((N_DEV, 1, tk, n), lambda l, i: (0, l, i, 0))],
        out_specs=pl.BlockSpec((1, tk, n_out), lambda l, i: (l, i, 0)),
        compiler_params=_cp(dimension_semantics=("arbitrary", "arbitrary")))(gathered)


def _split_cols(dws, mapping, n, *, name):
    L = len(dws)
    K, n_in = dws[0].shape
    tk = _row_tile(K)
    segs = _col_segments(n, mapping)

    def body(*refs):
        ins, o32, o16 = refs[:L], refs[L], refs[L + 1]
        for li in range(L):
            @pl.when(pl.program_id(0) == li)
            def _(li=li):
                for p in range(N_DEV):
                    for s0, d0, w in segs[p]:
                        val = ins[li][:, d0:d0 + w]
                        o32[p, 0, :, s0:s0 + w] = val
                        o16[p, 0, :, s0:s0 + w] = _bf(val)

    out = pl.BlockSpec((N_DEV, 1, tk, n), lambda l, i: (0, l, i, 0))
    return pl.pallas_call(
        body, name=name, grid=(L, K // tk), out_shape=(S((N_DEV, L, K, n), f32), S((N_DEV, L, K, n), bf16)),
        in_specs=[pl.BlockSpec((tk, n_in), lambda l, i: (i, 0))] * L, out_specs=(out, out),
        compiler_params=_cp(dimension_semantics=("arbitrary", "arbitrary")))(*dws)


def _split_rows(dws, k, *, name):
    L = len(dws)
    N = dws[0].shape[1]

    def body(*refs):
        ins, o32, o16 = refs[:L], refs[L], refs[L + 1]
        for li in range(L):
            @pl.when(pl.program_id(0) == li)
            def _(li=li):
                val = ins[li][...]
                o32[0, 0] = val
                o16[0, 0] = _bf(val)

    out = pl.BlockSpec((1, 1, k, N), lambda l, p: (p, l, 0, 0))
    return pl.pallas_call(
        body, name=name, grid=(L, N_DEV), out_shape=(S((N_DEV, L, k, N), f32), S((N_DEV, L, k, N), bf16)),
        in_specs=[pl.BlockSpec((k, N), lambda l, p: (p, 0))] * L, out_specs=(out, out),
        compiler_params=_cp(dimension_semantics=("arbitrary", "arbitrary")))(*dws)


def _rows_full(gathered):
    _, L, k, N = gathered.shape
    return jnp.transpose(gathered, (1, 0, 2, 3)).reshape(L, N_DEV * k, N)


def _all_gather(xs, *, name):
    na = len(xs)

    def body(*refs):
        x_refs, out_refs = refs[:na], refs[na:2 * na]
        send_sems, recv_sems, local_sems = refs[2 * na:]
        mx, my, mc = lax.axis_index("x"), lax.axis_index("y"), lax.axis_index("c")
        me, sibling = (mx, my, mc), (mx, my, 1 - mc)
        chips = [(1 - mx, my), (mx, 1 - my), (1 - mx, 1 - my)]

        def slot(a, px, py, pc):
            return out_refs[a].at[4 * px + 2 * py + pc]

        def copy(a, kk, block, to, src=None):
            return pltpu.make_async_remote_copy(
                src_ref=slot(a, *block) if src is None else src, dst_ref=slot(a, *block),
                send_sem=send_sems.at[7 * a + kk], recv_sem=recv_sems.at[7 * a + kk], device_id=to, device_id_type=MESH)

        mine = [pltpu.make_async_copy(x_refs[a], slot(a, *me), local_sems.at[a]) for a in range(na)]
        for cp in mine:
            cp.start()
        first = []
        for a in range(na):
            first.append(copy(a, 0, me, sibling, src=x_refs[a]))
            first += [copy(a, 1 + j, me, (*chip, mc), src=x_refs[a]) for j, chip in enumerate(chips)]
        for cp in first:
            cp.start()
        passed = []
        for j, chip in enumerate(chips):
            for a in range(na):
                copy(a, 1 + j, (*chip, mc), me).wait_recv()
                fwd = copy(a, 4 + j, (*chip, mc), sibling)
                fwd.start()
                passed.append(fwd)
        for a in range(na):
            copy(a, 0, sibling, me).wait_recv()
            for j, chip in enumerate(chips):
                copy(a, 4 + j, (*chip, 1 - mc), me).wait_recv()
        for cp in first + passed:
            cp.wait_send()
        for cp in mine:
            cp.wait()

    return pl.pallas_call(
        body, name=name, out_shape=tuple(S((N_DEV,) + x.shape, x.dtype) for x in xs),
        in_specs=[ANY] * na, out_specs=(ANY,) * na,
        scratch_shapes=[pltpu.SemaphoreType.DMA((7 * na,)), pltpu.SemaphoreType.DMA((7 * na,)),
                        pltpu.SemaphoreType.DMA((na,))],
        compiler_params=pltpu.CompilerParams(has_side_effects=True))(*xs)


_FLIPS = [(fx, fy, fc) for fx in (0, 1) for fy in (0, 1) for fc in (0, 1)][1:]


def _flip_peer(flip):
    x, y, c = lax.axis_index("x"), lax.axis_index("y"), lax.axis_index("c")
    return tuple(1 - a if f else a for a, f in zip((x, y, c), flip))


def _dev_index(p):
    return 4 * p[0] + 2 * p[1] + p[2]


HBM_SPEC = pl.BlockSpec(memory_space=pltpu.HBM)
SEM_SPEC = pl.BlockSpec(memory_space=pltpu.SEMAPHORE)


def _direct_start(srcs, lands, per_peer, *, name):
    na = len(srcs)

    def body(*refs):
        src_refs, land_refs = refs[:na], refs[na:2 * na]
        send_sems, recv_sems = refs[2 * na], refs[2 * na + 1]
        token = refs[-1]
        me = _dev_index((lax.axis_index("x"), lax.axis_index("y"), lax.axis_index("c")))
        for a in range(na):
            for r, flip in enumerate(_FLIPS):
                peer = _flip_peer(flip)
                src = src_refs[a].at[_dev_index(peer)] if per_peer else src_refs[a]
                pltpu.make_async_remote_copy(
                    src_ref=src, dst_ref=land_refs[a].at[me], send_sem=send_sems.at[7 * a + r],
                    recv_sem=recv_sems.at[7 * a + r], device_id=peer, device_id_type=MESH).start()
        token[...] = jnp.zeros_like(token)

    hbm = lambda t: pltpu.with_memory_space_constraint(t, pltpu.HBM)
    out = pl.pallas_call(
        body, name=name,
        out_shape=(pltpu.SemaphoreType.DMA((7 * na,)), pltpu.SemaphoreType.DMA((7 * na,)))
        + tuple(pltpu.HBM(t.shape, t.dtype) for t in list(srcs) + list(lands)) + (S((8, LANE), f32),),
        in_specs=[HBM_SPEC] * (2 * na),
        out_specs=(SEM_SPEC, SEM_SPEC) + (HBM_SPEC,) * (2 * na) + (pl.BlockSpec(memory_space=pltpu.VMEM),),
        input_output_aliases={i: 2 + i for i in range(2 * na)},
        compiler_params=pltpu.CompilerParams(has_side_effects=pltpu.SideEffectType.DATAFLOW_SIDE_EFFECTING))(
            *[hbm(t) for t in srcs], *[hbm(t) for t in lands])
    return out[0], out[1], list(out[2:2 + na]), list(out[2 + na:2 + 2 * na]), out[-1]


def _direct_wait(send_sems, recv_sems, srcs, lands, per_peer, after, *, name):
    na = len(srcs)

    def body(*refs):
        src_refs, land_refs = refs[:na], refs[na:2 * na]
        ssem, rsem = refs[2 * na], refs[2 * na + 1]
        me = _dev_index((lax.axis_index("x"), lax.axis_index("y"), lax.axis_index("c")))
        for a in range(na):
            for r, flip in enumerate(_FLIPS):
                peer = _flip_peer(flip)
                src = src_refs[a].at[_dev_index(peer)] if per_peer else src_refs[a]
                cp = pltpu.make_async_remote_copy(
                    src_ref=src, dst_ref=land_refs[a].at[me], send_sem=ssem.at[7 * a + r],
                    recv_sem=rsem.at[7 * a + r], device_id=peer, device_id_type=MESH)
                cp.wait_send()
                cp.wait_recv()

    out = pl.pallas_call(
        body, name=name, out_shape=tuple(pltpu.HBM(t.shape, t.dtype) for t in list(srcs) + list(lands)),
        in_specs=[HBM_SPEC] * (2 * na) + [SEM_SPEC, SEM_SPEC, ANY], out_specs=(HBM_SPEC,) * (2 * na),
        input_output_aliases={i: i for i in range(2 * na)},
        compiler_params=pltpu.CompilerParams(has_side_effects=pltpu.SideEffectType.DATAFLOW_SIDE_EFFECTING))(
            *srcs, *lands, send_sems, recv_sems, after)
    return list(out[:na]), list(out[na:])


def _pack_small(parts, width):
    rows, offs, r = [], [], 0
    for a in parts:
        n = a.size
        nr = -(-n // width)
        flat = a.reshape(-1).astype(f32)
        if nr * width != n:
            flat = jnp.pad(flat, (0, nr * width - n))
        rows.append(flat.reshape(nr, width))
        offs.append((r, nr))
        r += nr
    buf = jnp.concatenate(rows, axis=0)
    pad = (-r) % 8
    if pad:
        buf = jnp.pad(buf, ((0, pad), (0, 0)))
    return buf, offs


def _unpack_small(buf, off, shape):
    r, nr = off
    return buf[r:r + nr].reshape(-1)[:math.prod(shape)].reshape(shape)


def _local_step(x, target, meta, a_w_in, a_w_out, small, start_token, late_weights, grads_ready):
    SEQ, D = x.shape
    n_meta = meta.shape[0]
    first_row = PADF - n_meta
    H = small["a_log"].shape[-1]

    head = jnp.concatenate([jnp.zeros((first_row, D), f32), meta], axis=0)

    def lanes(a):
        return jnp.pad(a.reshape(1, -1), ((0, 0), (0, LANE - a.size)))

    def after_token(a, token):
        return a if token is None else a + token[0:1, 0:1]

    alog, dtb = after_token(lanes(small["a_log"][0]), start_token), lanes(small["a_dt_bias"][0])
    a_conv, b_conv = small["a_conv"][0], small["b_conv"][0]
    nw = small["a_norm"][0].reshape(1, DH)
    lmg, lmb, lfg, lfb = small["ln_mix_g"], small["ln_mix_b"], small["ln_ffn_g"], small["ln_ffn_b"]

    h0, pre_a, z, raw, q, k, v, beta, g, t_all = _gdn_in_fwd(x, head, a_w_in, a_conv, alog, dtb,
                                                             first_row=first_row, H=H)
    o, y, s_all, pre1, h1 = _delta_fwd(q, k, v, g, beta, t_all, z, nw, h0, a_w_out, lmg[0:1], lmb[0:1],
                                       first_row=first_row, H=H)
    wts = late_weights(h1)
    up0, act0, pre2, h2 = _ffn_fwd(h1, wts["ffn_w_up"], small["ffn_conv"][0], wts["ffn_w_down"],
                                   lfg[0:1], lfb[0:1], layer=0, first_row=first_row, name="ffn_fwd0")
    proj_b, bu, pre3, h3 = _sc_fwd(h2, wts["b_w_in"], b_conv, wts["b_w_out"], lmg[1:2], lmb[1:2], first_row=first_row)
    up1, act1, pre4, h4 = _ffn_fwd(h3, wts["ffn_w_up"], small["ffn_conv"][1], wts["ffn_w_down"],
                                   lfg[1:2], lfb[1:2], layer=1, first_row=first_row, name="ffn_fwd1")
    gs = {}
    dpre4, dlfg1, dlfb1, loss_tile = _loss_head(h4, target, pre4, lfg[1:2], first_row=first_row)

    def ffn_backward(dpre, up, act, h_in, layer, tag, ln_in, token=None):
        dup, dcw, dpre_in, dg, db = _ffn_bwd(
            dpre, up, wts["ffn_w_down"], after_token(small["ffn_conv"][layer], token),
            wts["ffn_w_up"], ln_in[0], ln_in[1], layer=layer, first_row=first_row, name="ffn_bwd" + tag)
        dwd = _linear_dw(act, dpre, name="dw_down" + tag)
        dwu = _linear_dw(h_in, dup, name="dw_up" + tag)
        return dpre_in, dg, db, dwu, dwd, dcw[0:3]

    dpre3, dlmg1, dlmb1, dwu1, dwd1, dcf1 = ffn_backward(dpre4, up1, act1, h3, 1, "1", (pre3, lmg[1:2]))

    dproj_b, dcb, dpre2, dlfg0, dlfb0 = _sc_bwd(dpre3, proj_b, b_conv, wts["b_w_out"], wts["b_w_in"], pre2, lfg[0:1],
                                                first_row=first_row)
    dwb_in = _linear_dw(h2, dproj_b, name="dw_b_in")
    token = grads_ready("layer1", dict(ffn_w_up=dwu1, ffn_w_down=dwd1, b_w_in=dwb_in))

    dpre1, dlmg0, dlmb0, dwu0, dwd0, dcf0 = ffn_backward(dpre2, up0, act0, h1, 0, "0", (pre1, lmg[0:1]), token)
    token = grads_ready("layer0", dict(ffn_w_up=dwu0, ffn_w_down=dwd0))

    dq, dk, dv, dz, dg_, dbeta, dnw = _delta_bwd(dpre1, a_w_out, o, z, after_token(nw, token), q, k, v, g, beta,
                                                 s_all, t_all, H=H)
    dproj_a, dca, dal, ddt, grad_x, dhead = _gdn_in_bwd(dq, dk, dv, dz, dg_, dbeta, pre_a, raw, a_conv, alog, dtb,
                                                        a_w_in, dpre1, first_row=first_row, H=H)
    token = grads_ready("last", dict(a_w_in=_linear_dw(h0, dproj_a, name="dw_a_in")))
    grads_ready("tail", dict(a_w_out=_linear_dw(y, dpre1, name="dw_a_out", after=token),
                             b_w_out=_linear_dw(bu, dpre3, name="dw_b_out", after=token)))

    gs["meta"] = dhead[first_row:PADF]
    gs["a_conv"] = dca[0:a_conv.shape[0]][None]
    gs["a_log"] = dal[0:1, 0:H]
    gs["a_dt_bias"] = ddt[0:1, 0:H]
    gs["a_norm"] = dnw[0:1]
    gs["b_conv"] = dcb[0:b_conv.shape[0]][None]
    gs["ln_mix_g"] = jnp.stack([dlmg0[0], dlmg1[0]])
    gs["ln_mix_b"] = jnp.stack([dlmb0[0], dlmb1[0]])
    gs["ffn_conv"] = jnp.stack([dcf0, dcf1])
    gs["ln_ffn_g"] = jnp.stack([dlfg0[0], dlfg1[0]])
    gs["ln_ffn_b"] = jnp.stack([dlfb0[0], dlfb1[0]])
    return loss_tile, grad_x, gs


_BIG = ("a_w_in", "a_w_out", "b_w_in", "b_w_out", "ffn_w_up", "ffn_w_down")
_BIG_COL = ("a_w_in", "b_w_in", "ffn_w_up")
_SMALL = ("meta", "a_conv", "a_log", "a_dt_bias", "a_norm", "b_conv", "ln_mix_g", "ln_mix_b",
          "ffn_conv", "ln_ffn_g", "ln_ffn_b")
_SMALL_SHARDED = ("meta", "a_conv", "b_conv", "ffn_conv")
_ORDER = ("meta", "a_w_in", "a_conv", "a_log", "a_dt_bias", "a_norm", "a_w_out", "b_w_in", "b_conv", "b_w_out",
          "ln_mix_g", "ln_mix_b", "ffn_w_up", "ffn_conv", "ffn_w_down", "ln_ffn_g", "ln_ffn_b")


def _a_w_in_map(H):
    W4 = 4 * H * DH
    return [(0, W4, 0), (W4, W4 + H, W4), (W4 + H, W4 + 2 * H, W4 + LANE)], W4 + 2 * LANE


def kernel(x, meta, a_w_in, a_conv, a_log, a_dt_bias, a_norm, a_w_out, b_w_in, b_conv, b_w_out, ln_mix_g, ln_mix_b, ffn_w_up, ffn_conv, ffn_w_down, ln_ffn_g, ln_ffn_b, loss_target, m_meta, m_a_w_in, m_a_conv, m_a_log, m_a_dt_bias, m_a_norm, m_a_w_out, m_b_w_in, m_b_conv, m_b_w_out, m_ln_mix_g, m_ln_mix_b, m_ffn_w_up, m_ffn_conv, m_ffn_w_down, m_ln_ffn_g, m_ln_ffn_b, v_meta, v_a_w_in, v_a_conv, v_a_log, v_a_dt_bias, v_a_norm, v_a_w_out, v_b_w_in, v_b_conv, v_b_w_out, v_ln_mix_g, v_ln_mix_b, v_ffn_w_up, v_ffn_conv, v_ffn_w_down, v_ln_ffn_g, v_ln_ffn_b):
    wloc = dict(meta=meta, a_w_in=a_w_in, a_conv=a_conv, a_log=a_log, a_dt_bias=a_dt_bias, a_norm=a_norm,
                a_w_out=a_w_out, b_w_in=b_w_in, b_conv=b_conv, b_w_out=b_w_out, ln_mix_g=ln_mix_g, ln_mix_b=ln_mix_b,
                ffn_w_up=ffn_w_up, ffn_conv=ffn_conv, ffn_w_down=ffn_w_down, ln_ffn_g=ln_ffn_g, ln_ffn_b=ln_ffn_b)
    mloc = dict(meta=m_meta, a_w_in=m_a_w_in, a_conv=m_a_conv, a_log=m_a_log, a_dt_bias=m_a_dt_bias, a_norm=m_a_norm,
                a_w_out=m_a_w_out, b_w_in=m_b_w_in, b_conv=m_b_conv, b_w_out=m_b_w_out, ln_mix_g=m_ln_mix_g,
                ln_mix_b=m_ln_mix_b, ffn_w_up=m_ffn_w_up, ffn_conv=m_ffn_conv, ffn_w_down=m_ffn_w_down,
                ln_ffn_g=m_ln_ffn_g, ln_ffn_b=m_ln_ffn_b)
    vloc = dict(meta=v_meta, a_w_in=v_a_w_in, a_conv=v_a_conv, a_log=v_a_log, a_dt_bias=v_a_dt_bias, a_norm=v_a_norm,
                a_w_out=v_a_w_out, b_w_in=v_b_w_in, b_conv=v_b_conv, b_w_out=v_b_w_out, ln_mix_g=v_ln_mix_g,
                ln_mix_b=v_ln_mix_b, ffn_w_up=v_ffn_w_up, ffn_conv=v_ffn_conv, ffn_w_down=v_ffn_w_down,
                ln_ffn_g=v_ln_ffn_g, ln_ffn_b=v_ln_ffn_b)
    H = a_log.shape[-1]
    mx, my, mc = lax.axis_index("x"), lax.axis_index("y"), lax.axis_index("c")
    me = 4 * mx + 2 * my + mc

    a_map, a_cols = _a_w_in_map(H)
    col_maps = {"a_w_in": (a_map, a_cols)}
    for n in ("b_w_in", "ffn_w_up"):
        ncols = N_DEV * wloc[n].shape[-1]
        col_maps[n] = ([(0, ncols, 0)], ncols)
    sm_sh = [wloc[n] for n in _SMALL_SHARDED]
    sbuf, soffs = _pack_small(sm_sh, 128)
    g_a_w_in, g_a_w_out, sg = _all_gather([_bf(wloc["a_w_in"]), _bf(wloc["a_w_out"]), sbuf], name="gather_first")
    w_a_in = _assemble_cols(g_a_w_in, *col_maps["a_w_in"], name="assemble_a_w_in")[0]
    w_a_out = _rows_full(g_a_w_out)[0]
    late = [n for n in _BIG if n not in ("a_w_in", "a_w_out")]
    ssem, rsem, srcs_t, lands_t, start_token = _direct_start(
        [_bf(wloc[n]) for n in late], [lax.empty((N_DEV,) + wloc[n].shape, bf16) for n in late], False,
        name="gather_rest_start")

    def late_weights(after):
        srcs_d, landed = _direct_wait(ssem, rsem, srcs_t, lands_t, False, after, name="gather_rest_wait")
        wts = {}
        for n, own, got in zip(late, srcs_d, landed):
            full = lax.dynamic_update_index_in_dim(got, own, me, 0)
            if n in _BIG_COL:
                wts[n] = _assemble_cols(full, *col_maps[n], name="assemble_" + n)
            elif n == "ffn_w_down":
                wts[n] = full
            else:
                wts[n] = _rows_full(full)
        for n in ("b_w_in", "b_w_out"):
            wts[n] = wts[n][0]
        return wts

    small = {n: wloc[n] for n in _SMALL}
    for n, off in zip(_SMALL_SHARDED, soffs):
        sh = wloc[n].shape
        parts = jnp.stack([_unpack_small(sg[d], off, sh) for d in range(N_DEV)])
        nd = len(sh)
        small[n] = jnp.transpose(parts, tuple(range(1, nd)) + (0, nd)).reshape(sh[:-1] + (N_DEV * sh[-1],))

    def split(n, dws, tag):
        if n in _BIG_COL:
            return _split_cols(dws, col_maps[n][0], wloc[n].shape[-1], name="split_" + n + tag)
        return _split_rows(dws, wloc[n].shape[-2], name="split_" + n + tag)

    sent = {}

    def grads_ready(stage, grads):
        names = sorted(grads)
        parts = [split(n, [grads[n]], "_" + stage) for n in names]
        handles = _direct_start([p[1] for p in parts], [jnp.zeros(p[1].shape, bf16) for p in parts], True,
                                name="grads_" + stage + "_start")
        sent[stage] = (names, [p[0] for p in parts], handles)
        return handles[4]

    loss_tile, grad_x, gs = _local_step(x[0], loss_target[0], small["meta"], w_a_in, w_a_out, small, start_token,
                                        late_weights, grads_ready)

    def landed(stage, after):
        names, own32, (ssem_g, rsem_g, srcs_g, lands_g, _) = sent[stage]
        _, got = _direct_wait(ssem_g, rsem_g, srcs_g, lands_g, True, after, name="grads_" + stage + "_wait")
        return list(zip(names, own32, got))

    parts = {}
    for stage in ("layer0", "layer1"):
        for n, o32, r in landed(stage, grad_x):
            parts.setdefault(n, []).append((o32, r))
    me1 = jnp.stack([me]).astype(jnp.int32)
    big_out = {n: _adamw_direct([p[0] for p in ps], [p[1] for p in ps], wloc[n], mloc[n], vloc[n], me1,
                                name="adamw_" + n) for n, ps in parts.items()}
    names = list(_SMALL)
    pbuf, poffs = _pack_small([gs[n] for n in names] + [loss_tile[0:1, 0:1]], 1024)
    psum = _sum_devices(_all_gather([pbuf], name="gather_small_grads")[0])
    loss = psum[poffs[-1][0], 0]
    g_small = {}
    for n, off in zip(names, poffs[:-1]):
        full_shape = gs[n].shape
        gfull = _unpack_small(psum, off, full_shape)
        if n in _SMALL_SHARDED:
            ns = wloc[n].shape[-1]
            gfull = lax.dynamic_slice_in_dim(gfull, me * ns, ns, axis=gfull.ndim - 1)
        g_small[n] = gfull.reshape(wloc[n].shape)
    gbuf, aoffs = _pack_small([g_small[n] for n in names], 128)
    wbuf, _ = _pack_small([wloc[n] for n in names], 128)
    mbuf, _ = _pack_small([mloc[n] for n in names], 128)
    vbuf, _ = _pack_small([vloc[n] for n in names], 128)
    _, d_s, m_s, v_s = _adamw([gbuf], wbuf, mbuf, vbuf, name="adamw_small")

    done = d_s[0, 0]
    for out in big_out.values():
        done = done + out[1][0, 0, 0]
    for stage in ("last", "tail"):
        for n, o32, r in landed(stage, done.reshape(1, 1)):
            big_out[n] = _adamw_direct([o32], [r], wloc[n], mloc[n], vloc[n], me1, name="adamw_" + n)

    grads, deltas, new_m, new_v = {}, {}, {}, {}
    for n in _BIG:
        grads[n], deltas[n], new_m[n], new_v[n] = big_out[n]
    for n, off in zip(names, aoffs):
        sh = wloc[n].shape
        grads[n] = g_small[n]
        deltas[n], new_m[n], new_v[n] = (_unpack_small(b_, off, sh) for b_ in (d_s, m_s, v_s))
    return (loss, grad_x[None], *[grads[n] for n in _ORDER], *[deltas[n] for n in _ORDER],
            *[new_m[n] for n in _ORDER], *[new_v[n] for n in _ORDER])
```

```python
import math

import jax
import jax.numpy as jnp
from jax import lax
from jax.experimental import pallas as pl
from jax.experimental.pallas import tpu as pltpu

f32, bf16 = jnp.float32, jnp.bfloat16
S = jax.ShapeDtypeStruct
HI = lax.Precision.HIGHEST
MESH = pl.DeviceIdType.MESH

V7X_VMEM_LIMIT = 56 * 1024 * 1024
LANE = 128
DH = 128
CH = 64
PADF = 256
TM = 256
TMM = 768
N_DEV = 8
BWD_HEAD_GROUP = 4
BWD_GROUP_LAG = 4

DEPTH = 2
ALPHA = (2.0 * DEPTH) ** 0.25
LN_EPS = 1e-5
RMS_EPS = 1e-6
L2_EPS = 1e-6
ADAM_LR, ADAM_B1, ADAM_B2, ADAM_EPS, ADAM_WD, ADAM_STEP = 0.001, 0.9, 0.999, 1e-08, 0.01, 10


def _cp(**kw):
    return pltpu.CompilerParams(vmem_limit_bytes=V7X_VMEM_LIMIT, **kw)


def _bf(x):
    return x.astype(bf16)


def _dot(a, b, precision=None):
    return jnp.dot(a, b, preferred_element_type=f32, precision=precision)


def _dot_nt(a, b):
    return lax.dot_general(a, b, (((1,), (1,)), ((), ())), preferred_element_type=f32)


def _dot_tn(a, b):
    return lax.dot_general(a, b, (((0,), (0,)), ((), ())), preferred_element_type=f32)


def _sigmoid(x):
    return 1.0 / (1.0 + jnp.exp(-x))


def _load_once(pairs, sem):
    @pl.when(pl.program_id(0) == 0)
    def _():
        cps = [pltpu.make_async_copy(src, dst, sem.at[n]) for n, (src, dst) in enumerate(pairs)]
        for c in cps:
            c.start()
        for c in cps:
            c.wait()


def _row_ids(i, tm, width):
    return i * tm + lax.broadcasted_iota(jnp.int32, (tm, width), 0)


def _ln_fwd(pre, g, b, rows, first_row):
    mu = jnp.mean(pre, axis=-1, keepdims=True)
    xc = pre - mu
    var = jnp.mean(xc * xc, axis=-1, keepdims=True)
    y = xc * lax.rsqrt(var + LN_EPS) * g + b
    return jnp.where(rows >= first_row, y, 0.0)


ANY = pl.BlockSpec(memory_space=pl.ANY)


def _taps_back(prev8, x, kw):
    xe = jnp.concatenate([prev8, x], axis=0)
    return [pltpu.roll(xe, kw - 1 - j, 0)[8:] for j in range(kw - 1)] + [x]


def _taps_ahead(x, next8, kw):
    n = x.shape[0]
    xe = jnp.concatenate([x, next8], axis=0)
    return [pltpu.roll(xe, n + 8 - (kw - 1 - j), 0)[:n] for j in range(kw - 1)] + [x]


def _conv(cw, taps):
    acc = cw[0:1, :] * taps[0]
    for j in range(1, len(taps)):
        acc = acc + cw[j:j + 1, :] * taps[j]
    return acc


def _linear_dw(x, dy, *, name, after=None):
    L, K = x.shape
    N = dy.shape[1]
    tm = TMM if L % TMM == 0 else TM
    tn = LANE
    for d in range(N // LANE, 0, -1):
        if (N // LANE) % d == 0 and K * d * LANE * 4 <= 9 * 1024 * 1024:
            tn = d * LANE
            break

    def body(x_ref, dy_ref, *rest):
        o_ref = rest[-1]

        @pl.when(pl.program_id(1) == 0)
        def _():
            o_ref[...] = jnp.zeros_like(o_ref)
        o_ref[...] += _dot_tn(_bf(x_ref[...]), _bf(dy_ref[...]))

    in_specs = [pl.BlockSpec((tm, K), lambda j, i: (i, 0)), pl.BlockSpec((tm, tn), lambda j, i: (i, j))]
    args = [x, dy]
    if after is not None:
        in_specs.append(pl.BlockSpec(after.shape, lambda j, i: (0, 0)))
        args.append(after)
    return pl.pallas_call(
        body, name=name, grid=(N // tn, L // tm), out_shape=S((K, N), f32),
        in_specs=in_specs, out_specs=pl.BlockSpec((K, tn), lambda j, i: (0, j)),
        compiler_params=_cp(dimension_semantics=("arbitrary", "arbitrary")))(*args)


def _ln_bwd_rows(dout, pre, g, rows, first_row):
    mu = jnp.mean(pre, axis=-1, keepdims=True)
    xc = pre - mu
    rstd = lax.rsqrt(jnp.mean(xc * xc, axis=-1, keepdims=True) + LN_EPS)
    xh = xc * rstd
    dy = jnp.where(rows >= first_row, dout, 0.0)
    dxh = dy * g
    dpre = rstd * (dxh - jnp.mean(dxh, axis=-1, keepdims=True) - xh * jnp.mean(dxh * xh, axis=-1, keepdims=True))
    return dpre, jnp.sum(dy * xh, axis=0, keepdims=True), jnp.sum(dy, axis=0, keepdims=True)


def _gdn_in_fwd(x, head, w_full, conv_w, alog, dtb, *, first_row, H):
    D = x.shape[1]
    L = PADF + x.shape[0]
    W = H * DH
    NW = w_full.shape[1]
    KW = conv_w.shape[0]
    tm = TM
    pb = PADF // tm

    def body(x_ref, head_ref, w_hbm, cw_ref, alog_ref, dtb_ref,
             h_ref, pre_ref, z_ref, raw_ref, q_ref, k_ref, v_ref, beta_ref, g_ref, t_ref,
             w_vmem, carry, sem):
        i = pl.program_id(0)
        _load_once([(w_hbm, w_vmem)], sem)

        @pl.when(i == 0)
        def _():
            carry[...] = jnp.zeros_like(carry)

        hv = jnp.where(i < pb, head_ref[...], x_ref[...])
        h_ref[...] = hv
        hb = _bf(hv)
        outs = (q_ref, k_ref, v_ref)

        def section(s):
            pre = _dot(hb, w_vmem[:, s * W:(s + 1) * W])
            pre_ref[:, s * W:(s + 1) * W] = pre
            c = _conv(cw_ref[:, s * W:(s + 1) * W], _taps_back(carry[s], pre, KW))
            carry[s] = pre[tm - 8:tm, :]
            sl = c * _sigmoid(c)
            if s < 2:
                scale = DH ** -0.5 if s == 0 else 1.0
                for hh in range(H):
                    seg = sl[:, hh * DH:(hh + 1) * DH]
                    r = lax.rsqrt(jnp.sum(seg * seg, axis=-1, keepdims=True) + L2_EPS)
                    outs[s][:, hh * DH:(hh + 1) * DH] = seg * (r * scale)
            else:
                v_ref[...] = sl

        raw = _dot(hb, w_vmem[:, 4 * W:4 * W + 2 * LANE])
        raw_ref[...] = raw
        ok = (_row_ids(i, tm, LANE) >= first_row) & (lax.broadcasted_iota(jnp.int32, (tm, LANE), 1) < H)
        beta = jnp.where(ok, _sigmoid(raw[:, :LANE]), 0.0)
        beta_ref[...] = beta
        a = raw[:, LANE:] + dtb_ref[...]
        sp = jnp.maximum(a, 0.0) + jnp.log(1.0 + jnp.exp(-jnp.abs(a)))
        gv = jnp.where(ok, -jnp.exp(alog_ref[...]) * sp, 0.0)
        gam = _dot(_chunk_tri(tm, lower=True), gv, HI)
        g_ref[...] = gam
        section(1)
        ii = lax.broadcasted_iota(jnp.int32, (CH, CH), 0)
        jj = lax.broadcasted_iota(jnp.int32, (CH, CH), 1)
        eye = (ii == jj).astype(f32)
        gam_t = gam.T

        def inverses(chunks):
            ms = []
            for c in chunks:
                rows = slice(c * CH, (c + 1) * CH)
                for hh in range(H):
                    kh = k_ref[rows, hh * DH:(hh + 1) * DH]
                    dec = jnp.exp(jnp.minimum(gam[rows, hh:hh + 1] - gam_t[hh:hh + 1, rows], 0.0))
                    kk = _dot_nt(_bf(kh * beta[rows, hh:hh + 1]), _bf(kh))
                    ms.append(jnp.where(ii > jj, kk * dec, 0.0))
            for n, t in enumerate(_tri_inv_many(ms, eye)):
                t_ref[chunks[n // H], n % H] = t

        section(2)
        z_ref[...] = _dot(hb, w_vmem[:, 3 * W:4 * W])
        section(0)
        inverses(list(range(tm // CH)))

    row = lambda i: (i, 0)
    fix = lambda i: (0, 0)
    out_shape = (S((L, D), f32), S((L, 3 * W), f32), S((L, W), f32), S((L, 2 * LANE), f32),
                 S((L, W), f32), S((L, W), f32), S((L, W), f32), S((L, LANE), f32), S((L, LANE), f32),
                 S((L // CH, H, CH, CH), f32))
    out_specs = (pl.BlockSpec((tm, D), row),
                 pl.BlockSpec((tm, 3 * W), row), pl.BlockSpec((tm, W), row), pl.BlockSpec((tm, 2 * LANE), row),
                 pl.BlockSpec((tm, W), row), pl.BlockSpec((tm, W), row), pl.BlockSpec((tm, W), row),
                 pl.BlockSpec((tm, LANE), row), pl.BlockSpec((tm, LANE), row),
                 pl.BlockSpec((tm // CH, H, CH, CH), lambda i: (i, 0, 0, 0)))
    return pl.pallas_call(
        body, name="gdn_in_fwd", grid=(L // tm,), out_shape=out_shape,
        in_specs=[pl.BlockSpec((tm, D), lambda i: (jnp.maximum(i - pb, 0), 0)),
                  pl.BlockSpec((tm, D), lambda i: (jnp.minimum(i, pb - 1), 0)), ANY, pl.BlockSpec((KW, 3 * W), fix),
                  pl.BlockSpec((1, LANE), fix), pl.BlockSpec((1, LANE), fix)],
        out_specs=out_specs,
        scratch_shapes=[pltpu.VMEM((D, NW), w_full.dtype), pltpu.VMEM((3, 8, W), f32), pltpu.SemaphoreType.DMA((1,))],
        compiler_params=_cp(dimension_semantics=("arbitrary",)))(x, head, w_full, conv_w, alog, dtb)


def _gdn_in_bwd(dq, dk, dv, dz, dg, dbeta, pre, raw, conv_w, alog, dtb, w_full, res, *, first_row, H):
    L = dq.shape[0]
    D = res.shape[1]
    W = H * DH
    KW = conv_w.shape[0]
    tm = TM
    nb = L // tm
    NW = 4 * W + 2 * LANE
    fb = PADF // tm
    alpha = ALPHA

    def body(dq_ref, dk_ref, dv_ref, dz_ref, dg_ref, dbeta_ref, pre_ref, hq_ref, hk_ref, hv_ref, raw_ref,
             cw_ref, alog_ref, dtb_ref, w_hbm, res_ref,
             dproj_ref, dcw_ref, dal_ref, ddt_ref, dx_ref, dfront_ref, w_vmem, carry, tmp, sem):
        i = pl.program_id(0)
        blk = nb - 1 - i
        _load_once([(w_hbm, w_vmem)], sem)

        @pl.when(i == 0)
        def _():
            carry[...] = jnp.zeros_like(carry)
            dcw_ref[...] = jnp.zeros_like(dcw_ref)
            dal_ref[...] = jnp.zeros_like(dal_ref)
            ddt_ref[...] = jnp.zeros_like(ddt_ref)

        halos = (hq_ref, hk_ref, hv_ref)
        douts = (dq_ref, dk_ref, dv_ref)
        for s in range(3):
            sec = slice(s * W, (s + 1) * W)
            pre = pre_ref[:, sec]
            c = _conv(cw_ref[:, sec], _taps_back(jnp.where(blk > 0, halos[s][...], 0.0), pre, KW))
            sig = _sigmoid(c)
            sl = c * sig
            if s < 2:
                scale = DH ** -0.5 if s == 0 else 1.0
                for hh in range(H):
                    hs = slice(hh * DH, (hh + 1) * DH)
                    seg = sl[:, hs]
                    r = lax.rsqrt(jnp.sum(seg * seg, axis=-1, keepdims=True) + L2_EPS)
                    n = seg * r
                    dqs = douts[s][:, hs]
                    tmp[:, hs] = (scale * r) * (dqs - n * jnp.sum(n * dqs, axis=-1, keepdims=True))
                dsl = tmp[...]
            else:
                dsl = dv_ref[...]
            dc = dsl * (sig * (1.0 + c * (1.0 - sig)))
            ahead = _taps_ahead(dc, carry[s], KW)
            carry[s] = dc[0:8, :]
            dproj_ref[:, sec] = _bf(_conv(cw_ref[:, sec], ahead))
            for j in range(KW):
                dcw_ref[j:j + 1, sec] += jnp.sum(ahead[j] * pre, axis=0, keepdims=True)
        dproj_ref[:, 3 * W:4 * W] = _bf(dz_ref[...])
        raw_v = raw_ref[...]
        ok = (_row_ids(blk, tm, LANE) >= first_row) & (lax.broadcasted_iota(jnp.int32, (tm, LANE), 1) < H)
        beta = _sigmoid(raw_v[:, :LANE])
        dbraw = jnp.where(ok, dbeta_ref[...] * beta * (1.0 - beta), 0.0)
        a = raw_v[:, LANE:] + dtb_ref[...]
        sp = jnp.maximum(a, 0.0) + jnp.log(1.0 + jnp.exp(-jnp.abs(a)))
        nea = -jnp.exp(alog_ref[...])
        dgm = jnp.where(ok, _dot(_chunk_tri(tm, lower=False), dg_ref[...], HI), 0.0)
        daraw = dgm * nea * _sigmoid(a)
        dal_ref[0:1, :] += jnp.sum(dgm * nea * sp, axis=0, keepdims=True)
        ddt_ref[0:1, :] += jnp.sum(daraw, axis=0, keepdims=True)
        dproj_ref[:, 4 * W:4 * W + LANE] = _bf(dbraw)
        dproj_ref[:, 4 * W + LANE:4 * W + 2 * LANE] = _bf(daraw)
        dh = alpha * res_ref[...] + _dot_nt(dproj_ref[...], w_vmem[...])

        @pl.when(blk >= fb)
        def _():
            dx_ref[...] = dh

        @pl.when(blk < fb)
        def _():
            dfront_ref[...] = dh

    rev = lambda i: (nb - 1 - i, 0)
    fix = lambda i: (0, 0)

    def halo(col):
        return pl.BlockSpec((8, W), lambda i: (jnp.maximum((nb - 1 - i) * (tm // 8) - 1, 0), col))

    return pl.pallas_call(
        body, name="gdn_in_bwd", grid=(nb,),
        out_shape=(S((L, NW), bf16), S((8, 3 * W), f32), S((8, LANE), f32), S((8, LANE), f32),
                   S((L - PADF, D), f32), S((PADF, D), f32)),
        in_specs=[pl.BlockSpec((tm, W), rev)] * 4 + [pl.BlockSpec((tm, LANE), rev)] * 2
        + [pl.BlockSpec((tm, 3 * W), rev), halo(0), halo(1), halo(2), pl.BlockSpec((tm, 2 * LANE), rev),
           pl.BlockSpec((KW, 3 * W), fix), pl.BlockSpec((1, LANE), fix), pl.BlockSpec((1, LANE), fix),
           ANY, pl.BlockSpec((tm, D), rev)],
        out_specs=(pl.BlockSpec((tm, NW), rev), pl.BlockSpec((8, 3 * W), fix),
                   pl.BlockSpec((8, LANE), fix), pl.BlockSpec((8, LANE), fix),
                   pl.BlockSpec((tm, D), lambda i: (jnp.maximum(nb - 1 - i - fb, 0), 0)),
                   pl.BlockSpec((tm, D), lambda i: (jnp.minimum(nb - 1 - i, fb - 1), 0))),
        scratch_shapes=[pltpu.VMEM((D, NW), w_full.dtype), pltpu.VMEM((3, 8, W), f32), pltpu.VMEM((tm, W), f32),
                        pltpu.SemaphoreType.DMA((1,))],
        compiler_params=_cp(dimension_semantics=("arbitrary",)))(
            dq, dk, dv, dz, dg, dbeta, pre, pre, pre, pre, raw, conv_w, alog, dtb, w_full, res)


def _chunk_tri(n, lower):
    i = lax.broadcasted_iota(jnp.int32, (n, n), 0)
    j = lax.broadcasted_iota(jnp.int32, (n, n), 1)
    sh = int(math.log2(CH))
    same = lax.shift_right_logical(i, sh) == lax.shift_right_logical(j, sh)
    return (same & ((i >= j) if lower else (j >= i))).astype(f32)


def _tri_inv_many(ms, eye):
    ts = [eye - m for m in ms]
    ps = list(ms)
    for _ in range(int(math.log2(CH)) - 1):
        pb = [_bf(p) for p in ps]
        ps = [_dot(p, p) for p in pb]
        ts = [t + _dot(_bf(t), _bf(p)) for t, p in zip(ts, ps)]
    return ts


def _chunk_local(q, k, v, gcol, grow, glast, bcol, ii, jj):
    dec = jnp.where(ii >= jj, jnp.exp(jnp.minimum(gcol - grow, 0.0)), 0.0)
    eg = jnp.exp(gcol)
    kb = k * bcol
    kbg = kb * eg
    vb = v * bcol
    qt = q * eg
    kt = k * jnp.exp(glast - gcol)
    kbb, qb, kbf = _bf(kb), _bf(q), _bf(k)
    return dec, eg, kb, kbg, vb, qt, kt, _dot_nt(kbb, kbf), _dot_nt(qb, kbf), jnp.concatenate([kbb, qb], axis=0)


def _delta_fwd(q, k, v, g, beta, t_all, z, nw, h, w_out, ln_g, ln_b, *, first_row, H):
    L = q.shape[0]
    W = H * DH
    D = h.shape[1]
    rb = TM
    nc = rb // CH
    nblk = L // rb
    alpha = ALPHA

    def body(q_ref, k_ref, v_ref, g_ref, b_ref, t_ref, z_ref, nw_ref, h_ref, wout_hbm, lg_ref, lb_ref,
             o_ref, y_ref, s_out, pre_ref, out_ref, s_scr, wout, sem):
        _load_once([(wout_hbm, wout)], sem)

        @pl.when(pl.program_id(0) == 0)
        def _():
            s_scr[...] = jnp.zeros_like(s_scr)

        ii = lax.broadcasted_iota(jnp.int32, (CH, CH), 0)
        jj = lax.broadcasted_iota(jnp.int32, (CH, CH), 1)
        eye = (ii == jj).astype(f32)
        nwv = nw_ref[...]

        heads = range(H)
        hsl = [slice(hh * DH, (hh + 1) * DH) for hh in heads]

        def chunk(c, carry):
            r0 = pl.multiple_of(c * CH, CH)
            rows = pl.ds(r0, CH)
            gam = g_ref[rows, :]
            gam_t = gam.T
            bb = b_ref[rows, :]
            glast = [gam[CH - 1:CH, hh:hh + 1] for hh in heads]
            loc = [_chunk_local(q_ref[rows, hsl[hh]], k_ref[rows, hsl[hh]], v_ref[rows, hsl[hh]],
                                gam[:, hh:hh + 1], gam_t[hh:hh + 1, :], glast[hh], bb[:, hh:hh + 1], ii, jj)
                   for hh in heads]
            st = [s_scr[hh] for hh in heads]
            zs = [z_ref[rows, hsl[hh]] for hh in heads]
            ts = [_bf(t_ref[c, hh]) for hh in heads]
            us = [_dot(t, _bf(l[4])) for t, l in zip(ts, loc)]
            ws = [_dot(t, _bf(l[3])) for t, l in zip(ts, loc)]
            stb = [_bf(s) for s in st]
            vn = [u - _dot(_bf(w), sb) for u, w, sb in zip(us, ws, stb)]
            vnb = [_bf(x) for x in vn]
            snew = [s * jnp.exp(gl) + _dot_tn(_bf(l[6]), xb) for s, gl, l, xb in zip(st, glast, loc, vnb)]
            os_ = [_dot(_bf(l[5]), sb) + _dot(_bf(l[8] * l[0]), xb) for l, sb, xb in zip(loc, stb, vnb)]
            for hh in heads:
                o = os_[hh]
                s_out[c, hh] = st[hh]
                s_scr[hh] = snew[hh]
                o_ref[rows, hsl[hh]] = o
                on = o * lax.rsqrt(jnp.mean(o * o, axis=-1, keepdims=True) + RMS_EPS) * nwv
                y_ref[rows, hsl[hh]] = _bf(on * (zs[hh] * _sigmoid(zs[hh])))
            return carry

        lax.fori_loop(0, nc, chunk, 0)
        pre = alpha * h_ref[...] + _dot(y_ref[...], wout[...])
        pre_ref[...] = pre
        out_ref[...] = _ln_fwd(pre, lg_ref[...], lb_ref[...], _row_ids(pl.program_id(0), rb, D), first_row)

    row = lambda i: (i, 0)
    fix = lambda i: (0, 0)
    return pl.pallas_call(
        body, name="delta_fwd", grid=(nblk,),
        out_shape=(S((L, W), f32), S((L, W), bf16), S((L // CH, H, DH, DH), f32), S((L, D), f32), S((L, D), f32)),
        in_specs=[pl.BlockSpec((rb, W), row)] * 3 + [pl.BlockSpec((rb, LANE), row)] * 2
        + [pl.BlockSpec((nc, H, CH, CH), lambda i: (i, 0, 0, 0)),
           pl.BlockSpec((rb, W), row), pl.BlockSpec((1, DH), fix), pl.BlockSpec((rb, D), row), ANY,
           pl.BlockSpec((1, D), fix), pl.BlockSpec((1, D), fix)],
        out_specs=(pl.BlockSpec((rb, W), row), pl.BlockSpec((rb, W), row),
                   pl.BlockSpec((nc, H, DH, DH), lambda i: (i, 0, 0, 0)),
                   pl.BlockSpec((rb, D), row), pl.BlockSpec((rb, D), row)),
        scratch_shapes=[pltpu.VMEM((H, DH, DH), f32), pltpu.VMEM((W, D), w_out.dtype), pltpu.SemaphoreType.DMA((1,))],
        compiler_params=_cp(dimension_semantics=("arbitrary",)))(q, k, v, g, beta, t_all, z, nw, h, w_out, ln_g, ln_b)


def _delta_bwd(dpre, w_out, o, z, nw, q, k, v, g, beta, s_all, t_all, *, H):
    L = q.shape[0]
    W = H * DH
    D = dpre.shape[1]
    rb = TM
    nc = rb // CH
    nblk = L // rb

    def body(dpre_ref, wout_hbm, o_ref, z_ref, nw_ref, q_ref, k_ref, v_ref, g_ref, b_ref, s_ref, t_ref,
             dq_ref, dk_ref, dv_ref, dz_ref, dg_ref, db_ref, dnw_ref, ds_scr, wout, dy_scr, sem):
        _load_once([(wout_hbm, wout)], sem)

        @pl.when(pl.program_id(0) == 0)
        def _():
            ds_scr[...] = jnp.zeros_like(ds_scr)
            dnw_ref[...] = jnp.zeros_like(dnw_ref)

        dy_scr[...] = _dot_nt(_bf(dpre_ref[...]), wout[...])

        ii = lax.broadcasted_iota(jnp.int32, (CH, CH), 0)
        jj = lax.broadcasted_iota(jnp.int32, (CH, CH), 1)
        lane = lax.broadcasted_iota(jnp.int32, (CH, LANE), 1)
        last_row = lax.broadcasted_iota(jnp.int32, (CH, 1), 0) == CH - 1
        nwv = nw_ref[...]

        def chunk(cc, carry):
            c = nc - 1 - cc
            r0 = pl.multiple_of(c * CH, CH)
            rows = pl.ds(r0, CH)
            gam = g_ref[rows, :]
            gam_t = gam.T
            bb = b_ref[rows, :]

            def head(hh):
                hs = slice(hh * DH, (hh + 1) * DH)
                gcol, grow, glast = gam[:, hh:hh + 1], gam_t[hh:hh + 1, :], gam[CH - 1:CH, hh:hh + 1]
                bcol = bb[:, hh:hh + 1]
                qh, kh, vh = q_ref[rows, hs], k_ref[rows, hs], v_ref[rows, hs]
                oh, zh, dyh = o_ref[rows, hs], z_ref[rows, hs], dy_scr[rows, hs]
                t = t_ref[c, hh]
                st = s_ref[c, hh]
                dsn = ds_scr[hh]
                rms = lax.rsqrt(jnp.mean(oh * oh, axis=-1, keepdims=True) + RMS_EPS)
                on = oh * rms
                sig = _sigmoid(zh)
                sz = zh * sig
                dz_ref[rows, hs] = dyh * on * nwv * (sig * (1.0 + zh * (1.0 - sig)))
                dnw = jnp.sum(dyh * on * sz, axis=0, keepdims=True)
                don = dyh * nwv * sz
                do = rms * (don - on * jnp.mean(don * on, axis=-1, keepdims=True))
                dec, eg, kb, kbg, vb, qt, kt, kk, qk, kqb = _chunk_local(qh, kh, vh, gcol, grow, glast, bcol, ii, jj)
                stb, dsnb, dob, tb, kbgb = _bf(st), _bf(dsn), _bf(do), _bf(t), _bf(kbg)
                r = vb - _dot(kbgb, stb)
                yield
                mm = jnp.where(ii > jj, kk * dec, 0.0)
                attn = qk * dec
                yield
                rbf = _bf(r)
                vn = _dot(tb, rbf)
                dvn = _dot_tn(_bf(attn), dob) + _dot(_bf(kt), dsnb)
                egl = jnp.exp(glast)
                ekt = jnp.exp(glast - gcol)
                yield
                vnb, dvnb = _bf(vn), _bf(dvn)
                dattn = jnp.where(ii >= jj, _dot_nt(dob, vnb), 0.0)
                dkt = _dot_nt(vnb, dsnb)
                dvb = _dot_tn(tb, dvnb)
                dt = _dot_nt(dvnb, rbf)
                yield
                dglast = egl * jnp.sum(jnp.sum(dsn * st, axis=0, keepdims=True), axis=1, keepdims=True)
                yield
                dv_ref[rows, hs] = dvb * bcol
                dod = jnp.concatenate([dob, -_bf(dvb)], axis=0)
                ds_scr[hh] = egl * dsn + _dot_tn(jnp.concatenate([_bf(qt), kbgb], axis=0), dod)
                both = _dot_nt(dod, stb)
                dqt, dkbg = both[:CH], both[CH:]
                x = _dot_nt(_bf(dt), tb)
                yield
                dm = jnp.where(ii > jj, -_dot_tn(tb, _bf(x)), 0.0)
                dkk = dm * dec
                dqk = dattn * dec
                e = dm * mm + dattn * attn
                dgam = jnp.sum(e, axis=1, keepdims=True) - jnp.sum(e.T, axis=1, keepdims=True)
                dd = _bf(jnp.concatenate([dkk, dqk], axis=0))
                both = _dot(dd, _bf(kh))
                dkb = both[:CH] + dkbg * eg
                yield
                dk_ref[rows, hs] = _dot_tn(dd, kqb) + dkt * ekt + dkb * bcol
                dq_ref[rows, hs] = both[CH:] + dqt * eg
                yield
                dktkt = dkt * kt
                dgam = dgam + jnp.sum(dqt * qt - dktkt + dkbg * kbg, axis=1, keepdims=True)
                dglast = dglast + jnp.sum(jnp.sum(dktkt, axis=0, keepdims=True), axis=1, keepdims=True)
                dgam = dgam + jnp.where(last_row, dglast, 0.0)
                dbeta = jnp.sum(dkb * kh + dvb * vh, axis=1, keepdims=True)
                return dgam, dbeta, dnw

            res = [None] * H
            gens = [head(hh) for hh in range(H)]
            step = 0
            while any(r is None for r in res):
                for hh in range(H):
                    if res[hh] is None and step >= (hh // BWD_HEAD_GROUP) * BWD_GROUP_LAG:
                        try:
                            next(gens[hh])
                        except StopIteration as stop:
                            res[hh] = stop.value
                step += 1
            dgam_all = jnp.zeros((CH, LANE), f32)
            dbeta_all = jnp.zeros((CH, LANE), f32)
            dnw_acc = jnp.zeros((1, DH), f32)
            for hh in range(H):
                dgam, dbeta, dnw = res[hh]
                dgam_all = dgam_all + jnp.where(lane == hh, dgam, 0.0)
                dbeta_all = dbeta_all + jnp.where(lane == hh, dbeta, 0.0)
                dnw_acc = dnw_acc + dnw
            dg_ref[rows, :] = dgam_all
            db_ref[rows, :] = dbeta_all
            dnw_ref[0:1, :] += dnw_acc
            return carry

        lax.fori_loop(0, nc, chunk, 0)

    rev = lambda i: (nblk - 1 - i, 0)
    rev4 = lambda i: (nblk - 1 - i, 0, 0, 0)
    fix = lambda i: (0, 0)
    wide = pl.BlockSpec((rb, W), rev)
    thin = pl.BlockSpec((rb, LANE), rev)
    return pl.pallas_call(
        body, name="delta_bwd", grid=(nblk,),
        out_shape=(S((L, W), f32),) * 4 + (S((L, LANE), f32),) * 2 + (S((8, DH), f32),),
        in_specs=[pl.BlockSpec((rb, D), rev), ANY, wide, wide, pl.BlockSpec((1, DH), fix), wide, wide, wide, thin, thin,
                  pl.BlockSpec((nc, H, DH, DH), rev4), pl.BlockSpec((nc, H, CH, CH), rev4)],
        out_specs=(wide,) * 4 + (thin, thin, pl.BlockSpec((8, DH), fix)),
        scratch_shapes=[pltpu.VMEM((H, DH, DH), f32), pltpu.VMEM((W, D), w_out.dtype), pltpu.VMEM((rb, W), f32),
                        pltpu.SemaphoreType.DMA((1,))],
        compiler_params=_cp(dimension_semantics=("arbitrary",)))(
            dpre, w_out, o, z, nw, q, k, v, g, beta, s_all, t_all)


def _sc_fwd(h, w_in, conv_w, w_out, g, b, *, first_row):
    L, D = h.shape
    W = w_out.shape[0]
    KW = conv_w.shape[0]
    tm = TM
    alpha = ALPHA

    def body(h_ref, win_hbm, cw_ref, wout_hbm, g_ref, b_ref, proj_ref, bu_ref, pre_ref, out_ref,
             win, wout, carry, sem):
        i = pl.program_id(0)
        _load_once([(win_hbm, win), (wout_hbm, wout)], sem)

        @pl.when(i == 0)
        def _():
            carry[...] = jnp.zeros_like(carry)

        hv = h_ref[...]
        hb = _bf(hv)
        bg = _dot(hb, win[:, 0:W])
        cg = _dot(hb, win[:, W:2 * W])
        xv = _dot(hb, win[:, 2 * W:3 * W])
        proj_ref[:, 0:W] = bg
        proj_ref[:, W:2 * W] = cg
        proj_ref[:, 2 * W:3 * W] = xv
        p = cg * xv
        u = _conv(cw_ref[...], _taps_back(carry[...], p, KW))
        carry[...] = p[tm - 8:tm, :]
        bu = _bf(bg * u)
        bu_ref[...] = bu
        pre = alpha * hv + _dot(bu, wout[...])
        pre_ref[...] = pre
        out_ref[...] = _ln_fwd(pre, g_ref[...], b_ref[...], _row_ids(i, tm, D), first_row)

    row = lambda i: (i, 0)
    fix = lambda i: (0, 0)
    return pl.pallas_call(
        body, name="sc_fwd", grid=(L // tm,),
        out_shape=(S((L, 3 * W), f32), S((L, W), bf16), S((L, D), f32), S((L, D), f32)),
        in_specs=[pl.BlockSpec((tm, D), row), ANY, pl.BlockSpec((KW, W), fix), ANY,
                  pl.BlockSpec((1, D), fix), pl.BlockSpec((1, D), fix)],
        out_specs=(pl.BlockSpec((tm, 3 * W), row), pl.BlockSpec((tm, W), row),
                   pl.BlockSpec((tm, D), row), pl.BlockSpec((tm, D), row)),
        scratch_shapes=[pltpu.VMEM((D, 3 * W), w_in.dtype), pltpu.VMEM((W, D), w_out.dtype),
                        pltpu.VMEM((8, W), f32), pltpu.SemaphoreType.DMA((2,))],
        compiler_params=_cp(dimension_semantics=("arbitrary",)))(h, w_in, conv_w, w_out, g, b)


def _sc_bwd(dpre, proj, conv_w, w_out, w_in, pre_in, g_in, *, first_row):
    L, D = dpre.shape
    W = w_out.shape[0]
    KW = conv_w.shape[0]
    tm = TM
    nb = L // tm
    alpha = ALPHA

    def body(dpre_ref, proj_ref, hc_ref, hx_ref, cw_ref, wout_hbm, win_hbm, pin_ref, g_ref,
             dproj_ref, dcw_ref, dpin_ref, dg_ref, db_ref, wout, win, carry, sem):
        i = pl.program_id(0)
        blk = nb - 1 - i
        _load_once([(wout_hbm, wout), (win_hbm, win)], sem)

        @pl.when(i == 0)
        def _():
            carry[...] = jnp.zeros_like(carry)
            dcw_ref[...] = jnp.zeros_like(dcw_ref)
            dg_ref[...] = jnp.zeros_like(dg_ref)
            db_ref[...] = jnp.zeros_like(db_ref)

        bg, cg, xv = proj_ref[:, 0:W], proj_ref[:, W:2 * W], proj_ref[:, 2 * W:3 * W]
        p = cg * xv
        u = _conv(cw_ref[...], _taps_back(jnp.where(blk > 0, hc_ref[...] * hx_ref[...], 0.0), p, KW))
        dpre_v = dpre_ref[...]
        d = _dot_nt(_bf(dpre_v), wout[...])
        dproj_ref[:, 0:W] = _bf(d * u)
        du = d * bg
        ahead = _taps_ahead(du, carry[...], KW)
        carry[...] = du[0:8, :]
        dp = _conv(cw_ref[...], ahead)
        for j in range(KW):
            dcw_ref[j:j + 1, :] += jnp.sum(ahead[j] * p, axis=0, keepdims=True)
        dproj_ref[:, W:2 * W] = _bf(dp * xv)
        dproj_ref[:, 2 * W:3 * W] = _bf(dp * cg)
        dh = alpha * dpre_v + _dot_nt(dproj_ref[...], win[...])
        dpin, dg, dbias = _ln_bwd_rows(dh, pin_ref[...], g_ref[...], _row_ids(blk, tm, D), first_row)
        dpin_ref[...] = dpin
        dg_ref[0:1, :] += dg
        db_ref[0:1, :] += dbias

    rev = lambda i: (nb - 1 - i, 0)
    fix = lambda i: (0, 0)

    def halo(col):
        return pl.BlockSpec((8, W), lambda i: (jnp.maximum((nb - 1 - i) * (tm // 8) - 1, 0), col))

    return pl.pallas_call(
        body, name="sc_bwd", grid=(nb,),
        out_shape=(S((L, 3 * W), bf16), S((8, W), f32), S((L, D), f32), S((8, D), f32), S((8, D), f32)),
        in_specs=[pl.BlockSpec((tm, D), rev), pl.BlockSpec((tm, 3 * W), rev), halo(1), halo(2),
                  pl.BlockSpec((KW, W), fix), ANY, ANY, pl.BlockSpec((tm, D), rev), pl.BlockSpec((1, D), fix)],
        out_specs=(pl.BlockSpec((tm, 3 * W), rev), pl.BlockSpec((8, W), fix), pl.BlockSpec((tm, D), rev),
                   pl.BlockSpec((8, D), fix), pl.BlockSpec((8, D), fix)),
        scratch_shapes=[pltpu.VMEM((W, D), w_out.dtype), pltpu.VMEM((D, 3 * W), w_in.dtype), pltpu.VMEM((8, W), f32),
                        pltpu.SemaphoreType.DMA((2,))],
        compiler_params=_cp(dimension_semantics=("arbitrary",)))(
            dpre, proj, proj, proj, conv_w, w_out, w_in, pre_in, g_in)


def _ffn_cols(F):
    fc = F
    for cand in (1408, 1024, 512, 256, 128):
        if F % cand == 0:
            fc = cand
            break
    return fc


def _ffn_weight_copies(wup_hbm, wdn_hbm, wup, wdn, layer):
    k = wdn_hbm.shape[2]
    return [(wup_hbm.at[layer], wup)] + [(wdn_hbm.at[p, layer], wdn.at[pl.ds(p * k, k), :]) for p in range(N_DEV)]


def _ffn_fwd(h, w_up, conv_w, w_down, g, b, *, layer, first_row, name):
    L, D = h.shape
    F = N_DEV * w_down.shape[2]
    KW = conv_w.shape[0]
    tm = TM
    fc = _ffn_cols(F)
    alpha = ALPHA

    def body(h_ref, wup_hbm, cw_ref, wdn_hbm, g_ref, b_ref, up_ref, a_ref, pre_ref, out_ref,
             wup, wdn, carry, sem):
        i = pl.program_id(0)
        _load_once(_ffn_weight_copies(wup_hbm, wdn_hbm, wup, wdn, layer), sem)

        @pl.when(i == 0)
        def _():
            carry[...] = jnp.zeros_like(carry)

        hv = h_ref[...]
        hb = _bf(hv)
        pre = alpha * hv
        for c0 in range(0, F, fc):
            cs = slice(c0, c0 + fc)
            u = _dot(hb, wup[:, cs])
            gate = _dot(hb, wup[:, F + c0:F + c0 + fc])
            up_ref[:, cs] = u
            up_ref[:, F + c0:F + c0 + fc] = gate
            uc = _conv(cw_ref[:, cs], _taps_back(carry[:, cs], u, KW))
            carry[:, cs] = u[tm - 8:tm, :]
            ab = _bf(uc * _sigmoid(uc) * gate)
            a_ref[:, cs] = ab
            pre = pre + _dot(ab, wdn[cs, :])
        pre_ref[...] = pre
        out_ref[...] = _ln_fwd(pre, g_ref[...], b_ref[...], _row_ids(i, tm, D), first_row)

    row = lambda i: (i, 0)
    fix = lambda i: (0, 0)
    return pl.pallas_call(
        body, name=name, grid=(L // tm,),
        out_shape=(S((L, 2 * F), f32), S((L, F), bf16), S((L, D), f32), S((L, D), f32)),
        in_specs=[pl.BlockSpec((tm, D), row), ANY, pl.BlockSpec((KW, F), fix), ANY,
                  pl.BlockSpec((1, D), fix), pl.BlockSpec((1, D), fix)],
        out_specs=(pl.BlockSpec((tm, 2 * F), row), pl.BlockSpec((tm, F), row),
                   pl.BlockSpec((tm, D), row), pl.BlockSpec((tm, D), row)),
        scratch_shapes=[pltpu.VMEM((D, 2 * F), w_up.dtype), pltpu.VMEM((F, D), w_down.dtype),
                        pltpu.VMEM((8, F), f32), pltpu.SemaphoreType.DMA((1 + N_DEV,))],
        compiler_params=_cp(dimension_semantics=("arbitrary",)))(h, w_up, conv_w, w_down, g, b)


def _ffn_bwd(dpre, up, w_down, conv_w, w_up, pre_in, g_in, *, layer, first_row, name):
    L, D = dpre.shape
    F = N_DEV * w_down.shape[2]
    KW = conv_w.shape[0]
    tm = TM
    nb = L // tm
    fc = F
    alpha = ALPHA

    def body(dpre_ref, up_ref, halo_ref, wdn_hbm, cw_ref, wup_hbm, pin_ref, g_ref,
             dup_ref, dcw_ref, dpin_ref, dg_ref, db_ref, wdn, wup, carry, sem):
        i = pl.program_id(0)
        blk = nb - 1 - i
        _load_once(_ffn_weight_copies(wup_hbm, wdn_hbm, wup, wdn, layer), sem)

        @pl.when(i == 0)
        def _():
            carry[...] = jnp.zeros_like(carry)
            dcw_ref[...] = jnp.zeros_like(dcw_ref)
            dg_ref[...] = jnp.zeros_like(dg_ref)
            db_ref[...] = jnp.zeros_like(db_ref)

        dpre_v = dpre_ref[...]
        db = _bf(dpre_v)
        dh = alpha * dpre_v
        for c0 in range(0, F, fc):
            cs = slice(c0, c0 + fc)
            gs_ = slice(F + c0, F + c0 + fc)
            da = _dot_nt(db, wdn[cs, :])
            gate = up_ref[:, gs_]
            u = up_ref[:, cs]
            uc = _conv(cw_ref[:, cs], _taps_back(jnp.where(blk > 0, halo_ref[:, cs], 0.0), u, KW))
            sig = _sigmoid(uc)
            dgate = _bf(da * (uc * sig))
            dup_ref[:, gs_] = dgate
            duc = da * gate * (sig * (1.0 + uc * (1.0 - sig)))
            ahead = _taps_ahead(duc, carry[:, cs], KW)
            carry[:, cs] = duc[0:8, :]
            du = _bf(_conv(cw_ref[:, cs], ahead))
            dup_ref[:, cs] = du
            for j in range(KW):
                dcw_ref[j:j + 1, cs] += jnp.sum(ahead[j] * u, axis=0, keepdims=True)
            dh = dh + _dot_nt(du, wup[:, cs]) + _dot_nt(dgate, wup[:, gs_])
        dpin, dg, dbias = _ln_bwd_rows(dh, pin_ref[...], g_ref[...], _row_ids(blk, tm, D), first_row)
        dpin_ref[...] = dpin
        dg_ref[0:1, :] += dg
        db_ref[0:1, :] += dbias

    rev = lambda i: (nb - 1 - i, 0)
    fix = lambda i: (0, 0)
    return pl.pallas_call(
        body, name=name, grid=(nb,),
        out_shape=(S((L, 2 * F), bf16), S((8, F), f32), S((L, D), f32), S((8, D), f32), S((8, D), f32)),
        in_specs=[pl.BlockSpec((tm, D), rev), pl.BlockSpec((tm, 2 * F), rev),
                  pl.BlockSpec((8, F), lambda i: (jnp.maximum((nb - 1 - i) * (tm // 8) - 1, 0), 0)),
                  ANY, pl.BlockSpec((KW, F), fix), ANY, pl.BlockSpec((tm, D), rev), pl.BlockSpec((1, D), fix)],
        out_specs=(pl.BlockSpec((tm, 2 * F), rev), pl.BlockSpec((8, F), fix), pl.BlockSpec((tm, D), rev),
                   pl.BlockSpec((8, D), fix), pl.BlockSpec((8, D), fix)),
        scratch_shapes=[pltpu.VMEM((F, D), w_down.dtype), pltpu.VMEM((D, 2 * F), w_up.dtype), pltpu.VMEM((8, F), f32),
                        pltpu.SemaphoreType.DMA((1 + N_DEV,))],
        compiler_params=_cp(dimension_semantics=("arbitrary",)))(dpre, up, up, w_down, conv_w, w_up, pre_in, g_in)


def _loss_head(h, target, pre, g, *, first_row):
    L, D = h.shape
    tm = TM
    pb = PADF // tm

    def body(h_ref, t_ref, pre_ref, g_ref, dpre_ref, dg_ref, db_ref, loss_ref):
        i = pl.program_id(0)

        @pl.when(i == 0)
        def _():
            loss_ref[...] = jnp.zeros_like(loss_ref)
            dg_ref[...] = jnp.zeros_like(dg_ref)
            db_ref[...] = jnp.zeros_like(db_ref)

        valid = i >= pb
        err = h_ref[...] - t_ref[...]
        dh = jnp.where(valid, err * (1.0 / D), 0.0)
        part = 0.5 * jnp.sum(jnp.sum(err * err, axis=-1, keepdims=True) * (1.0 / D), axis=0, keepdims=True)
        loss_ref[...] += jnp.where(valid, part, 0.0)
        dpre, dg, db = _ln_bwd_rows(dh, pre_ref[...], g_ref[...], _row_ids(i, tm, D), first_row)
        dpre_ref[...] = dpre
        dg_ref[0:1, :] += dg
        db_ref[0:1, :] += db

    row = lambda i: (i, 0)
    fix = lambda i: (0, 0)
    return pl.pallas_call(
        body, name="loss_head", grid=(L // tm,),
        out_shape=(S((L, D), f32), S((8, D), f32), S((8, D), f32), S((8, LANE), f32)),
        in_specs=[pl.BlockSpec((tm, D), row), pl.BlockSpec((tm, D), lambda i: (jnp.maximum(i - pb, 0), 0)),
                  pl.BlockSpec((tm, D), row), pl.BlockSpec((1, D), fix)],
        out_specs=(pl.BlockSpec((tm, D), row), pl.BlockSpec((8, D), fix), pl.BlockSpec((8, D), fix),
                   pl.BlockSpec((8, LANE), fix)),
        compiler_params=_cp(dimension_semantics=("arbitrary",)))(h, target, pre, g)


def _adamw(g_terms, w, m, v, *, name):
    R, C = w.shape
    tr = _row_tile(R)
    n = len(g_terms)
    c1 = 1.0 - ADAM_B1 ** ADAM_STEP
    c2 = 1.0 - ADAM_B2 ** ADAM_STEP

    def body(*refs):
        g = refs[0][...].astype(f32)
        for r in refs[1:n]:
            g = g + r[...].astype(f32)
        w_ref, m_ref, v_ref, g_out, d_out, m_out, v_out = refs[n:]
        mn = ADAM_B1 * m_ref[...] + (1.0 - ADAM_B1) * g
        vn = ADAM_B2 * v_ref[...] + (1.0 - ADAM_B2) * (g * g)
        g_out[...] = g
        m_out[...] = mn
        v_out[...] = vn
        d_out[...] = -ADAM_LR * ((mn / c1) / (jnp.sqrt(vn / c2) + ADAM_EPS) + ADAM_WD * w_ref[...])

    spec = pl.BlockSpec((tr, C), lambda i: (i, 0))
    return pl.pallas_call(
        body, name=name, grid=(R // tr,), out_shape=(S((R, C), f32),) * 4,
        in_specs=[spec] * (n + 3), out_specs=(spec,) * 4,
        compiler_params=_cp(dimension_semantics=("arbitrary",)))(*g_terms, w, m, v)


def _sum_devices(x):
    n, R, C = x.shape

    def body(x_ref, o_ref):
        acc = x_ref[0]
        for d in range(1, n):
            acc = acc + x_ref[d]
        o_ref[...] = acc

    return pl.pallas_call(body, name="sum_devices", out_shape=S((R, C), f32), compiler_params=_cp())(x)


def _row_tile(R):
    for step in (16, 8):
        for t in range(256, 0, -step):
            if R % t == 0:
                return t
    return R


def _adamw_direct(s32s, recvs, w, m, v, me, *, name):
    L, K, n = w.shape
    tk = _row_tile(K)
    c1 = 1.0 - ADAM_B1 ** ADAM_STEP
    c2 = 1.0 - ADAM_B2 ** ADAM_STEP

    def body(me_ref, *refs):
        own_refs, recv_refs = refs[:L], refs[L:2 * L]
        w_ref, m_ref, v_ref, g_out, d_out, m_out, v_out = refs[2 * L:]
        for li in range(L):
            @pl.when(pl.program_id(0) == li)
            def _(li=li):
                g = own_refs[li][0, 0]
                for d in range(N_DEV):
                    g = g + recv_refs[li][d, 0].astype(f32)
                mn = ADAM_B1 * m_ref[0] + (1.0 - ADAM_B1) * g
                vn = ADAM_B2 * v_ref[0] + (1.0 - ADAM_B2) * (g * g)
                g_out[0] = g
                m_out[0] = mn
                v_out[0] = vn
                d_out[0] = -ADAM_LR * ((mn / c1) / (jnp.sqrt(vn / c2) + ADAM_EPS) + ADAM_WD * w_ref[0])

    own = pl.BlockSpec((1, tk, n), lambda l, i, ix: (l, i, 0))
    grid_spec = pltpu.PrefetchScalarGridSpec(
        num_scalar_prefetch=1, grid=(L, K // tk),
        in_specs=[pl.BlockSpec((1, 1, tk, n), lambda l, i, ix: (ix[0], 0, i, 0))] * L
        + [pl.BlockSpec((N_DEV, 1, tk, n), lambda l, i, ix: (0, 0, i, 0))] * L + [own, own, own],
        out_specs=(own,) * 4)
    return pl.pallas_call(
        body, name=name, grid_spec=grid_spec, out_shape=(S((L, K, n), f32),) * 4,
        compiler_params=_cp(dimension_semantics=("arbitrary", "arbitrary")))(me, *s32s, *recvs, w, m, v)


def _col_segments(n, mapping):
    segs = []
    for p in range(N_DEV):
        lo, hi = p * n, (p + 1) * n
        out = []
        for c0, c1, e0 in mapping:
            a, b = max(lo, c0), min(hi, c1)
            if a < b:
                out.append((a - lo, e0 + (a - c0), b - a))
        segs.append(out)
    return segs


def _assemble_cols(gathered, mapping, n_out, *, name):
    _, L, K, n = gathered.shape
    tk = _row_tile(K)
    segs = _col_segments(n, mapping)
    covered = sum(w for s in segs for (_, _, w) in s)

    def body(g_ref, o_ref):
        if covered != n_out:
            o_ref[...] = jnp.zeros_like(o_ref)
        for p in range(N_DEV):
            for s0, d0, w in segs[p]:
                o_ref[0, :, d0:d0 + w] = g_ref[p, 0, :, s0:s0 + w]

    return pl.pallas_call(
        body, name=name, grid=(L, K // tk), out_shape=S((L, K, n_out), gathered.dtype),
        in_specs=[pl.BlockSpec((N_DEV, 1, tk, n), lambda l, i: (0, l, i, 0))],
        out_specs=pl.BlockSpec((1, tk, n_out), lambda l, i: (l, i, 0)),
        compiler_params=_cp(dimension_semantics=("arbitrary", "arbitrary")))(gathered)


def _split_cols(dws, mapping, n, *, name):
    L = len(dws)
    K, n_in = dws[0].shape
    tk = _row_tile(K)
    segs = _col_segments(n, mapping)

    def body(*refs):
        ins, o32, o16 = refs[:L], refs[L], refs[L + 1]
        for li in range(L):
            @pl.when(pl.program_id(0) == li)
            def _(li=li):
                for p in range(N_DEV):
                    for s0, d0, w in segs[p]:
                        val = ins[li][:, d0:d0 + w]
                        o32[p, 0, :, s0:s0 + w] = val
                        o16[p, 0, :, s0:s0 + w] = _bf(val)

    out = pl.BlockSpec((N_DEV, 1, tk, n), lambda l, i: (0, l, i, 0))
    return pl.pallas_call(
        body, name=name, grid=(L, K // tk), out_shape=(S((N_DEV, L, K, n), f32), S((N_DEV, L, K, n), bf16)),
        in_specs=[pl.BlockSpec((tk, n_in), lambda l, i: (i, 0))] * L, out_specs=(out, out),
        compiler_params=_cp(dimension_semantics=("arbitrary", "arbitrary")))(*dws)


def _split_rows(dws, k, *, name):
    L = len(dws)
    N = dws[0].shape[1]

    def body(*refs):
        ins, o32, o16 = refs[:L], refs[L], refs[L + 1]
        for li in range(L):
            @pl.when(pl.program_id(0) == li)
            def _(li=li):
                val = ins[li][...]
                o32[0, 0] = val
                o16[0, 0] = _bf(val)

    out = pl.BlockSpec((1, 1, k, N), lambda l, p: (p, l, 0, 0))
    return pl.pallas_call(
        body, name=name, grid=(L, N_DEV), out_shape=(S((N_DEV, L, k, N), f32), S((N_DEV, L, k, N), bf16)),
        in_specs=[pl.BlockSpec((k, N), lambda l, p: (p, 0))] * L, out_specs=(out, out),
        compiler_params=_cp(dimension_semantics=("arbitrary", "arbitrary")))(*dws)


def _rows_full(gathered):
    _, L, k, N = gathered.shape
    return jnp.transpose(gathered, (1, 0, 2, 3)).reshape(L, N_DEV * k, N)


def _all_gather(xs, *, name):
    na = len(xs)

    def body(*refs):
        x_refs, out_refs = refs[:na], refs[na:2 * na]
        send_sems, recv_sems, local_sems = refs[2 * na:]
        mx, my, mc = lax.axis_index("x"), lax.axis_index("y"), lax.axis_index("c")
        me, sibling = (mx, my, mc), (mx, my, 1 - mc)
        chips = [(1 - mx, my), (mx, 1 - my), (1 - mx, 1 - my)]

        def slot(a, px, py, pc):
            return out_refs[a].at[4 * px + 2 * py + pc]

        def copy(a, kk, block, to, src=None):
            return pltpu.make_async_remote_copy(
                src_ref=slot(a, *block) if src is None else src, dst_ref=slot(a, *block),
                send_sem=send_sems.at[7 * a + kk], recv_sem=recv_sems.at[7 * a + kk], device_id=to, device_id_type=MESH)

        mine = [pltpu.make_async_copy(x_refs[a], slot(a, *me), local_sems.at[a]) for a in range(na)]
        for cp in mine:
            cp.start()
        first = []
        for a in range(na):
            first.append(copy(a, 0, me, sibling, src=x_refs[a]))
            first += [copy(a, 1 + j, me, (*chip, mc), src=x_refs[a]) for j, chip in enumerate(chips)]
        for cp in first:
            cp.start()
        passed = []
        for j, chip in enumerate(chips):
            for a in range(na):
                copy(a, 1 + j, (*chip, mc), me).wait_recv()
                fwd = copy(a, 4 + j, (*chip, mc), sibling)
                fwd.start()
                passed.append(fwd)
        for a in range(na):
            copy(a, 0, sibling, me).wait_recv()
            for j, chip in enumerate(chips):
                copy(a, 4 + j, (*chip, 1 - mc), me).wait_recv()
        for cp in first + passed:
            cp.wait_send()
        for cp in mine:
            cp.wait()

    return pl.pallas_call(
        body, name=name, out_shape=tuple(S((N_DEV,) + x.shape, x.dtype) for x in xs),
        in_specs=[ANY] * na, out_specs=(ANY,) * na,
        scratch_shapes=[pltpu.SemaphoreType.DMA((7 * na,)), pltpu.SemaphoreType.DMA((7 * na,)),
                        pltpu.SemaphoreType.DMA((na,))],
        compiler_params=pltpu.CompilerParams(has_side_effects=True))(*xs)


_FLIPS = [(fx, fy, fc) for fx in (0, 1) for fy in (0, 1) for fc in (0, 1)][1:]


def _flip_peer(flip):
    x, y, c = lax.axis_index("x"), lax.axis_index("y"), lax.axis_index("c")
    return tuple(1 - a if f else a for a, f in zip((x, y, c), flip))


def _dev_index(p):
    return 4 * p[0] + 2 * p[1] + p[2]


HBM_SPEC = pl.BlockSpec(memory_space=pltpu.HBM)
SEM_SPEC = pl.BlockSpec(memory_space=pltpu.SEMAPHORE)


def _direct_start(srcs, lands, per_peer, *, name):
    na = len(srcs)

    def body(*refs):
        src_refs, land_refs = refs[:na], refs[na:2 * na]
        send_sems, recv_sems = refs[2 * na], refs[2 * na + 1]
        token = refs[-1]
        me = _dev_index((lax.axis_index("x"), lax.axis_index("y"), lax.axis_index("c")))
        for a in range(na):
            for r, flip in enumerate(_FLIPS):
                peer = _flip_peer(flip)
                src = src_refs[a].at[_dev_index(peer)] if per_peer else src_refs[a]
                pltpu.make_async_remote_copy(
                    src_ref=src, dst_ref=land_refs[a].at[me], send_sem=send_sems.at[7 * a + r],
                    recv_sem=recv_sems.at[7 * a + r], device_id=peer, device_id_type=MESH).start()
        token[...] = jnp.zeros_like(token)

    hbm = lambda t: pltpu.with_memory_space_constraint(t, pltpu.HBM)
    out = pl.pallas_call(
        body, name=name,
        out_shape=(pltpu.SemaphoreType.DMA((7 * na,)), pltpu.SemaphoreType.DMA((7 * na,)))
        + tuple(pltpu.HBM(t.shape, t.dtype) for t in list(srcs) + list(lands)) + (S((8, LANE), f32),),
        in_specs=[HBM_SPEC] * (2 * na),
        out_specs=(SEM_SPEC, SEM_SPEC) + (HBM_SPEC,) * (2 * na) + (pl.BlockSpec(memory_space=pltpu.VMEM),),
        input_output_aliases={i: 2 + i for i in range(2 * na)},
        compiler_params=pltpu.CompilerParams(has_side_effects=pltpu.SideEffectType.DATAFLOW_SIDE_EFFECTING))(
            *[hbm(t) for t in srcs], *[hbm(t) for t in lands])
    return out[0], out[1], list(out[2:2 + na]), list(out[2 + na:2 + 2 * na]), out[-1]


def _direct_wait(send_sems, recv_sems, srcs, lands, per_peer, after, *, name):
    na = len(srcs)

    def body(*refs):
        src_refs, land_refs = refs[:na], refs[na:2 * na]
        ssem, rsem = refs[2 * na], refs[2 * na + 1]
        me = _dev_index((lax.axis_index("x"), lax.axis_index("y"), lax.axis_index("c")))
        for a in range(na):
            for r, flip in enumerate(_FLIPS):
                peer = _flip_peer(flip)
                src = src_refs[a].at[_dev_index(peer)] if per_peer else src_refs[a]
                cp = pltpu.make_async_remote_copy(
                    src_ref=src, dst_ref=land_refs[a].at[me], send_sem=ssem.at[7 * a + r],
                    recv_sem=rsem.at[7 * a + r], device_id=peer, device_id_type=MESH)
                cp.wait_send()
                cp.wait_recv()

    out = pl.pallas_call(
        body, name=name, out_shape=tuple(pltpu.HBM(t.shape, t.dtype) for t in list(srcs) + list(lands)),
        in_specs=[HBM_SPEC] * (2 * na) + [SEM_SPEC, SEM_SPEC, ANY], out_specs=(HBM_SPEC,) * (2 * na),
        input_output_aliases={i: i for i in range(2 * na)},
        compiler_params=pltpu.CompilerParams(has_side_effects=pltpu.SideEffectType.DATAFLOW_SIDE_EFFECTING))(
            *srcs, *lands, send_sems, recv_sems, after)
    return list(out[:na]), list(out[na:])


def _pack_small(parts, width):
    rows, offs, r = [], [], 0
    for a in parts:
        n = a.size
        nr = -(-n // width)
        flat = a.reshape(-1).astype(f32)
        if nr * width != n:
            flat = jnp.pad(flat, (0, nr * width - n))
        rows.append(flat.reshape(nr, width))
        offs.append((r, nr))
        r += nr
    buf = jnp.concatenate(rows, axis=0)
    pad = (-r) % 8
    if pad:
        buf = jnp.pad(buf, ((0, pad), (0, 0)))
    return buf, offs


def _unpack_small(buf, off, shape):
    r, nr = off
    return buf[r:r + nr].reshape(-1)[:math.prod(shape)].reshape(shape)


def _local_step(x, target, meta, a_w_in, a_w_out, small, start_token, late_weights, grads_ready):
    SEQ, D = x.shape
    n_meta = meta.shape[0]
    first_row = PADF - n_meta
    H = small["a_log"].shape[-1]

    head = jnp.concatenate([jnp.zeros((first_row, D), f32), meta], axis=0)

    def lanes(a):
        return jnp.pad(a.reshape(1, -1), ((0, 0), (0, LANE - a.size)))

    def after_token(a, token):
        return a if token is None else a + token[0:1, 0:1]

    alog, dtb = after_token(lanes(small["a_log"][0]), start_token), lanes(small["a_dt_bias"][0])
    a_conv, b_conv = small["a_conv"][0], small["b_conv"][0]
    nw = small["a_norm"][0].reshape(1, DH)
    lmg, lmb, lfg, lfb = small["ln_mix_g"], small["ln_mix_b"], small["ln_ffn_g"], small["ln_ffn_b"]

    h0, pre_a, z, raw, q, k, v, beta, g, t_all = _gdn_in_fwd(x, head, a_w_in, a_conv, alog, dtb,
                                                             first_row=first_row, H=H)
    o, y, s_all, pre1, h1 = _delta_fwd(q, k, v, g, beta, t_all, z, nw, h0, a_w_out, lmg[0:1], lmb[0:1],
                                       first_row=first_row, H=H)
    wts = late_weights(h1)
    up0, act0, pre2, h2 = _ffn_fwd(h1, wts["ffn_w_up"], small["ffn_conv"][0], wts["ffn_w_down"],
                                   lfg[0:1], lfb[0:1], layer=0, first_row=first_row, name="ffn_fwd0")
    proj_b, bu, pre3, h3 = _sc_fwd(h2, wts["b_w_in"], b_conv, wts["b_w_out"], lmg[1:2], lmb[1:2], first_row=first_row)
    up1, act1, pre4, h4 = _ffn_fwd(h3, wts["ffn_w_up"], small["ffn_conv"][1], wts["ffn_w_down"],
                                   lfg[1:2], lfb[1:2], layer=1, first_row=first_row, name="ffn_fwd1")
    gs = {}
    dpre4, dlfg1, dlfb1, loss_tile = _loss_head(h4, target, pre4, lfg[1:2], first_row=first_row)

    def ffn_backward(dpre, up, act, h_in, layer, tag, ln_in, token=None):
        dup, dcw, dpre_in, dg, db = _ffn_bwd(
            dpre, up, wts["ffn_w_down"], after_token(small["ffn_conv"][layer], token),
            wts["ffn_w_up"], ln_in[0], ln_in[1], layer=layer, first_row=first_row, name="ffn_bwd" + tag)
        dwd = _linear_dw(act, dpre, name="dw_down" + tag)
        dwu = _linear_dw(h_in, dup, name="dw_up" + tag)
        return dpre_in, dg, db, dwu, dwd, dcw[0:3]

    dpre3, dlmg1, dlmb1, dwu1, dwd1, dcf1 = ffn_backward(dpre4, up1, act1, h3, 1, "1", (pre3, lmg[1:2]))

    dproj_b, dcb, dpre2, dlfg0, dlfb0 = _sc_bwd(dpre3, proj_b, b_conv, wts["b_w_out"], wts["b_w_in"], pre2, lfg[0:1],
                                                first_row=first_row)
    dwb_in = _linear_dw(h2, dproj_b, name="dw_b_in")
    token = grads_ready("layer1", dict(ffn_w_up=dwu1, ffn_w_down=dwd1, b_w_in=dwb_in))

    dpre1, dlmg0, dlmb0, dwu0, dwd0, dcf0 = ffn_backward(dpre2, up0, act0, h1, 0, "0", (pre1, lmg[0:1]), token)
    token = grads_ready("layer0", dict(ffn_w_up=dwu0, ffn_w_down=dwd0))

    dq, dk, dv, dz, dg_, dbeta, dnw = _delta_bwd(dpre1, a_w_out, o, z, after_token(nw, token), q, k, v, g, beta,
                                                 s_all, t_all, H=H)
    dproj_a, dca, dal, ddt, grad_x, dhead = _gdn_in_bwd(dq, dk, dv, dz, dg_, dbeta, pre_a, raw, a_conv, alog, dtb,
                                                        a_w_in, dpre1, first_row=first_row, H=H)
    token = grads_ready("last", dict(a_w_in=_linear_dw(h0, dproj_a, name="dw_a_in")))
    grads_ready("tail", dict(a_w_out=_linear_dw(y, dpre1, name="dw_a_out", after=token),
                             b_w_out=_linear_dw(bu, dpre3, name="dw_b_out", after=token)))

    gs["meta"] = dhead[first_row:PADF]
    gs["a_conv"] = dca[0:a_conv.shape[0]][None]
    gs["a_log"] = dal[0:1, 0:H]
    gs["a_dt_bias"] = ddt[0:1, 0:H]
    gs["a_norm"] = dnw[0:1]
    gs["b_conv"] = dcb[0:b_conv.shape[0]][None]
    gs["ln_mix_g"] = jnp.stack([dlmg0[0], dlmg1[0]])
    gs["ln_mix_b"] = jnp.stack([dlmb0[0], dlmb1[0]])
    gs["ffn_conv"] = jnp.stack([dcf0, dcf1])
    gs["ln_ffn_g"] = jnp.stack([dlfg0[0], dlfg1[0]])
    gs["ln_ffn_b"] = jnp.stack([dlfb0[0], dlfb1[0]])
    return loss_tile, grad_x, gs


_BIG = ("a_w_in", "a_w_out", "b_w_in", "b_w_out", "ffn_w_up", "ffn_w_down")
_BIG_COL = ("a_w_in", "b_w_in", "ffn_w_up")
_SMALL = ("meta", "a_conv", "a_log", "a_dt_bias", "a_norm", "b_conv", "ln_mix_g", "ln_mix_b",
          "ffn_conv", "ln_ffn_g", "ln_ffn_b")
_SMALL_SHARDED = ("meta", "a_conv", "b_conv", "ffn_conv")
_ORDER = ("meta", "a_w_in", "a_conv", "a_log", "a_dt_bias", "a_norm", "a_w_out", "b_w_in", "b_conv", "b_w_out",
          "ln_mix_g", "ln_mix_b", "ffn_w_up", "ffn_conv", "ffn_w_down", "ln_ffn_g", "ln_ffn_b")


def _a_w_in_map(H):
    W4 = 4 * H * DH
    return [(0, W4, 0), (W4, W4 + H, W4), (W4 + H, W4 + 2 * H, W4 + LANE)], W4 + 2 * LANE


def kernel(x, meta, a_w_in, a_conv, a_log, a_dt_bias, a_norm, a_w_out, b_w_in, b_conv, b_w_out, ln_mix_g, ln_mix_b, ffn_w_up, ffn_conv, ffn_w_down, ln_ffn_g, ln_ffn_b, loss_target, m_meta, m_a_w_in, m_a_conv, m_a_log, m_a_dt_bias, m_a_norm, m_a_w_out, m_b_w_in, m_b_conv, m_b_w_out, m_ln_mix_g, m_ln_mix_b, m_ffn_w_up, m_ffn_conv, m_ffn_w_down, m_ln_ffn_g, m_ln_ffn_b, v_meta, v_a_w_in, v_a_conv, v_a_log, v_a_dt_bias, v_a_norm, v_a_w_out, v_b_w_in, v_b_conv, v_b_w_out, v_ln_mix_g, v_ln_mix_b, v_ffn_w_up, v_ffn_conv, v_ffn_w_down, v_ln_ffn_g, v_ln_ffn_b):
    wloc = dict(meta=meta, a_w_in=a_w_in, a_conv=a_conv, a_log=a_log, a_dt_bias=a_dt_bias, a_norm=a_norm,
                a_w_out=a_w_out, b_w_in=b_w_in, b_conv=b_conv, b_w_out=b_w_out, ln_mix_g=ln_mix_g, ln_mix_b=ln_mix_b,
                ffn_w_up=ffn_w_up, ffn_conv=ffn_conv, ffn_w_down=ffn_w_down, ln_ffn_g=ln_ffn_g, ln_ffn_b=ln_ffn_b)
    mloc = dict(meta=m_meta, a_w_in=m_a_w_in, a_conv=m_a_conv, a_log=m_a_log, a_dt_bias=m_a_dt_bias, a_norm=m_a_norm,
                a_w_out=m_a_w_out, b_w_in=m_b_w_in, b_conv=m_b_conv, b_w_out=m_b_w_out, ln_mix_g=m_ln_mix_g,
                ln_mix_b=m_ln_mix_b, ffn_w_up=m_ffn_w_up, ffn_conv=m_ffn_conv, ffn_w_down=m_ffn_w_down,
                ln_ffn_g=m_ln_ffn_g, ln_ffn_b=m_ln_ffn_b)
    vloc = dict(meta=v_meta, a_w_in=v_a_w_in, a_conv=v_a_conv, a_log=v_a_log, a_dt_bias=v_a_dt_bias, a_norm=v_a_norm,
                a_w_out=v_a_w_out, b_w_in=v_b_w_in, b_conv=v_b_conv, b_w_out=v_b_w_out, ln_mix_g=v_ln_mix_g,
                ln_mix_b=v_ln_mix_b, ffn_w_up=v_ffn_w_up, ffn_conv=v_ffn_conv, ffn_w_down=v_ffn_w_down,
                ln_ffn_g=v_ln_ffn_g, ln_ffn_b=v_ln_ffn_b)
    H = a_log.shape[-1]
    mx, my, mc = lax.axis_index("x"), lax.axis_index("y"), lax.axis_index("c")
    me = 4 * mx + 2 * my + mc

    a_map, a_cols = _a_w_in_map(H)
    col_maps = {"a_w_in": (a_map, a_cols)}
    for n in ("b_w_in", "ffn_w_up"):
        ncols = N_DEV * wloc[n].shape[-1]
        col_maps[n] = ([(0, ncols, 0)], ncols)
    sm_sh = [wloc[n] for n in _SMALL_SHARDED]
    sbuf, soffs = _pack_small(sm_sh, 128)
    g_a_w_in, g_a_w_out, sg = _all_gather([_bf(wloc["a_w_in"]), _bf(wloc["a_w_out"]), sbuf], name="gather_first")
    w_a_in = _assemble_cols(g_a_w_in, *col_maps["a_w_in"], name="assemble_a_w_in")[0]
    w_a_out = _rows_full(g_a_w_out)[0]
    late = [n for n in _BIG if n not in ("a_w_in", "a_w_out")]
    ssem, rsem, srcs_t, lands_t, start_token = _direct_start(
        [_bf(wloc[n]) for n in late], [lax.empty((N_DEV,) + wloc[n].shape, bf16) for n in late], False,
        name="gather_rest_start")

    def late_weights(after):
        srcs_d, landed = _direct_wait(ssem, rsem, srcs_t, lands_t, False, after, name="gather_rest_wait")
        wts = {}
        for n, own, got in zip(late, srcs_d, landed):
            full = lax.dynamic_update_index_in_dim(got, own, me, 0)
            if n in _BIG_COL:
                wts[n] = _assemble_cols(full, *col_maps[n], name="assemble_" + n)
            elif n == "ffn_w_down":
                wts[n] = full
            else:
                wts[n] = _rows_full(full)
        for n in ("b_w_in", "b_w_out"):
            wts[n] = wts[n][0]
        return wts

    small = {n: wloc[n] for n in _SMALL}
    for n, off in zip(_SMALL_SHARDED, soffs):
        sh = wloc[n].shape
        parts = jnp.stack([_unpack_small(sg[d], off, sh) for d in range(N_DEV)])
        nd = len(sh)
        small[n] = jnp.transpose(parts, tuple(range(1, nd)) + (0, nd)).reshape(sh[:-1] + (N_DEV * sh[-1],))

    def split(n, dws, tag):
        if n in _BIG_COL:
            return _split_cols(dws, col_maps[n][0], wloc[n].shape[-1], name="split_" + n + tag)
        return _split_rows(dws, wloc[n].shape[-2], name="split_" + n + tag)

    sent = {}

    def grads_ready(stage, grads):
        names = sorted(grads)
        parts = [split(n, [grads[n]], "_" + stage) for n in names]
        handles = _direct_start([p[1] for p in parts], [jnp.zeros(p[1].shape, bf16) for p in parts], True,
                                name="grads_" + stage + "_start")
        sent[stage] = (names, [p[0] for p in parts], handles)
        return handles[4]

    loss_tile, grad_x, gs = _local_step(x[0], loss_target[0], small["meta"], w_a_in, w_a_out, small, start_token,
                                        late_weights, grads_ready)

    def landed(stage, after):
        names, own32, (ssem_g, rsem_g, srcs_g, lands_g, _) = sent[stage]
        _, got = _direct_wait(ssem_g, rsem_g, srcs_g, lands_g, True, after, name="grads_" + stage + "_wait")
        return list(zip(names, own32, got))

    parts = {}
    for stage in ("layer0", "layer1"):
        for n, o32, r in landed(stage, grad_x):
            parts.setdefault(n, []).append((o32, r))
    me1 = jnp.stack([me]).astype(jnp.int32)
    big_out = {n: _adamw_direct([p[0] for p in ps], [p[1] for p in ps], wloc[n], mloc[n], vloc[n], me1,
                                name="adamw_" + n) for n, ps in parts.items()}
    names = list(_SMALL)
    pbuf, poffs = _pack_small([gs[n] for n in names] + [loss_tile[0:1, 0:1]], 1024)
    psum = _sum_devices(_all_gather([pbuf], name="gather_small_grads")[0])
    loss = psum[poffs[-1][0], 0]
    g_small = {}
    for n, off in zip(names, poffs[:-1]):
        full_shape = gs[n].shape
        gfull = _unpack_small(psum, off, full_shape)
        if n in _SMALL_SHARDED:
            ns = wloc[n].shape[-1]
            gfull = lax.dynamic_slice_in_dim(gfull, me * ns, ns, axis=gfull.ndim - 1)
        g_small[n] = gfull.reshape(wloc[n].shape)
    gbuf, aoffs = _pack_small([g_small[n] for n in names], 128)
    wbuf, _ = _pack_small([wloc[n] for n in names], 128)
    mbuf, _ = _pack_small([mloc[n] for n in names], 128)
    vbuf, _ = _pack_small([vloc[n] for n in names], 128)
    _, d_s, m_s, v_s = _adamw([gbuf], wbuf, mbuf, vbuf, name="adamw_small")

    done = d_s[0, 0]
    for out in big_out.values():
        done = done + out[1][0, 0, 0]
    for stage in ("last", "tail"):
        for n, o32, r in landed(stage, done.reshape(1, 1)):
            big_out[n] = _adamw_direct([o32], [r], wloc[n], mloc[n], vloc[n], me1, name="adamw_" + n)

    grads, deltas, new_m, new_v = {}, {}, {}, {}
    for n in _BIG:
        grads[n], deltas[n], new_m[n], new_v[n] = big_out[n]
    for n, off in zip(names, aoffs):
        sh = wloc[n].shape
        grads[n] = g_small[n]
        deltas[n], new_m[n], new_v[n] = (_unpack_small(b_, off, sh) for b_ in (d_s, m_s, v_s))
    return (loss, grad_x[None], *[grads[n] for n in _ORDER], *[deltas[n] for n in _ORDER],
            *[new_m[n] for n in _ORDER], *[new_v[n] for n in _ORDER])
```

```python
import math

import jax
import jax.numpy as jnp
from jax import lax
from jax.experimental import pallas as pl
from jax.experimental.pallas import tpu as pltpu

f32, bf16 = jnp.float32, jnp.bfloat16
S = jax.ShapeDtypeStruct
HI = lax.Precision.HIGHEST
MESH = pl.DeviceIdType.MESH

V7X_VMEM_LIMIT = 56 * 1024 * 1024
LANE = 128
DH = 128
CH = 64
PADF = 256
TM = 256
TMM = 768
N_DEV = 8
BWD_HEAD_GROUP = 4
BWD_GROUP_LAG = 4

DEPTH = 2
ALPHA = (2.0 * DEPTH) ** 0.25
LN_EPS = 1e-5
RMS_EPS = 1e-6
L2_EPS = 1e-6
ADAM_LR, ADAM_B1, ADAM_B2, ADAM_EPS, ADAM_WD, ADAM_STEP = 0.001, 0.9, 0.999, 1e-08, 0.01, 10


def _cp(**kw):
    return pltpu.CompilerParams(vmem_limit_bytes=V7X_VMEM_LIMIT, **kw)


def _bf(x):
    return x.astype(bf16)


def _dot(a, b, precision=None):
    return jnp.dot(a, b, preferred_element_type=f32, precision=precision)


def _dot_nt(a, b):
    return lax.dot_general(a, b, (((1,), (1,)), ((), ())), preferred_element_type=f32)


def _dot_tn(a, b):
    return lax.dot_general(a, b, (((0,), (0,)), ((), ())), preferred_element_type=f32)


def _sigmoid(x):
    return 1.0 / (1.0 + jnp.exp(-x))


def _load_once(pairs, sem):
    @pl.when(pl.program_id(0) == 0)
    def _():
        cps = [pltpu.make_async_copy(src, dst, sem.at[n]) for n, (src, dst) in enumerate(pairs)]
        for c in cps:
            c.start()
        for c in cps:
            c.wait()


def _row_ids(i, tm, width):
    return i * tm + lax.broadcasted_iota(jnp.int32, (tm, width), 0)


def _ln_fwd(pre, g, b, rows, first_row):
    mu = jnp.mean(pre, axis=-1, keepdims=True)
    xc = pre - mu
    var = jnp.mean(xc * xc, axis=-1, keepdims=True)
    y = xc * lax.rsqrt(var + LN_EPS) * g + b
    return jnp.where(rows >= first_row, y, 0.0)


ANY = pl.BlockSpec(memory_space=pl.ANY)


def _taps_back(prev8, x, kw):
    xe = jnp.concatenate([prev8, x], axis=0)
    return [pltpu.roll(xe, kw - 1 - j, 0)[8:] for j in range(kw - 1)] + [x]


def _taps_ahead(x, next8, kw):
    n = x.shape[0]
    xe = jnp.concatenate([x, next8], axis=0)
    return [pltpu.roll(xe, n + 8 - (kw - 1 - j), 0)[:n] for j in range(kw - 1)] + [x]


def _conv(cw, taps):
    acc = cw[0:1, :] * taps[0]
    for j in range(1, len(taps)):
        acc = acc + cw[j:j + 1, :] * taps[j]
    return acc


def _linear_dw(x, dy, *, name, after=None):
    L, K = x.shape
    N = dy.shape[1]
    tm = TMM if L % TMM == 0 else TM
    tn = LANE
    for d in range(N // LANE, 0, -1):
        if (N // LANE) % d == 0 and K * d * LANE * 4 <= 9 * 1024 * 1024:
            tn = d * LANE
            break

    def body(x_ref, dy_ref, *rest):
        o_ref = rest[-1]

        @pl.when(pl.program_id(1) == 0)
        def _():
            o_ref[...] = jnp.zeros_like(o_ref)
        o_ref[...] += _dot_tn(_bf(x_ref[...]), _bf(dy_ref[...]))

    in_specs = [pl.BlockSpec((tm, K), lambda j, i: (i, 0)), pl.BlockSpec((tm, tn), lambda j, i: (i, j))]
    args = [x, dy]
    if after is not None:
        in_specs.append(pl.BlockSpec(after.shape, lambda j, i: (0, 0)))
        args.append(after)
    return pl.pallas_call(
        body, name=name, grid=(N // tn, L // tm), out_shape=S((K, N), f32),
        in_specs=in_specs, out_specs=pl.BlockSpec((K, tn), lambda j, i: (0, j)),
        compiler_params=_cp(dimension_semantics=("arbitrary", "arbitrary")))(*args)


def _ln_bwd_rows(dout, pre, g, rows, first_row):
    mu = jnp.mean(pre, axis=-1, keepdims=True)
    xc = pre - mu
    rstd = lax.rsqrt(jnp.mean(xc * xc, axis=-1, keepdims=True) + LN_EPS)
    xh = xc * rstd
    dy = jnp.where(rows >= first_row, dout, 0.0)
    dxh = dy * g
    dpre = rstd * (dxh - jnp.mean(dxh, axis=-1, keepdims=True) - xh * jnp.mean(dxh * xh, axis=-1, keepdims=True))
    return dpre, jnp.sum(dy * xh, axis=0, keepdims=True), jnp.sum(dy, axis=0, keepdims=True)


def _gdn_in_fwd(x, head, w_full, conv_w, alog, dtb, *, first_row, H):
    D = x.shape[1]
    L = PADF + x.shape[0]
    W = H * DH
    NW = w_full.shape[1]
    KW = conv_w.shape[0]
    tm = TM
    pb = PADF // tm

    def body(x_ref, head_ref, w_hbm, cw_ref, alog_ref, dtb_ref,
             h_ref, pre_ref, z_ref, raw_ref, q_ref, k_ref, v_ref, beta_ref, g_ref, t_ref,
             w_vmem, carry, sem):
        i = pl.program_id(0)
        _load_once([(w_hbm, w_vmem)], sem)

        @pl.when(i == 0)
        def _():
            carry[...] = jnp.zeros_like(carry)

        hv = jnp.where(i < pb, head_ref[...], x_ref[...])
        h_ref[...] = hv
        hb = _bf(hv)
        outs = (q_ref, k_ref, v_ref)

        def section(s):
            pre = _dot(hb, w_vmem[:, s * W:(s + 1) * W])
            pre_ref[:, s * W:(s + 1) * W] = pre
            c = _conv(cw_ref[:, s * W:(s + 1) * W], _taps_back(carry[s], pre, KW))
            carry[s] = pre[tm - 8:tm, :]
            sl = c * _sigmoid(c)
            if s < 2:
                scale = DH ** -0.5 if s == 0 else 1.0
                for hh in range(H):
                    seg = sl[:, hh * DH:(hh + 1) * DH]
                    r = lax.rsqrt(jnp.sum(seg * seg, axis=-1, keepdims=True) + L2_EPS)
                    outs[s][:, hh * DH:(hh + 1) * DH] = seg * (r * scale)
            else:
                v_ref[...] = sl

        raw = _dot(hb, w_vmem[:, 4 * W:4 * W + 2 * LANE])
        raw_ref[...] = raw
        ok = (_row_ids(i, tm, LANE) >= first_row) & (lax.broadcasted_iota(jnp.int32, (tm, LANE), 1) < H)
        beta = jnp.where(ok, _sigmoid(raw[:, :LANE]), 0.0)
        beta_ref[...] = beta
        a = raw[:, LANE:] + dtb_ref[...]
        sp = jnp.maximum(a, 0.0) + jnp.log(1.0 + jnp.exp(-jnp.abs(a)))
        gv = jnp.where(ok, -jnp.exp(alog_ref[...]) * sp, 0.0)
        gam = _dot(_chunk_tri(tm, lower=True), gv, HI)
        g_ref[...] = gam
        section(1)
        ii = lax.broadcasted_iota(jnp.int32, (CH, CH), 0)
        jj = lax.broadcasted_iota(jnp.int32, (CH, CH), 1)
        eye = (ii == jj).astype(f32)
        gam_t = gam.T

        def inverses(chunks):
            ms = []
            for c in chunks:
                rows = slice(c * CH, (c + 1) * CH)
                for hh in range(H):
                    kh = k_ref[rows, hh * DH:(hh + 1) * DH]
                    dec = jnp.exp(jnp.minimum(gam[rows, hh:hh + 1] - gam_t[hh:hh + 1, rows], 0.0))
                    kk = _dot_nt(_bf(kh * beta[rows, hh:hh + 1]), _bf(kh))
                    ms.append(jnp.where(ii > jj, kk * dec, 0.0))
            for n, t in enumerate(_tri_inv_many(ms, eye)):
                t_ref[chunks[n // H], n % H] = t

        section(2)
        z_ref[...] = _dot(hb, w_vmem[:, 3 * W:4 * W])
        section(0)
        inverses(list(range(tm // CH)))

    row = lambda i: (i, 0)
    fix = lambda i: (0, 0)
    out_shape = (S((L, D), f32), S((L, 3 * W), f32), S((L, W), f32), S((L, 2 * LANE), f32),
                 S((L, W), f32), S((L, W), f32), S((L, W), f32), S((L, LANE), f32), S((L, LANE), f32),
                 S((L // CH, H, CH, CH), f32))
    out_specs = (pl.BlockSpec((tm, D), row),
                 pl.BlockSpec((tm, 3 * W), row), pl.BlockSpec((tm, W), row), pl.BlockSpec((tm, 2 * LANE), row),
                 pl.BlockSpec((tm, W), row), pl.BlockSpec((tm, W), row), pl.BlockSpec((tm, W), row),
                 pl.BlockSpec((tm, LANE), row), pl.BlockSpec((tm, LANE), row),
                 pl.BlockSpec((tm // CH, H, CH, CH), lambda i: (i, 0, 0, 0)))
    return pl.pallas_call(
        body, name="gdn_in_fwd", grid=(L // tm,), out_shape=out_shape,
        in_specs=[pl.BlockSpec((tm, D), lambda i: (jnp.maximum(i - pb, 0), 0)),
                  pl.BlockSpec((tm, D), lambda i: (jnp.minimum(i, pb - 1), 0)), ANY, pl.BlockSpec((KW, 3 * W), fix),
                  pl.BlockSpec((1, LANE), fix), pl.BlockSpec((1, LANE), fix)],
        out_specs=out_specs,
        scratch_shapes=[pltpu.VMEM((D, NW), w_full.dtype), pltpu.VMEM((3, 8, W), f32), pltpu.SemaphoreType.DMA((1,))],
        compiler_params=_cp(dimension_semantics=("arbitrary",)))(x, head, w_full, conv_w, alog, dtb)


def _gdn_in_bwd(dq, dk, dv, dz, dg, dbeta, pre, raw, conv_w, alog, dtb, w_full, res, *, first_row, H):
    L = dq.shape[0]
    D = res.shape[1]
    W = H * DH
    KW = conv_w.shape[0]
    tm = TM
    nb = L // tm
    NW = 4 * W + 2 * LANE
    fb = PADF // tm
    alpha = ALPHA

    def body(dq_ref, dk_ref, dv_ref, dz_ref, dg_ref, dbeta_ref, pre_ref, hq_ref, hk_ref, hv_ref, raw_ref,
             cw_ref, alog_ref, dtb_ref, w_hbm, res_ref,
             dproj_ref, dcw_ref, dal_ref, ddt_ref, dx_ref, dfront_ref, w_vmem, carry, tmp, sem):
        i = pl.program_id(0)
        blk = nb - 1 - i
        _load_once([(w_hbm, w_vmem)], sem)

        @pl.when(i == 0)
        def _():
            carry[...] = jnp.zeros_like(carry)
            dcw_ref[...] = jnp.zeros_like(dcw_ref)
            dal_ref[...] = jnp.zeros_like(dal_ref)
            ddt_ref[...] = jnp.zeros_like(ddt_ref)

        halos = (hq_ref, hk_ref, hv_ref)
        douts = (dq_ref, dk_ref, dv_ref)
        for s in range(3):
            sec = slice(s * W, (s + 1) * W)
            pre = pre_ref[:, sec]
            c = _conv(cw_ref[:, sec], _taps_back(jnp.where(blk > 0, halos[s][...], 0.0), pre, KW))
            sig = _sigmoid(c)
            sl = c * sig
            if s < 2:
                scale = DH ** -0.5 if s == 0 else 1.0
                for hh in range(H):
                    hs = slice(hh * DH, (hh + 1) * DH)
                    seg = sl[:, hs]
                    r = lax.rsqrt(jnp.sum(seg * seg, axis=-1, keepdims=True) + L2_EPS)
                    n = seg * r
                    dqs = douts[s][:, hs]
                    tmp[:, hs] = (scale * r) * (dqs - n * jnp.sum(n * dqs, axis=-1, keepdims=True))
                dsl = tmp[...]
            else:
                dsl = dv_ref[...]
            dc = dsl * (sig * (1.0 + c * (1.0 - sig)))
            ahead = _taps_ahead(dc, carry[s], KW)
            carry[s] = dc[0:8, :]
            dproj_ref[:, sec] = _bf(_conv(cw_ref[:, sec], ahead))
            for j in range(KW):
                dcw_ref[j:j + 1, sec] += jnp.sum(ahead[j] * pre, axis=0, keepdims=True)
        dproj_ref[:, 3 * W:4 * W] = _bf(dz_ref[...])
        raw_v = raw_ref[...]
        ok = (_row_ids(blk, tm, LANE) >= first_row) & (lax.broadcasted_iota(jnp.int32, (tm, LANE), 1) < H)
        beta = _sigmoid(raw_v[:, :LANE])
        dbraw = jnp.where(ok, dbeta_ref[...] * beta * (1.0 - beta), 0.0)
        a = raw_v[:, LANE:] + dtb_ref[...]
        sp = jnp.maximum(a, 0.0) + jnp.log(1.0 + jnp.exp(-jnp.abs(a)))
        nea = -jnp.exp(alog_ref[...])
        dgm = jnp.where(ok, _dot(_chunk_tri(tm, lower=False), dg_ref[...], HI), 0.0)
        daraw = dgm * nea * _sigmoid(a)
        dal_ref[0:1, :] += jnp.sum(dgm * nea * sp, axis=0, keepdims=True)
        ddt_ref[0:1, :] += jnp.sum(daraw, axis=0, keepdims=True)
        dproj_ref[:, 4 * W:4 * W + LANE] = _bf(dbraw)
        dproj_ref[:, 4 * W + LANE:4 * W + 2 * LANE] = _bf(daraw)
        dh = alpha * res_ref[...] + _dot_nt(dproj_ref[...], w_vmem[...])

        @pl.when(blk >= fb)
        def _():
            dx_ref[...] = dh

        @pl.when(blk < fb)
        def _():
            dfront_ref[...] = dh

    rev = lambda i: (nb - 1 - i, 0)
    fix = lambda i: (0, 0)

    def halo(col):
        return pl.BlockSpec((8, W), lambda i: (jnp.maximum((nb - 1 - i) * (tm // 8) - 1, 0), col))

    return pl.pallas_call(
        body, name="gdn_in_bwd", grid=(nb,),
        out_shape=(S((L, NW), bf16), S((8, 3 * W), f32), S((8, LANE), f32), S((8, LANE), f32),
                   S((L - PADF, D), f32), S((PADF, D), f32)),
        in_specs=[pl.BlockSpec((tm, W), rev)] * 4 + [pl.BlockSpec((tm, LANE), rev)] * 2
        + [pl.BlockSpec((tm, 3 * W), rev), halo(0), halo(1), halo(2), pl.BlockSpec((tm, 2 * LANE), rev),
           pl.BlockSpec((KW, 3 * W), fix), pl.BlockSpec((1, LANE), fix), pl.BlockSpec((1, LANE), fix),
           ANY, pl.BlockSpec((tm, D), rev)],
        out_specs=(pl.BlockSpec((tm, NW), rev), pl.BlockSpec((8, 3 * W), fix),
                   pl.BlockSpec((8, LANE), fix), pl.BlockSpec((8, LANE), fix),
                   pl.BlockSpec((tm, D), lambda i: (jnp.maximum(nb - 1 - i - fb, 0), 0)),
                   pl.BlockSpec((tm, D), lambda i: (jnp.minimum(nb - 1 - i, fb - 1), 0))),
        scratch_shapes=[pltpu.VMEM((D, NW), w_full.dtype), pltpu.VMEM((3, 8, W), f32), pltpu.VMEM((tm, W), f32),
                        pltpu.SemaphoreType.DMA((1,))],
        compiler_params=_cp(dimension_semantics=("arbitrary",)))(
            dq, dk, dv, dz, dg, dbeta, pre, pre, pre, pre, raw, conv_w, alog, dtb, w_full, res)


def _chunk_tri(n, lower):
    i = lax.broadcasted_iota(jnp.int32, (n, n), 0)
    j = lax.broadcasted_iota(jnp.int32, (n, n), 1)
    sh = int(math.log2(CH))
    same = lax.shift_right_logical(i, sh) == lax.shift_right_logical(j, sh)
    return (same & ((i >= j) if lower else (j >= i))).astype(f32)


def _tri_inv_many(ms, eye):
    ts = [eye - m for m in ms]
    ps = list(ms)
    for _ in range(int(math.log2(CH)) - 1):
        pb = [_bf(p) for p in ps]
        ps = [_dot(p, p) for p in pb]
        ts = [t + _dot(_bf(t), _bf(p)) for t, p in zip(ts, ps)]
    return ts


def _chunk_local(q, k, v, gcol, grow, glast, bcol, ii, jj):
    dec = jnp.where(ii >= jj, jnp.exp(jnp.minimum(gcol - grow, 0.0)), 0.0)
    eg = jnp.exp(gcol)
    kb = k * bcol
    kbg = kb * eg
    vb = v * bcol
    qt = q * eg
    kt = k * jnp.exp(glast - gcol)
    kbb, qb, kbf = _bf(kb), _bf(q), _bf(k)
    return dec, eg, kb, kbg, vb, qt, kt, _dot_nt(kbb, kbf), _dot_nt(qb, kbf), jnp.concatenate([kbb, qb], axis=0)


def _delta_fwd(q, k, v, g, beta, t_all, z, nw, h, w_out, ln_g, ln_b, *, first_row, H):
    L = q.shape[0]
    W = H * DH
    D = h.shape[1]
    rb = TM
    nc = rb // CH
    nblk = L // rb
    alpha = ALPHA

    def body(q_ref, k_ref, v_ref, g_ref, b_ref, t_ref, z_ref, nw_ref, h_ref, wout_hbm, lg_ref, lb_ref,
             o_ref, y_ref, s_out, pre_ref, out_ref, s_scr, wout, sem):
        _load_once([(wout_hbm, wout)], sem)

        @pl.when(pl.program_id(0) == 0)
        def _():
            s_scr[...] = jnp.zeros_like(s_scr)

        ii = lax.broadcasted_iota(jnp.int32, (CH, CH), 0)
        jj = lax.broadcasted_iota(jnp.int32, (CH, CH), 1)
        eye = (ii == jj).astype(f32)
        nwv = nw_ref[...]

        heads = range(H)
        hsl = [slice(hh * DH, (hh + 1) * DH) for hh in heads]

        def chunk(c, carry):
            r0 = pl.multiple_of(c * CH, CH)
            rows = pl.ds(r0, CH)
            gam = g_ref[rows, :]
            gam_t = gam.T
            bb = b_ref[rows, :]
            glast = [gam[CH - 1:CH, hh:hh + 1] for hh in heads]
            loc = [_chunk_local(q_ref[rows, hsl[hh]], k_ref[rows, hsl[hh]], v_ref[rows, hsl[hh]],
                                gam[:, hh:hh + 1], gam_t[hh:hh + 1, :], glast[hh], bb[:, hh:hh + 1], ii, jj)
                   for hh in heads]
            st = [s_scr[hh] for hh in heads]
            zs = [z_ref[rows, hsl[hh]] for hh in heads]
            ts = [_bf(t_ref[c, hh]) for hh in heads]
            us = [_dot(t, _bf(l[4])) for t, l in zip(ts, loc)]
            ws = [_dot(t, _bf(l[3])) for t, l in zip(ts, loc)]
            stb = [_bf(s) for s in st]
            vn = [u - _dot(_bf(w), sb) for u, w, sb in zip(us, ws, stb)]
            vnb = [_bf(x) for x in vn]
            snew = [s * jnp.exp(gl) + _dot_tn(_bf(l[6]), xb) for s, gl, l, xb in zip(st, glast, loc, vnb)]
            os_ = [_dot(_bf(l[5]), sb) + _dot(_bf(l[8] * l[0]), xb) for l, sb, xb in zip(loc, stb, vnb)]
            for hh in heads:
                o = os_[hh]
                s_out[c, hh] = st[hh]
                s_scr[hh] = snew[hh]
                o_ref[rows, hsl[hh]] = o
                on = o * lax.rsqrt(jnp.mean(o * o, axis=-1, keepdims=True) + RMS_EPS) * nwv
                y_ref[rows, hsl[hh]] = _bf(on * (zs[hh] * _sigmoid(zs[hh])))
            return carry

        lax.fori_loop(0, nc, chunk, 0)
        pre = alpha * h_ref[...] + _dot(y_ref[...], wout[...])
        pre_ref[...] = pre
        out_ref[...] = _ln_fwd(pre, lg_ref[...], lb_ref[...], _row_ids(pl.program_id(0), rb, D), first_row)

    row = lambda i: (i, 0)
    fix = lambda i: (0, 0)
    return pl.pallas_call(
        body, name="delta_fwd", grid=(nblk,),
        out_shape=(S((L, W), f32), S((L, W), bf16), S((L // CH, H, DH, DH), f32), S((L, D), f32), S((L, D), f32)),
        in_specs=[pl.BlockSpec((rb, W), row)] * 3 + [pl.BlockSpec((rb, LANE), row)] * 2
        + [pl.BlockSpec((nc, H, CH, CH), lambda i: (i, 0, 0, 0)),
           pl.BlockSpec((rb, W), row), pl.BlockSpec((1, DH), fix), pl.BlockSpec((rb, D), row), ANY,
           pl.BlockSpec((1, D), fix), pl.BlockSpec((1, D), fix)],
        out_specs=(pl.BlockSpec((rb, W), row), pl.BlockSpec((rb, W), row),
                   pl.BlockSpec((nc, H, DH, DH), lambda i: (i, 0, 0, 0)),
                   pl.BlockSpec((rb, D), row), pl.BlockSpec((rb, D), row)),
        scratch_shapes=[pltpu.VMEM((H, DH, DH), f32), pltpu.VMEM((W, D), w_out.dtype), pltpu.SemaphoreType.DMA((1,))],
        compiler_params=_cp(dimension_semantics=("arbitrary",)))(q, k, v, g, beta, t_all, z, nw, h, w_out, ln_g, ln_b)


def _delta_bwd(dpre, w_out, o, z, nw, q, k, v, g, beta, s_all, t_all, *, H):
    L = q.shape[0]
    W = H * DH
    D = dpre.shape[1]
    rb = TM
    nc = rb // CH
    nblk = L // rb

    def body(dpre_ref, wout_hbm, o_ref, z_ref, nw_ref, q_ref, k_ref, v_ref, g_ref, b_ref, s_ref, t_ref,
             dq_ref, dk_ref, dv_ref, dz_ref, dg_ref, db_ref, dnw_ref, ds_scr, wout, dy_scr, sem):
        _load_once([(wout_hbm, wout)], sem)

        @pl.when(pl.program_id(0) == 0)
        def _():
            ds_scr[...] = jnp.zeros_like(ds_scr)
            dnw_ref[...] = jnp.zeros_like(dnw_ref)

        dy_scr[...] = _dot_nt(_bf(dpre_ref[...]), wout[...])

        ii = lax.broadcasted_iota(jnp.int32, (CH, CH), 0)
        jj = lax.broadcasted_iota(jnp.int32, (CH, CH), 1)
        lane = lax.broadcasted_iota(jnp.int32, (CH, LANE), 1)
        last_row = lax.broadcasted_iota(jnp.int32, (CH, 1), 0) == CH - 1
        nwv = nw_ref[...]

        def chunk(cc, carry):
            c = nc - 1 - cc
            r0 = pl.multiple_of(c * CH, CH)
            rows = pl.ds(r0, CH)
            gam = g_ref[rows, :]
            gam_t = gam.T
            bb = b_ref[rows, :]

            def head(hh):
                hs = slice(hh * DH, (hh + 1) * DH)
                gcol, grow, glast = gam[:, hh:hh + 1], gam_t[hh:hh + 1, :], gam[CH - 1:CH, hh:hh + 1]
                bcol = bb[:, hh:hh + 1]
                qh, kh, vh = q_ref[rows, hs], k_ref[rows, hs], v_ref[rows, hs]
                oh, zh, dyh = o_ref[rows, hs], z_ref[rows, hs], dy_scr[rows, hs]
                t = t_ref[c, hh]
                st = s_ref[c, hh]
                dsn = ds_scr[hh]
                rms = lax.rsqrt(jnp.mean(oh * oh, axis=-1, keepdims=True) + RMS_EPS)
                on = oh * rms
                sig = _sigmoid(zh)
                sz = zh * sig
                dz_ref[rows, hs] = dyh * on * nwv * (sig * (1.0 + zh * (1.0 - sig)))
                dnw = jnp.sum(dyh * on * sz, axis=0, keepdims=True)
                don = dyh * nwv * sz
                do = rms * (don - on * jnp.mean(don * on, axis=-1, keepdims=True))
                dec, eg, kb, kbg, vb, qt, kt, kk, qk, kqb = _chunk_local(qh, kh, vh, gcol, grow, glast, bcol, ii, jj)
                stb, dsnb, dob, tb, kbgb = _bf(st), _bf(dsn), _bf(do), _bf(t), _bf(kbg)
                r = vb - _dot(kbgb, stb)
                yield
                mm = jnp.where(ii > jj, kk * dec, 0.0)
                attn = qk * dec
                yield
                rbf = _bf(r)
                vn = _dot(tb, rbf)
                dvn = _dot_tn(_bf(attn), dob) + _dot(_bf(kt), dsnb)
                egl = jnp.exp(glast)
                ekt = jnp.exp(glast - gcol)
                yield
                vnb, dvnb = _bf(vn), _bf(dvn)
                dattn = jnp.where(ii >= jj, _dot_nt(dob, vnb), 0.0)
                dkt = _dot_nt(vnb, dsnb)
                dvb = _dot_tn(tb, dvnb)
                dt = _dot_nt(dvnb, rbf)
                yield
                dglast = egl * jnp.sum(jnp.sum(dsn * st, axis=0, keepdims=True), axis=1, keepdims=True)
                yield
                dv_ref[rows, hs] = dvb * bcol
                dod = jnp.concatenate([dob, -_bf(dvb)], axis=0)
                ds_scr[hh] = egl * dsn + _dot_tn(jnp.concatenate([_bf(qt), kbgb], axis=0), dod)
                both = _dot_nt(dod, stb)
                dqt, dkbg = both[:CH], both[CH:]
                x = _dot_nt(_bf(dt), tb)
                yield
                dm = jnp.where(ii > jj, -_dot_tn(tb, _bf(x)), 0.0)
                dkk = dm * dec
                dqk = dattn * dec
                e = dm * mm + dattn * attn
                dgam = jnp.sum(e, axis=1, keepdims=True) - jnp.sum(e.T, axis=1, keepdims=True)
                dd = _bf(jnp.concatenate([dkk, dqk], axis=0))
                both = _dot(dd, _bf(kh))
                dkb = both[:CH] + dkbg * eg
                yield
                dk_ref[rows, hs] = _dot_tn(dd, kqb) + dkt * ekt + dkb * bcol
                dq_ref[rows, hs] = both[CH:] + dqt * eg
                yield
                dktkt = dkt * kt
                dgam = dgam + jnp.sum(dqt * qt - dktkt + dkbg * kbg, axis=1, keepdims=True)
                dglast = dglast + jnp.sum(jnp.sum(dktkt, axis=0, keepdims=True), axis=1, keepdims=True)
                dgam = dgam + jnp.where(last_row, dglast, 0.0)
                dbeta = jnp.sum(dkb * kh + dvb * vh, axis=1, keepdims=True)
                return dgam, dbeta, dnw

            res = [None] * H
            gens = [head(hh) for hh in range(H)]
            step = 0
            while any(r is None for r in res):
                for hh in range(H):
                    if res[hh] is None and step >= (hh // BWD_HEAD_GROUP) * BWD_GROUP_LAG:
                        try:
                            next(gens[hh])
                        except StopIteration as stop:
                            res[hh] = stop.value
                step += 1
            dgam_all = jnp.zeros((CH, LANE), f32)
            dbeta_all = jnp.zeros((CH, LANE), f32)
            dnw_acc = jnp.zeros((1, DH), f32)
            for hh in range(H):
                dgam, dbeta, dnw = res[hh]
                dgam_all = dgam_all + jnp.where(lane == hh, dgam, 0.0)
                dbeta_all = dbeta_all + jnp.where(lane == hh, dbeta, 0.0)
                dnw_acc = dnw_acc + dnw
            dg_ref[rows, :] = dgam_all
            db_ref[rows, :] = dbeta_all
            dnw_ref[0:1, :] += dnw_acc
            return carry

        lax.fori_loop(0, nc, chunk, 0)

    rev = lambda i: (nblk - 1 - i, 0)
    rev4 = lambda i: (nblk - 1 - i, 0, 0, 0)
    fix = lambda i: (0, 0)
    wide = pl.BlockSpec((rb, W), rev)
    thin = pl.BlockSpec((rb, LANE), rev)
    return pl.pallas_call(
        body, name="delta_bwd", grid=(nblk,),
        out_shape=(S((L, W), f32),) * 4 + (S((L, LANE), f32),) * 2 + (S((8, DH), f32),),
        in_specs=[pl.BlockSpec((rb, D), rev), ANY, wide, wide, pl.BlockSpec((1, DH), fix), wide, wide, wide, thin, thin,
                  pl.BlockSpec((nc, H, DH, DH), rev4), pl.BlockSpec((nc, H, CH, CH), rev4)],
        out_specs=(wide,) * 4 + (thin, thin, pl.BlockSpec((8, DH), fix)),
        scratch_shapes=[pltpu.VMEM((H, DH, DH), f32), pltpu.VMEM((W, D), w_out.dtype), pltpu.VMEM((rb, W), f32),
                        pltpu.SemaphoreType.DMA((1,))],
        compiler_params=_cp(dimension_semantics=("arbitrary",)))(
            dpre, w_out, o, z, nw, q, k, v, g, beta, s_all, t_all)


def _sc_fwd(h, w_in, conv_w, w_out, g, b, *, first_row):
    L, D = h.shape
    W = w_out.shape[0]
    KW = conv_w.shape[0]
    tm = TM
    alpha = ALPHA

    def body(h_ref, win_hbm, cw_ref, wout_hbm, g_ref, b_ref, proj_ref, bu_ref, pre_ref, out_ref,
             win, wout, carry, sem):
        i = pl.program_id(0)
        _load_once([(win_hbm, win), (wout_hbm, wout)], sem)

        @pl.when(i == 0)
        def _():
            carry[...] = jnp.zeros_like(carry)

        hv = h_ref[...]
        hb = _bf(hv)
        bg = _dot(hb, win[:, 0:W])
        cg = _dot(hb, win[:, W:2 * W])
        xv = _dot(hb, win[:, 2 * W:3 * W])
        proj_ref[:, 0:W] = bg
        proj_ref[:, W:2 * W] = cg
        proj_ref[:, 2 * W:3 * W] = xv
        p = cg * xv
        u = _conv(cw_ref[...], _taps_back(carry[...], p, KW))
        carry[...] = p[tm - 8:tm, :]
        bu = _bf(bg * u)
        bu_ref[...] = bu
        pre = alpha * hv + _dot(bu, wout[...])
        pre_ref[...] = pre
        out_ref[...] = _ln_fwd(pre, g_ref[...], b_ref[...], _row_ids(i, tm, D), first_row)

    row = lambda i: (i, 0)
    fix = lambda i: (0, 0)
    return pl.pallas_call(
        body, name="sc_fwd", grid=(L // tm,),
        out_shape=(S((L, 3 * W), f32), S((L, W), bf16), S((L, D), f32), S((L, D), f32)),
        in_specs=[pl.BlockSpec((tm, D), row), ANY, pl.BlockSpec((KW, W), fix), ANY,
                  pl.BlockSpec((1, D), fix), pl.BlockSpec((1, D), fix)],
        out_specs=(pl.BlockSpec((tm, 3 * W), row), pl.BlockSpec((tm, W), row),
                   pl.BlockSpec((tm, D), row), pl.BlockSpec((tm, D), row)),
        scratch_shapes=[pltpu.VMEM((D, 3 * W), w_in.dtype), pltpu.VMEM((W, D), w_out.dtype),
                        pltpu.VMEM((8, W), f32), pltpu.SemaphoreType.DMA((2,))],
        compiler_params=_cp(dimension_semantics=("arbitrary",)))(h, w_in, conv_w, w_out, g, b)


def _sc_bwd(dpre, proj, conv_w, w_out, w_in, pre_in, g_in, *, first_row):
    L, D = dpre.shape
    W = w_out.shape[0]
    KW = conv_w.shape[0]
    tm = TM
    nb = L // tm
    alpha = ALPHA

    def body(dpre_ref, proj_ref, hc_ref, hx_ref, cw_ref, wout_hbm, win_hbm, pin_ref, g_ref,
             dproj_ref, dcw_ref, dpin_ref, dg_ref, db_ref, wout, win, carry, sem):
        i = pl.program_id(0)
        blk = nb - 1 - i
        _load_once([(wout_hbm, wout), (win_hbm, win)], sem)

        @pl.when(i == 0)
        def _():
            carry[...] = jnp.zeros_like(carry)
            dcw_ref[...] = jnp.zeros_like(dcw_ref)
            dg_ref[...] = jnp.zeros_like(dg_ref)
            db_ref[...] = jnp.zeros_like(db_ref)

        bg, cg, xv = proj_ref[:, 0:W], proj_ref[:, W:2 * W], proj_ref[:, 2 * W:3 * W]
        p = cg * xv
        u = _conv(cw_ref[...], _taps_back(jnp.where(blk > 0, hc_ref[...] * hx_ref[...], 0.0), p, KW))
        dpre_v = dpre_ref[...]
        d = _dot_nt(_bf(dpre_v), wout[...])
        dproj_ref[:, 0:W] = _bf(d * u)
        du = d * bg
        ahead = _taps_ahead(du, carry[...], KW)
        carry[...] = du[0:8, :]
        dp = _conv(cw_ref[...], ahead)
        for j in range(KW):
            dcw_ref[j:j + 1, :] += jnp.sum(ahead[j] * p, axis=0, keepdims=True)
        dproj_ref[:, W:2 * W] = _bf(dp * xv)
        dproj_ref[:, 2 * W:3 * W] = _bf(dp * cg)
        dh = alpha * dpre_v + _dot_nt(dproj_ref[...], win[...])
        dpin, dg, dbias = _ln_bwd_rows(dh, pin_ref[...], g_ref[...], _row_ids(blk, tm, D), first_row)
        dpin_ref[...] = dpin
        dg_ref[0:1, :] += dg
        db_ref[0:1, :] += dbias

    rev = lambda i: (nb - 1 - i, 0)
    fix = lambda i: (0, 0)

    def halo(col):
        return pl.BlockSpec((8, W), lambda i: (jnp.maximum((nb - 1 - i) * (tm // 8) - 1, 0), col))

    return pl.pallas_call(
        body, name="sc_bwd", grid=(nb,),
        out_shape=(S((L, 3 * W), bf16), S((8, W), f32), S((L, D), f32), S((8, D), f32), S((8, D), f32)),
        in_specs=[pl.BlockSpec((tm, D), rev), pl.BlockSpec((tm, 3 * W), rev), halo(1), halo(2),
                  pl.BlockSpec((KW, W), fix), ANY, ANY, pl.BlockSpec((tm, D), rev), pl.BlockSpec((1, D), fix)],
        out_specs=(pl.BlockSpec((tm, 3 * W), rev), pl.BlockSpec((8, W), fix), pl.BlockSpec((tm, D), rev),
                   pl.BlockSpec((8, D), fix), pl.BlockSpec((8, D), fix)),
        scratch_shapes=[pltpu.VMEM((W, D), w_out.dtype), pltpu.VMEM((D, 3 * W), w_in.dtype), pltpu.VMEM((8, W), f32),
                        pltpu.SemaphoreType.DMA((2,))],
        compiler_params=_cp(dimension_semantics=("arbitrary",)))(
            dpre, proj, proj, proj, conv_w, w_out, w_in, pre_in, g_in)


def _ffn_cols(F):
    fc = F
    for cand in (1408, 1024, 512, 256, 128):
        if F % cand == 0:
            fc = cand
            break
    return fc


def _ffn_weight_copies(wup_hbm, wdn_hbm, wup, wdn, layer):
    k = wdn_hbm.shape[2]
    return [(wup_hbm.at[layer], wup)] + [(wdn_hbm.at[p, layer], wdn.at[pl.ds(p * k, k), :]) for p in range(N_DEV)]


def _ffn_fwd(h, w_up, conv_w, w_down, g, b, *, layer, first_row, name):
    L, D = h.shape
    F = N_DEV * w_down.shape[2]
    KW = conv_w.shape[0]
    tm = TM
    fc = _ffn_cols(F)
    alpha = ALPHA

    def body(h_ref, wup_hbm, cw_ref, wdn_hbm, g_ref, b_ref, up_ref, a_ref, pre_ref, out_ref,
             wup, wdn, carry, sem):
        i = pl.program_id(0)
        _load_once(_ffn_weight_copies(wup_hbm, wdn_hbm, wup, wdn, layer), sem)

        @pl.when(i == 0)
        def _():
            carry[...] = jnp.zeros_like(carry)

        hv = h_ref[...]
        hb = _bf(hv)
        pre = alpha * hv
        for c0 in range(0, F, fc):
            cs = slice(c0, c0 + fc)
            u = _dot(hb, wup[:, cs])
            gate = _dot(hb, wup[:, F + c0:F + c0 + fc])
            up_ref[:, cs] = u
            up_ref[:, F + c0:F + c0 + fc] = gate
            uc = _conv(cw_ref[:, cs], _taps_back(carry[:, cs], u, KW))
            carry[:, cs] = u[tm - 8:tm, :]
            ab = _bf(uc * _sigmoid(uc) * gate)
            a_ref[:, cs] = ab
            pre = pre + _dot(ab, wdn[cs, :])
        pre_ref[...] = pre
        out_ref[...] = _ln_fwd(pre, g_ref[...], b_ref[...], _row_ids(i, tm, D), first_row)

    row = lambda i: (i, 0)
    fix = lambda i: (0, 0)
    return pl.pallas_call(
        body, name=name, grid=(L // tm,),
        out_shape=(S((L, 2 * F), f32), S((L, F), bf16), S((L, D), f32), S((L, D), f32)),
        in_specs=[pl.BlockSpec((tm, D), row), ANY, pl.BlockSpec((KW, F), fix), ANY,
                  pl.BlockSpec((1, D), fix), pl.BlockSpec((1, D), fix)],
        out_specs=(pl.BlockSpec((tm, 2 * F), row), pl.BlockSpec((tm, F), row),
                   pl.BlockSpec((tm, D), row), pl.BlockSpec((tm, D), row)),
        scratch_shapes=[pltpu.VMEM((D, 2 * F), w_up.dtype), pltpu.VMEM((F, D), w_down.dtype),
                        pltpu.VMEM((8, F), f32), pltpu.SemaphoreType.DMA((1 + N_DEV,))],
        compiler_params=_cp(dimension_semantics=("arbitrary",)))(h, w_up, conv_w, w_down, g, b)


def _ffn_bwd(dpre, up, w_down, conv_w, w_up, pre_in, g_in, *, layer, first_row, name):
    L, D = dpre.shape
    F = N_DEV * w_down.shape[2]
    KW = conv_w.shape[0]
    tm = TM
    nb = L // tm
    fc = F
    alpha = ALPHA

    def body(dpre_ref, up_ref, halo_ref, wdn_hbm, cw_ref, wup_hbm, pin_ref, g_ref,
             dup_ref, dcw_ref, dpin_ref, dg_ref, db_ref, wdn, wup, carry, sem):
        i = pl.program_id(0)
        blk = nb - 1 - i
        _load_once(_ffn_weight_copies(wup_hbm, wdn_hbm, wup, wdn, layer), sem)

        @pl.when(i == 0)
        def _():
            carry[...] = jnp.zeros_like(carry)
            dcw_ref[...] = jnp.zeros_like(dcw_ref)
            dg_ref[...] = jnp.zeros_like(dg_ref)
            db_ref[...] = jnp.zeros_like(db_ref)

        dpre_v = dpre_ref[...]
        db = _bf(dpre_v)
        dh = alpha * dpre_v
        for c0 in range(0, F, fc):
            cs = slice(c0, c0 + fc)
            gs_ = slice(F + c0, F + c0 + fc)
            da = _dot_nt(db, wdn[cs, :])
            gate = up_ref[:, gs_]
            u = up_ref[:, cs]
            uc = _conv(cw_ref[:, cs], _taps_back(jnp.where(blk > 0, halo_ref[:, cs], 0.0), u, KW))
            sig = _sigmoid(uc)
            dgate = _bf(da * (uc * sig))
            dup_ref[:, gs_] = dgate
            duc = da * gate * (sig * (1.0 + uc * (1.0 - sig)))
            ahead = _taps_ahead(duc, carry[:, cs], KW)
            carry[:, cs] = duc[0:8, :]
            du = _bf(_conv(cw_ref[:, cs], ahead))
            dup_ref[:, cs] = du
            for j in range(KW):
                dcw_ref[j:j + 1, cs] += jnp.sum(ahead[j] * u, axis=0, keepdims=True)
            dh = dh + _dot_nt(du, wup[:, cs]) + _dot_nt(dgate, wup[:, gs_])
        dpin, dg, dbias = _ln_bwd_rows(dh, pin_ref[...], g_ref[...], _row_ids(blk, tm, D), first_row)
        dpin_ref[...] = dpin
        dg_ref[0:1, :] += dg
        db_ref[0:1, :] += dbias

    rev = lambda i: (nb - 1 - i, 0)
    fix = lambda i: (0, 0)
    return pl.pallas_call(
        body, name=name, grid=(nb,),
        out_shape=(S((L, 2 * F), bf16), S((8, F), f32), S((L, D), f32), S((8, D), f32), S((8, D), f32)),
        in_specs=[pl.BlockSpec((tm, D), rev), pl.BlockSpec((tm, 2 * F), rev),
                  pl.BlockSpec((8, F), lambda i: (jnp.maximum((nb - 1 - i) * (tm // 8) - 1, 0), 0)),
                  ANY, pl.BlockSpec((KW, F), fix), ANY, pl.BlockSpec((tm, D), rev), pl.BlockSpec((1, D), fix)],
        out_specs=(pl.BlockSpec((tm, 2 * F), rev), pl.BlockSpec((8, F), fix), pl.BlockSpec((tm, D), rev),
                   pl.BlockSpec((8, D), fix), pl.BlockSpec((8, D), fix)),
        scratch_shapes=[pltpu.VMEM((F, D), w_down.dtype), pltpu.VMEM((D, 2 * F), w_up.dtype), pltpu.VMEM((8, F), f32),
                        pltpu.SemaphoreType.DMA((1 + N_DEV,))],
        compiler_params=_cp(dimension_semantics=("arbitrary",)))(dpre, up, up, w_down, conv_w, w_up, pre_in, g_in)


def _loss_head(h, target, pre, g, *, first_row):
    L, D = h.shape
    tm = TM
    pb = PADF // tm

    def body(h_ref, t_ref, pre_ref, g_ref, dpre_ref, dg_ref, db_ref, loss_ref):
        i = pl.program_id(0)

        @pl.when(i == 0)
        def _():
            loss_ref[...] = jnp.zeros_like(loss_ref)
            dg_ref[...] = jnp.zeros_like(dg_ref)
            db_ref[...] = jnp.zeros_like(db_ref)

        valid = i >= pb
        err = h_ref[...] - t_ref[...]
        dh = jnp.where(valid, err * (1.0 / D), 0.0)
        part = 0.5 * jnp.sum(jnp.sum(err * err, axis=-1, keepdims=True) * (1.0 / D), axis=0, keepdims=True)
        loss_ref[...] += jnp.where(valid, part, 0.0)
        dpre, dg, db = _ln_bwd_rows(dh, pre_ref[...], g_ref[...], _row_ids(i, tm, D), first_row)
        dpre_ref[...] = dpre
        dg_ref[0:1, :] += dg
        db_ref[0:1, :] += db

    row = lambda i: (i, 0)
    fix = lambda i: (0, 0)
    return pl.pallas_call(
        body, name="loss_head", grid=(L // tm,),
        out_shape=(S((L, D), f32), S((8, D), f32), S((8, D), f32), S((8, LANE), f32)),
        in_specs=[pl.BlockSpec((tm, D), row), pl.BlockSpec((tm, D), lambda i: (jnp.maximum(i - pb, 0), 0)),
                  pl.BlockSpec((tm, D), row), pl.BlockSpec((1, D), fix)],
        out_specs=(pl.BlockSpec((tm, D), row), pl.BlockSpec((8, D), fix), pl.BlockSpec((8, D), fix),
                   pl.BlockSpec((8, LANE), fix)),
        compiler_params=_cp(dimension_semantics=("arbitrary",)))(h, target, pre, g)


def _adamw(g_terms, w, m, v, *, name):
    R, C = w.shape
    tr = _row_tile(R)
    n = len(g_terms)
    c1 = 1.0 - ADAM_B1 ** ADAM_STEP
    c2 = 1.0 - ADAM_B2 ** ADAM_STEP

    def body(*refs):
        g = refs[0][...].astype(f32)
        for r in refs[1:n]:
            g = g + r[...].astype(f32)
        w_ref, m_ref, v_ref, g_out, d_out, m_out, v_out = refs[n:]
        mn = ADAM_B1 * m_ref[...] + (1.0 - ADAM_B1) * g
        vn = ADAM_B2 * v_ref[...] + (1.0 - ADAM_B2) * (g * g)
        g_out[...] = g
        m_out[...] = mn
        v_out[...] = vn
        d_out[...] = -ADAM_LR * ((mn / c1) / (jnp.sqrt(vn / c2) + ADAM_EPS) + ADAM_WD * w_ref[...])

    spec = pl.BlockSpec((tr, C), lambda i: (i, 0))
    return pl.pallas_call(
        body, name=name, grid=(R // tr,), out_shape=(S((R, C), f32),) * 4,
        in_specs=[spec] * (n + 3), out_specs=(spec,) * 4,
        compiler_params=_cp(dimension_semantics=("arbitrary",)))(*g_terms, w, m, v)


def _sum_devices(x):
    n, R, C = x.shape

    def body(x_ref, o_ref):
        acc = x_ref[0]
        for d in range(1, n):
            acc = acc + x_ref[d]
        o_ref[...] = acc

    return pl.pallas_call(body, name="sum_devices", out_shape=S((R, C), f32), compiler_params=_cp())(x)


def _row_tile(R):
    for step in (16, 8):
        for t in range(256, 0, -step):
            if R % t == 0:
                return t
    return R


def _adamw_direct(s32s, recvs, w, m, v, me, *, name):
    L, K, n = w.shape
    tk = _row_tile(K)
    c1 = 1.0 - ADAM_B1 ** ADAM_STEP
    c2 = 1.0 - ADAM_B2 ** ADAM_STEP

    def body(me_ref, *refs):
        own_refs, recv_refs = refs[:L], refs[L:2 * L]
        w_ref, m_ref, v_ref, g_out, d_out, m_out, v_out = refs[2 * L:]
        for li in range(L):
            @pl.when(pl.program_id(0) == li)
            def _(li=li):
                g = own_refs[li][0, 0]
                for d in range(N_DEV):
                    g = g + recv_refs[li][d, 0].astype(f32)
                mn = ADAM_B1 * m_ref[0] + (1.0 - ADAM_B1) * g
                vn = ADAM_B2 * v_ref[0] + (1.0 - ADAM_B2) * (g * g)
                g_out[0] = g
                m_out[0] = mn
                v_out[0] = vn
                d_out[0] = -ADAM_LR * ((mn / c1) / (jnp.sqrt(vn / c2) + ADAM_EPS) + ADAM_WD * w_ref[0])

    own = pl.BlockSpec((1, tk, n), lambda l, i, ix: (l, i, 0))
    grid_spec = pltpu.PrefetchScalarGridSpec(
        num_scalar_prefetch=1, grid=(L, K // tk),
        in_specs=[pl.BlockSpec((1, 1, tk, n), lambda l, i, ix: (ix[0], 0, i, 0))] * L
        + [pl.BlockSpec((N_DEV, 1, tk, n), lambda l, i, ix: (0, 0, i, 0))] * L + [own, own, own],
        out_specs=(own,) * 4)
    return pl.pallas_call(
        body, name=name, grid_spec=grid_spec, out_shape=(S((L, K, n), f32),) * 4,
        compiler_params=_cp(dimension_semantics=("arbitrary", "arbitrary")))(me, *s32s, *recvs, w, m, v)


def _col_segments(n, mapping):
    segs = []
    for p in range(N_DEV):
        lo, hi = p * n, (p + 1) * n
        out = []
        for c0, c1, e0 in mapping:
            a, b = max(lo, c0), min(hi, c1)
            if a < b:
                out.append((a - lo, e0 + (a - c0), b - a))
        segs.append(out)
    return segs


def _assemble_cols(gathered, mapping, n_out, *, name):
    _, L, K, n = gathered.shape
    tk = _row_tile(K)
    segs = _col_segments(n, mapping)
    covered = sum(w for s in segs for (_, _, w) in s)

    def body(g_ref, o_ref):
        if covered != n_out:
            o_ref[...] = jnp.zeros_like(o_ref)
        for p in range(N_DEV):
            for s0, d0, w in segs[p]:
                o_ref[0, :, d0:d0 + w] = g_ref[p, 0, :, s0:s0 + w]

    return pl.pallas_call(
        body, name=name, grid=(L, K // tk), out_shape=S((L, K, n_out), gathered.dtype),
        in_specs=[pl.BlockSpec((N_DEV, 1, tk, n), lambda l, i: (0, l, i, 0))],
        out_specs=pl.BlockSpec((1, tk, n_out), lambda l, i: (l, i, 0)),
        compiler_params=_cp(dimension_semantics=("arbitrary", "arbitrary")))(gathered)


def _split_cols(dws, mapping, n, *, name):
    L = len(dws)
    K, n_in = dws[0].shape
    tk = _row_tile(K)
    segs = _col_segments(n, mapping)

    def body(*refs):
        ins, o32, o16 = refs[:L], refs[L], refs[L + 1]
        for li in range(L):
            @pl.when(pl.program_id(0) == li)
            def _(li=li):
                for p in range(N_DEV):
                    for s0, d0, w in segs[p]:
                        val = ins[li][:, d0:d0 + w]
                        o32[p, 0, :, s0:s0 + w] = val
                        o16[p, 0, :, s0:s0 + w] = _bf(val)

    out = pl.BlockSpec((N_DEV, 1, tk, n), lambda l, i: (0, l, i, 0))
    return pl.pallas_call(
        body, name=name, grid=(L, K // tk), out_shape=(S((N_DEV, L, K, n), f32), S((N_DEV, L, K, n), bf16)),
        in_specs=[pl.BlockSpec((tk, n_in), lambda l, i: (i, 0))] * L, out_specs=(out, out),
        compiler_params=_cp(dimension_semantics=("arbitrary", "arbitrary")))(*dws)


def _split_rows(dws, k, *, name):
    L = len(dws)
    N = dws[0].shape[1]

    def body(*refs):
        ins, o32, o16 = refs[:L], refs[L], refs[L + 1]
        for li in range(L):
            @pl.when(pl.program_id(0) == li)
            def _(li=li):
                val = ins[li][...]
                o32[0, 0] = val
                o16[0, 0] = _bf(val)

    out = pl.BlockSpec((1, 1, k, N), lambda l, p: (p, l, 0, 0))
    return pl.pallas_call(
        body, name=name, grid=(L, N_DEV), out_shape=(S((N_DEV, L, k, N), f32), S((N_DEV, L, k, N), bf16)),
        in_specs=[pl.BlockSpec((k, N), lambda l, p: (p, 0))] * L, out_specs=(out, out),
        compiler_params=_cp(dimension_semantics=("arbitrary", "arbitrary")))(*dws)


def _rows_full(gathered):
    _, L, k, N = gathered.shape
    return jnp.transpose(gathered, (1, 0, 2, 3)).reshape(L, N_DEV * k, N)


def _all_gather(xs, *, name):
    na = len(xs)

    def body(*refs):
        x_refs, out_refs = refs[:na], refs[na:2 * na]
        send_sems, recv_sems, local_sems = refs[2 * na:]
        mx, my, mc = lax.axis_index("x"), lax.axis_index("y"), lax.axis_index("c")
        me, sibling = (mx, my, mc), (mx, my, 1 - mc)
        chips = [(1 - mx, my), (mx, 1 - my), (1 - mx, 1 - my)]

        def slot(a, px, py, pc):
            return out_refs[a].at[4 * px + 2 * py + pc]

        def copy(a, kk, block, to, src=None):
            return pltpu.make_async_remote_copy(
                src_ref=slot(a, *block) if src is None else src, dst_ref=slot(a, *block),
                send_sem=send_sems.at[7 * a + kk], recv_sem=recv_sems.at[7 * a + kk], device_id=to, device_id_type=MESH)

        mine = [pltpu.make_async_copy(x_refs[a], slot(a, *me), local_sems.at[a]) for a in range(na)]
        for cp in mine:
            cp.start()
        first = []
        for a in range(na):
            first.append(copy(a, 0, me, sibling, src=x_refs[a]))
            first += [copy(a, 1 + j, me, (*chip, mc), src=x_refs[a]) for j, chip in enumerate(chips)]
        for cp in first:
            cp.start()
        passed = []
        for j, chip in enumerate(chips):
            for a in range(na):
                copy(a, 1 + j, (*chip, mc), me).wait_recv()
                fwd = copy(a, 4 + j, (*chip, mc), sibling)
                fwd.start()
                passed.append(fwd)
        for a in range(na):
            copy(a, 0, sibling, me).wait_recv()
            for j, chip in enumerate(chips):
                copy(a, 4 + j, (*chip, 1 - mc), me).wait_recv()
        for cp in first + passed:
            cp.wait_send()
        for cp in mine:
            cp.wait()

    return pl.pallas_call(
        body, name=name, out_shape=tuple(S((N_DEV,) + x.shape, x.dtype) for x in xs),
        in_specs=[ANY] * na, out_specs=(ANY,) * na,
        scratch_shapes=[pltpu.SemaphoreType.DMA((7 * na,)), pltpu.SemaphoreType.DMA((7 * na,)),
                        pltpu.SemaphoreType.DMA((na,))],
        compiler_params=pltpu.CompilerParams(has_side_effects=True))(*xs)


_FLIPS = [(fx, fy, fc) for fx in (0, 1) for fy in (0, 1) for fc in (0, 1)][1:]


def _flip_peer(flip):
    x, y, c = lax.axis_index("x"), lax.axis_index("y"), lax.axis_index("c")
    return tuple(1 - a if f else a for a, f in zip((x, y, c), flip))


def _dev_index(p):
    return 4 * p[0] + 2 * p[1] + p[2]


HBM_SPEC = pl.BlockSpec(memory_space=pltpu.HBM)
SEM_SPEC = pl.BlockSpec(memory_space=pltpu.SEMAPHORE)


def _direct_start(srcs, lands, per_peer, *, name, after=None):
    na = len(srcs)
    order = [] if after is None else [after]

    def body(*refs):
        src_refs, land_refs = refs[:na], refs[na:2 * na]
        send_sems, recv_sems = refs[2 * na + len(order)], refs[2 * na + len(order) + 1]
        token = refs[-1]
        me = _dev_index((lax.axis_index("x"), lax.axis_index("y"), lax.axis_index("c")))
        for a in range(na):
            for r, flip in enumerate(_FLIPS):
                peer = _flip_peer(flip)
                src = src_refs[a].at[_dev_index(peer)] if per_peer else src_refs[a]
                pltpu.make_async_remote_copy(
                    src_ref=src, dst_ref=land_refs[a].at[me], send_sem=send_sems.at[7 * a + r],
                    recv_sem=recv_sems.at[7 * a + r], device_id=peer, device_id_type=MESH).start()
        token[...] = jnp.zeros_like(token)

    hbm = lambda t: pltpu.with_memory_space_constraint(t, pltpu.HBM)
    out = pl.pallas_call(
        body, name=name,
        out_shape=(pltpu.SemaphoreType.DMA((7 * na,)), pltpu.SemaphoreType.DMA((7 * na,)))
        + tuple(pltpu.HBM(t.shape, t.dtype) for t in list(srcs) + list(lands)) + (S((8, LANE), f32),),
        in_specs=[HBM_SPEC] * (2 * na) + [ANY] * len(order),
        out_specs=(SEM_SPEC, SEM_SPEC) + (HBM_SPEC,) * (2 * na) + (pl.BlockSpec(memory_space=pltpu.VMEM),),
        input_output_aliases={i: 2 + i for i in range(2 * na)},
        compiler_params=pltpu.CompilerParams(has_side_effects=pltpu.SideEffectType.DATAFLOW_SIDE_EFFECTING))(
            *[hbm(t) for t in srcs], *[hbm(t) for t in lands], *order)
    return out[0], out[1], list(out[2:2 + na]), list(out[2 + na:2 + 2 * na]), out[-1]


def _direct_wait(send_sems, recv_sems, srcs, lands, per_peer, after, *, name):
    na = len(srcs)

    def body(*refs):
        src_refs, land_refs = refs[:na], refs[na:2 * na]
        ssem, rsem = refs[2 * na], refs[2 * na + 1]
        me = _dev_index((lax.axis_index("x"), lax.axis_index("y"), lax.axis_index("c")))
        for a in range(na):
            for r, flip in enumerate(_FLIPS):
                peer = _flip_peer(flip)
                src = src_refs[a].at[_dev_index(peer)] if per_peer else src_refs[a]
                cp = pltpu.make_async_remote_copy(
                    src_ref=src, dst_ref=land_refs[a].at[me], send_sem=ssem.at[7 * a + r],
                    recv_sem=rsem.at[7 * a + r], device_id=peer, device_id_type=MESH)
                cp.wait_send()
                cp.wait_recv()

    out = pl.pallas_call(
        body, name=name, out_shape=tuple(pltpu.HBM(t.shape, t.dtype) for t in list(srcs) + list(lands)),
        in_specs=[HBM_SPEC] * (2 * na) + [SEM_SPEC, SEM_SPEC, ANY], out_specs=(HBM_SPEC,) * (2 * na),
        input_output_aliases={i: i for i in range(2 * na)},
        compiler_params=pltpu.CompilerParams(has_side_effects=pltpu.SideEffectType.DATAFLOW_SIDE_EFFECTING))(
            *srcs, *lands, send_sems, recv_sems, after)
    return list(out[:na]), list(out[na:])


def _pack_small(parts, width):
    rows, offs, r = [], [], 0
    for a in parts:
        n = a.size
        nr = -(-n // width)
        flat = a.reshape(-1).astype(f32)
        if nr * width != n:
            flat = jnp.pad(flat, (0, nr * width - n))
        rows.append(flat.reshape(nr, width))
        offs.append((r, nr))
        r += nr
    buf = jnp.concatenate(rows, axis=0)
    pad = (-r) % 8
    if pad:
        buf = jnp.pad(buf, ((0, pad), (0, 0)))
    return buf, offs


def _unpack_small(buf, off, shape):
    r, nr = off
    return buf[r:r + nr].reshape(-1)[:math.prod(shape)].reshape(shape)


def _local_step(x, target, meta, a_w_in, a_w_out, small, start_token, late_weights, grads_ready):
    SEQ, D = x.shape
    n_meta = meta.shape[0]
    first_row = PADF - n_meta
    H = small["a_log"].shape[-1]

    head = jnp.concatenate([jnp.zeros((first_row, D), f32), meta], axis=0)

    def lanes(a):
        return jnp.pad(a.reshape(1, -1), ((0, 0), (0, LANE - a.size)))

    def after_token(a, token):
        return a if token is None else a + token[0:1, 0:1]

    alog, dtb = after_token(lanes(small["a_log"][0]), start_token), lanes(small["a_dt_bias"][0])
    a_conv, b_conv = small["a_conv"][0], small["b_conv"][0]
    nw = small["a_norm"][0].reshape(1, DH)
    lmg, lmb, lfg, lfb = small["ln_mix_g"], small["ln_mix_b"], small["ln_ffn_g"], small["ln_ffn_b"]

    h0, pre_a, z, raw, q, k, v, beta, g, t_all = _gdn_in_fwd(x, head, a_w_in, a_conv, alog, dtb,
                                                             first_row=first_row, H=H)
    o, y, s_all, pre1, h1 = _delta_fwd(q, k, v, g, beta, t_all, z, nw, h0, a_w_out, lmg[0:1], lmb[0:1],
                                       first_row=first_row, H=H)
    wts = late_weights("ffn", h1)
    up0, act0, pre2, h2 = _ffn_fwd(h1, wts["ffn_w_up"], small["ffn_conv"][0], wts["ffn_w_down"],
                                   lfg[0:1], lfb[0:1], layer=0, first_row=first_row, name="ffn_fwd0")
    wts.update(late_weights("b", h2))
    proj_b, bu, pre3, h3 = _sc_fwd(h2, wts["b_w_in"], b_conv, wts["b_w_out"], lmg[1:2], lmb[1:2], first_row=first_row)
    up1, act1, pre4, h4 = _ffn_fwd(h3, wts["ffn_w_up"], small["ffn_conv"][1], wts["ffn_w_down"],
                                   lfg[1:2], lfb[1:2], layer=1, first_row=first_row, name="ffn_fwd1")
    gs = {}
    dpre4, dlfg1, dlfb1, loss_tile = _loss_head(h4, target, pre4, lfg[1:2], first_row=first_row)

    def ffn_backward(dpre, up, act, h_in, layer, tag, ln_in, token=None):
        dup, dcw, dpre_in, dg, db = _ffn_bwd(
            dpre, up, wts["ffn_w_down"], after_token(small["ffn_conv"][layer], token),
            wts["ffn_w_up"], ln_in[0], ln_in[1], layer=layer, first_row=first_row, name="ffn_bwd" + tag)
        dwd = _linear_dw(act, dpre, name="dw_down" + tag)
        dwu = _linear_dw(h_in, dup, name="dw_up" + tag)
        return dpre_in, dg, db, dwu, dwd, dcw[0:3]

    dpre3, dlmg1, dlmb1, dwu1, dwd1, dcf1 = ffn_backward(dpre4, up1, act1, h3, 1, "1", (pre3, lmg[1:2]))

    dproj_b, dcb, dpre2, dlfg0, dlfb0 = _sc_bwd(dpre3, proj_b, b_conv, wts["b_w_out"], wts["b_w_in"], pre2, lfg[0:1],
                                                first_row=first_row)
    dwb_in = _linear_dw(h2, dproj_b, name="dw_b_in")
    token = grads_ready("layer1", dict(ffn_w_up=dwu1, ffn_w_down=dwd1, b_w_in=dwb_in))

    dpre1, dlmg0, dlmb0, dwu0, dwd0, dcf0 = ffn_backward(dpre2, up0, act0, h1, 0, "0", (pre1, lmg[0:1]), token)
    token = grads_ready("layer0", dict(ffn_w_up=dwu0, ffn_w_down=dwd0))

    dq, dk, dv, dz, dg_, dbeta, dnw = _delta_bwd(dpre1, a_w_out, o, z, after_token(nw, token), q, k, v, g, beta,
                                                 s_all, t_all, H=H)
    dproj_a, dca, dal, ddt, grad_x, dhead = _gdn_in_bwd(dq, dk, dv, dz, dg_, dbeta, pre_a, raw, a_conv, alog, dtb,
                                                        a_w_in, dpre1, first_row=first_row, H=H)
    token = grads_ready("last", dict(a_w_in=_linear_dw(h0, dproj_a, name="dw_a_in")))
    grads_ready("tail", dict(a_w_out=_linear_dw(y, dpre1, name="dw_a_out", after=token),
                             b_w_out=_linear_dw(bu, dpre3, name="dw_b_out", after=token)))

    gs["meta"] = dhead[first_row:PADF]
    gs["a_conv"] = dca[0:a_conv.shape[0]][None]
    gs["a_log"] = dal[0:1, 0:H]
    gs["a_dt_bias"] = ddt[0:1, 0:H]
    gs["a_norm"] = dnw[0:1]
    gs["b_conv"] = dcb[0:b_conv.shape[0]][None]
    gs["ln_mix_g"] = jnp.stack([dlmg0[0], dlmg1[0]])
    gs["ln_mix_b"] = jnp.stack([dlmb0[0], dlmb1[0]])
    gs["ffn_conv"] = jnp.stack([dcf0, dcf1])
    gs["ln_ffn_g"] = jnp.stack([dlfg0[0], dlfg1[0]])
    gs["ln_ffn_b"] = jnp.stack([dlfb0[0], dlfb1[0]])
    return loss_tile, grad_x, gs


_BIG = ("a_w_in", "a_w_out", "b_w_in", "b_w_out", "ffn_w_up", "ffn_w_down")
_BIG_COL = ("a_w_in", "b_w_in", "ffn_w_up")
_SMALL = ("meta", "a_conv", "a_log", "a_dt_bias", "a_norm", "b_conv", "ln_mix_g", "ln_mix_b",
          "ffn_conv", "ln_ffn_g", "ln_ffn_b")
_SMALL_SHARDED = ("meta", "a_conv", "b_conv", "ffn_conv")
_ORDER = ("meta", "a_w_in", "a_conv", "a_log", "a_dt_bias", "a_norm", "a_w_out", "b_w_in", "b_conv", "b_w_out",
          "ln_mix_g", "ln_mix_b", "ffn_w_up", "ffn_conv", "ffn_w_down", "ln_ffn_g", "ln_ffn_b")


def _a_w_in_map(H):
    W4 = 4 * H * DH
    return [(0, W4, 0), (W4, W4 + H, W4), (W4 + H, W4 + 2 * H, W4 + LANE)], W4 + 2 * LANE


def kernel(x, meta, a_w_in, a_conv, a_log, a_dt_bias, a_norm, a_w_out, b_w_in, b_conv, b_w_out, ln_mix_g, ln_mix_b, ffn_w_up, ffn_conv, ffn_w_down, ln_ffn_g, ln_ffn_b, loss_target, m_meta, m_a_w_in, m_a_conv, m_a_log, m_a_dt_bias, m_a_norm, m_a_w_out, m_b_w_in, m_b_conv, m_b_w_out, m_ln_mix_g, m_ln_mix_b, m_ffn_w_up, m_ffn_conv, m_ffn_w_down, m_ln_ffn_g, m_ln_ffn_b, v_meta, v_a_w_in, v_a_conv, v_a_log, v_a_dt_bias, v_a_norm, v_a_w_out, v_b_w_in, v_b_conv, v_b_w_out, v_ln_mix_g, v_ln_mix_b, v_ffn_w_up, v_ffn_conv, v_ffn_w_down, v_ln_ffn_g, v_ln_ffn_b):
    wloc = dict(meta=meta, a_w_in=a_w_in, a_conv=a_conv, a_log=a_log, a_dt_bias=a_dt_bias, a_norm=a_norm,
                a_w_out=a_w_out, b_w_in=b_w_in, b_conv=b_conv, b_w_out=b_w_out, ln_mix_g=ln_mix_g, ln_mix_b=ln_mix_b,
                ffn_w_up=ffn_w_up, ffn_conv=ffn_conv, ffn_w_down=ffn_w_down, ln_ffn_g=ln_ffn_g, ln_ffn_b=ln_ffn_b)
    mloc = dict(meta=m_meta, a_w_in=m_a_w_in, a_conv=m_a_conv, a_log=m_a_log, a_dt_bias=m_a_dt_bias, a_norm=m_a_norm,
                a_w_out=m_a_w_out, b_w_in=m_b_w_in, b_conv=m_b_conv, b_w_out=m_b_w_out, ln_mix_g=m_ln_mix_g,
                ln_mix_b=m_ln_mix_b, ffn_w_up=m_ffn_w_up, ffn_conv=m_ffn_conv, ffn_w_down=m_ffn_w_down,
                ln_ffn_g=m_ln_ffn_g, ln_ffn_b=m_ln_ffn_b)
    vloc = dict(meta=v_meta, a_w_in=v_a_w_in, a_conv=v_a_conv, a_log=v_a_log, a_dt_bias=v_a_dt_bias, a_norm=v_a_norm,
                a_w_out=v_a_w_out, b_w_in=v_b_w_in, b_conv=v_b_conv, b_w_out=v_b_w_out, ln_mix_g=v_ln_mix_g,
                ln_mix_b=v_ln_mix_b, ffn_w_up=v_ffn_w_up, ffn_conv=v_ffn_conv, ffn_w_down=v_ffn_w_down,
                ln_ffn_g=v_ln_ffn_g, ln_ffn_b=v_ln_ffn_b)
    H = a_log.shape[-1]
    mx, my, mc = lax.axis_index("x"), lax.axis_index("y"), lax.axis_index("c")
    me = 4 * mx + 2 * my + mc

    a_map, a_cols = _a_w_in_map(H)
    col_maps = {"a_w_in": (a_map, a_cols)}
    for n in ("b_w_in", "ffn_w_up"):
        ncols = N_DEV * wloc[n].shape[-1]
        col_maps[n] = ([(0, ncols, 0)], ncols)
    sm_sh = [wloc[n] for n in _SMALL_SHARDED]
    sbuf, soffs = _pack_small(sm_sh, 128)
    g_a_w_in, g_a_w_out, sg = _all_gather([_bf(wloc["a_w_in"]), _bf(wloc["a_w_out"]), sbuf], name="gather_first")
    w_a_in = _assemble_cols(g_a_w_in, *col_maps["a_w_in"], name="assemble_a_w_in")[0]
    w_a_out = _rows_full(g_a_w_out)[0]
    late = {"ffn": ("ffn_w_up", "ffn_w_down"), "b": ("b_w_in", "b_w_out")}

    def start_gather(tag, after=None):
        names = late[tag]
        return _direct_start([_bf(wloc[n]) for n in names], [lax.empty((N_DEV,) + wloc[n].shape, bf16) for n in names],
                             False, name=f"gather_{tag}_start", after=after)

    started = {"ffn": start_gather("ffn")}
    start_token = started["ffn"][4]

    def late_weights(tag, after):
        ssem, rsem, srcs_t, lands_t, _ = started[tag]
        srcs_d, landed = _direct_wait(ssem, rsem, srcs_t, lands_t, False, after, name=f"gather_{tag}_wait")
        if tag == "ffn":
            started["b"] = start_gather("b", after=landed[0])
        wts = {}
        for n, own, got in zip(late[tag], srcs_d, landed):
            full = lax.dynamic_update_index_in_dim(got, own, me, 0)
            if n in _BIG_COL:
                wts[n] = _assemble_cols(full, *col_maps[n], name="assemble_" + n)
            elif n == "ffn_w_down":
                wts[n] = full
            else:
                wts[n] = _rows_full(full)
        if tag == "b":
            for n in late[tag]:
                wts[n] = wts[n][0]
        return wts

    small = {n: wloc[n] for n in _SMALL}
    for n, off in zip(_SMALL_SHARDED, soffs):
        sh = wloc[n].shape
        parts = jnp.stack([_unpack_small(sg[d], off, sh) for d in range(N_DEV)])
        nd = len(sh)
        small[n] = jnp.transpose(parts, tuple(range(1, nd)) + (0, nd)).reshape(sh[:-1] + (N_DEV * sh[-1],))

    def split(n, dws, tag):
        if n in _BIG_COL:
            return _split_cols(dws, col_maps[n][0], wloc[n].shape[-1], name="split_" + n + tag)
        return _split_rows(dws, wloc[n].shape[-2], name="split_" + n + tag)

    sent = {}

    def grads_ready(stage, grads):
        names = sorted(grads)
        parts = [split(n, [grads[n]], "_" + stage) for n in names]
        handles = _direct_start([p[1] for p in parts], [jnp.zeros(p[1].shape, bf16) for p in parts], True,
                                name="grads_" + stage + "_start")
        sent[stage] = (names, [p[0] for p in parts], handles)
        return handles[4]

    loss_tile, grad_x, gs = _local_step(x[0], loss_target[0], small["meta"], w_a_in, w_a_out, small, start_token,
                                        late_weights, grads_ready)

    def landed(stage, after):
        names, own32, (ssem_g, rsem_g, srcs_g, lands_g, _) = sent[stage]
        _, got = _direct_wait(ssem_g, rsem_g, srcs_g, lands_g, True, after, name="grads_" + stage + "_wait")
        return list(zip(names, own32, got))

    parts = {}
    for stage in ("layer0", "layer1"):
        for n, o32, r in landed(stage, grad_x):
            parts.setdefault(n, []).append((o32, r))
    me1 = jnp.stack([me]).astype(jnp.int32)
    big_out = {n: _adamw_direct([p[0] for p in ps], [p[1] for p in ps], wloc[n], mloc[n], vloc[n], me1,
                                name="adamw_" + n) for n, ps in parts.items()}
    names = list(_SMALL)
    pbuf, poffs = _pack_small([gs[n] for n in names] + [loss_tile[0:1, 0:1]], 1024)
    psum = _sum_devices(_all_gather([pbuf], name="gather_small_grads")[0])
    loss = psum[poffs[-1][0], 0]
    g_small = {}
    for n, off in zip(names, poffs[:-1]):
        full_shape = gs[n].shape
        gfull = _unpack_small(psum, off, full_shape)
        if n in _SMALL_SHARDED:
            ns = wloc[n].shape[-1]
            gfull = lax.dynamic_slice_in_dim(gfull, me * ns, ns, axis=gfull.ndim - 1)
        g_small[n] = gfull.reshape(wloc[n].shape)
    gbuf, aoffs = _pack_small([g_small[n] for n in names], 128)
    wbuf, _ = _pack_small([wloc[n] for n in names], 128)
    mbuf, _ = _pack_small([mloc[n] for n in names], 128)
    vbuf, _ = _pack_small([vloc[n] for n in names], 128)
    _, d_s, m_s, v_s = _adamw([gbuf], wbuf, mbuf, vbuf, name="adamw_small")

    done = d_s[0, 0]
    for out in big_out.values():
        done = done + out[1][0, 0, 0]
    for stage in ("last", "tail"):
        for n, o32, r in landed(stage, done.reshape(1, 1)):
            big_out[n] = _adamw_direct([o32], [r], wloc[n], mloc[n], vloc[n], me1, name="adamw_" + n)

    grads, deltas, new_m, new_v = {}, {}, {}, {}
    for n in _BIG:
        grads[n], deltas[n], new_m[n], new_v[n] = big_out[n]
    for n, off in zip(names, aoffs):
        sh = wloc[n].shape
        grads[n] = g_small[n]
        deltas[n], new_m[n], new_v[n] = (_unpack_small(b_, off, sh) for b_ in (d_s, m_s, v_s))
    return (loss, grad_x[None], *[grads[n] for n in _ORDER], *[deltas[n] for n in _ORDER],
            *[new_m[n] for n in _ORDER], *[new_v[n] for n in _ORDER])
```

```python
import math

import jax
import jax.numpy as jnp
from jax import lax
from jax.experimental import pallas as pl
from jax.experimental.pallas import tpu as pltpu

f32, bf16 = jnp.float32, jnp.bfloat16
S = jax.ShapeDtypeStruct
HI = lax.Precision.HIGHEST
MESH = pl.DeviceIdType.MESH

V7X_VMEM_LIMIT = 56 * 1024 * 1024
LANE = 128
DH = 128
CH = 64
PADF = 256
TM = 256
TMM = 768
N_DEV = 8
BWD_HEAD_GROUP = 4
BWD_GROUP_LAG = 4

DEPTH = 2
ALPHA = (2.0 * DEPTH) ** 0.25
LN_EPS = 1e-5
RMS_EPS = 1e-6
L2_EPS = 1e-6
ADAM_LR, ADAM_B1, ADAM_B2, ADAM_EPS, ADAM_WD, ADAM_STEP = 0.001, 0.9, 0.999, 1e-08, 0.01, 10


def _cp(**kw):
    return pltpu.CompilerParams(vmem_limit_bytes=V7X_VMEM_LIMIT, **kw)


def _bf(x):
    return x.astype(bf16)


def _dot(a, b, precision=None):
    return jnp.dot(a, b, preferred_element_type=f32, precision=precision)


def _dot_nt(a, b):
    return lax.dot_general(a, b, (((1,), (1,)), ((), ())), preferred_element_type=f32)


def _dot_tn(a, b):
    return lax.dot_general(a, b, (((0,), (0,)), ((), ())), preferred_element_type=f32)


def _sigmoid(x):
    return 1.0 / (1.0 + jnp.exp(-x))


def _load_once(pairs, sem):
    @pl.when(pl.program_id(0) == 0)
    def _():
        cps = [pltpu.make_async_copy(src, dst, sem.at[n]) for n, (src, dst) in enumerate(pairs)]
        for c in cps:
            c.start()
        for c in cps:
            c.wait()


def _row_ids(i, tm, width):
    return i * tm + lax.broadcasted_iota(jnp.int32, (tm, width), 0)


def _ln_fwd(pre, g, b, rows, first_row):
    mu = jnp.mean(pre, axis=-1, keepdims=True)
    xc = pre - mu
    var = jnp.mean(xc * xc, axis=-1, keepdims=True)
    y = xc * lax.rsqrt(var + LN_EPS) * g + b
    return jnp.where(rows >= first_row, y, 0.0)


ANY = pl.BlockSpec(memory_space=pl.ANY)


def _taps_back(prev8, x, kw):
    xe = jnp.concatenate([prev8, x], axis=0)
    return [pltpu.roll(xe, kw - 1 - j, 0)[8:] for j in range(kw - 1)] + [x]


def _taps_ahead(x, next8, kw):
    n = x.shape[0]
    xe = jnp.concatenate([x, next8], axis=0)
    return [pltpu.roll(xe, n + 8 - (kw - 1 - j), 0)[:n] for j in range(kw - 1)] + [x]


def _conv(cw, taps):
    acc = cw[0:1, :] * taps[0]
    for j in range(1, len(taps)):
        acc = acc + cw[j:j + 1, :] * taps[j]
    return acc


def _linear_dw(x, dy, *, name, after=None):
    L, K = x.shape
    N = dy.shape[1]
    tm = TMM if L % TMM == 0 else TM
    tn = LANE
    for d in range(N // LANE, 0, -1):
        if (N // LANE) % d == 0 and K * d * LANE * 4 <= 9 * 1024 * 1024:
            tn = d * LANE
            break

    def body(x_ref, dy_ref, *rest):
        o_ref = rest[-1]

        @pl.when(pl.program_id(1) == 0)
        def _():
            o_ref[...] = jnp.zeros_like(o_ref)
        o_ref[...] += _dot_tn(_bf(x_ref[...]), _bf(dy_ref[...]))

    in_specs = [pl.BlockSpec((tm, K), lambda j, i: (i, 0)), pl.BlockSpec((tm, tn), lambda j, i: (i, j))]
    args = [x, dy]
    if after is not None:
        in_specs.append(pl.BlockSpec(after.shape, lambda j, i: (0, 0)))
        args.append(after)
    return pl.pallas_call(
        body, name=name, grid=(N // tn, L // tm), out_shape=S((K, N), f32),
        in_specs=in_specs, out_specs=pl.BlockSpec((K, tn), lambda j, i: (0, j)),
        compiler_params=_cp(dimension_semantics=("arbitrary", "arbitrary")))(*args)


def _ln_bwd_rows(dout, pre, g, rows, first_row):
    mu = jnp.mean(pre, axis=-1, keepdims=True)
    xc = pre - mu
    rstd = lax.rsqrt(jnp.mean(xc * xc, axis=-1, keepdims=True) + LN_EPS)
    xh = xc * rstd
    dy = jnp.where(rows >= first_row, dout, 0.0)
    dxh = dy * g
    dpre = rstd * (dxh - jnp.mean(dxh, axis=-1, keepdims=True) - xh * jnp.mean(dxh * xh, axis=-1, keepdims=True))
    return dpre, jnp.sum(dy * xh, axis=0, keepdims=True), jnp.sum(dy, axis=0, keepdims=True)


def _gdn_in_fwd(x, head, w_full, conv_w, alog, dtb, *, first_row, H):
    D = x.shape[1]
    L = PADF + x.shape[0]
    W = H * DH
    NW = w_full.shape[1]
    KW = conv_w.shape[0]
    tm = TM
    pb = PADF // tm

    def body(x_ref, head_ref, w_hbm, cw_ref, alog_ref, dtb_ref,
             h_ref, pre_ref, z_ref, raw_ref, q_ref, k_ref, v_ref, beta_ref, g_ref, t_ref,
             w_vmem, carry, sem):
        i = pl.program_id(0)
        _load_once([(w_hbm, w_vmem)], sem)

        @pl.when(i == 0)
        def _():
            carry[...] = jnp.zeros_like(carry)

        hv = jnp.where(i < pb, head_ref[...], x_ref[...])
        h_ref[...] = hv
        hb = _bf(hv)
        outs = (q_ref, k_ref, v_ref)

        def section(s):
            pre = _dot(hb, w_vmem[:, s * W:(s + 1) * W])
            pre_ref[:, s * W:(s + 1) * W] = pre
            c = _conv(cw_ref[:, s * W:(s + 1) * W], _taps_back(carry[s], pre, KW))
            carry[s] = pre[tm - 8:tm, :]
            sl = c * _sigmoid(c)
            if s < 2:
                scale = DH ** -0.5 if s == 0 else 1.0
                for hh in range(H):
                    seg = sl[:, hh * DH:(hh + 1) * DH]
                    r = lax.rsqrt(jnp.sum(seg * seg, axis=-1, keepdims=True) + L2_EPS)
                    outs[s][:, hh * DH:(hh + 1) * DH] = seg * (r * scale)
            else:
                v_ref[...] = sl

        raw = _dot(hb, w_vmem[:, 4 * W:4 * W + 2 * LANE])
        raw_ref[...] = raw
        ok = (_row_ids(i, tm, LANE) >= first_row) & (lax.broadcasted_iota(jnp.int32, (tm, LANE), 1) < H)
        beta = jnp.where(ok, _sigmoid(raw[:, :LANE]), 0.0)
        beta_ref[...] = beta
        a = raw[:, LANE:] + dtb_ref[...]
        sp = jnp.maximum(a, 0.0) + jnp.log(1.0 + jnp.exp(-jnp.abs(a)))
        gv = jnp.where(ok, -jnp.exp(alog_ref[...]) * sp, 0.0)
        gam = _dot(_chunk_tri(tm, lower=True), gv, HI)
        g_ref[...] = gam
        section(1)
        ii = lax.broadcasted_iota(jnp.int32, (CH, CH), 0)
        jj = lax.broadcasted_iota(jnp.int32, (CH, CH), 1)
        eye = (ii == jj).astype(f32)
        gam_t = gam.T

        def inverses(chunks):
            ms = []
            for c in chunks:
                rows = slice(c * CH, (c + 1) * CH)
                for hh in range(H):
                    kh = k_ref[rows, hh * DH:(hh + 1) * DH]
                    dec = jnp.exp(jnp.minimum(gam[rows, hh:hh + 1] - gam_t[hh:hh + 1, rows], 0.0))
                    kk = _dot_nt(_bf(kh * beta[rows, hh:hh + 1]), _bf(kh))
                    ms.append(jnp.where(ii > jj, kk * dec, 0.0))
            for n, t in enumerate(_tri_inv_many(ms, eye)):
                t_ref[chunks[n // H], n % H] = t

        section(2)
        z_ref[...] = _dot(hb, w_vmem[:, 3 * W:4 * W])
        section(0)
        inverses(list(range(tm // CH)))

    row = lambda i: (i, 0)
    fix = lambda i: (0, 0)
    out_shape = (S((L, D), f32), S((L, 3 * W), f32), S((L, W), f32), S((L, 2 * LANE), f32),
                 S((L, W), f32), S((L, W), f32), S((L, W), f32), S((L, LANE), f32), S((L, LANE), f32),
                 S((L // CH, H, CH, CH), f32))
    out_specs = (pl.BlockSpec((tm, D), row),
                 pl.BlockSpec((tm, 3 * W), row), pl.BlockSpec((tm, W), row), pl.BlockSpec((tm, 2 * LANE), row),
                 pl.BlockSpec((tm, W), row), pl.BlockSpec((tm, W), row), pl.BlockSpec((tm, W), row),
                 pl.BlockSpec((tm, LANE), row), pl.BlockSpec((tm, LANE), row),
                 pl.BlockSpec((tm // CH, H, CH, CH), lambda i: (i, 0, 0, 0)))
    return pl.pallas_call(
        body, name="gdn_in_fwd", grid=(L // tm,), out_shape=out_shape,
        in_specs=[pl.BlockSpec((tm, D), lambda i: (jnp.maximum(i - pb, 0), 0)),
                  pl.BlockSpec((tm, D), lambda i: (jnp.minimum(i, pb - 1), 0)), ANY, pl.BlockSpec((KW, 3 * W), fix),
                  pl.BlockSpec((1, LANE), fix), pl.BlockSpec((1, LANE), fix)],
        out_specs=out_specs,
        scratch_shapes=[pltpu.VMEM((D, NW), w_full.dtype), pltpu.VMEM((3, 8, W), f32), pltpu.SemaphoreType.DMA((1,))],
        compiler_params=_cp(dimension_semantics=("arbitrary",)))(x, head, w_full, conv_w, alog, dtb)


def _gdn_in_bwd(dq, dk, dv, dz, dg, dbeta, pre, raw, conv_w, alog, dtb, w_full, res, *, first_row, H):
    L = dq.shape[0]
    D = res.shape[1]
    W = H * DH
    KW = conv_w.shape[0]
    tm = TM
    nb = L // tm
    NW = 4 * W + 2 * LANE
    fb = PADF // tm
    alpha = ALPHA

    def body(dq_ref, dk_ref, dv_ref, dz_ref, dg_ref, dbeta_ref, pre_ref, hq_ref, hk_ref, hv_ref, raw_ref,
             cw_ref, alog_ref, dtb_ref, w_hbm, res_ref,
             dproj_ref, dcw_ref, dal_ref, ddt_ref, dx_ref, dfront_ref, w_vmem, carry, tmp, sem):
        i = pl.program_id(0)
        blk = nb - 1 - i
        _load_once([(w_hbm, w_vmem)], sem)

        @pl.when(i == 0)
        def _():
            carry[...] = jnp.zeros_like(carry)
            dcw_ref[...] = jnp.zeros_like(dcw_ref)
            dal_ref[...] = jnp.zeros_like(dal_ref)
            ddt_ref[...] = jnp.zeros_like(ddt_ref)

        halos = (hq_ref, hk_ref, hv_ref)
        douts = (dq_ref, dk_ref, dv_ref)
        dh = alpha * res_ref[...]
        for s in range(3):
            sec = slice(s * W, (s + 1) * W)
            pre = pre_ref[:, sec]
            c = _conv(cw_ref[:, sec], _taps_back(jnp.where(blk > 0, halos[s][...], 0.0), pre, KW))
            sig = _sigmoid(c)
            sl = c * sig
            if s < 2:
                scale = DH ** -0.5 if s == 0 else 1.0
                for hh in range(H):
                    hs = slice(hh * DH, (hh + 1) * DH)
                    seg = sl[:, hs]
                    r = lax.rsqrt(jnp.sum(seg * seg, axis=-1, keepdims=True) + L2_EPS)
                    n = seg * r
                    dqs = douts[s][:, hs]
                    tmp[:, hs] = (scale * r) * (dqs - n * jnp.sum(n * dqs, axis=-1, keepdims=True))
                dsl = tmp[...]
            else:
                dsl = dv_ref[...]
            dc = dsl * (sig * (1.0 + c * (1.0 - sig)))
            ahead = _taps_ahead(dc, carry[s], KW)
            carry[s] = dc[0:8, :]
            dproj_ref[:, sec] = _bf(_conv(cw_ref[:, sec], ahead))
            dh = dh + _dot_nt(dproj_ref[:, sec], w_vmem[:, sec])
            for j in range(KW):
                dcw_ref[j:j + 1, sec] += jnp.sum(ahead[j] * pre, axis=0, keepdims=True)
        dproj_ref[:, 3 * W:4 * W] = _bf(dz_ref[...])
        raw_v = raw_ref[...]
        ok = (_row_ids(blk, tm, LANE) >= first_row) & (lax.broadcasted_iota(jnp.int32, (tm, LANE), 1) < H)
        beta = _sigmoid(raw_v[:, :LANE])
        dbraw = jnp.where(ok, dbeta_ref[...] * beta * (1.0 - beta), 0.0)
        a = raw_v[:, LANE:] + dtb_ref[...]
        sp = jnp.maximum(a, 0.0) + jnp.log(1.0 + jnp.exp(-jnp.abs(a)))
        nea = -jnp.exp(alog_ref[...])
        dgm = jnp.where(ok, _dot(_chunk_tri(tm, lower=False), dg_ref[...], HI), 0.0)
        daraw = dgm * nea * _sigmoid(a)
        dal_ref[0:1, :] += jnp.sum(dgm * nea * sp, axis=0, keepdims=True)
        ddt_ref[0:1, :] += jnp.sum(daraw, axis=0, keepdims=True)
        dproj_ref[:, 4 * W:4 * W + LANE] = _bf(dbraw)
        dproj_ref[:, 4 * W + LANE:4 * W + 2 * LANE] = _bf(daraw)
        dh = dh + _dot_nt(dproj_ref[:, 3 * W:], w_vmem[:, 3 * W:])

        @pl.when(blk >= fb)
        def _():
            dx_ref[...] = dh

        @pl.when(blk < fb)
        def _():
            dfront_ref[...] = dh

    rev = lambda i: (nb - 1 - i, 0)
    fix = lambda i: (0, 0)

    def halo(col):
        return pl.BlockSpec((8, W), lambda i: (jnp.maximum((nb - 1 - i) * (tm // 8) - 1, 0), col))

    return pl.pallas_call(
        body, name="gdn_in_bwd", grid=(nb,),
        out_shape=(S((L, NW), bf16), S((8, 3 * W), f32), S((8, LANE), f32), S((8, LANE), f32),
                   S((L - PADF, D), f32), S((PADF, D), f32)),
        in_specs=[pl.BlockSpec((tm, W), rev)] * 4 + [pl.BlockSpec((tm, LANE), rev)] * 2
        + [pl.BlockSpec((tm, 3 * W), rev), halo(0), halo(1), halo(2), pl.BlockSpec((tm, 2 * LANE), rev),
           pl.BlockSpec((KW, 3 * W), fix), pl.BlockSpec((1, LANE), fix), pl.BlockSpec((1, LANE), fix),
           ANY, pl.BlockSpec((tm, D), rev)],
        out_specs=(pl.BlockSpec((tm, NW), rev), pl.BlockSpec((8, 3 * W), fix),
                   pl.BlockSpec((8, LANE), fix), pl.BlockSpec((8, LANE), fix),
                   pl.BlockSpec((tm, D), lambda i: (jnp.maximum(nb - 1 - i - fb, 0), 0)),
                   pl.BlockSpec((tm, D), lambda i: (jnp.minimum(nb - 1 - i, fb - 1), 0))),
        scratch_shapes=[pltpu.VMEM((D, NW), w_full.dtype), pltpu.VMEM((3, 8, W), f32), pltpu.VMEM((tm, W), f32),
                        pltpu.SemaphoreType.DMA((1,))],
        compiler_params=_cp(dimension_semantics=("arbitrary",)))(
            dq, dk, dv, dz, dg, dbeta, pre, pre, pre, pre, raw, conv_w, alog, dtb, w_full, res)


def _chunk_tri(n, lower):
    i = lax.broadcasted_iota(jnp.int32, (n, n), 0)
    j = lax.broadcasted_iota(jnp.int32, (n, n), 1)
    sh = int(math.log2(CH))
    same = lax.shift_right_logical(i, sh) == lax.shift_right_logical(j, sh)
    return (same & ((i >= j) if lower else (j >= i))).astype(f32)


def _tri_inv_many(ms, eye):
    ts = [eye - m for m in ms]
    ps = list(ms)
    for _ in range(int(math.log2(CH)) - 1):
        pb = [_bf(p) for p in ps]
        ps = [_dot(p, p) for p in pb]
        ts = [t + _dot(_bf(t), _bf(p)) for t, p in zip(ts, ps)]
    return ts


def _chunk_local(q, k, v, gcol, grow, glast, bcol, ii, jj):
    dec = jnp.where(ii >= jj, jnp.exp(jnp.minimum(gcol - grow, 0.0)), 0.0)
    eg = jnp.exp(gcol)
    kb = k * bcol
    kbg = kb * eg
    vb = v * bcol
    qt = q * eg
    kt = k * jnp.exp(glast - gcol)
    kbb, qb, kbf = _bf(kb), _bf(q), _bf(k)
    return dec, eg, kb, kbg, vb, qt, kt, _dot_nt(kbb, kbf), _dot_nt(qb, kbf), jnp.concatenate([kbb, qb], axis=0)


def _delta_fwd(q, k, v, g, beta, t_all, z, nw, h, w_out, ln_g, ln_b, *, first_row, H):
    L = q.shape[0]
    W = H * DH
    D = h.shape[1]
    rb = TM
    nc = rb // CH
    nblk = L // rb
    alpha = ALPHA

    def body(q_ref, k_ref, v_ref, g_ref, b_ref, t_ref, z_ref, nw_ref, h_ref, wout_hbm, lg_ref, lb_ref,
             o_ref, y_ref, s_out, pre_ref, out_ref, s_scr, wout, sem):
        _load_once([(wout_hbm, wout)], sem)

        @pl.when(pl.program_id(0) == 0)
        def _():
            s_scr[...] = jnp.zeros_like(s_scr)

        ii = lax.broadcasted_iota(jnp.int32, (CH, CH), 0)
        jj = lax.broadcasted_iota(jnp.int32, (CH, CH), 1)
        eye = (ii == jj).astype(f32)
        nwv = nw_ref[...]

        heads = range(H)
        hsl = [slice(hh * DH, (hh + 1) * DH) for hh in heads]

        def chunk(c, carry):
            r0 = pl.multiple_of(c * CH, CH)
            rows = pl.ds(r0, CH)
            gam = g_ref[rows, :]
            gam_t = gam.T
            bb = b_ref[rows, :]
            glast = [gam[CH - 1:CH, hh:hh + 1] for hh in heads]
            loc = [_chunk_local(q_ref[rows, hsl[hh]], k_ref[rows, hsl[hh]], v_ref[rows, hsl[hh]],
                                gam[:, hh:hh + 1], gam_t[hh:hh + 1, :], glast[hh], bb[:, hh:hh + 1], ii, jj)
                   for hh in heads]
            st = [s_scr[hh] for hh in heads]
            zs = [z_ref[rows, hsl[hh]] for hh in heads]
            ts = [_bf(t_ref[c, hh]) for hh in heads]
            us = [_dot(t, _bf(l[4])) for t, l in zip(ts, loc)]
            ws = [_dot(t, _bf(l[3])) for t, l in zip(ts, loc)]
            stb = [_bf(s) for s in st]
            vn = [u - _dot(_bf(w), sb) for u, w, sb in zip(us, ws, stb)]
            vnb = [_bf(x) for x in vn]
            snew = [s * jnp.exp(gl) + _dot_tn(_bf(l[6]), xb) for s, gl, l, xb in zip(st, glast, loc, vnb)]
            os_ = [_dot(_bf(l[5]), sb) + _dot(_bf(l[8] * l[0]), xb) for l, sb, xb in zip(loc, stb, vnb)]
            for hh in heads:
                o = os_[hh]
                s_out[c, hh] = st[hh]
                s_scr[hh] = snew[hh]
                o_ref[rows, hsl[hh]] = o
                on = o * lax.rsqrt(jnp.mean(o * o, axis=-1, keepdims=True) + RMS_EPS) * nwv
                y_ref[rows, hsl[hh]] = _bf(on * (zs[hh] * _sigmoid(zs[hh])))
            return carry

        lax.fori_loop(0, nc, chunk, 0)
        pre = alpha * h_ref[...] + _dot(y_ref[...], wout[...])
        pre_ref[...] = pre
        out_ref[...] = _ln_fwd(pre, lg_ref[...], lb_ref[...], _row_ids(pl.program_id(0), rb, D), first_row)

    row = lambda i: (i, 0)
    fix = lambda i: (0, 0)
    return pl.pallas_call(
        body, name="delta_fwd", grid=(nblk,),
        out_shape=(S((L, W), f32), S((L, W), bf16), S((L // CH, H, DH, DH), f32), S((L, D), f32), S((L, D), f32)),
        in_specs=[pl.BlockSpec((rb, W), row)] * 3 + [pl.BlockSpec((rb, LANE), row)] * 2
        + [pl.BlockSpec((nc, H, CH, CH), lambda i: (i, 0, 0, 0)),
           pl.BlockSpec((rb, W), row), pl.BlockSpec((1, DH), fix), pl.BlockSpec((rb, D), row), ANY,
           pl.BlockSpec((1, D), fix), pl.BlockSpec((1, D), fix)],
        out_specs=(pl.BlockSpec((rb, W), row), pl.BlockSpec((rb, W), row),
                   pl.BlockSpec((nc, H, DH, DH), lambda i: (i, 0, 0, 0)),
                   pl.BlockSpec((rb, D), row), pl.BlockSpec((rb, D), row)),
        scratch_shapes=[pltpu.VMEM((H, DH, DH), f32), pltpu.VMEM((W, D), w_out.dtype), pltpu.SemaphoreType.DMA((1,))],
        compiler_params=_cp(dimension_semantics=("arbitrary",)))(q, k, v, g, beta, t_all, z, nw, h, w_out, ln_g, ln_b)


def _delta_bwd(dpre, w_out, o, z, nw, q, k, v, g, beta, s_all, t_all, *, H):
    L = q.shape[0]
    W = H * DH
    D = dpre.shape[1]
    rb = TM
    nc = rb // CH
    nblk = L // rb

    def body(dpre_ref, wout_hbm, o_ref, z_ref, nw_ref, q_ref, k_ref, v_ref, g_ref, b_ref, s_ref, t_ref,
             dq_ref, dk_ref, dv_ref, dz_ref, dg_ref, db_ref, dnw_ref, ds_scr, wout, dy_scr, sem):
        _load_once([(wout_hbm, wout)], sem)

        @pl.when(pl.program_id(0) == 0)
        def _():
            ds_scr[...] = jnp.zeros_like(ds_scr)
            dnw_ref[...] = jnp.zeros_like(dnw_ref)

        dy_scr[...] = _dot_nt(_bf(dpre_ref[...]), wout[...])

        ii = lax.broadcasted_iota(jnp.int32, (CH, CH), 0)
        jj = lax.broadcasted_iota(jnp.int32, (CH, CH), 1)
        lane = lax.broadcasted_iota(jnp.int32, (CH, LANE), 1)
        last_row = lax.broadcasted_iota(jnp.int32, (CH, 1), 0) == CH - 1
        nwv = nw_ref[...]

        def chunk(cc, carry):
            c = nc - 1 - cc
            r0 = pl.multiple_of(c * CH, CH)
            rows = pl.ds(r0, CH)
            gam = g_ref[rows, :]
            gam_t = gam.T
            bb = b_ref[rows, :]

            def head(hh):
                hs = slice(hh * DH, (hh + 1) * DH)
                gcol, grow, glast = gam[:, hh:hh + 1], gam_t[hh:hh + 1, :], gam[CH - 1:CH, hh:hh + 1]
                bcol = bb[:, hh:hh + 1]
                qh, kh, vh = q_ref[rows, hs], k_ref[rows, hs], v_ref[rows, hs]
                oh, zh, dyh = o_ref[rows, hs], z_ref[rows, hs], dy_scr[rows, hs]
                t = t_ref[c, hh]
                st = s_ref[c, hh]
                dsn = ds_scr[hh]
                rms = lax.rsqrt(jnp.mean(oh * oh, axis=-1, keepdims=True) + RMS_EPS)
                on = oh * rms
                sig = _sigmoid(zh)
                sz = zh * sig
                dz_ref[rows, hs] = dyh * on * nwv * (sig * (1.0 + zh * (1.0 - sig)))
                dnw = jnp.sum(dyh * on * sz, axis=0, keepdims=True)
                don = dyh * nwv * sz
                do = rms * (don - on * jnp.mean(don * on, axis=-1, keepdims=True))
                dec, eg, kb, kbg, vb, qt, kt, kk, qk, kqb = _chunk_local(qh, kh, vh, gcol, grow, glast, bcol, ii, jj)
                stb, dsnb, dob, tb, kbgb = _bf(st), _bf(dsn), _bf(do), _bf(t), _bf(kbg)
                r = vb - _dot(kbgb, stb)
                yield
                mm = jnp.where(ii > jj, kk * dec, 0.0)
                attn = qk * dec
                yield
                rbf = _bf(r)
                vn = _dot(tb, rbf)
                dvn = _dot_tn(_bf(attn), dob) + _dot(_bf(kt), dsnb)
                egl = jnp.exp(glast)
                ekt = jnp.exp(glast - gcol)
                yield
                vnb, dvnb = _bf(vn), _bf(dvn)
                dattn = jnp.where(ii >= jj, _dot_nt(dob, vnb), 0.0)
                dkt = _dot_nt(vnb, dsnb)
                dvb = _dot_tn(tb, dvnb)
                dt = _dot_nt(dvnb, rbf)
                yield
                dglast = egl * jnp.sum(jnp.sum(dsn * st, axis=0, keepdims=True), axis=1, keepdims=True)
                yield
                dv_ref[rows, hs] = dvb * bcol
                dod = jnp.concatenate([dob, -_bf(dvb)], axis=0)
                ds_scr[hh] = egl * dsn + _dot_tn(jnp.concatenate([_bf(qt), kbgb], axis=0), dod)
                both = _dot_nt(dod, stb)
                dqt, dkbg = both[:CH], both[CH:]
                x = _dot_nt(_bf(dt), tb)
                yield
                dm = jnp.where(ii > jj, -_dot_tn(tb, _bf(x)), 0.0)
                dkk = dm * dec
                dqk = dattn * dec
                e = dm * mm + dattn * attn
                dgam = jnp.sum(e, axis=1, keepdims=True) - jnp.sum(e.T, axis=1, keepdims=True)
                dd = _bf(jnp.concatenate([dkk, dqk], axis=0))
                both = _dot(dd, _bf(kh))
                dkb = both[:CH] + dkbg * eg
                yield
                dk_ref[rows, hs] = _dot_tn(dd, kqb) + dkt * ekt + dkb * bcol
                dq_ref[rows, hs] = both[CH:] + dqt * eg
                yield
                dktkt = dkt * kt
                dgam = dgam + jnp.sum(dqt * qt - dktkt + dkbg * kbg, axis=1, keepdims=True)
                dglast = dglast + jnp.sum(jnp.sum(dktkt, axis=0, keepdims=True), axis=1, keepdims=True)
                dgam = dgam + jnp.where(last_row, dglast, 0.0)
                dbeta = jnp.sum(dkb * kh + dvb * vh, axis=1, keepdims=True)
                return dgam, dbeta, dnw

            res = [None] * H
            gens = [head(hh) for hh in range(H)]
            step = 0
            while any(r is None for r in res):
                for hh in range(H):
                    if res[hh] is None and step >= (hh // BWD_HEAD_GROUP) * BWD_GROUP_LAG:
                        try:
                            next(gens[hh])
                        except StopIteration as stop:
                            res[hh] = stop.value
                step += 1
            dgam_all = jnp.zeros((CH, LANE), f32)
            dbeta_all = jnp.zeros((CH, LANE), f32)
            dnw_acc = jnp.zeros((1, DH), f32)
            for hh in range(H):
                dgam, dbeta, dnw = res[hh]
                dgam_all = dgam_all + jnp.where(lane == hh, dgam, 0.0)
                dbeta_all = dbeta_all + jnp.where(lane == hh, dbeta, 0.0)
                dnw_acc = dnw_acc + dnw
            dg_ref[rows, :] = dgam_all
            db_ref[rows, :] = dbeta_all
            dnw_ref[0:1, :] += dnw_acc
            return carry

        lax.fori_loop(0, nc, chunk, 0)

    rev = lambda i: (nblk - 1 - i, 0)
    rev4 = lambda i: (nblk - 1 - i, 0, 0, 0)
    fix = lambda i: (0, 0)
    wide = pl.BlockSpec((rb, W), rev)
    thin = pl.BlockSpec((rb, LANE), rev)
    return pl.pallas_call(
        body, name="delta_bwd", grid=(nblk,),
        out_shape=(S((L, W), f32),) * 4 + (S((L, LANE), f32),) * 2 + (S((8, DH), f32),),
        in_specs=[pl.BlockSpec((rb, D), rev), ANY, wide, wide, pl.BlockSpec((1, DH), fix), wide, wide, wide, thin, thin,
                  pl.BlockSpec((nc, H, DH, DH), rev4), pl.BlockSpec((nc, H, CH, CH), rev4)],
        out_specs=(wide,) * 4 + (thin, thin, pl.BlockSpec((8, DH), fix)),
        scratch_shapes=[pltpu.VMEM((H, DH, DH), f32), pltpu.VMEM((W, D), w_out.dtype), pltpu.VMEM((rb, W), f32),
                        pltpu.SemaphoreType.DMA((1,))],
        compiler_params=_cp(dimension_semantics=("arbitrary",)))(
            dpre, w_out, o, z, nw, q, k, v, g, beta, s_all, t_all)


def _sc_fwd(h, w_in, conv_w, w_out, g, b, *, first_row):
    L, D = h.shape
    W = w_out.shape[0]
    KW = conv_w.shape[0]
    tm = TM
    alpha = ALPHA

    def body(h_ref, win_hbm, cw_ref, wout_hbm, g_ref, b_ref, proj_ref, bu_ref, pre_ref, out_ref,
             win, wout, carry, sem):
        i = pl.program_id(0)
        _load_once([(win_hbm, win), (wout_hbm, wout)], sem)

        @pl.when(i == 0)
        def _():
            carry[...] = jnp.zeros_like(carry)

        hv = h_ref[...]
        hb = _bf(hv)
        bg = _dot(hb, win[:, 0:W])
        cg = _dot(hb, win[:, W:2 * W])
        xv = _dot(hb, win[:, 2 * W:3 * W])
        proj_ref[:, 0:W] = bg
        proj_ref[:, W:2 * W] = cg
        proj_ref[:, 2 * W:3 * W] = xv
        p = cg * xv
        u = _conv(cw_ref[...], _taps_back(carry[...], p, KW))
        carry[...] = p[tm - 8:tm, :]
        bu = _bf(bg * u)
        bu_ref[...] = bu
        pre = alpha * hv + _dot(bu, wout[...])
        pre_ref[...] = pre
        out_ref[...] = _ln_fwd(pre, g_ref[...], b_ref[...], _row_ids(i, tm, D), first_row)

    row = lambda i: (i, 0)
    fix = lambda i: (0, 0)
    return pl.pallas_call(
        body, name="sc_fwd", grid=(L // tm,),
        out_shape=(S((L, 3 * W), f32), S((L, W), bf16), S((L, D), f32), S((L, D), f32)),
        in_specs=[pl.BlockSpec((tm, D), row), ANY, pl.BlockSpec((KW, W), fix), ANY,
                  pl.BlockSpec((1, D), fix), pl.BlockSpec((1, D), fix)],
        out_specs=(pl.BlockSpec((tm, 3 * W), row), pl.BlockSpec((tm, W), row),
                   pl.BlockSpec((tm, D), row), pl.BlockSpec((tm, D), row)),
        scratch_shapes=[pltpu.VMEM((D, 3 * W), w_in.dtype), pltpu.VMEM((W, D), w_out.dtype),
                        pltpu.VMEM((8, W), f32), pltpu.SemaphoreType.DMA((2,))],
        compiler_params=_cp(dimension_semantics=("arbitrary",)))(h, w_in, conv_w, w_out, g, b)


def _sc_bwd(dpre, proj, conv_w, w_out, w_in, pre_in, g_in, *, first_row):
    L, D = dpre.shape
    W = w_out.shape[0]
    KW = conv_w.shape[0]
    tm = TM
    nb = L // tm
    alpha = ALPHA

    def body(dpre_ref, proj_ref, hc_ref, hx_ref, cw_ref, wout_hbm, win_hbm, pin_ref, g_ref,
             dproj_ref, dcw_ref, dpin_ref, dg_ref, db_ref, wout, win, carry, sem):
        i = pl.program_id(0)
        blk = nb - 1 - i
        _load_once([(wout_hbm, wout), (win_hbm, win)], sem)

        @pl.when(i == 0)
        def _():
            carry[...] = jnp.zeros_like(carry)
            dcw_ref[...] = jnp.zeros_like(dcw_ref)
            dg_ref[...] = jnp.zeros_like(dg_ref)
            db_ref[...] = jnp.zeros_like(db_ref)

        bg, cg, xv = proj_ref[:, 0:W], proj_ref[:, W:2 * W], proj_ref[:, 2 * W:3 * W]
        p = cg * xv
        u = _conv(cw_ref[...], _taps_back(jnp.where(blk > 0, hc_ref[...] * hx_ref[...], 0.0), p, KW))
        dpre_v = dpre_ref[...]
        d = _dot_nt(_bf(dpre_v), wout[...])
        dproj_ref[:, 0:W] = _bf(d * u)
        du = d * bg
        ahead = _taps_ahead(du, carry[...], KW)
        carry[...] = du[0:8, :]
        dp = _conv(cw_ref[...], ahead)
        for j in range(KW):
            dcw_ref[j:j + 1, :] += jnp.sum(ahead[j] * p, axis=0, keepdims=True)
        dproj_ref[:, W:2 * W] = _bf(dp * xv)
        dproj_ref[:, 2 * W:3 * W] = _bf(dp * cg)
        dh = alpha * dpre_v + _dot_nt(dproj_ref[...], win[...])
        dpin, dg, dbias = _ln_bwd_rows(dh, pin_ref[...], g_ref[...], _row_ids(blk, tm, D), first_row)
        dpin_ref[...] = dpin
        dg_ref[0:1, :] += dg
        db_ref[0:1, :] += dbias

    rev = lambda i: (nb - 1 - i, 0)
    fix = lambda i: (0, 0)

    def halo(col):
        return pl.BlockSpec((8, W), lambda i: (jnp.maximum((nb - 1 - i) * (tm // 8) - 1, 0), col))

    return pl.pallas_call(
        body, name="sc_bwd", grid=(nb,),
        out_shape=(S((L, 3 * W), bf16), S((8, W), f32), S((L, D), f32), S((8, D), f32), S((8, D), f32)),
        in_specs=[pl.BlockSpec((tm, D), rev), pl.BlockSpec((tm, 3 * W), rev), halo(1), halo(2),
                  pl.BlockSpec((KW, W), fix), ANY, ANY, pl.BlockSpec((tm, D), rev), pl.BlockSpec((1, D), fix)],
        out_specs=(pl.BlockSpec((tm, 3 * W), rev), pl.BlockSpec((8, W), fix), pl.BlockSpec((tm, D), rev),
                   pl.BlockSpec((8, D), fix), pl.BlockSpec((8, D), fix)),
        scratch_shapes=[pltpu.VMEM((W, D), w_out.dtype), pltpu.VMEM((D, 3 * W), w_in.dtype), pltpu.VMEM((8, W), f32),
                        pltpu.SemaphoreType.DMA((2,))],
        compiler_params=_cp(dimension_semantics=("arbitrary",)))(
            dpre, proj, proj, proj, conv_w, w_out, w_in, pre_in, g_in)


def _ffn_cols(F):
    fc = F
    for cand in (1408, 1024, 512, 256, 128):
        if F % cand == 0:
            fc = cand
            break
    return fc


def _ffn_weight_copies(wup_hbm, wdn_hbm, wup, wdn, layer):
    k = wdn_hbm.shape[2]
    return [(wup_hbm.at[layer], wup)] + [(wdn_hbm.at[p, layer], wdn.at[pl.ds(p * k, k), :]) for p in range(N_DEV)]


def _ffn_fwd(h, w_up, conv_w, w_down, g, b, *, layer, first_row, name):
    L, D = h.shape
    F = N_DEV * w_down.shape[2]
    KW = conv_w.shape[0]
    tm = TM
    fc = _ffn_cols(F)
    alpha = ALPHA

    def body(h_ref, wup_hbm, cw_ref, wdn_hbm, g_ref, b_ref, up_ref, a_ref, pre_ref, out_ref,
             wup, wdn, carry, sem):
        i = pl.program_id(0)
        _load_once(_ffn_weight_copies(wup_hbm, wdn_hbm, wup, wdn, layer), sem)

        @pl.when(i == 0)
        def _():
            carry[...] = jnp.zeros_like(carry)

        hv = h_ref[...]
        hb = _bf(hv)
        pre = alpha * hv
        for c0 in range(0, F, fc):
            cs = slice(c0, c0 + fc)
            u = _dot(hb, wup[:, cs])
            gate = _dot(hb, wup[:, F + c0:F + c0 + fc])
            up_ref[:, cs] = u
            up_ref[:, F + c0:F + c0 + fc] = gate
            uc = _conv(cw_ref[:, cs], _taps_back(carry[:, cs], u, KW))
            carry[:, cs] = u[tm - 8:tm, :]
            ab = _bf(uc * _sigmoid(uc) * gate)
            a_ref[:, cs] = ab
            pre = pre + _dot(ab, wdn[cs, :])
        pre_ref[...] = pre
        out_ref[...] = _ln_fwd(pre, g_ref[...], b_ref[...], _row_ids(i, tm, D), first_row)

    row = lambda i: (i, 0)
    fix = lambda i: (0, 0)
    return pl.pallas_call(
        body, name=name, grid=(L // tm,),
        out_shape=(S((L, 2 * F), f32), S((L, F), bf16), S((L, D), f32), S((L, D), f32)),
        in_specs=[pl.BlockSpec((tm, D), row), ANY, pl.BlockSpec((KW, F), fix), ANY,
                  pl.BlockSpec((1, D), fix), pl.BlockSpec((1, D), fix)],
        out_specs=(pl.BlockSpec((tm, 2 * F), row), pl.BlockSpec((tm, F), row),
                   pl.BlockSpec((tm, D), row), pl.BlockSpec((tm, D), row)),
        scratch_shapes=[pltpu.VMEM((D, 2 * F), w_up.dtype), pltpu.VMEM((F, D), w_down.dtype),
                        pltpu.VMEM((8, F), f32), pltpu.SemaphoreType.DMA((1 + N_DEV,))],
        compiler_params=_cp(dimension_semantics=("arbitrary",)))(h, w_up, conv_w, w_down, g, b)


def _ffn_bwd(dpre, up, w_down, conv_w, w_up, pre_in, g_in, *, layer, first_row, name):
    L, D = dpre.shape
    F = N_DEV * w_down.shape[2]
    KW = conv_w.shape[0]
    tm = TM
    nb = L // tm
    fc = F
    alpha = ALPHA

    def body(dpre_ref, up_ref, halo_ref, wdn_hbm, cw_ref, wup_hbm, pin_ref, g_ref,
             dup_ref, dcw_ref, dpin_ref, dg_ref, db_ref, wdn, wup, carry, sem):
        i = pl.program_id(0)
        blk = nb - 1 - i
        _load_once(_ffn_weight_copies(wup_hbm, wdn_hbm, wup, wdn, layer), sem)

        @pl.when(i == 0)
        def _():
            carry[...] = jnp.zeros_like(carry)
            dcw_ref[...] = jnp.zeros_like(dcw_ref)
            dg_ref[...] = jnp.zeros_like(dg_ref)
            db_ref[...] = jnp.zeros_like(db_ref)

        dpre_v = dpre_ref[...]
        db = _bf(dpre_v)
        dh = alpha * dpre_v
        for c0 in range(0, F, fc):
            cs = slice(c0, c0 + fc)
            gs_ = slice(F + c0, F + c0 + fc)
            da = _dot_nt(db, wdn[cs, :])
            gate = up_ref[:, gs_]
            u = up_ref[:, cs]
            uc = _conv(cw_ref[:, cs], _taps_back(jnp.where(blk > 0, halo_ref[:, cs], 0.0), u, KW))
            sig = _sigmoid(uc)
            dgate = _bf(da * (uc * sig))
            dup_ref[:, gs_] = dgate
            duc = da * gate * (sig * (1.0 + uc * (1.0 - sig)))
            ahead = _taps_ahead(duc, carry[:, cs], KW)
            carry[:, cs] = duc[0:8, :]
            du = _bf(_conv(cw_ref[:, cs], ahead))
            dup_ref[:, cs] = du
            for j in range(KW):
                dcw_ref[j:j + 1, cs] += jnp.sum(ahead[j] * u, axis=0, keepdims=True)
            dh = dh + _dot_nt(du, wup[:, cs]) + _dot_nt(dgate, wup[:, gs_])
        dpin, dg, dbias = _ln_bwd_rows(dh, pin_ref[...], g_ref[...], _row_ids(blk, tm, D), first_row)
        dpin_ref[...] = dpin
        dg_ref[0:1, :] += dg
        db_ref[0:1, :] += dbias

    rev = lambda i: (nb - 1 - i, 0)
    fix = lambda i: (0, 0)
    return pl.pallas_call(
        body, name=name, grid=(nb,),
        out_shape=(S((L, 2 * F), bf16), S((8, F), f32), S((L, D), f32), S((8, D), f32), S((8, D), f32)),
        in_specs=[pl.BlockSpec((tm, D), rev), pl.BlockSpec((tm, 2 * F), rev),
                  pl.BlockSpec((8, F), lambda i: (jnp.maximum((nb - 1 - i) * (tm // 8) - 1, 0), 0)),
                  ANY, pl.BlockSpec((KW, F), fix), ANY, pl.BlockSpec((tm, D), rev), pl.BlockSpec((1, D), fix)],
        out_specs=(pl.BlockSpec((tm, 2 * F), rev), pl.BlockSpec((8, F), fix), pl.BlockSpec((tm, D), rev),
                   pl.BlockSpec((8, D), fix), pl.BlockSpec((8, D), fix)),
        scratch_shapes=[pltpu.VMEM((F, D), w_down.dtype), pltpu.VMEM((D, 2 * F), w_up.dtype), pltpu.VMEM((8, F), f32),
                        pltpu.SemaphoreType.DMA((1 + N_DEV,))],
        compiler_params=_cp(dimension_semantics=("arbitrary",)))(dpre, up, up, w_down, conv_w, w_up, pre_in, g_in)


def _loss_head(h, target, pre, g, *, first_row):
    L, D = h.shape
    tm = TM
    pb = PADF // tm

    def body(h_ref, t_ref, pre_ref, g_ref, dpre_ref, dg_ref, db_ref, loss_ref):
        i = pl.program_id(0)

        @pl.when(i == 0)
        def _():
            loss_ref[...] = jnp.zeros_like(loss_ref)
            dg_ref[...] = jnp.zeros_like(dg_ref)
            db_ref[...] = jnp.zeros_like(db_ref)

        valid = i >= pb
        err = h_ref[...] - t_ref[...]
        dh = jnp.where(valid, err * (1.0 / D), 0.0)
        part = 0.5 * jnp.sum(jnp.sum(err * err, axis=-1, keepdims=True) * (1.0 / D), axis=0, keepdims=True)
        loss_ref[...] += jnp.where(valid, part, 0.0)
        dpre, dg, db = _ln_bwd_rows(dh, pre_ref[...], g_ref[...], _row_ids(i, tm, D), first_row)
        dpre_ref[...] = dpre
        dg_ref[0:1, :] += dg
        db_ref[0:1, :] += db

    row = lambda i: (i, 0)
    fix = lambda i: (0, 0)
    return pl.pallas_call(
        body, name="loss_head", grid=(L // tm,),
        out_shape=(S((L, D), f32), S((8, D), f32), S((8, D), f32), S((8, LANE), f32)),
        in_specs=[pl.BlockSpec((tm, D), row), pl.BlockSpec((tm, D), lambda i: (jnp.maximum(i - pb, 0), 0)),
                  pl.BlockSpec((tm, D), row), pl.BlockSpec((1, D), fix)],
        out_specs=(pl.BlockSpec((tm, D), row), pl.BlockSpec((8, D), fix), pl.BlockSpec((8, D), fix),
                   pl.BlockSpec((8, LANE), fix)),
        compiler_params=_cp(dimension_semantics=("arbitrary",)))(h, target, pre, g)


def _adamw(g_terms, w, m, v, *, name):
    R, C = w.shape
    tr = _row_tile(R)
    n = len(g_terms)
    c1 = 1.0 - ADAM_B1 ** ADAM_STEP
    c2 = 1.0 - ADAM_B2 ** ADAM_STEP

    def body(*refs):
        g = refs[0][...].astype(f32)
        for r in refs[1:n]:
            g = g + r[...].astype(f32)
        w_ref, m_ref, v_ref, g_out, d_out, m_out, v_out = refs[n:]
        mn = ADAM_B1 * m_ref[...] + (1.0 - ADAM_B1) * g
        vn = ADAM_B2 * v_ref[...] + (1.0 - ADAM_B2) * (g * g)
        g_out[...] = g
        m_out[...] = mn
        v_out[...] = vn
        d_out[...] = -ADAM_LR * ((mn / c1) / (jnp.sqrt(vn / c2) + ADAM_EPS) + ADAM_WD * w_ref[...])

    spec = pl.BlockSpec((tr, C), lambda i: (i, 0))
    return pl.pallas_call(
        body, name=name, grid=(R // tr,), out_shape=(S((R, C), f32),) * 4,
        in_specs=[spec] * (n + 3), out_specs=(spec,) * 4,
        compiler_params=_cp(dimension_semantics=("arbitrary",)))(*g_terms, w, m, v)


def _sum_devices(x):
    n, R, C = x.shape

    def body(x_ref, o_ref):
        acc = x_ref[0]
        for d in range(1, n):
            acc = acc + x_ref[d]
        o_ref[...] = acc

    return pl.pallas_call(body, name="sum_devices", out_shape=S((R, C), f32), compiler_params=_cp())(x)


def _row_tile(R):
    for step in (16, 8):
        for t in range(256, 0, -step):
            if R % t == 0:
                return t
    return R


def _adamw_direct(s32s, recvs, w, m, v, me, *, name):
    L, K, n = w.shape
    tk = _row_tile(K)
    c1 = 1.0 - ADAM_B1 ** ADAM_STEP
    c2 = 1.0 - ADAM_B2 ** ADAM_STEP

    def body(me_ref, *refs):
        own_refs, recv_refs = refs[:L], refs[L:2 * L]
        w_ref, m_ref, v_ref, g_out, d_out, m_out, v_out = refs[2 * L:]
        for li in range(L):
            @pl.when(pl.program_id(0) == li)
            def _(li=li):
                g = own_refs[li][0, 0]
                for d in range(N_DEV):
                    g = g + recv_refs[li][d, 0].astype(f32)
                mn = ADAM_B1 * m_ref[0] + (1.0 - ADAM_B1) * g
                vn = ADAM_B2 * v_ref[0] + (1.0 - ADAM_B2) * (g * g)
                g_out[0] = g
                m_out[0] = mn
                v_out[0] = vn
                d_out[0] = -ADAM_LR * ((mn / c1) / (jnp.sqrt(vn / c2) + ADAM_EPS) + ADAM_WD * w_ref[0])

    own = pl.BlockSpec((1, tk, n), lambda l, i, ix: (l, i, 0))
    grid_spec = pltpu.PrefetchScalarGridSpec(
        num_scalar_prefetch=1, grid=(L, K // tk),
        in_specs=[pl.BlockSpec((1, 1, tk, n), lambda l, i, ix: (ix[0], 0, i, 0))] * L
        + [pl.BlockSpec((N_DEV, 1, tk, n), lambda l, i, ix: (0, 0, i, 0))] * L + [own, own, own],
        out_specs=(own,) * 4)
    return pl.pallas_call(
        body, name=name, grid_spec=grid_spec, out_shape=(S((L, K, n), f32),) * 4,
        compiler_params=_cp(dimension_semantics=("arbitrary", "arbitrary")))(me, *s32s, *recvs, w, m, v)


def _col_segments(n, mapping):
    segs = []
    for p in range(N_DEV):
        lo, hi = p * n, (p + 1) * n
        out = []
        for c0, c1, e0 in mapping:
            a, b = max(lo, c0), min(hi, c1)
            if a < b:
                out.append((a - lo, e0 + (a - c0), b - a))
        segs.append(out)
    return segs


def _assemble_cols(gathered, mapping, n_out, *, name):
    _, L, K, n = gathered.shape
    tk = _row_tile(K)
    segs = _col_segments(n, mapping)
    covered = sum(w for s in segs for (_, _, w) in s)

    def body(g_ref, o_ref):
        if covered != n_out:
            o_ref[...] = jnp.zeros_like(o_ref)
        for p in range(N_DEV):
            for s0, d0, w in segs[p]:
                o_ref[0, :, d0:d0 + w] = g_ref[p, 0, :, s0:s0 + w]

    return pl.pallas_call(
        body, name=name, grid=(L, K // tk), out_shape=S((L, K, n_out), gathered.dtype),
        in_specs=[pl.BlockSpec((N_DEV, 1, tk, n), lambda l, i: (0, l, i, 0))],
        out_specs=pl.BlockSpec((1, tk, n_out), lambda l, i: (l, i, 0)),
        compiler_params=_cp(dimension_semantics=("arbitrary", "arbitrary")))(gathered)


def _split_cols(dws, mapping, n, *, name):
    L = len(dws)
    K, n_in = dws[0].shape
    tk = _row_tile(K)
    segs = _col_segments(n, mapping)

    def body(*refs):
        ins, o32, o16 = refs[:L], refs[L], refs[L + 1]
        for li in range(L):
            @pl.when(pl.program_id(0) == li)
            def _(li=li):
                for p in range(N_DEV):
                    for s0, d0, w in segs[p]:
                        val = ins[li][:, d0:d0 + w]
                        o32[p, 0, :, s0:s0 + w] = val
                        o16[p, 0, :, s0:s0 + w] = _bf(val)

    out = pl.BlockSpec((N_DEV, 1, tk, n), lambda l, i: (0, l, i, 0))
    return pl.pallas_call(
        body, name=name, grid=(L, K // tk), out_shape=(S((N_DEV, L, K, n), f32), S((N_DEV, L, K, n), bf16)),
        in_specs=[pl.BlockSpec((tk, n_in), lambda l, i: (i, 0))] * L, out_specs=(out, out),
        compiler_params=_cp(dimension_semantics=("arbitrary", "arbitrary")))(*dws)


def _split_rows(dws, k, *, name):
    L = len(dws)
    N = dws[0].shape[1]

    def body(*refs):
        ins, o32, o16 = refs[:L], refs[L], refs[L + 1]
        for li in range(L):
            @pl.when(pl.program_id(0) == li)
            def _(li=li):
                val = ins[li][...]
                o32[0, 0] = val
                o16[0, 0] = _bf(val)

    out = pl.BlockSpec((1, 1, k, N), lambda l, p: (p, l, 0, 0))
    return pl.pallas_call(
        body, name=name, grid=(L, N_DEV), out_shape=(S((N_DEV, L, k, N), f32), S((N_DEV, L, k, N), bf16)),
        in_specs=[pl.BlockSpec((k, N), lambda l, p: (p, 0))] * L, out_specs=(out, out),
        compiler_params=_cp(dimension_semantics=("arbitrary", "arbitrary")))(*dws)


def _rows_full(gathered):
    _, L, k, N = gathered.shape
    return jnp.transpose(gathered, (1, 0, 2, 3)).reshape(L, N_DEV * k, N)


def _all_gather(xs, *, name):
    na = len(xs)

    def body(*refs):
        x_refs, out_refs = refs[:na], refs[na:2 * na]
        send_sems, recv_sems, local_sems = refs[2 * na:]
        mx, my, mc = lax.axis_index("x"), lax.axis_index("y"), lax.axis_index("c")
        me, sibling = (mx, my, mc), (mx, my, 1 - mc)
        chips = [(1 - mx, my), (mx, 1 - my), (1 - mx, 1 - my)]

        def slot(a, px, py, pc):
            return out_refs[a].at[4 * px + 2 * py + pc]

        def copy(a, kk, block, to, src=None):
            return pltpu.make_async_remote_copy(
                src_ref=slot(a, *block) if src is None else src, dst_ref=slot(a, *block),
                send_sem=send_sems.at[7 * a + kk], recv_sem=recv_sems.at[7 * a + kk], device_id=to, device_id_type=MESH)

        mine = [pltpu.make_async_copy(x_refs[a], slot(a, *me), local_sems.at[a]) for a in range(na)]
        for cp in mine:
            cp.start()
        first = []
        for a in range(na):
            first.append(copy(a, 0, me, sibling, src=x_refs[a]))
            first += [copy(a, 1 + j, me, (*chip, mc), src=x_refs[a]) for j, chip in enumerate(chips)]
        for cp in first:
            cp.start()
        passed = []
        for j, chip in enumerate(chips):
            for a in range(na):
                copy(a, 1 + j, (*chip, mc), me).wait_recv()
                fwd = copy(a, 4 + j, (*chip, mc), sibling)
                fwd.start()
                passed.append(fwd)
        for a in range(na):
            copy(a, 0, sibling, me).wait_recv()
            for j, chip in enumerate(chips):
                copy(a, 4 + j, (*chip, 1 - mc), me).wait_recv()
        for cp in first + passed:
            cp.wait_send()
        for cp in mine:
            cp.wait()

    return pl.pallas_call(
        body, name=name, out_shape=tuple(S((N_DEV,) + x.shape, x.dtype) for x in xs),
        in_specs=[ANY] * na, out_specs=(ANY,) * na,
        scratch_shapes=[pltpu.SemaphoreType.DMA((7 * na,)), pltpu.SemaphoreType.DMA((7 * na,)),
                        pltpu.SemaphoreType.DMA((na,))],
        compiler_params=pltpu.CompilerParams(has_side_effects=True))(*xs)


_FLIPS = [(fx, fy, fc) for fx in (0, 1) for fy in (0, 1) for fc in (0, 1)][1:]


def _flip_peer(flip):
    x, y, c = lax.axis_index("x"), lax.axis_index("y"), lax.axis_index("c")
    return tuple(1 - a if f else a for a, f in zip((x, y, c), flip))


def _dev_index(p):
    return 4 * p[0] + 2 * p[1] + p[2]


HBM_SPEC = pl.BlockSpec(memory_space=pltpu.HBM)
SEM_SPEC = pl.BlockSpec(memory_space=pltpu.SEMAPHORE)


def _direct_start(srcs, lands, per_peer, *, name, after=None):
    na = len(srcs)
    order = [] if after is None else [after]

    def body(*refs):
        src_refs, land_refs = refs[:na], refs[na:2 * na]
        send_sems, recv_sems = refs[2 * na + len(order)], refs[2 * na + len(order) + 1]
        token = refs[-1]
        me = _dev_index((lax.axis_index("x"), lax.axis_index("y"), lax.axis_index("c")))
        for a in range(na):
            for r, flip in enumerate(_FLIPS):
                peer = _flip_peer(flip)
                src = src_refs[a].at[_dev_index(peer)] if per_peer else src_refs[a]
                pltpu.make_async_remote_copy(
                    src_ref=src, dst_ref=land_refs[a].at[me], send_sem=send_sems.at[7 * a + r],
                    recv_sem=recv_sems.at[7 * a + r], device_id=peer, device_id_type=MESH).start()
        token[...] = jnp.zeros_like(token)

    hbm = lambda t: pltpu.with_memory_space_constraint(t, pltpu.HBM)
    out = pl.pallas_call(
        body, name=name,
        out_shape=(pltpu.SemaphoreType.DMA((7 * na,)), pltpu.SemaphoreType.DMA((7 * na,)))
        + tuple(pltpu.HBM(t.shape, t.dtype) for t in list(srcs) + list(lands)) + (S((8, LANE), f32),),
        in_specs=[HBM_SPEC] * (2 * na) + [ANY] * len(order),
        out_specs=(SEM_SPEC, SEM_SPEC) + (HBM_SPEC,) * (2 * na) + (pl.BlockSpec(memory_space=pltpu.VMEM),),
        input_output_aliases={i: 2 + i for i in range(2 * na)},
        compiler_params=pltpu.CompilerParams(has_side_effects=pltpu.SideEffectType.DATAFLOW_SIDE_EFFECTING))(
            *[hbm(t) for t in srcs], *[hbm(t) for t in lands], *order)
    return out[0], out[1], list(out[2:2 + na]), list(out[2 + na:2 + 2 * na]), out[-1]


def _direct_wait(send_sems, recv_sems, srcs, lands, per_peer, after, *, name):
    na = len(srcs)

    def body(*refs):
        src_refs, land_refs = refs[:na], refs[na:2 * na]
        ssem, rsem = refs[2 * na], refs[2 * na + 1]
        me = _dev_index((lax.axis_index("x"), lax.axis_index("y"), lax.axis_index("c")))
        for a in range(na):
            for r, flip in enumerate(_FLIPS):
                peer = _flip_peer(flip)
                src = src_refs[a].at[_dev_index(peer)] if per_peer else src_refs[a]
                cp = pltpu.make_async_remote_copy(
                    src_ref=src, dst_ref=land_refs[a].at[me], send_sem=ssem.at[7 * a + r],
                    recv_sem=rsem.at[7 * a + r], device_id=peer, device_id_type=MESH)
                cp.wait_send()
                cp.wait_recv()

    out = pl.pallas_call(
        body, name=name, out_shape=tuple(pltpu.HBM(t.shape, t.dtype) for t in list(srcs) + list(lands)),
        in_specs=[HBM_SPEC] * (2 * na) + [SEM_SPEC, SEM_SPEC, ANY], out_specs=(HBM_SPEC,) * (2 * na),
        input_output_aliases={i: i for i in range(2 * na)},
        compiler_params=pltpu.CompilerParams(has_side_effects=pltpu.SideEffectType.DATAFLOW_SIDE_EFFECTING))(
            *srcs, *lands, send_sems, recv_sems, after)
    return list(out[:na]), list(out[na:])


def _pack_small(parts, width):
    rows, offs, r = [], [], 0
    for a in parts:
        n = a.size
        nr = -(-n // width)
        flat = a.reshape(-1).astype(f32)
        if nr * width != n:
            flat = jnp.pad(flat, (0, nr * width - n))
        rows.append(flat.reshape(nr, width))
        offs.append((r, nr))
        r += nr
    buf = jnp.concatenate(rows, axis=0)
    pad = (-r) % 8
    if pad:
        buf = jnp.pad(buf, ((0, pad), (0, 0)))
    return buf, offs


def _unpack_small(buf, off, shape):
    r, nr = off
    return buf[r:r + nr].reshape(-1)[:math.prod(shape)].reshape(shape)


def _local_step(x, target, meta, a_w_in, a_w_out, small, start_token, late_weights, grads_ready):
    SEQ, D = x.shape
    n_meta = meta.shape[0]
    first_row = PADF - n_meta
    H = small["a_log"].shape[-1]

    head = jnp.concatenate([jnp.zeros((first_row, D), f32), meta], axis=0)

    def lanes(a):
        return jnp.pad(a.reshape(1, -1), ((0, 0), (0, LANE - a.size)))

    def after_token(a, token):
        return a if token is None else a + token[0:1, 0:1]

    alog, dtb = after_token(lanes(small["a_log"][0]), start_token), lanes(small["a_dt_bias"][0])
    a_conv, b_conv = small["a_conv"][0], small["b_conv"][0]
    nw = small["a_norm"][0].reshape(1, DH)
    lmg, lmb, lfg, lfb = small["ln_mix_g"], small["ln_mix_b"], small["ln_ffn_g"], small["ln_ffn_b"]

    h0, pre_a, z, raw, q, k, v, beta, g, t_all = _gdn_in_fwd(x, head, a_w_in, a_conv, alog, dtb,
                                                             first_row=first_row, H=H)
    o, y, s_all, pre1, h1 = _delta_fwd(q, k, v, g, beta, t_all, z, nw, h0, a_w_out, lmg[0:1], lmb[0:1],
                                       first_row=first_row, H=H)
    wts = late_weights("ffn", h1)
    up0, act0, pre2, h2 = _ffn_fwd(h1, wts["ffn_w_up"], small["ffn_conv"][0], wts["ffn_w_down"],
                                   lfg[0:1], lfb[0:1], layer=0, first_row=first_row, name="ffn_fwd0")
    wts.update(late_weights("b", h2))
    proj_b, bu, pre3, h3 = _sc_fwd(h2, wts["b_w_in"], b_conv, wts["b_w_out"], lmg[1:2], lmb[1:2], first_row=first_row)
    up1, act1, pre4, h4 = _ffn_fwd(h3, wts["ffn_w_up"], small["ffn_conv"][1], wts["ffn_w_down"],
                                   lfg[1:2], lfb[1:2], layer=1, first_row=first_row, name="ffn_fwd1")
    gs = {}
    dpre4, dlfg1, dlfb1, loss_tile = _loss_head(h4, target, pre4, lfg[1:2], first_row=first_row)

    def ffn_backward(dpre, up, act, h_in, layer, tag, ln_in, token=None):
        dup, dcw, dpre_in, dg, db = _ffn_bwd(
            dpre, up, wts["ffn_w_down"], after_token(small["ffn_conv"][layer], token),
            wts["ffn_w_up"], ln_in[0], ln_in[1], layer=layer, first_row=first_row, name="ffn_bwd" + tag)
        dwd = _linear_dw(act, dpre, name="dw_down" + tag)
        dwu = _linear_dw(h_in, dup, name="dw_up" + tag)
        return dpre_in, dg, db, dwu, dwd, dcw[0:3]

    dpre3, dlmg1, dlmb1, dwu1, dwd1, dcf1 = ffn_backward(dpre4, up1, act1, h3, 1, "1", (pre3, lmg[1:2]))

    dproj_b, dcb, dpre2, dlfg0, dlfb0 = _sc_bwd(dpre3, proj_b, b_conv, wts["b_w_out"], wts["b_w_in"], pre2, lfg[0:1],
                                                first_row=first_row)
    dwb_in = _linear_dw(h2, dproj_b, name="dw_b_in")
    token = grads_ready("layer1", dict(ffn_w_up=dwu1, ffn_w_down=dwd1, b_w_in=dwb_in))

    dpre1, dlmg0, dlmb0, dwu0, dwd0, dcf0 = ffn_backward(dpre2, up0, act0, h1, 0, "0", (pre1, lmg[0:1]), token)
    token = grads_ready("layer0", dict(ffn_w_up=dwu0, ffn_w_down=dwd0))

    dq, dk, dv, dz, dg_, dbeta, dnw = _delta_bwd(dpre1, a_w_out, o, z, after_token(nw, token), q, k, v, g, beta,
                                                 s_all, t_all, H=H)
    dproj_a, dca, dal, ddt, grad_x, dhead = _gdn_in_bwd(dq, dk, dv, dz, dg_, dbeta, pre_a, raw, a_conv, alog, dtb,
                                                        a_w_in, dpre1, first_row=first_row, H=H)
    token = grads_ready("last", dict(a_w_in=_linear_dw(h0, dproj_a, name="dw_a_in")))
    grads_ready("tail", dict(a_w_out=_linear_dw(y, dpre1, name="dw_a_out", after=token),
                             b_w_out=_linear_dw(bu, dpre3, name="dw_b_out", after=token)))

    gs["meta"] = dhead[first_row:PADF]
    gs["a_conv"] = dca[0:a_conv.shape[0]][None]
    gs["a_log"] = dal[0:1, 0:H]
    gs["a_dt_bias"] = ddt[0:1, 0:H]
    gs["a_norm"] = dnw[0:1]
    gs["b_conv"] = dcb[0:b_conv.shape[0]][None]
    gs["ln_mix_g"] = jnp.stack([dlmg0[0], dlmg1[0]])
    gs["ln_mix_b"] = jnp.stack([dlmb0[0], dlmb1[0]])
    gs["ffn_conv"] = jnp.stack([dcf0, dcf1])
    gs["ln_ffn_g"] = jnp.stack([dlfg0[0], dlfg1[0]])
    gs["ln_ffn_b"] = jnp.stack([dlfb0[0], dlfb1[0]])
    return loss_tile, grad_x, gs


_BIG = ("a_w_in", "a_w_out", "b_w_in", "b_w_out", "ffn_w_up", "ffn_w_down")
_BIG_COL = ("a_w_in", "b_w_in", "ffn_w_up")
_SMALL = ("meta", "a_conv", "a_log", "a_dt_bias", "a_norm", "b_conv", "ln_mix_g", "ln_mix_b",
          "ffn_conv", "ln_ffn_g", "ln_ffn_b")
_SMALL_SHARDED = ("meta", "a_conv", "b_conv", "ffn_conv")
_ORDER = ("meta", "a_w_in", "a_conv", "a_log", "a_dt_bias", "a_norm", "a_w_out", "b_w_in", "b_conv", "b_w_out",
          "ln_mix_g", "ln_mix_b", "ffn_w_up", "ffn_conv", "ffn_w_down", "ln_ffn_g", "ln_ffn_b")


def _a_w_in_map(H):
    W4 = 4 * H * DH
    return [(0, W4, 0), (W4, W4 + H, W4), (W4 + H, W4 + 2 * H, W4 + LANE)], W4 + 2 * LANE


def kernel(x, meta, a_w_in, a_conv, a_log, a_dt_bias, a_norm, a_w_out, b_w_in, b_conv, b_w_out, ln_mix_g, ln_mix_b, ffn_w_up, ffn_conv, ffn_w_down, ln_ffn_g, ln_ffn_b, loss_target, m_meta, m_a_w_in, m_a_conv, m_a_log, m_a_dt_bias, m_a_norm, m_a_w_out, m_b_w_in, m_b_conv, m_b_w_out, m_ln_mix_g, m_ln_mix_b, m_ffn_w_up, m_ffn_conv, m_ffn_w_down, m_ln_ffn_g, m_ln_ffn_b, v_meta, v_a_w_in, v_a_conv, v_a_log, v_a_dt_bias, v_a_norm, v_a_w_out, v_b_w_in, v_b_conv, v_b_w_out, v_ln_mix_g, v_ln_mix_b, v_ffn_w_up, v_ffn_conv, v_ffn_w_down, v_ln_ffn_g, v_ln_ffn_b):
    wloc = dict(meta=meta, a_w_in=a_w_in, a_conv=a_conv, a_log=a_log, a_dt_bias=a_dt_bias, a_norm=a_norm,
                a_w_out=a_w_out, b_w_in=b_w_in, b_conv=b_conv, b_w_out=b_w_out, ln_mix_g=ln_mix_g, ln_mix_b=ln_mix_b,
                ffn_w_up=ffn_w_up, ffn_conv=ffn_conv, ffn_w_down=ffn_w_down, ln_ffn_g=ln_ffn_g, ln_ffn_b=ln_ffn_b)
    mloc = dict(meta=m_meta, a_w_in=m_a_w_in, a_conv=m_a_conv, a_log=m_a_log, a_dt_bias=m_a_dt_bias, a_norm=m_a_norm,
                a_w_out=m_a_w_out, b_w_in=m_b_w_in, b_conv=m_b_conv, b_w_out=m_b_w_out, ln_mix_g=m_ln_mix_g,
                ln_mix_b=m_ln_mix_b, ffn_w_up=m_ffn_w_up, ffn_conv=m_ffn_conv, ffn_w_down=m_ffn_w_down,
                ln_ffn_g=m_ln_ffn_g, ln_ffn_b=m_ln_ffn_b)
    vloc = dict(meta=v_meta, a_w_in=v_a_w_in, a_conv=v_a_conv, a_log=v_a_log, a_dt_bias=v_a_dt_bias, a_norm=v_a_norm,
                a_w_out=v_a_w_out, b_w_in=v_b_w_in, b_conv=v_b_conv, b_w_out=v_b_w_out, ln_mix_g=v_ln_mix_g,
                ln_mix_b=v_ln_mix_b, ffn_w_up=v_ffn_w_up, ffn_conv=v_ffn_conv, ffn_w_down=v_ffn_w_down,
                ln_ffn_g=v_ln_ffn_g, ln_ffn_b=v_ln_ffn_b)
    H = a_log.shape[-1]
    mx, my, mc = lax.axis_index("x"), lax.axis_index("y"), lax.axis_index("c")
    me = 4 * mx + 2 * my + mc

    a_map, a_cols = _a_w_in_map(H)
    col_maps = {"a_w_in": (a_map, a_cols)}
    for n in ("b_w_in", "ffn_w_up"):
        ncols = N_DEV * wloc[n].shape[-1]
        col_maps[n] = ([(0, ncols, 0)], ncols)
    sm_sh = [wloc[n] for n in _SMALL_SHARDED]
    sbuf, soffs = _pack_small(sm_sh, 128)
    g_a_w_in, g_a_w_out, sg = _all_gather([_bf(wloc["a_w_in"]), _bf(wloc["a_w_out"]), sbuf], name="gather_first")
    w_a_in = _assemble_cols(g_a_w_in, *col_maps["a_w_in"], name="assemble_a_w_in")[0]
    w_a_out = _rows_full(g_a_w_out)[0]
    late = {"ffn": ("ffn_w_up", "ffn_w_down"), "b": ("b_w_in", "b_w_out")}

    def start_gather(tag, after=None):
        names = late[tag]
        return _direct_start([_bf(wloc[n]) for n in names], [lax.empty((N_DEV,) + wloc[n].shape, bf16) for n in names],
                             False, name=f"gather_{tag}_start", after=after)

    started = {"ffn": start_gather("ffn")}
    start_token = started["ffn"][4]

    def late_weights(tag, after):
        ssem, rsem, srcs_t, lands_t, _ = started[tag]
        srcs_d, landed = _direct_wait(ssem, rsem, srcs_t, lands_t, False, after, name=f"gather_{tag}_wait")
        if tag == "ffn":
            started["b"] = start_gather("b", after=landed[0])
        wts = {}
        for n, own, got in zip(late[tag], srcs_d, landed):
            full = lax.dynamic_update_index_in_dim(got, own, me, 0)
            if n in _BIG_COL:
                wts[n] = _assemble_cols(full, *col_maps[n], name="assemble_" + n)
            elif n == "ffn_w_down":
                wts[n] = full
            else:
                wts[n] = _rows_full(full)
        if tag == "b":
            for n in late[tag]:
                wts[n] = wts[n][0]
        return wts

    small = {n: wloc[n] for n in _SMALL}
    for n, off in zip(_SMALL_SHARDED, soffs):
        sh = wloc[n].shape
        parts = jnp.stack([_unpack_small(sg[d], off, sh) for d in range(N_DEV)])
        nd = len(sh)
        small[n] = jnp.transpose(parts, tuple(range(1, nd)) + (0, nd)).reshape(sh[:-1] + (N_DEV * sh[-1],))

    def split(n, dws, tag):
        if n in _BIG_COL:
            return _split_cols(dws, col_maps[n][0], wloc[n].shape[-1], name="split_" + n + tag)
        return _split_rows(dws, wloc[n].shape[-2], name="split_" + n + tag)

    sent = {}

    def grads_ready(stage, grads):
        names = sorted(grads)
        parts = [split(n, [grads[n]], "_" + stage) for n in names]
        handles = _direct_start([p[1] for p in parts], [jnp.zeros(p[1].shape, bf16) for p in parts], True,
                                name="grads_" + stage + "_start")
        sent[stage] = (names, [p[0] for p in parts], handles)
        return handles[4]

    loss_tile, grad_x, gs = _local_step(x[0], loss_target[0], small["meta"], w_a_in, w_a_out, small, start_token,
                                        late_weights, grads_ready)

    def landed(stage, after):
        names, own32, (ssem_g, rsem_g, srcs_g, lands_g, _) = sent[stage]
        _, got = _direct_wait(ssem_g, rsem_g, srcs_g, lands_g, True, after, name="grads_" + stage + "_wait")
        return list(zip(names, own32, got))

    parts = {}
    for stage in ("layer0", "layer1"):
        for n, o32, r in landed(stage, grad_x):
            parts.setdefault(n, []).append((o32, r))
    me1 = jnp.stack([me]).astype(jnp.int32)
    big_out = {n: _adamw_direct([p[0] for p in ps], [p[1] for p in ps], wloc[n], mloc[n], vloc[n], me1,
                                name="adamw_" + n) for n, ps in parts.items()}
    names = list(_SMALL)
    pbuf, poffs = _pack_small([gs[n] for n in names] + [loss_tile[0:1, 0:1]], 1024)
    psum = _sum_devices(_all_gather([pbuf], name="gather_small_grads")[0])
    loss = psum[poffs[-1][0], 0]
    g_small = {}
    for n, off in zip(names, poffs[:-1]):
        full_shape = gs[n].shape
        gfull = _unpack_small(psum, off, full_shape)
        if n in _SMALL_SHARDED:
            ns = wloc[n].shape[-1]
            gfull = lax.dynamic_slice_in_dim(gfull, me * ns, ns, axis=gfull.ndim - 1)
        g_small[n] = gfull.reshape(wloc[n].shape)
    gbuf, aoffs = _pack_small([g_small[n] for n in names], 128)
    wbuf, _ = _pack_small([wloc[n] for n in names], 128)
    mbuf, _ = _pack_small([mloc[n] for n in names], 128)
    vbuf, _ = _pack_small([vloc[n] for n in names], 128)
    _, d_s, m_s, v_s = _adamw([gbuf], wbuf, mbuf, vbuf, name="adamw_small")

    done = d_s[0, 0]
    for out in big_out.values():
        done = done + out[1][0, 0, 0]
    for stage in ("last", "tail"):
        for n, o32, r in landed(stage, done.reshape(1, 1)):
            big_out[n] = _adamw_direct([o32], [r], wloc[n], mloc[n], vloc[n], me1, name="adamw_" + n)

    grads, deltas, new_m, new_v = {}, {}, {}, {}
    for n in _BIG:
        grads[n], deltas[n], new_m[n], new_v[n] = big_out[n]
    for n, off in zip(names, aoffs):
        sh = wloc[n].shape
        grads[n] = g_small[n]
        deltas[n], new_m[n], new_v[n] = (_unpack_small(b_, off, sh) for b_ in (d_s, m_s, v_s))
    return (loss, grad_x[None], *[grads[n] for n in _ORDER], *[deltas[n] for n in _ORDER],
            *[new_m[n] for n in _ORDER], *[new_v[n] for n in _ORDER])
```
